```python
import jax, jax.numpy as jnp
from jax import lax
import numpy as np

D_MODEL = 1024
BATCH = 8
SEQ = 16384
DEPTH = 2

D_MIX = D_MODEL
N_GROUPS = 4
D_GROUP = D_MIX // N_GROUPS
HEAD_DIM = 64
N_HEADS_GROUP = D_GROUP // HEAD_DIM
CHUNK = 128
Q_BLOCK = 128
SHORT_CONV_W = 3
CONF_CONV_W = 31
D_FF = 4 * D_MODEL
EPS = 1e-6
D_IN_PROJ = 10 * D_GROUP
SPLIT_POINTS = (2 * D_GROUP, 5 * D_GROUP, 8 * D_GROUP)

kernel_name = 'hymba_parallel_gmlp_shortconv_stickbreak_conformer'


def rms_norm(x, g=None):
    xf = x.astype(jnp.float32)
    y = xf * lax.rsqrt(jnp.mean(xf * xf, axis=-1, keepdims=True) + EPS)
    if g is not None:
        y = y * g.astype(jnp.float32)
    return y.astype(x.dtype)


def layer_norm(x, g, b):
    xf = x.astype(jnp.float32)
    xc = xf - jnp.mean(xf, axis=-1, keepdims=True)
    var = jnp.mean(xc * xc, axis=-1, keepdims=True)
    y = xc * lax.rsqrt(var + EPS) * g.astype(jnp.float32) + b.astype(jnp.float32)
    return y.astype(x.dtype)


def causal_depthwise_conv(x, w):
    k_w, c = w.shape
    return lax.conv_general_dilated(
        x, w[:, None, :].astype(x.dtype), window_strides=(1,), padding=[(k_w - 1, 0)],
        dimension_numbers=('NWC', 'WIO', 'NWC'), feature_group_count=c)


def spatial_gating_mixer(z, v_gain, w_s, b_s):
    z = jax.nn.gelu(z)
    u, v = jnp.split(z, 2, axis=-1)
    v = rms_norm(v, v_gain)
    b, s, _ = v.shape
    v = v.reshape(b, s // CHUNK, CHUNK, N_HEADS_GROUP, HEAD_DIM)
    mask = jnp.tril(jnp.ones((CHUNK, CHUNK), dtype=bool))
    w = jnp.where(mask, w_s, jnp.zeros_like(w_s))
    f = jnp.einsum('hts,bnshd->bnthd', w, v) + b_s.T[:, :, None]
    return u * f.reshape(b, s, D_GROUP)


def short_conv_mixer(z, w_conv):
    gate_b, gate_c, h = jnp.split(z, 3, axis=-1)
    return gate_b * causal_depthwise_conv(gate_c * h, w_conv)


def stick_breaking_attention(z):
    q, k, v = jnp.split(z, 3, axis=-1)
    b, s, _ = q.shape
    q = q.reshape(b, s, N_HEADS_GROUP, HEAD_DIM)
    k = k.reshape(b, s, N_HEADS_GROUP, HEAD_DIM)
    v = v.reshape(b, s, N_HEADS_GROUP, HEAD_DIM)
    scale = HEAD_DIM ** -0.5
    n_blocks = s // Q_BLOCK
    q_blocks = q.reshape(b, n_blocks, Q_BLOCK, N_HEADS_GROUP, HEAD_DIM).transpose(1, 0, 2, 3, 4)
    key_pos = jnp.arange(s)

    def one_block(args):
        q_blk, blk_idx = args
        logits = jnp.einsum('bqhd,bkhd->bhqk', q_blk, k,
                            preferred_element_type=jnp.float32) * scale
        q_pos = blk_idx * Q_BLOCK + jnp.arange(Q_BLOCK)
        causal = key_pos[None, :] < q_pos[:, None]
        log_beta = jax.nn.log_sigmoid(logits)
        log_one_minus = jnp.where(causal, jax.nn.log_sigmoid(-logits), 0.0)
        log_stick = lax.cumsum(log_one_minus, axis=3, reverse=True) - log_one_minus
        weights = jnp.where(causal, jnp.exp(log_beta + log_stick), 0.0)
        return jnp.einsum('bhqk,bkhd->bqhd', weights.astype(v.dtype), v)

    out = lax.map(one_block, (q_blocks, jnp.arange(n_blocks)))
    return out.transpose(1, 0, 2, 3, 4).reshape(b, s, D_GROUP)


def conformer_conv_mixer(z, w_conv, ln_g, ln_b):
    a, g = jnp.split(z, 2, axis=-1)
    h = a * jax.nn.sigmoid(g)
    h = causal_depthwise_conv(h, w_conv)
    h = layer_norm(h, ln_g, ln_b)
    return jax.nn.silu(h)


def _fwd_setup_inputs(seed: int = 0) -> dict:
    key = jax.random.key(seed)
    ks = jax.random.split(key, 17)
    nrm = jax.random.normal
    x = nrm(ks[0], (BATCH, SEQ, D_MODEL), jnp.float32)
    norm_mix_g = 1.0 + 0.05 * nrm(ks[1], (DEPTH, D_MODEL), jnp.float32)
    w_in = nrm(ks[2], (DEPTH, D_MODEL, D_IN_PROJ), jnp.float32) * D_MODEL ** -0.5
    gmlp_v_g = 1.0 + 0.05 * nrm(ks[3], (DEPTH, D_GROUP), jnp.float32)
    gmlp_w_s = nrm(ks[4], (DEPTH, N_HEADS_GROUP, CHUNK, CHUNK), jnp.float32) * CHUNK ** -0.5
    gmlp_b_s = 1.0 + 0.05 * nrm(ks[5], (DEPTH, N_HEADS_GROUP, CHUNK), jnp.float32)
    short_conv_w = nrm(ks[6], (DEPTH, SHORT_CONV_W, D_GROUP), jnp.float32) * SHORT_CONV_W ** -0.5
    conf_conv_w = nrm(ks[7], (DEPTH, CONF_CONV_W, D_GROUP), jnp.float32) * CONF_CONV_W ** -0.5
    conf_ln_g = 1.0 + 0.05 * nrm(ks[8], (DEPTH, D_GROUP), jnp.float32)
    conf_ln_b = 0.02 * nrm(ks[9], (DEPTH, D_GROUP), jnp.float32)
    mix_out_g = 1.0 + 0.05 * nrm(ks[10], (DEPTH, D_MIX), jnp.float32)
    w_out = nrm(ks[11], (DEPTH, D_MIX, D_MODEL), jnp.float32) * D_MIX ** -0.5
    norm_ffn_g = 1.0 + 0.05 * nrm(ks[12], (DEPTH, D_MODEL), jnp.float32)
    w_up = nrm(ks[13], (DEPTH, D_MODEL, D_FF), jnp.float32) * D_MODEL ** -0.5
    w_down = nrm(ks[14], (DEPTH, D_FF, D_MODEL), jnp.float32) * D_FF ** -0.5
    final_norm_g = 1.0 + 0.05 * nrm(ks[15], (D_MODEL,), jnp.float32)
    return {'x': x, 'norm_mix_g': norm_mix_g, 'w_in': w_in, 'gmlp_v_g': gmlp_v_g,
            'gmlp_w_s': gmlp_w_s, 'gmlp_b_s': gmlp_b_s, 'short_conv_w': short_conv_w,
            'conf_conv_w': conf_conv_w, 'conf_ln_g': conf_ln_g, 'conf_ln_b': conf_ln_b,
            'mix_out_g': mix_out_g, 'w_out': w_out, 'norm_ffn_g': norm_ffn_g,
            'w_up': w_up, 'w_down': w_down, 'final_norm_g': final_norm_g}


def _fwd_reference(x, norm_mix_g, w_in, gmlp_v_g, gmlp_w_s, gmlp_b_s, short_conv_w, conf_conv_w,
              conf_ln_g, conf_ln_b, mix_out_g, w_out, norm_ffn_g, w_up, w_down, final_norm_g):
    for l in range(DEPTH):
        h = rms_norm(x, norm_mix_g[l])
        z = jnp.einsum('bsd,de->bse', h, w_in[l])
        z_a, z_b, z_c, z_d = jnp.split(z, SPLIT_POINTS, axis=-1)
        y_a = spatial_gating_mixer(z_a, gmlp_v_g[l], gmlp_w_s[l], gmlp_b_s[l])
        y_b = short_conv_mixer(z_b, short_conv_w[l])
        y_c = stick_breaking_attention(z_c)
        y_d = conformer_conv_mixer(z_d, conf_conv_w[l], conf_ln_g[l], conf_ln_b[l])
        y = jnp.concatenate([rms_norm(y_a), rms_norm(y_b), rms_norm(y_c), rms_norm(y_d)],
                            axis=-1) * mix_out_g[l]
        x = x + jnp.einsum('bse,ed->bsd', y, w_out[l])
        h = rms_norm(x, norm_ffn_g[l])
        a = jax.nn.relu(jnp.einsum('bsd,df->bsf', h, w_up[l]))
        x = x + jnp.einsum('bsf,fd->bsd', a * a, w_down[l])
    return rms_norm(x, final_norm_g)


import jax as _jax
import jax.numpy as _jnp

TWIN_FORMAT = 'train_step'
FWD_PARAMS = ['x', 'norm_mix_g', 'w_in', 'gmlp_v_g', 'gmlp_w_s', 'gmlp_b_s', 'short_conv_w', 'conf_conv_w', 'conf_ln_g', 'conf_ln_b', 'mix_out_g', 'w_out', 'norm_ffn_g', 'w_up', 'w_down', 'final_norm_g']
TWIN_WEIGHTS = ['norm_mix_g', 'w_in', 'gmlp_v_g', 'gmlp_w_s', 'gmlp_b_s', 'short_conv_w', 'conf_conv_w', 'conf_ln_g', 'conf_ln_b', 'mix_out_g', 'w_out', 'norm_ffn_g', 'w_up', 'w_down', 'final_norm_g']
TWIN_DIFF_INPUT = 'x'
TWIN_INPUTS = ['x', 'norm_mix_g', 'w_in', 'gmlp_v_g', 'gmlp_w_s', 'gmlp_b_s', 'short_conv_w', 'conf_conv_w', 'conf_ln_g', 'conf_ln_b', 'mix_out_g', 'w_out', 'norm_ffn_g', 'w_up', 'w_down', 'final_norm_g', 'loss_target', 'm_norm_mix_g', 'm_w_in', 'm_gmlp_v_g', 'm_gmlp_w_s', 'm_gmlp_b_s', 'm_short_conv_w', 'm_conf_conv_w', 'm_conf_ln_g', 'm_conf_ln_b', 'm_mix_out_g', 'm_w_out', 'm_norm_ffn_g', 'm_w_up', 'm_w_down', 'm_final_norm_g', 'v_norm_mix_g', 'v_w_in', 'v_gmlp_v_g', 'v_gmlp_w_s', 'v_gmlp_b_s', 'v_short_conv_w', 'v_conf_conv_w', 'v_conf_ln_g', 'v_conf_ln_b', 'v_mix_out_g', 'v_w_out', 'v_norm_ffn_g', 'v_w_up', 'v_w_down', 'v_final_norm_g']
TWIN_OUTPUTS = ['loss', 'grad_x', 'grad_norm_mix_g', 'grad_w_in', 'grad_gmlp_v_g', 'grad_gmlp_w_s', 'grad_gmlp_b_s', 'grad_short_conv_w', 'grad_conf_conv_w', 'grad_conf_ln_g', 'grad_conf_ln_b', 'grad_mix_out_g', 'grad_w_out', 'grad_norm_ffn_g', 'grad_w_up', 'grad_w_down', 'grad_final_norm_g', 'delta_norm_mix_g', 'delta_w_in', 'delta_gmlp_v_g', 'delta_gmlp_w_s', 'delta_gmlp_b_s', 'delta_short_conv_w', 'delta_conf_conv_w', 'delta_conf_ln_g', 'delta_conf_ln_b', 'delta_mix_out_g', 'delta_w_out', 'delta_norm_ffn_g', 'delta_w_up', 'delta_w_down', 'delta_final_norm_g', 'new_m_norm_mix_g', 'new_m_w_in', 'new_m_gmlp_v_g', 'new_m_gmlp_w_s', 'new_m_gmlp_b_s', 'new_m_short_conv_w', 'new_m_conf_conv_w', 'new_m_conf_ln_g', 'new_m_conf_ln_b', 'new_m_mix_out_g', 'new_m_w_out', 'new_m_norm_ffn_g', 'new_m_w_up', 'new_m_w_down', 'new_m_final_norm_g', 'new_v_norm_mix_g', 'new_v_w_in', 'new_v_gmlp_v_g', 'new_v_gmlp_w_s', 'new_v_gmlp_b_s', 'new_v_short_conv_w', 'new_v_conf_conv_w', 'new_v_conf_ln_g', 'new_v_conf_ln_b', 'new_v_mix_out_g', 'new_v_w_out', 'new_v_norm_ffn_g', 'new_v_w_up', 'new_v_w_down', 'new_v_final_norm_g']
TWIN_LEAF_KINDS = {'loss': 'loss', 'grad_x': 'grad_x', 'grad_norm_mix_g': 'grad_w', 'grad_w_in': 'grad_w', 'grad_gmlp_v_g': 'grad_w', 'grad_gmlp_w_s': 'grad_w', 'grad_gmlp_b_s': 'grad_w', 'grad_short_conv_w': 'grad_w', 'grad_conf_conv_w': 'grad_w', 'grad_conf_ln_g': 'grad_w', 'grad_conf_ln_b': 'grad_w', 'grad_mix_out_g': 'grad_w', 'grad_w_out': 'grad_w', 'grad_norm_ffn_g': 'grad_w', 'grad_w_up': 'grad_w', 'grad_w_down': 'grad_w', 'grad_final_norm_g': 'grad_w', 'delta_norm_mix_g': 'delta_w', 'delta_w_in': 'delta_w', 'delta_gmlp_v_g': 'delta_w', 'delta_gmlp_w_s': 'delta_w', 'delta_gmlp_b_s': 'delta_w', 'delta_short_conv_w': 'delta_w', 'delta_conf_conv_w': 'delta_w', 'delta_conf_ln_g': 'delta_w', 'delta_conf_ln_b': 'delta_w', 'delta_mix_out_g': 'delta_w', 'delta_w_out': 'delta_w', 'delta_norm_ffn_g': 'delta_w', 'delta_w_up': 'delta_w', 'delta_w_down': 'delta_w', 'delta_final_norm_g': 'delta_w', 'new_m_norm_mix_g': 'new_m', 'new_m_w_in': 'new_m', 'new_m_gmlp_v_g': 'new_m', 'new_m_gmlp_w_s': 'new_m', 'new_m_gmlp_b_s': 'new_m', 'new_m_short_conv_w': 'new_m', 'new_m_conf_conv_w': 'new_m', 'new_m_conf_ln_g': 'new_m', 'new_m_conf_ln_b': 'new_m', 'new_m_mix_out_g': 'new_m', 'new_m_w_out': 'new_m', 'new_m_norm_ffn_g': 'new_m', 'new_m_w_up': 'new_m', 'new_m_w_down': 'new_m', 'new_m_final_norm_g': 'new_m', 'new_v_norm_mix_g': 'new_v', 'new_v_w_in': 'new_v', 'new_v_gmlp_v_g': 'new_v', 'new_v_gmlp_w_s': 'new_v', 'new_v_gmlp_b_s': 'new_v', 'new_v_short_conv_w': 'new_v', 'new_v_conf_conv_w': 'new_v', 'new_v_conf_ln_g': 'new_v', 'new_v_conf_ln_b': 'new_v', 'new_v_mix_out_g': 'new_v', 'new_v_w_out': 'new_v', 'new_v_norm_ffn_g': 'new_v', 'new_v_w_up': 'new_v', 'new_v_w_down': 'new_v', 'new_v_final_norm_g': 'new_v'}


def _forward(args):
    return _fwd_reference(*[args[k] for k in FWD_PARAMS])


def _output_shape():
    def fwd():
        inp = _fwd_setup_inputs(0)
        return _fwd_reference(*[inp[k] for k in FWD_PARAMS])
    out = _jax.eval_shape(fwd)
    return out.shape, out.dtype

N_MICROBATCH = 1
ADAM_LR = 0.001
ADAM_B1 = 0.9
ADAM_B2 = 0.999
ADAM_EPS = 1e-08
ADAM_WD = 0.01
ADAM_STEP = 10
PER_EXAMPLE_BATCH_AXIS = {'x': 0, 'loss_target': 0}
SHARED_INPUTS = []
_WEIGHT_DTYPES = {'norm_mix_g': _jnp.float32, 'w_in': _jnp.float32, 'gmlp_v_g': _jnp.float32, 'gmlp_w_s': _jnp.float32, 'gmlp_b_s': _jnp.float32, 'short_conv_w': _jnp.float32, 'conf_conv_w': _jnp.float32, 'conf_ln_g': _jnp.float32, 'conf_ln_b': _jnp.float32, 'mix_out_g': _jnp.float32, 'w_out': _jnp.float32, 'norm_ffn_g': _jnp.float32, 'w_up': _jnp.float32, 'w_down': _jnp.float32, 'final_norm_g': _jnp.float32}
MOMENT_SCALE = {'norm_mix_g': 4.030136e-01, 'w_in': 2.363072e-01, 'gmlp_v_g': 1.400555e-01, 'gmlp_w_s': 9.925765e-02, 'gmlp_b_s': 1.350072e-01, 'short_conv_w': 2.660201e-01, 'conf_conv_w': 3.087387e-01, 'conf_ln_g': 7.677874e-01, 'conf_ln_b': 1.275152e+00, 'mix_out_g': 4.149816e-01, 'w_out': 4.042978e-01, 'norm_ffn_g': 3.271408e-01, 'w_up': 1.648837e-01, 'w_down': 6.808766e-01, 'final_norm_g': 1.298893e+02}


def _to_microbatches(a, axis):
    t = _jnp.moveaxis(a, axis, 0)
    t = t.reshape((N_MICROBATCH, t.shape[0] // N_MICROBATCH) + t.shape[1:])
    return _jnp.moveaxis(t, 1, axis + 1)


def setup_inputs(seed: int = 0) -> dict:
    inp = _fwd_setup_inputs(seed)
    key = _jax.random.fold_in(_jax.random.key(seed), 7919)
    shape, _ = _output_shape()
    out = dict(inp)
    out["loss_target"] = _jax.random.normal(_jax.random.fold_in(key, 0), shape, _jnp.float32)
    for i, name in enumerate(TWIN_WEIGHTS):
        w = inp[name].astype(_jnp.float32)
        if MOMENT_SCALE is None:
            s = _jnp.sqrt(_jnp.mean(_jnp.square(w)) + 1e-30)
        else:
            s = MOMENT_SCALE[name]
        km, kv = _jax.random.split(_jax.random.fold_in(key, i + 1))
        out[name] = w
        out["m_" + name] = s * _jax.random.normal(km, w.shape, _jnp.float32)
        out["v_" + name] = (s * s) * _jax.random.uniform(kv, w.shape, _jnp.float32, 0.5, 1.5)
    if N_MICROBATCH > 1:
        for name, axis in PER_EXAMPLE_BATCH_AXIS.items():
            out[name] = _to_microbatches(out[name], axis)
    return {'x': out['x'], 'norm_mix_g': out['norm_mix_g'], 'w_in': out['w_in'], 'gmlp_v_g': out['gmlp_v_g'], 'gmlp_w_s': out['gmlp_w_s'], 'gmlp_b_s': out['gmlp_b_s'], 'short_conv_w': out['short_conv_w'], 'conf_conv_w': out['conf_conv_w'], 'conf_ln_g': out['conf_ln_g'], 'conf_ln_b': out['conf_ln_b'], 'mix_out_g': out['mix_out_g'], 'w_out': out['w_out'], 'norm_ffn_g': out['norm_ffn_g'], 'w_up': out['w_up'], 'w_down': out['w_down'], 'final_norm_g': out['final_norm_g'], 'loss_target': out['loss_target'], 'm_norm_mix_g': out['m_norm_mix_g'], 'm_w_in': out['m_w_in'], 'm_gmlp_v_g': out['m_gmlp_v_g'], 'm_gmlp_w_s': out['m_gmlp_w_s'], 'm_gmlp_b_s': out['m_gmlp_b_s'], 'm_short_conv_w': out['m_short_conv_w'], 'm_conf_conv_w': out['m_conf_conv_w'], 'm_conf_ln_g': out['m_conf_ln_g'], 'm_conf_ln_b': out['m_conf_ln_b'], 'm_mix_out_g': out['m_mix_out_g'], 'm_w_out': out['m_w_out'], 'm_norm_ffn_g': out['m_norm_ffn_g'], 'm_w_up': out['m_w_up'], 'm_w_down': out['m_w_down'], 'm_final_norm_g': out['m_final_norm_g'], 'v_norm_mix_g': out['v_norm_mix_g'], 'v_w_in': out['v_w_in'], 'v_gmlp_v_g': out['v_gmlp_v_g'], 'v_gmlp_w_s': out['v_gmlp_w_s'], 'v_gmlp_b_s': out['v_gmlp_b_s'], 'v_short_conv_w': out['v_short_conv_w'], 'v_conf_conv_w': out['v_conf_conv_w'], 'v_conf_ln_g': out['v_conf_ln_g'], 'v_conf_ln_b': out['v_conf_ln_b'], 'v_mix_out_g': out['v_mix_out_g'], 'v_w_out': out['v_w_out'], 'v_norm_ffn_g': out['v_norm_ffn_g'], 'v_w_up': out['v_w_up'], 'v_w_down': out['v_w_down'], 'v_final_norm_g': out['v_final_norm_g']}


def _loss(weights, diff, rest, loss_target):
    with _jax.named_scope("forward"):
        args = {**rest, TWIN_DIFF_INPUT: diff, **{k: w.astype(_WEIGHT_DTYPES[k]) for k, w in weights.items()}}
        y = _forward(args)
    with _jax.named_scope("loss_head"):
        err = _jnp.square(y.astype(_jnp.float32) - loss_target)
        return 0.5 * _jnp.sum(_jnp.mean(err, axis=-1)) if err.ndim else 0.5 * err


def _adamw(w, g, m, v):
    m = ADAM_B1 * m + (1.0 - ADAM_B1) * g
    v = ADAM_B2 * v + (1.0 - ADAM_B2) * _jnp.square(g)
    m_hat = m / (1.0 - ADAM_B1 ** ADAM_STEP)
    v_hat = v / (1.0 - ADAM_B2 ** ADAM_STEP)
    delta = -ADAM_LR * (m_hat / (_jnp.sqrt(v_hat) + ADAM_EPS) + ADAM_WD * w)
    return delta, m, v


def reference(x, norm_mix_g, w_in, gmlp_v_g, gmlp_w_s, gmlp_b_s, short_conv_w, conf_conv_w, conf_ln_g, conf_ln_b, mix_out_g, w_out, norm_ffn_g, w_up, w_down, final_norm_g, loss_target, m_norm_mix_g, m_w_in, m_gmlp_v_g, m_gmlp_w_s, m_gmlp_b_s, m_short_conv_w, m_conf_conv_w, m_conf_ln_g, m_conf_ln_b, m_mix_out_g, m_w_out, m_norm_ffn_g, m_w_up, m_w_down, m_final_norm_g, v_norm_mix_g, v_w_in, v_gmlp_v_g, v_gmlp_w_s, v_gmlp_b_s, v_short_conv_w, v_conf_conv_w, v_conf_ln_g, v_conf_ln_b, v_mix_out_g, v_w_out, v_norm_ffn_g, v_w_up, v_w_down, v_final_norm_g):
    given = dict(x=x, norm_mix_g=norm_mix_g, w_in=w_in, gmlp_v_g=gmlp_v_g, gmlp_w_s=gmlp_w_s, gmlp_b_s=gmlp_b_s, short_conv_w=short_conv_w, conf_conv_w=conf_conv_w, conf_ln_g=conf_ln_g, conf_ln_b=conf_ln_b, mix_out_g=mix_out_g, w_out=w_out, norm_ffn_g=norm_ffn_g, w_up=w_up, w_down=w_down, final_norm_g=final_norm_g, loss_target=loss_target, m_norm_mix_g=m_norm_mix_g, m_w_in=m_w_in, m_gmlp_v_g=m_gmlp_v_g, m_gmlp_w_s=m_gmlp_w_s, m_gmlp_b_s=m_gmlp_b_s, m_short_conv_w=m_short_conv_w, m_conf_conv_w=m_conf_conv_w, m_conf_ln_g=m_conf_ln_g, m_conf_ln_b=m_conf_ln_b, m_mix_out_g=m_mix_out_g, m_w_out=m_w_out, m_norm_ffn_g=m_norm_ffn_g, m_w_up=m_w_up, m_w_down=m_w_down, m_final_norm_g=m_final_norm_g, v_norm_mix_g=v_norm_mix_g, v_w_in=v_w_in, v_gmlp_v_g=v_gmlp_v_g, v_gmlp_w_s=v_gmlp_w_s, v_gmlp_b_s=v_gmlp_b_s, v_short_conv_w=v_short_conv_w, v_conf_conv_w=v_conf_conv_w, v_conf_ln_g=v_conf_ln_g, v_conf_ln_b=v_conf_ln_b, v_mix_out_g=v_mix_out_g, v_w_out=v_w_out, v_norm_ffn_g=v_norm_ffn_g, v_w_up=v_w_up, v_w_down=v_w_down, v_final_norm_g=v_final_norm_g)
    weights = {n: given[n] for n in TWIN_WEIGHTS}
    shared = {n: given[n] for n in SHARED_INPUTS}
    per_example = {n: given[n] for n in ['x']}
    grad_fn = _jax.value_and_grad(_loss, argnums=(0, 1))

    def one_microbatch(ex, loss_target):
        ex = dict(ex)
        diff = ex.pop(TWIN_DIFF_INPUT)
        return grad_fn(weights, diff, {**shared, **ex}, loss_target)

    if N_MICROBATCH == 1:
        loss, (grad_w, grad_x) = one_microbatch(per_example, given["loss_target"])
    else:
        def body(carry, xs):
            loss_sum, grad_sum = carry
            l_k, (gw_k, gx_k) = one_microbatch(xs[0], xs[1])
            with _jax.named_scope("update"):
                return (loss_sum + l_k, _jax.tree.map(_jnp.add, grad_sum, gw_k)), gx_k

        init = (_jnp.zeros((), _jnp.float32), _jax.tree.map(_jnp.zeros_like, weights))
        (loss, grad_w), grad_x = _jax.lax.scan(body, init, (per_example, given["loss_target"]))
    with _jax.named_scope("update"):
        delta_w, new_m, new_v = {}, {}, {}
        for n in TWIN_WEIGHTS:
            delta_w[n], new_m[n], new_v[n] = _adamw(weights[n], grad_w[n], given["m_" + n], given["v_" + n])
    return (loss, grad_x, *[grad_w[n] for n in TWIN_WEIGHTS], *[delta_w[n] for n in TWIN_WEIGHTS],
            *[new_m[n] for n in TWIN_WEIGHTS], *[new_v[n] for n in TWIN_WEIGHTS])
```

```python
import math

import jax
import jax.numpy as jnp
from jax import lax
from jax.experimental import pallas as pl
from jax.experimental.pallas import tpu as pltpu

F32 = jnp.float32
BF16 = jnp.bfloat16

D_MODEL = 1024
D_GROUP = 256
N_HEADS = 4
HEAD_DIM = 64
CHUNK = 128
D_IN = 2560
N_BLK = 4
W_IN_BLK = D_IN // N_BLK
D_FF_BLK = 1024
DEPTH = 2
EPS = 1e-6
HALO = 32
SHORT_K = 3
CONF_K = 31
ATT_TQ = 128
ATT_TK = 256
ATT_SCALE = 0.125
ATT_DEAD = -104.0
V7X_VMEM_LIMIT = 56 * 1024 * 1024

ADAM_LR, ADAM_B1, ADAM_B2, ADAM_EPS, ADAM_WD, ADAM_STEP = 0.001, 0.9, 0.999, 1e-08, 0.01, 10

PACK_ROWS = 6144
HALF_ROWS = PACK_ROWS // 2
MESH = pl.DeviceIdType.MESH


def _params(sem, vmem=None):
    return pltpu.CompilerParams(dimension_semantics=sem, vmem_limit_bytes=vmem)


def _tile(s, t):
    return min(s, t)


def _rsqrt_mean(v):
    return lax.rsqrt(jnp.mean(v * v, axis=-1, keepdims=True) + EPS)


def _sigmoid(v):
    return 1.0 / (1.0 + jnp.exp(-v))


_GELU_C = math.sqrt(2.0 / math.pi)


def _gelu(v):
    t = jnp.tanh(_GELU_C * (v + 0.044715 * (v * v * v)))
    return v * (0.5 * (1.0 + t))


def _gelu_grad(v):
    t = jnp.tanh(_GELU_C * (v + 0.044715 * (v * v * v)))
    return 0.5 * (1.0 + t) + v * (0.5 * (1.0 - t * t) * _GELU_C * (1.0 + 3.0 * 0.044715 * (v * v)))


def _dot(a, b):
    return jnp.dot(a, b, preferred_element_type=F32)


def _dot_nt(a, b):
    return lax.dot_general(a, b, (((1,), (1,)), ((), ())), preferred_element_type=F32)


def _dot_tn(a, b):
    return lax.dot_general(a, b, (((0,), (0,)), ((), ())), preferred_element_type=F32)


def _cast_bf16(w, name):
    shape = w.shape
    w3 = w.reshape((-1,) + shape[-2:])
    n, r, c = w3.shape
    tr = _tile(r, 256)

    def body(w_ref, o_ref):
        o_ref[...] = w_ref[...].astype(BF16)

    out = pl.pallas_call(
        body, name=name, grid=(n, r // tr),
        in_specs=[pl.BlockSpec((None, tr, c), lambda a, b: (a, b, 0))],
        out_specs=pl.BlockSpec((None, tr, c), lambda a, b: (a, b, 0)),
        out_shape=jax.ShapeDtypeStruct(w3.shape, BF16),
        compiler_params=_params(("parallel", "parallel")),
    )(w3)
    return out.reshape(shape)


def _in_proj(x, g, w, layer):
    s = x.shape[0]
    t = _tile(s, 512)

    def body(x_ref, g_ref, w_ref, z_ref, h_ref):
        @pl.when(pl.program_id(1) == 0)
        def _():
            xv = x_ref[...]
            h_ref[...] = (xv * _rsqrt_mean(xv) * g_ref[...]).astype(BF16)

        z_ref[...] = _dot(h_ref[...], w_ref[...])

    return pl.pallas_call(
        body, name="in_proj", grid=(s // t, N_BLK),
        in_specs=[pl.BlockSpec((t, D_MODEL), lambda i, j: (i, 0)),
                  pl.BlockSpec((1, D_MODEL), lambda i, j: (0, 0)),
                  pl.BlockSpec((None, None, D_MODEL, W_IN_BLK), lambda i, j: (j, layer, 0, 0))],
        out_specs=[pl.BlockSpec((t, W_IN_BLK), lambda i, j: (i, j)),
                   pl.BlockSpec((t, D_MODEL), lambda i, j: (i, 0))],
        out_shape=[jax.ShapeDtypeStruct((s, D_IN), F32), jax.ShapeDtypeStruct((s, D_MODEL), BF16)],
        compiler_params=_params(("parallel", "arbitrary")),
    )(x, g, w)


def _head_split(src, col_block, name):
    s = src.shape[0]
    t = _tile(s, 512)

    def body(x_ref, r_ref, t_ref):
        xv = x_ref[...]
        xt = xv.T
        for h in range(N_HEADS):
            r_ref[h] = xv[:, h * HEAD_DIM:(h + 1) * HEAD_DIM].astype(BF16)
            t_ref[h] = xt[h * HEAD_DIM:(h + 1) * HEAD_DIM, :].astype(BF16)

    return pl.pallas_call(
        body, name=name, grid=(s // t,),
        in_specs=[pl.BlockSpec((t, D_GROUP), lambda i: (i, col_block))],
        out_specs=[pl.BlockSpec((N_HEADS, t, HEAD_DIM), lambda i: (0, i, 0)),
                   pl.BlockSpec((N_HEADS, HEAD_DIM, t), lambda i: (0, 0, i))],
        out_shape=[jax.ShapeDtypeStruct((N_HEADS, s, HEAD_DIM), BF16),
                   jax.ShapeDtypeStruct((N_HEADS, HEAD_DIM, s), BF16)],
        compiler_params=_params(("parallel",)),
    )(src)


def _mix_a_fwd(z_ref, vg, wt_ref, bmat, t):
    zu = z_ref[:, 0:256]
    zv = z_ref[:, 256:512]
    u = _gelu(zu)
    v = _gelu(zv)
    rv = _rsqrt_mean(v)
    vh = v * rv
    vnb = (vh * vg).astype(BF16)
    head = lax.broadcasted_iota(jnp.int32, (CHUNK, D_GROUP), 1) // HEAD_DIM
    fs = []
    for c in range(t // CHUNK):
        vc = vnb[c * CHUNK:(c + 1) * CHUNK, :]
        fc = bmat
        for h in range(N_HEADS):
            fc = fc + jnp.where(head == h, _dot(wt_ref[h], vc), 0.0)
        fs.append(fc)
    f = jnp.concatenate(fs, axis=0) if len(fs) > 1 else fs[0]
    return zu, zv, u, rv, vh, vnb, f


def _mix_b_fwd(z_ref, zh_ref, first, scw_ref, ext_ref, t):
    gb = z_ref[:, 512:768]
    uh = zh_ref[:, 768:1024] * zh_ref[:, 1024:1280]
    ext_ref[0:HALO, :] = jnp.where(first, 0.0, uh)
    ext_ref[HALO:HALO + t, :] = z_ref[:, 768:1024] * z_ref[:, 1024:1280]
    cv = jnp.zeros((t, D_GROUP), F32)
    for k in range(SHORT_K):
        cv = cv + scw_ref[k:k + 1, :] * ext_ref[pl.ds(HALO - (SHORT_K - 1) + k, t), :]
    return gb, cv


def _mix_d_fwd(z_ref, zh_ref, first, ccw_ref, lg, lb, ext_ref, t):
    hh = zh_ref[:, 2048:2304] * _sigmoid(zh_ref[:, 2304:2560])
    ext_ref[0:HALO, :] = jnp.where(first, 0.0, hh)
    ext_ref[HALO:HALO + t, :] = z_ref[:, 2048:2304] * _sigmoid(z_ref[:, 2304:2560])
    cv = jnp.zeros((t, D_GROUP), F32)
    for k in range(CONF_K):
        cv = cv + ccw_ref[k:k + 1, :] * ext_ref[pl.ds(HALO - (CONF_K - 1) + k, t), :]
    xc = cv - jnp.mean(cv, axis=-1, keepdims=True)
    rs = lax.rsqrt(jnp.mean(xc * xc, axis=-1, keepdims=True) + EPS)
    xh = xc * rs
    ln = xh * lg + lb
    return xh, rs, ln, _sigmoid(ln)


def _mix_specs(t, s):
    per = t // HALO
    return [pl.BlockSpec((t, D_IN), lambda i: (i, 0)),
            pl.BlockSpec((HALO, D_IN), lambda i: (jnp.maximum(i * per - 1, 0), 0))]


def _full(shape):
    return pl.BlockSpec(shape, lambda i: (0,) * len(shape))


def _mixers_fwd(z, p):
    s = z.shape[0]
    t = _tile(s, 256)

    def body(z_ref, zh_ref, vg_ref, wt_ref, bm_ref, scw_ref, ccw_ref, lg_ref, lb_ref, y_ref, eb_ref, ed_ref):
        first = pl.program_id(0) == 0
        _, _, u, _, _, _, f = _mix_a_fwd(z_ref, vg_ref[...], wt_ref, bm_ref[...], t)
        ya = u * f
        y_ref[:, 0:256] = ya * _rsqrt_mean(ya)
        gb, cv = _mix_b_fwd(z_ref, zh_ref, first, scw_ref, eb_ref, t)
        yb = gb * cv
        y_ref[:, 256:512] = yb * _rsqrt_mean(yb)
        _, _, ln, sg = _mix_d_fwd(z_ref, zh_ref, first, ccw_ref, lg_ref[...], lb_ref[...], ed_ref, t)
        yd = ln * sg
        y_ref[:, 512:768] = yd * _rsqrt_mean(yd)

    return pl.pallas_call(
        body, name="mixers_fwd", grid=(s // t,),
        in_specs=_mix_specs(t, s) + [_full((1, D_GROUP)), _full((N_HEADS, CHUNK, CHUNK)), _full((CHUNK, D_GROUP)),
                                     _full((8, D_GROUP)), _full((32, D_GROUP)), _full((1, D_GROUP)), _full((1, D_GROUP))],
        out_specs=pl.BlockSpec((t, 768), lambda i: (i, 0)),
        out_shape=jax.ShapeDtypeStruct((s, 768), F32),
        scratch_shapes=[pltpu.VMEM((HALO + t, D_GROUP), F32), pltpu.VMEM((HALO + t, D_GROUP), F32)],
        compiler_params=_params(("parallel",)),
    )(z, z, p["vg"], p["wt"], p["bmat"], p["scw"], p["ccw"], p["lg"], p["lb"])


def _mixers_bwd_a(z, dyn, o_t, p):
    s = z.shape[0]
    t = _tile(s, 256)
    n_chunk = t // CHUNK

    def body(z_ref, zh_ref, dyn_ref, ot_ref, vg_ref, wt_ref, wtt_ref, bm_ref, scw_ref, ccw_ref, lg_ref, lb_ref,
             dza_ref, dcb_ref, dcd_ref, do_ref, dvg_ref, dws_ref, dbm_ref, dscw_ref, dccw_ref, dlg_ref, dlb_ref,
             eb_ref, ed_ref):
        i = pl.program_id(0)
        first = i == 0

        @pl.when(first)
        def _():
            for r in (dvg_ref, dws_ref, dbm_ref, dscw_ref, dccw_ref, dlg_ref, dlb_ref):
                r[...] = jnp.zeros_like(r)

        def rms_bwd(y, dn):
            r = _rsqrt_mean(y)
            yn = y * r
            return r * (dn - yn * jnp.mean(dn * yn, axis=-1, keepdims=True))

        vg = vg_ref[...]
        zu, zv, u, rv, vh, vnb, f = _mix_a_fwd(z_ref, vg, wt_ref, bm_ref[...], t)
        dya = rms_bwd(u * f, dyn_ref[:, 0:256])
        du = dya * f
        df = dya * u
        head = lax.broadcasted_iota(jnp.int32, (CHUNK, D_GROUP), 1) // HEAD_DIM
        dvns = []
        dbm = jnp.zeros((CHUNK, D_GROUP), F32)
        for c in range(n_chunk):
            dfc = df[c * CHUNK:(c + 1) * CHUNK, :]
            vc = vnb[c * CHUNK:(c + 1) * CHUNK, :]
            dbm = dbm + dfc
            dvn = jnp.zeros((CHUNK, D_GROUP), F32)
            for h in range(N_HEADS):
                dfh = jnp.where(head == h, dfc, 0.0).astype(BF16)
                dvn = dvn + _dot(wtt_ref[h], dfh)
                dws_ref[h] += _dot_nt(dfh, vc)
            dvns.append(dvn)
        dbm_ref[...] += dbm
        dvn = jnp.concatenate(dvns, axis=0) if n_chunk > 1 else dvns[0]
        dvg_ref[...] += jnp.sum(dvn * vh, axis=0, keepdims=True)
        dvh = dvn * vg
        dv = rv * (dvh - vh * jnp.mean(dvh * vh, axis=-1, keepdims=True))
        dza_ref[:, 0:256] = (du * _gelu_grad(zu)).astype(BF16)
        dza_ref[:, 256:512] = (dv * _gelu_grad(zv)).astype(BF16)

        gb, cv = _mix_b_fwd(z_ref, zh_ref, first, scw_ref, eb_ref, t)
        dyb = rms_bwd(gb * cv, dyn_ref[:, 256:512])
        dza_ref[:, 512:768] = (dyb * cv).astype(BF16)
        dcb = dyb * gb
        dcb_ref[...] = dcb
        for k in range(SHORT_K):
            dscw_ref[k:k + 1, :] += jnp.sum(dcb * eb_ref[pl.ds(HALO - (SHORT_K - 1) + k, t), :], axis=0, keepdims=True)

        lg = lg_ref[...]
        xh, rs, ln, sg = _mix_d_fwd(z_ref, zh_ref, first, ccw_ref, lg, lb_ref[...], ed_ref, t)
        dyd = rms_bwd(ln * sg, dyn_ref[:, 768:1024])
        dln = dyd * (sg * (1.0 + ln * (1.0 - sg)))
        dlg_ref[...] += jnp.sum(dln * xh, axis=0, keepdims=True)
        dlb_ref[...] += jnp.sum(dln, axis=0, keepdims=True)
        dxh = dln * lg
        dcd = rs * (dxh - jnp.mean(dxh, axis=-1, keepdims=True) - xh * jnp.mean(dxh * xh, axis=-1, keepdims=True))
        dcd_ref[...] = dcd
        for k in range(CONF_K):
            dccw_ref[k:k + 1, :] += jnp.sum(dcd * ed_ref[pl.ds(HALO - (CONF_K - 1) + k, t), :], axis=0, keepdims=True)

        o = ot_ref[...].reshape(D_GROUP, t).T
        do_ref[...] = rms_bwd(o, dyn_ref[:, 512:768])

    small = [(1, D_GROUP), (N_HEADS, CHUNK, CHUNK), (CHUNK, D_GROUP), (8, D_GROUP), (32, D_GROUP), (1, D_GROUP), (1, D_GROUP)]
    return pl.pallas_call(
        body, name="mixers_bwd_a", grid=(s // t,),
        in_specs=_mix_specs(t, s) + [pl.BlockSpec((t, D_MODEL), lambda i: (i, 0)),
                                     pl.BlockSpec((N_HEADS, HEAD_DIM, t), lambda i: (0, 0, i)),
                                     _full((1, D_GROUP)), _full((N_HEADS, CHUNK, CHUNK)), _full((N_HEADS, CHUNK, CHUNK)),
                                     _full((CHUNK, D_GROUP)), _full((8, D_GROUP)), _full((32, D_GROUP)),
                                     _full((1, D_GROUP)), _full((1, D_GROUP))],
        out_specs=[pl.BlockSpec((t, 768), lambda i: (i, 0)), pl.BlockSpec((t, D_GROUP), lambda i: (i, 0)),
                   pl.BlockSpec((t, D_GROUP), lambda i: (i, 0)), pl.BlockSpec((t, D_GROUP), lambda i: (i, 0))]
                  + [_full(sh) for sh in small],
        out_shape=[jax.ShapeDtypeStruct((s, 768), BF16), jax.ShapeDtypeStruct((s, D_GROUP), F32),
                   jax.ShapeDtypeStruct((s, D_GROUP), F32), jax.ShapeDtypeStruct((s, D_GROUP), F32)]
                  + [jax.ShapeDtypeStruct(sh, F32) for sh in small],
        scratch_shapes=[pltpu.VMEM((HALO + t, D_GROUP), F32), pltpu.VMEM((HALO + t, D_GROUP), F32)],
        compiler_params=_params(("arbitrary",)),
    )(z, z, dyn, o_t, p["vg"], p["wt"], p["wtt"], p["bmat"], p["scw"], p["ccw"], p["lg"], p["lb"])


def _mixers_bwd_b(z, dza, dcb, dcd, dq_t, dk_t, dv_t, p):
    s = z.shape[0]
    t = _tile(s, 256)
    per = t // HALO
    n_halo = s // HALO

    def body(z_ref, dza_ref, dcb_ref, dcbn_ref, dcd_ref, dcdn_ref, dq_ref, dk_ref, dv_ref, scw_ref, ccw_ref,
             dz_ref, eb_ref, ed_ref):
        last = pl.program_id(0) == pl.num_programs(0) - 1
        dz_ref[:, 0:768] = dza_ref[...]
        eb_ref[0:t, :] = dcb_ref[...]
        eb_ref[t:t + HALO, :] = jnp.where(last, 0.0, dcbn_ref[...])
        du = jnp.zeros((t, D_GROUP), F32)
        for k in range(SHORT_K):
            du = du + scw_ref[k:k + 1, :] * eb_ref[pl.ds(SHORT_K - 1 - k, t), :]
        dz_ref[:, 768:1024] = (du * z_ref[:, 1024:1280]).astype(BF16)
        dz_ref[:, 1024:1280] = (du * z_ref[:, 768:1024]).astype(BF16)
        for n, r in enumerate((dq_ref, dk_ref, dv_ref)):
            dz_ref[:, 1280 + 256 * n:1536 + 256 * n] = r[...].reshape(D_GROUP, t).T.astype(BF16)
        ed_ref[0:t, :] = dcd_ref[...]
        ed_ref[t:t + HALO, :] = jnp.where(last, 0.0, dcdn_ref[...])
        dh = jnp.zeros((t, D_GROUP), F32)
        for k in range(CONF_K):
            dh = dh + ccw_ref[k:k + 1, :] * ed_ref[pl.ds(CONF_K - 1 - k, t), :]
        a = z_ref[:, 2048:2304]
        sg = _sigmoid(z_ref[:, 2304:2560])
        dz_ref[:, 2048:2304] = (dh * sg).astype(BF16)
        dz_ref[:, 2304:2560] = (dh * a * sg * (1.0 - sg)).astype(BF16)

    nxt = lambda i: (jnp.minimum((i + 1) * per, n_halo - 1), 0)
    tr = pl.BlockSpec((N_HEADS, HEAD_DIM, t), lambda i: (0, 0, i))
    return pl.pallas_call(
        body, name="mixers_bwd_b", grid=(s // t,),
        in_specs=[pl.BlockSpec((t, D_IN), lambda i: (i, 0)), pl.BlockSpec((t, 768), lambda i: (i, 0)),
                  pl.BlockSpec((t, D_GROUP), lambda i: (i, 0)), pl.BlockSpec((HALO, D_GROUP), nxt),
                  pl.BlockSpec((t, D_GROUP), lambda i: (i, 0)), pl.BlockSpec((HALO, D_GROUP), nxt),
                  tr, tr, tr, _full((8, D_GROUP)), _full((32, D_GROUP))],
        out_specs=pl.BlockSpec((t, D_IN), lambda i: (i, 0)),
        out_shape=jax.ShapeDtypeStruct((s, D_IN), BF16),
        scratch_shapes=[pltpu.VMEM((HALO + t, D_GROUP), F32), pltpu.VMEM((HALO + t, D_GROUP), F32)],
        compiler_params=_params(("parallel",)),
    )(z, dza, dcb, dcb, dcd, dcd, dq_t, dk_t, dv_t, p["scw"], p["ccw"])


def _split_bf16(v):
    hi = v.astype(BF16)
    return hi, (v - hi.astype(F32)).astype(BF16)


def _att_tile(q, kt, i, j, carry, tri):
    z = _dot(q, kt) * ATT_SCALE
    row = lax.broadcasted_iota(jnp.int32, (ATT_TQ, ATT_TK), 0) + i * ATT_TQ
    col = lax.broadcasted_iota(jnp.int32, (ATT_TQ, ATT_TK), 1) + j * ATT_TK
    causal = col < row
    soft = jnp.log(1.0 + jnp.exp(-jnp.abs(z)))
    lm = jnp.where(causal, -(jnp.maximum(z, 0.0) + soft), 0.0)
    lb = -(jnp.maximum(-z, 0.0) + soft)
    hi, lo = _split_bf16(lm)
    right = _dot(hi, tri) + _dot(lo, tri)
    w = jnp.where(causal, jnp.exp(lb + right + carry), 0.0)
    return causal, w, lb, jnp.sum(lm, axis=1, keepdims=True)


def _attn_fwd(q_r, k_t, v_t):
    s = q_r.shape[1]

    def body(q_ref, kt_ref, vt_ref, o_ref):
        i = pl.program_id(1)
        q = q_ref[...]
        tri = (lax.broadcasted_iota(jnp.int32, (ATT_TK, ATT_TK), 0)
               > lax.broadcasted_iota(jnp.int32, (ATT_TK, ATT_TK), 1)).astype(BF16)

        def cond(c):
            return jnp.logical_and(c[0] >= 0, jnp.max(c[1]) > ATT_DEAD)

        def step(c):
            j, carry, acc = c
            off = pl.multiple_of(j * ATT_TK, ATT_TK)
            _, w, _, tot = _att_tile(q, kt_ref[:, pl.ds(off, ATT_TK)], i, j, carry, tri)
            acc = acc + _dot_nt(vt_ref[:, pl.ds(off, ATT_TK)], w.astype(BF16))
            return j - 1, carry + tot, acc

        j_hi = ((i + 1) * ATT_TQ - 1) // ATT_TK
        _, _, acc = lax.while_loop(cond, step, (j_hi, jnp.zeros((ATT_TQ, 1), F32), jnp.zeros((HEAD_DIM, ATT_TQ), F32)))
        o_ref[...] = acc

    whole = pl.BlockSpec((None, HEAD_DIM, s), lambda h, i: (h, 0, 0))
    return pl.pallas_call(
        body, name="attn_fwd", grid=(N_HEADS, s // ATT_TQ),
        in_specs=[pl.BlockSpec((None, ATT_TQ, HEAD_DIM), lambda h, i: (h, i, 0)), whole, whole],
        out_specs=pl.BlockSpec((None, HEAD_DIM, ATT_TQ), lambda h, i: (h, 0, i)),
        out_shape=jax.ShapeDtypeStruct((N_HEADS, HEAD_DIM, s), F32),
        compiler_params=_params(("parallel", "arbitrary"), V7X_VMEM_LIMIT),
    )(q_r, k_t, v_t)


def _attn_bwd(q_r, q_t, k_t, v_t, do_r, do_t):
    s = q_r.shape[1]
    n_k = max(s // ATT_TK, 1)

    def body(q_ref, qt_ref, kt_ref, vt_ref, do_ref, dot_ref, dq_ref, dk_ref, dv_ref, g_ref, sg_ref):
        i = pl.program_id(1)

        @pl.when(i == 0)
        def _():
            dk_ref[...] = jnp.zeros_like(dk_ref)
            dv_ref[...] = jnp.zeros_like(dv_ref)

        q = q_ref[...]
        qt = qt_ref[...]
        do = do_ref[...]
        dot = dot_ref[...]
        r_i = lax.broadcasted_iota(jnp.int32, (ATT_TK, ATT_TK), 0)
        c_i = lax.broadcasted_iota(jnp.int32, (ATT_TK, ATT_TK), 1)
        tri_r = (r_i > c_i).astype(BF16)
        tri_l = (r_i < c_i).astype(BF16)

        def cond(c):
            return jnp.logical_and(c[0] >= 0, jnp.max(c[1]) > ATT_DEAD)

        def sweep1(c):
            j, carry = c
            off = pl.multiple_of(j * ATT_TK, ATT_TK)
            _, w, lb, tot = _att_tile(q, kt_ref[:, pl.ds(off, ATT_TK)], i, j, carry, tri_r)
            g_ref[j] = w * _dot(do, vt_ref[:, pl.ds(off, ATT_TK)])
            sg_ref[j] = jnp.exp(lb).astype(BF16)
            dv_ref[:, pl.ds(off, ATT_TK)] += _dot(dot, w.astype(BF16))
            return j - 1, carry + tot

        j_hi = ((i + 1) * ATT_TQ - 1) // ATT_TK
        j_end, _ = lax.while_loop(cond, sweep1, (j_hi, jnp.zeros((ATT_TQ, 1), F32)))

        def sweep2(j, c):
            pre, acc = c
            off = pl.multiple_of(j * ATT_TK, ATT_TK)
            g = g_ref[j]
            sg = sg_ref[j].astype(F32)
            hi, lo = _split_bf16(g)
            left = pre + _dot(hi, tri_l) + _dot(lo, tri_l)
            row = lax.broadcasted_iota(jnp.int32, (ATT_TQ, ATT_TK), 0) + i * ATT_TQ
            col = lax.broadcasted_iota(jnp.int32, (ATT_TQ, ATT_TK), 1) + j * ATT_TK
            dz = jnp.where(col < row, (g * (1.0 - sg) - left * sg) * ATT_SCALE, 0.0).astype(BF16)
            acc = acc + _dot_nt(kt_ref[:, pl.ds(off, ATT_TK)], dz)
            dk_ref[:, pl.ds(off, ATT_TK)] += _dot(qt, dz)
            return pre + jnp.sum(g, axis=1, keepdims=True), acc

        _, acc = lax.fori_loop(j_end + 1, j_hi + 1, sweep2,
                               (jnp.zeros((ATT_TQ, 1), F32), jnp.zeros((HEAD_DIM, ATT_TQ), F32)))
        dq_ref[...] = acc

    whole = pl.BlockSpec((None, HEAD_DIM, s), lambda h, i: (h, 0, 0))
    rows = pl.BlockSpec((None, ATT_TQ, HEAD_DIM), lambda h, i: (h, i, 0))
    cols = pl.BlockSpec((None, HEAD_DIM, ATT_TQ), lambda h, i: (h, 0, i))
    shape = jax.ShapeDtypeStruct((N_HEADS, HEAD_DIM, s), F32)
    return pl.pallas_call(
        body, name="attn_bwd", grid=(N_HEADS, s // ATT_TQ),
        in_specs=[rows, cols, whole, whole, rows, cols],
        out_specs=[cols, whole, whole],
        out_shape=[shape, shape, shape],
        scratch_shapes=[pltpu.VMEM((n_k, ATT_TQ, ATT_TK), F32), pltpu.VMEM((n_k, ATT_TQ, ATT_TK), BF16)],
        compiler_params=_params(("parallel", "arbitrary"), V7X_VMEM_LIMIT),
    )(q_r, q_t, k_t, v_t, do_r, do_t)


def _out_proj(x, y_abd, o_t, gain, w, layer):
    s = x.shape[0]
    t = _tile(s, 512)

    def body(x_ref, y_ref, ot_ref, g_ref, w_ref, x1_ref, yn_ref):
        o = ot_ref[...].reshape(D_GROUP, t).T
        yn_ref[:, 0:512] = y_ref[:, 0:512]
        yn_ref[:, 512:768] = o * _rsqrt_mean(o)
        yn_ref[:, 768:1024] = y_ref[:, 512:768]
        yg = (yn_ref[...] * g_ref[...]).astype(BF16)
        acc = _dot(yg[:, 0:256], w_ref[0])
        for b in range(1, N_BLK):
            acc = acc + _dot(yg[:, 256 * b:256 * (b + 1)], w_ref[b])
        x1_ref[...] = x_ref[...] + acc

    return pl.pallas_call(
        body, name="out_proj", grid=(s // t,),
        in_specs=[pl.BlockSpec((t, D_MODEL), lambda i: (i, 0)), pl.BlockSpec((t, 768), lambda i: (i, 0)),
                  pl.BlockSpec((N_HEADS, HEAD_DIM, t), lambda i: (0, 0, i)), _full((1, D_MODEL)),
                  pl.BlockSpec((N_BLK, None, D_GROUP, D_MODEL), lambda i: (0, layer, 0, 0))],
        out_specs=[pl.BlockSpec((t, D_MODEL), lambda i: (i, 0)), pl.BlockSpec((t, D_MODEL), lambda i: (i, 0))],
        out_shape=[jax.ShapeDtypeStruct((s, D_MODEL), F32), jax.ShapeDtypeStruct((s, D_MODEL), F32)],
        compiler_params=_params(("parallel",)),
    )(x, y_abd, o_t, gain, w)


def _out_proj_bwd(dx, yn, gain, w, layer):
    s = dx.shape[0]
    t = _tile(s, 512)

    def body(dx_ref, yn_ref, g_ref, w_ref, dyn_ref, dg_ref, dw_ref):
        @pl.when(pl.program_id(0) == 0)
        def _():
            dg_ref[...] = jnp.zeros_like(dg_ref)
            dw_ref[...] = jnp.zeros_like(dw_ref)

        dxb = dx_ref[...].astype(BF16)
        g = g_ref[...]
        yn = yn_ref[...]
        yg = (yn * g).astype(BF16)
        for b in range(N_BLK):
            cols = slice(256 * b, 256 * (b + 1))
            dyg = _dot_nt(dxb, w_ref[b])
            dw_ref[b] += _dot_tn(yg[:, cols], dxb)
            dg_ref[:, cols] += jnp.sum(dyg * yn[:, cols], axis=0, keepdims=True)
            dyn_ref[:, cols] = dyg * g[:, cols]

    return pl.pallas_call(
        body, name="out_proj_bwd", grid=(s // t,),
        in_specs=[pl.BlockSpec((t, D_MODEL), lambda i: (i, 0)), pl.BlockSpec((t, D_MODEL), lambda i: (i, 0)),
                  _full((1, D_MODEL)), pl.BlockSpec((N_BLK, None, D_GROUP, D_MODEL), lambda i: (0, layer, 0, 0))],
        out_specs=[pl.BlockSpec((t, D_MODEL), lambda i: (i, 0)), _full((1, D_MODEL)), _full((N_BLK, D_GROUP, D_MODEL))],
        out_shape=[jax.ShapeDtypeStruct((s, D_MODEL), F32), jax.ShapeDtypeStruct((1, D_MODEL), F32),
                   jax.ShapeDtypeStruct((N_BLK, D_GROUP, D_MODEL), F32)],
        compiler_params=_params(("arbitrary",)),
    )(dx, yn, gain, w)


def _ffn(x, g, w_up, w_down, layer):
    s = x.shape[0]
    t = _tile(s, 512)

    def body(x_ref, g_ref, wu_ref, wd_ref, x2_ref, p_ref, h_ref):
        @pl.when(pl.program_id(1) == 0)
        def _():
            xv = x_ref[...]
            h_ref[...] = (xv * _rsqrt_mean(xv) * g_ref[...]).astype(BF16)
            x2_ref[...] = xv

        pre = _dot(h_ref[...], wu_ref[...])
        p_ref[...] = pre.astype(BF16)
        a = jnp.maximum(pre, 0.0)
        x2_ref[...] += _dot((a * a).astype(BF16), wd_ref[...])

    wspec = pl.BlockSpec((None, None, D_MODEL, D_FF_BLK), lambda i, j: (j, layer, 0, 0))
    return pl.pallas_call(
        body, name="ffn", grid=(s // t, N_BLK),
        in_specs=[pl.BlockSpec((t, D_MODEL), lambda i, j: (i, 0)), pl.BlockSpec((1, D_MODEL), lambda i, j: (0, 0)),
                  wspec, wspec],
        out_specs=[pl.BlockSpec((t, D_MODEL), lambda i, j: (i, 0)), pl.BlockSpec((t, D_FF_BLK), lambda i, j: (i, j))],
        out_shape=[jax.ShapeDtypeStruct((s, D_MODEL), F32), jax.ShapeDtypeStruct((s, N_BLK * D_FF_BLK), BF16)],
        scratch_shapes=[pltpu.VMEM((t, D_MODEL), BF16)],
        compiler_params=_params(("parallel", "arbitrary")),
    )(x, g, w_up, w_down)


def _rms_bwd_rows(xv, g, dh, dres):
    r = _rsqrt_mean(xv)
    xh = xv * r
    dxh = dh * g
    dx = dres + r * (dxh - xh * jnp.mean(dxh * xh, axis=-1, keepdims=True))
    return dx, jnp.sum(dh * xh, axis=0, keepdims=True)


def _ffn_bwd(x, g, dx2, p, w_up, w_down, layer):
    s = x.shape[0]
    t = _tile(s, 512)

    def body(x_ref, g_ref, dx2_ref, p_ref, wu_ref, wd_ref, dx1_ref, dg_ref, hb_ref, dxb_ref, dp_ref, dh_ref):
        i, j = pl.program_id(0), pl.program_id(1)

        @pl.when(jnp.logical_and(i == 0, j == 0))
        def _():
            dg_ref[...] = jnp.zeros_like(dg_ref)

        @pl.when(j == 0)
        def _():
            xv = x_ref[...]
            hb_ref[...] = (xv * _rsqrt_mean(xv) * g_ref[...]).astype(BF16)
            dxb_ref[...] = dx2_ref[...].astype(BF16)
            dh_ref[...] = jnp.zeros_like(dh_ref)

        da = _dot_nt(dxb_ref[...], wd_ref[...])
        a = jnp.maximum(p_ref[...].astype(F32), 0.0)
        dp = (da * (2.0 * a)).astype(BF16)
        dp_ref[...] = dp
        dh_ref[...] += _dot_nt(dp, wu_ref[...])

        @pl.when(j == N_BLK - 1)
        def _():
            dx, dg = _rms_bwd_rows(x_ref[...], g_ref[...], dh_ref[...], dx2_ref[...])
            dx1_ref[...] = dx
            dg_ref[...] += dg

    wspec = pl.BlockSpec((None, None, D_MODEL, D_FF_BLK), lambda i, j: (j, layer, 0, 0))
    row = pl.BlockSpec((t, D_MODEL), lambda i, j: (i, 0))
    return pl.pallas_call(
        body, name="ffn_bwd", grid=(s // t, N_BLK),
        in_specs=[row, pl.BlockSpec((1, D_MODEL), lambda i, j: (0, 0)), row,
                  pl.BlockSpec((t, D_FF_BLK), lambda i, j: (i, j)), wspec, wspec],
        out_specs=[row, pl.BlockSpec((1, D_MODEL), lambda i, j: (0, 0)), row, row,
                   pl.BlockSpec((t, D_FF_BLK), lambda i, j: (i, j))],
        out_shape=[jax.ShapeDtypeStruct((s, D_MODEL), F32), jax.ShapeDtypeStruct((1, D_MODEL), F32),
                   jax.ShapeDtypeStruct((s, D_MODEL), BF16), jax.ShapeDtypeStruct((s, D_MODEL), BF16),
                   jax.ShapeDtypeStruct((s, N_BLK * D_FF_BLK), BF16)],
        scratch_shapes=[pltpu.VMEM((t, D_MODEL), F32)],
        compiler_params=_params(("arbitrary", "arbitrary")),
    )(x, g, dx2, p, w_up, w_down)


def _ffn_wgrad(hb, p, dp, dxb):
    s = hb.shape[0]
    t = _tile(s, 512)

    def body(h_ref, p_ref, dp_ref, dx_ref, du_ref, dd_ref):
        @pl.when(pl.program_id(1) == 0)
        def _():
            du_ref[...] = jnp.zeros_like(du_ref)
            dd_ref[...] = jnp.zeros_like(dd_ref)

        a = jnp.maximum(p_ref[...].astype(F32), 0.0)
        du_ref[...] += _dot_tn(h_ref[...], dp_ref[...])
        dd_ref[...] += _dot_tn((a * a).astype(BF16), dx_ref[...])

    row = pl.BlockSpec((t, D_MODEL), lambda j, i: (i, 0))
    blk = pl.BlockSpec((t, D_FF_BLK), lambda j, i: (i, j))
    out = pl.BlockSpec((None, D_MODEL, D_FF_BLK), lambda j, i: (j, 0, 0))
    shape = jax.ShapeDtypeStruct((N_BLK, D_MODEL, D_FF_BLK), F32)
    return pl.pallas_call(
        body, name="ffn_wgrad", grid=(N_BLK, s // t),
        in_specs=[row, blk, blk, row], out_specs=[out, out], out_shape=[shape, shape],
        compiler_params=_params(("parallel", "arbitrary")),
    )(hb, p, dp, dxb)


def _in_proj_bwd(x, g, dx1, dz, w, layer):
    s = x.shape[0]
    t = _tile(s, 512)

    def body(x_ref, g_ref, dx1_ref, dz_ref, w_ref, dx0_ref, dg_ref, dh_ref):
        i, j = pl.program_id(0), pl.program_id(1)

        @pl.when(jnp.logical_and(i == 0, j == 0))
        def _():
            dg_ref[...] = jnp.zeros_like(dg_ref)

        @pl.when(j == 0)
        def _():
            dh_ref[...] = jnp.zeros_like(dh_ref)

        dh_ref[...] += _dot_nt(dz_ref[...], w_ref[...])

        @pl.when(j == N_BLK - 1)
        def _():
            dx, dg = _rms_bwd_rows(x_ref[...], g_ref[...], dh_ref[...], dx1_ref[...])
            dx0_ref[...] = dx
            dg_ref[...] += dg

    row = pl.BlockSpec((t, D_MODEL), lambda i, j: (i, 0))
    return pl.pallas_call(
        body, name="in_proj_bwd", grid=(s // t, N_BLK),
        in_specs=[row, pl.BlockSpec((1, D_MODEL), lambda i, j: (0, 0)), row,
                  pl.BlockSpec((t, W_IN_BLK), lambda i, j: (i, j)),
                  pl.BlockSpec((None, None, D_MODEL, W_IN_BLK), lambda i, j: (j, layer, 0, 0))],
        out_specs=[row, pl.BlockSpec((1, D_MODEL), lambda i, j: (0, 0))],
        out_shape=[jax.ShapeDtypeStruct((s, D_MODEL), F32), jax.ShapeDtypeStruct((1, D_MODEL), F32)],
        scratch_shapes=[pltpu.VMEM((t, D_MODEL), F32)],
        compiler_params=_params(("arbitrary", "arbitrary")),
    )(x, g, dx1, dz, w)


def _in_proj_wgrad(hb, dz):
    s = hb.shape[0]
    t = _tile(s, 512)

    def body(h_ref, dz_ref, dw_ref):
        @pl.when(pl.program_id(1) == 0)
        def _():
            dw_ref[...] = jnp.zeros_like(dw_ref)

        dw_ref[...] += _dot_tn(h_ref[...], dz_ref[...])

    return pl.pallas_call(
        body, name="in_proj_wgrad", grid=(N_BLK, s // t),
        in_specs=[pl.BlockSpec((t, D_MODEL), lambda j, i: (i, 0)), pl.BlockSpec((t, W_IN_BLK), lambda j, i: (i, j))],
        out_specs=pl.BlockSpec((None, D_MODEL, W_IN_BLK), lambda j, i: (j, 0, 0)),
        out_shape=jax.ShapeDtypeStruct((N_BLK, D_MODEL, W_IN_BLK), F32),
        compiler_params=_params(("parallel", "arbitrary")),
    )(hb, dz)


def _loss_head(x, g, target):
    s = x.shape[0]
    t = _tile(s, 512)

    def body(x_ref, g_ref, t_ref, l_ref, dx_ref, dg_ref):
        @pl.when(pl.program_id(0) == 0)
        def _():
            l_ref[...] = jnp.zeros_like(l_ref)
            dg_ref[...] = jnp.zeros_like(dg_ref)

        xv = x_ref[...]
        g = g_ref[...]
        r = _rsqrt_mean(xv)
        xh = xv * r
        err = xh * g - t_ref[...]
        l_ref[...] += 0.5 * jnp.sum(jnp.mean(err * err, axis=-1, keepdims=True), axis=0, keepdims=True)
        dy = err * (1.0 / D_MODEL)
        dg_ref[...] += jnp.sum(dy * xh, axis=0, keepdims=True)
        dxh = dy * g
        dx_ref[...] = r * (dxh - xh * jnp.mean(dxh * xh, axis=-1, keepdims=True))

    row = pl.BlockSpec((t, D_MODEL), lambda i: (i, 0))
    return pl.pallas_call(
        body, name="loss_head", grid=(s // t,),
        in_specs=[row, _full((1, D_MODEL)), row],
        out_specs=[_full((1, 128)), row, _full((1, D_MODEL))],
        out_shape=[jax.ShapeDtypeStruct((1, 128), F32), jax.ShapeDtypeStruct((s, D_MODEL), F32),
                   jax.ShapeDtypeStruct((1, D_MODEL), F32)],
        compiler_params=_params(("arbitrary",)),
    )(x, g, target)


def _layer_params(small, layer):
    tril = jnp.tril(jnp.ones((CHUNK, CHUNK), bool))
    ws = jnp.where(tril, small["gmlp_w_s"][layer], 0.0)
    bmat = jnp.repeat(small["gmlp_b_s"][layer].T, HEAD_DIM, axis=1)
    scw = jnp.zeros((8, D_GROUP), F32).at[:SHORT_K].set(small["short_conv_w"][layer])
    ccw = jnp.zeros((32, D_GROUP), F32).at[:CONF_K].set(small["conf_conv_w"][layer])
    return dict(vg=small["gmlp_v_g"][layer][None], wt=ws.astype(BF16), wtt=jnp.swapaxes(ws, 1, 2).astype(BF16),
                bmat=bmat, scw=scw, ccw=ccw, lg=small["conf_ln_g"][layer][None], lb=small["conf_ln_b"][layer][None])


def _local_step(x, target, big, small):
    saved = []
    for l in range(DEPTH):
        p = _layer_params(small, l)
        z, hb = _in_proj(x, small["norm_mix_g"][l][None], big["w_in"], l)
        q_r, q_t = _head_split(z, 5, "split_q")
        _, k_t = _head_split(z, 6, "split_k")
        _, v_t = _head_split(z, 7, "split_v")
        y_abd = _mixers_fwd(z, p)
        o_t = _attn_fwd(q_r, k_t, v_t)
        x1, yn = _out_proj(x, y_abd, o_t, small["mix_out_g"][l][None], big["w_out"], l)
        x2, pre = _ffn(x1, small["norm_ffn_g"][l][None], big["w_up"], big["w_down"], l)
        saved.append(dict(p=p, x0=x, z=z, hb=hb, q_r=q_r, q_t=q_t, k_t=k_t, v_t=v_t, o_t=o_t, x1=x1, yn=yn, pre=pre))
        x = x2

    loss, dx, d_final = _loss_head(x, small["final_norm_g"][None], target)

    g = {k: [None] * DEPTH for k in ("w_in", "w_out", "w_up", "w_down", "norm_mix_g", "gmlp_v_g", "gmlp_w_s", "gmlp_b_s",
                                     "short_conv_w", "conf_conv_w", "conf_ln_g", "conf_ln_b", "mix_out_g", "norm_ffn_g")}
    tril = jnp.tril(jnp.ones((CHUNK, CHUNK), bool))
    for l in reversed(range(DEPTH)):
        sv = saved[l]
        p = sv["p"]
        dx1, g["norm_ffn_g"][l], h2b, dx2b, dpre = _ffn_bwd(sv["x1"], small["norm_ffn_g"][l][None], dx, sv["pre"],
                                                             big["w_up"], big["w_down"], l)
        g["w_up"][l], g["w_down"][l] = _ffn_wgrad(h2b, sv["pre"], dpre, dx2b)
        dyn, g["mix_out_g"][l], g["w_out"][l] = _out_proj_bwd(dx1, sv["yn"], small["mix_out_g"][l][None], big["w_out"], l)
        dza, dcb, dcd, do, dvg, dws, dbm, dscw, dccw, dlg, dlb = _mixers_bwd_a(sv["z"], dyn, sv["o_t"], p)
        do_r, do_t = _head_split(do, 0, "split_do")
        dq_t, dk_t, dv_t = _attn_bwd(sv["q_r"], sv["q_t"], sv["k_t"], sv["v_t"], do_r, do_t)
        dz = _mixers_bwd_b(sv["z"], dza, dcb, dcd, dq_t, dk_t, dv_t, p)
        dx, g["norm_mix_g"][l] = _in_proj_bwd(sv["x0"], small["norm_mix_g"][l][None], dx1, dz, big["w_in"], l)
        g["w_in"][l] = _in_proj_wgrad(sv["hb"], dz)
        g["gmlp_v_g"][l] = dvg[0]
        g["gmlp_w_s"][l] = jnp.where(tril, dws, 0.0)
        g["gmlp_b_s"][l] = dbm.reshape(CHUNK, N_HEADS, HEAD_DIM).sum(-1).T
        g["short_conv_w"][l] = dscw[:SHORT_K]
        g["conf_conv_w"][l] = dccw[:CONF_K]
        g["conf_ln_g"][l] = dlg[0]
        g["conf_ln_b"][l] = dlb[0]
        g["norm_mix_g"][l] = g["norm_mix_g"][l][0]
        g["mix_out_g"][l] = g["mix_out_g"][l][0]
        g["norm_ffn_g"][l] = g["norm_ffn_g"][l][0]
    grads = {k: jnp.stack(v) for k, v in g.items()}
    grads["final_norm_g"] = d_final[0]
    return loss, dx, grads


_ANY = pl.BlockSpec(memory_space=pl.ANY)


def _mesh_place():
    x, y, c = lax.axis_index("x"), lax.axis_index("y"), lax.axis_index("c")
    chips = [(1 - x, y), (x, 1 - y), (1 - x, 1 - y)]
    return x, y, c, 2 * x + y, chips


def _gather_blocks(blocks):
    n = len(blocks)

    def body(*refs):
        ins, outs = refs[:n], refs[n:2 * n]
        send_sems, recv_sems, local_sems = refs[2 * n:]
        x, y, c, me, chips = _mesh_place()

        def remote(k, r, block, to):
            return pltpu.make_async_remote_copy(
                src_ref=ins[k], dst_ref=outs[k].at[block], send_sem=send_sems.at[3 * k + r],
                recv_sem=recv_sems.at[3 * k + r], device_id=(to[0], to[1], c), device_id_type=MESH)

        local = [pltpu.make_async_copy(ins[k], outs[k].at[me], local_sems.at[k]) for k in range(n)]
        sent = [remote(k, r, me, chip) for k in range(n) for r, chip in enumerate(chips)]
        for cp in local + sent:
            cp.start()
        for k in range(n):
            for r, chip in enumerate(chips):
                remote(k, r, 2 * chip[0] + chip[1], chip).wait_recv()
        for cp in sent:
            cp.wait_send()
        for cp in local:
            cp.wait()

    return pl.pallas_call(
        body, name="gather_weights",
        in_specs=[_ANY] * n, out_specs=[_ANY] * n,
        out_shape=[jax.ShapeDtypeStruct((N_BLK,) + b.shape, b.dtype) for b in blocks],
        scratch_shapes=[pltpu.SemaphoreType.DMA((3 * n,)), pltpu.SemaphoreType.DMA((3 * n,)), pltpu.SemaphoreType.DMA((n,))],
    )(*blocks)


def _swap_halves(gp):
    def body(g_ref, r_ref, send_sem, recv_sem):
        x, y, c, _, _ = _mesh_place()
        cp = pltpu.make_async_remote_copy(
            src_ref=g_ref.at[:, pl.ds((1 - c) * HALF_ROWS, HALF_ROWS), :], dst_ref=r_ref,
            send_sem=send_sem, recv_sem=recv_sem, device_id=(x, y, 1 - c), device_id_type=MESH)
        cp.start()
        cp.wait()

    return pl.pallas_call(
        body, name="swap_halves", in_specs=[_ANY], out_specs=_ANY,
        out_shape=jax.ShapeDtypeStruct((N_BLK, HALF_ROWS, D_MODEL), F32),
        scratch_shapes=[pltpu.SemaphoreType.DMA, pltpu.SemaphoreType.DMA],
    )(gp)


def _send_to_owners(s1):
    def body(s_ref, r_ref, send_sems, recv_sems, local_sem):
        x, y, c, me, chips = _mesh_place()

        def remote(r, src_block, dst_block, to):
            return pltpu.make_async_remote_copy(
                src_ref=s_ref.at[src_block], dst_ref=r_ref.at[dst_block], send_sem=send_sems.at[r],
                recv_sem=recv_sems.at[r], device_id=(to[0], to[1], c), device_id_type=MESH)

        local = pltpu.make_async_copy(s_ref.at[me], r_ref.at[me], local_sem)
        local.start()
        sent = [remote(r, 2 * chip[0] + chip[1], me, chip) for r, chip in enumerate(chips)]
        for cp in sent:
            cp.start()
        for r, chip in enumerate(chips):
            remote(r, me, 2 * chip[0] + chip[1], chip).wait_recv()
        for cp in sent:
            cp.wait_send()
        local.wait()

    return pl.pallas_call(
        body, name="send_to_owners", in_specs=[_ANY], out_specs=_ANY,
        out_shape=jax.ShapeDtypeStruct((N_BLK, HALF_ROWS, D_MODEL), F32),
        scratch_shapes=[pltpu.SemaphoreType.DMA((3,)), pltpu.SemaphoreType.DMA((3,)), pltpu.SemaphoreType.DMA],
    )(s1)


def _join_halves(f):
    def body(f_ref, o_ref, send_sem, recv_sem, local_sem):
        x, y, c, _, _ = _mesh_place()
        local = pltpu.make_async_copy(f_ref, o_ref.at[c], local_sem)
        local.start()
        cp = pltpu.make_async_remote_copy(src_ref=f_ref, dst_ref=o_ref.at[c], send_sem=send_sem, recv_sem=recv_sem,
                                          device_id=(x, y, 1 - c), device_id_type=MESH)
        cp.start()
        pltpu.make_async_remote_copy(src_ref=f_ref, dst_ref=o_ref.at[1 - c], send_sem=send_sem, recv_sem=recv_sem,
                                     device_id=(x, y, 1 - c), device_id_type=MESH).wait_recv()
        cp.wait_send()
        local.wait()

    return pl.pallas_call(
        body, name="join_halves", in_specs=[_ANY], out_specs=_ANY,
        out_shape=jax.ShapeDtypeStruct((2, HALF_ROWS, D_MODEL), F32),
        scratch_shapes=[pltpu.SemaphoreType.DMA, pltpu.SemaphoreType.DMA, pltpu.SemaphoreType.DMA],
    )(f)


def _add_pairs(a, b):
    n, r, _ = a.shape
    t = 512

    def body(a_ref, b_ref, o_ref):
        o_ref[...] = a_ref[...] + b_ref[...]

    spec = pl.BlockSpec((None, t, D_MODEL), lambda i, j: (i, j, 0))
    return pl.pallas_call(
        body, name="add_pairs", grid=(n, r // t), in_specs=[spec, spec], out_specs=spec,
        out_shape=jax.ShapeDtypeStruct(a.shape, F32), compiler_params=_params(("parallel", "parallel")),
    )(a, b)


def _add_chips(r2):
    _, r, _ = r2.shape
    t = 512

    def body(r_ref, o_ref):
        o_ref[...] = ((r_ref[0] + r_ref[1]) + r_ref[2]) + r_ref[3]

    return pl.pallas_call(
        body, name="add_chips", grid=(r // t,),
        in_specs=[pl.BlockSpec((N_BLK, t, D_MODEL), lambda i: (0, i, 0))],
        out_specs=pl.BlockSpec((t, D_MODEL), lambda i: (i, 0)),
        out_shape=jax.ShapeDtypeStruct((r, D_MODEL), F32), compiler_params=_params(("parallel",)),
    )(r2)


def _adamw(g, w, m, v):
    r = g.shape[0]
    t = 512
    c1 = 1.0 - ADAM_B1 ** ADAM_STEP
    c2 = 1.0 - ADAM_B2 ** ADAM_STEP

    def body(g_ref, w_ref, m_ref, v_ref, d_ref, mo_ref, vo_ref):
        gv = g_ref[...]
        m_new = ADAM_B1 * m_ref[...] + (1.0 - ADAM_B1) * gv
        v_new = ADAM_B2 * v_ref[...] + (1.0 - ADAM_B2) * (gv * gv)
        mo_ref[...] = m_new
        vo_ref[...] = v_new
        d_ref[...] = -ADAM_LR * ((m_new / c1) / (jnp.sqrt(v_new / c2) + ADAM_EPS) + ADAM_WD * w_ref[...])

    spec = pl.BlockSpec((t, D_MODEL), lambda i: (i, 0))
    shape = jax.ShapeDtypeStruct(g.shape, F32)
    return pl.pallas_call(
        body, name="adamw", grid=(r // t,), in_specs=[spec] * 4, out_specs=[spec] * 3, out_shape=[shape] * 3,
        compiler_params=_params(("parallel",)),
    )(g, w, m, v)


_REPLICATED = ("norm_mix_g", "gmlp_v_g", "gmlp_w_s", "gmlp_b_s", "conf_ln_g", "conf_ln_b", "mix_out_g", "norm_ffn_g",
               "final_norm_g")
_REP_SHAPES = {"norm_mix_g": (DEPTH, D_MODEL), "gmlp_v_g": (DEPTH, D_GROUP), "gmlp_w_s": (DEPTH, N_HEADS, CHUNK, CHUNK),
               "gmlp_b_s": (DEPTH, N_HEADS, CHUNK), "conf_ln_g": (DEPTH, D_GROUP), "conf_ln_b": (DEPTH, D_GROUP),
               "mix_out_g": (DEPTH, D_MODEL), "norm_ffn_g": (DEPTH, D_MODEL), "final_norm_g": (D_MODEL,)}
_BIG_ROWS = (("w_in", 1280), ("w_out", 512), ("w_up", 2048), ("w_down", 2048))
_CONV_ROWS = 8
_REP_ROWS = 144
_CH_BLK = D_GROUP // N_BLK


def _pad_rows(flat, rows):
    pad = rows * D_MODEL - flat.shape[-1]
    flat = jnp.pad(flat, [(0, 0)] * (flat.ndim - 1) + [(0, pad)])
    return flat.reshape(flat.shape[:-1] + (rows, D_MODEL))


def _pack(big, scw, ccw, rep):
    lead = scw.shape[:-3]
    conv = jnp.concatenate([scw.reshape(lead + (-1,)), ccw.reshape(lead + (-1,))], axis=-1)
    flat = jnp.concatenate([rep[k].reshape(-1) for k in _REPLICATED])
    flat = jnp.broadcast_to(flat, lead + flat.shape)
    used = sum(r for _, r in _BIG_ROWS) + _CONV_ROWS + _REP_ROWS
    parts = [big[k] for k, _ in _BIG_ROWS] + [_pad_rows(conv, _CONV_ROWS), _pad_rows(flat, _REP_ROWS),
                                             jnp.zeros(lead + (PACK_ROWS - used, D_MODEL), F32)]
    return jnp.concatenate(parts, axis=-2)


def _unpack(pk):
    out = {}
    row = 0
    shapes = {"w_in": (DEPTH, D_MODEL, W_IN_BLK), "w_out": (DEPTH, D_GROUP, D_MODEL),
              "w_up": (DEPTH, D_MODEL, D_FF_BLK), "w_down": (DEPTH, D_FF_BLK, D_MODEL)}
    for k, r in _BIG_ROWS:
        out[k] = pk[row:row + r].reshape(shapes[k])
        row += r
    conv = pk[row:row + _CONV_ROWS].reshape(-1)
    n_s = DEPTH * SHORT_K * _CH_BLK
    out["short_conv_w"] = conv[:n_s].reshape(DEPTH, SHORT_K, _CH_BLK)
    out["conf_conv_w"] = conv[n_s:n_s + DEPTH * CONF_K * _CH_BLK].reshape(DEPTH, CONF_K, _CH_BLK)
    row += _CONV_ROWS
    flat = pk[row:row + _REP_ROWS].reshape(-1)
    at = 0
    for k in _REPLICATED:
        n = math.prod(_REP_SHAPES[k])
        out[k] = flat[at:at + n].reshape(_REP_SHAPES[k])
        at += n
    return out


def _conv_blocks(w):
    d, k, _ = w.shape
    return w.reshape(d, k, N_BLK, _CH_BLK).transpose(2, 0, 1, 3)


_WEIGHTS = ("norm_mix_g", "w_in", "gmlp_v_g", "gmlp_w_s", "gmlp_b_s", "short_conv_w", "conf_conv_w", "conf_ln_g",
            "conf_ln_b", "mix_out_g", "w_out", "norm_ffn_g", "w_up", "w_down", "final_norm_g")


def kernel(x, norm_mix_g, w_in, gmlp_v_g, gmlp_w_s, gmlp_b_s, short_conv_w, conf_conv_w, conf_ln_g, conf_ln_b, mix_out_g, w_out, norm_ffn_g, w_up, w_down, final_norm_g, loss_target, m_norm_mix_g, m_w_in, m_gmlp_v_g, m_gmlp_w_s, m_gmlp_b_s, m_short_conv_w, m_conf_conv_w, m_conf_ln_g, m_conf_ln_b, m_mix_out_g, m_w_out, m_norm_ffn_g, m_w_up, m_w_down, m_final_norm_g, v_norm_mix_g, v_w_in, v_gmlp_v_g, v_gmlp_w_s, v_gmlp_b_s, v_short_conv_w, v_conf_conv_w, v_conf_ln_g, v_conf_ln_b, v_mix_out_g, v_w_out, v_norm_ffn_g, v_w_up, v_w_down, v_final_norm_g):
    w = dict(norm_mix_g=norm_mix_g, w_in=w_in, gmlp_v_g=gmlp_v_g, gmlp_w_s=gmlp_w_s, gmlp_b_s=gmlp_b_s,
             short_conv_w=short_conv_w, conf_conv_w=conf_conv_w, conf_ln_g=conf_ln_g, conf_ln_b=conf_ln_b,
             mix_out_g=mix_out_g, w_out=w_out, norm_ffn_g=norm_ffn_g, w_up=w_up, w_down=w_down, final_norm_g=final_norm_g)
    m = dict(norm_mix_g=m_norm_mix_g, w_in=m_w_in, gmlp_v_g=m_gmlp_v_g, gmlp_w_s=m_gmlp_w_s, gmlp_b_s=m_gmlp_b_s,
             short_conv_w=m_short_conv_w, conf_conv_w=m_conf_conv_w, conf_ln_g=m_conf_ln_g, conf_ln_b=m_conf_ln_b,
             mix_out_g=m_mix_out_g, w_out=m_w_out, norm_ffn_g=m_norm_ffn_g, w_up=m_w_up, w_down=m_w_down,
             final_norm_g=m_final_norm_g)
    v = dict(norm_mix_g=v_norm_mix_g, w_in=v_w_in, gmlp_v_g=v_gmlp_v_g, gmlp_w_s=v_gmlp_w_s, gmlp_b_s=v_gmlp_b_s,
             short_conv_w=v_short_conv_w, conf_conv_w=v_conf_conv_w, conf_ln_g=v_conf_ln_g, conf_ln_b=v_conf_ln_b,
             mix_out_g=v_mix_out_g, w_out=v_w_out, norm_ffn_g=v_norm_ffn_g, w_up=v_w_up, w_down=v_w_down,
             final_norm_g=v_final_norm_g)
    c = lax.axis_index("c")

    conv_mine = _pad_rows(jnp.concatenate([short_conv_w.reshape(-1), conf_conv_w.reshape(-1)]), _CONV_ROWS)
    gathered = _gather_blocks([_cast_bf16(w[k], "cast_" + k) for k, _ in _BIG_ROWS] + [conv_mine])
    big = dict(zip([k for k, _ in _BIG_ROWS], gathered[:4]))
    conv_all = gathered[4].reshape(N_BLK, -1)
    n_s = DEPTH * SHORT_K * _CH_BLK
    scw_all = conv_all[:, :n_s].reshape(N_BLK, DEPTH, SHORT_K, _CH_BLK)
    ccw_all = conv_all[:, n_s:n_s + DEPTH * CONF_K * _CH_BLK].reshape(N_BLK, DEPTH, CONF_K, _CH_BLK)
    small = {k: w[k] for k in _REPLICATED}
    small["short_conv_w"] = scw_all.transpose(1, 2, 0, 3).reshape(DEPTH, SHORT_K, D_GROUP)
    small["conf_conv_w"] = ccw_all.transpose(1, 2, 0, 3).reshape(DEPTH, CONF_K, D_GROUP)

    loss, dx, g = _local_step(x[0], loss_target[0], big, small)

    g_big = {k: jnp.swapaxes(g[k], 0, 1).reshape(N_BLK, r, D_MODEL) for k, r in _BIG_ROWS}
    gp = _pack(g_big, _conv_blocks(g["short_conv_w"]), _conv_blocks(g["conf_conv_w"]), g)
    mine = lax.dynamic_index_in_dim(gp.reshape(N_BLK, 2, HALF_ROWS, D_MODEL), c, axis=1, keepdims=False)
    chip_sum = _add_pairs(mine, _swap_halves(gp))
    total_half = _add_chips(_send_to_owners(chip_sum))
    g_mine = _join_halves(total_half).reshape(PACK_ROWS, D_MODEL)

    def pack_own(t):
        return _pack({k: t[k].reshape(r, D_MODEL) for k, r in _BIG_ROWS}, t["short_conv_w"], t["conf_conv_w"], t)

    delta, m_new, v_new = _adamw(g_mine, pack_own(w), pack_own(m), pack_own(v))

    outs = [lax.psum(loss[0, 0], ("x", "y", "c")), dx[None]]
    for pk in (g_mine, delta, m_new, v_new):
        parts = _unpack(pk)
        outs += [parts[k] for k in _WEIGHTS]
    return tuple(outs)
```

```python
import math

import jax
import jax.numpy as jnp
from jax import lax
from jax.experimental import pallas as pl
from jax.experimental.pallas import tpu as pltpu

F32 = jnp.float32
BF16 = jnp.bfloat16

D_MODEL = 1024
D_GROUP = 256
N_HEADS = 4
HEAD_DIM = 64
CHUNK = 128
D_IN = 2560
N_BLK = 4
W_IN_BLK = D_IN // N_BLK
D_FF_BLK = 1024
DEPTH = 2
EPS = 1e-6
HALO = 32
SHORT_K = 3
CONF_K = 31
ATT_TQ = 256
ATT_TK = 256
ATT_SCALE = 0.125
ATT_DEAD = -104.0
V7X_VMEM_LIMIT = 56 * 1024 * 1024

ADAM_LR, ADAM_B1, ADAM_B2, ADAM_EPS, ADAM_WD, ADAM_STEP = 0.001, 0.9, 0.999, 1e-08, 0.01, 10

PACK_ROWS = 6144
HALF_ROWS = PACK_ROWS // 2
MESH = pl.DeviceIdType.MESH


def _params(sem, vmem=None):
    return pltpu.CompilerParams(dimension_semantics=sem, vmem_limit_bytes=vmem)


def _tile(s, t):
    return min(s, t)


def _rsqrt_mean(v):
    return lax.rsqrt(jnp.mean(v * v, axis=-1, keepdims=True) + EPS)


def _sigmoid(v):
    return 1.0 / (1.0 + jnp.exp(-v))


_GELU_C = math.sqrt(2.0 / math.pi)


def _gelu(v):
    t = jnp.tanh(_GELU_C * (v + 0.044715 * (v * v * v)))
    return v * (0.5 * (1.0 + t))


def _gelu_grad(v):
    t = jnp.tanh(_GELU_C * (v + 0.044715 * (v * v * v)))
    return 0.5 * (1.0 + t) + v * (0.5 * (1.0 - t * t) * _GELU_C * (1.0 + 3.0 * 0.044715 * (v * v)))


def _dot(a, b):
    return jnp.dot(a, b, preferred_element_type=F32)


def _dot_nt(a, b):
    return lax.dot_general(a, b, (((1,), (1,)), ((), ())), preferred_element_type=F32)


def _dot_tn(a, b):
    return lax.dot_general(a, b, (((0,), (0,)), ((), ())), preferred_element_type=F32)


def _cast_bf16(w, name):
    shape = w.shape
    w3 = w.reshape((-1,) + shape[-2:])
    n, r, c = w3.shape
    tr = _tile(r, 256)

    def body(w_ref, o_ref):
        o_ref[...] = w_ref[...].astype(BF16)

    out = pl.pallas_call(
        body, name=name, grid=(n, r // tr),
        in_specs=[pl.BlockSpec((None, tr, c), lambda a, b: (a, b, 0))],
        out_specs=pl.BlockSpec((None, tr, c), lambda a, b: (a, b, 0)),
        out_shape=jax.ShapeDtypeStruct(w3.shape, BF16),
        compiler_params=_params(("parallel", "parallel")),
    )(w3)
    return out.reshape(shape)


def _in_proj(x, g, w, layer):
    s = x.shape[0]
    t = _tile(s, 512)

    def body(x_ref, g_ref, w_ref, z_ref, h_ref):
        xv = x_ref[...]
        h = (xv * _rsqrt_mean(xv) * g_ref[...]).astype(BF16)
        h_ref[...] = h
        for b in range(N_BLK):
            z_ref[:, b * W_IN_BLK:(b + 1) * W_IN_BLK] = _dot(h, w_ref[b])

    return pl.pallas_call(
        body, name="in_proj", grid=(s // t,),
        in_specs=[pl.BlockSpec((t, D_MODEL), lambda i: (i, 0)), _full((1, D_MODEL)),
                  pl.BlockSpec((N_BLK, None, D_MODEL, W_IN_BLK), lambda i: (0, layer, 0, 0))],
        out_specs=[pl.BlockSpec((t, D_IN), lambda i: (i, 0)), pl.BlockSpec((t, D_MODEL), lambda i: (i, 0))],
        out_shape=[jax.ShapeDtypeStruct((s, D_IN), F32), jax.ShapeDtypeStruct((s, D_MODEL), BF16)],
        compiler_params=_params(("parallel",), V7X_VMEM_LIMIT),
    )(x, g, w)


def _head_split(src, col_block, name, scale=1.0):
    s = src.shape[0]
    t = _tile(s, 512)

    def body(x_ref, r_ref, t_ref):
        xv = x_ref[...] * scale
        xt = xv.T
        for h in range(N_HEADS):
            r_ref[h] = xv[:, h * HEAD_DIM:(h + 1) * HEAD_DIM].astype(BF16)
            t_ref[h] = xt[h * HEAD_DIM:(h + 1) * HEAD_DIM, :].astype(BF16)

    return pl.pallas_call(
        body, name=name, grid=(s // t,),
        in_specs=[pl.BlockSpec((t, D_GROUP), lambda i: (i, col_block))],
        out_specs=[pl.BlockSpec((N_HEADS, t, HEAD_DIM), lambda i: (0, i, 0)),
                   pl.BlockSpec((N_HEADS, HEAD_DIM, t), lambda i: (0, 0, i))],
        out_shape=[jax.ShapeDtypeStruct((N_HEADS, s, HEAD_DIM), BF16),
                   jax.ShapeDtypeStruct((N_HEADS, HEAD_DIM, s), BF16)],
        compiler_params=_params(("parallel",)),
    )(src)


def _mix_a_fwd(z_ref, vg, wt_ref, bmat, t):
    zu = z_ref[:, 0:256]
    zv = z_ref[:, 256:512]
    u = _gelu(zu)
    v = _gelu(zv)
    rv = _rsqrt_mean(v)
    vh = v * rv
    vnb = (vh * vg).astype(BF16)
    head = lax.broadcasted_iota(jnp.int32, (CHUNK, D_GROUP), 1) // HEAD_DIM
    fs = []
    for c in range(t // CHUNK):
        vc = vnb[c * CHUNK:(c + 1) * CHUNK, :]
        fc = bmat
        for h in range(N_HEADS):
            fc = fc + jnp.where(head == h, _dot(wt_ref[h], vc), 0.0)
        fs.append(fc)
    f = jnp.concatenate(fs, axis=0) if len(fs) > 1 else fs[0]
    return zu, zv, u, rv, vh, vnb, f


def _mix_b_fwd(z_ref, zh_ref, first, scw_ref, ext_ref, t):
    gb = z_ref[:, 512:768]
    uh = zh_ref[:, 768:1024] * zh_ref[:, 1024:1280]
    ext_ref[0:HALO, :] = jnp.where(first, 0.0, uh)
    ext_ref[HALO:HALO + t, :] = z_ref[:, 768:1024] * z_ref[:, 1024:1280]
    cv = jnp.zeros((t, D_GROUP), F32)
    for k in range(SHORT_K):
        cv = cv + scw_ref[k:k + 1, :] * ext_ref[pl.ds(HALO - (SHORT_K - 1) + k, t), :]
    return gb, cv


def _mix_d_fwd(z_ref, zh_ref, first, ccw_ref, lg, lb, ext_ref, t):
    hh = zh_ref[:, 2048:2304] * _sigmoid(zh_ref[:, 2304:2560])
    ext_ref[0:HALO, :] = jnp.where(first, 0.0, hh)
    ext_ref[HALO:HALO + t, :] = z_ref[:, 2048:2304] * _sigmoid(z_ref[:, 2304:2560])
    cv = jnp.zeros((t, D_GROUP), F32)
    for k in range(CONF_K):
        cv = cv + ccw_ref[k:k + 1, :] * ext_ref[pl.ds(HALO - (CONF_K - 1) + k, t), :]
    xc = cv - jnp.mean(cv, axis=-1, keepdims=True)
    rs = lax.rsqrt(jnp.mean(xc * xc, axis=-1, keepdims=True) + EPS)
    xh = xc * rs
    ln = xh * lg + lb
    return xh, rs, ln, _sigmoid(ln)


def _mix_specs(t, s):
    per = t // HALO
    return [pl.BlockSpec((t, D_IN), lambda i: (i, 0)),
            pl.BlockSpec((HALO, D_IN), lambda i: (jnp.maximum(i * per - 1, 0), 0))]


def _full(shape):
    return pl.BlockSpec(shape, lambda i: (0,) * len(shape))


def _mixers_fwd(z, p):
    s = z.shape[0]
    t = _tile(s, 256)

    def body(z_ref, zh_ref, vg_ref, wt_ref, bm_ref, scw_ref, ccw_ref, lg_ref, lb_ref, y_ref, eb_ref, ed_ref):
        first = pl.program_id(0) == 0
        _, _, u, _, _, _, f = _mix_a_fwd(z_ref, vg_ref[...], wt_ref, bm_ref[...], t)
        ya = u * f
        y_ref[:, 0:256] = ya * _rsqrt_mean(ya)
        gb, cv = _mix_b_fwd(z_ref, zh_ref, first, scw_ref, eb_ref, t)
        yb = gb * cv
        y_ref[:, 256:512] = yb * _rsqrt_mean(yb)
        _, _, ln, sg = _mix_d_fwd(z_ref, zh_ref, first, ccw_ref, lg_ref[...], lb_ref[...], ed_ref, t)
        yd = ln * sg
        y_ref[:, 512:768] = yd * _rsqrt_mean(yd)

    return pl.pallas_call(
        body, name="mixers_fwd", grid=(s // t,),
        in_specs=_mix_specs(t, s) + [_full((1, D_GROUP)), _full((N_HEADS, CHUNK, CHUNK)), _full((CHUNK, D_GROUP)),
                                     _full((8, D_GROUP)), _full((32, D_GROUP)), _full((1, D_GROUP)), _full((1, D_GROUP))],
        out_specs=pl.BlockSpec((t, 768), lambda i: (i, 0)),
        out_shape=jax.ShapeDtypeStruct((s, 768), F32),
        scratch_shapes=[pltpu.VMEM((HALO + t, D_GROUP), F32), pltpu.VMEM((HALO + t, D_GROUP), F32)],
        compiler_params=_params(("parallel",)),
    )(z, z, p["vg"], p["wt"], p["bmat"], p["scw"], p["ccw"], p["lg"], p["lb"])


def _mixers_bwd_a(z, dyn, o_t, p):
    s = z.shape[0]
    t = _tile(s, 256)
    n_chunk = t // CHUNK

    def body(z_ref, zh_ref, dyn_ref, ot_ref, vg_ref, wt_ref, wtt_ref, bm_ref, scw_ref, ccw_ref, lg_ref, lb_ref,
             dza_ref, dcb_ref, dcd_ref, do_ref, ds_ref, dvg_ref, dws_ref, dbm_ref, dscw_ref, dccw_ref, dlg_ref, dlb_ref,
             eb_ref, ed_ref):
        i = pl.program_id(0)
        first = i == 0

        @pl.when(first)
        def _():
            for r in (dvg_ref, dws_ref, dbm_ref, dscw_ref, dccw_ref, dlg_ref, dlb_ref):
                r[...] = jnp.zeros_like(r)

        def rms_bwd(y, dn):
            r = _rsqrt_mean(y)
            yn = y * r
            return r * (dn - yn * jnp.mean(dn * yn, axis=-1, keepdims=True))

        vg = vg_ref[...]
        zu, zv, u, rv, vh, vnb, f = _mix_a_fwd(z_ref, vg, wt_ref, bm_ref[...], t)
        dya = rms_bwd(u * f, dyn_ref[:, 0:256])
        du = dya * f
        df = dya * u
        head = lax.broadcasted_iota(jnp.int32, (CHUNK, D_GROUP), 1) // HEAD_DIM
        dvns = []
        dbm = jnp.zeros((CHUNK, D_GROUP), F32)
        for c in range(n_chunk):
            dfc = df[c * CHUNK:(c + 1) * CHUNK, :]
            vc = vnb[c * CHUNK:(c + 1) * CHUNK, :]
            dbm = dbm + dfc
            dvn = jnp.zeros((CHUNK, D_GROUP), F32)
            for h in range(N_HEADS):
                dfh = jnp.where(head == h, dfc, 0.0).astype(BF16)
                dvn = dvn + _dot(wtt_ref[h], dfh)
                dws_ref[h] += _dot_nt(dfh, vc)
            dvns.append(dvn)
        dbm_ref[...] += dbm
        dvn = jnp.concatenate(dvns, axis=0) if n_chunk > 1 else dvns[0]
        dvg_ref[...] += jnp.sum(dvn * vh, axis=0, keepdims=True)
        dvh = dvn * vg
        dv = rv * (dvh - vh * jnp.mean(dvh * vh, axis=-1, keepdims=True))
        dza_ref[:, 0:256] = (du * _gelu_grad(zu)).astype(BF16)
        dza_ref[:, 256:512] = (dv * _gelu_grad(zv)).astype(BF16)

        gb, cv = _mix_b_fwd(z_ref, zh_ref, first, scw_ref, eb_ref, t)
        dyb = rms_bwd(gb * cv, dyn_ref[:, 256:512])
        dza_ref[:, 512:768] = (dyb * cv).astype(BF16)
        dcb = dyb * gb
        dcb_ref[...] = dcb
        for k in range(SHORT_K):
            dscw_ref[k:k + 1, :] += jnp.sum(dcb * eb_ref[pl.ds(HALO - (SHORT_K - 1) + k, t), :], axis=0, keepdims=True)

        lg = lg_ref[...]
        xh, rs, ln, sg = _mix_d_fwd(z_ref, zh_ref, first, ccw_ref, lg, lb_ref[...], ed_ref, t)
        dyd = rms_bwd(ln * sg, dyn_ref[:, 768:1024])
        dln = dyd * (sg * (1.0 + ln * (1.0 - sg)))
        dlg_ref[...] += jnp.sum(dln * xh, axis=0, keepdims=True)
        dlb_ref[...] += jnp.sum(dln, axis=0, keepdims=True)
        dxh = dln * lg
        dcd = rs * (dxh - jnp.mean(dxh, axis=-1, keepdims=True) - xh * jnp.mean(dxh * xh, axis=-1, keepdims=True))
        dcd_ref[...] = dcd
        for k in range(CONF_K):
            dccw_ref[k:k + 1, :] += jnp.sum(dcd * ed_ref[pl.ds(HALO - (CONF_K - 1) + k, t), :], axis=0, keepdims=True)

        o = ot_ref[...].reshape(D_GROUP, t).T
        do = rms_bwd(o, dyn_ref[:, 512:768])
        do_ref[...] = do
        prod = do.astype(BF16).astype(F32) * o
        for h in range(N_HEADS):
            ds_ref[h] = jnp.sum(prod[:, h * HEAD_DIM:(h + 1) * HEAD_DIM], axis=1, keepdims=True)

    small =[(1, D_GROUP), (N_HEADS, CHUNK, CHUNK), (CHUNK, D_GROUP), (8, D_GROUP), (32, D_GROUP), (1, D_GROUP), (1, D_GROUP)]
    return pl.pallas_call(
        body, name="mixers_bwd_a", grid=(s // t,),
        in_specs=_mix_specs(t, s) + [pl.BlockSpec((t, D_MODEL), lambda i: (i, 0)),
                                     pl.BlockSpec((N_HEADS, HEAD_DIM, t), lambda i: (0, 0, i)),
                                     _full((1, D_GROUP)), _full((N_HEADS, CHUNK, CHUNK)), _full((N_HEADS, CHUNK, CHUNK)),
                                     _full((CHUNK, D_GROUP)), _full((8, D_GROUP)), _full((32, D_GROUP)),
                                     _full((1, D_GROUP)), _full((1, D_GROUP))],
        out_specs=[pl.BlockSpec((t, 768), lambda i: (i, 0)), pl.BlockSpec((t, D_GROUP), lambda i: (i, 0)),
                   pl.BlockSpec((t, D_GROUP), lambda i: (i, 0)), pl.BlockSpec((t, D_GROUP), lambda i: (i, 0)),
                   pl.BlockSpec((N_HEADS, t, 1), lambda i: (0, i, 0))]
                  + [_full(sh) for sh in small],
        out_shape=[jax.ShapeDtypeStruct((s, 768), BF16), jax.ShapeDtypeStruct((s, D_GROUP), F32),
                   jax.ShapeDtypeStruct((s, D_GROUP), F32), jax.ShapeDtypeStruct((s, D_GROUP), F32),
                   jax.ShapeDtypeStruct((N_HEADS, s, 1), F32)]
                  + [jax.ShapeDtypeStruct(sh, F32) for sh in small],
        scratch_shapes=[pltpu.VMEM((HALO + t, D_GROUP), F32), pltpu.VMEM((HALO + t, D_GROUP), F32)],
        compiler_params=_params(("arbitrary",)),
    )(z, z, dyn, o_t, p["vg"], p["wt"], p["wtt"], p["bmat"], p["scw"], p["ccw"], p["lg"], p["lb"])


def _mixers_bwd_b(z, dza, dcb, dcd, dq_t, dk_t, dv_t, p):
    s = z.shape[0]
    t = _tile(s, 256)
    per = t // HALO
    n_halo = s // HALO

    def body(z_ref, dza_ref, dcb_ref, dcbn_ref, dcd_ref, dcdn_ref, dq_ref, dk_ref, dv_ref, scw_ref, ccw_ref,
             dz_ref, eb_ref, ed_ref):
        last = pl.program_id(0) == pl.num_programs(0) - 1
        dz_ref[:, 0:768] = dza_ref[...]
        eb_ref[0:t, :] = dcb_ref[...]
        eb_ref[t:t + HALO, :] = jnp.where(last, 0.0, dcbn_ref[...])
        du = jnp.zeros((t, D_GROUP), F32)
        for k in range(SHORT_K):
            du = du + scw_ref[k:k + 1, :] * eb_ref[pl.ds(SHORT_K - 1 - k, t), :]
        dz_ref[:, 768:1024] = (du * z_ref[:, 1024:1280]).astype(BF16)
        dz_ref[:, 1024:1280] = (du * z_ref[:, 768:1024]).astype(BF16)
        for n, r in enumerate((dq_ref, dk_ref, dv_ref)):
            dz_ref[:, 1280 + 256 * n:1536 + 256 * n] = r[...].reshape(D_GROUP, t).T.astype(BF16)
        ed_ref[0:t, :] = dcd_ref[...]
        ed_ref[t:t + HALO, :] = jnp.where(last, 0.0, dcdn_ref[...])
        dh = jnp.zeros((t, D_GROUP), F32)
        for k in range(CONF_K):
            dh = dh + ccw_ref[k:k + 1, :] * ed_ref[pl.ds(CONF_K - 1 - k, t), :]
        a = z_ref[:, 2048:2304]
        sg = _sigmoid(z_ref[:, 2304:2560])
        dz_ref[:, 2048:2304] = (dh * sg).astype(BF16)
        dz_ref[:, 2304:2560] = (dh * a * sg * (1.0 - sg)).astype(BF16)

    nxt = lambda i: (jnp.minimum((i + 1) * per, n_halo - 1), 0)
    tr = pl.BlockSpec((N_HEADS, HEAD_DIM, t), lambda i: (0, 0, i))
    return pl.pallas_call(
        body, name="mixers_bwd_b", grid=(s // t,),
        in_specs=[pl.BlockSpec((t, D_IN), lambda i: (i, 0)), pl.BlockSpec((t, 768), lambda i: (i, 0)),
                  pl.BlockSpec((t, D_GROUP), lambda i: (i, 0)), pl.BlockSpec((HALO, D_GROUP), nxt),
                  pl.BlockSpec((t, D_GROUP), lambda i: (i, 0)), pl.BlockSpec((HALO, D_GROUP), nxt),
                  tr, tr, tr, _full((8, D_GROUP)), _full((32, D_GROUP))],
        out_specs=pl.BlockSpec((t, D_IN), lambda i: (i, 0)),
        out_shape=jax.ShapeDtypeStruct((s, D_IN), BF16),
        scratch_shapes=[pltpu.VMEM((HALO + t, D_GROUP), F32), pltpu.VMEM((HALO + t, D_GROUP), F32)],
        compiler_params=_params(("parallel",)),
    )(z, dza, dcb, dcb, dcd, dcd, dq_t, dk_t, dv_t, p["scw"], p["ccw"])


def _split_bf16(v):
    hi = v.astype(BF16)
    return hi, (v - hi.astype(F32)).astype(BF16)


def _att_scores(qs, kts, carries, tri, mask):
    zs = [_dot(q, kt) for q, kt in zip(qs, kts)]
    lms, lbs, parts = [], [], []
    for z in zs:
        soft = jnp.log(1.0 + jnp.exp(-jnp.abs(z)))
        lm = -(jnp.maximum(z, 0.0) + soft)
        lbs.append(lm + z)
        if mask is not None:
            lm = jnp.where(mask, lm, 0.0)
        lms.append(lm)
        parts.append(_split_bf16(lm))
    rights = [_dot(hi, tri) + _dot(lo, tri) for hi, lo in parts]
    ws = []
    for lb, right, carry in zip(lbs, rights, carries):
        w = jnp.exp(lb + right + carry)
        ws.append(w if mask is None else jnp.where(mask, w, 0.0))
    return ws, lbs, [jnp.sum(lm, axis=1, keepdims=True) for lm in lms]


def _att_consts(i):
    j_hi = ((i + 1) * ATT_TQ - 1) // ATT_TK
    row = lax.broadcasted_iota(jnp.int32, (ATT_TQ, ATT_TK), 0) + i * ATT_TQ
    col = lax.broadcasted_iota(jnp.int32, (ATT_TQ, ATT_TK), 1) + j_hi * ATT_TK
    r_i = lax.broadcasted_iota(jnp.int32, (ATT_TK, ATT_TK), 0)
    c_i = lax.broadcasted_iota(jnp.int32, (ATT_TK, ATT_TK), 1)
    return j_hi, col < row, r_i, c_i


def _att_alive(j, carries):
    top = carries[0]
    for c in carries[1:]:
        top = jnp.maximum(top, c)
    return jnp.logical_and(j >= 0, jnp.max(top) > ATT_DEAD)


def _attn_fwd(q_r, k_t, v_t):
    s = q_r.shape[1]

    def body(q_ref, kt_ref, vt_ref, o_ref):
        j_hi, mask, r_i, c_i = _att_consts(pl.program_id(0))
        tri = (r_i > c_i).astype(BF16)

        heads = range(N_HEADS)

        def tiles(j, carries, accs, mask):
            cols = pl.ds(pl.multiple_of(j * ATT_TK, ATT_TK), ATT_TK)
            ws, _, tots = _att_scores([q_ref[h] for h in heads], [kt_ref[h, :, cols] for h in heads], carries, tri, mask)
            accs = [acc + _dot_nt(vt_ref[h, :, cols], w.astype(BF16)) for h, acc, w in zip(heads, accs, ws)]
            return [c + t for c, t in zip(carries, tots)], accs

        state = tiles(j_hi, [jnp.zeros((ATT_TQ, 1), F32)] * N_HEADS, [jnp.zeros((HEAD_DIM, ATT_TQ), F32)] * N_HEADS, mask)

        def cond(c):
            return _att_alive(c[0], c[1])

        def step(c):
            return (c[0] - 1,) + tuple(tiles(c[0], c[1], c[2], None))

        _, _, accs = lax.while_loop(cond, step, (j_hi - 1,) + tuple(state))
        for h in heads:
            o_ref[h] = accs[h]

    whole = pl.BlockSpec((N_HEADS, HEAD_DIM, s), lambda i: (0, 0, 0), pipeline_mode=pl.Buffered(1))
    return pl.pallas_call(
        body, name="attn_fwd", grid=(s // ATT_TQ,),
        in_specs=[pl.BlockSpec((N_HEADS, ATT_TQ, HEAD_DIM), lambda i: (0, i, 0)), whole, whole],
        out_specs=pl.BlockSpec((N_HEADS, HEAD_DIM, ATT_TQ), lambda i: (0, 0, i)),
        out_shape=jax.ShapeDtypeStruct((N_HEADS, HEAD_DIM, s), F32),
        compiler_params=_params(("arbitrary",), V7X_VMEM_LIMIT),
    )(q_r, k_t, v_t)


ATT_BWD_HEADS = 2


def _attn_bwd(q_r, q_t, k_t, v_t, do_r, do_t, dsum):
    s = q_r.shape[1]
    hps = ATT_BWD_HEADS

    def body(q_ref, qt_ref, kt_ref, vt_ref, do_ref, dot_ref, ds_ref, dq_ref, dk_ref, dv_ref):
        i = pl.program_id(1)

        @pl.when(i == 0)
        def _():
            dk_ref[...] = jnp.zeros_like(dk_ref)
            dv_ref[...] = jnp.zeros_like(dv_ref)

        j_hi, mask, r_i, c_i = _att_consts(i)
        tri_r = (r_i > c_i).astype(BF16)
        tri_ge = (r_i >= c_i).astype(BF16)

        heads = range(hps)

        def tiles(j, carries, gsums, accs, mask):
            cols = pl.ds(pl.multiple_of(j * ATT_TK, ATT_TK), ATT_TK)
            kts = [kt_ref[h, :, cols] for h in heads]
            das = [_dot(do_ref[h], vt_ref[h, :, cols]) for h in heads]
            ws, lbs, tots = _att_scores([q_ref[h] for h in heads], kts, carries, tri_r, mask)
            wbs = [w.astype(BF16) for w in ws]
            gs = [wb.astype(F32) * da for wb, da in zip(wbs, das)]
            parts = [_split_bf16(g) for g in gs]
            sfx = [_dot(hi, tri_ge) + _dot(lo, tri_ge) for hi, lo in parts]
            for h in heads:
                dv_ref[h, :, cols] += _dot(dot_ref[h], wbs[h])
            dzs = []
            for h in heads:
                left = ds_ref[h] - gsums[h] - sfx[h]
                dz = gs[h] - jnp.exp(lbs[h]) * (gs[h] + left)
                dzs.append((dz if mask is None else jnp.where(mask, dz, 0.0)).astype(BF16))
            for h in heads:
                dk_ref[h, :, cols] += _dot(qt_ref[h], dzs[h])
            accs = [accs[h] + _dot_nt(kts[h], dzs[h]) for h in heads]
            gsums = [gsums[h] + jnp.sum(gs[h], axis=1, keepdims=True) for h in heads]
            return [c + t for c, t in zip(carries, tots)], gsums, accs

        col0 = [jnp.zeros((ATT_TQ, 1), F32)] * hps
        state = tiles(j_hi, col0, col0, [jnp.zeros((HEAD_DIM, ATT_TQ), F32)] * hps, mask)

        def cond(c):
            return _att_alive(c[0], c[1])

        def step(c):
            return (c[0] - 1,) + tuple(tiles(c[0], c[1], c[2], c[3], None))

        _, _, _, accs = lax.while_loop(cond, step, (j_hi - 1,) + tuple(state))
        for h in heads:
            dq_ref[h] = accs[h] * ATT_SCALE

    whole = pl.BlockSpec((hps, HEAD_DIM, s), lambda g, i: (g, 0, 0))
    whole_in = pl.BlockSpec((hps, HEAD_DIM, s), lambda g, i: (g, 0, 0), pipeline_mode=pl.Buffered(1))
    rows = pl.BlockSpec((hps, ATT_TQ, HEAD_DIM), lambda g, i: (g, i, 0))
    cols = pl.BlockSpec((hps, HEAD_DIM, ATT_TQ), lambda g, i: (g, 0, i))
    shape = jax.ShapeDtypeStruct((N_HEADS, HEAD_DIM, s), F32)
    return pl.pallas_call(
        body, name="attn_bwd", grid=(N_HEADS // hps, s // ATT_TQ),
        in_specs=[rows, cols, whole_in, whole_in, rows, cols, pl.BlockSpec((hps, ATT_TQ, 1), lambda g, i: (g, i, 0))],
        out_specs=[cols, whole, whole],
        out_shape=[shape, shape, shape],
        compiler_params=_params(("parallel", "arbitrary"), V7X_VMEM_LIMIT),
    )(q_r, q_t, k_t, v_t, do_r, do_t, dsum)


def _out_proj(x, y_abd, o_t, gain, w, layer):
    s = x.shape[0]
    t = _tile(s, 512)

    def body(x_ref, y_ref, ot_ref, g_ref, w_ref, x1_ref, yn_ref):
        o = ot_ref[...].reshape(D_GROUP, t).T
        yn_ref[:, 0:512] = y_ref[:, 0:512]
        yn_ref[:, 512:768] = o * _rsqrt_mean(o)
        yn_ref[:, 768:1024] = y_ref[:, 512:768]
        yg = (yn_ref[...] * g_ref[...]).astype(BF16)
        acc = _dot(yg[:, 0:256], w_ref[0])
        for b in range(1, N_BLK):
            acc = acc + _dot(yg[:, 256 * b:256 * (b + 1)], w_ref[b])
        x1_ref[...] = x_ref[...] + acc

    return pl.pallas_call(
        body, name="out_proj", grid=(s // t,),
        in_specs=[pl.BlockSpec((t, D_MODEL), lambda i: (i, 0)), pl.BlockSpec((t, 768), lambda i: (i, 0)),
                  pl.BlockSpec((N_HEADS, HEAD_DIM, t), lambda i: (0, 0, i)), _full((1, D_MODEL)),
                  pl.BlockSpec((N_BLK, None, D_GROUP, D_MODEL), lambda i: (0, layer, 0, 0))],
        out_specs=[pl.BlockSpec((t, D_MODEL), lambda i: (i, 0)), pl.BlockSpec((t, D_MODEL), lambda i: (i, 0))],
        out_shape=[jax.ShapeDtypeStruct((s, D_MODEL), F32), jax.ShapeDtypeStruct((s, D_MODEL), F32)],
        compiler_params=_params(("parallel",)),
    )(x, y_abd, o_t, gain, w)


def _out_proj_bwd(dx, yn, gain, w, layer):
    s = dx.shape[0]
    t = _tile(s, 512)

    def body(dx_ref, yn_ref, g_ref, w_ref, dyn_ref, dg_ref, dw_ref):
        @pl.when(pl.program_id(0) == 0)
        def _():
            dg_ref[...] = jnp.zeros_like(dg_ref)
            dw_ref[...] = jnp.zeros_like(dw_ref)

        dxb = dx_ref[...].astype(BF16)
        g = g_ref[...]
        yn = yn_ref[...]
        yg = (yn * g).astype(BF16)
        for b in range(N_BLK):
            cols = slice(256 * b, 256 * (b + 1))
            dyg = _dot_nt(dxb, w_ref[b])
            dw_ref[b] += _dot_tn(yg[:, cols], dxb)
            dg_ref[:, cols] += jnp.sum(dyg * yn[:, cols], axis=0, keepdims=True)
            dyn_ref[:, cols] = dyg * g[:, cols]

    return pl.pallas_call(
        body, name="out_proj_bwd", grid=(s // t,),
        in_specs=[pl.BlockSpec((t, D_MODEL), lambda i: (i, 0)), pl.BlockSpec((t, D_MODEL), lambda i: (i, 0)),
                  _full((1, D_MODEL)), pl.BlockSpec((N_BLK, None, D_GROUP, D_MODEL), lambda i: (0, layer, 0, 0))],
        out_specs=[pl.BlockSpec((t, D_MODEL), lambda i: (i, 0)), _full((1, D_MODEL)), _full((N_BLK, D_GROUP, D_MODEL))],
        out_shape=[jax.ShapeDtypeStruct((s, D_MODEL), F32), jax.ShapeDtypeStruct((1, D_MODEL), F32),
                   jax.ShapeDtypeStruct((N_BLK, D_GROUP, D_MODEL), F32)],
        compiler_params=_params(("arbitrary",)),
    )(dx, yn, gain, w)


def _ffn(x, g, w_up, w_down, layer):
    s = x.shape[0]
    t = _tile(s, 1024)

    def body(x_ref, g_ref, wu_ref, wd_ref, x2_ref, p_ref, h_ref):
        @pl.when(pl.program_id(1) == 0)
        def _():
            xv = x_ref[...]
            h_ref[...] = (xv * _rsqrt_mean(xv) * g_ref[...]).astype(BF16)
            x2_ref[...] = xv

        pre = _dot(h_ref[...], wu_ref[...])
        p_ref[...] = pre.astype(BF16)
        a = jnp.maximum(pre, 0.0)
        x2_ref[...] += _dot((a * a).astype(BF16), wd_ref[...])

    wspec = pl.BlockSpec((None, None, D_MODEL, D_FF_BLK), lambda i, j: (j, layer, 0, 0))
    return pl.pallas_call(
        body, name="ffn", grid=(s // t, N_BLK),
        in_specs=[pl.BlockSpec((t, D_MODEL), lambda i, j: (i, 0)), pl.BlockSpec((1, D_MODEL), lambda i, j: (0, 0)),
                  wspec, wspec],
        out_specs=[pl.BlockSpec((t, D_MODEL), lambda i, j: (i, 0)), pl.BlockSpec((t, D_FF_BLK), lambda i, j: (i, j))],
        out_shape=[jax.ShapeDtypeStruct((s, D_MODEL), F32), jax.ShapeDtypeStruct((s, N_BLK * D_FF_BLK), BF16)],
        scratch_shapes=[pltpu.VMEM((t, D_MODEL), BF16)],
        compiler_params=_params(("parallel", "arbitrary"), V7X_VMEM_LIMIT),
    )(x, g, w_up, w_down)


def _rms_bwd_rows(xv, g, dh, dres):
    r = _rsqrt_mean(xv)
    xh = xv * r
    dxh = dh * g
    dx = dres + r * (dxh - xh * jnp.mean(dxh * xh, axis=-1, keepdims=True))
    return dx, jnp.sum(dh * xh, axis=0, keepdims=True)


def _ffn_bwd(x, g, dx2, p, w_up, w_down, layer):
    s = x.shape[0]
    t = _tile(s, 512)

    def body(x_ref, g_ref, dx2_ref, p_ref, wu_ref, wd_ref, dx1_ref, dg_ref, hb_ref, dxb_ref, dp_ref, dh_ref):
        i, j = pl.program_id(0), pl.program_id(1)

        @pl.when(jnp.logical_and(i == 0, j == 0))
        def _():
            dg_ref[...] = jnp.zeros_like(dg_ref)

        @pl.when(j == 0)
        def _():
            xv = x_ref[...]
            hb_ref[...] = (xv * _rsqrt_mean(xv) * g_ref[...]).astype(BF16)
            dxb_ref[...] = dx2_ref[...].astype(BF16)
            dh_ref[...] = jnp.zeros_like(dh_ref)

        da = _dot_nt(dxb_ref[...], wd_ref[...])
        a = jnp.maximum(p_ref[...].astype(F32), 0.0)
        dp = (da * (2.0 * a)).astype(BF16)
        dp_ref[...] = dp
        dh_ref[...] += _dot_nt(dp, wu_ref[...])

        @pl.when(j == N_BLK - 1)
        def _():
            dx, dg = _rms_bwd_rows(x_ref[...], g_ref[...], dh_ref[...], dx2_ref[...])
            dx1_ref[...] = dx
            dg_ref[...] += dg

    wspec = pl.BlockSpec((None, None, D_MODEL, D_FF_BLK), lambda i, j: (j, layer, 0, 0))
    row = pl.BlockSpec((t, D_MODEL), lambda i, j: (i, 0))
    return pl.pallas_call(
        body, name="ffn_bwd", grid=(s // t, N_BLK),
        in_specs=[row, pl.BlockSpec((1, D_MODEL), lambda i, j: (0, 0)), row,
                  pl.BlockSpec((t, D_FF_BLK), lambda i, j: (i, j)), wspec, wspec],
        out_specs=[row, pl.BlockSpec((1, D_MODEL), lambda i, j: (0, 0)), row, row,
                   pl.BlockSpec((t, D_FF_BLK), lambda i, j: (i, j))],
        out_shape=[jax.ShapeDtypeStruct((s, D_MODEL), F32), jax.ShapeDtypeStruct((1, D_MODEL), F32),
                   jax.ShapeDtypeStruct((s, D_MODEL), BF16), jax.ShapeDtypeStruct((s, D_MODEL), BF16),
                   jax.ShapeDtypeStruct((s, N_BLK * D_FF_BLK), BF16)],
        scratch_shapes=[pltpu.VMEM((t, D_MODEL), F32)],
        compiler_params=_params(("arbitrary", "arbitrary")),
    )(x, g, dx2, p, w_up, w_down)


def _ffn_wgrad(hb, p, dp, dxb):
    s = hb.shape[0]
    t = _tile(s, 1024)

    def body(h_ref, p_ref, dp_ref, dx_ref, du_ref, dd_ref):
        @pl.when(pl.program_id(1) == 0)
        def _():
            du_ref[...] = jnp.zeros_like(du_ref)
            dd_ref[...] = jnp.zeros_like(dd_ref)

        a = jnp.maximum(p_ref[...].astype(F32), 0.0)
        du_ref[...] += _dot_tn(h_ref[...], dp_ref[...])
        dd_ref[...] += _dot_tn((a * a).astype(BF16), dx_ref[...])

    row = pl.BlockSpec((t, D_MODEL), lambda j, i: (i, 0))
    blk = pl.BlockSpec((t, D_FF_BLK), lambda j, i: (i, j))
    out = pl.BlockSpec((None, D_MODEL, D_FF_BLK), lambda j, i: (j, 0, 0))
    shape = jax.ShapeDtypeStruct((N_BLK, D_MODEL, D_FF_BLK), F32)
    return pl.pallas_call(
        body, name="ffn_wgrad", grid=(N_BLK, s // t),
        in_specs=[row, blk, blk, row], out_specs=[out, out], out_shape=[shape, shape],
        compiler_params=_params(("parallel", "arbitrary"), V7X_VMEM_LIMIT),
    )(hb, p, dp, dxb)


def _in_proj_bwd(x, g, dx1, dz, w, layer):
    s = x.shape[0]
    t = _tile(s, 512)

    def body(x_ref, g_ref, dx1_ref, dz_ref, w_ref, dx0_ref, dg_ref):
        @pl.when(pl.program_id(0) == 0)
        def _():
            dg_ref[...] = jnp.zeros_like(dg_ref)

        dh = _dot_nt(dz_ref[:, 0:W_IN_BLK], w_ref[0])
        for b in range(1, N_BLK):
            dh = dh + _dot_nt(dz_ref[:, b * W_IN_BLK:(b + 1) * W_IN_BLK], w_ref[b])
        dx, dg = _rms_bwd_rows(x_ref[...], g_ref[...], dh, dx1_ref[...])
        dx0_ref[...] = dx
        dg_ref[...] += dg

    row = pl.BlockSpec((t, D_MODEL), lambda i: (i, 0))
    return pl.pallas_call(
        body, name="in_proj_bwd", grid=(s // t,),
        in_specs=[row, _full((1, D_MODEL)), row, pl.BlockSpec((t, D_IN), lambda i: (i, 0)),
                  pl.BlockSpec((N_BLK, None, D_MODEL, W_IN_BLK), lambda i: (0, layer, 0, 0))],
        out_specs=[row, _full((1, D_MODEL))],
        out_shape=[jax.ShapeDtypeStruct((s, D_MODEL), F32), jax.ShapeDtypeStruct((1, D_MODEL), F32)],
        compiler_params=_params(("arbitrary",), V7X_VMEM_LIMIT),
    )(x, g, dx1, dz, w)


def _in_proj_wgrad(hb, dz):
    s = hb.shape[0]
    t = _tile(s, 512)

    def body(h_ref, dz_ref, dw_ref):
        @pl.when(pl.program_id(0) == 0)
        def _():
            dw_ref[...] = jnp.zeros_like(dw_ref)

        h = h_ref[...]
        for b in range(N_BLK):
            dw_ref[b] += _dot_tn(h, dz_ref[:, b * W_IN_BLK:(b + 1) * W_IN_BLK])

    return pl.pallas_call(
        body, name="in_proj_wgrad", grid=(s // t,),
        in_specs=[pl.BlockSpec((t, D_MODEL), lambda i: (i, 0)), pl.BlockSpec((t, D_IN), lambda i: (i, 0))],
        out_specs=_full((N_BLK, D_MODEL, W_IN_BLK)),
        out_shape=jax.ShapeDtypeStruct((N_BLK, D_MODEL, W_IN_BLK), F32),
        compiler_params=_params(("arbitrary",), V7X_VMEM_LIMIT),
    )(hb, dz)


def _loss_head(x, g, target):
    s = x.shape[0]
    t = _tile(s, 512)

    def body(x_ref, g_ref, t_ref, l_ref, dx_ref, dg_ref):
        @pl.when(pl.program_id(0) == 0)
        def _():
            l_ref[...] = jnp.zeros_like(l_ref)
            dg_ref[...] = jnp.zeros_like(dg_ref)

        xv = x_ref[...]
        g = g_ref[...]
        r = _rsqrt_mean(xv)
        xh = xv * r
        err = xh * g - t_ref[...]
        l_ref[...] += 0.5 * jnp.sum(jnp.mean(err * err, axis=-1, keepdims=True), axis=0, keepdims=True)
        dy = err * (1.0 / D_MODEL)
        dg_ref[...] += jnp.sum(dy * xh, axis=0, keepdims=True)
        dxh = dy * g
        dx_ref[...] = r * (dxh - xh * jnp.mean(dxh * xh, axis=-1, keepdims=True))

    row = pl.BlockSpec((t, D_MODEL), lambda i: (i, 0))
    return pl.pallas_call(
        body, name="loss_head", grid=(s // t,),
        in_specs=[row, _full((1, D_MODEL)), row],
        out_specs=[_full((1, 128)), row, _full((1, D_MODEL))],
        out_shape=[jax.ShapeDtypeStruct((1, 128), F32), jax.ShapeDtypeStruct((s, D_MODEL), F32),
                   jax.ShapeDtypeStruct((1, D_MODEL), F32)],
        compiler_params=_params(("arbitrary",)),
    )(x, g, target)


def _layer_params(small, layer):
    tril = jnp.tril(jnp.ones((CHUNK, CHUNK), bool))
    ws = jnp.where(tril, small["gmlp_w_s"][layer], 0.0)
    bmat = jnp.repeat(small["gmlp_b_s"][layer].T, HEAD_DIM, axis=1)
    scw = jnp.zeros((8, D_GROUP), F32).at[:SHORT_K].set(small["short_conv_w"][layer])
    ccw = jnp.zeros((32, D_GROUP), F32).at[:CONF_K].set(small["conf_conv_w"][layer])
    return dict(vg=small["gmlp_v_g"][layer][None], wt=ws.astype(BF16), wtt=jnp.swapaxes(ws, 1, 2).astype(BF16),
                bmat=bmat, scw=scw, ccw=ccw, lg=small["conf_ln_g"][layer][None], lb=small["conf_ln_b"][layer][None])


def _local_step(x, target, big, small):
    saved = []
    for l in range(DEPTH):
        p = _layer_params(small, l)
        z, hb = _in_proj(x, small["norm_mix_g"][l][None], big["w_in"], l)
        q_r, q_t = _head_split(z, 5, "split_q", ATT_SCALE)
        _, k_t = _head_split(z, 6, "split_k")
        _, v_t = _head_split(z, 7, "split_v")
        y_abd = _mixers_fwd(z, p)
        o_t = _attn_fwd(q_r, k_t, v_t)
        x1, yn = _out_proj(x, y_abd, o_t, small["mix_out_g"][l][None], big["w_out"], l)
        x2, pre = _ffn(x1, small["norm_ffn_g"][l][None], big["w_up"], big["w_down"], l)
        saved.append(dict(p=p, x0=x, z=z, hb=hb, q_r=q_r, q_t=q_t, k_t=k_t, v_t=v_t, o_t=o_t, x1=x1, yn=yn, pre=pre))
        x = x2

    loss, dx, d_final = _loss_head(x, small["final_norm_g"][None], target)

    g = {k: [None] * DEPTH for k in ("w_in", "w_out", "w_up", "w_down", "norm_mix_g", "gmlp_v_g", "gmlp_w_s", "gmlp_b_s",
                                     "short_conv_w", "conf_conv_w", "conf_ln_g", "conf_ln_b", "mix_out_g", "norm_ffn_g")}
    tril = jnp.tril(jnp.ones((CHUNK, CHUNK), bool))
    for l in reversed(range(DEPTH)):
        sv = saved[l]
        p = sv["p"]
        dx1, g["norm_ffn_g"][l], h2b, dx2b, dpre = _ffn_bwd(sv["x1"], small["norm_ffn_g"][l][None], dx, sv["pre"],
                                                             big["w_up"], big["w_down"], l)
        g["w_up"][l], g["w_down"][l] = _ffn_wgrad(h2b, sv["pre"], dpre, dx2b)
        dyn, g["mix_out_g"][l], g["w_out"][l] = _out_proj_bwd(dx1, sv["yn"], small["mix_out_g"][l][None], big["w_out"], l)
        dza, dcb, dcd, do, dsum, dvg, dws, dbm, dscw, dccw, dlg, dlb = _mixers_bwd_a(sv["z"], dyn, sv["o_t"], p)
        do_r, do_t = _head_split(do, 0, "split_do")
        dq_t, dk_t, dv_t = _attn_bwd(sv["q_r"], sv["q_t"], sv["k_t"], sv["v_t"], do_r, do_t, dsum)
        dz = _mixers_bwd_b(sv["z"], dza, dcb, dcd, dq_t, dk_t, dv_t, p)
        dx, g["norm_mix_g"][l] = _in_proj_bwd(sv["x0"], small["norm_mix_g"][l][None], dx1, dz, big["w_in"], l)
        g["w_in"][l] = _in_proj_wgrad(sv["hb"], dz)
        g["gmlp_v_g"][l] = dvg[0]
        g["gmlp_w_s"][l] = jnp.where(tril, dws, 0.0)
        g["gmlp_b_s"][l] = dbm.reshape(CHUNK, N_HEADS, HEAD_DIM).sum(-1).T
        g["short_conv_w"][l] = dscw[:SHORT_K]
        g["conf_conv_w"][l] = dccw[:CONF_K]
        g["conf_ln_g"][l] = dlg[0]
        g["conf_ln_b"][l] = dlb[0]
        g["norm_mix_g"][l] = g["norm_mix_g"][l][0]
        g["mix_out_g"][l] = g["mix_out_g"][l][0]
        g["norm_ffn_g"][l] = g["norm_ffn_g"][l][0]
    grads = {k: jnp.stack(v) for k, v in g.items()}
    grads["final_norm_g"] = d_final[0]
    return loss, dx, grads


_ANY = pl.BlockSpec(memory_space=pl.ANY)


def _mesh_place():
    x, y, c = lax.axis_index("x"), lax.axis_index("y"), lax.axis_index("c")
    chips = [(1 - x, y), (x, 1 - y), (1 - x, 1 - y)]
    return x, y, c, 2 * x + y, chips


def _gather_blocks(blocks, whole):
    n, m = len(blocks), len(whole)

    def body(*refs):
        ins, outs = refs[:n + m], refs[n + m:2 * (n + m)]
        ici_send, ici_recv, d2d_send, d2d_recv, local_sems = refs[2 * (n + m):]
        x, y, c, me, chips = _mesh_place()
        blk = [2 * chip[0] + chip[1] for chip in chips]

        def ici(k, r, src, block, to):
            dst = outs[k].at[block, c] if k < n else outs[k].at[block]
            return pltpu.make_async_remote_copy(
                src_ref=src, dst_ref=dst, send_sem=ici_send.at[3 * k + r], recv_sem=ici_recv.at[3 * k + r],
                device_id=(to[0], to[1], c), device_id_type=MESH)

        def d2d(k, r, layer):
            part = outs[k].at[blk[r], layer]
            return pltpu.make_async_remote_copy(
                src_ref=part, dst_ref=part, send_sem=d2d_send.at[3 * k + r], recv_sem=d2d_recv.at[3 * k + r],
                device_id=(x, y, 1 - c), device_id_type=MESH)

        local = [pltpu.make_async_copy(ins[k], outs[k].at[me], local_sems.at[k]) for k in range(n + m)]
        sent = [ici(k, r, ins[k].at[c] if k < n else ins[k], me, chip) for k in range(n + m) for r, chip in enumerate(chips)]
        for cp in local + sent:
            cp.start()
        passed = []
        for k in range(n + m):
            for r, chip in enumerate(chips):
                ici(k, r, ins[k].at[c] if k < n else ins[k], blk[r], chip).wait_recv()
                if k < n:
                    passed.append(d2d(k, r, c))
                    passed[-1].start()
        for k in range(n):
            for r in range(3):
                d2d(k, r, 1 - c).wait_recv()
        for cp in sent + passed:
            cp.wait_send()
        for cp in local:
            cp.wait()

    arrays = list(blocks) + list(whole)
    return pl.pallas_call(
        body, name="gather_weights",
        in_specs=[_ANY] * (n + m), out_specs=[_ANY] * (n + m),
        out_shape=[jax.ShapeDtypeStruct((N_BLK,) + b.shape, b.dtype) for b in arrays],
        scratch_shapes=[pltpu.SemaphoreType.DMA((3 * (n + m),)), pltpu.SemaphoreType.DMA((3 * (n + m),)),
                        pltpu.SemaphoreType.DMA((3 * n,)), pltpu.SemaphoreType.DMA((3 * n,)),
                        pltpu.SemaphoreType.DMA((n + m,))],
    )(*arrays)


def _swap_halves(gp):
    def body(g_ref, r_ref, send_sem, recv_sem):
        x, y, c, _, _ = _mesh_place()
        cp = pltpu.make_async_remote_copy(
            src_ref=g_ref.at[:, pl.ds((1 - c) * HALF_ROWS, HALF_ROWS), :], dst_ref=r_ref,
            send_sem=send_sem, recv_sem=recv_sem, device_id=(x, y, 1 - c), device_id_type=MESH)
        cp.start()
        cp.wait()

    return pl.pallas_call(
        body, name="swap_halves", in_specs=[_ANY], out_specs=_ANY,
        out_shape=jax.ShapeDtypeStruct((N_BLK, HALF_ROWS, D_MODEL), F32),
        scratch_shapes=[pltpu.SemaphoreType.DMA, pltpu.SemaphoreType.DMA],
    )(gp)


def _send_to_owners(s1):
    def body(s_ref, r_ref, send_sems, recv_sems, local_sem):
        x, y, c, me, chips = _mesh_place()

        def remote(r, src_block, dst_block, to):
            return pltpu.make_async_remote_copy(
                src_ref=s_ref.at[src_block], dst_ref=r_ref.at[dst_block], send_sem=send_sems.at[r],
                recv_sem=recv_sems.at[r], device_id=(to[0], to[1], c), device_id_type=MESH)

        local = pltpu.make_async_copy(s_ref.at[me], r_ref.at[me], local_sem)
        local.start()
        sent = [remote(r, 2 * chip[0] + chip[1], me, chip) for r, chip in enumerate(chips)]
        for cp in sent:
            cp.start()
        for r, chip in enumerate(chips):
            remote(r, me, 2 * chip[0] + chip[1], chip).wait_recv()
        for cp in sent:
            cp.wait_send()
        local.wait()

    return pl.pallas_call(
        body, name="send_to_owners", in_specs=[_ANY], out_specs=_ANY,
        out_shape=jax.ShapeDtypeStruct((N_BLK, HALF_ROWS, D_MODEL), s1.dtype),
        scratch_shapes=[pltpu.SemaphoreType.DMA((3,)), pltpu.SemaphoreType.DMA((3,)), pltpu.SemaphoreType.DMA],
    )(s1)


def _join_halves(f):
    def body(f_ref, o_ref, send_sem, recv_sem, local_sem):
        x, y, c, _, _ = _mesh_place()
        local = pltpu.make_async_copy(f_ref, o_ref.at[c], local_sem)
        local.start()
        cp = pltpu.make_async_remote_copy(src_ref=f_ref, dst_ref=o_ref.at[c], send_sem=send_sem, recv_sem=recv_sem,
                                          device_id=(x, y, 1 - c), device_id_type=MESH)
        cp.start()
        pltpu.make_async_remote_copy(src_ref=f_ref, dst_ref=o_ref.at[1 - c], send_sem=send_sem, recv_sem=recv_sem,
                                     device_id=(x, y, 1 - c), device_id_type=MESH).wait_recv()
        cp.wait_send()
        local.wait()

    return pl.pallas_call(
        body, name="join_halves", in_specs=[_ANY], out_specs=_ANY,
        out_shape=jax.ShapeDtypeStruct((2, HALF_ROWS, D_MODEL), F32),
        scratch_shapes=[pltpu.SemaphoreType.DMA, pltpu.SemaphoreType.DMA, pltpu.SemaphoreType.DMA],
    )(f)


def _add_pairs(a, b):
    n, r, _ = a.shape
    t = 512

    def body(a_ref, b_ref, o_ref):
        o_ref[...] = (a_ref[...] + b_ref[...]).astype(BF16)

    spec = pl.BlockSpec((None, t, D_MODEL), lambda i, j: (i, j, 0))
    return pl.pallas_call(
        body, name="add_pairs", grid=(n, r // t), in_specs=[spec, spec], out_specs=spec,
        out_shape=jax.ShapeDtypeStruct(a.shape, BF16), compiler_params=_params(("parallel", "parallel")),
    )(a, b)


def _add_chips(r2):
    _, r, _ = r2.shape
    t = 512

    def body(r_ref, o_ref):
        o_ref[...] = ((r_ref[0].astype(F32) + r_ref[1].astype(F32)) + r_ref[2].astype(F32)) + r_ref[3].astype(F32)

    return pl.pallas_call(
        body, name="add_chips", grid=(r // t,),
        in_specs=[pl.BlockSpec((N_BLK, t, D_MODEL), lambda i: (0, i, 0))],
        out_specs=pl.BlockSpec((t, D_MODEL), lambda i: (i, 0)),
        out_shape=jax.ShapeDtypeStruct((r, D_MODEL), F32), compiler_params=_params(("parallel",)),
    )(r2)


def _adamw(g, w, m, v):
    r = g.shape[0]
    t = 512
    c1 = 1.0 - ADAM_B1 ** ADAM_STEP
    c2 = 1.0 - ADAM_B2 ** ADAM_STEP

    def body(g_ref, w_ref, m_ref, v_ref, d_ref, mo_ref, vo_ref):
        gv = g_ref[...]
        m_new = ADAM_B1 * m_ref[...] + (1.0 - ADAM_B1) * gv
        v_new = ADAM_B2 * v_ref[...] + (1.0 - ADAM_B2) * (gv * gv)
        mo_ref[...] = m_new
        vo_ref[...] = v_new
        d_ref[...] = -ADAM_LR * ((m_new / c1) / (jnp.sqrt(v_new / c2) + ADAM_EPS) + ADAM_WD * w_ref[...])

    spec = pl.BlockSpec((t, D_MODEL), lambda i: (i, 0))
    shape = jax.ShapeDtypeStruct(g.shape, F32)
    return pl.pallas_call(
        body, name="adamw", grid=(r // t,), in_specs=[spec] * 4, out_specs=[spec] * 3, out_shape=[shape] * 3,
        compiler_params=_params(("parallel",)),
    )(g, w, m, v)


_REPLICATED = ("norm_mix_g", "gmlp_v_g", "gmlp_w_s", "gmlp_b_s", "conf_ln_g", "conf_ln_b", "mix_out_g", "norm_ffn_g",
               "final_norm_g")
_REP_SHAPES = {"norm_mix_g": (DEPTH, D_MODEL), "gmlp_v_g": (DEPTH, D_GROUP), "gmlp_w_s": (DEPTH, N_HEADS, CHUNK, CHUNK),
               "gmlp_b_s": (DEPTH, N_HEADS, CHUNK), "conf_ln_g": (DEPTH, D_GROUP), "conf_ln_b": (DEPTH, D_GROUP),
               "mix_out_g": (DEPTH, D_MODEL), "norm_ffn_g": (DEPTH, D_MODEL), "final_norm_g": (D_MODEL,)}
_BIG_ROWS = (("w_in", 1280), ("w_out", 512), ("w_up", 2048), ("w_down", 2048))
_CONV_ROWS = 8
_REP_ROWS = 144
_CH_BLK = D_GROUP // N_BLK


def _pad_rows(flat, rows):
    pad = rows * D_MODEL - flat.shape[-1]
    flat = jnp.pad(flat, [(0, 0)] * (flat.ndim - 1) + [(0, pad)])
    return flat.reshape(flat.shape[:-1] + (rows, D_MODEL))


def _pack(big, scw, ccw, rep):
    lead = scw.shape[:-3]
    conv = jnp.concatenate([scw.reshape(lead + (-1,)), ccw.reshape(lead + (-1,))], axis=-1)
    flat = jnp.concatenate([rep[k].reshape(-1) for k in _REPLICATED])
    flat = jnp.broadcast_to(flat, lead + flat.shape)
    used = sum(r for _, r in _BIG_ROWS) + _CONV_ROWS + _REP_ROWS
    parts = [big[k] for k, _ in _BIG_ROWS] + [_pad_rows(conv, _CONV_ROWS), _pad_rows(flat, _REP_ROWS),
                                             jnp.zeros(lead + (PACK_ROWS - used, D_MODEL), F32)]
    return jnp.concatenate(parts, axis=-2)


def _unpack(pk):
    out = {}
    row = 0
    shapes = {"w_in": (DEPTH, D_MODEL, W_IN_BLK), "w_out": (DEPTH, D_GROUP, D_MODEL),
              "w_up": (DEPTH, D_MODEL, D_FF_BLK), "w_down": (DEPTH, D_FF_BLK, D_MODEL)}
    for k, r in _BIG_ROWS:
        out[k] = pk[row:row + r].reshape(shapes[k])
        row += r
    conv = pk[row:row + _CONV_ROWS].reshape(-1)
    n_s = DEPTH * SHORT_K * _CH_BLK
    out["short_conv_w"] = conv[:n_s].reshape(DEPTH, SHORT_K, _CH_BLK)
    out["conf_conv_w"] = conv[n_s:n_s + DEPTH * CONF_K * _CH_BLK].reshape(DEPTH, CONF_K, _CH_BLK)
    row += _CONV_ROWS
    flat = pk[row:row + _REP_ROWS].reshape(-1)
    at = 0
    for k in _REPLICATED:
        n = math.prod(_REP_SHAPES[k])
        out[k] = flat[at:at + n].reshape(_REP_SHAPES[k])
        at += n
    return out


def _conv_blocks(w):
    d, k, _ = w.shape
    return w.reshape(d, k, N_BLK, _CH_BLK).transpose(2, 0, 1, 3)


_WEIGHTS = ("norm_mix_g", "w_in", "gmlp_v_g", "gmlp_w_s", "gmlp_b_s", "short_conv_w", "conf_conv_w", "conf_ln_g",
            "conf_ln_b", "mix_out_g", "w_out", "norm_ffn_g", "w_up", "w_down", "final_norm_g")


def kernel(x, norm_mix_g, w_in, gmlp_v_g, gmlp_w_s, gmlp_b_s, short_conv_w, conf_conv_w, conf_ln_g, conf_ln_b, mix_out_g, w_out, norm_ffn_g, w_up, w_down, final_norm_g, loss_target, m_norm_mix_g, m_w_in, m_gmlp_v_g, m_gmlp_w_s, m_gmlp_b_s, m_short_conv_w, m_conf_conv_w, m_conf_ln_g, m_conf_ln_b, m_mix_out_g, m_w_out, m_norm_ffn_g, m_w_up, m_w_down, m_final_norm_g, v_norm_mix_g, v_w_in, v_gmlp_v_g, v_gmlp_w_s, v_gmlp_b_s, v_short_conv_w, v_conf_conv_w, v_conf_ln_g, v_conf_ln_b, v_mix_out_g, v_w_out, v_norm_ffn_g, v_w_up, v_w_down, v_final_norm_g):
    w = dict(norm_mix_g=norm_mix_g, w_in=w_in, gmlp_v_g=gmlp_v_g, gmlp_w_s=gmlp_w_s, gmlp_b_s=gmlp_b_s,
             short_conv_w=short_conv_w, conf_conv_w=conf_conv_w, conf_ln_g=conf_ln_g, conf_ln_b=conf_ln_b,
             mix_out_g=mix_out_g, w_out=w_out, norm_ffn_g=norm_ffn_g, w_up=w_up, w_down=w_down, final_norm_g=final_norm_g)
    m = dict(norm_mix_g=m_norm_mix_g, w_in=m_w_in, gmlp_v_g=m_gmlp_v_g, gmlp_w_s=m_gmlp_w_s, gmlp_b_s=m_gmlp_b_s,
             short_conv_w=m_short_conv_w, conf_conv_w=m_conf_conv_w, conf_ln_g=m_conf_ln_g, conf_ln_b=m_conf_ln_b,
             mix_out_g=m_mix_out_g, w_out=m_w_out, norm_ffn_g=m_norm_ffn_g, w_up=m_w_up, w_down=m_w_down,
             final_norm_g=m_final_norm_g)
    v = dict(norm_mix_g=v_norm_mix_g, w_in=v_w_in, gmlp_v_g=v_gmlp_v_g, gmlp_w_s=v_gmlp_w_s, gmlp_b_s=v_gmlp_b_s,
             short_conv_w=v_short_conv_w, conf_conv_w=v_conf_conv_w, conf_ln_g=v_conf_ln_g, conf_ln_b=v_conf_ln_b,
             mix_out_g=v_mix_out_g, w_out=v_w_out, norm_ffn_g=v_norm_ffn_g, w_up=v_w_up, w_down=v_w_down,
             final_norm_g=v_final_norm_g)
    c = lax.axis_index("c")

    conv_mine = _pad_rows(jnp.concatenate([short_conv_w.reshape(-1), conf_conv_w.reshape(-1)]), _CONV_ROWS)
    gathered = _gather_blocks([_cast_bf16(w[k], "cast_" + k) for k, _ in _BIG_ROWS], [conv_mine])
    big = dict(zip([k for k, _ in _BIG_ROWS], gathered[:4]))
    conv_all = gathered[4].reshape(N_BLK, -1)
    n_s = DEPTH * SHORT_K * _CH_BLK
    scw_all = conv_all[:, :n_s].reshape(N_BLK, DEPTH, SHORT_K, _CH_BLK)
    ccw_all = conv_all[:, n_s:n_s + DEPTH * CONF_K * _CH_BLK].reshape(N_BLK, DEPTH, CONF_K, _CH_BLK)
    small = {k: w[k] for k in _REPLICATED}
    small["short_conv_w"] = scw_all.transpose(1, 2, 0, 3).reshape(DEPTH, SHORT_K, D_GROUP)
    small["conf_conv_w"] = ccw_all.transpose(1, 2, 0, 3).reshape(DEPTH, CONF_K, D_GROUP)

    loss, dx, g = _local_step(x[0], loss_target[0], big, small)

    g_big = {k: jnp.swapaxes(g[k], 0, 1).reshape(N_BLK, r, D_MODEL) for k, r in _BIG_ROWS}
    gp = _pack(g_big, _conv_blocks(g["short_conv_w"]), _conv_blocks(g["conf_conv_w"]), g)
    mine = lax.dynamic_index_in_dim(gp.reshape(N_BLK, 2, HALF_ROWS, D_MODEL), c, axis=1, keepdims=False)
    chip_sum = _add_pairs(mine, _swap_halves(gp))
    total_half = _add_chips(_send_to_owners(chip_sum))
    g_mine = _join_halves(total_half).reshape(PACK_ROWS, D_MODEL)

    def pack_own(t):
        return _pack({k: t[k].reshape(r, D_MODEL) for k, r in _BIG_ROWS}, t["short_conv_w"], t["conf_conv_w"], t)

    delta, m_new, v_new = _adamw(g_mine, pack_own(w), pack_own(m), pack_own(v))

    outs = [lax.psum(loss[0, 0], ("x", "y", "c")), dx[None]]
    for pk in (g_mine, delta, m_new, v_new):
        parts = _unpack(pk)
        outs += [parts[k] for k in _WEIGHTS]
    return tuple(outs)
```

```python
import math

import jax
import jax.numpy as jnp
from jax import lax
from jax.experimental import pallas as pl
from jax.experimental.pallas import tpu as pltpu

F32 = jnp.float32
BF16 = jnp.bfloat16

D_MODEL = 1024
D_GROUP = 256
N_HEADS = 4
HEAD_DIM = 64
CHUNK = 128
D_IN = 2560
N_BLK = 4
W_IN_BLK = D_IN // N_BLK
D_FF_BLK = 1024
DEPTH = 2
EPS = 1e-6
HALO = 32
SHORT_K = 3
CONF_K = 31
ATT_TQ = 256
ATT_TK = 256
ATT_SCALE = 0.125
ATT_DEAD = -104.0
V7X_VMEM_LIMIT = 56 * 1024 * 1024

ADAM_LR, ADAM_B1, ADAM_B2, ADAM_EPS, ADAM_WD, ADAM_STEP = 0.001, 0.9, 0.999, 1e-08, 0.01, 10

PACK_ROWS = 6144
HALF_ROWS = PACK_ROWS // 2
MESH = pl.DeviceIdType.MESH


def _params(sem, vmem=None):
    return pltpu.CompilerParams(dimension_semantics=sem, vmem_limit_bytes=vmem)


def _tile(s, t):
    return min(s, t)


def _rsqrt_mean(v):
    return lax.rsqrt(jnp.mean(v * v, axis=-1, keepdims=True) + EPS)


def _sigmoid(v):
    return 1.0 / (1.0 + jnp.exp(-v))


_GELU_C = math.sqrt(2.0 / math.pi)


def _gelu(v):
    t = jnp.tanh(_GELU_C * (v + 0.044715 * (v * v * v)))
    return v * (0.5 * (1.0 + t))


def _gelu_grad(v):
    t = jnp.tanh(_GELU_C * (v + 0.044715 * (v * v * v)))
    return 0.5 * (1.0 + t) + v * (0.5 * (1.0 - t * t) * _GELU_C * (1.0 + 3.0 * 0.044715 * (v * v)))


def _dot(a, b):
    return jnp.dot(a, b, preferred_element_type=F32)


def _dot_nt(a, b):
    return lax.dot_general(a, b, (((1,), (1,)), ((), ())), preferred_element_type=F32)


def _dot_tn(a, b):
    return lax.dot_general(a, b, (((0,), (0,)), ((), ())), preferred_element_type=F32)


def _cast_into_slot(w, me, name):
    n, r, c = w.shape
    tr = _tile(r, 256)

    def body(me_ref, w_ref, o_ref):
        o_ref[...] = w_ref[...].astype(BF16)

    return pl.pallas_call(
        body, name=name,
        grid_spec=pltpu.PrefetchScalarGridSpec(
            num_scalar_prefetch=1, grid=(n, r // tr),
            in_specs=[pl.BlockSpec((None, tr, c), lambda a, b, me_ref: (a, b, 0))],
            out_specs=pl.BlockSpec((None, None, tr, c), lambda a, b, me_ref: (me_ref[0], a, b, 0))),
        out_shape=jax.ShapeDtypeStruct((N_BLK,) + w.shape, BF16),
        compiler_params=_params(("parallel", "parallel")),
    )(me, w)


def _in_proj(x, g, w, layer):
    s = x.shape[0]
    t = _tile(s, 512)

    def body(x_ref, g_ref, w_ref, z_ref, h_ref):
        xv = x_ref[...]
        h = (xv * _rsqrt_mean(xv) * g_ref[...]).astype(BF16)
        h_ref[...] = h
        for b in range(N_BLK):
            z_ref[:, b * W_IN_BLK:(b + 1) * W_IN_BLK] = _dot(h, w_ref[b])

    return pl.pallas_call(
        body, name="in_proj", grid=(s // t,),
        in_specs=[pl.BlockSpec((t, D_MODEL), lambda i: (i, 0)), _full((1, D_MODEL)),
                  pl.BlockSpec((N_BLK, None, D_MODEL, W_IN_BLK), lambda i: (0, layer, 0, 0))],
        out_specs=[pl.BlockSpec((t, D_IN), lambda i: (i, 0)), pl.BlockSpec((t, D_MODEL), lambda i: (i, 0))],
        out_shape=[jax.ShapeDtypeStruct((s, D_IN), F32), jax.ShapeDtypeStruct((s, D_MODEL), BF16)],
        compiler_params=_params(("parallel",), V7X_VMEM_LIMIT),
    )(x, g, w)


def _head_split(src, col_block, name, scale=1.0):
    s = src.shape[0]
    t = _tile(s, 512)

    def body(x_ref, r_ref, t_ref):
        xv = x_ref[...] * scale
        xt = xv.T
        for h in range(N_HEADS):
            r_ref[h] = xv[:, h * HEAD_DIM:(h + 1) * HEAD_DIM].astype(BF16)
            t_ref[h] = xt[h * HEAD_DIM:(h + 1) * HEAD_DIM, :].astype(BF16)

    return pl.pallas_call(
        body, name=name, grid=(s // t,),
        in_specs=[pl.BlockSpec((t, D_GROUP), lambda i: (i, col_block))],
        out_specs=[pl.BlockSpec((N_HEADS, t, HEAD_DIM), lambda i: (0, i, 0)),
                   pl.BlockSpec((N_HEADS, HEAD_DIM, t), lambda i: (0, 0, i))],
        out_shape=[jax.ShapeDtypeStruct((N_HEADS, s, HEAD_DIM), BF16),
                   jax.ShapeDtypeStruct((N_HEADS, HEAD_DIM, s), BF16)],
        compiler_params=_params(("parallel",)),
    )(src)


def _mix_a_fwd(z_ref, vg, wt_ref, bmat, t):
    zu = z_ref[:, 0:256]
    zv = z_ref[:, 256:512]
    u = _gelu(zu)
    v = _gelu(zv)
    rv = _rsqrt_mean(v)
    vh = v * rv
    vnb = (vh * vg).astype(BF16)
    head = lax.broadcasted_iota(jnp.int32, (CHUNK, D_GROUP), 1) // HEAD_DIM
    fs = []
    for c in range(t // CHUNK):
        vc = vnb[c * CHUNK:(c + 1) * CHUNK, :]
        fc = bmat
        for h in range(N_HEADS):
            fc = fc + jnp.where(head == h, _dot(wt_ref[h], vc), 0.0)
        fs.append(fc)
    f = jnp.concatenate(fs, axis=0) if len(fs) > 1 else fs[0]
    return zu, zv, u, rv, vh, vnb, f


def _mix_b_fwd(z_ref, zh_ref, first, scw_ref, ext_ref, t):
    gb = z_ref[:, 512:768]
    uh = zh_ref[:, 768:1024] * zh_ref[:, 1024:1280]
    ext_ref[0:HALO, :] = jnp.where(first, 0.0, uh)
    ext_ref[HALO:HALO + t, :] = z_ref[:, 768:1024] * z_ref[:, 1024:1280]
    cv = jnp.zeros((t, D_GROUP), F32)
    for k in range(SHORT_K):
        cv = cv + scw_ref[k:k + 1, :] * ext_ref[pl.ds(HALO - (SHORT_K - 1) + k, t), :]
    return gb, cv


def _mix_d_fwd(z_ref, zh_ref, first, ccw_ref, lg, lb, ext_ref, t):
    hh = zh_ref[:, 2048:2304] * _sigmoid(zh_ref[:, 2304:2560])
    ext_ref[0:HALO, :] = jnp.where(first, 0.0, hh)
    ext_ref[HALO:HALO + t, :] = z_ref[:, 2048:2304] * _sigmoid(z_ref[:, 2304:2560])
    cv = jnp.zeros((t, D_GROUP), F32)
    for k in range(CONF_K):
        cv = cv + ccw_ref[k:k + 1, :] * ext_ref[pl.ds(HALO - (CONF_K - 1) + k, t), :]
    xc = cv - jnp.mean(cv, axis=-1, keepdims=True)
    rs = lax.rsqrt(jnp.mean(xc * xc, axis=-1, keepdims=True) + EPS)
    xh = xc * rs
    ln = xh * lg + lb
    return xh, rs, ln, _sigmoid(ln)


def _mix_specs(t, s):
    per = t // HALO
    return [pl.BlockSpec((t, D_IN), lambda i: (i, 0)),
            pl.BlockSpec((HALO, D_IN), lambda i: (jnp.maximum(i * per - 1, 0), 0))]


def _full(shape):
    return pl.BlockSpec(shape, lambda i: (0,) * len(shape))


def _mixers_fwd(z, p):
    s = z.shape[0]
    t = _tile(s, 256)

    def body(z_ref, zh_ref, vg_ref, wt_ref, bm_ref, scw_ref, ccw_ref, lg_ref, lb_ref, y_ref, eb_ref, ed_ref):
        first = pl.program_id(0) == 0
        _, _, u, _, _, _, f = _mix_a_fwd(z_ref, vg_ref[...], wt_ref, bm_ref[...], t)
        ya = u * f
        y_ref[:, 0:256] = ya * _rsqrt_mean(ya)
        gb, cv = _mix_b_fwd(z_ref, zh_ref, first, scw_ref, eb_ref, t)
        yb = gb * cv
        y_ref[:, 256:512] = yb * _rsqrt_mean(yb)
        _, _, ln, sg = _mix_d_fwd(z_ref, zh_ref, first, ccw_ref, lg_ref[...], lb_ref[...], ed_ref, t)
        yd = ln * sg
        y_ref[:, 512:768] = yd * _rsqrt_mean(yd)

    return pl.pallas_call(
        body, name="mixers_fwd", grid=(s // t,),
        in_specs=_mix_specs(t, s) + [_full((1, D_GROUP)), _full((N_HEADS, CHUNK, CHUNK)), _full((CHUNK, D_GROUP)),
                                     _full((8, D_GROUP)), _full((32, D_GROUP)), _full((1, D_GROUP)), _full((1, D_GROUP))],
        out_specs=pl.BlockSpec((t, 768), lambda i: (i, 0)),
        out_shape=jax.ShapeDtypeStruct((s, 768), F32),
        scratch_shapes=[pltpu.VMEM((HALO + t, D_GROUP), F32), pltpu.VMEM((HALO + t, D_GROUP), F32)],
        compiler_params=_params(("parallel",)),
    )(z, z, p["vg"], p["wt"], p["bmat"], p["scw"], p["ccw"], p["lg"], p["lb"])


def _mixers_bwd_a(z, dyn, o_t, p):
    s = z.shape[0]
    t = _tile(s, 256)
    n_chunk = t // CHUNK

    def body(z_ref, zh_ref, dyn_ref, ot_ref, vg_ref, wt_ref, wtt_ref, bm_ref, scw_ref, ccw_ref, lg_ref, lb_ref,
             dza_ref, dcb_ref, dcd_ref, do_ref, ds_ref, dvg_ref, dws_ref, dbm_ref, dscw_ref, dccw_ref, dlg_ref, dlb_ref,
             eb_ref, ed_ref):
        i = pl.program_id(0)
        first = i == 0

        @pl.when(first)
        def _():
            for r in (dvg_ref, dws_ref, dbm_ref, dscw_ref, dccw_ref, dlg_ref, dlb_ref):
                r[...] = jnp.zeros_like(r)

        def rms_bwd(y, dn):
            r = _rsqrt_mean(y)
            yn = y * r
            return r * (dn - yn * jnp.mean(dn * yn, axis=-1, keepdims=True))

        vg = vg_ref[...]
        zu, zv, u, rv, vh, vnb, f = _mix_a_fwd(z_ref, vg, wt_ref, bm_ref[...], t)
        dya = rms_bwd(u * f, dyn_ref[:, 0:256])
        du = dya * f
        df = dya * u
        head = lax.broadcasted_iota(jnp.int32, (CHUNK, D_GROUP), 1) // HEAD_DIM
        dvns = []
        dbm = jnp.zeros((CHUNK, D_GROUP), F32)
        for c in range(n_chunk):
            dfc = df[c * CHUNK:(c + 1) * CHUNK, :]
            vc = vnb[c * CHUNK:(c + 1) * CHUNK, :]
            dbm = dbm + dfc
            dvn = jnp.zeros((CHUNK, D_GROUP), F32)
            for h in range(N_HEADS):
                dfh = jnp.where(head == h, dfc, 0.0).astype(BF16)
                dvn = dvn + _dot(wtt_ref[h], dfh)
                dws_ref[h] += _dot_nt(dfh, vc)
            dvns.append(dvn)
        dbm_ref[...] += dbm
        dvn = jnp.concatenate(dvns, axis=0) if n_chunk > 1 else dvns[0]
        dvg_ref[...] += jnp.sum(dvn * vh, axis=0, keepdims=True)
        dvh = dvn * vg
        dv = rv * (dvh - vh * jnp.mean(dvh * vh, axis=-1, keepdims=True))
        dza_ref[:, 0:256] = (du * _gelu_grad(zu)).astype(BF16)
        dza_ref[:, 256:512] = (dv * _gelu_grad(zv)).astype(BF16)

        gb, cv = _mix_b_fwd(z_ref, zh_ref, first, scw_ref, eb_ref, t)
        dyb = rms_bwd(gb * cv, dyn_ref[:, 256:512])
        dza_ref[:, 512:768] = (dyb * cv).astype(BF16)
        dcb = dyb * gb
        dcb_ref[...] = dcb
        for k in range(SHORT_K):
            dscw_ref[k:k + 1, :] += jnp.sum(dcb * eb_ref[pl.ds(HALO - (SHORT_K - 1) + k, t), :], axis=0, keepdims=True)

        lg = lg_ref[...]
        xh, rs, ln, sg = _mix_d_fwd(z_ref, zh_ref, first, ccw_ref, lg, lb_ref[...], ed_ref, t)
        dyd = rms_bwd(ln * sg, dyn_ref[:, 768:1024])
        dln = dyd * (sg * (1.0 + ln * (1.0 - sg)))
        dlg_ref[...] += jnp.sum(dln * xh, axis=0, keepdims=True)
        dlb_ref[...] += jnp.sum(dln, axis=0, keepdims=True)
        dxh = dln * lg
        dcd = rs * (dxh - jnp.mean(dxh, axis=-1, keepdims=True) - xh * jnp.mean(dxh * xh, axis=-1, keepdims=True))
        dcd_ref[...] = dcd
        for k in range(CONF_K):
            dccw_ref[k:k + 1, :] += jnp.sum(dcd * ed_ref[pl.ds(HALO - (CONF_K - 1) + k, t), :], axis=0, keepdims=True)

        o = ot_ref[...].reshape(D_GROUP, t).T
        do = rms_bwd(o, dyn_ref[:, 512:768])
        do_ref[...] = do
        prod = do.astype(BF16).astype(F32) * o
        for h in range(N_HEADS):
            ds_ref[h] = jnp.sum(prod[:, h * HEAD_DIM:(h + 1) * HEAD_DIM], axis=1, keepdims=True)

    small =[(1, D_GROUP), (N_HEADS, CHUNK, CHUNK), (CHUNK, D_GROUP), (8, D_GROUP), (32, D_GROUP), (1, D_GROUP), (1, D_GROUP)]
    return pl.pallas_call(
        body, name="mixers_bwd_a", grid=(s // t,),
        in_specs=_mix_specs(t, s) + [pl.BlockSpec((t, D_MODEL), lambda i: (i, 0)),
                                     pl.BlockSpec((N_HEADS, HEAD_DIM, t), lambda i: (0, 0, i)),
                                     _full((1, D_GROUP)), _full((N_HEADS, CHUNK, CHUNK)), _full((N_HEADS, CHUNK, CHUNK)),
                                     _full((CHUNK, D_GROUP)), _full((8, D_GROUP)), _full((32, D_GROUP)),
                                     _full((1, D_GROUP)), _full((1, D_GROUP))],
        out_specs=[pl.BlockSpec((t, 768), lambda i: (i, 0)), pl.BlockSpec((t, D_GROUP), lambda i: (i, 0)),
                   pl.BlockSpec((t, D_GROUP), lambda i: (i, 0)), pl.BlockSpec((t, D_GROUP), lambda i: (i, 0)),
                   pl.BlockSpec((N_HEADS, t, 1), lambda i: (0, i, 0))]
                  + [_full(sh) for sh in small],
        out_shape=[jax.ShapeDtypeStruct((s, 768), BF16), jax.ShapeDtypeStruct((s, D_GROUP), F32),
                   jax.ShapeDtypeStruct((s, D_GROUP), F32), jax.ShapeDtypeStruct((s, D_GROUP), F32),
                   jax.ShapeDtypeStruct((N_HEADS, s, 1), F32)]
                  + [jax.ShapeDtypeStruct(sh, F32) for sh in small],
        scratch_shapes=[pltpu.VMEM((HALO + t, D_GROUP), F32), pltpu.VMEM((HALO + t, D_GROUP), F32)],
        compiler_params=_params(("arbitrary",)),
    )(z, z, dyn, o_t, p["vg"], p["wt"], p["wtt"], p["bmat"], p["scw"], p["ccw"], p["lg"], p["lb"])


def _mixers_bwd_b(z, dza, dcb, dcd, dq_t, dk_t, dv_t, p):
    s = z.shape[0]
    t = _tile(s, 256)
    per = t // HALO
    n_halo = s // HALO

    def body(z_ref, dza_ref, dcb_ref, dcbn_ref, dcd_ref, dcdn_ref, dq_ref, dk_ref, dv_ref, scw_ref, ccw_ref,
             dz_ref, eb_ref, ed_ref):
        last = pl.program_id(0) == pl.num_programs(0) - 1
        dz_ref[:, 0:768] = dza_ref[...]
        eb_ref[0:t, :] = dcb_ref[...]
        eb_ref[t:t + HALO, :] = jnp.where(last, 0.0, dcbn_ref[...])
        du = jnp.zeros((t, D_GROUP), F32)
        for k in range(SHORT_K):
            du = du + scw_ref[k:k + 1, :] * eb_ref[pl.ds(SHORT_K - 1 - k, t), :]
        dz_ref[:, 768:1024] = (du * z_ref[:, 1024:1280]).astype(BF16)
        dz_ref[:, 1024:1280] = (du * z_ref[:, 768:1024]).astype(BF16)
        for n, r in enumerate((dq_ref, dk_ref, dv_ref)):
            dz_ref[:, 1280 + 256 * n:1536 + 256 * n] = r[...].reshape(D_GROUP, t).T.astype(BF16)
        ed_ref[0:t, :] = dcd_ref[...]
        ed_ref[t:t + HALO, :] = jnp.where(last, 0.0, dcdn_ref[...])
        dh = jnp.zeros((t, D_GROUP), F32)
        for k in range(CONF_K):
            dh = dh + ccw_ref[k:k + 1, :] * ed_ref[pl.ds(CONF_K - 1 - k, t), :]
        a = z_ref[:, 2048:2304]
        sg = _sigmoid(z_ref[:, 2304:2560])
        dz_ref[:, 2048:2304] = (dh * sg).astype(BF16)
        dz_ref[:, 2304:2560] = (dh * a * sg * (1.0 - sg)).astype(BF16)

    nxt = lambda i: (jnp.minimum((i + 1) * per, n_halo - 1), 0)
    tr = pl.BlockSpec((N_HEADS, HEAD_DIM, t), lambda i: (0, 0, i))
    return pl.pallas_call(
        body, name="mixers_bwd_b", grid=(s // t,),
        in_specs=[pl.BlockSpec((t, D_IN), lambda i: (i, 0)), pl.BlockSpec((t, 768), lambda i: (i, 0)),
                  pl.BlockSpec((t, D_GROUP), lambda i: (i, 0)), pl.BlockSpec((HALO, D_GROUP), nxt),
                  pl.BlockSpec((t, D_GROUP), lambda i: (i, 0)), pl.BlockSpec((HALO, D_GROUP), nxt),
                  tr, tr, tr, _full((8, D_GROUP)), _full((32, D_GROUP))],
        out_specs=pl.BlockSpec((t, D_IN), lambda i: (i, 0)),
        out_shape=jax.ShapeDtypeStruct((s, D_IN), BF16),
        scratch_shapes=[pltpu.VMEM((HALO + t, D_GROUP), F32), pltpu.VMEM((HALO + t, D_GROUP), F32)],
        compiler_params=_params(("parallel",)),
    )(z, dza, dcb, dcb, dcd, dcd, dq_t, dk_t, dv_t, p["scw"], p["ccw"])


def _split_bf16(v):
    hi = v.astype(BF16)
    return hi, (v - hi.astype(F32)).astype(BF16)


def _att_scores(qs, kts, carries, tri, mask):
    zs = [_dot(q, kt) for q, kt in zip(qs, kts)]
    lms, lbs, parts = [], [], []
    for z in zs:
        soft = jnp.log(1.0 + jnp.exp(-jnp.abs(z)))
        lm = -(jnp.maximum(z, 0.0) + soft)
        lbs.append(lm + z)
        if mask is not None:
            lm = jnp.where(mask, lm, 0.0)
        lms.append(lm)
        parts.append(_split_bf16(lm))
    rights = [_dot(hi, tri) + _dot(lo, tri) for hi, lo in parts]
    ws = []
    for lb, right, carry in zip(lbs, rights, carries):
        w = jnp.exp(lb + right + carry)
        ws.append(w if mask is None else jnp.where(mask, w, 0.0))
    return ws, lbs, [jnp.sum(lm, axis=1, keepdims=True) for lm in lms]


def _att_consts(i):
    j_hi = ((i + 1) * ATT_TQ - 1) // ATT_TK
    row = lax.broadcasted_iota(jnp.int32, (ATT_TQ, ATT_TK), 0) + i * ATT_TQ
    col = lax.broadcasted_iota(jnp.int32, (ATT_TQ, ATT_TK), 1) + j_hi * ATT_TK
    r_i = lax.broadcasted_iota(jnp.int32, (ATT_TK, ATT_TK), 0)
    c_i = lax.broadcasted_iota(jnp.int32, (ATT_TK, ATT_TK), 1)
    return j_hi, col < row, r_i, c_i


def _att_alive(j, carries):
    top = carries[0]
    for c in carries[1:]:
        top = jnp.maximum(top, c)
    return jnp.logical_and(j >= 0, jnp.max(top) > ATT_DEAD)


def _attn_fwd(q_r, k_t, v_t):
    s = q_r.shape[1]

    def body(q_ref, kt_ref, vt_ref, o_ref):
        j_hi, mask, r_i, c_i = _att_consts(pl.program_id(0))
        tri = (r_i > c_i).astype(BF16)

        heads = range(N_HEADS)

        def tiles(j, carries, accs, mask):
            cols = pl.ds(pl.multiple_of(j * ATT_TK, ATT_TK), ATT_TK)
            ws, _, tots = _att_scores([q_ref[h] for h in heads], [kt_ref[h, :, cols] for h in heads], carries, tri, mask)
            accs = [acc + _dot_nt(vt_ref[h, :, cols], w.astype(BF16)) for h, acc, w in zip(heads, accs, ws)]
            return [c + t for c, t in zip(carries, tots)], accs

        state = tiles(j_hi, [jnp.zeros((ATT_TQ, 1), F32)] * N_HEADS, [jnp.zeros((HEAD_DIM, ATT_TQ), F32)] * N_HEADS, mask)

        def cond(c):
            return _att_alive(c[0], c[1])

        def step(c):
            return (c[0] - 1,) + tuple(tiles(c[0], c[1], c[2], None))

        _, _, accs = lax.while_loop(cond, step, (j_hi - 1,) + tuple(state))
        for h in heads:
            o_ref[h] = accs[h]

    whole = pl.BlockSpec((N_HEADS, HEAD_DIM, s), lambda i: (0, 0, 0), pipeline_mode=pl.Buffered(1))
    return pl.pallas_call(
        body, name="attn_fwd", grid=(s // ATT_TQ,),
        in_specs=[pl.BlockSpec((N_HEADS, ATT_TQ, HEAD_DIM), lambda i: (0, i, 0)), whole, whole],
        out_specs=pl.BlockSpec((N_HEADS, HEAD_DIM, ATT_TQ), lambda i: (0, 0, i)),
        out_shape=jax.ShapeDtypeStruct((N_HEADS, HEAD_DIM, s), F32),
        compiler_params=_params(("arbitrary",), V7X_VMEM_LIMIT),
    )(q_r, k_t, v_t)


ATT_BWD_HEADS = 2


def _attn_bwd(q_r, q_t, k_t, v_t, do_r, do_t, dsum):
    s = q_r.shape[1]
    hps = ATT_BWD_HEADS

    def body(q_ref, qt_ref, kt_ref, vt_ref, do_ref, dot_ref, ds_ref, dq_ref, dk_ref, dv_ref):
        i = pl.program_id(1)

        @pl.when(i == 0)
        def _():
            dk_ref[...] = jnp.zeros_like(dk_ref)
            dv_ref[...] = jnp.zeros_like(dv_ref)

        j_hi, mask, r_i, c_i = _att_consts(i)
        tri_r = (r_i > c_i).astype(BF16)
        tri_ge = (r_i >= c_i).astype(BF16)

        heads = range(hps)

        def tiles(j, carries, gsums, accs, mask):
            cols = pl.ds(pl.multiple_of(j * ATT_TK, ATT_TK), ATT_TK)
            kts = [kt_ref[h, :, cols] for h in heads]
            das = [_dot(do_ref[h], vt_ref[h, :, cols]) for h in heads]
            ws, lbs, tots = _att_scores([q_ref[h] for h in heads], kts, carries, tri_r, mask)
            wbs = [w.astype(BF16) for w in ws]
            gs = [wb.astype(F32) * da for wb, da in zip(wbs, das)]
            parts = [_split_bf16(g) for g in gs]
            sfx = [_dot(hi, tri_ge) + _dot(lo, tri_ge) for hi, lo in parts]
            for h in heads:
                dv_ref[h, :, cols] += _dot(dot_ref[h], wbs[h])
            dzs = []
            for h in heads:
                left = ds_ref[h] - gsums[h] - sfx[h]
                dz = gs[h] - jnp.exp(lbs[h]) * (gs[h] + left)
                dzs.append((dz if mask is None else jnp.where(mask, dz, 0.0)).astype(BF16))
            for h in heads:
                dk_ref[h, :, cols] += _dot(qt_ref[h], dzs[h])
            accs = [accs[h] + _dot_nt(kts[h], dzs[h]) for h in heads]
            gsums = [gsums[h] + jnp.sum(gs[h], axis=1, keepdims=True) for h in heads]
            return [c + t for c, t in zip(carries, tots)], gsums, accs

        col0 = [jnp.zeros((ATT_TQ, 1), F32)] * hps
        state = tiles(j_hi, col0, col0, [jnp.zeros((HEAD_DIM, ATT_TQ), F32)] * hps, mask)

        def cond(c):
            return _att_alive(c[0], c[1])

        def step(c):
            return (c[0] - 1,) + tuple(tiles(c[0], c[1], c[2], c[3], None))

        _, _, _, accs = lax.while_loop(cond, step, (j_hi - 1,) + tuple(state))
        for h in heads:
            dq_ref[h] = accs[h] * ATT_SCALE

    whole = pl.BlockSpec((hps, HEAD_DIM, s), lambda g, i: (g, 0, 0))
    whole_in = pl.BlockSpec((hps, HEAD_DIM, s), lambda g, i: (g, 0, 0), pipeline_mode=pl.Buffered(1))
    rows = pl.BlockSpec((hps, ATT_TQ, HEAD_DIM), lambda g, i: (g, i, 0))
    cols = pl.BlockSpec((hps, HEAD_DIM, ATT_TQ), lambda g, i: (g, 0, i))
    shape = jax.ShapeDtypeStruct((N_HEADS, HEAD_DIM, s), F32)
    return pl.pallas_call(
        body, name="attn_bwd", grid=(N_HEADS // hps, s // ATT_TQ),
        in_specs=[rows, cols, whole_in, whole_in, rows, cols, pl.BlockSpec((hps, ATT_TQ, 1), lambda g, i: (g, i, 0))],
        out_specs=[cols, whole, whole],
        out_shape=[shape, shape, shape],
        compiler_params=_params(("parallel", "arbitrary"), V7X_VMEM_LIMIT),
    )(q_r, q_t, k_t, v_t, do_r, do_t, dsum)


def _out_proj(x, y_abd, o_t, gain, w, layer):
    s = x.shape[0]
    t = _tile(s, 512)

    def body(x_ref, y_ref, ot_ref, g_ref, w_ref, x1_ref, yn_ref):
        o = ot_ref[...].reshape(D_GROUP, t).T
        yn_ref[:, 0:512] = y_ref[:, 0:512]
        yn_ref[:, 512:768] = o * _rsqrt_mean(o)
        yn_ref[:, 768:1024] = y_ref[:, 512:768]
        yg = (yn_ref[...] * g_ref[...]).astype(BF16)
        acc = _dot(yg[:, 0:256], w_ref[0])
        for b in range(1, N_BLK):
            acc = acc + _dot(yg[:, 256 * b:256 * (b + 1)], w_ref[b])
        x1_ref[...] = x_ref[...] + acc

    return pl.pallas_call(
        body, name="out_proj", grid=(s // t,),
        in_specs=[pl.BlockSpec((t, D_MODEL), lambda i: (i, 0)), pl.BlockSpec((t, 768), lambda i: (i, 0)),
                  pl.BlockSpec((N_HEADS, HEAD_DIM, t), lambda i: (0, 0, i)), _full((1, D_MODEL)),
                  pl.BlockSpec((N_BLK, None, D_GROUP, D_MODEL), lambda i: (0, layer, 0, 0))],
        out_specs=[pl.BlockSpec((t, D_MODEL), lambda i: (i, 0)), pl.BlockSpec((t, D_MODEL), lambda i: (i, 0))],
        out_shape=[jax.ShapeDtypeStruct((s, D_MODEL), F32), jax.ShapeDtypeStruct((s, D_MODEL), F32)],
        compiler_params=_params(("parallel",)),
    )(x, y_abd, o_t, gain, w)


def _out_proj_bwd(x1, g_ffn, dh, dx2, yn, gain, w, layer):
    s = dx2.shape[0]
    t = _tile(s, 512)

    def body(x_ref, gf_ref, dh_ref, dx2_ref, yn_ref, g_ref, w_ref, dx1_ref, dgf_ref, dyn_ref, dg_ref, dw_ref):
        @pl.when(pl.program_id(0) == 0)
        def _():
            dg_ref[...] = jnp.zeros_like(dg_ref)
            dw_ref[...] = jnp.zeros_like(dw_ref)
            dgf_ref[...] = jnp.zeros_like(dgf_ref)

        dx1, dgf = _rms_bwd_rows(x_ref[...], gf_ref[...], dh_ref[...], dx2_ref[...])
        dx1_ref[...] = dx1
        dgf_ref[...] += dgf
        dxb = dx1.astype(BF16)
        g = g_ref[...]
        yn = yn_ref[...]
        yg = (yn * g).astype(BF16)
        for b in range(N_BLK):
            cols = slice(256 * b, 256 * (b + 1))
            dyg = _dot_nt(dxb, w_ref[b])
            dw_ref[b] += _dot_tn(yg[:, cols], dxb)
            dg_ref[:, cols] += jnp.sum(dyg * yn[:, cols], axis=0, keepdims=True)
            dyn_ref[:, cols] = dyg * g[:, cols]

    row = pl.BlockSpec((t, D_MODEL), lambda i: (i, 0))
    vec = _full((1, D_MODEL))
    return pl.pallas_call(
        body, name="out_proj_bwd", grid=(s // t,),
        in_specs=[row, vec, row, row, row, vec, pl.BlockSpec((N_BLK, None, D_GROUP, D_MODEL), lambda i: (0, layer, 0, 0))],
        out_specs=[row, vec, row, vec, _full((N_BLK, D_GROUP, D_MODEL))],
        out_shape=[jax.ShapeDtypeStruct((s, D_MODEL), F32), jax.ShapeDtypeStruct((1, D_MODEL), F32),
                   jax.ShapeDtypeStruct((s, D_MODEL), F32), jax.ShapeDtypeStruct((1, D_MODEL), F32),
                   jax.ShapeDtypeStruct((N_BLK, D_GROUP, D_MODEL), F32)],
        compiler_params=_params(("arbitrary",), V7X_VMEM_LIMIT),
    )(x1, g_ffn, dh, dx2, yn, gain, w)


def _ffn(x, g, w_up, w_down, layer):
    s = x.shape[0]
    t = _tile(s, 1024)

    def body(x_ref, g_ref, wu_ref, wd_ref, x2_ref, p_ref, h_ref):
        @pl.when(pl.program_id(1) == 0)
        def _():
            xv = x_ref[...]
            h_ref[...] = (xv * _rsqrt_mean(xv) * g_ref[...]).astype(BF16)
            x2_ref[...] = xv

        pre = _dot(h_ref[...], wu_ref[...])
        p_ref[...] = pre.astype(BF16)
        a = jnp.maximum(pre, 0.0)
        x2_ref[...] += _dot((a * a).astype(BF16), wd_ref[...])

    wspec = pl.BlockSpec((None, None, D_MODEL, D_FF_BLK), lambda i, j: (j, layer, 0, 0))
    row = pl.BlockSpec((t, D_MODEL), lambda i, j: (i, 0))
    return pl.pallas_call(
        body, name="ffn", grid=(s // t, N_BLK),
        in_specs=[row, pl.BlockSpec((1, D_MODEL), lambda i, j: (0, 0)), wspec, wspec],
        out_specs=[row, pl.BlockSpec((t, D_FF_BLK), lambda i, j: (i, j)), row],
        out_shape=[jax.ShapeDtypeStruct((s, D_MODEL), F32), jax.ShapeDtypeStruct((s, N_BLK * D_FF_BLK), BF16),
                   jax.ShapeDtypeStruct((s, D_MODEL), BF16)],
        compiler_params=_params(("parallel", "arbitrary"), V7X_VMEM_LIMIT),
    )(x, g, w_up, w_down)


def _rms_bwd_rows(xv, g, dh, dres):
    r = _rsqrt_mean(xv)
    xh = xv * r
    dxh = dh * g
    dx = dres + r * (dxh - xh * jnp.mean(dxh * xh, axis=-1, keepdims=True))
    return dx, jnp.sum(dh * xh, axis=0, keepdims=True)


def _ffn_bwd(dxb, p, w_up, w_down, layer):
    s = dxb.shape[0]
    t = _tile(s, 1024)

    def body(dx_ref, p_ref, wu_ref, wd_ref, dp_ref, dh_ref):
        da = _dot_nt(dx_ref[...], wd_ref[...])
        a = jnp.maximum(p_ref[...].astype(F32), 0.0)
        dp = (da * (2.0 * a)).astype(BF16)
        dp_ref[...] = dp
        dh = _dot_nt(dp, wu_ref[...])

        @pl.when(pl.program_id(1) == 0)
        def _():
            dh_ref[...] = dh

        @pl.when(pl.program_id(1) != 0)
        def _():
            dh_ref[...] += dh

    wspec = pl.BlockSpec((None, None, D_MODEL, D_FF_BLK), lambda i, j: (j, layer, 0, 0))
    row = pl.BlockSpec((t, D_MODEL), lambda i, j: (i, 0))
    blk = pl.BlockSpec((t, D_FF_BLK), lambda i, j: (i, j))
    return pl.pallas_call(
        body, name="ffn_bwd", grid=(s // t, N_BLK),
        in_specs=[row, blk, wspec, wspec], out_specs=[blk, row],
        out_shape=[jax.ShapeDtypeStruct((s, N_BLK * D_FF_BLK), BF16), jax.ShapeDtypeStruct((s, D_MODEL), F32)],
        compiler_params=_params(("parallel", "arbitrary"), V7X_VMEM_LIMIT),
    )(dxb, p, w_up, w_down)


def _ffn_wgrad(hb, p, dp, dxb):
    s = hb.shape[0]
    t = _tile(s, 1024)

    def body(h_ref, p_ref, dp_ref, dx_ref, du_ref, dd_ref):
        @pl.when(pl.program_id(1) == 0)
        def _():
            du_ref[...] = jnp.zeros_like(du_ref)
            dd_ref[...] = jnp.zeros_like(dd_ref)

        a = jnp.maximum(p_ref[...].astype(F32), 0.0)
        du_ref[...] += _dot_tn(h_ref[...], dp_ref[...])
        dd_ref[...] += _dot_tn((a * a).astype(BF16), dx_ref[...])

    row = pl.BlockSpec((t, D_MODEL), lambda j, i: (i, 0))
    blk = pl.BlockSpec((t, D_FF_BLK), lambda j, i: (i, j))
    out = pl.BlockSpec((None, D_MODEL, D_FF_BLK), lambda j, i: (j, 0, 0))
    shape = jax.ShapeDtypeStruct((N_BLK, D_MODEL, D_FF_BLK), F32)
    return pl.pallas_call(
        body, name="ffn_wgrad", grid=(N_BLK, s // t),
        in_specs=[row, blk, blk, row], out_specs=[out, out], out_shape=[shape, shape],
        compiler_params=_params(("parallel", "arbitrary"), V7X_VMEM_LIMIT),
    )(hb, p, dp, dxb)


def _in_proj_bwd(x, g, dx1, dz, w, layer):
    s = x.shape[0]
    t = _tile(s, 512)

    def body(x_ref, g_ref, dx1_ref, dz_ref, w_ref, dx0_ref, dxb_ref, dg_ref):
        @pl.when(pl.program_id(0) == 0)
        def _():
            dg_ref[...] = jnp.zeros_like(dg_ref)

        dh = _dot_nt(dz_ref[:, 0:W_IN_BLK], w_ref[0])
        for b in range(1, N_BLK):
            dh = dh + _dot_nt(dz_ref[:, b * W_IN_BLK:(b + 1) * W_IN_BLK], w_ref[b])
        dx, dg = _rms_bwd_rows(x_ref[...], g_ref[...], dh, dx1_ref[...])
        dx0_ref[...] = dx
        dxb_ref[...] = dx.astype(BF16)
        dg_ref[...] += dg

    row = pl.BlockSpec((t, D_MODEL), lambda i: (i, 0))
    return pl.pallas_call(
        body, name="in_proj_bwd", grid=(s // t,),
        in_specs=[row, _full((1, D_MODEL)), row, pl.BlockSpec((t, D_IN), lambda i: (i, 0)),
                  pl.BlockSpec((N_BLK, None, D_MODEL, W_IN_BLK), lambda i: (0, layer, 0, 0))],
        out_specs=[row, row, _full((1, D_MODEL))],
        out_shape=[jax.ShapeDtypeStruct((s, D_MODEL), F32), jax.ShapeDtypeStruct((s, D_MODEL), BF16),
                   jax.ShapeDtypeStruct((1, D_MODEL), F32)],
        compiler_params=_params(("arbitrary",), V7X_VMEM_LIMIT),
    )(x, g, dx1, dz, w)


def _in_proj_wgrad(hb, dz):
    s = hb.shape[0]
    t = _tile(s, 512)

    def body(h_ref, dz_ref, dw_ref):
        @pl.when(pl.program_id(0) == 0)
        def _():
            dw_ref[...] = jnp.zeros_like(dw_ref)

        h = h_ref[...]
        for b in range(N_BLK):
            dw_ref[b] += _dot_tn(h, dz_ref[:, b * W_IN_BLK:(b + 1) * W_IN_BLK])

    return pl.pallas_call(
        body, name="in_proj_wgrad", grid=(s // t,),
        in_specs=[pl.BlockSpec((t, D_MODEL), lambda i: (i, 0)), pl.BlockSpec((t, D_IN), lambda i: (i, 0))],
        out_specs=_full((N_BLK, D_MODEL, W_IN_BLK)),
        out_shape=jax.ShapeDtypeStruct((N_BLK, D_MODEL, W_IN_BLK), F32),
        compiler_params=_params(("arbitrary",), V7X_VMEM_LIMIT),
    )(hb, dz)


def _loss_head(x, g, target):
    s = x.shape[0]
    t = _tile(s, 512)

    def body(x_ref, g_ref, t_ref, l_ref, dx_ref, dxb_ref, dg_ref):
        @pl.when(pl.program_id(0) == 0)
        def _():
            l_ref[...] = jnp.zeros_like(l_ref)
            dg_ref[...] = jnp.zeros_like(dg_ref)

        xv = x_ref[...]
        g = g_ref[...]
        r = _rsqrt_mean(xv)
        xh = xv * r
        err = xh * g - t_ref[...]
        l_ref[...] += 0.5 * jnp.sum(jnp.mean(err * err, axis=-1, keepdims=True), axis=0, keepdims=True)
        dy = err * (1.0 / D_MODEL)
        dg_ref[...] += jnp.sum(dy * xh, axis=0, keepdims=True)
        dxh = dy * g
        dx = r * (dxh - xh * jnp.mean(dxh * xh, axis=-1, keepdims=True))
        dx_ref[...] = dx
        dxb_ref[...] = dx.astype(BF16)

    row = pl.BlockSpec((t, D_MODEL), lambda i: (i, 0))
    return pl.pallas_call(
        body, name="loss_head", grid=(s // t,),
        in_specs=[row, _full((1, D_MODEL)), row],
        out_specs=[_full((1, 128)), row, row, _full((1, D_MODEL))],
        out_shape=[jax.ShapeDtypeStruct((1, 128), F32), jax.ShapeDtypeStruct((s, D_MODEL), F32),
                   jax.ShapeDtypeStruct((s, D_MODEL), BF16), jax.ShapeDtypeStruct((1, D_MODEL), F32)],
        compiler_params=_params(("arbitrary",)),
    )(x, g, target)


def _layer_params(small, layer):
    tril = jnp.tril(jnp.ones((CHUNK, CHUNK), bool))
    ws = jnp.where(tril, small["gmlp_w_s"][layer], 0.0)
    bmat = jnp.repeat(small["gmlp_b_s"][layer].T, HEAD_DIM, axis=1)
    scw = jnp.zeros((8, D_GROUP), F32).at[:SHORT_K].set(small["short_conv_w"][layer])
    ccw = jnp.zeros((32, D_GROUP), F32).at[:CONF_K].set(small["conf_conv_w"][layer])
    return dict(vg=small["gmlp_v_g"][layer][None], wt=ws.astype(BF16), wtt=jnp.swapaxes(ws, 1, 2).astype(BF16),
                bmat=bmat, scw=scw, ccw=ccw, lg=small["conf_ln_g"][layer][None], lb=small["conf_ln_b"][layer][None])


def _local_step(x, target, big, small):
    saved = []
    for l in range(DEPTH):
        p = _layer_params(small, l)
        z, hb = _in_proj(x, small["norm_mix_g"][l][None], big["w_in"], l)
        q_r, q_t = _head_split(z, 5, "split_q", ATT_SCALE)
        _, k_t = _head_split(z, 6, "split_k")
        _, v_t = _head_split(z, 7, "split_v")
        y_abd = _mixers_fwd(z, p)
        o_t = _attn_fwd(q_r, k_t, v_t)
        x1, yn = _out_proj(x, y_abd, o_t, small["mix_out_g"][l][None], big["w_out"], l)
        x2, pre, h2b = _ffn(x1, small["norm_ffn_g"][l][None], big["w_up"], big["w_down"], l)
        saved.append(dict(p=p, x0=x, z=z, hb=hb, q_r=q_r, q_t=q_t, k_t=k_t, v_t=v_t, o_t=o_t, x1=x1, yn=yn, pre=pre,
                          h2b=h2b))
        x = x2

    loss, dx, dxb, d_final = _loss_head(x, small["final_norm_g"][None], target)

    g = {k: [None] * DEPTH for k in ("w_in", "w_out", "w_up", "w_down", "norm_mix_g", "gmlp_v_g", "gmlp_w_s", "gmlp_b_s",
                                     "short_conv_w", "conf_conv_w", "conf_ln_g", "conf_ln_b", "mix_out_g", "norm_ffn_g")}
    tril = jnp.tril(jnp.ones((CHUNK, CHUNK), bool))
    for l in reversed(range(DEPTH)):
        sv = saved[l]
        p = sv["p"]
        dpre, dh = _ffn_bwd(dxb, sv["pre"], big["w_up"], big["w_down"], l)
        g["w_up"][l], g["w_down"][l] = _ffn_wgrad(sv["h2b"], sv["pre"], dpre, dxb)
        dx1, g["norm_ffn_g"][l], dyn, g["mix_out_g"][l], g["w_out"][l] = _out_proj_bwd(
            sv["x1"], small["norm_ffn_g"][l][None], dh, dx, sv["yn"], small["mix_out_g"][l][None], big["w_out"], l)
        dza, dcb, dcd, do, dsum, dvg, dws, dbm, dscw, dccw, dlg, dlb = _mixers_bwd_a(sv["z"], dyn, sv["o_t"], p)
        do_r, do_t = _head_split(do, 0, "split_do")
        dq_t, dk_t, dv_t = _attn_bwd(sv["q_r"], sv["q_t"], sv["k_t"], sv["v_t"], do_r, do_t, dsum)
        dz = _mixers_bwd_b(sv["z"], dza, dcb, dcd, dq_t, dk_t, dv_t, p)
        dx, dxb, g["norm_mix_g"][l] = _in_proj_bwd(sv["x0"], small["norm_mix_g"][l][None], dx1, dz, big["w_in"], l)
        g["w_in"][l] = _in_proj_wgrad(sv["hb"], dz)
        g["gmlp_v_g"][l] = dvg[0]
        g["gmlp_w_s"][l] = jnp.where(tril, dws, 0.0)
        g["gmlp_b_s"][l] = dbm.reshape(CHUNK, N_HEADS, HEAD_DIM).sum(-1).T
        g["short_conv_w"][l] = dscw[:SHORT_K]
        g["conf_conv_w"][l] = dccw[:CONF_K]
        g["conf_ln_g"][l] = dlg[0]
        g["conf_ln_b"][l] = dlb[0]
        g["norm_mix_g"][l] = g["norm_mix_g"][l][0]
        g["mix_out_g"][l] = g["mix_out_g"][l][0]
        g["norm_ffn_g"][l] = g["norm_ffn_g"][l][0]
    grads = {k: jnp.stack(v, axis=1 if k in ("w_in", "w_out", "w_up", "w_down") else 0) for k, v in g.items()}
    grads["final_norm_g"] = d_final[0]
    return loss, dx, grads


_ANY = pl.BlockSpec(memory_space=pl.ANY)


def _mesh_place():
    x, y, c = lax.axis_index("x"), lax.axis_index("y"), lax.axis_index("c")
    chips = [(1 - x, y), (x, 1 - y), (1 - x, 1 - y)]
    return x, y, c, 2 * x + y, chips


def _gather_blocks(blocks, whole):
    n, m = len(blocks), len(whole)

    def body(*refs):
        ins, outs = refs[:n + m], refs[n + m:2 * (n + m)]
        ici_send, ici_recv, d2d_send, d2d_recv, local_sems = refs[2 * (n + m):]
        x, y, c, me, chips = _mesh_place()
        blk = [2 * chip[0] + chip[1] for chip in chips]

        def ici(k, r, block, to):
            src = outs[k].at[me, c] if k < n else ins[k]
            dst = outs[k].at[block, c] if k < n else outs[k].at[block]
            return pltpu.make_async_remote_copy(
                src_ref=src, dst_ref=dst, send_sem=ici_send.at[3 * k + r], recv_sem=ici_recv.at[3 * k + r],
                device_id=(to[0], to[1], c), device_id_type=MESH)

        def d2d(k, r, layer):
            part = outs[k].at[blk[r], layer]
            return pltpu.make_async_remote_copy(
                src_ref=part, dst_ref=part, send_sem=d2d_send.at[3 * k + r], recv_sem=d2d_recv.at[3 * k + r],
                device_id=(x, y, 1 - c), device_id_type=MESH)

        local = [pltpu.make_async_copy(ins[k], outs[k].at[me], local_sems.at[k - n]) for k in range(n, n + m)]
        sent = [ici(k, r, me, chip) for k in range(n + m) for r, chip in enumerate(chips)]
        for cp in local + sent:
            cp.start()
        passed = []
        for k in range(n + m):
            for r, chip in enumerate(chips):
                ici(k, r, blk[r], chip).wait_recv()
                if k < n:
                    passed.append(d2d(k, r, c))
                    passed[-1].start()
        for k in range(n):
            for r in range(3):
                d2d(k, r, 1 - c).wait_recv()
        for cp in sent + passed:
            cp.wait_send()
        for cp in local:
            cp.wait()

    arrays = list(blocks) + list(whole)
    return pl.pallas_call(
        body, name="gather_weights",
        in_specs=[_ANY] * (n + m), out_specs=[_ANY] * (n + m),
        out_shape=[jax.ShapeDtypeStruct(b.shape, b.dtype) for b in blocks]
                  + [jax.ShapeDtypeStruct((N_BLK,) + b.shape, b.dtype) for b in whole],
        input_output_aliases={k: k for k in range(n)},
        scratch_shapes=[pltpu.SemaphoreType.DMA((3 * (n + m),)), pltpu.SemaphoreType.DMA((3 * (n + m),)),
                        pltpu.SemaphoreType.DMA((3 * n,)), pltpu.SemaphoreType.DMA((3 * n,)),
                        pltpu.SemaphoreType.DMA((m,))],
    )(*arrays)


def _swap_halves(gp):
    def body(g_ref, r_ref, send_sem, recv_sem):
        x, y, c, _, _ = _mesh_place()
        cp = pltpu.make_async_remote_copy(
            src_ref=g_ref.at[:, pl.ds((1 - c) * HALF_ROWS, HALF_ROWS), :], dst_ref=r_ref,
            send_sem=send_sem, recv_sem=recv_sem, device_id=(x, y, 1 - c), device_id_type=MESH)
        cp.start()
        cp.wait()

    return pl.pallas_call(
        body, name="swap_halves", in_specs=[_ANY], out_specs=_ANY,
        out_shape=jax.ShapeDtypeStruct((N_BLK, HALF_ROWS, D_MODEL), F32),
        scratch_shapes=[pltpu.SemaphoreType.DMA, pltpu.SemaphoreType.DMA],
    )(gp)


def _send_to_owners(s1):
    def body(s_ref, r_ref, send_sems, recv_sems):
        x, y, c, me, chips = _mesh_place()

        def remote(r, src_block, dst_block, to):
            return pltpu.make_async_remote_copy(
                src_ref=s_ref.at[src_block], dst_ref=r_ref.at[dst_block], send_sem=send_sems.at[r],
                recv_sem=recv_sems.at[r], device_id=(to[0], to[1], c), device_id_type=MESH)

        sent = [remote(r, 2 * chip[0] + chip[1], me, chip) for r, chip in enumerate(chips)]
        for cp in sent:
            cp.start()
        for r, chip in enumerate(chips):
            remote(r, me, 2 * chip[0] + chip[1], chip).wait_recv()
        for cp in sent:
            cp.wait_send()

    return pl.pallas_call(
        body, name="send_to_owners", in_specs=[_ANY], out_specs=_ANY,
        out_shape=jax.ShapeDtypeStruct((N_BLK, HALF_ROWS, D_MODEL), s1.dtype),
        scratch_shapes=[pltpu.SemaphoreType.DMA((3,)), pltpu.SemaphoreType.DMA((3,))],
    )(s1)


def _swap_reduced(f):
    def body(f_ref, o_ref, send_sem, recv_sem):
        x, y, c, _, _ = _mesh_place()
        cp = pltpu.make_async_remote_copy(src_ref=f_ref, dst_ref=o_ref, send_sem=send_sem, recv_sem=recv_sem,
                                          device_id=(x, y, 1 - c), device_id_type=MESH)
        cp.start()
        cp.wait()

    return pl.pallas_call(
        body, name="swap_reduced", in_specs=[_ANY], out_specs=_ANY,
        out_shape=jax.ShapeDtypeStruct((HALF_ROWS, D_MODEL), F32),
        scratch_shapes=[pltpu.SemaphoreType.DMA, pltpu.SemaphoreType.DMA],
    )(f)


def _add_pairs(core, gp, other):
    n = gp.shape[0]
    t = 512
    per_half = HALF_ROWS // t

    def body(c_ref, a_ref, b_ref, o_ref):
        o_ref[...] = (a_ref[...] + b_ref[...]).astype(BF16)

    spec = pl.BlockSpec((None, t, D_MODEL), lambda i, j, c_ref: (i, j, 0))
    return pl.pallas_call(
        body, name="add_pairs",
        grid_spec=pltpu.PrefetchScalarGridSpec(
            num_scalar_prefetch=1, grid=(n, per_half),
            in_specs=[pl.BlockSpec((None, t, D_MODEL), lambda i, j, c_ref: (i, c_ref[0] * per_half + j, 0)), spec],
            out_specs=spec),
        out_shape=jax.ShapeDtypeStruct(other.shape, BF16), compiler_params=_params(("parallel", "parallel")),
    )(core, gp, other)


def _add_chips(me, s1, r2):
    _, r, _ = r2.shape
    t = 512

    def body(me_ref, s_ref, r_ref, o_ref):
        own = s_ref[...].astype(F32)
        parts = [jnp.where(me_ref[0] == k, own, r_ref[k].astype(F32)) for k in range(N_BLK)]
        o_ref[...] = ((parts[0] + parts[1]) + parts[2]) + parts[3]

    return pl.pallas_call(
        body, name="add_chips",
        grid_spec=pltpu.PrefetchScalarGridSpec(
            num_scalar_prefetch=1, grid=(r // t,),
            in_specs=[pl.BlockSpec((None, t, D_MODEL), lambda i, me_ref: (me_ref[0], i, 0)),
                      pl.BlockSpec((N_BLK, t, D_MODEL), lambda i, me_ref: (0, i, 0))],
            out_specs=pl.BlockSpec((t, D_MODEL), lambda i, me_ref: (i, 0))),
        out_shape=jax.ShapeDtypeStruct((r, D_MODEL), F32), compiler_params=_params(("parallel",)),
    )(me, s1, r2)


def _adamw(core, mine, other, w, m, v):
    t = 512
    per_half = HALF_ROWS // t
    c1 = 1.0 - ADAM_B1 ** ADAM_STEP
    c2 = 1.0 - ADAM_B2 ** ADAM_STEP

    def body(c_ref, a_ref, b_ref, w_ref, m_ref, v_ref, g_ref, d_ref, mo_ref, vo_ref):
        gv = jnp.where(pl.program_id(0) // per_half == c_ref[0], a_ref[...], b_ref[...])
        g_ref[...] = gv
        m_new = ADAM_B1 * m_ref[...] + (1.0 - ADAM_B1) * gv
        v_new = ADAM_B2 * v_ref[...] + (1.0 - ADAM_B2) * (gv * gv)
        mo_ref[...] = m_new
        vo_ref[...] = v_new
        d_ref[...] = -ADAM_LR * ((m_new / c1) / (jnp.sqrt(v_new / c2) + ADAM_EPS) + ADAM_WD * w_ref[...])

    half = pl.BlockSpec((t, D_MODEL), lambda i, c_ref: (i % per_half, 0))
    spec = pl.BlockSpec((t, D_MODEL), lambda i, c_ref: (i, 0))
    shape = jax.ShapeDtypeStruct((PACK_ROWS, D_MODEL), F32)
    return pl.pallas_call(
        body, name="adamw",
        grid_spec=pltpu.PrefetchScalarGridSpec(
            num_scalar_prefetch=1, grid=(PACK_ROWS // t,),
            in_specs=[half, half, spec, spec, spec], out_specs=[spec] * 4),
        out_shape=[shape] * 4, compiler_params=_params(("parallel",)),
    )(core, mine, other, w, m, v)


_REPLICATED = ("norm_mix_g", "gmlp_v_g", "gmlp_w_s", "gmlp_b_s", "conf_ln_g", "conf_ln_b", "mix_out_g", "norm_ffn_g",
               "final_norm_g")
_REP_SHAPES = {"norm_mix_g": (DEPTH, D_MODEL), "gmlp_v_g": (DEPTH, D_GROUP), "gmlp_w_s": (DEPTH, N_HEADS, CHUNK, CHUNK),
               "gmlp_b_s": (DEPTH, N_HEADS, CHUNK), "conf_ln_g": (DEPTH, D_GROUP), "conf_ln_b": (DEPTH, D_GROUP),
               "mix_out_g": (DEPTH, D_MODEL), "norm_ffn_g": (DEPTH, D_MODEL), "final_norm_g": (D_MODEL,)}
_BIG_ROWS = (("w_in", 1280), ("w_out", 512), ("w_up", 2048), ("w_down", 2048))
_CONV_ROWS = 8
_REP_ROWS = 144
_CH_BLK = D_GROUP // N_BLK


def _pad_rows(flat, rows):
    pad = rows * D_MODEL - flat.shape[-1]
    flat = jnp.pad(flat, [(0, 0)] * (flat.ndim - 1) + [(0, pad)])
    return flat.reshape(flat.shape[:-1] + (rows, D_MODEL))


def _pack(big, scw, ccw, rep):
    lead = scw.shape[:-3]
    conv = jnp.concatenate([scw.reshape(lead + (-1,)), ccw.reshape(lead + (-1,))], axis=-1)
    flat = jnp.concatenate([rep[k].reshape(-1) for k in _REPLICATED])
    flat = jnp.broadcast_to(flat, lead + flat.shape)
    used = sum(r for _, r in _BIG_ROWS) + _CONV_ROWS + _REP_ROWS
    parts = [big[k] for k, _ in _BIG_ROWS] + [_pad_rows(conv, _CONV_ROWS), _pad_rows(flat, _REP_ROWS),
                                             jnp.zeros(lead + (PACK_ROWS - used, D_MODEL), F32)]
    return jnp.concatenate(parts, axis=-2)


def _unpack(pk):
    out = {}
    row = 0
    shapes = {"w_in": (DEPTH, D_MODEL, W_IN_BLK), "w_out": (DEPTH, D_GROUP, D_MODEL),
              "w_up": (DEPTH, D_MODEL, D_FF_BLK), "w_down": (DEPTH, D_FF_BLK, D_MODEL)}
    for k, r in _BIG_ROWS:
        out[k] = pk[row:row + r].reshape(shapes[k])
        row += r
    conv = pk[row:row + _CONV_ROWS].reshape(-1)
    n_s = DEPTH * SHORT_K * _CH_BLK
    out["short_conv_w"] = conv[:n_s].reshape(DEPTH, SHORT_K, _CH_BLK)
    out["conf_conv_w"] = conv[n_s:n_s + DEPTH * CONF_K * _CH_BLK].reshape(DEPTH, CONF_K, _CH_BLK)
    row += _CONV_ROWS
    flat = pk[row:row + _REP_ROWS].reshape(-1)
    at = 0
    for k in _REPLICATED:
        n = math.prod(_REP_SHAPES[k])
        out[k] = flat[at:at + n].reshape(_REP_SHAPES[k])
        at += n
    return out


def _conv_blocks(w):
    d, k, _ = w.shape
    return w.reshape(d, k, N_BLK, _CH_BLK).transpose(2, 0, 1, 3)


_WEIGHTS = ("norm_mix_g", "w_in", "gmlp_v_g", "gmlp_w_s", "gmlp_b_s", "short_conv_w", "conf_conv_w", "conf_ln_g",
            "conf_ln_b", "mix_out_g", "w_out", "norm_ffn_g", "w_up", "w_down", "final_norm_g")


def kernel(x, norm_mix_g, w_in, gmlp_v_g, gmlp_w_s, gmlp_b_s, short_conv_w, conf_conv_w, conf_ln_g, conf_ln_b, mix_out_g, w_out, norm_ffn_g, w_up, w_down, final_norm_g, loss_target, m_norm_mix_g, m_w_in, m_gmlp_v_g, m_gmlp_w_s, m_gmlp_b_s, m_short_conv_w, m_conf_conv_w, m_conf_ln_g, m_conf_ln_b, m_mix_out_g, m_w_out, m_norm_ffn_g, m_w_up, m_w_down, m_final_norm_g, v_norm_mix_g, v_w_in, v_gmlp_v_g, v_gmlp_w_s, v_gmlp_b_s, v_short_conv_w, v_conf_conv_w, v_conf_ln_g, v_conf_ln_b, v_mix_out_g, v_w_out, v_norm_ffn_g, v_w_up, v_w_down, v_final_norm_g):
    w = dict(norm_mix_g=norm_mix_g, w_in=w_in, gmlp_v_g=gmlp_v_g, gmlp_w_s=gmlp_w_s, gmlp_b_s=gmlp_b_s,
             short_conv_w=short_conv_w, conf_conv_w=conf_conv_w, conf_ln_g=conf_ln_g, conf_ln_b=conf_ln_b,
             mix_out_g=mix_out_g, w_out=w_out, norm_ffn_g=norm_ffn_g, w_up=w_up, w_down=w_down, final_norm_g=final_norm_g)
    m = dict(norm_mix_g=m_norm_mix_g, w_in=m_w_in, gmlp_v_g=m_gmlp_v_g, gmlp_w_s=m_gmlp_w_s, gmlp_b_s=m_gmlp_b_s,
             short_conv_w=m_short_conv_w, conf_conv_w=m_conf_conv_w, conf_ln_g=m_conf_ln_g, conf_ln_b=m_conf_ln_b,
             mix_out_g=m_mix_out_g, w_out=m_w_out, norm_ffn_g=m_norm_ffn_g, w_up=m_w_up, w_down=m_w_down,
             final_norm_g=m_final_norm_g)
    v = dict(norm_mix_g=v_norm_mix_g, w_in=v_w_in, gmlp_v_g=v_gmlp_v_g, gmlp_w_s=v_gmlp_w_s, gmlp_b_s=v_gmlp_b_s,
             short_conv_w=v_short_conv_w, conf_conv_w=v_conf_conv_w, conf_ln_g=v_conf_ln_g, conf_ln_b=v_conf_ln_b,
             mix_out_g=v_mix_out_g, w_out=v_w_out, norm_ffn_g=v_norm_ffn_g, w_up=v_w_up, w_down=v_w_down,
             final_norm_g=v_final_norm_g)
    core = lax.axis_index("c").astype(jnp.int32).reshape(1)
    me = (2 * lax.axis_index("x") + lax.axis_index("y")).astype(jnp.int32).reshape(1)

    conv_mine = _pad_rows(jnp.concatenate([short_conv_w.reshape(-1), conf_conv_w.reshape(-1)]), _CONV_ROWS)
    gathered = _gather_blocks([_cast_into_slot(w[k], me, "cast_" + k) for k, _ in _BIG_ROWS], [conv_mine])
    big = dict(zip([k for k, _ in _BIG_ROWS], gathered[:4]))
    conv_all = gathered[4].reshape(N_BLK, -1)
    n_s = DEPTH * SHORT_K * _CH_BLK
    scw_all = conv_all[:, :n_s].reshape(N_BLK, DEPTH, SHORT_K, _CH_BLK)
    ccw_all = conv_all[:, n_s:n_s + DEPTH * CONF_K * _CH_BLK].reshape(N_BLK, DEPTH, CONF_K, _CH_BLK)
    small = {k: w[k] for k in _REPLICATED}
    small["short_conv_w"] = scw_all.transpose(1, 2, 0, 3).reshape(DEPTH, SHORT_K, D_GROUP)
    small["conf_conv_w"] = ccw_all.transpose(1, 2, 0, 3).reshape(DEPTH, CONF_K, D_GROUP)

    loss, dx, g = _local_step(x[0], loss_target[0], big, small)

    g_big = {k: g[k].reshape(N_BLK, r, D_MODEL) for k, r in _BIG_ROWS}
    gp = _pack(g_big, _conv_blocks(g["short_conv_w"]), _conv_blocks(g["conf_conv_w"]), g)
    chip_sum = _add_pairs(core, gp, _swap_halves(gp))
    total_half = _add_chips(me, chip_sum, _send_to_owners(chip_sum))

    def pack_own(t):
        return _pack({k: t[k].reshape(r, D_MODEL) for k, r in _BIG_ROWS}, t["short_conv_w"], t["conf_conv_w"], t)

    g_mine, delta, m_new, v_new = _adamw(core, total_half, _swap_reduced(total_half), pack_own(w), pack_own(m), pack_own(v))

    outs = [lax.psum(loss[0, 0], ("x", "y", "c")), dx[None]]
    for pk in (g_mine, delta, m_new, v_new):
        parts = _unpack(pk)
        outs += [parts[k] for k in _WEIGHTS]
    return tuple(outs)
```

```python
import math

import jax
import jax.numpy as jnp
from jax import lax
from jax.experimental import pallas as pl
from jax.experimental.pallas import tpu as pltpu

F32 = jnp.float32
BF16 = jnp.bfloat16

D_MODEL = 1024
D_GROUP = 256
N_HEADS = 4
HEAD_DIM = 64
CHUNK = 128
D_IN = 2560
N_BLK = 4
W_IN_BLK = D_IN // N_BLK
D_FF_BLK = 1024
DEPTH = 2
EPS = 1e-6
HALO = 32
SHORT_K = 3
CONF_K = 31
ATT_TQ = 256
ATT_TK = 256
ATT_SCALE = 0.125
ATT_DEAD = -104.0
V7X_VMEM_LIMIT = 56 * 1024 * 1024

ADAM_LR, ADAM_B1, ADAM_B2, ADAM_EPS, ADAM_WD, ADAM_STEP = 0.001, 0.9, 0.999, 1e-08, 0.01, 10

MESH = pl.DeviceIdType.MESH


def _params(sem, vmem=None):
    return pltpu.CompilerParams(dimension_semantics=sem, vmem_limit_bytes=vmem)


def _tile(s, t):
    return min(s, t)


def _rsqrt_mean(v):
    return lax.rsqrt(jnp.mean(v * v, axis=-1, keepdims=True) + EPS)


def _sigmoid(v):
    return 1.0 / (1.0 + jnp.exp(-v))


_GELU_C = math.sqrt(2.0 / math.pi)


def _gelu(v):
    t = jnp.tanh(_GELU_C * (v + 0.044715 * (v * v * v)))
    return v * (0.5 * (1.0 + t))


def _gelu_grad(v):
    t = jnp.tanh(_GELU_C * (v + 0.044715 * (v * v * v)))
    return 0.5 * (1.0 + t) + v * (0.5 * (1.0 - t * t) * _GELU_C * (1.0 + 3.0 * 0.044715 * (v * v)))


def _dot(a, b):
    return jnp.dot(a, b, preferred_element_type=F32)


def _dot_nt(a, b):
    return lax.dot_general(a, b, (((1,), (1,)), ((), ())), preferred_element_type=F32)


def _dot_tn(a, b):
    return lax.dot_general(a, b, (((0,), (0,)), ((), ())), preferred_element_type=F32)


def _cast_into_slot(w, me, name):
    n, r, c = w.shape
    tr = _tile(r, 256)

    def body(me_ref, w_ref, o_ref):
        o_ref[...] = w_ref[...].astype(BF16)

    return pl.pallas_call(
        body, name=name,
        grid_spec=pltpu.PrefetchScalarGridSpec(
            num_scalar_prefetch=1, grid=(n, r // tr),
            in_specs=[pl.BlockSpec((None, tr, c), lambda a, b, me_ref: (a, b, 0))],
            out_specs=pl.BlockSpec((None, None, tr, c), lambda a, b, me_ref: (me_ref[0], a, b, 0))),
        out_shape=jax.ShapeDtypeStruct((N_BLK,) + w.shape, BF16),
        compiler_params=_params(("parallel", "parallel")),
    )(me, w)


def _split_heads(xv, rows_ref, cols_ref):
    if rows_ref is not None:
        for h in range(N_HEADS):
            rows_ref[h] = xv[:, h * HEAD_DIM:(h + 1) * HEAD_DIM].astype(BF16)
    if cols_ref is not None:
        xt = xv.T
        for h in range(N_HEADS):
            cols_ref[h] = xt[h * HEAD_DIM:(h + 1) * HEAD_DIM, :].astype(BF16)


def _head_specs(t, s):
    rows = (pl.BlockSpec((N_HEADS, t, HEAD_DIM), lambda i: (0, i, 0)), jax.ShapeDtypeStruct((N_HEADS, s, HEAD_DIM), BF16))
    cols = (pl.BlockSpec((N_HEADS, HEAD_DIM, t), lambda i: (0, 0, i)), jax.ShapeDtypeStruct((N_HEADS, HEAD_DIM, s), BF16))
    return rows, cols


def _in_proj(x, g, w, layer):
    s = x.shape[0]
    t = _tile(s, 512)

    def body(x_ref, g_ref, w_ref, z_ref, h_ref, qr_ref, qt_ref, kt_ref, vt_ref):
        xv = x_ref[...]
        h = (xv * _rsqrt_mean(xv) * g_ref[...]).astype(BF16)
        h_ref[...] = h
        for b in range(N_BLK):
            z_ref[:, b * W_IN_BLK:(b + 1) * W_IN_BLK] = _dot(h, w_ref[b])
        _split_heads(z_ref[:, 1280:1536] * ATT_SCALE, qr_ref, qt_ref)
        _split_heads(z_ref[:, 1536:1792], None, kt_ref)
        _split_heads(z_ref[:, 1792:2048], None, vt_ref)

    rows, cols = _head_specs(t, s)
    return pl.pallas_call(
        body, name="in_proj", grid=(s // t,),
        in_specs=[pl.BlockSpec((t, D_MODEL), lambda i: (i, 0)), _full((1, D_MODEL)),
                  pl.BlockSpec((N_BLK, None, D_MODEL, W_IN_BLK), lambda i: (0, layer, 0, 0))],
        out_specs=[pl.BlockSpec((t, D_IN), lambda i: (i, 0)), pl.BlockSpec((t, D_MODEL), lambda i: (i, 0)),
                   rows[0], cols[0], cols[0], cols[0]],
        out_shape=[jax.ShapeDtypeStruct((s, D_IN), F32), jax.ShapeDtypeStruct((s, D_MODEL), BF16),
                   rows[1], cols[1], cols[1], cols[1]],
        compiler_params=_params(("parallel",), V7X_VMEM_LIMIT),
    )(x, g, w)


def _mix_a_fwd(z_ref, vg, wt_ref, bmat, t):
    zu = z_ref[:, 0:256]
    zv = z_ref[:, 256:512]
    u = _gelu(zu)
    v = _gelu(zv)
    rv = _rsqrt_mean(v)
    vh = v * rv
    vnb = (vh * vg).astype(BF16)
    head = lax.broadcasted_iota(jnp.int32, (CHUNK, D_GROUP), 1) // HEAD_DIM
    fs = []
    for c in range(t // CHUNK):
        vc = vnb[c * CHUNK:(c + 1) * CHUNK, :]
        fc = bmat
        for h in range(N_HEADS):
            fc = fc + jnp.where(head == h, _dot(wt_ref[h], vc), 0.0)
        fs.append(fc)
    f = jnp.concatenate(fs, axis=0) if len(fs) > 1 else fs[0]
    return zu, zv, u, rv, vh, vnb, f


def _mix_b_fwd(z_ref, zh_ref, first, scw_ref, ext_ref, t):
    gb = z_ref[:, 512:768]
    uh = zh_ref[:, 768:1024] * zh_ref[:, 1024:1280]
    ext_ref[0:HALO, :] = jnp.where(first, 0.0, uh)
    ext_ref[HALO:HALO + t, :] = z_ref[:, 768:1024] * z_ref[:, 1024:1280]
    cv = jnp.zeros((t, D_GROUP), F32)
    for k in range(SHORT_K):
        cv = cv + scw_ref[k:k + 1, :] * ext_ref[pl.ds(HALO - (SHORT_K - 1) + k, t), :]
    return gb, cv


def _mix_d_fwd(z_ref, zh_ref, first, ccw_ref, lg, lb, ext_ref, t):
    hh = zh_ref[:, 2048:2304] * _sigmoid(zh_ref[:, 2304:2560])
    ext_ref[0:HALO, :] = jnp.where(first, 0.0, hh)
    ext_ref[HALO:HALO + t, :] = z_ref[:, 2048:2304] * _sigmoid(z_ref[:, 2304:2560])
    cv = jnp.zeros((t, D_GROUP), F32)
    for k in range(CONF_K):
        cv = cv + ccw_ref[k:k + 1, :] * ext_ref[pl.ds(HALO - (CONF_K - 1) + k, t), :]
    xc = cv - jnp.mean(cv, axis=-1, keepdims=True)
    rs = lax.rsqrt(jnp.mean(xc * xc, axis=-1, keepdims=True) + EPS)
    xh = xc * rs
    ln = xh * lg + lb
    return xh, rs, ln, _sigmoid(ln)


def _mix_specs(t, s):
    per = t // HALO
    return [pl.BlockSpec((t, D_IN), lambda i: (i, 0)),
            pl.BlockSpec((HALO, D_IN), lambda i: (jnp.maximum(i * per - 1, 0), 0))]


def _full(shape):
    return pl.BlockSpec(shape, lambda i: (0,) * len(shape))


def _mixers_fwd(z, p):
    s = z.shape[0]
    t = _tile(s, 256)

    def body(z_ref, zh_ref, vg_ref, wt_ref, bm_ref, scw_ref, ccw_ref, lg_ref, lb_ref, y_ref, eb_ref, ed_ref):
        first = pl.program_id(0) == 0
        _, _, u, _, _, _, f = _mix_a_fwd(z_ref, vg_ref[...], wt_ref, bm_ref[...], t)
        ya = u * f
        y_ref[:, 0:256] = ya * _rsqrt_mean(ya)
        gb, cv = _mix_b_fwd(z_ref, zh_ref, first, scw_ref, eb_ref, t)
        yb = gb * cv
        y_ref[:, 256:512] = yb * _rsqrt_mean(yb)
        _, _, ln, sg = _mix_d_fwd(z_ref, zh_ref, first, ccw_ref, lg_ref[...], lb_ref[...], ed_ref, t)
        yd = ln * sg
        y_ref[:, 512:768] = yd * _rsqrt_mean(yd)

    return pl.pallas_call(
        body, name="mixers_fwd", grid=(s // t,),
        in_specs=_mix_specs(t, s) + [_full((1, D_GROUP)), _full((N_HEADS, CHUNK, CHUNK)), _full((CHUNK, D_GROUP)),
                                     _full((8, D_GROUP)), _full((32, D_GROUP)), _full((1, D_GROUP)), _full((1, D_GROUP))],
        out_specs=pl.BlockSpec((t, 768), lambda i: (i, 0)),
        out_shape=jax.ShapeDtypeStruct((s, 768), F32),
        scratch_shapes=[pltpu.VMEM((HALO + t, D_GROUP), F32), pltpu.VMEM((HALO + t, D_GROUP), F32)],
        compiler_params=_params(("parallel",)),
    )(z, z, p["vg"], p["wt"], p["bmat"], p["scw"], p["ccw"], p["lg"], p["lb"])


def _mixers_bwd_a(z, dyn, o_t, p):
    s = z.shape[0]
    t = _tile(s, 256)
    n_chunk = t // CHUNK

    def body(z_ref, zh_ref, dyn_ref, ot_ref, vg_ref, wt_ref, wtt_ref, bm_ref, scw_ref, ccw_ref, lg_ref, lb_ref,
             dza_ref, dcb_ref, dcd_ref, dor_ref, dot_ref, ds_ref, dvg_ref, dws_ref, dbm_ref, dscw_ref, dccw_ref, dlg_ref, dlb_ref,
             eb_ref, ed_ref):
        i = pl.program_id(0)
        first = i == 0

        @pl.when(first)
        def _():
            for r in (dvg_ref, dws_ref, dbm_ref, dscw_ref, dccw_ref, dlg_ref, dlb_ref):
                r[...] = jnp.zeros_like(r)

        def rms_bwd(y, dn):
            r = _rsqrt_mean(y)
            yn = y * r
            return r * (dn - yn * jnp.mean(dn * yn, axis=-1, keepdims=True))

        vg = vg_ref[...]
        zu, zv, u, rv, vh, vnb, f = _mix_a_fwd(z_ref, vg, wt_ref, bm_ref[...], t)
        dya = rms_bwd(u * f, dyn_ref[:, 0:256])
        du = dya * f
        df = dya * u
        head = lax.broadcasted_iota(jnp.int32, (CHUNK, D_GROUP), 1) // HEAD_DIM
        dvns = []
        dbm = jnp.zeros((CHUNK, D_GROUP), F32)
        for c in range(n_chunk):
            dfc = df[c * CHUNK:(c + 1) * CHUNK, :]
            vc = vnb[c * CHUNK:(c + 1) * CHUNK, :]
            dbm = dbm + dfc
            dvn = jnp.zeros((CHUNK, D_GROUP), F32)
            for h in range(N_HEADS):
                dfh = jnp.where(head == h, dfc, 0.0).astype(BF16)
                dvn = dvn + _dot(wtt_ref[h], dfh)
                dws_ref[h] += _dot_nt(dfh, vc)
            dvns.append(dvn)
        dbm_ref[...] += dbm
        dvn = jnp.concatenate(dvns, axis=0) if n_chunk > 1 else dvns[0]
        dvg_ref[...] += jnp.sum(dvn * vh, axis=0, keepdims=True)
        dvh = dvn * vg
        dv = rv * (dvh - vh * jnp.mean(dvh * vh, axis=-1, keepdims=True))
        dza_ref[:, 0:256] = (du * _gelu_grad(zu)).astype(BF16)
        dza_ref[:, 256:512] = (dv * _gelu_grad(zv)).astype(BF16)

        gb, cv = _mix_b_fwd(z_ref, zh_ref, first, scw_ref, eb_ref, t)
        dyb = rms_bwd(gb * cv, dyn_ref[:, 256:512])
        dza_ref[:, 512:768] = (dyb * cv).astype(BF16)
        dcb = dyb * gb
        dcb_ref[...] = dcb
        for k in range(SHORT_K):
            dscw_ref[k:k + 1, :] += jnp.sum(dcb * eb_ref[pl.ds(HALO - (SHORT_K - 1) + k, t), :], axis=0, keepdims=True)

        lg = lg_ref[...]
        xh, rs, ln, sg = _mix_d_fwd(z_ref, zh_ref, first, ccw_ref, lg, lb_ref[...], ed_ref, t)
        dyd = rms_bwd(ln * sg, dyn_ref[:, 768:1024])
        dln = dyd * (sg * (1.0 + ln * (1.0 - sg)))
        dlg_ref[...] += jnp.sum(dln * xh, axis=0, keepdims=True)
        dlb_ref[...] += jnp.sum(dln, axis=0, keepdims=True)
        dxh = dln * lg
        dcd = rs * (dxh - jnp.mean(dxh, axis=-1, keepdims=True) - xh * jnp.mean(dxh * xh, axis=-1, keepdims=True))
        dcd_ref[...] = dcd
        for k in range(CONF_K):
            dccw_ref[k:k + 1, :] += jnp.sum(dcd * ed_ref[pl.ds(HALO - (CONF_K - 1) + k, t), :], axis=0, keepdims=True)

        o = ot_ref[...].reshape(D_GROUP, t).T
        do = rms_bwd(o, dyn_ref[:, 512:768])
        _split_heads(do, dor_ref, dot_ref)
        prod = do.astype(BF16).astype(F32) * o
        for h in range(N_HEADS):
            ds_ref[h] = jnp.sum(prod[:, h * HEAD_DIM:(h + 1) * HEAD_DIM], axis=1, keepdims=True)

    small = [(1, D_GROUP), (N_HEADS, CHUNK, CHUNK), (CHUNK, D_GROUP), (8, D_GROUP), (32, D_GROUP), (1, D_GROUP), (1, D_GROUP)]
    rows, cols = _head_specs(t, s)
    return pl.pallas_call(
        body, name="mixers_bwd_a", grid=(s // t,),
        in_specs=_mix_specs(t, s) + [pl.BlockSpec((t, D_MODEL), lambda i: (i, 0)),
                                     pl.BlockSpec((N_HEADS, HEAD_DIM, t), lambda i: (0, 0, i)),
                                     _full((1, D_GROUP)), _full((N_HEADS, CHUNK, CHUNK)), _full((N_HEADS, CHUNK, CHUNK)),
                                     _full((CHUNK, D_GROUP)), _full((8, D_GROUP)), _full((32, D_GROUP)),
                                     _full((1, D_GROUP)), _full((1, D_GROUP))],
        out_specs=[pl.BlockSpec((t, 768), lambda i: (i, 0)), pl.BlockSpec((t, D_GROUP), lambda i: (i, 0)),
                   pl.BlockSpec((t, D_GROUP), lambda i: (i, 0)), rows[0], cols[0],
                   pl.BlockSpec((N_HEADS, t, 1), lambda i: (0, i, 0))]
                  + [_full(sh) for sh in small],
        out_shape=[jax.ShapeDtypeStruct((s, 768), BF16), jax.ShapeDtypeStruct((s, D_GROUP), F32),
                   jax.ShapeDtypeStruct((s, D_GROUP), F32), rows[1], cols[1],
                   jax.ShapeDtypeStruct((N_HEADS, s, 1), F32)]
                  + [jax.ShapeDtypeStruct(sh, F32) for sh in small],
        scratch_shapes=[pltpu.VMEM((HALO + t, D_GROUP), F32), pltpu.VMEM((HALO + t, D_GROUP), F32)],
        compiler_params=_params(("arbitrary",)),
    )(z, z, dyn, o_t, p["vg"], p["wt"], p["wtt"], p["bmat"], p["scw"], p["ccw"], p["lg"], p["lb"])


def _mixers_bwd_b(z, dza, dcb, dcd, dq_t, dk_t, dv_t, p):
    s = z.shape[0]
    t = _tile(s, 256)
    per = t // HALO
    n_halo = s // HALO

    def body(z_ref, dza_ref, dcb_ref, dcbn_ref, dcd_ref, dcdn_ref, dq_ref, dk_ref, dv_ref, scw_ref, ccw_ref,
             dz_ref, eb_ref, ed_ref):
        last = pl.program_id(0) == pl.num_programs(0) - 1
        dz_ref[:, 0:768] = dza_ref[...]
        eb_ref[0:t, :] = dcb_ref[...]
        eb_ref[t:t + HALO, :] = jnp.where(last, 0.0, dcbn_ref[...])
        du = jnp.zeros((t, D_GROUP), F32)
        for k in range(SHORT_K):
            du = du + scw_ref[k:k + 1, :] * eb_ref[pl.ds(SHORT_K - 1 - k, t), :]
        dz_ref[:, 768:1024] = (du * z_ref[:, 1024:1280]).astype(BF16)
        dz_ref[:, 1024:1280] = (du * z_ref[:, 768:1024]).astype(BF16)
        for n, r in enumerate((dq_ref, dk_ref, dv_ref)):
            dz_ref[:, 1280 + 256 * n:1536 + 256 * n] = r[...].reshape(D_GROUP, t).T.astype(BF16)
        ed_ref[0:t, :] = dcd_ref[...]
        ed_ref[t:t + HALO, :] = jnp.where(last, 0.0, dcdn_ref[...])
        dh = jnp.zeros((t, D_GROUP), F32)
        for k in range(CONF_K):
            dh = dh + ccw_ref[k:k + 1, :] * ed_ref[pl.ds(CONF_K - 1 - k, t), :]
        a = z_ref[:, 2048:2304]
        sg = _sigmoid(z_ref[:, 2304:2560])
        dz_ref[:, 2048:2304] = (dh * sg).astype(BF16)
        dz_ref[:, 2304:2560] = (dh * a * sg * (1.0 - sg)).astype(BF16)

    nxt = lambda i: (jnp.minimum((i + 1) * per, n_halo - 1), 0)
    tr = pl.BlockSpec((N_HEADS, HEAD_DIM, t), lambda i: (0, 0, i))
    return pl.pallas_call(
        body, name="mixers_bwd_b", grid=(s // t,),
        in_specs=[pl.BlockSpec((t, D_IN), lambda i: (i, 0)), pl.BlockSpec((t, 768), lambda i: (i, 0)),
                  pl.BlockSpec((t, D_GROUP), lambda i: (i, 0)), pl.BlockSpec((HALO, D_GROUP), nxt),
                  pl.BlockSpec((t, D_GROUP), lambda i: (i, 0)), pl.BlockSpec((HALO, D_GROUP), nxt),
                  tr, tr, tr, _full((8, D_GROUP)), _full((32, D_GROUP))],
        out_specs=pl.BlockSpec((t, D_IN), lambda i: (i, 0)),
        out_shape=jax.ShapeDtypeStruct((s, D_IN), BF16),
        scratch_shapes=[pltpu.VMEM((HALO + t, D_GROUP), F32), pltpu.VMEM((HALO + t, D_GROUP), F32)],
        compiler_params=_params(("parallel",)),
    )(z, dza, dcb, dcb, dcd, dcd, dq_t, dk_t, dv_t, p["scw"], p["ccw"])


def _split_bf16(v):
    hi = v.astype(BF16)
    return hi, (v - hi.astype(F32)).astype(BF16)


def _att_scores(qs, kts, carries, tri, mask):
    zs = [_dot(q, kt) for q, kt in zip(qs, kts)]
    lms, lbs, parts = [], [], []
    for z in zs:
        soft = jnp.log(1.0 + jnp.exp(-jnp.abs(z)))
        lm = -(jnp.maximum(z, 0.0) + soft)
        lbs.append(lm + z)
        if mask is not None:
            lm = jnp.where(mask, lm, 0.0)
        lms.append(lm)
        parts.append(_split_bf16(lm))
    rights = [_dot(hi, tri) + _dot(lo, tri) for hi, lo in parts]
    ws = []
    for lb, right, carry in zip(lbs, rights, carries):
        w = jnp.exp(lb + right + carry)
        ws.append(w if mask is None else jnp.where(mask, w, 0.0))
    return ws, lbs, [jnp.sum(lm, axis=1, keepdims=True) for lm in lms]


def _att_consts(i):
    j_hi = ((i + 1) * ATT_TQ - 1) // ATT_TK
    row = lax.broadcasted_iota(jnp.int32, (ATT_TQ, ATT_TK), 0) + i * ATT_TQ
    col = lax.broadcasted_iota(jnp.int32, (ATT_TQ, ATT_TK), 1) + j_hi * ATT_TK
    r_i = lax.broadcasted_iota(jnp.int32, (ATT_TK, ATT_TK), 0)
    c_i = lax.broadcasted_iota(jnp.int32, (ATT_TK, ATT_TK), 1)
    return j_hi, col < row, r_i, c_i


def _att_alive(j, carries):
    top = carries[0]
    for c in carries[1:]:
        top = jnp.maximum(top, c)
    return jnp.logical_and(j >= 0, jnp.max(top) > ATT_DEAD)


def _attn_fwd(q_r, k_t, v_t):
    s = q_r.shape[1]

    def body(q_ref, kt_ref, vt_ref, o_ref):
        j_hi, mask, r_i, c_i = _att_consts(pl.program_id(0))
        tri = (r_i > c_i).astype(BF16)

        heads = range(N_HEADS)

        def tiles(j, carries, accs, mask):
            cols = pl.ds(pl.multiple_of(j * ATT_TK, ATT_TK), ATT_TK)
            ws, _, tots = _att_scores([q_ref[h] for h in heads], [kt_ref[h, :, cols] for h in heads], carries, tri, mask)
            accs = [acc + _dot_nt(vt_ref[h, :, cols], w.astype(BF16)) for h, acc, w in zip(heads, accs, ws)]
            return [c + t for c, t in zip(carries, tots)], accs

        state = tiles(j_hi, [jnp.zeros((ATT_TQ, 1), F32)] * N_HEADS, [jnp.zeros((HEAD_DIM, ATT_TQ), F32)] * N_HEADS, mask)

        def cond(c):
            return _att_alive(c[0], c[1])

        def step(c):
            return (c[0] - 1,) + tuple(tiles(c[0], c[1], c[2], None))

        _, _, accs = lax.while_loop(cond, step, (j_hi - 1,) + tuple(state))
        for h in heads:
            o_ref[h] = accs[h]

    whole = pl.BlockSpec((N_HEADS, HEAD_DIM, s), lambda i: (0, 0, 0), pipeline_mode=pl.Buffered(1))
    return pl.pallas_call(
        body, name="attn_fwd", grid=(s // ATT_TQ,),
        in_specs=[pl.BlockSpec((N_HEADS, ATT_TQ, HEAD_DIM), lambda i: (0, i, 0)), whole, whole],
        out_specs=pl.BlockSpec((N_HEADS, HEAD_DIM, ATT_TQ), lambda i: (0, 0, i)),
        out_shape=jax.ShapeDtypeStruct((N_HEADS, HEAD_DIM, s), F32),
        compiler_params=_params(("arbitrary",), V7X_VMEM_LIMIT),
    )(q_r, k_t, v_t)


ATT_BWD_HEADS = 2


def _attn_bwd(q_r, q_t, k_t, v_t, do_r, do_t, dsum):
    s = q_r.shape[1]
    hps = ATT_BWD_HEADS

    def body(q_ref, qt_ref, kt_ref, vt_ref, do_ref, dot_ref, ds_ref, dq_ref, dk_ref, dv_ref):
        i = pl.program_id(1)

        @pl.when(i == 0)
        def _():
            dk_ref[...] = jnp.zeros_like(dk_ref)
            dv_ref[...] = jnp.zeros_like(dv_ref)

        j_hi, mask, r_i, c_i = _att_consts(i)
        tri_r = (r_i > c_i).astype(BF16)
        tri_ge = (r_i >= c_i).astype(BF16)

        heads = range(hps)

        def tiles(j, carries, gsums, accs, mask):
            cols = pl.ds(pl.multiple_of(j * ATT_TK, ATT_TK), ATT_TK)
            kts = [kt_ref[h, :, cols] for h in heads]
            das = [_dot(do_ref[h], vt_ref[h, :, cols]) for h in heads]
            ws, lbs, tots = _att_scores([q_ref[h] for h in heads], kts, carries, tri_r, mask)
            wbs = [w.astype(BF16) for w in ws]
            gs = [wb.astype(F32) * da for wb, da in zip(wbs, das)]
            parts = [_split_bf16(g) for g in gs]
            sfx = [_dot(hi, tri_ge) + _dot(lo, tri_ge) for hi, lo in parts]
            for h in heads:
                dv_ref[h, :, cols] += _dot(dot_ref[h], wbs[h])
            dzs = []
            for h in heads:
                left = ds_ref[h] - gsums[h] - sfx[h]
                dz = gs[h] - jnp.exp(lbs[h]) * (gs[h] + left)
                dzs.append((dz if mask is None else jnp.where(mask, dz, 0.0)).astype(BF16))
            for h in heads:
                dk_ref[h, :, cols] += _dot(qt_ref[h], dzs[h])
            accs = [accs[h] + _dot_nt(kts[h], dzs[h]) for h in heads]
            gsums = [gsums[h] + jnp.sum(gs[h], axis=1, keepdims=True) for h in heads]
            return [c + t for c, t in zip(carries, tots)], gsums, accs

        col0 = [jnp.zeros((ATT_TQ, 1), F32)] * hps
        state = tiles(j_hi, col0, col0, [jnp.zeros((HEAD_DIM, ATT_TQ), F32)] * hps, mask)

        def cond(c):
            return _att_alive(c[0], c[1])

        def step(c):
            return (c[0] - 1,) + tuple(tiles(c[0], c[1], c[2], c[3], None))

        _, _, _, accs = lax.while_loop(cond, step, (j_hi - 1,) + tuple(state))
        for h in heads:
            dq_ref[h] = accs[h] * ATT_SCALE

    whole = pl.BlockSpec((hps, HEAD_DIM, s), lambda g, i: (g, 0, 0))
    whole_in = pl.BlockSpec((hps, HEAD_DIM, s), lambda g, i: (g, 0, 0), pipeline_mode=pl.Buffered(1))
    rows = pl.BlockSpec((hps, ATT_TQ, HEAD_DIM), lambda g, i: (g, i, 0))
    cols = pl.BlockSpec((hps, HEAD_DIM, ATT_TQ), lambda g, i: (g, 0, i))
    shape = jax.ShapeDtypeStruct((N_HEADS, HEAD_DIM, s), F32)
    return pl.pallas_call(
        body, name="attn_bwd", grid=(N_HEADS // hps, s // ATT_TQ),
        in_specs=[rows, cols, whole_in, whole_in, rows, cols, pl.BlockSpec((hps, ATT_TQ, 1), lambda g, i: (g, i, 0))],
        out_specs=[cols, whole, whole],
        out_shape=[shape, shape, shape],
        compiler_params=_params(("parallel", "arbitrary"), V7X_VMEM_LIMIT),
    )(q_r, q_t, k_t, v_t, do_r, do_t, dsum)


def _out_proj(x, y_abd, o_t, gain, w, layer):
    s = x.shape[0]
    t = _tile(s, 512)

    def body(x_ref, y_ref, ot_ref, g_ref, w_ref, x1_ref, yn_ref):
        o = ot_ref[...].reshape(D_GROUP, t).T
        yn_ref[:, 0:512] = y_ref[:, 0:512]
        yn_ref[:, 512:768] = o * _rsqrt_mean(o)
        yn_ref[:, 768:1024] = y_ref[:, 512:768]
        yg = (yn_ref[...] * g_ref[...]).astype(BF16)
        acc = _dot(yg[:, 0:256], w_ref[0])
        for b in range(1, N_BLK):
            acc = acc + _dot(yg[:, 256 * b:256 * (b + 1)], w_ref[b])
        x1_ref[...] = x_ref[...] + acc

    return pl.pallas_call(
        body, name="out_proj", grid=(s // t,),
        in_specs=[pl.BlockSpec((t, D_MODEL), lambda i: (i, 0)), pl.BlockSpec((t, 768), lambda i: (i, 0)),
                  pl.BlockSpec((N_HEADS, HEAD_DIM, t), lambda i: (0, 0, i)), _full((1, D_MODEL)),
                  pl.BlockSpec((N_BLK, None, D_GROUP, D_MODEL), lambda i: (0, layer, 0, 0))],
        out_specs=[pl.BlockSpec((t, D_MODEL), lambda i: (i, 0)), pl.BlockSpec((t, D_MODEL), lambda i: (i, 0))],
        out_shape=[jax.ShapeDtypeStruct((s, D_MODEL), F32), jax.ShapeDtypeStruct((s, D_MODEL), F32)],
        compiler_params=_params(("parallel",)),
    )(x, y_abd, o_t, gain, w)


def _out_proj_bwd(x1, g_ffn, dh, dx2, yn, gain, w, layer):
    s = dx2.shape[0]
    t = _tile(s, 512)

    def body(x_ref, gf_ref, dh_ref, dx2_ref, yn_ref, g_ref, w_ref, dx1_ref, dgf_ref, dyn_ref, dg_ref, dw_ref):
        @pl.when(pl.program_id(0) == 0)
        def _():
            dg_ref[...] = jnp.zeros_like(dg_ref)
            dw_ref[...] = jnp.zeros_like(dw_ref)
            dgf_ref[...] = jnp.zeros_like(dgf_ref)

        dx1, dgf = _rms_bwd_rows(x_ref[...], gf_ref[...], dh_ref[...], dx2_ref[...])
        dx1_ref[...] = dx1
        dgf_ref[...] += dgf
        dxb = dx1.astype(BF16)
        g = g_ref[...]
        yn = yn_ref[...]
        yg = (yn * g).astype(BF16)
        for b in range(N_BLK):
            cols = slice(256 * b, 256 * (b + 1))
            dyg = _dot_nt(dxb, w_ref[b])
            dw_ref[b] += _dot_tn(yg[:, cols], dxb)
            dg_ref[:, cols] += jnp.sum(dyg * yn[:, cols], axis=0, keepdims=True)
            dyn_ref[:, cols] = dyg * g[:, cols]

    row = pl.BlockSpec((t, D_MODEL), lambda i: (i, 0))
    vec = _full((1, D_MODEL))
    return pl.pallas_call(
        body, name="out_proj_bwd", grid=(s // t,),
        in_specs=[row, vec, row, row, row, vec, pl.BlockSpec((N_BLK, None, D_GROUP, D_MODEL), lambda i: (0, layer, 0, 0))],
        out_specs=[row, vec, row, vec, _full((N_BLK, D_GROUP, D_MODEL))],
        out_shape=[jax.ShapeDtypeStruct((s, D_MODEL), F32), jax.ShapeDtypeStruct((1, D_MODEL), F32),
                   jax.ShapeDtypeStruct((s, D_MODEL), F32), jax.ShapeDtypeStruct((1, D_MODEL), F32),
                   jax.ShapeDtypeStruct((N_BLK, D_GROUP, D_MODEL), F32)],
        compiler_params=_params(("arbitrary",), V7X_VMEM_LIMIT),
    )(x1, g_ffn, dh, dx2, yn, gain, w)


def _ffn(x, g, w_up, w_down, layer):
    s = x.shape[0]
    t = _tile(s, 1024)

    def body(x_ref, g_ref, wu_ref, wd_ref, x2_ref, p_ref, h_ref):
        @pl.when(pl.program_id(1) == 0)
        def _():
            xv = x_ref[...]
            h_ref[...] = (xv * _rsqrt_mean(xv) * g_ref[...]).astype(BF16)
            x2_ref[...] = xv

        pre = _dot(h_ref[...], wu_ref[...])
        p_ref[...] = pre.astype(BF16)
        a = jnp.maximum(pre, 0.0)
        x2_ref[...] += _dot((a * a).astype(BF16), wd_ref[...])

    wspec = pl.BlockSpec((None, None, D_MODEL, D_FF_BLK), lambda i, j: (j, layer, 0, 0))
    row = pl.BlockSpec((t, D_MODEL), lambda i, j: (i, 0))
    return pl.pallas_call(
        body, name="ffn", grid=(s // t, N_BLK),
        in_specs=[row, pl.BlockSpec((1, D_MODEL), lambda i, j: (0, 0)), wspec, wspec],
        out_specs=[row, pl.BlockSpec((t, D_FF_BLK), lambda i, j: (i, j)), row],
        out_shape=[jax.ShapeDtypeStruct((s, D_MODEL), F32), jax.ShapeDtypeStruct((s, N_BLK * D_FF_BLK), BF16),
                   jax.ShapeDtypeStruct((s, D_MODEL), BF16)],
        compiler_params=_params(("parallel", "arbitrary"), V7X_VMEM_LIMIT),
    )(x, g, w_up, w_down)


def _rms_bwd_rows(xv, g, dh, dres):
    r = _rsqrt_mean(xv)
    xh = xv * r
    dxh = dh * g
    dx = dres + r * (dxh - xh * jnp.mean(dxh * xh, axis=-1, keepdims=True))
    return dx, jnp.sum(dh * xh, axis=0, keepdims=True)


def _ffn_bwd(dxb, p, w_up, w_down, layer):
    s = dxb.shape[0]
    t = _tile(s, 1024)

    def body(dx_ref, p_ref, wu_ref, wd_ref, dp_ref, dh_ref):
        da = _dot_nt(dx_ref[...], wd_ref[...])
        a = jnp.maximum(p_ref[...].astype(F32), 0.0)
        dp = (da * (2.0 * a)).astype(BF16)
        dp_ref[...] = dp
        dh = _dot_nt(dp, wu_ref[...])

        @pl.when(pl.program_id(1) == 0)
        def _():
            dh_ref[...] = dh

        @pl.when(pl.program_id(1) != 0)
        def _():
            dh_ref[...] += dh

    wspec = pl.BlockSpec((None, None, D_MODEL, D_FF_BLK), lambda i, j: (j, layer, 0, 0))
    row = pl.BlockSpec((t, D_MODEL), lambda i, j: (i, 0))
    blk = pl.BlockSpec((t, D_FF_BLK), lambda i, j: (i, j))
    return pl.pallas_call(
        body, name="ffn_bwd", grid=(s // t, N_BLK),
        in_specs=[row, blk, wspec, wspec], out_specs=[blk, row],
        out_shape=[jax.ShapeDtypeStruct((s, N_BLK * D_FF_BLK), BF16), jax.ShapeDtypeStruct((s, D_MODEL), F32)],
        compiler_params=_params(("parallel", "arbitrary"), V7X_VMEM_LIMIT),
    )(dxb, p, w_up, w_down)


def _ffn_wgrad(hb, p, dp, dxb):
    s = hb.shape[0]
    t = _tile(s, 1024)

    def body(h_ref, p_ref, dp_ref, dx_ref, du_ref, dd_ref):
        @pl.when(pl.program_id(1) == 0)
        def _():
            du_ref[...] = jnp.zeros_like(du_ref)
            dd_ref[...] = jnp.zeros_like(dd_ref)

        a = jnp.maximum(p_ref[...].astype(F32), 0.0)
        du_ref[...] += _dot_tn(h_ref[...], dp_ref[...])
        dd_ref[...] += _dot_tn((a * a).astype(BF16), dx_ref[...])

    row = pl.BlockSpec((t, D_MODEL), lambda j, i: (i, 0))
    blk = pl.BlockSpec((t, D_FF_BLK), lambda j, i: (i, j))
    out = pl.BlockSpec((None, D_MODEL, D_FF_BLK), lambda j, i: (j, 0, 0))
    shape = jax.ShapeDtypeStruct((N_BLK, D_MODEL, D_FF_BLK), F32)
    return pl.pallas_call(
        body, name="ffn_wgrad", grid=(N_BLK, s // t),
        in_specs=[row, blk, blk, row], out_specs=[out, out], out_shape=[shape, shape],
        compiler_params=_params(("parallel", "arbitrary"), V7X_VMEM_LIMIT),
    )(hb, p, dp, dxb)


def _in_proj_bwd(x, g, dx1, dz, w, layer):
    s = x.shape[0]
    t = _tile(s, 512)

    def body(x_ref, g_ref, dx1_ref, dz_ref, w_ref, dx0_ref, dxb_ref, dg_ref):
        @pl.when(pl.program_id(0) == 0)
        def _():
            dg_ref[...] = jnp.zeros_like(dg_ref)

        dh = _dot_nt(dz_ref[:, 0:W_IN_BLK], w_ref[0])
        for b in range(1, N_BLK):
            dh = dh + _dot_nt(dz_ref[:, b * W_IN_BLK:(b + 1) * W_IN_BLK], w_ref[b])
        dx, dg = _rms_bwd_rows(x_ref[...], g_ref[...], dh, dx1_ref[...])
        dx0_ref[...] = dx
        dxb_ref[...] = dx.astype(BF16)
        dg_ref[...] += dg

    row = pl.BlockSpec((t, D_MODEL), lambda i: (i, 0))
    return pl.pallas_call(
        body, name="in_proj_bwd", grid=(s // t,),
        in_specs=[row, _full((1, D_MODEL)), row, pl.BlockSpec((t, D_IN), lambda i: (i, 0)),
                  pl.BlockSpec((N_BLK, None, D_MODEL, W_IN_BLK), lambda i: (0, layer, 0, 0))],
        out_specs=[row, row, _full((1, D_MODEL))],
        out_shape=[jax.ShapeDtypeStruct((s, D_MODEL), F32), jax.ShapeDtypeStruct((s, D_MODEL), BF16),
                   jax.ShapeDtypeStruct((1, D_MODEL), F32)],
        compiler_params=_params(("arbitrary",), V7X_VMEM_LIMIT),
    )(x, g, dx1, dz, w)


def _in_proj_wgrad(hb, dz):
    s = hb.shape[0]
    t = _tile(s, 512)

    def body(h_ref, dz_ref, dw_ref):
        @pl.when(pl.program_id(0) == 0)
        def _():
            dw_ref[...] = jnp.zeros_like(dw_ref)

        h = h_ref[...]
        for b in range(N_BLK):
            dw_ref[b] += _dot_tn(h, dz_ref[:, b * W_IN_BLK:(b + 1) * W_IN_BLK])

    return pl.pallas_call(
        body, name="in_proj_wgrad", grid=(s // t,),
        in_specs=[pl.BlockSpec((t, D_MODEL), lambda i: (i, 0)), pl.BlockSpec((t, D_IN), lambda i: (i, 0))],
        out_specs=_full((N_BLK, D_MODEL, W_IN_BLK)),
        out_shape=jax.ShapeDtypeStruct((N_BLK, D_MODEL, W_IN_BLK), F32),
        compiler_params=_params(("arbitrary",), V7X_VMEM_LIMIT),
    )(hb, dz)


def _loss_head(x, g, target):
    s = x.shape[0]
    t = _tile(s, 512)

    def body(x_ref, g_ref, t_ref, l_ref, dx_ref, dxb_ref, dg_ref):
        @pl.when(pl.program_id(0) == 0)
        def _():
            l_ref[...] = jnp.zeros_like(l_ref)
            dg_ref[...] = jnp.zeros_like(dg_ref)

        xv = x_ref[...]
        g = g_ref[...]
        r = _rsqrt_mean(xv)
        xh = xv * r
        err = xh * g - t_ref[...]
        l_ref[...] += 0.5 * jnp.sum(jnp.mean(err * err, axis=-1, keepdims=True), axis=0, keepdims=True)
        dy = err * (1.0 / D_MODEL)
        dg_ref[...] += jnp.sum(dy * xh, axis=0, keepdims=True)
        dxh = dy * g
        dx = r * (dxh - xh * jnp.mean(dxh * xh, axis=-1, keepdims=True))
        dx_ref[...] = dx
        dxb_ref[...] = dx.astype(BF16)

    row = pl.BlockSpec((t, D_MODEL), lambda i: (i, 0))
    return pl.pallas_call(
        body, name="loss_head", grid=(s // t,),
        in_specs=[row, _full((1, D_MODEL)), row],
        out_specs=[_full((1, 128)), row, row, _full((1, D_MODEL))],
        out_shape=[jax.ShapeDtypeStruct((1, 128), F32), jax.ShapeDtypeStruct((s, D_MODEL), F32),
                   jax.ShapeDtypeStruct((s, D_MODEL), BF16), jax.ShapeDtypeStruct((1, D_MODEL), F32)],
        compiler_params=_params(("arbitrary",)),
    )(x, g, target)


def _layer_params(small, layer):
    tril = jnp.tril(jnp.ones((CHUNK, CHUNK), bool))
    ws = jnp.where(tril, small["gmlp_w_s"][layer], 0.0)
    bmat = jnp.repeat(small["gmlp_b_s"][layer].T, HEAD_DIM, axis=1)
    scw = jnp.zeros((8, D_GROUP), F32).at[:SHORT_K].set(small["short_conv_w"][layer])
    ccw = jnp.zeros((32, D_GROUP), F32).at[:CONF_K].set(small["conf_conv_w"][layer])
    return dict(vg=small["gmlp_v_g"][layer][None], wt=ws.astype(BF16), wtt=jnp.swapaxes(ws, 1, 2).astype(BF16),
                bmat=bmat, scw=scw, ccw=ccw, lg=small["conf_ln_g"][layer][None], lb=small["conf_ln_b"][layer][None])


def _local_step(x, target, big, small):
    saved = []
    for l in range(DEPTH):
        p = _layer_params(small, l)
        z, hb, q_r, q_t, k_t, v_t = _in_proj(x, small["norm_mix_g"][l][None], big["w_in"], l)
        y_abd = _mixers_fwd(z, p)
        o_t = _attn_fwd(q_r, k_t, v_t)
        x1, yn = _out_proj(x, y_abd, o_t, small["mix_out_g"][l][None], big["w_out"], l)
        x2, pre, h2b = _ffn(x1, small["norm_ffn_g"][l][None], big["w_up"], big["w_down"], l)
        saved.append(dict(p=p, x0=x, z=z, hb=hb, q_r=q_r, q_t=q_t, k_t=k_t, v_t=v_t, o_t=o_t, x1=x1, yn=yn, pre=pre,
                          h2b=h2b))
        x = x2

    loss, dx, dxb, d_final = _loss_head(x, small["final_norm_g"][None], target)

    g = {k: [None] * DEPTH for k in ("w_in", "w_out", "w_up", "w_down", "norm_mix_g", "gmlp_v_g", "gmlp_w_s", "gmlp_b_s",
                                     "short_conv_w", "conf_conv_w", "conf_ln_g", "conf_ln_b", "mix_out_g", "norm_ffn_g")}
    tril = jnp.tril(jnp.ones((CHUNK, CHUNK), bool))
    for l in reversed(range(DEPTH)):
        sv = saved[l]
        p = sv["p"]
        dpre, dh = _ffn_bwd(dxb, sv["pre"], big["w_up"], big["w_down"], l)
        g["w_up"][l], g["w_down"][l] = _ffn_wgrad(sv["h2b"], sv["pre"], dpre, dxb)
        dx1, g["norm_ffn_g"][l], dyn, g["mix_out_g"][l], g["w_out"][l] = _out_proj_bwd(
            sv["x1"], small["norm_ffn_g"][l][None], dh, dx, sv["yn"], small["mix_out_g"][l][None], big["w_out"], l)
        dza, dcb, dcd, do_r, do_t, dsum, dvg, dws, dbm, dscw, dccw, dlg, dlb = _mixers_bwd_a(sv["z"], dyn, sv["o_t"], p)
        dq_t, dk_t, dv_t = _attn_bwd(sv["q_r"], sv["q_t"], sv["k_t"], sv["v_t"], do_r, do_t, dsum)
        dz = _mixers_bwd_b(sv["z"], dza, dcb, dcd, dq_t, dk_t, dv_t, p)
        dx, dxb, g["norm_mix_g"][l] = _in_proj_bwd(sv["x0"], small["norm_mix_g"][l][None], dx1, dz, big["w_in"], l)
        g["w_in"][l] = _in_proj_wgrad(sv["hb"], dz)
        g["gmlp_v_g"][l] = dvg[0]
        g["gmlp_w_s"][l] = jnp.where(tril, dws, 0.0)
        g["gmlp_b_s"][l] = dbm.reshape(CHUNK, N_HEADS, HEAD_DIM).sum(-1).T
        g["short_conv_w"][l] = dscw[:SHORT_K]
        g["conf_conv_w"][l] = dccw[:CONF_K]
        g["conf_ln_g"][l] = dlg[0]
        g["conf_ln_b"][l] = dlb[0]
        g["norm_mix_g"][l] = g["norm_mix_g"][l][0]
        g["mix_out_g"][l] = g["mix_out_g"][l][0]
        g["norm_ffn_g"][l] = g["norm_ffn_g"][l][0]
    grads = {k: v if k in ("w_in", "w_out", "w_up", "w_down") else jnp.stack(v) for k, v in g.items()}
    grads["final_norm_g"] = d_final[0]
    return loss, dx, grads


_ANY = pl.BlockSpec(memory_space=pl.ANY)


def _mesh_place():
    x, y, c = lax.axis_index("x"), lax.axis_index("y"), lax.axis_index("c")
    chips = [(1 - x, y), (x, 1 - y), (1 - x, 1 - y)]
    return x, y, c, 2 * x + y, chips


def _gather_blocks(blocks, whole):
    n, m = len(blocks), len(whole)

    def body(*refs):
        ins, outs = refs[:n + m], refs[n + m:2 * (n + m)]
        ici_send, ici_recv, d2d_send, d2d_recv, local_sems = refs[2 * (n + m):]
        x, y, c, me, chips = _mesh_place()
        blk = [2 * chip[0] + chip[1] for chip in chips]

        def ici(k, r, block, to):
            src = outs[k].at[me, c] if k < n else ins[k]
            dst = outs[k].at[block, c] if k < n else outs[k].at[block]
            return pltpu.make_async_remote_copy(
                src_ref=src, dst_ref=dst, send_sem=ici_send.at[3 * k + r], recv_sem=ici_recv.at[3 * k + r],
                device_id=(to[0], to[1], c), device_id_type=MESH)

        def d2d(k, r, layer):
            part = outs[k].at[blk[r], layer]
            return pltpu.make_async_remote_copy(
                src_ref=part, dst_ref=part, send_sem=d2d_send.at[3 * k + r], recv_sem=d2d_recv.at[3 * k + r],
                device_id=(x, y, 1 - c), device_id_type=MESH)

        local = [pltpu.make_async_copy(ins[k], outs[k].at[me], local_sems.at[k - n]) for k in range(n, n + m)]
        sent = [ici(k, r, me, chip) for k in range(n + m) for r, chip in enumerate(chips)]
        for cp in local + sent:
            cp.start()
        passed = []
        for k in range(n + m):
            for r, chip in enumerate(chips):
                ici(k, r, blk[r], chip).wait_recv()
                if k < n:
                    passed.append(d2d(k, r, c))
                    passed[-1].start()
        for k in range(n):
            for r in range(3):
                d2d(k, r, 1 - c).wait_recv()
        for cp in sent + passed:
            cp.wait_send()
        for cp in local:
            cp.wait()

    arrays = list(blocks) + list(whole)
    return pl.pallas_call(
        body, name="gather_weights",
        in_specs=[_ANY] * (n + m), out_specs=[_ANY] * (n + m),
        out_shape=[jax.ShapeDtypeStruct(b.shape, b.dtype) for b in blocks]
                  + [jax.ShapeDtypeStruct((N_BLK,) + b.shape, b.dtype) for b in whole],
        input_output_aliases={k: k for k in range(n)},
        scratch_shapes=[pltpu.SemaphoreType.DMA((3 * (n + m),)), pltpu.SemaphoreType.DMA((3 * (n + m),)),
                        pltpu.SemaphoreType.DMA((3 * n,)), pltpu.SemaphoreType.DMA((3 * n,)),
                        pltpu.SemaphoreType.DMA((m,))],
    )(*arrays)


def _swap_halves(gs):
    n = len(gs)

    def body(*refs):
        ins, outs, (send_sems, recv_sems) = refs[:n], refs[n:2 * n], refs[2 * n:]
        x, y, c, _, _ = _mesh_place()
        cps = []
        for k in range(n):
            half = ins[k].shape[1] // 2
            cps.append(pltpu.make_async_remote_copy(
                src_ref=ins[k].at[:, pl.ds((1 - c) * half, half), :], dst_ref=outs[k],
                send_sem=send_sems.at[k], recv_sem=recv_sems.at[k], device_id=(x, y, 1 - c), device_id_type=MESH))
        for cp in cps:
            cp.start()
        for cp in cps:
            cp.wait()

    return pl.pallas_call(
        body, name="swap_halves", in_specs=[_ANY] * n, out_specs=[_ANY] * n,
        out_shape=[jax.ShapeDtypeStruct((g.shape[0], g.shape[1] // 2, g.shape[2]), F32) for g in gs],
        scratch_shapes=[pltpu.SemaphoreType.DMA((n,)), pltpu.SemaphoreType.DMA((n,))],
    )(*gs)


def _send_to_owners(sums):
    n = len(sums)

    def body(*refs):
        ins, outs, (send_sems, recv_sems) = refs[:n], refs[n:2 * n], refs[2 * n:]
        x, y, c, me, chips = _mesh_place()

        def remote(k, r, src_block, dst_block, to):
            return pltpu.make_async_remote_copy(
                src_ref=ins[k].at[src_block], dst_ref=outs[k].at[dst_block], send_sem=send_sems.at[3 * k + r],
                recv_sem=recv_sems.at[3 * k + r], device_id=(to[0], to[1], c), device_id_type=MESH)

        sent = [remote(k, r, 2 * chip[0] + chip[1], me, chip) for k in range(n) for r, chip in enumerate(chips)]
        for cp in sent:
            cp.start()
        for k in range(n):
            for r, chip in enumerate(chips):
                remote(k, r, me, 2 * chip[0] + chip[1], chip).wait_recv()
        for cp in sent:
            cp.wait_send()

    return pl.pallas_call(
        body, name="send_to_owners", in_specs=[_ANY] * n, out_specs=[_ANY] * n,
        out_shape=[jax.ShapeDtypeStruct(s.shape, s.dtype) for s in sums],
        scratch_shapes=[pltpu.SemaphoreType.DMA((3 * n,)), pltpu.SemaphoreType.DMA((3 * n,))],
    )(*sums)


def _swap_reduced(fs):
    n = len(fs)

    def body(*refs):
        ins, outs, (send_sems, recv_sems) = refs[:n], refs[n:2 * n], refs[2 * n:]
        x, y, c, _, _ = _mesh_place()
        cps = [pltpu.make_async_remote_copy(src_ref=ins[k], dst_ref=outs[k], send_sem=send_sems.at[k],
                                            recv_sem=recv_sems.at[k], device_id=(x, y, 1 - c), device_id_type=MESH)
               for k in range(n)]
        for cp in cps:
            cp.start()
        for cp in cps:
            cp.wait()

    return pl.pallas_call(
        body, name="swap_reduced", in_specs=[_ANY] * n, out_specs=[_ANY] * n,
        out_shape=[jax.ShapeDtypeStruct(f.shape, F32) for f in fs],
        scratch_shapes=[pltpu.SemaphoreType.DMA((n,)), pltpu.SemaphoreType.DMA((n,))],
    )(*fs)


def _row_tile(rows):
    return min(rows, 256)


def _add_pairs(core, g, other):
    n, half, cols = other.shape
    t = _row_tile(half)
    per_half = half // t

    def body(c_ref, a_ref, b_ref, o_ref):
        o_ref[...] = (a_ref[...] + b_ref[...]).astype(BF16)

    spec = pl.BlockSpec((None, t, cols), lambda i, j, c_ref: (i, j, 0))
    return pl.pallas_call(
        body, name="add_pairs",
        grid_spec=pltpu.PrefetchScalarGridSpec(
            num_scalar_prefetch=1, grid=(n, per_half),
            in_specs=[pl.BlockSpec((None, t, cols), lambda i, j, c_ref: (i, c_ref[0] * per_half + j, 0)), spec],
            out_specs=spec),
        out_shape=jax.ShapeDtypeStruct(other.shape, BF16), compiler_params=_params(("parallel", "parallel")),
    )(core, g, other)


def _add_chips(me, s1, r2):
    _, r, cols = r2.shape
    t = _row_tile(r)

    def body(me_ref, s_ref, r_ref, o_ref):
        own = s_ref[...].astype(F32)
        parts = [jnp.where(me_ref[0] == k, own, r_ref[k].astype(F32)) for k in range(N_BLK)]
        o_ref[...] = ((parts[0] + parts[1]) + parts[2]) + parts[3]

    return pl.pallas_call(
        body, name="add_chips",
        grid_spec=pltpu.PrefetchScalarGridSpec(
            num_scalar_prefetch=1, grid=(r // t,),
            in_specs=[pl.BlockSpec((None, t, cols), lambda i, me_ref: (me_ref[0], i, 0)),
                      pl.BlockSpec((N_BLK, t, cols), lambda i, me_ref: (0, i, 0))],
            out_specs=pl.BlockSpec((t, cols), lambda i, me_ref: (i, 0))),
        out_shape=jax.ShapeDtypeStruct((r, cols), F32), compiler_params=_params(("parallel",)),
    )(me, s1, r2)


def _adamw(core, mine, other, w, m, v, layer, earlier=None):
    half, cols = mine.shape
    t = _row_tile(half)
    per_half = half // t
    c1 = 1.0 - ADAM_B1 ** ADAM_STEP
    c2 = 1.0 - ADAM_B2 ** ADAM_STEP

    def body(c_ref, a_ref, b_ref, w_ref, m_ref, v_ref, *rest):
        g_ref, d_ref, mo_ref, vo_ref = rest[-4:]
        gv = jnp.where(pl.program_id(0) // per_half == c_ref[0], a_ref[...], b_ref[...])
        g_ref[...] = gv
        m_new = ADAM_B1 * m_ref[...] + (1.0 - ADAM_B1) * gv
        v_new = ADAM_B2 * v_ref[...] + (1.0 - ADAM_B2) * (gv * gv)
        mo_ref[...] = m_new
        vo_ref[...] = v_new
        d_ref[...] = -ADAM_LR * ((m_new / c1) / (jnp.sqrt(v_new / c2) + ADAM_EPS) + ADAM_WD * w_ref[...])

    part = pl.BlockSpec((t, cols), lambda i, c_ref: (i % per_half, 0))
    spec = pl.BlockSpec((None, t, cols), lambda i, c_ref: (layer, i, 0))
    kept = [] if earlier is None else list(earlier)
    return pl.pallas_call(
        body, name="adamw",
        grid_spec=pltpu.PrefetchScalarGridSpec(
            num_scalar_prefetch=1, grid=(2 * per_half,),
            in_specs=[part, part, spec, spec, spec] + [_ANY] * len(kept), out_specs=[spec] * 4),
        out_shape=[jax.ShapeDtypeStruct(w.shape, F32)] * 4,
        input_output_aliases={6 + k: k for k in range(len(kept))},
        compiler_params=_params(("parallel",)),
    )(core, mine, other, w, m, v, *kept)


_REPLICATED = ("norm_mix_g", "gmlp_v_g", "gmlp_w_s", "gmlp_b_s", "conf_ln_g", "conf_ln_b", "mix_out_g", "norm_ffn_g",
               "final_norm_g")
_REP_SHAPES = {"norm_mix_g": (DEPTH, D_MODEL), "gmlp_v_g": (DEPTH, D_GROUP), "gmlp_w_s": (DEPTH, N_HEADS, CHUNK, CHUNK),
               "gmlp_b_s": (DEPTH, N_HEADS, CHUNK), "conf_ln_g": (DEPTH, D_GROUP), "conf_ln_b": (DEPTH, D_GROUP),
               "mix_out_g": (DEPTH, D_MODEL), "norm_ffn_g": (DEPTH, D_MODEL), "final_norm_g": (D_MODEL,)}
_BIG = ("w_in", "w_out", "w_up", "w_down")
_CONV_ROWS = 8
_REP_ROWS = 144
_SMALL_ROWS = 160
_CH_BLK = D_GROUP // N_BLK


def _pad_rows(flat, rows):
    pad = rows * D_MODEL - flat.shape[-1]
    flat = jnp.pad(flat, [(0, 0)] * (flat.ndim - 1) + [(0, pad)])
    return flat.reshape(flat.shape[:-1] + (rows, D_MODEL))


def _pack_small(scw, ccw, rep):
    lead = scw.shape[:-3]
    conv = jnp.concatenate([scw.reshape(lead + (-1,)), ccw.reshape(lead + (-1,))], axis=-1)
    flat = jnp.concatenate([rep[k].reshape(-1) for k in _REPLICATED])
    flat = jnp.broadcast_to(flat, lead + flat.shape)
    parts = [_pad_rows(conv, _CONV_ROWS), _pad_rows(flat, _REP_ROWS),
             jnp.zeros(lead + (_SMALL_ROWS - _CONV_ROWS - _REP_ROWS, D_MODEL), F32)]
    return jnp.concatenate(parts, axis=-2)


def _unpack_small(pk):
    out = {}
    conv = pk[:_CONV_ROWS].reshape(-1)
    n_s = DEPTH * SHORT_K * _CH_BLK
    out["short_conv_w"] = conv[:n_s].reshape(DEPTH, SHORT_K, _CH_BLK)
    out["conf_conv_w"] = conv[n_s:n_s + DEPTH * CONF_K * _CH_BLK].reshape(DEPTH, CONF_K, _CH_BLK)
    row = _CONV_ROWS
    flat = pk[row:row + _REP_ROWS].reshape(-1)
    at = 0
    for k in _REPLICATED:
        n = math.prod(_REP_SHAPES[k])
        out[k] = flat[at:at + n].reshape(_REP_SHAPES[k])
        at += n
    return out


def _conv_blocks(w):
    d, k, _ = w.shape
    return w.reshape(d, k, N_BLK, _CH_BLK).transpose(2, 0, 1, 3)


_WEIGHTS = ("norm_mix_g", "w_in", "gmlp_v_g", "gmlp_w_s", "gmlp_b_s", "short_conv_w", "conf_conv_w", "conf_ln_g",
            "conf_ln_b", "mix_out_g", "w_out", "norm_ffn_g", "w_up", "w_down", "final_norm_g")


def kernel(x, norm_mix_g, w_in, gmlp_v_g, gmlp_w_s, gmlp_b_s, short_conv_w, conf_conv_w, conf_ln_g, conf_ln_b, mix_out_g, w_out, norm_ffn_g, w_up, w_down, final_norm_g, loss_target, m_norm_mix_g, m_w_in, m_gmlp_v_g, m_gmlp_w_s, m_gmlp_b_s, m_short_conv_w, m_conf_conv_w, m_conf_ln_g, m_conf_ln_b, m_mix_out_g, m_w_out, m_norm_ffn_g, m_w_up, m_w_down, m_final_norm_g, v_norm_mix_g, v_w_in, v_gmlp_v_g, v_gmlp_w_s, v_gmlp_b_s, v_short_conv_w, v_conf_conv_w, v_conf_ln_g, v_conf_ln_b, v_mix_out_g, v_w_out, v_norm_ffn_g, v_w_up, v_w_down, v_final_norm_g):
    w = dict(norm_mix_g=norm_mix_g, w_in=w_in, gmlp_v_g=gmlp_v_g, gmlp_w_s=gmlp_w_s, gmlp_b_s=gmlp_b_s,
             short_conv_w=short_conv_w, conf_conv_w=conf_conv_w, conf_ln_g=conf_ln_g, conf_ln_b=conf_ln_b,
             mix_out_g=mix_out_g, w_out=w_out, norm_ffn_g=norm_ffn_g, w_up=w_up, w_down=w_down, final_norm_g=final_norm_g)
    m = dict(norm_mix_g=m_norm_mix_g, w_in=m_w_in, gmlp_v_g=m_gmlp_v_g, gmlp_w_s=m_gmlp_w_s, gmlp_b_s=m_gmlp_b_s,
             short_conv_w=m_short_conv_w, conf_conv_w=m_conf_conv_w, conf_ln_g=m_conf_ln_g, conf_ln_b=m_conf_ln_b,
             mix_out_g=m_mix_out_g, w_out=m_w_out, norm_ffn_g=m_norm_ffn_g, w_up=m_w_up, w_down=m_w_down,
             final_norm_g=m_final_norm_g)
    v = dict(norm_mix_g=v_norm_mix_g, w_in=v_w_in, gmlp_v_g=v_gmlp_v_g, gmlp_w_s=v_gmlp_w_s, gmlp_b_s=v_gmlp_b_s,
             short_conv_w=v_short_conv_w, conf_conv_w=v_conf_conv_w, conf_ln_g=v_conf_ln_g, conf_ln_b=v_conf_ln_b,
             mix_out_g=v_mix_out_g, w_out=v_w_out, norm_ffn_g=v_norm_ffn_g, w_up=v_w_up, w_down=v_w_down,
             final_norm_g=v_final_norm_g)
    core = lax.axis_index("c").astype(jnp.int32).reshape(1)
    me = (2 * lax.axis_index("x") + lax.axis_index("y")).astype(jnp.int32).reshape(1)

    conv_mine = _pad_rows(jnp.concatenate([short_conv_w.reshape(-1), conf_conv_w.reshape(-1)]), _CONV_ROWS)
    gathered = _gather_blocks([_cast_into_slot(w[k], me, "cast_" + k) for k in _BIG], [conv_mine])
    big = dict(zip(_BIG, gathered[:4]))
    conv_all = gathered[4].reshape(N_BLK, -1)
    n_s = DEPTH * SHORT_K * _CH_BLK
    scw_all = conv_all[:, :n_s].reshape(N_BLK, DEPTH, SHORT_K, _CH_BLK)
    ccw_all = conv_all[:, n_s:n_s + DEPTH * CONF_K * _CH_BLK].reshape(N_BLK, DEPTH, CONF_K, _CH_BLK)
    small = {k: w[k] for k in _REPLICATED}
    small["short_conv_w"] = scw_all.transpose(1, 2, 0, 3).reshape(DEPTH, SHORT_K, D_GROUP)
    small["conf_conv_w"] = ccw_all.transpose(1, 2, 0, 3).reshape(DEPTH, CONF_K, D_GROUP)

    loss, dx, g = _local_step(x[0], loss_target[0], big, small)

    where = [(k, l) for k in _BIG for l in range(DEPTH)]
    grads = [g[k][l] for k, l in where] + [_pack_small(_conv_blocks(g["short_conv_w"]), _conv_blocks(g["conf_conv_w"]), g)]
    sums = [_add_pairs(core, a, b) for a, b in zip(grads, _swap_halves(grads))]
    mine = [_add_chips(me, s, r) for s, r in zip(sums, _send_to_owners(sums))]
    other = _swap_reduced(mine)

    done = {}
    for n, (k, l) in enumerate(where):
        done[k] = _adamw(core, mine[n], other[n], w[k], m[k], v[k], l, done.get(k))
    small_own = [_pack_small(t["short_conv_w"], t["conf_conv_w"], t)[None] for t in (w, m, v)]
    small_done = [_unpack_small(a[0]) for a in _adamw(core, mine[-1], other[-1], *small_own, 0)]

    outs = [lax.psum(loss[0, 0], ("x", "y", "c")), dx[None]]
    for kind in range(4):
        outs += [done[k][kind] if k in _BIG else small_done[kind][k] for k in _WEIGHTS]
    return tuple(outs)
```

```python
import math

import jax
import jax.numpy as jnp
from jax import lax
from jax.experimental import pallas as pl
from jax.experimental.pallas import tpu as pltpu

F32 = jnp.float32
BF16 = jnp.bfloat16

D_MODEL = 1024
D_GROUP = 256
N_HEADS = 4
HEAD_DIM = 64
CHUNK = 128
D_IN = 2560
N_BLK = 4
W_IN_BLK = D_IN // N_BLK
D_FF_BLK = 1024
DEPTH = 2
EPS = 1e-6
HALO = 32
SHORT_K = 3
CONF_K = 31
ATT_TQ = 256
ATT_TK = 256
ATT_SCALE = 0.125
ATT_DEAD = -104.0
V7X_VMEM_LIMIT = 56 * 1024 * 1024

ADAM_LR, ADAM_B1, ADAM_B2, ADAM_EPS, ADAM_WD, ADAM_STEP = 0.001, 0.9, 0.999, 1e-08, 0.01, 10

MESH = pl.DeviceIdType.MESH


def _params(sem, vmem=None):
    return pltpu.CompilerParams(dimension_semantics=sem, vmem_limit_bytes=vmem)


def _tile(s, t):
    return min(s, t)


def _rsqrt_mean(v):
    return lax.rsqrt(jnp.mean(v * v, axis=-1, keepdims=True) + EPS)


def _sigmoid(v):
    return 1.0 / (1.0 + jnp.exp(-v))


_GELU_C = math.sqrt(2.0 / math.pi)


def _gelu_tanh(v):
    return jnp.tanh(_GELU_C * (v + 0.044715 * (v * v * v)))


def _gelu(v, t):
    return v * (0.5 * (1.0 + t))


def _gelu_grad(v, t):
    return 0.5 * (1.0 + t) + v * (0.5 * (1.0 - t * t) * _GELU_C * (1.0 + 3.0 * 0.044715 * (v * v)))


def _dot(a, b):
    return jnp.dot(a, b, preferred_element_type=F32)


def _dot_nt(a, b):
    return lax.dot_general(a, b, (((1,), (1,)), ((), ())), preferred_element_type=F32)


def _dot_tn(a, b):
    return lax.dot_general(a, b, (((0,), (0,)), ((), ())), preferred_element_type=F32)


def _cast_into_slot(w, me, name):
    n, r, c = w.shape
    tr = _tile(r, 256)

    def body(me_ref, w_ref, o_ref):
        o_ref[...] = w_ref[...].astype(BF16)

    return pl.pallas_call(
        body, name=name,
        grid_spec=pltpu.PrefetchScalarGridSpec(
            num_scalar_prefetch=1, grid=(n, r // tr),
            in_specs=[pl.BlockSpec((None, tr, c), lambda a, b, me_ref: (a, b, 0))],
            out_specs=pl.BlockSpec((None, None, tr, c), lambda a, b, me_ref: (me_ref[0], a, b, 0))),
        out_shape=jax.ShapeDtypeStruct((N_BLK,) + w.shape, BF16),
        compiler_params=_params(("parallel", "parallel")),
    )(me, w)


def _split_heads(xv, rows_ref, cols_ref):
    if rows_ref is not None:
        for h in range(N_HEADS):
            rows_ref[h] = xv[:, h * HEAD_DIM:(h + 1) * HEAD_DIM].astype(BF16)
    if cols_ref is not None:
        xt = xv.T
        for h in range(N_HEADS):
            cols_ref[h] = xt[h * HEAD_DIM:(h + 1) * HEAD_DIM, :].astype(BF16)


def _head_specs(t, s):
    rows = (pl.BlockSpec((N_HEADS, t, HEAD_DIM), lambda i: (0, i, 0)), jax.ShapeDtypeStruct((N_HEADS, s, HEAD_DIM), BF16))
    cols = (pl.BlockSpec((N_HEADS, HEAD_DIM, t), lambda i: (0, 0, i)), jax.ShapeDtypeStruct((N_HEADS, HEAD_DIM, s), BF16))
    return rows, cols


def _in_proj(x, g, w, layer):
    s = x.shape[0]
    t = _tile(s, 512)

    def body(x_ref, g_ref, w_ref, z_ref, h_ref, qr_ref, qt_ref, kt_ref, vt_ref):
        xv = x_ref[...]
        h = (xv * _rsqrt_mean(xv) * g_ref[...]).astype(BF16)
        h_ref[...] = h
        for b in range(N_BLK):
            z_ref[:, b * W_IN_BLK:(b + 1) * W_IN_BLK] = _dot(h, w_ref[b])
        _split_heads(z_ref[:, 1280:1536] * ATT_SCALE, qr_ref, qt_ref)
        _split_heads(z_ref[:, 1536:1792], None, kt_ref)
        _split_heads(z_ref[:, 1792:2048], None, vt_ref)

    rows, cols = _head_specs(t, s)
    return pl.pallas_call(
        body, name="in_proj", grid=(s // t,),
        in_specs=[pl.BlockSpec((t, D_MODEL), lambda i: (i, 0)), _full((1, D_MODEL)),
                  pl.BlockSpec((N_BLK, None, D_MODEL, W_IN_BLK), lambda i: (0, layer, 0, 0))],
        out_specs=[pl.BlockSpec((t, D_IN), lambda i: (i, 0)), pl.BlockSpec((t, D_MODEL), lambda i: (i, 0)),
                   rows[0], cols[0], cols[0], cols[0]],
        out_shape=[jax.ShapeDtypeStruct((s, D_IN), F32), jax.ShapeDtypeStruct((s, D_MODEL), BF16),
                   rows[1], cols[1], cols[1], cols[1]],
        compiler_params=_params(("parallel",), V7X_VMEM_LIMIT),
    )(x, g, w)


def _mix_a_fwd(z_ref, vg, wt_ref, bmat, t):
    zu = z_ref[:, 0:256]
    zv = z_ref[:, 256:512]
    tu = _gelu_tanh(zu)
    tv = _gelu_tanh(zv)
    u = _gelu(zu, tu)
    v = _gelu(zv, tv)
    rv = _rsqrt_mean(v)
    vh = v * rv
    vnb = (vh * vg).astype(BF16)
    head = lax.broadcasted_iota(jnp.int32, (CHUNK, D_GROUP), 1) // HEAD_DIM
    fs = []
    for c in range(t // CHUNK):
        vc = vnb[c * CHUNK:(c + 1) * CHUNK, :]
        fc = bmat
        for h in range(N_HEADS):
            fc = fc + jnp.where(head == h, _dot(wt_ref[h], vc), 0.0)
        fs.append(fc)
    f = jnp.concatenate(fs, axis=0) if len(fs) > 1 else fs[0]
    return (zu, tu), (zv, tv), u, rv, vh, vnb, f


def _windows(ext_ref, sh_ref, t):
    for b in range(1, 8):
        sh_ref[b - 1] = ext_ref[pl.ds(b, HALO + t - 8), :]

    def window(o):
        a, b = divmod(o, 8)
        return ext_ref[pl.ds(8 * a, t), :] if b == 0 else sh_ref[b - 1, pl.ds(8 * a, t), :]

    return window


def _mix_b_fwd(z_ref, zh_ref, first, scw_ref, ext_ref, t):
    gb = z_ref[:, 512:768]
    uh = zh_ref[:, 768:1024] * zh_ref[:, 1024:1280]
    ext_ref[0:HALO, :] = jnp.where(first, 0.0, uh)
    ext_ref[HALO:HALO + t, :] = z_ref[:, 768:1024] * z_ref[:, 1024:1280]
    cv = jnp.zeros((t, D_GROUP), F32)
    for k in range(SHORT_K):
        cv = cv + scw_ref[k:k + 1, :] * ext_ref[pl.ds(HALO - (SHORT_K - 1) + k, t), :]
    return gb, cv


def _mix_d_fwd(z_ref, zh_ref, first, ccw_ref, lg, lb, ext_ref, sh_ref, t):
    hh = zh_ref[:, 2048:2304] * _sigmoid(zh_ref[:, 2304:2560])
    ext_ref[0:HALO, :] = jnp.where(first, 0.0, hh)
    ext_ref[HALO:HALO + t, :] = z_ref[:, 2048:2304] * _sigmoid(z_ref[:, 2304:2560])
    window = _windows(ext_ref, sh_ref, t)
    cv = jnp.zeros((t, D_GROUP), F32)
    for k in range(CONF_K):
        cv = cv + ccw_ref[k:k + 1, :] * window(HALO - (CONF_K - 1) + k)
    xc = cv - jnp.mean(cv, axis=-1, keepdims=True)
    rs = lax.rsqrt(jnp.mean(xc * xc, axis=-1, keepdims=True) + EPS)
    xh = xc * rs
    ln = xh * lg + lb
    return xh, rs, ln, _sigmoid(ln), window


def _mix_specs(t, s):
    per = t // HALO
    return [pl.BlockSpec((t, D_IN), lambda i: (i, 0)),
            pl.BlockSpec((HALO, D_IN), lambda i: (jnp.maximum(i * per - 1, 0), 0))]


def _full(shape):
    return pl.BlockSpec(shape, lambda i: (0,) * len(shape))


def _mixers_fwd(z, p):
    s = z.shape[0]
    t = _tile(s, 256)

    def body(z_ref, zh_ref, vg_ref, wt_ref, bm_ref, scw_ref, ccw_ref, lg_ref, lb_ref, y_ref, eb_ref, ed_ref, sh_ref):
        first = pl.program_id(0) == 0
        _, _, u, _, _, _, f = _mix_a_fwd(z_ref, vg_ref[...], wt_ref, bm_ref[...], t)
        ya = u * f
        y_ref[:, 0:256] = ya * _rsqrt_mean(ya)
        gb, cv = _mix_b_fwd(z_ref, zh_ref, first, scw_ref, eb_ref, t)
        yb = gb * cv
        y_ref[:, 256:512] = yb * _rsqrt_mean(yb)
        _, _, ln, sg, _ = _mix_d_fwd(z_ref, zh_ref, first, ccw_ref, lg_ref[...], lb_ref[...], ed_ref, sh_ref, t)
        yd = ln * sg
        y_ref[:, 512:768] = yd * _rsqrt_mean(yd)

    return pl.pallas_call(
        body, name="mixers_fwd", grid=(s // t,),
        in_specs=_mix_specs(t, s) + [_full((1, D_GROUP)), _full((N_HEADS, CHUNK, CHUNK)), _full((CHUNK, D_GROUP)),
                                     _full((8, D_GROUP)), _full((32, D_GROUP)), _full((1, D_GROUP)), _full((1, D_GROUP))],
        out_specs=pl.BlockSpec((t, 768), lambda i: (i, 0)),
        out_shape=jax.ShapeDtypeStruct((s, 768), F32),
        scratch_shapes=[pltpu.VMEM((HALO + t, D_GROUP), F32), pltpu.VMEM((HALO + t, D_GROUP), F32),
                        pltpu.VMEM((7, HALO + t - 8, D_GROUP), F32)],
        compiler_params=_params(("parallel",)),
    )(z, z, p["vg"], p["wt"], p["bmat"], p["scw"], p["ccw"], p["lg"], p["lb"])


def _mixers_bwd_a(z, dyn, o_t, p):
    s = z.shape[0]
    t = _tile(s, 256)
    n_chunk = t // CHUNK

    def body(z_ref, zh_ref, dyn_ref, ot_ref, vg_ref, wt_ref, wtt_ref, bm_ref, scw_ref, ccw_ref, lg_ref, lb_ref,
             dza_ref, dcb_ref, dcd_ref, dor_ref, dot_ref, ds_ref, dvg_ref, dws_ref, dbm_ref, dscw_ref, dccw_ref, dlg_ref, dlb_ref,
             eb_ref, ed_ref, sh_ref):
        i = pl.program_id(0)
        first = i == 0

        @pl.when(first)
        def _():
            for r in (dvg_ref, dws_ref, dbm_ref, dscw_ref, dccw_ref, dlg_ref, dlb_ref):
                r[...] = jnp.zeros_like(r)

        def rms_bwd(y, dn):
            r = _rsqrt_mean(y)
            yn = y * r
            return r * (dn - yn * jnp.mean(dn * yn, axis=-1, keepdims=True))

        vg = vg_ref[...]
        gelu_u, gelu_v, u, rv, vh, vnb, f = _mix_a_fwd(z_ref, vg, wt_ref, bm_ref[...], t)
        dya = rms_bwd(u * f, dyn_ref[:, 0:256])
        du = dya * f
        df = dya * u
        head = lax.broadcasted_iota(jnp.int32, (CHUNK, D_GROUP), 1) // HEAD_DIM
        dvns = []
        dbm = jnp.zeros((CHUNK, D_GROUP), F32)
        for c in range(n_chunk):
            dfc = df[c * CHUNK:(c + 1) * CHUNK, :]
            vc = vnb[c * CHUNK:(c + 1) * CHUNK, :]
            dbm = dbm + dfc
            dvn = jnp.zeros((CHUNK, D_GROUP), F32)
            for h in range(N_HEADS):
                dfh = jnp.where(head == h, dfc, 0.0).astype(BF16)
                dvn = dvn + _dot(wtt_ref[h], dfh)
                dws_ref[h] += _dot_nt(dfh, vc)
            dvns.append(dvn)
        dbm_ref[...] += dbm
        dvn = jnp.concatenate(dvns, axis=0) if n_chunk > 1 else dvns[0]
        dvg_ref[...] += jnp.sum(dvn * vh, axis=0, keepdims=True)
        dvh = dvn * vg
        dv = rv * (dvh - vh * jnp.mean(dvh * vh, axis=-1, keepdims=True))
        dza_ref[:, 0:256] = (du * _gelu_grad(*gelu_u)).astype(BF16)
        dza_ref[:, 256:512] = (dv * _gelu_grad(*gelu_v)).astype(BF16)

        gb, cv = _mix_b_fwd(z_ref, zh_ref, first, scw_ref, eb_ref, t)
        dyb = rms_bwd(gb * cv, dyn_ref[:, 256:512])
        dza_ref[:, 512:768] = (dyb * cv).astype(BF16)
        dcb = dyb * gb
        dcb_ref[...] = dcb
        for k in range(SHORT_K):
            dscw_ref[k:k + 1, :] += jnp.sum(dcb * eb_ref[pl.ds(HALO - (SHORT_K - 1) + k, t), :], axis=0, keepdims=True)

        lg = lg_ref[...]
        xh, rs, ln, sg, window = _mix_d_fwd(z_ref, zh_ref, first, ccw_ref, lg, lb_ref[...], ed_ref, sh_ref, t)
        dyd = rms_bwd(ln * sg, dyn_ref[:, 768:1024])
        dln = dyd * (sg * (1.0 + ln * (1.0 - sg)))
        dlg_ref[...] += jnp.sum(dln * xh, axis=0, keepdims=True)
        dlb_ref[...] += jnp.sum(dln, axis=0, keepdims=True)
        dxh = dln * lg
        dcd = rs * (dxh - jnp.mean(dxh, axis=-1, keepdims=True) - xh * jnp.mean(dxh * xh, axis=-1, keepdims=True))
        dcd_ref[...] = dcd
        for k in range(CONF_K):
            dccw_ref[k:k + 1, :] += jnp.sum(dcd * window(HALO - (CONF_K - 1) + k), axis=0, keepdims=True)

        o = ot_ref[...].reshape(D_GROUP, t).T
        do = rms_bwd(o, dyn_ref[:, 512:768])
        _split_heads(do, dor_ref, dot_ref)
        prod = do.astype(BF16).astype(F32) * o
        for h in range(N_HEADS):
            ds_ref[h] = jnp.sum(prod[:, h * HEAD_DIM:(h + 1) * HEAD_DIM], axis=1, keepdims=True)

    small = [(1, D_GROUP), (N_HEADS, CHUNK, CHUNK), (CHUNK, D_GROUP), (8, D_GROUP), (32, D_GROUP), (1, D_GROUP), (1, D_GROUP)]
    rows, cols = _head_specs(t, s)
    return pl.pallas_call(
        body, name="mixers_bwd_a", grid=(s // t,),
        in_specs=_mix_specs(t, s) + [pl.BlockSpec((t, D_MODEL), lambda i: (i, 0)),
                                     pl.BlockSpec((N_HEADS, HEAD_DIM, t), lambda i: (0, 0, i)),
                                     _full((1, D_GROUP)), _full((N_HEADS, CHUNK, CHUNK)), _full((N_HEADS, CHUNK, CHUNK)),
                                     _full((CHUNK, D_GROUP)), _full((8, D_GROUP)), _full((32, D_GROUP)),
                                     _full((1, D_GROUP)), _full((1, D_GROUP))],
        out_specs=[pl.BlockSpec((t, 768), lambda i: (i, 0)), pl.BlockSpec((t, D_GROUP), lambda i: (i, 0)),
                   pl.BlockSpec((t, D_GROUP), lambda i: (i, 0)), rows[0], cols[0],
                   pl.BlockSpec((N_HEADS, t, 1), lambda i: (0, i, 0))]
                  + [_full(sh) for sh in small],
        out_shape=[jax.ShapeDtypeStruct((s, 768), BF16), jax.ShapeDtypeStruct((s, D_GROUP), F32),
                   jax.ShapeDtypeStruct((s, D_GROUP), F32), rows[1], cols[1],
                   jax.ShapeDtypeStruct((N_HEADS, s, 1), F32)]
                  + [jax.ShapeDtypeStruct(sh, F32) for sh in small],
        scratch_shapes=[pltpu.VMEM((HALO + t, D_GROUP), F32), pltpu.VMEM((HALO + t, D_GROUP), F32),
                        pltpu.VMEM((7, HALO + t - 8, D_GROUP), F32)],
        compiler_params=_params(("arbitrary",)),
    )(z, z, dyn, o_t, p["vg"], p["wt"], p["wtt"], p["bmat"], p["scw"], p["ccw"], p["lg"], p["lb"])


def _mixers_bwd_b(z, dza, dcb, dcd, dq_t, dk_t, dv_t, p):
    s = z.shape[0]
    t = _tile(s, 256)
    per = t // HALO
    n_halo = s // HALO

    def body(z_ref, dza_ref, dcb_ref, dcbn_ref, dcd_ref, dcdn_ref, dq_ref, dk_ref, dv_ref, scw_ref, ccw_ref,
             dz_ref, eb_ref, ed_ref, sh_ref):
        last = pl.program_id(0) == pl.num_programs(0) - 1
        dz_ref[:, 0:768] = dza_ref[...]
        eb_ref[0:t, :] = dcb_ref[...]
        eb_ref[t:t + HALO, :] = jnp.where(last, 0.0, dcbn_ref[...])
        du = jnp.zeros((t, D_GROUP), F32)
        for k in range(SHORT_K):
            du = du + scw_ref[k:k + 1, :] * eb_ref[pl.ds(SHORT_K - 1 - k, t), :]
        dz_ref[:, 768:1024] = (du * z_ref[:, 1024:1280]).astype(BF16)
        dz_ref[:, 1024:1280] = (du * z_ref[:, 768:1024]).astype(BF16)
        for n, r in enumerate((dq_ref, dk_ref, dv_ref)):
            dz_ref[:, 1280 + 256 * n:1536 + 256 * n] = r[...].reshape(D_GROUP, t).T.astype(BF16)
        ed_ref[0:t, :] = dcd_ref[...]
        ed_ref[t:t + HALO, :] = jnp.where(last, 0.0, dcdn_ref[...])
        window = _windows(ed_ref, sh_ref, t)
        dh = jnp.zeros((t, D_GROUP), F32)
        for k in range(CONF_K):
            dh = dh + ccw_ref[k:k + 1, :] * window(CONF_K - 1 - k)
        a = z_ref[:, 2048:2304]
        sg = _sigmoid(z_ref[:, 2304:2560])
        dz_ref[:, 2048:2304] = (dh * sg).astype(BF16)
        dz_ref[:, 2304:2560] = (dh * a * sg * (1.0 - sg)).astype(BF16)

    nxt = lambda i: (jnp.minimum((i + 1) * per, n_halo - 1), 0)
    tr = pl.BlockSpec((N_HEADS, HEAD_DIM, t), lambda i: (0, 0, i))
    return pl.pallas_call(
        body, name="mixers_bwd_b", grid=(s // t,),
        in_specs=[pl.BlockSpec((t, D_IN), lambda i: (i, 0)), pl.BlockSpec((t, 768), lambda i: (i, 0)),
                  pl.BlockSpec((t, D_GROUP), lambda i: (i, 0)), pl.BlockSpec((HALO, D_GROUP), nxt),
                  pl.BlockSpec((t, D_GROUP), lambda i: (i, 0)), pl.BlockSpec((HALO, D_GROUP), nxt),
                  tr, tr, tr, _full((8, D_GROUP)), _full((32, D_GROUP))],
        out_specs=pl.BlockSpec((t, D_IN), lambda i: (i, 0)),
        out_shape=jax.ShapeDtypeStruct((s, D_IN), BF16),
        scratch_shapes=[pltpu.VMEM((HALO + t, D_GROUP), F32), pltpu.VMEM((HALO + t, D_GROUP), F32),
                        pltpu.VMEM((7, HALO + t - 8, D_GROUP), F32)],
        compiler_params=_params(("parallel",)),
    )(z, dza, dcb, dcb, dcd, dcd, dq_t, dk_t, dv_t, p["scw"], p["ccw"])


def _split_bf16(v):
    hi = v.astype(BF16)
    return hi, (v - hi.astype(F32)).astype(BF16)


def _att_scores(qs, kts, carries, tri, mask):
    zs = [_dot(q, kt) for q, kt in zip(qs, kts)]
    lms, lbs, parts = [], [], []
    for z in zs:
        soft = jnp.log(1.0 + jnp.exp(-jnp.abs(z)))
        lm = -(jnp.maximum(z, 0.0) + soft)
        lbs.append(lm + z)
        if mask is not None:
            lm = jnp.where(mask, lm, 0.0)
        lms.append(lm)
        parts.append(_split_bf16(lm))
    rights = [_dot(hi, tri) + _dot(lo, tri) for hi, lo in parts]
    ws = []
    for lb, right, carry in zip(lbs, rights, carries):
        w = jnp.exp(lb + right + carry)
        ws.append(w if mask is None else jnp.where(mask, w, 0.0))
    return ws, lbs, [jnp.sum(lm, axis=1, keepdims=True) for lm in lms]


def _att_consts(i):
    j_hi = ((i + 1) * ATT_TQ - 1) // ATT_TK
    row = lax.broadcasted_iota(jnp.int32, (ATT_TQ, ATT_TK), 0) + i * ATT_TQ
    col = lax.broadcasted_iota(jnp.int32, (ATT_TQ, ATT_TK), 1) + j_hi * ATT_TK
    r_i = lax.broadcasted_iota(jnp.int32, (ATT_TK, ATT_TK), 0)
    c_i = lax.broadcasted_iota(jnp.int32, (ATT_TK, ATT_TK), 1)
    return j_hi, col < row, r_i, c_i


def _att_alive(j, carries):
    top = carries[0]
    for c in carries[1:]:
        top = jnp.maximum(top, c)
    return jnp.logical_and(j >= 0, jnp.max(top) > ATT_DEAD)


def _attn_fwd(q_r, k_t, v_t):
    s = q_r.shape[1]

    def body(q_ref, kt_ref, vt_ref, o_ref):
        j_hi, mask, r_i, c_i = _att_consts(pl.program_id(0))
        tri = (r_i > c_i).astype(BF16)

        heads = range(N_HEADS)

        def tiles(j, carries, accs, mask):
            cols = pl.ds(pl.multiple_of(j * ATT_TK, ATT_TK), ATT_TK)
            ws, _, tots = _att_scores([q_ref[h] for h in heads], [kt_ref[h, :, cols] for h in heads], carries, tri, mask)
            accs = [acc + _dot_nt(vt_ref[h, :, cols], w.astype(BF16)) for h, acc, w in zip(heads, accs, ws)]
            return [c + t for c, t in zip(carries, tots)], accs

        state = tiles(j_hi, [jnp.zeros((ATT_TQ, 1), F32)] * N_HEADS, [jnp.zeros((HEAD_DIM, ATT_TQ), F32)] * N_HEADS, mask)

        def cond(c):
            return _att_alive(c[0], c[1])

        def step(c):
            return (c[0] - 1,) + tuple(tiles(c[0], c[1], c[2], None))

        _, _, accs = lax.while_loop(cond, step, (j_hi - 1,) + tuple(state))
        for h in heads:
            o_ref[h] = accs[h]

    whole = pl.BlockSpec((N_HEADS, HEAD_DIM, s), lambda i: (0, 0, 0), pipeline_mode=pl.Buffered(1))
    return pl.pallas_call(
        body, name="attn_fwd", grid=(s // ATT_TQ,),
        in_specs=[pl.BlockSpec((N_HEADS, ATT_TQ, HEAD_DIM), lambda i: (0, i, 0)), whole, whole],
        out_specs=pl.BlockSpec((N_HEADS, HEAD_DIM, ATT_TQ), lambda i: (0, 0, i)),
        out_shape=jax.ShapeDtypeStruct((N_HEADS, HEAD_DIM, s), F32),
        compiler_params=_params(("arbitrary",), V7X_VMEM_LIMIT),
    )(q_r, k_t, v_t)


ATT_BWD_HEADS = 2


def _attn_bwd(q_r, q_t, k_t, v_t, do_r, do_t, dsum):
    s = q_r.shape[1]
    hps = ATT_BWD_HEADS

    def body(q_ref, qt_ref, kt_ref, vt_ref, do_ref, dot_ref, ds_ref, dq_ref, dk_ref, dv_ref):
        i = pl.program_id(1)

        @pl.when(i == 0)
        def _():
            dk_ref[...] = jnp.zeros_like(dk_ref)
            dv_ref[...] = jnp.zeros_like(dv_ref)

        j_hi, mask, r_i, c_i = _att_consts(i)
        tri_r = (r_i > c_i).astype(BF16)
        tri_ge = (r_i >= c_i).astype(BF16)

        heads = range(hps)

        def tiles(j, carries, gsums, accs, mask):
            cols = pl.ds(pl.multiple_of(j * ATT_TK, ATT_TK), ATT_TK)
            kts = [kt_ref[h, :, cols] for h in heads]
            das = [_dot(do_ref[h], vt_ref[h, :, cols]) for h in heads]
            ws, lbs, tots = _att_scores([q_ref[h] for h in heads], kts, carries, tri_r, mask)
            wbs = [w.astype(BF16) for w in ws]
            gs = [wb.astype(F32) * da for wb, da in zip(wbs, das)]
            parts = [_split_bf16(g) for g in gs]
            sfx = [_dot(hi, tri_ge) + _dot(lo, tri_ge) for hi, lo in parts]
            for h in heads:
                dv_ref[h, :, cols] += _dot(dot_ref[h], wbs[h])
            dzs = []
            for h in heads:
                left = ds_ref[h] - gsums[h] - sfx[h]
                dz = gs[h] - jnp.exp(lbs[h]) * (gs[h] + left)
                dzs.append((dz if mask is None else jnp.where(mask, dz, 0.0)).astype(BF16))
            for h in heads:
                dk_ref[h, :, cols] += _dot(qt_ref[h], dzs[h])
            accs = [accs[h] + _dot_nt(kts[h], dzs[h]) for h in heads]
            gsums = [gsums[h] + jnp.sum(gs[h], axis=1, keepdims=True) for h in heads]
            return [c + t for c, t in zip(carries, tots)], gsums, accs

        col0 = [jnp.zeros((ATT_TQ, 1), F32)] * hps
        state = tiles(j_hi, col0, col0, [jnp.zeros((HEAD_DIM, ATT_TQ), F32)] * hps, mask)

        def cond(c):
            return _att_alive(c[0], c[1])

        def step(c):
            return (c[0] - 1,) + tuple(tiles(c[0], c[1], c[2], c[3], None))

        _, _, _, accs = lax.while_loop(cond, step, (j_hi - 1,) + tuple(state))
        for h in heads:
            dq_ref[h] = accs[h] * ATT_SCALE

    whole = pl.BlockSpec((hps, HEAD_DIM, s), lambda g, i: (g, 0, 0))
    whole_in = pl.BlockSpec((hps, HEAD_DIM, s), lambda g, i: (g, 0, 0), pipeline_mode=pl.Buffered(1))
    rows = pl.BlockSpec((hps, ATT_TQ, HEAD_DIM), lambda g, i: (g, i, 0))
    cols = pl.BlockSpec((hps, HEAD_DIM, ATT_TQ), lambda g, i: (g, 0, i))
    shape = jax.ShapeDtypeStruct((N_HEADS, HEAD_DIM, s), F32)
    return pl.pallas_call(
        body, name="attn_bwd", grid=(N_HEADS // hps, s // ATT_TQ),
        in_specs=[rows, cols, whole_in, whole_in, rows, cols, pl.BlockSpec((hps, ATT_TQ, 1), lambda g, i: (g, i, 0))],
        out_specs=[cols, whole, whole],
        out_shape=[shape, shape, shape],
        compiler_params=_params(("parallel", "arbitrary"), V7X_VMEM_LIMIT),
    )(q_r, q_t, k_t, v_t, do_r, do_t, dsum)


def _out_proj(x, y_abd, o_t, gain, w, layer):
    s = x.shape[0]
    t = _tile(s, 512)

    def body(x_ref, y_ref, ot_ref, g_ref, w_ref, x1_ref, yn_ref):
        o = ot_ref[...].reshape(D_GROUP, t).T
        yn_ref[:, 0:512] = y_ref[:, 0:512]
        yn_ref[:, 512:768] = o * _rsqrt_mean(o)
        yn_ref[:, 768:1024] = y_ref[:, 512:768]
        yg = (yn_ref[...] * g_ref[...]).astype(BF16)
        acc = _dot(yg[:, 0:256], w_ref[0])
        for b in range(1, N_BLK):
            acc = acc + _dot(yg[:, 256 * b:256 * (b + 1)], w_ref[b])
        x1_ref[...] = x_ref[...] + acc

    return pl.pallas_call(
        body, name="out_proj", grid=(s // t,),
        in_specs=[pl.BlockSpec((t, D_MODEL), lambda i: (i, 0)), pl.BlockSpec((t, 768), lambda i: (i, 0)),
                  pl.BlockSpec((N_HEADS, HEAD_DIM, t), lambda i: (0, 0, i)), _full((1, D_MODEL)),
                  pl.BlockSpec((N_BLK, None, D_GROUP, D_MODEL), lambda i: (0, layer, 0, 0))],
        out_specs=[pl.BlockSpec((t, D_MODEL), lambda i: (i, 0)), pl.BlockSpec((t, D_MODEL), lambda i: (i, 0))],
        out_shape=[jax.ShapeDtypeStruct((s, D_MODEL), F32), jax.ShapeDtypeStruct((s, D_MODEL), F32)],
        compiler_params=_params(("parallel",)),
    )(x, y_abd, o_t, gain, w)


def _out_proj_bwd(x1, g_ffn, dh, dx2, yn, gain, w, layer):
    s = dx2.shape[0]
    t = _tile(s, 512)

    def body(x_ref, gf_ref, dh_ref, dx2_ref, yn_ref, g_ref, w_ref, dx1_ref, dgf_ref, dyn_ref, dg_ref, dw_ref):
        @pl.when(pl.program_id(0) == 0)
        def _():
            dg_ref[...] = jnp.zeros_like(dg_ref)
            dw_ref[...] = jnp.zeros_like(dw_ref)
            dgf_ref[...] = jnp.zeros_like(dgf_ref)

        dx1, dgf = _rms_bwd_rows(x_ref[...], gf_ref[...], dh_ref[...], dx2_ref[...])
        dx1_ref[...] = dx1
        dgf_ref[...] += dgf
        dxb = dx1.astype(BF16)
        g = g_ref[...]
        yn = yn_ref[...]
        yg = (yn * g).astype(BF16)
        for b in range(N_BLK):
            cols = slice(256 * b, 256 * (b + 1))
            dyg = _dot_nt(dxb, w_ref[b])
            dw_ref[b] += _dot_tn(yg[:, cols], dxb)
            dg_ref[:, cols] += jnp.sum(dyg * yn[:, cols], axis=0, keepdims=True)
            dyn_ref[:, cols] = dyg * g[:, cols]

    row = pl.BlockSpec((t, D_MODEL), lambda i: (i, 0))
    vec = _full((1, D_MODEL))
    return pl.pallas_call(
        body, name="out_proj_bwd", grid=(s // t,),
        in_specs=[row, vec, row, row, row, vec, pl.BlockSpec((N_BLK, None, D_GROUP, D_MODEL), lambda i: (0, layer, 0, 0))],
        out_specs=[row, vec, row, vec, _full((N_BLK, D_GROUP, D_MODEL))],
        out_shape=[jax.ShapeDtypeStruct((s, D_MODEL), F32), jax.ShapeDtypeStruct((1, D_MODEL), F32),
                   jax.ShapeDtypeStruct((s, D_MODEL), F32), jax.ShapeDtypeStruct((1, D_MODEL), F32),
                   jax.ShapeDtypeStruct((N_BLK, D_GROUP, D_MODEL), F32)],
        compiler_params=_params(("arbitrary",), V7X_VMEM_LIMIT),
    )(x1, g_ffn, dh, dx2, yn, gain, w)


def _ffn(x, g, w_up, w_down, layer):
    s = x.shape[0]
    t = _tile(s, 1024)

    def body(x_ref, g_ref, wu_ref, wd_ref, x2_ref, p_ref, h_ref):
        @pl.when(pl.program_id(1) == 0)
        def _():
            xv = x_ref[...]
            h_ref[...] = (xv * _rsqrt_mean(xv) * g_ref[...]).astype(BF16)
            x2_ref[...] = xv

        pre = _dot(h_ref[...], wu_ref[...])
        p_ref[...] = pre.astype(BF16)
        a = jnp.maximum(pre, 0.0)
        x2_ref[...] += _dot((a * a).astype(BF16), wd_ref[...])

    wspec = pl.BlockSpec((None, None, D_MODEL, D_FF_BLK), lambda i, j: (j, layer, 0, 0))
    row = pl.BlockSpec((t, D_MODEL), lambda i, j: (i, 0))
    return pl.pallas_call(
        body, name="ffn", grid=(s // t, N_BLK),
        in_specs=[row, pl.BlockSpec((1, D_MODEL), lambda i, j: (0, 0)), wspec, wspec],
        out_specs=[row, pl.BlockSpec((t, D_FF_BLK), lambda i, j: (i, j)), row],
        out_shape=[jax.ShapeDtypeStruct((s, D_MODEL), F32), jax.ShapeDtypeStruct((s, N_BLK * D_FF_BLK), BF16),
                   jax.ShapeDtypeStruct((s, D_MODEL), BF16)],
        compiler_params=_params(("parallel", "arbitrary"), V7X_VMEM_LIMIT),
    )(x, g, w_up, w_down)


def _rms_bwd_rows(xv, g, dh, dres):
    r = _rsqrt_mean(xv)
    xh = xv * r
    dxh = dh * g
    dx = dres + r * (dxh - xh * jnp.mean(dxh * xh, axis=-1, keepdims=True))
    return dx, jnp.sum(dh * xh, axis=0, keepdims=True)


def _ffn_bwd(dxb, p, w_up, w_down, layer):
    s = dxb.shape[0]
    t = _tile(s, 1024)

    def body(dx_ref, p_ref, wu_ref, wd_ref, dp_ref, dh_ref):
        da = _dot_nt(dx_ref[...], wd_ref[...])
        a = jnp.maximum(p_ref[...].astype(F32), 0.0)
        dp = (da * (2.0 * a)).astype(BF16)
        dp_ref[...] = dp
        dh = _dot_nt(dp, wu_ref[...])

        @pl.when(pl.program_id(1) == 0)
        def _():
            dh_ref[...] = dh

        @pl.when(pl.program_id(1) != 0)
        def _():
            dh_ref[...] += dh

    wspec = pl.BlockSpec((None, None, D_MODEL, D_FF_BLK), lambda i, j: (j, layer, 0, 0))
    row = pl.BlockSpec((t, D_MODEL), lambda i, j: (i, 0))
    blk = pl.BlockSpec((t, D_FF_BLK), lambda i, j: (i, j))
    return pl.pallas_call(
        body, name="ffn_bwd", grid=(s // t, N_BLK),
        in_specs=[row, blk, wspec, wspec], out_specs=[blk, row],
        out_shape=[jax.ShapeDtypeStruct((s, N_BLK * D_FF_BLK), BF16), jax.ShapeDtypeStruct((s, D_MODEL), F32)],
        compiler_params=_params(("parallel", "arbitrary"), V7X_VMEM_LIMIT),
    )(dxb, p, w_up, w_down)


def _ffn_wgrad(hb, p, dp, dxb):
    s = hb.shape[0]
    t = _tile(s, 1024)

    def body(h_ref, p_ref, dp_ref, dx_ref, du_ref, dd_ref):
        @pl.when(pl.program_id(1) == 0)
        def _():
            du_ref[...] = jnp.zeros_like(du_ref)
            dd_ref[...] = jnp.zeros_like(dd_ref)

        a = jnp.maximum(p_ref[...].astype(F32), 0.0)
        du_ref[...] += _dot_tn(h_ref[...], dp_ref[...])
        dd_ref[...] += _dot_tn((a * a).astype(BF16), dx_ref[...])

    row = pl.BlockSpec((t, D_MODEL), lambda j, i: (i, 0))
    blk = pl.BlockSpec((t, D_FF_BLK), lambda j, i: (i, j))
    out = pl.BlockSpec((None, D_MODEL, D_FF_BLK), lambda j, i: (j, 0, 0))
    shape = jax.ShapeDtypeStruct((N_BLK, D_MODEL, D_FF_BLK), F32)
    return pl.pallas_call(
        body, name="ffn_wgrad", grid=(N_BLK, s // t),
        in_specs=[row, blk, blk, row], out_specs=[out, out], out_shape=[shape, shape],
        compiler_params=_params(("parallel", "arbitrary"), V7X_VMEM_LIMIT),
    )(hb, p, dp, dxb)


def _in_proj_bwd(x, g, dx1, dz, w, layer):
    s = x.shape[0]
    t = _tile(s, 512)

    def body(x_ref, g_ref, dx1_ref, dz_ref, w_ref, dx0_ref, dxb_ref, dg_ref):
        @pl.when(pl.program_id(0) == 0)
        def _():
            dg_ref[...] = jnp.zeros_like(dg_ref)

        dh = _dot_nt(dz_ref[:, 0:W_IN_BLK], w_ref[0])
        for b in range(1, N_BLK):
            dh = dh + _dot_nt(dz_ref[:, b * W_IN_BLK:(b + 1) * W_IN_BLK], w_ref[b])
        dx, dg = _rms_bwd_rows(x_ref[...], g_ref[...], dh, dx1_ref[...])
        dx0_ref[...] = dx
        dxb_ref[...] = dx.astype(BF16)
        dg_ref[...] += dg

    row = pl.BlockSpec((t, D_MODEL), lambda i: (i, 0))
    return pl.pallas_call(
        body, name="in_proj_bwd", grid=(s // t,),
        in_specs=[row, _full((1, D_MODEL)), row, pl.BlockSpec((t, D_IN), lambda i: (i, 0)),
                  pl.BlockSpec((N_BLK, None, D_MODEL, W_IN_BLK), lambda i: (0, layer, 0, 0))],
        out_specs=[row, row, _full((1, D_MODEL))],
        out_shape=[jax.ShapeDtypeStruct((s, D_MODEL), F32), jax.ShapeDtypeStruct((s, D_MODEL), BF16),
                   jax.ShapeDtypeStruct((1, D_MODEL), F32)],
        compiler_params=_params(("arbitrary",), V7X_VMEM_LIMIT),
    )(x, g, dx1, dz, w)


def _in_proj_wgrad(hb, dz):
    s = hb.shape[0]
    t = _tile(s, 512)

    def body(h_ref, dz_ref, dw_ref):
        @pl.when(pl.program_id(0) == 0)
        def _():
            dw_ref[...] = jnp.zeros_like(dw_ref)

        h = h_ref[...]
        for b in range(N_BLK):
            dw_ref[b] += _dot_tn(h, dz_ref[:, b * W_IN_BLK:(b + 1) * W_IN_BLK])

    return pl.pallas_call(
        body, name="in_proj_wgrad", grid=(s // t,),
        in_specs=[pl.BlockSpec((t, D_MODEL), lambda i: (i, 0)), pl.BlockSpec((t, D_IN), lambda i: (i, 0))],
        out_specs=_full((N_BLK, D_MODEL, W_IN_BLK)),
        out_shape=jax.ShapeDtypeStruct((N_BLK, D_MODEL, W_IN_BLK), F32),
        compiler_params=_params(("arbitrary",), V7X_VMEM_LIMIT),
    )(hb, dz)


def _loss_head(x, g, target):
    s = x.shape[0]
    t = _tile(s, 512)

    def body(x_ref, g_ref, t_ref, l_ref, dx_ref, dxb_ref, dg_ref):
        @pl.when(pl.program_id(0) == 0)
        def _():
            l_ref[...] = jnp.zeros_like(l_ref)
            dg_ref[...] = jnp.zeros_like(dg_ref)

        xv = x_ref[...]
        g = g_ref[...]
        r = _rsqrt_mean(xv)
        xh = xv * r
        err = xh * g - t_ref[...]
        l_ref[...] += 0.5 * jnp.sum(jnp.mean(err * err, axis=-1, keepdims=True), axis=0, keepdims=True)
        dy = err * (1.0 / D_MODEL)
        dg_ref[...] += jnp.sum(dy * xh, axis=0, keepdims=True)
        dxh = dy * g
        dx = r * (dxh - xh * jnp.mean(dxh * xh, axis=-1, keepdims=True))
        dx_ref[...] = dx
        dxb_ref[...] = dx.astype(BF16)

    row = pl.BlockSpec((t, D_MODEL), lambda i: (i, 0))
    return pl.pallas_call(
        body, name="loss_head", grid=(s // t,),
        in_specs=[row, _full((1, D_MODEL)), row],
        out_specs=[_full((1, 128)), row, row, _full((1, D_MODEL))],
        out_shape=[jax.ShapeDtypeStruct((1, 128), F32), jax.ShapeDtypeStruct((s, D_MODEL), F32),
                   jax.ShapeDtypeStruct((s, D_MODEL), BF16), jax.ShapeDtypeStruct((1, D_MODEL), F32)],
        compiler_params=_params(("arbitrary",)),
    )(x, g, target)


def _layer_params(small, layer):
    tril = jnp.tril(jnp.ones((CHUNK, CHUNK), bool))
    ws = jnp.where(tril, small["gmlp_w_s"][layer], 0.0)
    bmat = jnp.repeat(small["gmlp_b_s"][layer].T, HEAD_DIM, axis=1)
    scw = jnp.zeros((8, D_GROUP), F32).at[:SHORT_K].set(small["short_conv_w"][layer])
    ccw = jnp.zeros((32, D_GROUP), F32).at[:CONF_K].set(small["conf_conv_w"][layer])
    return dict(vg=small["gmlp_v_g"][layer][None], wt=ws.astype(BF16), wtt=jnp.swapaxes(ws, 1, 2).astype(BF16),
                bmat=bmat, scw=scw, ccw=ccw, lg=small["conf_ln_g"][layer][None], lb=small["conf_ln_b"][layer][None])


def _local_step(x, target, big, small):
    saved = []
    for l in range(DEPTH):
        p = _layer_params(small, l)
        z, hb, q_r, q_t, k_t, v_t = _in_proj(x, small["norm_mix_g"][l][None], big["w_in"], l)
        y_abd = _mixers_fwd(z, p)
        o_t = _attn_fwd(q_r, k_t, v_t)
        x1, yn = _out_proj(x, y_abd, o_t, small["mix_out_g"][l][None], big["w_out"], l)
        x2, pre, h2b = _ffn(x1, small["norm_ffn_g"][l][None], big["w_up"], big["w_down"], l)
        saved.append(dict(p=p, x0=x, z=z, hb=hb, q_r=q_r, q_t=q_t, k_t=k_t, v_t=v_t, o_t=o_t, x1=x1, yn=yn, pre=pre,
                          h2b=h2b))
        x = x2

    loss, dx, dxb, d_final = _loss_head(x, small["final_norm_g"][None], target)

    g = {k: [None] * DEPTH for k in ("w_in", "w_out", "w_up", "w_down", "norm_mix_g", "gmlp_v_g", "gmlp_w_s", "gmlp_b_s",
                                     "short_conv_w", "conf_conv_w", "conf_ln_g", "conf_ln_b", "mix_out_g", "norm_ffn_g")}
    tril = jnp.tril(jnp.ones((CHUNK, CHUNK), bool))
    for l in reversed(range(DEPTH)):
        sv = saved[l]
        p = sv["p"]
        dpre, dh = _ffn_bwd(dxb, sv["pre"], big["w_up"], big["w_down"], l)
        g["w_up"][l], g["w_down"][l] = _ffn_wgrad(sv["h2b"], sv["pre"], dpre, dxb)
        dx1, g["norm_ffn_g"][l], dyn, g["mix_out_g"][l], g["w_out"][l] = _out_proj_bwd(
            sv["x1"], small["norm_ffn_g"][l][None], dh, dx, sv["yn"], small["mix_out_g"][l][None], big["w_out"], l)
        dza, dcb, dcd, do_r, do_t, dsum, dvg, dws, dbm, dscw, dccw, dlg, dlb = _mixers_bwd_a(sv["z"], dyn, sv["o_t"], p)
        dq_t, dk_t, dv_t = _attn_bwd(sv["q_r"], sv["q_t"], sv["k_t"], sv["v_t"], do_r, do_t, dsum)
        dz = _mixers_bwd_b(sv["z"], dza, dcb, dcd, dq_t, dk_t, dv_t, p)
        dx, dxb, g["norm_mix_g"][l] = _in_proj_bwd(sv["x0"], small["norm_mix_g"][l][None], dx1, dz, big["w_in"], l)
        g["w_in"][l] = _in_proj_wgrad(sv["hb"], dz)
        g["gmlp_v_g"][l] = dvg[0]
        g["gmlp_w_s"][l] = jnp.where(tril, dws, 0.0)
        g["gmlp_b_s"][l] = dbm.reshape(CHUNK, N_HEADS, HEAD_DIM).sum(-1).T
        g["short_conv_w"][l] = dscw[:SHORT_K]
        g["conf_conv_w"][l] = dccw[:CONF_K]
        g["conf_ln_g"][l] = dlg[0]
        g["conf_ln_b"][l] = dlb[0]
        g["norm_mix_g"][l] = g["norm_mix_g"][l][0]
        g["mix_out_g"][l] = g["mix_out_g"][l][0]
        g["norm_ffn_g"][l] = g["norm_ffn_g"][l][0]
    grads = {k: v if k in ("w_in", "w_out", "w_up", "w_down") else jnp.stack(v) for k, v in g.items()}
    grads["final_norm_g"] = d_final[0]
    return loss, dx, grads


_ANY = pl.BlockSpec(memory_space=pl.ANY)


def _mesh_place():
    x, y, c = lax.axis_index("x"), lax.axis_index("y"), lax.axis_index("c")
    chips = [(1 - x, y), (x, 1 - y), (1 - x, 1 - y)]
    return x, y, c, 2 * x + y, chips


def _gather_blocks(blocks, whole):
    n, m = len(blocks), len(whole)

    def body(*refs):
        ins, outs = refs[:n + m], refs[n + m:2 * (n + m)]
        ici_send, ici_recv, d2d_send, d2d_recv, local_sems = refs[2 * (n + m):]
        x, y, c, me, chips = _mesh_place()
        blk = [2 * chip[0] + chip[1] for chip in chips]

        def ici(k, r, block, to):
            src = outs[k].at[me, c] if k < n else ins[k]
            dst = outs[k].at[block, c] if k < n else outs[k].at[block]
            return pltpu.make_async_remote_copy(
                src_ref=src, dst_ref=dst, send_sem=ici_send.at[3 * k + r], recv_sem=ici_recv.at[3 * k + r],
                device_id=(to[0], to[1], c), device_id_type=MESH)

        def d2d(k, r, layer):
            part = outs[k].at[blk[r], layer]
            return pltpu.make_async_remote_copy(
                src_ref=part, dst_ref=part, send_sem=d2d_send.at[3 * k + r], recv_sem=d2d_recv.at[3 * k + r],
                device_id=(x, y, 1 - c), device_id_type=MESH)

        local = [pltpu.make_async_copy(ins[k], outs[k].at[me], local_sems.at[k - n]) for k in range(n, n + m)]
        sent = [ici(k, r, me, chip) for k in range(n + m) for r, chip in enumerate(chips)]
        for cp in local + sent:
            cp.start()
        passed = []
        for k in range(n + m):
            for r, chip in enumerate(chips):
                ici(k, r, blk[r], chip).wait_recv()
                if k < n:
                    passed.append(d2d(k, r, c))
                    passed[-1].start()
        for k in range(n):
            for r in range(3):
                d2d(k, r, 1 - c).wait_recv()
        for cp in sent + passed:
            cp.wait_send()
        for cp in local:
            cp.wait()

    arrays = list(blocks) + list(whole)
    return pl.pallas_call(
        body, name="gather_weights",
        in_specs=[_ANY] * (n + m), out_specs=[_ANY] * (n + m),
        out_shape=[jax.ShapeDtypeStruct(b.shape, b.dtype) for b in blocks]
                  + [jax.ShapeDtypeStruct((N_BLK,) + b.shape, b.dtype) for b in whole],
        input_output_aliases={k: k for k in range(n)},
        scratch_shapes=[pltpu.SemaphoreType.DMA((3 * (n + m),)), pltpu.SemaphoreType.DMA((3 * (n + m),)),
                        pltpu.SemaphoreType.DMA((3 * n,)), pltpu.SemaphoreType.DMA((3 * n,)),
                        pltpu.SemaphoreType.DMA((m,))],
    )(*arrays)


def _swap_halves(gs):
    n = len(gs)

    def body(*refs):
        ins, outs, (send_sems, recv_sems) = refs[:n], refs[n:2 * n], refs[2 * n:]
        x, y, c, _, _ = _mesh_place()
        cps = []
        for k in range(n):
            half = ins[k].shape[1] // 2
            cps.append(pltpu.make_async_remote_copy(
                src_ref=ins[k].at[:, pl.ds((1 - c) * half, half), :], dst_ref=outs[k],
                send_sem=send_sems.at[k], recv_sem=recv_sems.at[k], device_id=(x, y, 1 - c), device_id_type=MESH))
        for cp in cps:
            cp.start()
        for cp in cps:
            cp.wait()

    return pl.pallas_call(
        body, name="swap_halves", in_specs=[_ANY] * n, out_specs=[_ANY] * n,
        out_shape=[jax.ShapeDtypeStruct((g.shape[0], g.shape[1] // 2, g.shape[2]), F32) for g in gs],
        scratch_shapes=[pltpu.SemaphoreType.DMA((n,)), pltpu.SemaphoreType.DMA((n,))],
    )(*gs)


def _send_to_owners(sums):
    n = len(sums)

    def body(*refs):
        ins, outs, (send_sems, recv_sems) = refs[:n], refs[n:2 * n], refs[2 * n:]
        x, y, c, me, chips = _mesh_place()

        def remote(k, r, src_block, dst_block, to):
            return pltpu.make_async_remote_copy(
                src_ref=ins[k].at[src_block], dst_ref=outs[k].at[dst_block], send_sem=send_sems.at[3 * k + r],
                recv_sem=recv_sems.at[3 * k + r], device_id=(to[0], to[1], c), device_id_type=MESH)

        sent = [remote(k, r, 2 * chip[0] + chip[1], me, chip) for k in range(n) for r, chip in enumerate(chips)]
        for cp in sent:
            cp.start()
        for k in range(n):
            for r, chip in enumerate(chips):
                remote(k, r, me, 2 * chip[0] + chip[1], chip).wait_recv()
        for cp in sent:
            cp.wait_send()

    return pl.pallas_call(
        body, name="send_to_owners", in_specs=[_ANY] * n, out_specs=[_ANY] * n,
        out_shape=[jax.ShapeDtypeStruct(s.shape, s.dtype) for s in sums],
        scratch_shapes=[pltpu.SemaphoreType.DMA((3 * n,)), pltpu.SemaphoreType.DMA((3 * n,))],
    )(*sums)


def _swap_reduced(fs):
    n = len(fs)

    def body(*refs):
        ins, outs, (send_sems, recv_sems) = refs[:n], refs[n:2 * n], refs[2 * n:]
        x, y, c, _, _ = _mesh_place()
        cps = [pltpu.make_async_remote_copy(src_ref=ins[k], dst_ref=outs[k], send_sem=send_sems.at[k],
                                            recv_sem=recv_sems.at[k], device_id=(x, y, 1 - c), device_id_type=MESH)
               for k in range(n)]
        for cp in cps:
            cp.start()
        for cp in cps:
            cp.wait()

    return pl.pallas_call(
        body, name="swap_reduced", in_specs=[_ANY] * n, out_specs=[_ANY] * n,
        out_shape=[jax.ShapeDtypeStruct(f.shape, F32) for f in fs],
        scratch_shapes=[pltpu.SemaphoreType.DMA((n,)), pltpu.SemaphoreType.DMA((n,))],
    )(*fs)


def _row_tile(rows):
    return min(rows, 256)


def _add_pairs(core, g, other):
    n, half, cols = other.shape
    t = _row_tile(half)
    per_half = half // t

    def body(c_ref, a_ref, b_ref, o_ref):
        o_ref[...] = (a_ref[...] + b_ref[...]).astype(BF16)

    spec = pl.BlockSpec((None, t, cols), lambda i, j, c_ref: (i, j, 0))
    return pl.pallas_call(
        body, name="add_pairs",
        grid_spec=pltpu.PrefetchScalarGridSpec(
            num_scalar_prefetch=1, grid=(n, per_half),
            in_specs=[pl.BlockSpec((None, t, cols), lambda i, j, c_ref: (i, c_ref[0] * per_half + j, 0)), spec],
            out_specs=spec),
        out_shape=jax.ShapeDtypeStruct(other.shape, BF16), compiler_params=_params(("parallel", "parallel")),
    )(core, g, other)


def _add_chips(me, s1, r2):
    _, r, cols = r2.shape
    t = _row_tile(r)

    def body(me_ref, s_ref, r_ref, o_ref):
        own = s_ref[...].astype(F32)
        parts = [jnp.where(me_ref[0] == k, own, r_ref[k].astype(F32)) for k in range(N_BLK)]
        o_ref[...] = ((parts[0] + parts[1]) + parts[2]) + parts[3]

    return pl.pallas_call(
        body, name="add_chips",
        grid_spec=pltpu.PrefetchScalarGridSpec(
            num_scalar_prefetch=1, grid=(r // t,),
            in_specs=[pl.BlockSpec((None, t, cols), lambda i, me_ref: (me_ref[0], i, 0)),
                      pl.BlockSpec((N_BLK, t, cols), lambda i, me_ref: (0, i, 0))],
            out_specs=pl.BlockSpec((t, cols), lambda i, me_ref: (i, 0))),
        out_shape=jax.ShapeDtypeStruct((r, cols), F32), compiler_params=_params(("parallel",)),
    )(me, s1, r2)


def _adamw(core, mine, other, w, m, v, layer, earlier=None):
    half, cols = mine.shape
    t = _row_tile(half)
    per_half = half // t
    c1 = 1.0 - ADAM_B1 ** ADAM_STEP
    c2 = 1.0 - ADAM_B2 ** ADAM_STEP

    def body(c_ref, a_ref, b_ref, w_ref, m_ref, v_ref, *rest):
        g_ref, d_ref, mo_ref, vo_ref = rest[-4:]
        gv = jnp.where(pl.program_id(0) // per_half == c_ref[0], a_ref[...], b_ref[...])
        g_ref[...] = gv
        m_new = ADAM_B1 * m_ref[...] + (1.0 - ADAM_B1) * gv
        v_new = ADAM_B2 * v_ref[...] + (1.0 - ADAM_B2) * (gv * gv)
        mo_ref[...] = m_new
        vo_ref[...] = v_new
        d_ref[...] = -ADAM_LR * ((m_new / c1) / (jnp.sqrt(v_new / c2) + ADAM_EPS) + ADAM_WD * w_ref[...])

    part = pl.BlockSpec((t, cols), lambda i, c_ref: (i % per_half, 0))
    spec = pl.BlockSpec((None, t, cols), lambda i, c_ref: (layer, i, 0))
    kept = [] if earlier is None else list(earlier)
    return pl.pallas_call(
        body, name="adamw",
        grid_spec=pltpu.PrefetchScalarGridSpec(
            num_scalar_prefetch=1, grid=(2 * per_half,),
            in_specs=[part, part, spec, spec, spec] + [_ANY] * len(kept), out_specs=[spec] * 4),
        out_shape=[jax.ShapeDtypeStruct(w.shape, F32)] * 4,
        input_output_aliases={6 + k: k for k in range(len(kept))},
        compiler_params=_params(("parallel",)),
    )(core, mine, other, w, m, v, *kept)


_REPLICATED = ("norm_mix_g", "gmlp_v_g", "gmlp_w_s", "gmlp_b_s", "conf_ln_g", "conf_ln_b", "mix_out_g", "norm_ffn_g",
               "final_norm_g")
_REP_SHAPES = {"norm_mix_g": (DEPTH, D_MODEL), "gmlp_v_g": (DEPTH, D_GROUP), "gmlp_w_s": (DEPTH, N_HEADS, CHUNK, CHUNK),
               "gmlp_b_s": (DEPTH, N_HEADS, CHUNK), "conf_ln_g": (DEPTH, D_GROUP), "conf_ln_b": (DEPTH, D_GROUP),
               "mix_out_g": (DEPTH, D_MODEL), "norm_ffn_g": (DEPTH, D_MODEL), "final_norm_g": (D_MODEL,)}
_BIG = ("w_in", "w_out", "w_up", "w_down")
_CONV_ROWS = 8
_REP_ROWS = 144
_SMALL_ROWS = 160
_CH_BLK = D_GROUP // N_BLK


def _pad_rows(flat, rows):
    pad = rows * D_MODEL - flat.shape[-1]
    flat = jnp.pad(flat, [(0, 0)] * (flat.ndim - 1) + [(0, pad)])
    return flat.reshape(flat.shape[:-1] + (rows, D_MODEL))


def _pack_small(scw, ccw, rep):
    lead = scw.shape[:-3]
    conv = jnp.concatenate([scw.reshape(lead + (-1,)), ccw.reshape(lead + (-1,))], axis=-1)
    flat = jnp.concatenate([rep[k].reshape(-1) for k in _REPLICATED])
    flat = jnp.broadcast_to(flat, lead + flat.shape)
    parts = [_pad_rows(conv, _CONV_ROWS), _pad_rows(flat, _REP_ROWS),
             jnp.zeros(lead + (_SMALL_ROWS - _CONV_ROWS - _REP_ROWS, D_MODEL), F32)]
    return jnp.concatenate(parts, axis=-2)


def _unpack_small(pk):
    out = {}
    conv = pk[:_CONV_ROWS].reshape(-1)
    n_s = DEPTH * SHORT_K * _CH_BLK
    out["short_conv_w"] = conv[:n_s].reshape(DEPTH, SHORT_K, _CH_BLK)
    out["conf_conv_w"] = conv[n_s:n_s + DEPTH * CONF_K * _CH_BLK].reshape(DEPTH, CONF_K, _CH_BLK)
    row = _CONV_ROWS
    flat = pk[row:row + _REP_ROWS].reshape(-1)
    at = 0
    for k in _REPLICATED:
        n = math.prod(_REP_SHAPES[k])
        out[k] = flat[at:at + n].reshape(_REP_SHAPES[k])
        at += n
    return out


def _conv_blocks(w):
    d, k, _ = w.shape
    return w.reshape(d, k, N_BLK, _CH_BLK).transpose(2, 0, 1, 3)


_WEIGHTS = ("norm_mix_g", "w_in", "gmlp_v_g", "gmlp_w_s", "gmlp_b_s", "short_conv_w", "conf_conv_w", "conf_ln_g",
            "conf_ln_b", "mix_out_g", "w_out", "norm_ffn_g", "w_up", "w_down", "final_norm_g")


def kernel(x, norm_mix_g, w_in, gmlp_v_g, gmlp_w_s, gmlp_b_s, short_conv_w, conf_conv_w, conf_ln_g, conf_ln_b, mix_out_g, w_out, norm_ffn_g, w_up, w_down, final_norm_g, loss_target, m_norm_mix_g, m_w_in, m_gmlp_v_g, m_gmlp_w_s, m_gmlp_b_s, m_short_conv_w, m_conf_conv_w, m_conf_ln_g, m_conf_ln_b, m_mix_out_g, m_w_out, m_norm_ffn_g, m_w_up, m_w_down, m_final_norm_g, v_norm_mix_g, v_w_in, v_gmlp_v_g, v_gmlp_w_s, v_gmlp_b_s, v_short_conv_w, v_conf_conv_w, v_conf_ln_g, v_conf_ln_b, v_mix_out_g, v_w_out, v_norm_ffn_g, v_w_up, v_w_down, v_final_norm_g):
    w = dict(norm_mix_g=norm_mix_g, w_in=w_in, gmlp_v_g=gmlp_v_g, gmlp_w_s=gmlp_w_s, gmlp_b_s=gmlp_b_s,
             short_conv_w=short_conv_w, conf_conv_w=conf_conv_w, conf_ln_g=conf_ln_g, conf_ln_b=conf_ln_b,
             mix_out_g=mix_out_g, w_out=w_out, norm_ffn_g=norm_ffn_g, w_up=w_up, w_down=w_down, final_norm_g=final_norm_g)
    m = dict(norm_mix_g=m_norm_mix_g, w_in=m_w_in, gmlp_v_g=m_gmlp_v_g, gmlp_w_s=m_gmlp_w_s, gmlp_b_s=m_gmlp_b_s,
             short_conv_w=m_short_conv_w, conf_conv_w=m_conf_conv_w, conf_ln_g=m_conf_ln_g, conf_ln_b=m_conf_ln_b,
             mix_out_g=m_mix_out_g, w_out=m_w_out, norm_ffn_g=m_norm_ffn_g, w_up=m_w_up, w_down=m_w_down,
             final_norm_g=m_final_norm_g)
    v = dict(norm_mix_g=v_norm_mix_g, w_in=v_w_in, gmlp_v_g=v_gmlp_v_g, gmlp_w_s=v_gmlp_w_s, gmlp_b_s=v_gmlp_b_s,
             short_conv_w=v_short_conv_w, conf_conv_w=v_conf_conv_w, conf_ln_g=v_conf_ln_g, conf_ln_b=v_conf_ln_b,
             mix_out_g=v_mix_out_g, w_out=v_w_out, norm_ffn_g=v_norm_ffn_g, w_up=v_w_up, w_down=v_w_down,
             final_norm_g=v_final_norm_g)
    core = lax.axis_index("c").astype(jnp.int32).reshape(1)
    me = (2 * lax.axis_index("x") + lax.axis_index("y")).astype(jnp.int32).reshape(1)

    conv_mine = _pad_rows(jnp.concatenate([short_conv_w.reshape(-1), conf_conv_w.reshape(-1)]), _CONV_ROWS)
    gathered = _gather_blocks([_cast_into_slot(w[k], me, "cast_" + k) for k in _BIG], [conv_mine])
    big = dict(zip(_BIG, gathered[:4]))
    conv_all = gathered[4].reshape(N_BLK, -1)
    n_s = DEPTH * SHORT_K * _CH_BLK
    scw_all = conv_all[:, :n_s].reshape(N_BLK, DEPTH, SHORT_K, _CH_BLK)
    ccw_all = conv_all[:, n_s:n_s + DEPTH * CONF_K * _CH_BLK].reshape(N_BLK, DEPTH, CONF_K, _CH_BLK)
    small = {k: w[k] for k in _REPLICATED}
    small["short_conv_w"] = scw_all.transpose(1, 2, 0, 3).reshape(DEPTH, SHORT_K, D_GROUP)
    small["conf_conv_w"] = ccw_all.transpose(1, 2, 0, 3).reshape(DEPTH, CONF_K, D_GROUP)

    loss, dx, g = _local_step(x[0], loss_target[0], big, small)

    where = [(k, l) for k in _BIG for l in range(DEPTH)]
    grads = [g[k][l] for k, l in where] + [_pack_small(_conv_blocks(g["short_conv_w"]), _conv_blocks(g["conf_conv_w"]), g)]
    sums = [_add_pairs(core, a, b) for a, b in zip(grads, _swap_halves(grads))]
    mine = [_add_chips(me, s, r) for s, r in zip(sums, _send_to_owners(sums))]
    other = _swap_reduced(mine)

    done = {}
    for n, (k, l) in enumerate(where):
        done[k] = _adamw(core, mine[n], other[n], w[k], m[k], v[k], l, done.get(k))
    small_own = [_pack_small(t["short_conv_w"], t["conf_conv_w"], t)[None] for t in (w, m, v)]
    small_done = [_unpack_small(a[0]) for a in _adamw(core, mine[-1], other[-1], *small_own, 0)]

    outs = [lax.psum(loss[0, 0], ("x", "y", "c")), dx[None]]
    for kind in range(4):
        outs += [done[k][kind] if k in _BIG else small_done[kind][k] for k in _WEIGHTS]
    return tuple(outs)
```

```python
import math

import jax
import jax.numpy as jnp
from jax import lax
from jax.experimental import pallas as pl
from jax.experimental.pallas import tpu as pltpu

F32 = jnp.float32
BF16 = jnp.bfloat16

D_MODEL = 1024
D_GROUP = 256
N_HEADS = 4
HEAD_DIM = 64
CHUNK = 128
D_IN = 2560
N_BLK = 4
W_IN_BLK = D_IN // N_BLK
D_FF_BLK = 1024
DEPTH = 2
EPS = 1e-6
HALO = 32
SHORT_K = 3
CONF_K = 31
ATT_TQ = 256
ATT_TK = 256
ATT_SCALE = 0.125
ATT_DEAD = -104.0
V7X_VMEM_LIMIT = 56 * 1024 * 1024

ADAM_LR, ADAM_B1, ADAM_B2, ADAM_EPS, ADAM_WD, ADAM_STEP = 0.001, 0.9, 0.999, 1e-08, 0.01, 10

MESH = pl.DeviceIdType.MESH


def _params(sem, vmem=None):
    return pltpu.CompilerParams(dimension_semantics=sem, vmem_limit_bytes=vmem)


def _tile(s, t):
    return min(s, t)


def _rsqrt_mean(v):
    return lax.rsqrt(jnp.mean(v * v, axis=-1, keepdims=True) + EPS)


def _sigmoid(v):
    return 1.0 / (1.0 + jnp.exp(-v))


_GELU_C = math.sqrt(2.0 / math.pi)


def _gelu_tanh(v):
    return jnp.tanh(_GELU_C * (v + 0.044715 * (v * v * v)))


def _gelu(v, t):
    return v * (0.5 * (1.0 + t))


def _gelu_grad(v, t):
    return 0.5 * (1.0 + t) + v * (0.5 * (1.0 - t * t) * _GELU_C * (1.0 + 3.0 * 0.044715 * (v * v)))


def _dot(a, b):
    return jnp.dot(a, b, preferred_element_type=F32)


def _dot_nt(a, b):
    return lax.dot_general(a, b, (((1,), (1,)), ((), ())), preferred_element_type=F32)


def _dot_tn(a, b):
    return lax.dot_general(a, b, (((0,), (0,)), ((), ())), preferred_element_type=F32)


def _cast_into_slot(w, me, name):
    n, r, c = w.shape
    tr = _tile(r, 256)

    def body(me_ref, w_ref, o_ref):
        o_ref[...] = w_ref[...].astype(BF16)

    return pl.pallas_call(
        body, name=name,
        grid_spec=pltpu.PrefetchScalarGridSpec(
            num_scalar_prefetch=1, grid=(n, r // tr),
            in_specs=[pl.BlockSpec((None, tr, c), lambda a, b, me_ref: (a, b, 0))],
            out_specs=pl.BlockSpec((None, None, tr, c), lambda a, b, me_ref: (me_ref[0], a, b, 0))),
        out_shape=jax.ShapeDtypeStruct((N_BLK,) + w.shape, BF16),
        compiler_params=_params(("parallel", "parallel")),
    )(me, w)


def _split_heads(xv, rows_ref, cols_ref):
    if rows_ref is not None:
        for h in range(N_HEADS):
            rows_ref[h] = xv[:, h * HEAD_DIM:(h + 1) * HEAD_DIM].astype(BF16)
    if cols_ref is not None:
        xt = xv.T
        for h in range(N_HEADS):
            cols_ref[h] = xt[h * HEAD_DIM:(h + 1) * HEAD_DIM, :].astype(BF16)


def _head_specs(t, s):
    rows = (pl.BlockSpec((N_HEADS, t, HEAD_DIM), lambda i: (0, i, 0)), jax.ShapeDtypeStruct((N_HEADS, s, HEAD_DIM), BF16))
    cols = (pl.BlockSpec((N_HEADS, HEAD_DIM, t), lambda i: (0, 0, i)), jax.ShapeDtypeStruct((N_HEADS, HEAD_DIM, s), BF16))
    return rows, cols


def _in_proj(x, g, w, layer):
    s = x.shape[0]
    t = _tile(s, 512)

    def body(x_ref, g_ref, w_ref, z_ref, h_ref, qr_ref, qt_ref, kt_ref, vt_ref):
        xv = x_ref[...]
        h = (xv * _rsqrt_mean(xv) * g_ref[...]).astype(BF16)
        h_ref[...] = h
        for b in range(N_BLK):
            z_ref[:, b * W_IN_BLK:(b + 1) * W_IN_BLK] = _dot(h, w_ref[b])
        _split_heads(z_ref[:, 1280:1536] * ATT_SCALE, qr_ref, qt_ref)
        _split_heads(z_ref[:, 1536:1792], None, kt_ref)
        _split_heads(z_ref[:, 1792:2048], None, vt_ref)

    rows, cols = _head_specs(t, s)
    return pl.pallas_call(
        body, name="in_proj", grid=(s // t,),
        in_specs=[pl.BlockSpec((t, D_MODEL), lambda i: (i, 0)), _full((1, D_MODEL)),
                  pl.BlockSpec((N_BLK, None, D_MODEL, W_IN_BLK), lambda i: (0, layer, 0, 0))],
        out_specs=[pl.BlockSpec((t, D_IN), lambda i: (i, 0)), pl.BlockSpec((t, D_MODEL), lambda i: (i, 0)),
                   rows[0], cols[0], cols[0], cols[0]],
        out_shape=[jax.ShapeDtypeStruct((s, D_IN), F32), jax.ShapeDtypeStruct((s, D_MODEL), BF16),
                   rows[1], cols[1], cols[1], cols[1]],
        compiler_params=_params(("parallel",), V7X_VMEM_LIMIT),
    )(x, g, w)


def _mix_a_fwd(z_ref, vg, wt_ref, bmat, t):
    zu = z_ref[:, 0:256]
    zv = z_ref[:, 256:512]
    tu = _gelu_tanh(zu)
    tv = _gelu_tanh(zv)
    u = _gelu(zu, tu)
    v = _gelu(zv, tv)
    rv = _rsqrt_mean(v)
    vh = v * rv
    vnb = (vh * vg).astype(BF16)
    head = lax.broadcasted_iota(jnp.int32, (CHUNK, D_GROUP), 1) // HEAD_DIM
    fs = []
    for c in range(t // CHUNK):
        vc = vnb[c * CHUNK:(c + 1) * CHUNK, :]
        fc = bmat
        for h in range(N_HEADS):
            fc = fc + jnp.where(head == h, _dot(wt_ref[h], vc), 0.0)
        fs.append(fc)
    f = jnp.concatenate(fs, axis=0) if len(fs) > 1 else fs[0]
    return (zu, tu), (zv, tv), u, rv, vh, vnb, f


def _windows(ext_ref, sh_ref, t):
    for b in range(1, 8):
        sh_ref[b - 1] = ext_ref[pl.ds(b, HALO + t - 8), :]

    def window(o):
        a, b = divmod(o, 8)
        return ext_ref[pl.ds(8 * a, t), :] if b == 0 else sh_ref[b - 1, pl.ds(8 * a, t), :]

    return window


def _mix_b_fwd(z_ref, zh_ref, first, scw_ref, ext_ref, t):
    gb = z_ref[:, 512:768]
    uh = zh_ref[:, 768:1024] * zh_ref[:, 1024:1280]
    ext_ref[0:HALO, :] = jnp.where(first, 0.0, uh)
    ext_ref[HALO:HALO + t, :] = z_ref[:, 768:1024] * z_ref[:, 1024:1280]
    cv = jnp.zeros((t, D_GROUP), F32)
    for k in range(SHORT_K):
        cv = cv + scw_ref[k:k + 1, :] * ext_ref[pl.ds(HALO - (SHORT_K - 1) + k, t), :]
    return gb, cv


def _mix_d_fwd(z_ref, zh_ref, first, ccw_ref, lg, lb, ext_ref, sh_ref, t):
    hh = zh_ref[:, 2048:2304] * _sigmoid(zh_ref[:, 2304:2560])
    ext_ref[0:HALO, :] = jnp.where(first, 0.0, hh)
    ext_ref[HALO:HALO + t, :] = z_ref[:, 2048:2304] * _sigmoid(z_ref[:, 2304:2560])
    window = _windows(ext_ref, sh_ref, t)
    cv = jnp.zeros((t, D_GROUP), F32)
    for k in range(CONF_K):
        cv = cv + ccw_ref[k:k + 1, :] * window(HALO - (CONF_K - 1) + k)
    xc = cv - jnp.mean(cv, axis=-1, keepdims=True)
    rs = lax.rsqrt(jnp.mean(xc * xc, axis=-1, keepdims=True) + EPS)
    xh = xc * rs
    ln = xh * lg + lb
    return xh, rs, ln, _sigmoid(ln), window


def _mix_specs(t, s):
    per = t // HALO
    return [pl.BlockSpec((t, D_IN), lambda i: (i, 0)),
            pl.BlockSpec((HALO, D_IN), lambda i: (jnp.maximum(i * per - 1, 0), 0))]


def _full(shape):
    return pl.BlockSpec(shape, lambda i: (0,) * len(shape))


def _mixers_fwd(z, p, bufs=(), parts=()):
    s = z.shape[0]
    t = _tile(s, 256)
    nb = len(bufs)

    def body(z_ref, zh_ref, vg_ref, wt_ref, bm_ref, scw_ref, ccw_ref, lg_ref, lb_ref, *rest):
        y_ref = rest[nb]
        eb_ref, ed_ref, sh_ref = rest[2 * nb + 1:2 * nb + 4]
        if nb:
            _gather_in_steps(rest[nb + 1:2 * nb + 1], parts, rest[2 * nb + 4:], s // t)
        first = pl.program_id(0) == 0
        _, _, u, _, _, _, f = _mix_a_fwd(z_ref, vg_ref[...], wt_ref, bm_ref[...], t)
        ya = u * f
        y_ref[:, 0:256] = ya * _rsqrt_mean(ya)
        gb, cv = _mix_b_fwd(z_ref, zh_ref, first, scw_ref, eb_ref, t)
        yb = gb * cv
        y_ref[:, 256:512] = yb * _rsqrt_mean(yb)
        _, _, ln, sg, _ = _mix_d_fwd(z_ref, zh_ref, first, ccw_ref, lg_ref[...], lb_ref[...], ed_ref, sh_ref, t)
        yd = ln * sg
        y_ref[:, 512:768] = yd * _rsqrt_mean(yd)

    out = pl.pallas_call(
        body, name="mixers_fwd", grid=(s // t,),
        in_specs=_mix_specs(t, s) + [_full((1, D_GROUP)), _full((N_HEADS, CHUNK, CHUNK)), _full((CHUNK, D_GROUP)),
                                     _full((8, D_GROUP)), _full((32, D_GROUP)), _full((1, D_GROUP)), _full((1, D_GROUP))]
                 + [_ANY] * nb,
        out_specs=[pl.BlockSpec((t, 768), lambda i: (i, 0))] + [_ANY] * nb,
        out_shape=[jax.ShapeDtypeStruct((s, 768), F32)] + [jax.ShapeDtypeStruct(b.shape, b.dtype) for b in bufs],
        input_output_aliases={9 + k: 1 + k for k in range(nb)},
        scratch_shapes=[pltpu.VMEM((HALO + t, D_GROUP), F32), pltpu.VMEM((HALO + t, D_GROUP), F32),
                        pltpu.VMEM((7, HALO + t - 8, D_GROUP), F32)] + (_gather_sems(parts) if nb else []),
        compiler_params=_params(("arbitrary",) if nb else ("parallel",)),
    )(z, z, p["vg"], p["wt"], p["bmat"], p["scw"], p["ccw"], p["lg"], p["lb"], *bufs)
    return (out[0], out[1:]) if nb else out[0]


def _mixers_bwd_a(z, dyn, o_t, p):
    s = z.shape[0]
    t = _tile(s, 256)
    n_chunk = t // CHUNK

    def body(z_ref, zh_ref, dyn_ref, ot_ref, vg_ref, wt_ref, wtt_ref, bm_ref, scw_ref, ccw_ref, lg_ref, lb_ref,
             dza_ref, dcb_ref, dcd_ref, dor_ref, dot_ref, ds_ref, dvg_ref, dws_ref, dbm_ref, dscw_ref, dccw_ref, dlg_ref, dlb_ref,
             eb_ref, ed_ref, sh_ref):
        i = pl.program_id(0)
        first = i == 0

        @pl.when(first)
        def _():
            for r in (dvg_ref, dws_ref, dbm_ref, dscw_ref, dccw_ref, dlg_ref, dlb_ref):
                r[...] = jnp.zeros_like(r)

        def rms_bwd(y, dn):
            r = _rsqrt_mean(y)
            yn = y * r
            return r * (dn - yn * jnp.mean(dn * yn, axis=-1, keepdims=True))

        vg = vg_ref[...]
        gelu_u, gelu_v, u, rv, vh, vnb, f = _mix_a_fwd(z_ref, vg, wt_ref, bm_ref[...], t)
        dya = rms_bwd(u * f, dyn_ref[:, 0:256])
        du = dya * f
        df = dya * u
        head = lax.broadcasted_iota(jnp.int32, (CHUNK, D_GROUP), 1) // HEAD_DIM
        dvns = []
        dbm = jnp.zeros((CHUNK, D_GROUP), F32)
        for c in range(n_chunk):
            dfc = df[c * CHUNK:(c + 1) * CHUNK, :]
            vc = vnb[c * CHUNK:(c + 1) * CHUNK, :]
            dbm = dbm + dfc
            dvn = jnp.zeros((CHUNK, D_GROUP), F32)
            for h in range(N_HEADS):
                dfh = jnp.where(head == h, dfc, 0.0).astype(BF16)
                dvn = dvn + _dot(wtt_ref[h], dfh)
                dws_ref[h] += _dot_nt(dfh, vc)
            dvns.append(dvn)
        dbm_ref[...] += dbm
        dvn = jnp.concatenate(dvns, axis=0) if n_chunk > 1 else dvns[0]
        dvg_ref[...] += jnp.sum(dvn * vh, axis=0, keepdims=True)
        dvh = dvn * vg
        dv = rv * (dvh - vh * jnp.mean(dvh * vh, axis=-1, keepdims=True))
        dza_ref[:, 0:256] = (du * _gelu_grad(*gelu_u)).astype(BF16)
        dza_ref[:, 256:512] = (dv * _gelu_grad(*gelu_v)).astype(BF16)

        gb, cv = _mix_b_fwd(z_ref, zh_ref, first, scw_ref, eb_ref, t)
        dyb = rms_bwd(gb * cv, dyn_ref[:, 256:512])
        dza_ref[:, 512:768] = (dyb * cv).astype(BF16)
        dcb = dyb * gb
        dcb_ref[...] = dcb
        for k in range(SHORT_K):
            dscw_ref[k:k + 1, :] += jnp.sum(dcb * eb_ref[pl.ds(HALO - (SHORT_K - 1) + k, t), :], axis=0, keepdims=True)

        lg = lg_ref[...]
        xh, rs, ln, sg, window = _mix_d_fwd(z_ref, zh_ref, first, ccw_ref, lg, lb_ref[...], ed_ref, sh_ref, t)
        dyd = rms_bwd(ln * sg, dyn_ref[:, 768:1024])
        dln = dyd * (sg * (1.0 + ln * (1.0 - sg)))
        dlg_ref[...] += jnp.sum(dln * xh, axis=0, keepdims=True)
        dlb_ref[...] += jnp.sum(dln, axis=0, keepdims=True)
        dxh = dln * lg
        dcd = rs * (dxh - jnp.mean(dxh, axis=-1, keepdims=True) - xh * jnp.mean(dxh * xh, axis=-1, keepdims=True))
        dcd_ref[...] = dcd
        for k in range(CONF_K):
            dccw_ref[k:k + 1, :] += jnp.sum(dcd * window(HALO - (CONF_K - 1) + k), axis=0, keepdims=True)

        o = ot_ref[...].reshape(D_GROUP, t).T
        do = rms_bwd(o, dyn_ref[:, 512:768])
        _split_heads(do, dor_ref, dot_ref)
        prod = do.astype(BF16).astype(F32) * o
        for h in range(N_HEADS):
            ds_ref[h] = jnp.sum(prod[:, h * HEAD_DIM:(h + 1) * HEAD_DIM], axis=1, keepdims=True)

    small = [(1, D_GROUP), (N_HEADS, CHUNK, CHUNK), (CHUNK, D_GROUP), (8, D_GROUP), (32, D_GROUP), (1, D_GROUP), (1, D_GROUP)]
    rows, cols = _head_specs(t, s)
    return pl.pallas_call(
        body, name="mixers_bwd_a", grid=(s // t,),
        in_specs=_mix_specs(t, s) + [pl.BlockSpec((t, D_MODEL), lambda i: (i, 0)),
                                     pl.BlockSpec((N_HEADS, HEAD_DIM, t), lambda i: (0, 0, i)),
                                     _full((1, D_GROUP)), _full((N_HEADS, CHUNK, CHUNK)), _full((N_HEADS, CHUNK, CHUNK)),
                                     _full((CHUNK, D_GROUP)), _full((8, D_GROUP)), _full((32, D_GROUP)),
                                     _full((1, D_GROUP)), _full((1, D_GROUP))],
        out_specs=[pl.BlockSpec((t, 768), lambda i: (i, 0)), pl.BlockSpec((t, D_GROUP), lambda i: (i, 0)),
                   pl.BlockSpec((t, D_GROUP), lambda i: (i, 0)), rows[0], cols[0],
                   pl.BlockSpec((N_HEADS, t, 1), lambda i: (0, i, 0))]
                  + [_full(sh) for sh in small],
        out_shape=[jax.ShapeDtypeStruct((s, 768), BF16), jax.ShapeDtypeStruct((s, D_GROUP), F32),
                   jax.ShapeDtypeStruct((s, D_GROUP), F32), rows[1], cols[1],
                   jax.ShapeDtypeStruct((N_HEADS, s, 1), F32)]
                  + [jax.ShapeDtypeStruct(sh, F32) for sh in small],
        scratch_shapes=[pltpu.VMEM((HALO + t, D_GROUP), F32), pltpu.VMEM((HALO + t, D_GROUP), F32),
                        pltpu.VMEM((7, HALO + t - 8, D_GROUP), F32)],
        compiler_params=_params(("arbitrary",)),
    )(z, z, dyn, o_t, p["vg"], p["wt"], p["wtt"], p["bmat"], p["scw"], p["ccw"], p["lg"], p["lb"])


def _mixers_bwd_b(z, dza, dcb, dcd, dq_t, dk_t, dv_t, p):
    s = z.shape[0]
    t = _tile(s, 256)
    per = t // HALO
    n_halo = s // HALO

    def body(z_ref, dza_ref, dcb_ref, dcbn_ref, dcd_ref, dcdn_ref, dq_ref, dk_ref, dv_ref, scw_ref, ccw_ref,
             dz_ref, eb_ref, ed_ref, sh_ref):
        last = pl.program_id(0) == pl.num_programs(0) - 1
        dz_ref[:, 0:768] = dza_ref[...]
        eb_ref[0:t, :] = dcb_ref[...]
        eb_ref[t:t + HALO, :] = jnp.where(last, 0.0, dcbn_ref[...])
        du = jnp.zeros((t, D_GROUP), F32)
        for k in range(SHORT_K):
            du = du + scw_ref[k:k + 1, :] * eb_ref[pl.ds(SHORT_K - 1 - k, t), :]
        dz_ref[:, 768:1024] = (du * z_ref[:, 1024:1280]).astype(BF16)
        dz_ref[:, 1024:1280] = (du * z_ref[:, 768:1024]).astype(BF16)
        for n, r in enumerate((dq_ref, dk_ref, dv_ref)):
            dz_ref[:, 1280 + 256 * n:1536 + 256 * n] = r[...].reshape(D_GROUP, t).T.astype(BF16)
        ed_ref[0:t, :] = dcd_ref[...]
        ed_ref[t:t + HALO, :] = jnp.where(last, 0.0, dcdn_ref[...])
        window = _windows(ed_ref, sh_ref, t)
        dh = jnp.zeros((t, D_GROUP), F32)
        for k in range(CONF_K):
            dh = dh + ccw_ref[k:k + 1, :] * window(CONF_K - 1 - k)
        a = z_ref[:, 2048:2304]
        sg = _sigmoid(z_ref[:, 2304:2560])
        dz_ref[:, 2048:2304] = (dh * sg).astype(BF16)
        dz_ref[:, 2304:2560] = (dh * a * sg * (1.0 - sg)).astype(BF16)

    nxt = lambda i: (jnp.minimum((i + 1) * per, n_halo - 1), 0)
    tr = pl.BlockSpec((N_HEADS, HEAD_DIM, t), lambda i: (0, 0, i))
    return pl.pallas_call(
        body, name="mixers_bwd_b", grid=(s // t,),
        in_specs=[pl.BlockSpec((t, D_IN), lambda i: (i, 0)), pl.BlockSpec((t, 768), lambda i: (i, 0)),
                  pl.BlockSpec((t, D_GROUP), lambda i: (i, 0)), pl.BlockSpec((HALO, D_GROUP), nxt),
                  pl.BlockSpec((t, D_GROUP), lambda i: (i, 0)), pl.BlockSpec((HALO, D_GROUP), nxt),
                  tr, tr, tr, _full((8, D_GROUP)), _full((32, D_GROUP))],
        out_specs=pl.BlockSpec((t, D_IN), lambda i: (i, 0)),
        out_shape=jax.ShapeDtypeStruct((s, D_IN), BF16),
        scratch_shapes=[pltpu.VMEM((HALO + t, D_GROUP), F32), pltpu.VMEM((HALO + t, D_GROUP), F32),
                        pltpu.VMEM((7, HALO + t - 8, D_GROUP), F32)],
        compiler_params=_params(("parallel",)),
    )(z, dza, dcb, dcb, dcd, dcd, dq_t, dk_t, dv_t, p["scw"], p["ccw"])


def _split_bf16(v):
    hi = v.astype(BF16)
    return hi, (v - hi.astype(F32)).astype(BF16)


def _att_scores(qs, kts, carries, tri, mask):
    zs = [_dot(q, kt) for q, kt in zip(qs, kts)]
    lms, lbs, parts = [], [], []
    for z in zs:
        soft = jnp.log(1.0 + jnp.exp(-jnp.abs(z)))
        lm = -(jnp.maximum(z, 0.0) + soft)
        lbs.append(lm + z)
        if mask is not None:
            lm = jnp.where(mask, lm, 0.0)
        lms.append(lm)
        parts.append(_split_bf16(lm))
    rights = [_dot(hi, tri) + _dot(lo, tri) for hi, lo in parts]
    ws = []
    for lb, right, carry in zip(lbs, rights, carries):
        w = jnp.exp(lb + right + carry)
        ws.append(w if mask is None else jnp.where(mask, w, 0.0))
    return ws, lbs, [jnp.sum(lm, axis=1, keepdims=True) for lm in lms]


def _att_consts(i):
    j_hi = ((i + 1) * ATT_TQ - 1) // ATT_TK
    row = lax.broadcasted_iota(jnp.int32, (ATT_TQ, ATT_TK), 0) + i * ATT_TQ
    col = lax.broadcasted_iota(jnp.int32, (ATT_TQ, ATT_TK), 1) + j_hi * ATT_TK
    r_i = lax.broadcasted_iota(jnp.int32, (ATT_TK, ATT_TK), 0)
    c_i = lax.broadcasted_iota(jnp.int32, (ATT_TK, ATT_TK), 1)
    return j_hi, col < row, r_i, c_i


def _att_alive(j, carries):
    top = carries[0]
    for c in carries[1:]:
        top = jnp.maximum(top, c)
    return jnp.logical_and(j >= 0, jnp.max(top) > ATT_DEAD)


def _attn_fwd(q_r, k_t, v_t, bufs=(), parts=()):
    s = q_r.shape[1]
    nb = len(bufs)

    def body(q_ref, kt_ref, vt_ref, *rest):
        o_ref = rest[nb]
        if nb:
            _gather_in_steps(rest[nb + 1:2 * nb + 1], parts, rest[2 * nb + 1:], s // ATT_TQ)
        j_hi, mask, r_i, c_i = _att_consts(pl.program_id(0))
        tri = (r_i > c_i).astype(BF16)

        heads = range(N_HEADS)

        def tiles(j, carries, accs, mask):
            cols = pl.ds(pl.multiple_of(j * ATT_TK, ATT_TK), ATT_TK)
            ws, _, tots = _att_scores([q_ref[h] for h in heads], [kt_ref[h, :, cols] for h in heads], carries, tri, mask)
            accs = [acc + _dot_nt(vt_ref[h, :, cols], w.astype(BF16)) for h, acc, w in zip(heads, accs, ws)]
            return [c + t for c, t in zip(carries, tots)], accs

        state = tiles(j_hi, [jnp.zeros((ATT_TQ, 1), F32)] * N_HEADS, [jnp.zeros((HEAD_DIM, ATT_TQ), F32)] * N_HEADS, mask)

        def cond(c):
            return _att_alive(c[0], c[1])

        def step(c):
            return (c[0] - 1,) + tuple(tiles(c[0], c[1], c[2], None))

        _, _, accs = lax.while_loop(cond, step, (j_hi - 1,) + tuple(state))
        for h in heads:
            o_ref[h] = accs[h]

    whole = pl.BlockSpec((N_HEADS, HEAD_DIM, s), lambda i: (0, 0, 0), pipeline_mode=pl.Buffered(1))
    out = pl.pallas_call(
        body, name="attn_fwd", grid=(s // ATT_TQ,),
        in_specs=[pl.BlockSpec((N_HEADS, ATT_TQ, HEAD_DIM), lambda i: (0, i, 0)), whole, whole] + [_ANY] * nb,
        out_specs=[pl.BlockSpec((N_HEADS, HEAD_DIM, ATT_TQ), lambda i: (0, 0, i))] + [_ANY] * nb,
        out_shape=[jax.ShapeDtypeStruct((N_HEADS, HEAD_DIM, s), F32)] + [jax.ShapeDtypeStruct(b.shape, b.dtype) for b in bufs],
        input_output_aliases={3 + k: 1 + k for k in range(nb)},
        scratch_shapes=_gather_sems(parts) if nb else [],
        compiler_params=_params(("arbitrary",), V7X_VMEM_LIMIT),
    )(q_r, k_t, v_t, *bufs)
    return (out[0], out[1:]) if nb else out[0]


ATT_BWD_HEADS = 2


def _attn_bwd(q_r, q_t, k_t, v_t, do_r, do_t, dsum):
    s = q_r.shape[1]
    hps = ATT_BWD_HEADS

    def body(q_ref, qt_ref, kt_ref, vt_ref, do_ref, dot_ref, ds_ref, dq_ref, dk_ref, dv_ref):
        i = pl.program_id(1)

        @pl.when(i == 0)
        def _():
            dk_ref[...] = jnp.zeros_like(dk_ref)
            dv_ref[...] = jnp.zeros_like(dv_ref)

        j_hi, mask, r_i, c_i = _att_consts(i)
        tri_r = (r_i > c_i).astype(BF16)
        tri_ge = (r_i >= c_i).astype(BF16)

        heads = range(hps)

        def tiles(j, carries, gsums, accs, mask):
            cols = pl.ds(pl.multiple_of(j * ATT_TK, ATT_TK), ATT_TK)
            kts = [kt_ref[h, :, cols] for h in heads]
            das = [_dot(do_ref[h], vt_ref[h, :, cols]) for h in heads]
            ws, lbs, tots = _att_scores([q_ref[h] for h in heads], kts, carries, tri_r, mask)
            wbs = [w.astype(BF16) for w in ws]
            gs = [wb.astype(F32) * da for wb, da in zip(wbs, das)]
            parts = [_split_bf16(g) for g in gs]
            sfx = [_dot(hi, tri_ge) + _dot(lo, tri_ge) for hi, lo in parts]
            for h in heads:
                dv_ref[h, :, cols] += _dot(dot_ref[h], wbs[h])
            dzs = []
            for h in heads:
                left = ds_ref[h] - gsums[h] - sfx[h]
                dz = gs[h] - jnp.exp(lbs[h]) * (gs[h] + left)
                dzs.append((dz if mask is None else jnp.where(mask, dz, 0.0)).astype(BF16))
            for h in heads:
                dk_ref[h, :, cols] += _dot(qt_ref[h], dzs[h])
            accs = [accs[h] + _dot_nt(kts[h], dzs[h]) for h in heads]
            gsums = [gsums[h] + jnp.sum(gs[h], axis=1, keepdims=True) for h in heads]
            return [c + t for c, t in zip(carries, tots)], gsums, accs

        col0 = [jnp.zeros((ATT_TQ, 1), F32)] * hps
        state = tiles(j_hi, col0, col0, [jnp.zeros((HEAD_DIM, ATT_TQ), F32)] * hps, mask)

        def cond(c):
            return _att_alive(c[0], c[1])

        def step(c):
            return (c[0] - 1,) + tuple(tiles(c[0], c[1], c[2], c[3], None))

        _, _, _, accs = lax.while_loop(cond, step, (j_hi - 1,) + tuple(state))
        for h in heads:
            dq_ref[h] = accs[h] * ATT_SCALE

    whole = pl.BlockSpec((hps, HEAD_DIM, s), lambda g, i: (g, 0, 0))
    whole_in = pl.BlockSpec((hps, HEAD_DIM, s), lambda g, i: (g, 0, 0), pipeline_mode=pl.Buffered(1))
    rows = pl.BlockSpec((hps, ATT_TQ, HEAD_DIM), lambda g, i: (g, i, 0))
    cols = pl.BlockSpec((hps, HEAD_DIM, ATT_TQ), lambda g, i: (g, 0, i))
    shape = jax.ShapeDtypeStruct((N_HEADS, HEAD_DIM, s), F32)
    return pl.pallas_call(
        body, name="attn_bwd", grid=(N_HEADS // hps, s // ATT_TQ),
        in_specs=[rows, cols, whole_in, whole_in, rows, cols, pl.BlockSpec((hps, ATT_TQ, 1), lambda g, i: (g, i, 0))],
        out_specs=[cols, whole, whole],
        out_shape=[shape, shape, shape],
        compiler_params=_params(("parallel", "arbitrary"), V7X_VMEM_LIMIT),
    )(q_r, q_t, k_t, v_t, do_r, do_t, dsum)


def _out_proj(x, y_abd, o_t, gain, w, layer):
    s = x.shape[0]
    t = _tile(s, 512)

    def body(x_ref, y_ref, ot_ref, g_ref, w_ref, x1_ref, yn_ref):
        o = ot_ref[...].reshape(D_GROUP, t).T
        yn_ref[:, 0:512] = y_ref[:, 0:512]
        yn_ref[:, 512:768] = o * _rsqrt_mean(o)
        yn_ref[:, 768:1024] = y_ref[:, 512:768]
        yg = (yn_ref[...] * g_ref[...]).astype(BF16)
        acc = _dot(yg[:, 0:256], w_ref[0])
        for b in range(1, N_BLK):
            acc = acc + _dot(yg[:, 256 * b:256 * (b + 1)], w_ref[b])
        x1_ref[...] = x_ref[...] + acc

    return pl.pallas_call(
        body, name="out_proj", grid=(s // t,),
        in_specs=[pl.BlockSpec((t, D_MODEL), lambda i: (i, 0)), pl.BlockSpec((t, 768), lambda i: (i, 0)),
                  pl.BlockSpec((N_HEADS, HEAD_DIM, t), lambda i: (0, 0, i)), _full((1, D_MODEL)),
                  pl.BlockSpec((N_BLK, None, D_GROUP, D_MODEL), lambda i: (0, layer, 0, 0))],
        out_specs=[pl.BlockSpec((t, D_MODEL), lambda i: (i, 0)), pl.BlockSpec((t, D_MODEL), lambda i: (i, 0))],
        out_shape=[jax.ShapeDtypeStruct((s, D_MODEL), F32), jax.ShapeDtypeStruct((s, D_MODEL), F32)],
        compiler_params=_params(("parallel",)),
    )(x, y_abd, o_t, gain, w)


def _out_proj_bwd(x1, g_ffn, dh, dx2, yn, gain, w, layer):
    s = dx2.shape[0]
    t = _tile(s, 512)

    def body(x_ref, gf_ref, dh_ref, dx2_ref, yn_ref, g_ref, w_ref, dx1_ref, dgf_ref, dyn_ref, dg_ref, dw_ref):
        @pl.when(pl.program_id(0) == 0)
        def _():
            dg_ref[...] = jnp.zeros_like(dg_ref)
            dw_ref[...] = jnp.zeros_like(dw_ref)
            dgf_ref[...] = jnp.zeros_like(dgf_ref)

        dx1, dgf = _rms_bwd_rows(x_ref[...], gf_ref[...], dh_ref[...], dx2_ref[...])
        dx1_ref[...] = dx1
        dgf_ref[...] += dgf
        dxb = dx1.astype(BF16)
        g = g_ref[...]
        yn = yn_ref[...]
        yg = (yn * g).astype(BF16)
        for b in range(N_BLK):
            cols = slice(256 * b, 256 * (b + 1))
            dyg = _dot_nt(dxb, w_ref[b])
            dw_ref[b] += _dot_tn(yg[:, cols], dxb)
            dg_ref[:, cols] += jnp.sum(dyg * yn[:, cols], axis=0, keepdims=True)
            dyn_ref[:, cols] = dyg * g[:, cols]

    row = pl.BlockSpec((t, D_MODEL), lambda i: (i, 0))
    vec = _full((1, D_MODEL))
    return pl.pallas_call(
        body, name="out_proj_bwd", grid=(s // t,),
        in_specs=[row, vec, row, row, row, vec, pl.BlockSpec((N_BLK, None, D_GROUP, D_MODEL), lambda i: (0, layer, 0, 0))],
        out_specs=[row, vec, row, vec, _full((N_BLK, D_GROUP, D_MODEL))],
        out_shape=[jax.ShapeDtypeStruct((s, D_MODEL), F32), jax.ShapeDtypeStruct((1, D_MODEL), F32),
                   jax.ShapeDtypeStruct((s, D_MODEL), F32), jax.ShapeDtypeStruct((1, D_MODEL), F32),
                   jax.ShapeDtypeStruct((N_BLK, D_GROUP, D_MODEL), F32)],
        compiler_params=_params(("arbitrary",), V7X_VMEM_LIMIT),
    )(x1, g_ffn, dh, dx2, yn, gain, w)


def _ffn(x, g, w_up, w_down, layer):
    s = x.shape[0]
    t = _tile(s, 1024)

    def body(x_ref, g_ref, wu_ref, wd_ref, x2_ref, p_ref, h_ref):
        @pl.when(pl.program_id(1) == 0)
        def _():
            xv = x_ref[...]
            h_ref[...] = (xv * _rsqrt_mean(xv) * g_ref[...]).astype(BF16)
            x2_ref[...] = xv

        pre = _dot(h_ref[...], wu_ref[...])
        p_ref[...] = pre.astype(BF16)
        a = jnp.maximum(pre, 0.0)
        x2_ref[...] += _dot((a * a).astype(BF16), wd_ref[...])

    wspec = pl.BlockSpec((None, None, D_MODEL, D_FF_BLK), lambda i, j: (j, layer, 0, 0))
    row = pl.BlockSpec((t, D_MODEL), lambda i, j: (i, 0))
    return pl.pallas_call(
        body, name="ffn", grid=(s // t, N_BLK),
        in_specs=[row, pl.BlockSpec((1, D_MODEL), lambda i, j: (0, 0)), wspec, wspec],
        out_specs=[row, pl.BlockSpec((t, D_FF_BLK), lambda i, j: (i, j)), row],
        out_shape=[jax.ShapeDtypeStruct((s, D_MODEL), F32), jax.ShapeDtypeStruct((s, N_BLK * D_FF_BLK), BF16),
                   jax.ShapeDtypeStruct((s, D_MODEL), BF16)],
        compiler_params=_params(("parallel", "arbitrary"), V7X_VMEM_LIMIT),
    )(x, g, w_up, w_down)


def _rms_bwd_rows(xv, g, dh, dres):
    r = _rsqrt_mean(xv)
    xh = xv * r
    dxh = dh * g
    dx = dres + r * (dxh - xh * jnp.mean(dxh * xh, axis=-1, keepdims=True))
    return dx, jnp.sum(dh * xh, axis=0, keepdims=True)


def _ffn_bwd(dxb, p, w_up, w_down, layer):
    s = dxb.shape[0]
    t = _tile(s, 1024)

    def body(dx_ref, p_ref, wu_ref, wd_ref, dp_ref, dh_ref):
        da = _dot_nt(dx_ref[...], wd_ref[...])
        a = jnp.maximum(p_ref[...].astype(F32), 0.0)
        dp = (da * (2.0 * a)).astype(BF16)
        dp_ref[...] = dp
        dh = _dot_nt(dp, wu_ref[...])

        @pl.when(pl.program_id(1) == 0)
        def _():
            dh_ref[...] = dh

        @pl.when(pl.program_id(1) != 0)
        def _():
            dh_ref[...] += dh

    wspec = pl.BlockSpec((None, None, D_MODEL, D_FF_BLK), lambda i, j: (j, layer, 0, 0))
    row = pl.BlockSpec((t, D_MODEL), lambda i, j: (i, 0))
    blk = pl.BlockSpec((t, D_FF_BLK), lambda i, j: (i, j))
    return pl.pallas_call(
        body, name="ffn_bwd", grid=(s // t, N_BLK),
        in_specs=[row, blk, wspec, wspec], out_specs=[blk, row],
        out_shape=[jax.ShapeDtypeStruct((s, N_BLK * D_FF_BLK), BF16), jax.ShapeDtypeStruct((s, D_MODEL), F32)],
        compiler_params=_params(("parallel", "arbitrary"), V7X_VMEM_LIMIT),
    )(dxb, p, w_up, w_down)


def _ffn_wgrad(hb, p, dp, dxb):
    s = hb.shape[0]
    t = _tile(s, 1024)

    def body(h_ref, p_ref, dp_ref, dx_ref, du_ref, dd_ref):
        @pl.when(pl.program_id(1) == 0)
        def _():
            du_ref[...] = jnp.zeros_like(du_ref)
            dd_ref[...] = jnp.zeros_like(dd_ref)

        a = jnp.maximum(p_ref[...].astype(F32), 0.0)
        du_ref[...] += _dot_tn(h_ref[...], dp_ref[...])
        dd_ref[...] += _dot_tn((a * a).astype(BF16), dx_ref[...])

    row = pl.BlockSpec((t, D_MODEL), lambda j, i: (i, 0))
    blk = pl.BlockSpec((t, D_FF_BLK), lambda j, i: (i, j))
    out = pl.BlockSpec((None, D_MODEL, D_FF_BLK), lambda j, i: (j, 0, 0))
    shape = jax.ShapeDtypeStruct((N_BLK, D_MODEL, D_FF_BLK), F32)
    return pl.pallas_call(
        body, name="ffn_wgrad", grid=(N_BLK, s // t),
        in_specs=[row, blk, blk, row], out_specs=[out, out], out_shape=[shape, shape],
        compiler_params=_params(("parallel", "arbitrary"), V7X_VMEM_LIMIT),
    )(hb, p, dp, dxb)


def _in_proj_bwd(x, g, dx1, dz, w, layer):
    s = x.shape[0]
    t = _tile(s, 512)

    def body(x_ref, g_ref, dx1_ref, dz_ref, w_ref, dx0_ref, dxb_ref, dg_ref):
        @pl.when(pl.program_id(0) == 0)
        def _():
            dg_ref[...] = jnp.zeros_like(dg_ref)

        dh = _dot_nt(dz_ref[:, 0:W_IN_BLK], w_ref[0])
        for b in range(1, N_BLK):
            dh = dh + _dot_nt(dz_ref[:, b * W_IN_BLK:(b + 1) * W_IN_BLK], w_ref[b])
        dx, dg = _rms_bwd_rows(x_ref[...], g_ref[...], dh, dx1_ref[...])
        dx0_ref[...] = dx
        dxb_ref[...] = dx.astype(BF16)
        dg_ref[...] += dg

    row = pl.BlockSpec((t, D_MODEL), lambda i: (i, 0))
    return pl.pallas_call(
        body, name="in_proj_bwd", grid=(s // t,),
        in_specs=[row, _full((1, D_MODEL)), row, pl.BlockSpec((t, D_IN), lambda i: (i, 0)),
                  pl.BlockSpec((N_BLK, None, D_MODEL, W_IN_BLK), lambda i: (0, layer, 0, 0))],
        out_specs=[row, row, _full((1, D_MODEL))],
        out_shape=[jax.ShapeDtypeStruct((s, D_MODEL), F32), jax.ShapeDtypeStruct((s, D_MODEL), BF16),
                   jax.ShapeDtypeStruct((1, D_MODEL), F32)],
        compiler_params=_params(("arbitrary",), V7X_VMEM_LIMIT),
    )(x, g, dx1, dz, w)


def _in_proj_wgrad(hb, dz):
    s = hb.shape[0]
    t = _tile(s, 512)

    def body(h_ref, dz_ref, dw_ref):
        @pl.when(pl.program_id(0) == 0)
        def _():
            dw_ref[...] = jnp.zeros_like(dw_ref)

        h = h_ref[...]
        for b in range(N_BLK):
            dw_ref[b] += _dot_tn(h, dz_ref[:, b * W_IN_BLK:(b + 1) * W_IN_BLK])

    return pl.pallas_call(
        body, name="in_proj_wgrad", grid=(s // t,),
        in_specs=[pl.BlockSpec((t, D_MODEL), lambda i: (i, 0)), pl.BlockSpec((t, D_IN), lambda i: (i, 0))],
        out_specs=_full((N_BLK, D_MODEL, W_IN_BLK)),
        out_shape=jax.ShapeDtypeStruct((N_BLK, D_MODEL, W_IN_BLK), F32),
        compiler_params=_params(("arbitrary",), V7X_VMEM_LIMIT),
    )(hb, dz)


def _loss_head(x, g, target):
    s = x.shape[0]
    t = _tile(s, 512)

    def body(x_ref, g_ref, t_ref, l_ref, dx_ref, dxb_ref, dg_ref):
        @pl.when(pl.program_id(0) == 0)
        def _():
            l_ref[...] = jnp.zeros_like(l_ref)
            dg_ref[...] = jnp.zeros_like(dg_ref)

        xv = x_ref[...]
        g = g_ref[...]
        r = _rsqrt_mean(xv)
        xh = xv * r
        err = xh * g - t_ref[...]
        l_ref[...] += 0.5 * jnp.sum(jnp.mean(err * err, axis=-1, keepdims=True), axis=0, keepdims=True)
        dy = err * (1.0 / D_MODEL)
        dg_ref[...] += jnp.sum(dy * xh, axis=0, keepdims=True)
        dxh = dy * g
        dx = r * (dxh - xh * jnp.mean(dxh * xh, axis=-1, keepdims=True))
        dx_ref[...] = dx
        dxb_ref[...] = dx.astype(BF16)

    row = pl.BlockSpec((t, D_MODEL), lambda i: (i, 0))
    return pl.pallas_call(
        body, name="loss_head", grid=(s // t,),
        in_specs=[row, _full((1, D_MODEL)), row],
        out_specs=[_full((1, 128)), row, row, _full((1, D_MODEL))],
        out_shape=[jax.ShapeDtypeStruct((1, 128), F32), jax.ShapeDtypeStruct((s, D_MODEL), F32),
                   jax.ShapeDtypeStruct((s, D_MODEL), BF16), jax.ShapeDtypeStruct((1, D_MODEL), F32)],
        compiler_params=_params(("arbitrary",)),
    )(x, g, target)


def _layer_params(small, layer):
    tril = jnp.tril(jnp.ones((CHUNK, CHUNK), bool))
    ws = jnp.where(tril, small["gmlp_w_s"][layer], 0.0)
    bmat = jnp.repeat(small["gmlp_b_s"][layer].T, HEAD_DIM, axis=1)
    scw = jnp.zeros((8, D_GROUP), F32).at[:SHORT_K].set(small["short_conv_w"][layer])
    ccw = jnp.zeros((32, D_GROUP), F32).at[:CONF_K].set(small["conf_conv_w"][layer])
    return dict(vg=small["gmlp_v_g"][layer][None], wt=ws.astype(BF16), wtt=jnp.swapaxes(ws, 1, 2).astype(BF16),
                bmat=bmat, scw=scw, ccw=ccw, lg=small["conf_ln_g"][layer][None], lb=small["conf_ln_b"][layer][None])


def _local_step(x, target, big, small, gather_pending=False):
    saved = []
    for l in range(DEPTH):
        p = _layer_params(small, l)
        z, hb, q_r, q_t, k_t, v_t = _in_proj(x, small["norm_mix_g"][l][None], big["w_in"], l)
        if gather_pending and l == 0:
            late = ("w_out", "w_up", "w_down")
            y_abd, filled = _mixers_fwd(z, p, [big[k] for k in late], [(n, 0) for n in range(len(late))])
            big = {**big, **dict(zip(late, filled))}
            o_t, filled = _attn_fwd(q_r, k_t, v_t, [big[k] for k in _BIG], [(n, 1) for n in range(len(_BIG))])
            big = dict(zip(_BIG, filled))
        else:
            y_abd = _mixers_fwd(z, p)
            o_t = _attn_fwd(q_r, k_t, v_t)
        x1, yn = _out_proj(x, y_abd, o_t, small["mix_out_g"][l][None], big["w_out"], l)
        x2, pre, h2b = _ffn(x1, small["norm_ffn_g"][l][None], big["w_up"], big["w_down"], l)
        saved.append(dict(p=p, x0=x, z=z, hb=hb, q_r=q_r, q_t=q_t, k_t=k_t, v_t=v_t, o_t=o_t, x1=x1, yn=yn, pre=pre,
                          h2b=h2b))
        x = x2

    loss, dx, dxb, d_final = _loss_head(x, small["final_norm_g"][None], target)

    g = {k: [None] * DEPTH for k in ("w_in", "w_out", "w_up", "w_down", "norm_mix_g", "gmlp_v_g", "gmlp_w_s", "gmlp_b_s",
                                     "short_conv_w", "conf_conv_w", "conf_ln_g", "conf_ln_b", "mix_out_g", "norm_ffn_g")}
    tril = jnp.tril(jnp.ones((CHUNK, CHUNK), bool))
    for l in reversed(range(DEPTH)):
        sv = saved[l]
        p = sv["p"]
        dpre, dh = _ffn_bwd(dxb, sv["pre"], big["w_up"], big["w_down"], l)
        g["w_up"][l], g["w_down"][l] = _ffn_wgrad(sv["h2b"], sv["pre"], dpre, dxb)
        dx1, g["norm_ffn_g"][l], dyn, g["mix_out_g"][l], g["w_out"][l] = _out_proj_bwd(
            sv["x1"], small["norm_ffn_g"][l][None], dh, dx, sv["yn"], small["mix_out_g"][l][None], big["w_out"], l)
        dza, dcb, dcd, do_r, do_t, dsum, dvg, dws, dbm, dscw, dccw, dlg, dlb = _mixers_bwd_a(sv["z"], dyn, sv["o_t"], p)
        dq_t, dk_t, dv_t = _attn_bwd(sv["q_r"], sv["q_t"], sv["k_t"], sv["v_t"], do_r, do_t, dsum)
        dz = _mixers_bwd_b(sv["z"], dza, dcb, dcd, dq_t, dk_t, dv_t, p)
        dx, dxb, g["norm_mix_g"][l] = _in_proj_bwd(sv["x0"], small["norm_mix_g"][l][None], dx1, dz, big["w_in"], l)
        g["w_in"][l] = _in_proj_wgrad(sv["hb"], dz)
        g["gmlp_v_g"][l] = dvg[0]
        g["gmlp_w_s"][l] = jnp.where(tril, dws, 0.0)
        g["gmlp_b_s"][l] = dbm.reshape(CHUNK, N_HEADS, HEAD_DIM).sum(-1).T
        g["short_conv_w"][l] = dscw[:SHORT_K]
        g["conf_conv_w"][l] = dccw[:CONF_K]
        g["conf_ln_g"][l] = dlg[0]
        g["conf_ln_b"][l] = dlb[0]
        g["norm_mix_g"][l] = g["norm_mix_g"][l][0]
        g["mix_out_g"][l] = g["mix_out_g"][l][0]
        g["norm_ffn_g"][l] = g["norm_ffn_g"][l][0]
    grads = {k: v if k in ("w_in", "w_out", "w_up", "w_down") else jnp.stack(v) for k, v in g.items()}
    grads["final_norm_g"] = d_final[0]
    return loss, dx, grads


_ANY = pl.BlockSpec(memory_space=pl.ANY)


def _mesh_place():
    x, y, c = lax.axis_index("x"), lax.axis_index("y"), lax.axis_index("c")
    chips = [(1 - x, y), (x, 1 - y), (1 - x, 1 - y)]
    return x, y, c, 2 * x + y, chips


def _gather_stages(bufs, parts, sems):
    ici_send, ici_recv, d2d_send, d2d_recv = sems
    x, y, c, me, chips = _mesh_place()
    blk = [2 * chip[0] + chip[1] for chip in chips]
    pairs = [(p, r) for p in range(len(parts)) for r in range(3)]

    def rows(p, block, half_of):
        k, layer = parts[p]
        half = bufs[k].shape[2] // 2
        return bufs[k].at[block, layer, pl.ds(half_of * half, half), :]

    def ici(p, r, block):
        return pltpu.make_async_remote_copy(
            src_ref=rows(p, me, c), dst_ref=rows(p, block, c), send_sem=ici_send.at[3 * p + r],
            recv_sem=ici_recv.at[3 * p + r], device_id=(chips[r][0], chips[r][1], c), device_id_type=MESH)

    def d2d(p, r, half_of):
        part = rows(p, blk[r], half_of)
        return pltpu.make_async_remote_copy(
            src_ref=part, dst_ref=part, send_sem=d2d_send.at[3 * p + r], recv_sem=d2d_recv.at[3 * p + r],
            device_id=(x, y, 1 - c), device_id_type=MESH)

    def start():
        for p, r in pairs:
            ici(p, r, me).start()

    def forward(p):
        for r in range(3):
            ici(p, r, blk[r]).wait_recv()
            d2d(p, r, c).start()

    def finish():
        for p, r in pairs:
            d2d(p, r, 1 - c).wait_recv()
        for p, r in pairs:
            ici(p, r, me).wait_send()
            d2d(p, r, c).wait_send()

    return start, forward, finish


def _gather_sems(parts):
    return [pltpu.SemaphoreType.DMA((3 * len(parts),)) for _ in range(4)]


def _gather_in_steps(bufs, parts, sems, n_steps):
    start, forward, finish = _gather_stages(bufs, parts, sems)
    i = pl.program_id(0)
    pl.when(i == 0)(start)
    for p in range(len(parts)):
        pl.when(i == n_steps * (2 * p + 3) // (2 * len(parts) + 2))(lambda p=p: forward(p))
    pl.when(i == n_steps - 1)(finish)


def _gather_first(bufs, parts, whole):
    n, m = len(bufs), len(whole)

    def body(*refs):
        whole_in, buf_out, whole_out = refs[n:n + m], refs[n + m:2 * n + m], refs[2 * n + m:2 * (n + m)]
        sems = refs[2 * (n + m):]
        send_sems, recv_sems, local_sems = sems[4:]
        x, y, c, me, chips = _mesh_place()
        start, forward, finish = _gather_stages(buf_out, parts, sems[:4])

        def push(k, r, block):
            return pltpu.make_async_remote_copy(
                src_ref=whole_in[k], dst_ref=whole_out[k].at[block], send_sem=send_sems.at[3 * k + r],
                recv_sem=recv_sems.at[3 * k + r], device_id=(chips[r][0], chips[r][1], c), device_id_type=MESH)

        local = [pltpu.make_async_copy(whole_in[k], whole_out[k].at[me], local_sems.at[k]) for k in range(m)]
        for cp in local:
            cp.start()
        start()
        for k in range(m):
            for r in range(3):
                push(k, r, me).start()
        for p in range(len(parts)):
            forward(p)
        for k in range(m):
            for r, chip in enumerate(chips):
                push(k, r, 2 * chip[0] + chip[1]).wait_recv()
        for k in range(m):
            for r in range(3):
                push(k, r, me).wait_send()
        finish()
        for cp in local:
            cp.wait()

    return pl.pallas_call(
        body, name="gather_first",
        in_specs=[_ANY] * (n + m), out_specs=[_ANY] * (n + m),
        out_shape=[jax.ShapeDtypeStruct(b.shape, b.dtype) for b in bufs]
                  + [jax.ShapeDtypeStruct((N_BLK,) + b.shape, b.dtype) for b in whole],
        input_output_aliases={k: k for k in range(n)},
        scratch_shapes=_gather_sems(parts) + [pltpu.SemaphoreType.DMA((3 * m,)), pltpu.SemaphoreType.DMA((3 * m,)),
                                              pltpu.SemaphoreType.DMA((m,))],
    )(*bufs, *whole)


def _swap_halves(gs):
    n = len(gs)

    def body(*refs):
        ins, outs, (send_sems, recv_sems) = refs[:n], refs[n:2 * n], refs[2 * n:]
        x, y, c, _, _ = _mesh_place()
        cps = []
        for k in range(n):
            half = ins[k].shape[1] // 2
            cps.append(pltpu.make_async_remote_copy(
                src_ref=ins[k].at[:, pl.ds((1 - c) * half, half), :], dst_ref=outs[k],
                send_sem=send_sems.at[k], recv_sem=recv_sems.at[k], device_id=(x, y, 1 - c), device_id_type=MESH))
        for cp in cps:
            cp.start()
        for cp in cps:
            cp.wait()

    return pl.pallas_call(
        body, name="swap_halves", in_specs=[_ANY] * n, out_specs=[_ANY] * n,
        out_shape=[jax.ShapeDtypeStruct((g.shape[0], g.shape[1] // 2, g.shape[2]), F32) for g in gs],
        scratch_shapes=[pltpu.SemaphoreType.DMA((n,)), pltpu.SemaphoreType.DMA((n,))],
    )(*gs)


def _send_to_owners(sums):
    n = len(sums)

    def body(*refs):
        ins, outs, (send_sems, recv_sems) = refs[:n], refs[n:2 * n], refs[2 * n:]
        x, y, c, me, chips = _mesh_place()

        def remote(k, r, src_block, dst_block, to):
            return pltpu.make_async_remote_copy(
                src_ref=ins[k].at[src_block], dst_ref=outs[k].at[dst_block], send_sem=send_sems.at[3 * k + r],
                recv_sem=recv_sems.at[3 * k + r], device_id=(to[0], to[1], c), device_id_type=MESH)

        sent = [remote(k, r, 2 * chip[0] + chip[1], me, chip) for k in range(n) for r, chip in enumerate(chips)]
        for cp in sent:
            cp.start()
        for k in range(n):
            for r, chip in enumerate(chips):
                remote(k, r, me, 2 * chip[0] + chip[1], chip).wait_recv()
        for cp in sent:
            cp.wait_send()

    return pl.pallas_call(
        body, name="send_to_owners", in_specs=[_ANY] * n, out_specs=[_ANY] * n,
        out_shape=[jax.ShapeDtypeStruct(s.shape, s.dtype) for s in sums],
        scratch_shapes=[pltpu.SemaphoreType.DMA((3 * n,)), pltpu.SemaphoreType.DMA((3 * n,))],
    )(*sums)


def _swap_reduced(fs):
    n = len(fs)

    def body(*refs):
        ins, outs, (send_sems, recv_sems) = refs[:n], refs[n:2 * n], refs[2 * n:]
        x, y, c, _, _ = _mesh_place()
        cps = [pltpu.make_async_remote_copy(src_ref=ins[k], dst_ref=outs[k], send_sem=send_sems.at[k],
                                            recv_sem=recv_sems.at[k], device_id=(x, y, 1 - c), device_id_type=MESH)
               for k in range(n)]
        for cp in cps:
            cp.start()
        for cp in cps:
            cp.wait()

    return pl.pallas_call(
        body, name="swap_reduced", in_specs=[_ANY] * n, out_specs=[_ANY] * n,
        out_shape=[jax.ShapeDtypeStruct(f.shape, F32) for f in fs],
        scratch_shapes=[pltpu.SemaphoreType.DMA((n,)), pltpu.SemaphoreType.DMA((n,))],
    )(*fs)


def _row_tile(rows):
    return min(rows, 256)


def _add_pairs(core, g, other):
    n, half, cols = other.shape
    t = _row_tile(half)
    per_half = half // t

    def body(c_ref, a_ref, b_ref, o_ref):
        o_ref[...] = (a_ref[...] + b_ref[...]).astype(BF16)

    spec = pl.BlockSpec((None, t, cols), lambda i, j, c_ref: (i, j, 0))
    return pl.pallas_call(
        body, name="add_pairs",
        grid_spec=pltpu.PrefetchScalarGridSpec(
            num_scalar_prefetch=1, grid=(n, per_half),
            in_specs=[pl.BlockSpec((None, t, cols), lambda i, j, c_ref: (i, c_ref[0] * per_half + j, 0)), spec],
            out_specs=spec),
        out_shape=jax.ShapeDtypeStruct(other.shape, BF16), compiler_params=_params(("parallel", "parallel")),
    )(core, g, other)


def _add_chips(me, s1, r2):
    _, r, cols = r2.shape
    t = _row_tile(r)

    def body(me_ref, s_ref, r_ref, o_ref):
        own = s_ref[...].astype(F32)
        parts = [jnp.where(me_ref[0] == k, own, r_ref[k].astype(F32)) for k in range(N_BLK)]
        o_ref[...] = ((parts[0] + parts[1]) + parts[2]) + parts[3]

    return pl.pallas_call(
        body, name="add_chips",
        grid_spec=pltpu.PrefetchScalarGridSpec(
            num_scalar_prefetch=1, grid=(r // t,),
            in_specs=[pl.BlockSpec((None, t, cols), lambda i, me_ref: (me_ref[0], i, 0)),
                      pl.BlockSpec((N_BLK, t, cols), lambda i, me_ref: (0, i, 0))],
            out_specs=pl.BlockSpec((t, cols), lambda i, me_ref: (i, 0))),
        out_shape=jax.ShapeDtypeStruct((r, cols), F32), compiler_params=_params(("parallel",)),
    )(me, s1, r2)


def _adamw(core, mine, other, w, m, v, layer, earlier=None):
    half, cols = mine.shape
    t = _row_tile(half)
    per_half = half // t
    c1 = 1.0 - ADAM_B1 ** ADAM_STEP
    c2 = 1.0 - ADAM_B2 ** ADAM_STEP

    def body(c_ref, a_ref, b_ref, w_ref, m_ref, v_ref, *rest):
        g_ref, d_ref, mo_ref, vo_ref = rest[-4:]
        gv = jnp.where(pl.program_id(0) // per_half == c_ref[0], a_ref[...], b_ref[...])
        g_ref[...] = gv
        m_new = ADAM_B1 * m_ref[...] + (1.0 - ADAM_B1) * gv
        v_new = ADAM_B2 * v_ref[...] + (1.0 - ADAM_B2) * (gv * gv)
        mo_ref[...] = m_new
        vo_ref[...] = v_new
        d_ref[...] = -ADAM_LR * ((m_new / c1) / (jnp.sqrt(v_new / c2) + ADAM_EPS) + ADAM_WD * w_ref[...])

    part = pl.BlockSpec((t, cols), lambda i, c_ref: (i % per_half, 0))
    spec = pl.BlockSpec((None, t, cols), lambda i, c_ref: (layer, i, 0))
    kept = [] if earlier is None else list(earlier)
    return pl.pallas_call(
        body, name="adamw",
        grid_spec=pltpu.PrefetchScalarGridSpec(
            num_scalar_prefetch=1, grid=(2 * per_half,),
            in_specs=[part, part, spec, spec, spec] + [_ANY] * len(kept), out_specs=[spec] * 4),
        out_shape=[jax.ShapeDtypeStruct(w.shape, F32)] * 4,
        input_output_aliases={6 + k: k for k in range(len(kept))},
        compiler_params=_params(("parallel",)),
    )(core, mine, other, w, m, v, *kept)


_REPLICATED = ("norm_mix_g", "gmlp_v_g", "gmlp_w_s", "gmlp_b_s", "conf_ln_g", "conf_ln_b", "mix_out_g", "norm_ffn_g",
               "final_norm_g")
_REP_SHAPES = {"norm_mix_g": (DEPTH, D_MODEL), "gmlp_v_g": (DEPTH, D_GROUP), "gmlp_w_s": (DEPTH, N_HEADS, CHUNK, CHUNK),
               "gmlp_b_s": (DEPTH, N_HEADS, CHUNK), "conf_ln_g": (DEPTH, D_GROUP), "conf_ln_b": (DEPTH, D_GROUP),
               "mix_out_g": (DEPTH, D_MODEL), "norm_ffn_g": (DEPTH, D_MODEL), "final_norm_g": (D_MODEL,)}
_BIG = ("w_in", "w_out", "w_up", "w_down")
_CONV_ROWS = 8
_REP_ROWS = 144
_SMALL_ROWS = 160
_CH_BLK = D_GROUP // N_BLK


def _pad_rows(flat, rows):
    pad = rows * D_MODEL - flat.shape[-1]
    flat = jnp.pad(flat, [(0, 0)] * (flat.ndim - 1) + [(0, pad)])
    return flat.reshape(flat.shape[:-1] + (rows, D_MODEL))


def _pack_small(scw, ccw, rep):
    lead = scw.shape[:-3]
    conv = jnp.concatenate([scw.reshape(lead + (-1,)), ccw.reshape(lead + (-1,))], axis=-1)
    flat = jnp.concatenate([rep[k].reshape(-1) for k in _REPLICATED])
    flat = jnp.broadcast_to(flat, lead + flat.shape)
    parts = [_pad_rows(conv, _CONV_ROWS), _pad_rows(flat, _REP_ROWS),
             jnp.zeros(lead + (_SMALL_ROWS - _CONV_ROWS - _REP_ROWS, D_MODEL), F32)]
    return jnp.concatenate(parts, axis=-2)


def _unpack_small(pk):
    out = {}
    conv = pk[:_CONV_ROWS].reshape(-1)
    n_s = DEPTH * SHORT_K * _CH_BLK
    out["short_conv_w"] = conv[:n_s].reshape(DEPTH, SHORT_K, _CH_BLK)
    out["conf_conv_w"] = conv[n_s:n_s + DEPTH * CONF_K * _CH_BLK].reshape(DEPTH, CONF_K, _CH_BLK)
    row = _CONV_ROWS
    flat = pk[row:row + _REP_ROWS].reshape(-1)
    at = 0
    for k in _REPLICATED:
        n = math.prod(_REP_SHAPES[k])
        out[k] = flat[at:at + n].reshape(_REP_SHAPES[k])
        at += n
    return out


def _conv_blocks(w):
    d, k, _ = w.shape
    return w.reshape(d, k, N_BLK, _CH_BLK).transpose(2, 0, 1, 3)


_WEIGHTS = ("norm_mix_g", "w_in", "gmlp_v_g", "gmlp_w_s", "gmlp_b_s", "short_conv_w", "conf_conv_w", "conf_ln_g",
            "conf_ln_b", "mix_out_g", "w_out", "norm_ffn_g", "w_up", "w_down", "final_norm_g")


def kernel(x, norm_mix_g, w_in, gmlp_v_g, gmlp_w_s, gmlp_b_s, short_conv_w, conf_conv_w, conf_ln_g, conf_ln_b, mix_out_g, w_out, norm_ffn_g, w_up, w_down, final_norm_g, loss_target, m_norm_mix_g, m_w_in, m_gmlp_v_g, m_gmlp_w_s, m_gmlp_b_s, m_short_conv_w, m_conf_conv_w, m_conf_ln_g, m_conf_ln_b, m_mix_out_g, m_w_out, m_norm_ffn_g, m_w_up, m_w_down, m_final_norm_g, v_norm_mix_g, v_w_in, v_gmlp_v_g, v_gmlp_w_s, v_gmlp_b_s, v_short_conv_w, v_conf_conv_w, v_conf_ln_g, v_conf_ln_b, v_mix_out_g, v_w_out, v_norm_ffn_g, v_w_up, v_w_down, v_final_norm_g):
    w = dict(norm_mix_g=norm_mix_g, w_in=w_in, gmlp_v_g=gmlp_v_g, gmlp_w_s=gmlp_w_s, gmlp_b_s=gmlp_b_s,
             short_conv_w=short_conv_w, conf_conv_w=conf_conv_w, conf_ln_g=conf_ln_g, conf_ln_b=conf_ln_b,
             mix_out_g=mix_out_g, w_out=w_out, norm_ffn_g=norm_ffn_g, w_up=w_up, w_down=w_down, final_norm_g=final_norm_g)
    m = dict(norm_mix_g=m_norm_mix_g, w_in=m_w_in, gmlp_v_g=m_gmlp_v_g, gmlp_w_s=m_gmlp_w_s, gmlp_b_s=m_gmlp_b_s,
             short_conv_w=m_short_conv_w, conf_conv_w=m_conf_conv_w, conf_ln_g=m_conf_ln_g, conf_ln_b=m_conf_ln_b,
             mix_out_g=m_mix_out_g, w_out=m_w_out, norm_ffn_g=m_norm_ffn_g, w_up=m_w_up, w_down=m_w_down,
             final_norm_g=m_final_norm_g)
    v = dict(norm_mix_g=v_norm_mix_g, w_in=v_w_in, gmlp_v_g=v_gmlp_v_g, gmlp_w_s=v_gmlp_w_s, gmlp_b_s=v_gmlp_b_s,
             short_conv_w=v_short_conv_w, conf_conv_w=v_conf_conv_w, conf_ln_g=v_conf_ln_g, conf_ln_b=v_conf_ln_b,
             mix_out_g=v_mix_out_g, w_out=v_w_out, norm_ffn_g=v_norm_ffn_g, w_up=v_w_up, w_down=v_w_down,
             final_norm_g=v_final_norm_g)
    core = lax.axis_index("c").astype(jnp.int32).reshape(1)
    me = (2 * lax.axis_index("x") + lax.axis_index("y")).astype(jnp.int32).reshape(1)

    conv_mine = _pad_rows(jnp.concatenate([short_conv_w.reshape(-1), conf_conv_w.reshape(-1)]), _CONV_ROWS)
    big = {k: _cast_into_slot(w[k], me, "cast_" + k) for k in _BIG}
    big["w_in"], conv_all = _gather_first([big["w_in"]], [(0, 0)], [conv_mine])
    conv_all = conv_all.reshape(N_BLK, -1)
    n_s = DEPTH * SHORT_K * _CH_BLK
    scw_all = conv_all[:, :n_s].reshape(N_BLK, DEPTH, SHORT_K, _CH_BLK)
    ccw_all = conv_all[:, n_s:n_s + DEPTH * CONF_K * _CH_BLK].reshape(N_BLK, DEPTH, CONF_K, _CH_BLK)
    small = {k: w[k] for k in _REPLICATED}
    small["short_conv_w"] = scw_all.transpose(1, 2, 0, 3).reshape(DEPTH, SHORT_K, D_GROUP)
    small["conf_conv_w"] = ccw_all.transpose(1, 2, 0, 3).reshape(DEPTH, CONF_K, D_GROUP)

    loss, dx, g = _local_step(x[0], loss_target[0], big, small, gather_pending=True)

    where = [(k, l) for k in _BIG for l in range(DEPTH)]
    grads = [g[k][l] for k, l in where] + [_pack_small(_conv_blocks(g["short_conv_w"]), _conv_blocks(g["conf_conv_w"]), g)]
    sums = [_add_pairs(core, a, b) for a, b in zip(grads, _swap_halves(grads))]
    mine = [_add_chips(me, s, r) for s, r in zip(sums, _send_to_owners(sums))]
    other = _swap_reduced(mine)

    done = {}
    for n, (k, l) in enumerate(where):
        done[k] = _adamw(core, mine[n], other[n], w[k], m[k], v[k], l, done.get(k))
    small_own = [_pack_small(t["short_conv_w"], t["conf_conv_w"], t)[None] for t in (w, m, v)]
    small_done = [_unpack_small(a[0]) for a in _adamw(core, mine[-1], other[-1], *small_own, 0)]

    outs = [lax.psum(loss[0, 0], ("x", "y", "c")), dx[None]]
    for kind in range(4):
        outs += [done[k][kind] if k in _BIG else small_done[kind][k] for k in _WEIGHTS]
    return tuple(outs)
```

```python
import math

import jax
import jax.numpy as jnp
from jax import lax
from jax.experimental import pallas as pl
from jax.experimental.pallas import tpu as pltpu

F32 = jnp.float32
BF16 = jnp.bfloat16

D_MODEL = 1024
D_GROUP = 256
N_HEADS = 4
HEAD_DIM = 64
CHUNK = 128
D_IN = 2560
N_BLK = 4
W_IN_BLK = D_IN // N_BLK
D_FF_BLK = 1024
DEPTH = 2
EPS = 1e-6
HALO = 32
SHORT_K = 3
CONF_K = 31
ATT_TQ = 256
ATT_TK = 256
ATT_SCALE = 0.125
ATT_DEAD = -104.0
V7X_VMEM_LIMIT = 56 * 1024 * 1024

ADAM_LR, ADAM_B1, ADAM_B2, ADAM_EPS, ADAM_WD, ADAM_STEP = 0.001, 0.9, 0.999, 1e-08, 0.01, 10

MESH = pl.DeviceIdType.MESH


def _params(sem, vmem=None):
    return pltpu.CompilerParams(dimension_semantics=sem, vmem_limit_bytes=vmem)


def _tile(s, t):
    return min(s, t)


def _rsqrt_mean(v):
    return lax.rsqrt(jnp.mean(v * v, axis=-1, keepdims=True) + EPS)


def _sigmoid(v):
    return 1.0 / (1.0 + jnp.exp(-v))


_GELU_C = math.sqrt(2.0 / math.pi)


def _gelu_tanh(v):
    return jnp.tanh(_GELU_C * (v + 0.044715 * (v * v * v)))


def _gelu(v, t):
    return v * (0.5 * (1.0 + t))


def _gelu_grad(v, t):
    return 0.5 * (1.0 + t) + v * (0.5 * (1.0 - t * t) * _GELU_C * (1.0 + 3.0 * 0.044715 * (v * v)))


def _dot(a, b):
    return jnp.dot(a, b, preferred_element_type=F32)


def _dot_nt(a, b):
    return lax.dot_general(a, b, (((1,), (1,)), ((), ())), preferred_element_type=F32)


def _dot_tn(a, b):
    return lax.dot_general(a, b, (((0,), (0,)), ((), ())), preferred_element_type=F32)


def _cast_into_slot(w, me, name):
    n, r, c = w.shape
    tr = _tile(r, 256)

    def body(me_ref, w_ref, o_ref):
        o_ref[...] = w_ref[...].astype(BF16)

    return pl.pallas_call(
        body, name=name,
        grid_spec=pltpu.PrefetchScalarGridSpec(
            num_scalar_prefetch=1, grid=(n, r // tr),
            in_specs=[pl.BlockSpec((None, tr, c), lambda a, b, me_ref: (a, b, 0))],
            out_specs=pl.BlockSpec((None, None, tr, c), lambda a, b, me_ref: (me_ref[0], a, b, 0))),
        out_shape=jax.ShapeDtypeStruct((N_BLK,) + w.shape, BF16),
        compiler_params=_params(("parallel", "parallel")),
    )(me, w)


def _split_heads(xv, rows_ref, cols_ref):
    if rows_ref is not None:
        for h in range(N_HEADS):
            rows_ref[h] = xv[:, h * HEAD_DIM:(h + 1) * HEAD_DIM].astype(BF16)
    if cols_ref is not None:
        xt = xv.T
        for h in range(N_HEADS):
            cols_ref[h] = xt[h * HEAD_DIM:(h + 1) * HEAD_DIM, :].astype(BF16)


def _head_specs(t, s):
    rows = (pl.BlockSpec((N_HEADS, t, HEAD_DIM), lambda i: (0, i, 0)), jax.ShapeDtypeStruct((N_HEADS, s, HEAD_DIM), BF16))
    cols = (pl.BlockSpec((N_HEADS, HEAD_DIM, t), lambda i: (0, 0, i)), jax.ShapeDtypeStruct((N_HEADS, HEAD_DIM, s), BF16))
    return rows, cols


def _in_proj(x, g, w, layer):
    s = x.shape[0]
    t = _tile(s, 512)

    def body(x_ref, g_ref, w_ref, z_ref, h_ref, qr_ref, qt_ref, kt_ref, vt_ref):
        xv = x_ref[...]
        h = (xv * _rsqrt_mean(xv) * g_ref[...]).astype(BF16)
        h_ref[...] = h
        for b in range(N_BLK):
            z_ref[:, b * W_IN_BLK:(b + 1) * W_IN_BLK] = _dot(h, w_ref[b])
        _split_heads(z_ref[:, 1280:1536] * ATT_SCALE, qr_ref, qt_ref)
        _split_heads(z_ref[:, 1536:1792], None, kt_ref)
        _split_heads(z_ref[:, 1792:2048], None, vt_ref)

    rows, cols = _head_specs(t, s)
    return pl.pallas_call(
        body, name="in_proj", grid=(s // t,),
        in_specs=[pl.BlockSpec((t, D_MODEL), lambda i: (i, 0)), _full((1, D_MODEL)),
                  pl.BlockSpec((N_BLK, None, D_MODEL, W_IN_BLK), lambda i: (0, layer, 0, 0))],
        out_specs=[pl.BlockSpec((t, D_IN), lambda i: (i, 0)), pl.BlockSpec((t, D_MODEL), lambda i: (i, 0)),
                   rows[0], cols[0], cols[0], cols[0]],
        out_shape=[jax.ShapeDtypeStruct((s, D_IN), F32), jax.ShapeDtypeStruct((s, D_MODEL), BF16),
                   rows[1], cols[1], cols[1], cols[1]],
        compiler_params=_params(("parallel",), V7X_VMEM_LIMIT),
    )(x, g, w)


def _mix_a_fwd(z_ref, vg, wt_ref, bmat, t):
    zu = z_ref[:, 0:256]
    zv = z_ref[:, 256:512]
    tu = _gelu_tanh(zu)
    tv = _gelu_tanh(zv)
    u = _gelu(zu, tu)
    v = _gelu(zv, tv)
    rv = _rsqrt_mean(v)
    vh = v * rv
    vnb = (vh * vg).astype(BF16)
    head = lax.broadcasted_iota(jnp.int32, (CHUNK, D_GROUP), 1) // HEAD_DIM
    fs = []
    for c in range(t // CHUNK):
        vc = vnb[c * CHUNK:(c + 1) * CHUNK, :]
        fc = bmat
        for h in range(N_HEADS):
            fc = fc + jnp.where(head == h, _dot(wt_ref[h], vc), 0.0)
        fs.append(fc)
    f = jnp.concatenate(fs, axis=0) if len(fs) > 1 else fs[0]
    return (zu, tu), (zv, tv), u, rv, vh, vnb, f


def _windows(ext_ref, sh_ref, t):
    for b in range(1, 8):
        sh_ref[b - 1] = ext_ref[pl.ds(b, HALO + t - 8), :]

    def window(o):
        a, b = divmod(o, 8)
        return ext_ref[pl.ds(8 * a, t), :] if b == 0 else sh_ref[b - 1, pl.ds(8 * a, t), :]

    return window


def _mix_b_fwd(z_ref, zh_ref, first, scw_ref, ext_ref, t):
    gb = z_ref[:, 512:768]
    uh = zh_ref[:, 768:1024] * zh_ref[:, 1024:1280]
    ext_ref[0:HALO, :] = jnp.where(first, 0.0, uh)
    ext_ref[HALO:HALO + t, :] = z_ref[:, 768:1024] * z_ref[:, 1024:1280]
    cv = jnp.zeros((t, D_GROUP), F32)
    for k in range(SHORT_K):
        cv = cv + scw_ref[k:k + 1, :] * ext_ref[pl.ds(HALO - (SHORT_K - 1) + k, t), :]
    return gb, cv


def _mix_d_fwd(z_ref, zh_ref, first, ccw_ref, lg, lb, ext_ref, sh_ref, t):
    hh = zh_ref[:, 2048:2304] * _sigmoid(zh_ref[:, 2304:2560])
    ext_ref[0:HALO, :] = jnp.where(first, 0.0, hh)
    ext_ref[HALO:HALO + t, :] = z_ref[:, 2048:2304] * _sigmoid(z_ref[:, 2304:2560])
    window = _windows(ext_ref, sh_ref, t)
    cv = jnp.zeros((t, D_GROUP), F32)
    for k in range(CONF_K):
        cv = cv + ccw_ref[k:k + 1, :] * window(HALO - (CONF_K - 1) + k)
    xc = cv - jnp.mean(cv, axis=-1, keepdims=True)
    rs = lax.rsqrt(jnp.mean(xc * xc, axis=-1, keepdims=True) + EPS)
    xh = xc * rs
    ln = xh * lg + lb
    return xh, rs, ln, _sigmoid(ln), window


def _mix_specs(t, s):
    per = t // HALO
    return [pl.BlockSpec((t, D_IN), lambda i: (i, 0)),
            pl.BlockSpec((HALO, D_IN), lambda i: (jnp.maximum(i * per - 1, 0), 0))]


def _full(shape):
    return pl.BlockSpec(shape, lambda i: (0,) * len(shape))


def _mixers_fwd(z, p, bufs=(), parts=()):
    s = z.shape[0]
    t = _tile(s, 256)
    nb = len(bufs)

    def body(z_ref, zh_ref, vg_ref, wt_ref, bm_ref, scw_ref, ccw_ref, lg_ref, lb_ref, *rest):
        y_ref = rest[nb]
        eb_ref, ed_ref, sh_ref = rest[2 * nb + 1:2 * nb + 4]
        if nb:
            _gather_in_steps(rest[nb + 1:2 * nb + 1], parts, rest[2 * nb + 4:], s // t)
        first = pl.program_id(0) == 0
        _, _, u, _, _, _, f = _mix_a_fwd(z_ref, vg_ref[...], wt_ref, bm_ref[...], t)
        ya = u * f
        y_ref[:, 0:256] = ya * _rsqrt_mean(ya)
        gb, cv = _mix_b_fwd(z_ref, zh_ref, first, scw_ref, eb_ref, t)
        yb = gb * cv
        y_ref[:, 256:512] = yb * _rsqrt_mean(yb)
        _, _, ln, sg, _ = _mix_d_fwd(z_ref, zh_ref, first, ccw_ref, lg_ref[...], lb_ref[...], ed_ref, sh_ref, t)
        yd = ln * sg
        y_ref[:, 512:768] = yd * _rsqrt_mean(yd)

    out = pl.pallas_call(
        body, name="mixers_fwd", grid=(s // t,),
        in_specs=_mix_specs(t, s) + [_full((1, D_GROUP)), _full((N_HEADS, CHUNK, CHUNK)), _full((CHUNK, D_GROUP)),
                                     _full((8, D_GROUP)), _full((32, D_GROUP)), _full((1, D_GROUP)), _full((1, D_GROUP))]
                 + [_ANY] * nb,
        out_specs=[pl.BlockSpec((t, 768), lambda i: (i, 0))] + [_ANY] * nb,
        out_shape=[jax.ShapeDtypeStruct((s, 768), F32)] + [jax.ShapeDtypeStruct(b.shape, b.dtype) for b in bufs],
        input_output_aliases={9 + k: 1 + k for k in range(nb)},
        scratch_shapes=[pltpu.VMEM((HALO + t, D_GROUP), F32), pltpu.VMEM((HALO + t, D_GROUP), F32),
                        pltpu.VMEM((7, HALO + t - 8, D_GROUP), F32)] + (_gather_sems(parts) if nb else []),
        compiler_params=_params(("arbitrary",) if nb else ("parallel",)),
    )(z, z, p["vg"], p["wt"], p["bmat"], p["scw"], p["ccw"], p["lg"], p["lb"], *bufs)
    return (out[0], out[1:]) if nb else out[0]


def _mixers_bwd_a(z, dyn, o_t, p, sums=()):
    s = z.shape[0]
    t = _tile(s, 256)
    n_chunk = t // CHUNK
    ns = len(sums)

    def body(*refs):
        (z_ref, zh_ref, dyn_ref, ot_ref, vg_ref, wt_ref, wtt_ref, bm_ref, scw_ref, ccw_ref, lg_ref, lb_ref) = refs[:12]
        (dza_ref, dcb_ref, dcd_ref, dor_ref, dot_ref, ds_ref, dvg_ref, dws_ref, dbm_ref, dscw_ref, dccw_ref, dlg_ref,
         dlb_ref) = refs[12 + ns:25 + ns]
        eb_ref, ed_ref, sh_ref = refs[25 + 2 * ns:28 + 2 * ns]
        i = pl.program_id(0)
        first = i == 0
        if ns:
            _owners_in_steps(refs[12:12 + ns], refs[25 + ns:25 + 2 * ns], refs[28 + 2 * ns:], first, i == s // t - 1)

        @pl.when(first)
        def _():
            for r in (dvg_ref, dws_ref, dbm_ref, dscw_ref, dccw_ref, dlg_ref, dlb_ref):
                r[...] = jnp.zeros_like(r)

        def rms_bwd(y, dn):
            r = _rsqrt_mean(y)
            yn = y * r
            return r * (dn - yn * jnp.mean(dn * yn, axis=-1, keepdims=True))

        vg = vg_ref[...]
        gelu_u, gelu_v, u, rv, vh, vnb, f = _mix_a_fwd(z_ref, vg, wt_ref, bm_ref[...], t)
        dya = rms_bwd(u * f, dyn_ref[:, 0:256])
        du = dya * f
        df = dya * u
        head = lax.broadcasted_iota(jnp.int32, (CHUNK, D_GROUP), 1) // HEAD_DIM
        dvns = []
        dbm = jnp.zeros((CHUNK, D_GROUP), F32)
        for c in range(n_chunk):
            dfc = df[c * CHUNK:(c + 1) * CHUNK, :]
            vc = vnb[c * CHUNK:(c + 1) * CHUNK, :]
            dbm = dbm + dfc
            dvn = jnp.zeros((CHUNK, D_GROUP), F32)
            for h in range(N_HEADS):
                dfh = jnp.where(head == h, dfc, 0.0).astype(BF16)
                dvn = dvn + _dot(wtt_ref[h], dfh)
                dws_ref[h] += _dot_nt(dfh, vc)
            dvns.append(dvn)
        dbm_ref[...] += dbm
        dvn = jnp.concatenate(dvns, axis=0) if n_chunk > 1 else dvns[0]
        dvg_ref[...] += jnp.sum(dvn * vh, axis=0, keepdims=True)
        dvh = dvn * vg
        dv = rv * (dvh - vh * jnp.mean(dvh * vh, axis=-1, keepdims=True))
        dza_ref[:, 0:256] = (du * _gelu_grad(*gelu_u)).astype(BF16)
        dza_ref[:, 256:512] = (dv * _gelu_grad(*gelu_v)).astype(BF16)

        gb, cv = _mix_b_fwd(z_ref, zh_ref, first, scw_ref, eb_ref, t)
        dyb = rms_bwd(gb * cv, dyn_ref[:, 256:512])
        dza_ref[:, 512:768] = (dyb * cv).astype(BF16)
        dcb = dyb * gb
        dcb_ref[...] = dcb
        for k in range(SHORT_K):
            dscw_ref[k:k + 1, :] += jnp.sum(dcb * eb_ref[pl.ds(HALO - (SHORT_K - 1) + k, t), :], axis=0, keepdims=True)

        lg = lg_ref[...]
        xh, rs, ln, sg, window = _mix_d_fwd(z_ref, zh_ref, first, ccw_ref, lg, lb_ref[...], ed_ref, sh_ref, t)
        dyd = rms_bwd(ln * sg, dyn_ref[:, 768:1024])
        dln = dyd * (sg * (1.0 + ln * (1.0 - sg)))
        dlg_ref[...] += jnp.sum(dln * xh, axis=0, keepdims=True)
        dlb_ref[...] += jnp.sum(dln, axis=0, keepdims=True)
        dxh = dln * lg
        dcd = rs * (dxh - jnp.mean(dxh, axis=-1, keepdims=True) - xh * jnp.mean(dxh * xh, axis=-1, keepdims=True))
        dcd_ref[...] = dcd
        for k in range(CONF_K):
            dccw_ref[k:k + 1, :] += jnp.sum(dcd * window(HALO - (CONF_K - 1) + k), axis=0, keepdims=True)

        o = ot_ref[...].reshape(D_GROUP, t).T
        do = rms_bwd(o, dyn_ref[:, 512:768])
        _split_heads(do, dor_ref, dot_ref)
        prod = do.astype(BF16).astype(F32) * o
        for h in range(N_HEADS):
            ds_ref[h] = jnp.sum(prod[:, h * HEAD_DIM:(h + 1) * HEAD_DIM], axis=1, keepdims=True)

    small = [(1, D_GROUP), (N_HEADS, CHUNK, CHUNK), (CHUNK, D_GROUP), (8, D_GROUP), (32, D_GROUP), (1, D_GROUP), (1, D_GROUP)]
    rows, cols = _head_specs(t, s)
    out = pl.pallas_call(
        body, name="mixers_bwd_a", grid=(s // t,),
        in_specs=_mix_specs(t, s) + [pl.BlockSpec((t, D_MODEL), lambda i: (i, 0)),
                                     pl.BlockSpec((N_HEADS, HEAD_DIM, t), lambda i: (0, 0, i)),
                                     _full((1, D_GROUP)), _full((N_HEADS, CHUNK, CHUNK)), _full((N_HEADS, CHUNK, CHUNK)),
                                     _full((CHUNK, D_GROUP)), _full((8, D_GROUP)), _full((32, D_GROUP)),
                                     _full((1, D_GROUP)), _full((1, D_GROUP))] + [_ANY] * ns,
        out_specs=[pl.BlockSpec((t, 768), lambda i: (i, 0)), pl.BlockSpec((t, D_GROUP), lambda i: (i, 0)),
                   pl.BlockSpec((t, D_GROUP), lambda i: (i, 0)), rows[0], cols[0],
                   pl.BlockSpec((N_HEADS, t, 1), lambda i: (0, i, 0))]
                  + [_full(sh) for sh in small] + [_ANY] * ns,
        out_shape=[jax.ShapeDtypeStruct((s, 768), BF16), jax.ShapeDtypeStruct((s, D_GROUP), F32),
                   jax.ShapeDtypeStruct((s, D_GROUP), F32), rows[1], cols[1],
                   jax.ShapeDtypeStruct((N_HEADS, s, 1), F32)]
                  + [jax.ShapeDtypeStruct(sh, F32) for sh in small]
                  + [jax.ShapeDtypeStruct(a.shape, a.dtype) for a in sums],
        scratch_shapes=[pltpu.VMEM((HALO + t, D_GROUP), F32), pltpu.VMEM((HALO + t, D_GROUP), F32),
                        pltpu.VMEM((7, HALO + t - 8, D_GROUP), F32)] + (_owner_sems(ns) if ns else []),
        compiler_params=_params(("arbitrary",)),
    )(z, z, dyn, o_t, p["vg"], p["wt"], p["wtt"], p["bmat"], p["scw"], p["ccw"], p["lg"], p["lb"], *sums)
    return tuple(out[:13]) + (out[13:],) if ns else out


def _mixers_bwd_b(z, dza, dcb, dcd, dq_t, dk_t, dv_t, p):
    s = z.shape[0]
    t = _tile(s, 256)
    per = t // HALO
    n_halo = s // HALO

    def body(z_ref, dza_ref, dcb_ref, dcbn_ref, dcd_ref, dcdn_ref, dq_ref, dk_ref, dv_ref, scw_ref, ccw_ref,
             dz_ref, eb_ref, ed_ref, sh_ref):
        last = pl.program_id(0) == pl.num_programs(0) - 1
        dz_ref[:, 0:768] = dza_ref[...]
        eb_ref[0:t, :] = dcb_ref[...]
        eb_ref[t:t + HALO, :] = jnp.where(last, 0.0, dcbn_ref[...])
        du = jnp.zeros((t, D_GROUP), F32)
        for k in range(SHORT_K):
            du = du + scw_ref[k:k + 1, :] * eb_ref[pl.ds(SHORT_K - 1 - k, t), :]
        dz_ref[:, 768:1024] = (du * z_ref[:, 1024:1280]).astype(BF16)
        dz_ref[:, 1024:1280] = (du * z_ref[:, 768:1024]).astype(BF16)
        for n, r in enumerate((dq_ref, dk_ref, dv_ref)):
            dz_ref[:, 1280 + 256 * n:1536 + 256 * n] = r[...].reshape(D_GROUP, t).T.astype(BF16)
        ed_ref[0:t, :] = dcd_ref[...]
        ed_ref[t:t + HALO, :] = jnp.where(last, 0.0, dcdn_ref[...])
        window = _windows(ed_ref, sh_ref, t)
        dh = jnp.zeros((t, D_GROUP), F32)
        for k in range(CONF_K):
            dh = dh + ccw_ref[k:k + 1, :] * window(CONF_K - 1 - k)
        a = z_ref[:, 2048:2304]
        sg = _sigmoid(z_ref[:, 2304:2560])
        dz_ref[:, 2048:2304] = (dh * sg).astype(BF16)
        dz_ref[:, 2304:2560] = (dh * a * sg * (1.0 - sg)).astype(BF16)

    nxt = lambda i: (jnp.minimum((i + 1) * per, n_halo - 1), 0)
    tr = pl.BlockSpec((N_HEADS, HEAD_DIM, t), lambda i: (0, 0, i))
    return pl.pallas_call(
        body, name="mixers_bwd_b", grid=(s // t,),
        in_specs=[pl.BlockSpec((t, D_IN), lambda i: (i, 0)), pl.BlockSpec((t, 768), lambda i: (i, 0)),
                  pl.BlockSpec((t, D_GROUP), lambda i: (i, 0)), pl.BlockSpec((HALO, D_GROUP), nxt),
                  pl.BlockSpec((t, D_GROUP), lambda i: (i, 0)), pl.BlockSpec((HALO, D_GROUP), nxt),
                  tr, tr, tr, _full((8, D_GROUP)), _full((32, D_GROUP))],
        out_specs=pl.BlockSpec((t, D_IN), lambda i: (i, 0)),
        out_shape=jax.ShapeDtypeStruct((s, D_IN), BF16),
        scratch_shapes=[pltpu.VMEM((HALO + t, D_GROUP), F32), pltpu.VMEM((HALO + t, D_GROUP), F32),
                        pltpu.VMEM((7, HALO + t - 8, D_GROUP), F32)],
        compiler_params=_params(("parallel",)),
    )(z, dza, dcb, dcb, dcd, dcd, dq_t, dk_t, dv_t, p["scw"], p["ccw"])


def _split_bf16(v):
    hi = v.astype(BF16)
    return hi, (v - hi.astype(F32)).astype(BF16)


def _att_scores(qs, kts, carries, tri, mask):
    zs = [_dot(q, kt) for q, kt in zip(qs, kts)]
    lms, lbs, parts = [], [], []
    for z in zs:
        soft = jnp.log(1.0 + jnp.exp(-jnp.abs(z)))
        lm = -(jnp.maximum(z, 0.0) + soft)
        lbs.append(lm + z)
        if mask is not None:
            lm = jnp.where(mask, lm, 0.0)
        lms.append(lm)
        parts.append(_split_bf16(lm))
    rights = [_dot(hi, tri) + _dot(lo, tri) for hi, lo in parts]
    ws = []
    for lb, right, carry in zip(lbs, rights, carries):
        w = jnp.exp(lb + right + carry)
        ws.append(w if mask is None else jnp.where(mask, w, 0.0))
    return ws, lbs, [jnp.sum(lm, axis=1, keepdims=True) for lm in lms]


def _att_consts(i):
    j_hi = ((i + 1) * ATT_TQ - 1) // ATT_TK
    row = lax.broadcasted_iota(jnp.int32, (ATT_TQ, ATT_TK), 0) + i * ATT_TQ
    col = lax.broadcasted_iota(jnp.int32, (ATT_TQ, ATT_TK), 1) + j_hi * ATT_TK
    r_i = lax.broadcasted_iota(jnp.int32, (ATT_TK, ATT_TK), 0)
    c_i = lax.broadcasted_iota(jnp.int32, (ATT_TK, ATT_TK), 1)
    return j_hi, col < row, r_i, c_i


def _att_alive(j, carries):
    top = carries[0]
    for c in carries[1:]:
        top = jnp.maximum(top, c)
    return jnp.logical_and(j >= 0, jnp.max(top) > ATT_DEAD)


def _attn_fwd(q_r, k_t, v_t, bufs=(), parts=()):
    s = q_r.shape[1]
    nb = len(bufs)

    def body(q_ref, kt_ref, vt_ref, *rest):
        o_ref = rest[nb]
        if nb:
            _gather_in_steps(rest[nb + 1:2 * nb + 1], parts, rest[2 * nb + 1:], s // ATT_TQ)
        j_hi, mask, r_i, c_i = _att_consts(pl.program_id(0))
        tri = (r_i > c_i).astype(BF16)

        heads = range(N_HEADS)

        def tiles(j, carries, accs, mask):
            cols = pl.ds(pl.multiple_of(j * ATT_TK, ATT_TK), ATT_TK)
            ws, _, tots = _att_scores([q_ref[h] for h in heads], [kt_ref[h, :, cols] for h in heads], carries, tri, mask)
            accs = [acc + _dot_nt(vt_ref[h, :, cols], w.astype(BF16)) for h, acc, w in zip(heads, accs, ws)]
            return [c + t for c, t in zip(carries, tots)], accs

        state = tiles(j_hi, [jnp.zeros((ATT_TQ, 1), F32)] * N_HEADS, [jnp.zeros((HEAD_DIM, ATT_TQ), F32)] * N_HEADS, mask)

        def cond(c):
            return _att_alive(c[0], c[1])

        def step(c):
            return (c[0] - 1,) + tuple(tiles(c[0], c[1], c[2], None))

        _, _, accs = lax.while_loop(cond, step, (j_hi - 1,) + tuple(state))
        for h in heads:
            o_ref[h] = accs[h]

    whole = pl.BlockSpec((N_HEADS, HEAD_DIM, s), lambda i: (0, 0, 0), pipeline_mode=pl.Buffered(1))
    out = pl.pallas_call(
        body, name="attn_fwd", grid=(s // ATT_TQ,),
        in_specs=[pl.BlockSpec((N_HEADS, ATT_TQ, HEAD_DIM), lambda i: (0, i, 0)), whole, whole] + [_ANY] * nb,
        out_specs=[pl.BlockSpec((N_HEADS, HEAD_DIM, ATT_TQ), lambda i: (0, 0, i))] + [_ANY] * nb,
        out_shape=[jax.ShapeDtypeStruct((N_HEADS, HEAD_DIM, s), F32)] + [jax.ShapeDtypeStruct(b.shape, b.dtype) for b in bufs],
        input_output_aliases={3 + k: 1 + k for k in range(nb)},
        scratch_shapes=_gather_sems(parts) if nb else [],
        compiler_params=_params(("arbitrary",), V7X_VMEM_LIMIT),
    )(q_r, k_t, v_t, *bufs)
    return (out[0], out[1:]) if nb else out[0]


ATT_BWD_HEADS = 2


def _attn_bwd(q_r, q_t, k_t, v_t, do_r, do_t, dsum):
    s = q_r.shape[1]
    hps = ATT_BWD_HEADS

    def body(q_ref, qt_ref, kt_ref, vt_ref, do_ref, dot_ref, ds_ref, dq_ref, dk_ref, dv_ref):
        i = pl.program_id(1)

        @pl.when(i == 0)
        def _():
            dk_ref[...] = jnp.zeros_like(dk_ref)
            dv_ref[...] = jnp.zeros_like(dv_ref)

        j_hi, mask, r_i, c_i = _att_consts(i)
        tri_r = (r_i > c_i).astype(BF16)
        tri_ge = (r_i >= c_i).astype(BF16)

        heads = range(hps)

        def tiles(j, carries, gsums, accs, mask):
            cols = pl.ds(pl.multiple_of(j * ATT_TK, ATT_TK), ATT_TK)
            kts = [kt_ref[h, :, cols] for h in heads]
            das = [_dot(do_ref[h], vt_ref[h, :, cols]) for h in heads]
            ws, lbs, tots = _att_scores([q_ref[h] for h in heads], kts, carries, tri_r, mask)
            wbs = [w.astype(BF16) for w in ws]
            gs = [wb.astype(F32) * da for wb, da in zip(wbs, das)]
            parts = [_split_bf16(g) for g in gs]
            sfx = [_dot(hi, tri_ge) + _dot(lo, tri_ge) for hi, lo in parts]
            for h in heads:
                dv_ref[h, :, cols] += _dot(dot_ref[h], wbs[h])
            dzs = []
            for h in heads:
                left = ds_ref[h] - gsums[h] - sfx[h]
                dz = gs[h] - jnp.exp(lbs[h]) * (gs[h] + left)
                dzs.append((dz if mask is None else jnp.where(mask, dz, 0.0)).astype(BF16))
            for h in heads:
                dk_ref[h, :, cols] += _dot(qt_ref[h], dzs[h])
            accs = [accs[h] + _dot_nt(kts[h], dzs[h]) for h in heads]
            gsums = [gsums[h] + jnp.sum(gs[h], axis=1, keepdims=True) for h in heads]
            return [c + t for c, t in zip(carries, tots)], gsums, accs

        col0 = [jnp.zeros((ATT_TQ, 1), F32)] * hps
        state = tiles(j_hi, col0, col0, [jnp.zeros((HEAD_DIM, ATT_TQ), F32)] * hps, mask)

        def cond(c):
            return _att_alive(c[0], c[1])

        def step(c):
            return (c[0] - 1,) + tuple(tiles(c[0], c[1], c[2], c[3], None))

        _, _, _, accs = lax.while_loop(cond, step, (j_hi - 1,) + tuple(state))
        for h in heads:
            dq_ref[h] = accs[h] * ATT_SCALE

    whole = pl.BlockSpec((hps, HEAD_DIM, s), lambda g, i: (g, 0, 0))
    whole_in = pl.BlockSpec((hps, HEAD_DIM, s), lambda g, i: (g, 0, 0), pipeline_mode=pl.Buffered(1))
    rows = pl.BlockSpec((hps, ATT_TQ, HEAD_DIM), lambda g, i: (g, i, 0))
    cols = pl.BlockSpec((hps, HEAD_DIM, ATT_TQ), lambda g, i: (g, 0, i))
    shape = jax.ShapeDtypeStruct((N_HEADS, HEAD_DIM, s), F32)
    return pl.pallas_call(
        body, name="attn_bwd", grid=(N_HEADS // hps, s // ATT_TQ),
        in_specs=[rows, cols, whole_in, whole_in, rows, cols, pl.BlockSpec((hps, ATT_TQ, 1), lambda g, i: (g, i, 0))],
        out_specs=[cols, whole, whole],
        out_shape=[shape, shape, shape],
        compiler_params=_params(("parallel", "arbitrary"), V7X_VMEM_LIMIT),
    )(q_r, q_t, k_t, v_t, do_r, do_t, dsum)


def _out_proj(x, y_abd, o_t, gain, w, layer):
    s = x.shape[0]
    t = _tile(s, 512)

    def body(x_ref, y_ref, ot_ref, g_ref, w_ref, x1_ref, yn_ref):
        o = ot_ref[...].reshape(D_GROUP, t).T
        yn_ref[:, 0:512] = y_ref[:, 0:512]
        yn_ref[:, 512:768] = o * _rsqrt_mean(o)
        yn_ref[:, 768:1024] = y_ref[:, 512:768]
        yg = (yn_ref[...] * g_ref[...]).astype(BF16)
        acc = _dot(yg[:, 0:256], w_ref[0])
        for b in range(1, N_BLK):
            acc = acc + _dot(yg[:, 256 * b:256 * (b + 1)], w_ref[b])
        x1_ref[...] = x_ref[...] + acc

    return pl.pallas_call(
        body, name="out_proj", grid=(s // t,),
        in_specs=[pl.BlockSpec((t, D_MODEL), lambda i: (i, 0)), pl.BlockSpec((t, 768), lambda i: (i, 0)),
                  pl.BlockSpec((N_HEADS, HEAD_DIM, t), lambda i: (0, 0, i)), _full((1, D_MODEL)),
                  pl.BlockSpec((N_BLK, None, D_GROUP, D_MODEL), lambda i: (0, layer, 0, 0))],
        out_specs=[pl.BlockSpec((t, D_MODEL), lambda i: (i, 0)), pl.BlockSpec((t, D_MODEL), lambda i: (i, 0))],
        out_shape=[jax.ShapeDtypeStruct((s, D_MODEL), F32), jax.ShapeDtypeStruct((s, D_MODEL), F32)],
        compiler_params=_params(("parallel",)),
    )(x, y_abd, o_t, gain, w)


def _out_proj_bwd(x1, g_ffn, dh, dx2, yn, gain, w, layer):
    s = dx2.shape[0]
    t = _tile(s, 512)

    def body(x_ref, gf_ref, dh_ref, dx2_ref, yn_ref, g_ref, w_ref, dx1_ref, dgf_ref, dyn_ref, dg_ref, dw_ref):
        @pl.when(pl.program_id(0) == 0)
        def _():
            dg_ref[...] = jnp.zeros_like(dg_ref)
            dw_ref[...] = jnp.zeros_like(dw_ref)
            dgf_ref[...] = jnp.zeros_like(dgf_ref)

        dx1, dgf = _rms_bwd_rows(x_ref[...], gf_ref[...], dh_ref[...], dx2_ref[...])
        dx1_ref[...] = dx1
        dgf_ref[...] += dgf
        dxb = dx1.astype(BF16)
        g = g_ref[...]
        yn = yn_ref[...]
        yg = (yn * g).astype(BF16)
        for b in range(N_BLK):
            cols = slice(256 * b, 256 * (b + 1))
            dyg = _dot_nt(dxb, w_ref[b])
            dw_ref[b] += _dot_tn(yg[:, cols], dxb)
            dg_ref[:, cols] += jnp.sum(dyg * yn[:, cols], axis=0, keepdims=True)
            dyn_ref[:, cols] = dyg * g[:, cols]

    row = pl.BlockSpec((t, D_MODEL), lambda i: (i, 0))
    vec = _full((1, D_MODEL))
    return pl.pallas_call(
        body, name="out_proj_bwd", grid=(s // t,),
        in_specs=[row, vec, row, row, row, vec, pl.BlockSpec((N_BLK, None, D_GROUP, D_MODEL), lambda i: (0, layer, 0, 0))],
        out_specs=[row, vec, row, vec, _full((N_BLK, D_GROUP, D_MODEL))],
        out_shape=[jax.ShapeDtypeStruct((s, D_MODEL), F32), jax.ShapeDtypeStruct((1, D_MODEL), F32),
                   jax.ShapeDtypeStruct((s, D_MODEL), F32), jax.ShapeDtypeStruct((1, D_MODEL), F32),
                   jax.ShapeDtypeStruct((N_BLK, D_GROUP, D_MODEL), F32)],
        compiler_params=_params(("arbitrary",), V7X_VMEM_LIMIT),
    )(x1, g_ffn, dh, dx2, yn, gain, w)


def _ffn(x, g, w_up, w_down, layer):
    s = x.shape[0]
    t = _tile(s, 1024)

    def body(x_ref, g_ref, wu_ref, wd_ref, x2_ref, p_ref, h_ref):
        @pl.when(pl.program_id(1) == 0)
        def _():
            xv = x_ref[...]
            h_ref[...] = (xv * _rsqrt_mean(xv) * g_ref[...]).astype(BF16)
            x2_ref[...] = xv

        pre = _dot(h_ref[...], wu_ref[...])
        p_ref[...] = pre.astype(BF16)
        a = jnp.maximum(pre, 0.0)
        x2_ref[...] += _dot((a * a).astype(BF16), wd_ref[...])

    wspec = pl.BlockSpec((None, None, D_MODEL, D_FF_BLK), lambda i, j: (j, layer, 0, 0))
    row = pl.BlockSpec((t, D_MODEL), lambda i, j: (i, 0))
    return pl.pallas_call(
        body, name="ffn", grid=(s // t, N_BLK),
        in_specs=[row, pl.BlockSpec((1, D_MODEL), lambda i, j: (0, 0)), wspec, wspec],
        out_specs=[row, pl.BlockSpec((t, D_FF_BLK), lambda i, j: (i, j)), row],
        out_shape=[jax.ShapeDtypeStruct((s, D_MODEL), F32), jax.ShapeDtypeStruct((s, N_BLK * D_FF_BLK), BF16),
                   jax.ShapeDtypeStruct((s, D_MODEL), BF16)],
        compiler_params=_params(("parallel", "arbitrary"), V7X_VMEM_LIMIT),
    )(x, g, w_up, w_down)


def _rms_bwd_rows(xv, g, dh, dres):
    r = _rsqrt_mean(xv)
    xh = xv * r
    dxh = dh * g
    dx = dres + r * (dxh - xh * jnp.mean(dxh * xh, axis=-1, keepdims=True))
    return dx, jnp.sum(dh * xh, axis=0, keepdims=True)


def _ffn_bwd(dxb, p, w_up, w_down, layer, sums=()):
    s = dxb.shape[0]
    t = _tile(s, 1024)
    ns = len(sums)

    def body(dx_ref, p_ref, wu_ref, wd_ref, *rest):
        dp_ref, dh_ref = rest[ns:ns + 2]
        if ns:
            i, j = pl.program_id(0), pl.program_id(1)
            _owners_in_steps(rest[:ns], rest[ns + 2:2 * ns + 2], rest[2 * ns + 2:],
                             jnp.logical_and(i == 0, j == 0), jnp.logical_and(i == s // t - 1, j == N_BLK - 1))
        da = _dot_nt(dx_ref[...], wd_ref[...])
        a = jnp.maximum(p_ref[...].astype(F32), 0.0)
        dp = (da * (2.0 * a)).astype(BF16)
        dp_ref[...] = dp
        dh = _dot_nt(dp, wu_ref[...])

        @pl.when(pl.program_id(1) == 0)
        def _():
            dh_ref[...] = dh

        @pl.when(pl.program_id(1) != 0)
        def _():
            dh_ref[...] += dh

    wspec = pl.BlockSpec((None, None, D_MODEL, D_FF_BLK), lambda i, j: (j, layer, 0, 0))
    row = pl.BlockSpec((t, D_MODEL), lambda i, j: (i, 0))
    blk = pl.BlockSpec((t, D_FF_BLK), lambda i, j: (i, j))
    out = pl.pallas_call(
        body, name="ffn_bwd", grid=(s // t, N_BLK),
        in_specs=[row, blk, wspec, wspec] + [_ANY] * ns, out_specs=[blk, row] + [_ANY] * ns,
        out_shape=[jax.ShapeDtypeStruct((s, N_BLK * D_FF_BLK), BF16), jax.ShapeDtypeStruct((s, D_MODEL), F32)]
                  + [jax.ShapeDtypeStruct(a.shape, a.dtype) for a in sums],
        scratch_shapes=_owner_sems(ns) if ns else [],
        compiler_params=_params(("arbitrary" if ns else "parallel", "arbitrary"), V7X_VMEM_LIMIT),
    )(dxb, p, w_up, w_down, *sums)
    return (out[0], out[1], out[2:]) if ns else out


def _ffn_wgrad(hb, p, dp, dxb):
    s = hb.shape[0]
    t = _tile(s, 1024)

    def body(h_ref, p_ref, dp_ref, dx_ref, du_ref, dd_ref):
        @pl.when(pl.program_id(1) == 0)
        def _():
            du_ref[...] = jnp.zeros_like(du_ref)
            dd_ref[...] = jnp.zeros_like(dd_ref)

        a = jnp.maximum(p_ref[...].astype(F32), 0.0)
        du_ref[...] += _dot_tn(h_ref[...], dp_ref[...])
        dd_ref[...] += _dot_tn((a * a).astype(BF16), dx_ref[...])

    row = pl.BlockSpec((t, D_MODEL), lambda j, i: (i, 0))
    blk = pl.BlockSpec((t, D_FF_BLK), lambda j, i: (i, j))
    out = pl.BlockSpec((None, D_MODEL, D_FF_BLK), lambda j, i: (j, 0, 0))
    shape = jax.ShapeDtypeStruct((N_BLK, D_MODEL, D_FF_BLK), F32)
    return pl.pallas_call(
        body, name="ffn_wgrad", grid=(N_BLK, s // t),
        in_specs=[row, blk, blk, row], out_specs=[out, out], out_shape=[shape, shape],
        compiler_params=_params(("parallel", "arbitrary"), V7X_VMEM_LIMIT),
    )(hb, p, dp, dxb)


def _in_proj_bwd(x, g, dx1, dz, w, layer):
    s = x.shape[0]
    t = _tile(s, 512)

    def body(x_ref, g_ref, dx1_ref, dz_ref, w_ref, dx0_ref, dxb_ref, dg_ref):
        @pl.when(pl.program_id(0) == 0)
        def _():
            dg_ref[...] = jnp.zeros_like(dg_ref)

        dh = _dot_nt(dz_ref[:, 0:W_IN_BLK], w_ref[0])
        for b in range(1, N_BLK):
            dh = dh + _dot_nt(dz_ref[:, b * W_IN_BLK:(b + 1) * W_IN_BLK], w_ref[b])
        dx, dg = _rms_bwd_rows(x_ref[...], g_ref[...], dh, dx1_ref[...])
        dx0_ref[...] = dx
        dxb_ref[...] = dx.astype(BF16)
        dg_ref[...] += dg

    row = pl.BlockSpec((t, D_MODEL), lambda i: (i, 0))
    return pl.pallas_call(
        body, name="in_proj_bwd", grid=(s // t,),
        in_specs=[row, _full((1, D_MODEL)), row, pl.BlockSpec((t, D_IN), lambda i: (i, 0)),
                  pl.BlockSpec((N_BLK, None, D_MODEL, W_IN_BLK), lambda i: (0, layer, 0, 0))],
        out_specs=[row, row, _full((1, D_MODEL))],
        out_shape=[jax.ShapeDtypeStruct((s, D_MODEL), F32), jax.ShapeDtypeStruct((s, D_MODEL), BF16),
                   jax.ShapeDtypeStruct((1, D_MODEL), F32)],
        compiler_params=_params(("arbitrary",), V7X_VMEM_LIMIT),
    )(x, g, dx1, dz, w)


def _in_proj_wgrad(hb, dz):
    s = hb.shape[0]
    t = _tile(s, 512)

    def body(h_ref, dz_ref, dw_ref):
        @pl.when(pl.program_id(0) == 0)
        def _():
            dw_ref[...] = jnp.zeros_like(dw_ref)

        h = h_ref[...]
        for b in range(N_BLK):
            dw_ref[b] += _dot_tn(h, dz_ref[:, b * W_IN_BLK:(b + 1) * W_IN_BLK])

    return pl.pallas_call(
        body, name="in_proj_wgrad", grid=(s // t,),
        in_specs=[pl.BlockSpec((t, D_MODEL), lambda i: (i, 0)), pl.BlockSpec((t, D_IN), lambda i: (i, 0))],
        out_specs=_full((N_BLK, D_MODEL, W_IN_BLK)),
        out_shape=jax.ShapeDtypeStruct((N_BLK, D_MODEL, W_IN_BLK), F32),
        compiler_params=_params(("arbitrary",), V7X_VMEM_LIMIT),
    )(hb, dz)


def _loss_head(x, g, target):
    s = x.shape[0]
    t = _tile(s, 512)

    def body(x_ref, g_ref, t_ref, l_ref, dx_ref, dxb_ref, dg_ref):
        @pl.when(pl.program_id(0) == 0)
        def _():
            l_ref[...] = jnp.zeros_like(l_ref)
            dg_ref[...] = jnp.zeros_like(dg_ref)

        xv = x_ref[...]
        g = g_ref[...]
        r = _rsqrt_mean(xv)
        xh = xv * r
        err = xh * g - t_ref[...]
        l_ref[...] += 0.5 * jnp.sum(jnp.mean(err * err, axis=-1, keepdims=True), axis=0, keepdims=True)
        dy = err * (1.0 / D_MODEL)
        dg_ref[...] += jnp.sum(dy * xh, axis=0, keepdims=True)
        dxh = dy * g
        dx = r * (dxh - xh * jnp.mean(dxh * xh, axis=-1, keepdims=True))
        dx_ref[...] = dx
        dxb_ref[...] = dx.astype(BF16)

    row = pl.BlockSpec((t, D_MODEL), lambda i: (i, 0))
    return pl.pallas_call(
        body, name="loss_head", grid=(s // t,),
        in_specs=[row, _full((1, D_MODEL)), row],
        out_specs=[_full((1, 128)), row, row, _full((1, D_MODEL))],
        out_shape=[jax.ShapeDtypeStruct((1, 128), F32), jax.ShapeDtypeStruct((s, D_MODEL), F32),
                   jax.ShapeDtypeStruct((s, D_MODEL), BF16), jax.ShapeDtypeStruct((1, D_MODEL), F32)],
        compiler_params=_params(("arbitrary",)),
    )(x, g, target)


def _layer_params(small, layer):
    tril = jnp.tril(jnp.ones((CHUNK, CHUNK), bool))
    ws = jnp.where(tril, small["gmlp_w_s"][layer], 0.0)
    bmat = jnp.repeat(small["gmlp_b_s"][layer].T, HEAD_DIM, axis=1)
    scw = jnp.zeros((8, D_GROUP), F32).at[:SHORT_K].set(small["short_conv_w"][layer])
    ccw = jnp.zeros((32, D_GROUP), F32).at[:CONF_K].set(small["conf_conv_w"][layer])
    return dict(vg=small["gmlp_v_g"][layer][None], wt=ws.astype(BF16), wtt=jnp.swapaxes(ws, 1, 2).astype(BF16),
                bmat=bmat, scw=scw, ccw=ccw, lg=small["conf_ln_g"][layer][None], lb=small["conf_ln_b"][layer][None])


def _local_step(x, target, big, small, gather_pending=False, core=None):
    saved = []
    for l in range(DEPTH):
        p = _layer_params(small, l)
        z, hb, q_r, q_t, k_t, v_t = _in_proj(x, small["norm_mix_g"][l][None], big["w_in"], l)
        if gather_pending and l == 0:
            late = ("w_out", "w_up", "w_down")
            y_abd, filled = _mixers_fwd(z, p, [big[k] for k in late], [(n, 0) for n in range(len(late))])
            big = {**big, **dict(zip(late, filled))}
            o_t, filled = _attn_fwd(q_r, k_t, v_t, [big[k] for k in _BIG], [(n, 1) for n in range(len(_BIG))])
            big = dict(zip(_BIG, filled))
        else:
            y_abd = _mixers_fwd(z, p)
            o_t = _attn_fwd(q_r, k_t, v_t)
        x1, yn = _out_proj(x, y_abd, o_t, small["mix_out_g"][l][None], big["w_out"], l)
        x2, pre, h2b = _ffn(x1, small["norm_ffn_g"][l][None], big["w_up"], big["w_down"], l)
        saved.append(dict(p=p, x0=x, z=z, hb=hb, q_r=q_r, q_t=q_t, k_t=k_t, v_t=v_t, o_t=o_t, x1=x1, yn=yn, pre=pre,
                          h2b=h2b))
        x = x2

    loss, dx, dxb, d_final = _loss_head(x, small["final_norm_g"][None], target)

    g = {k: [None] * DEPTH for k in ("w_in", "w_out", "w_up", "w_down", "norm_mix_g", "gmlp_v_g", "gmlp_w_s", "gmlp_b_s",
                                     "short_conv_w", "conf_conv_w", "conf_ln_g", "conf_ln_b", "mix_out_g", "norm_ffn_g")}
    tril = jnp.tril(jnp.ones((CHUNK, CHUNK), bool))
    early = {}
    for l in reversed(range(DEPTH)):
        sv = saved[l]
        p = sv["p"]
        riding = []
        if core is not None and l == 0:
            riding = [(k, 1) for k in _BIG]
        sums = _chip_sums(core, [g[k][n] for k, n in riding])
        dpre, dh, *got = _ffn_bwd(dxb, sv["pre"], big["w_up"], big["w_down"], l, sums)
        early.update(zip(riding, zip(sums, *got)))
        g["w_up"][l], g["w_down"][l] = _ffn_wgrad(sv["h2b"], sv["pre"], dpre, dxb)
        dx1, g["norm_ffn_g"][l], dyn, g["mix_out_g"][l], g["w_out"][l] = _out_proj_bwd(
            sv["x1"], small["norm_ffn_g"][l][None], dh, dx, sv["yn"], small["mix_out_g"][l][None], big["w_out"], l)
        if core is not None and l == 0:
            riding = [("w_up", 0), ("w_down", 0)]
        sums = _chip_sums(core, [g[k][n] for k, n in riding])
        (dza, dcb, dcd, do_r, do_t, dsum, dvg, dws, dbm, dscw, dccw, dlg, dlb, *got) = _mixers_bwd_a(
            sv["z"], dyn, sv["o_t"], p, sums)
        early.update(zip(riding, zip(sums, *got)))
        dq_t, dk_t, dv_t = _attn_bwd(sv["q_r"], sv["q_t"], sv["k_t"], sv["v_t"], do_r, do_t, dsum)
        dz = _mixers_bwd_b(sv["z"], dza, dcb, dcd, dq_t, dk_t, dv_t, p)
        dx, dxb, g["norm_mix_g"][l] = _in_proj_bwd(sv["x0"], small["norm_mix_g"][l][None], dx1, dz, big["w_in"], l)
        g["w_in"][l] = _in_proj_wgrad(sv["hb"], dz)
        g["gmlp_v_g"][l] = dvg[0]
        g["gmlp_w_s"][l] = jnp.where(tril, dws, 0.0)
        g["gmlp_b_s"][l] = dbm.reshape(CHUNK, N_HEADS, HEAD_DIM).sum(-1).T
        g["short_conv_w"][l] = dscw[:SHORT_K]
        g["conf_conv_w"][l] = dccw[:CONF_K]
        g["conf_ln_g"][l] = dlg[0]
        g["conf_ln_b"][l] = dlb[0]
        g["norm_mix_g"][l] = g["norm_mix_g"][l][0]
        g["mix_out_g"][l] = g["mix_out_g"][l][0]
        g["norm_ffn_g"][l] = g["norm_ffn_g"][l][0]
    grads = {k: v if k in ("w_in", "w_out", "w_up", "w_down") else jnp.stack(v) for k, v in g.items()}
    grads["final_norm_g"] = d_final[0]
    return loss, dx, grads, early


_ANY = pl.BlockSpec(memory_space=pl.ANY)


def _mesh_place():
    x, y, c = lax.axis_index("x"), lax.axis_index("y"), lax.axis_index("c")
    chips = [(1 - x, y), (x, 1 - y), (1 - x, 1 - y)]
    return x, y, c, 2 * x + y, chips


def _gather_stages(bufs, parts, sems):
    ici_send, ici_recv, d2d_send, d2d_recv = sems
    x, y, c, me, chips = _mesh_place()
    blk = [2 * chip[0] + chip[1] for chip in chips]
    pairs = [(p, r) for p in range(len(parts)) for r in range(3)]

    def rows(p, block, half_of):
        k, layer = parts[p]
        half = bufs[k].shape[2] // 2
        return bufs[k].at[block, layer, pl.ds(half_of * half, half), :]

    def ici(p, r, block):
        return pltpu.make_async_remote_copy(
            src_ref=rows(p, me, c), dst_ref=rows(p, block, c), send_sem=ici_send.at[3 * p + r],
            recv_sem=ici_recv.at[3 * p + r], device_id=(chips[r][0], chips[r][1], c), device_id_type=MESH)

    def d2d(p, r, half_of):
        part = rows(p, blk[r], half_of)
        return pltpu.make_async_remote_copy(
            src_ref=part, dst_ref=part, send_sem=d2d_send.at[3 * p + r], recv_sem=d2d_recv.at[3 * p + r],
            device_id=(x, y, 1 - c), device_id_type=MESH)

    def start():
        for p, r in pairs:
            ici(p, r, me).start()

    def forward(p):
        for r in range(3):
            ici(p, r, blk[r]).wait_recv()
            d2d(p, r, c).start()

    def finish():
        for p, r in pairs:
            d2d(p, r, 1 - c).wait_recv()
        for p, r in pairs:
            ici(p, r, me).wait_send()
            d2d(p, r, c).wait_send()

    return start, forward, finish


def _gather_sems(parts):
    return [pltpu.SemaphoreType.DMA((3 * len(parts),)) for _ in range(4)]


def _gather_in_steps(bufs, parts, sems, n_steps):
    start, forward, finish = _gather_stages(bufs, parts, sems)
    i = pl.program_id(0)
    pl.when(i == 0)(start)
    for p in range(len(parts)):
        pl.when(i == n_steps * (2 * p + 3) // (2 * len(parts) + 2))(lambda p=p: forward(p))
    pl.when(i == n_steps - 1)(finish)


def _gather_first(bufs, parts, whole):
    n, m = len(bufs), len(whole)

    def body(*refs):
        whole_in, buf_out, whole_out = refs[n:n + m], refs[n + m:2 * n + m], refs[2 * n + m:2 * (n + m)]
        sems = refs[2 * (n + m):]
        send_sems, recv_sems, local_sems = sems[4:]
        x, y, c, me, chips = _mesh_place()
        start, forward, finish = _gather_stages(buf_out, parts, sems[:4])

        def push(k, r, block):
            return pltpu.make_async_remote_copy(
                src_ref=whole_in[k], dst_ref=whole_out[k].at[block], send_sem=send_sems.at[3 * k + r],
                recv_sem=recv_sems.at[3 * k + r], device_id=(chips[r][0], chips[r][1], c), device_id_type=MESH)

        local = [pltpu.make_async_copy(whole_in[k], whole_out[k].at[me], local_sems.at[k]) for k in range(m)]
        for cp in local:
            cp.start()
        start()
        for k in range(m):
            for r in range(3):
                push(k, r, me).start()
        for p in range(len(parts)):
            forward(p)
        for k in range(m):
            for r, chip in enumerate(chips):
                push(k, r, 2 * chip[0] + chip[1]).wait_recv()
        for k in range(m):
            for r in range(3):
                push(k, r, me).wait_send()
        finish()
        for cp in local:
            cp.wait()

    return pl.pallas_call(
        body, name="gather_first",
        in_specs=[_ANY] * (n + m), out_specs=[_ANY] * (n + m),
        out_shape=[jax.ShapeDtypeStruct(b.shape, b.dtype) for b in bufs]
                  + [jax.ShapeDtypeStruct((N_BLK,) + b.shape, b.dtype) for b in whole],
        input_output_aliases={k: k for k in range(n)},
        scratch_shapes=_gather_sems(parts) + [pltpu.SemaphoreType.DMA((3 * m,)), pltpu.SemaphoreType.DMA((3 * m,)),
                                              pltpu.SemaphoreType.DMA((m,))],
    )(*bufs, *whole)


def _swap_halves(gs):
    n = len(gs)

    def body(*refs):
        ins, outs, (send_sems, recv_sems) = refs[:n], refs[n:2 * n], refs[2 * n:]
        x, y, c, _, _ = _mesh_place()
        cps = []
        for k in range(n):
            half = ins[k].shape[1] // 2
            cps.append(pltpu.make_async_remote_copy(
                src_ref=ins[k].at[:, pl.ds((1 - c) * half, half), :], dst_ref=outs[k],
                send_sem=send_sems.at[k], recv_sem=recv_sems.at[k], device_id=(x, y, 1 - c), device_id_type=MESH))
        for cp in cps:
            cp.start()
        for cp in cps:
            cp.wait()

    return pl.pallas_call(
        body, name="swap_halves", in_specs=[_ANY] * n, out_specs=[_ANY] * n,
        out_shape=[jax.ShapeDtypeStruct((g.shape[0], g.shape[1] // 2, g.shape[2]), F32) for g in gs],
        scratch_shapes=[pltpu.SemaphoreType.DMA((n,)), pltpu.SemaphoreType.DMA((n,))],
    )(*gs)


def _owner_stages(ins, outs, sems):
    send_sems, recv_sems = sems
    x, y, c, me, chips = _mesh_place()
    pairs = [(k, r) for k in range(len(ins)) for r in range(3)]

    def remote(k, r, src_block, dst_block):
        return pltpu.make_async_remote_copy(
            src_ref=ins[k].at[src_block], dst_ref=outs[k].at[dst_block], send_sem=send_sems.at[3 * k + r],
            recv_sem=recv_sems.at[3 * k + r], device_id=(chips[r][0], chips[r][1], c), device_id_type=MESH)

    def start():
        for k, r in pairs:
            remote(k, r, 2 * chips[r][0] + chips[r][1], me).start()

    def finish():
        for k, r in pairs:
            remote(k, r, me, 2 * chips[r][0] + chips[r][1]).wait_recv()
        for k, r in pairs:
            remote(k, r, 2 * chips[r][0] + chips[r][1], me).wait_send()

    return start, finish


def _owner_sems(n):
    return [pltpu.SemaphoreType.DMA((3 * n,)), pltpu.SemaphoreType.DMA((3 * n,))]


def _owners_in_steps(ins, outs, sems, first, last):
    start, finish = _owner_stages(ins, outs, sems)
    pl.when(first)(start)
    pl.when(last)(finish)


def _send_to_owners(sums):
    n = len(sums)

    def body(*refs):
        start, finish = _owner_stages(refs[:n], refs[n:2 * n], refs[2 * n:])
        start()
        finish()

    return pl.pallas_call(
        body, name="send_to_owners", in_specs=[_ANY] * n, out_specs=[_ANY] * n,
        out_shape=[jax.ShapeDtypeStruct(s.shape, s.dtype) for s in sums],
        scratch_shapes=_owner_sems(n),
    )(*sums)


def _swap_reduced(fs):
    n = len(fs)

    def body(*refs):
        ins, outs, (send_sems, recv_sems) = refs[:n], refs[n:2 * n], refs[2 * n:]
        x, y, c, _, _ = _mesh_place()
        cps = [pltpu.make_async_remote_copy(src_ref=ins[k], dst_ref=outs[k], send_sem=send_sems.at[k],
                                            recv_sem=recv_sems.at[k], device_id=(x, y, 1 - c), device_id_type=MESH)
               for k in range(n)]
        for cp in cps:
            cp.start()
        for cp in cps:
            cp.wait()

    return pl.pallas_call(
        body, name="swap_reduced", in_specs=[_ANY] * n, out_specs=[_ANY] * n,
        out_shape=[jax.ShapeDtypeStruct(f.shape, F32) for f in fs],
        scratch_shapes=[pltpu.SemaphoreType.DMA((n,)), pltpu.SemaphoreType.DMA((n,))],
    )(*fs)


def _row_tile(rows):
    return min(rows, 256)


def _chip_sums(core, grads):
    if not grads:
        return []
    return [_add_pairs(core, a, b) for a, b in zip(grads, _swap_halves(grads))]


def _add_pairs(core, g, other):
    n, half, cols = other.shape
    t = _row_tile(half)
    per_half = half // t

    def body(c_ref, a_ref, b_ref, o_ref):
        o_ref[...] = (a_ref[...] + b_ref[...]).astype(BF16)

    spec = pl.BlockSpec((None, t, cols), lambda i, j, c_ref: (i, j, 0))
    return pl.pallas_call(
        body, name="add_pairs",
        grid_spec=pltpu.PrefetchScalarGridSpec(
            num_scalar_prefetch=1, grid=(n, per_half),
            in_specs=[pl.BlockSpec((None, t, cols), lambda i, j, c_ref: (i, c_ref[0] * per_half + j, 0)), spec],
            out_specs=spec),
        out_shape=jax.ShapeDtypeStruct(other.shape, BF16), compiler_params=_params(("parallel", "parallel")),
    )(core, g, other)


def _add_chips(me, s1, r2):
    _, r, cols = r2.shape
    t = _row_tile(r)

    def body(me_ref, s_ref, r_ref, o_ref):
        own = s_ref[...].astype(F32)
        parts = [jnp.where(me_ref[0] == k, own, r_ref[k].astype(F32)) for k in range(N_BLK)]
        o_ref[...] = ((parts[0] + parts[1]) + parts[2]) + parts[3]

    return pl.pallas_call(
        body, name="add_chips",
        grid_spec=pltpu.PrefetchScalarGridSpec(
            num_scalar_prefetch=1, grid=(r // t,),
            in_specs=[pl.BlockSpec((None, t, cols), lambda i, me_ref: (me_ref[0], i, 0)),
                      pl.BlockSpec((N_BLK, t, cols), lambda i, me_ref: (0, i, 0))],
            out_specs=pl.BlockSpec((t, cols), lambda i, me_ref: (i, 0))),
        out_shape=jax.ShapeDtypeStruct((r, cols), F32), compiler_params=_params(("parallel",)),
    )(me, s1, r2)


def _adamw(core, mine, other, w, m, v, layer, earlier=None):
    half, cols = mine.shape
    t = _row_tile(half)
    per_half = half // t
    c1 = 1.0 - ADAM_B1 ** ADAM_STEP
    c2 = 1.0 - ADAM_B2 ** ADAM_STEP

    def body(c_ref, a_ref, b_ref, w_ref, m_ref, v_ref, *rest):
        g_ref, d_ref, mo_ref, vo_ref = rest[-4:]
        gv = jnp.where(pl.program_id(0) // per_half == c_ref[0], a_ref[...], b_ref[...])
        g_ref[...] = gv
        m_new = ADAM_B1 * m_ref[...] + (1.0 - ADAM_B1) * gv
        v_new = ADAM_B2 * v_ref[...] + (1.0 - ADAM_B2) * (gv * gv)
        mo_ref[...] = m_new
        vo_ref[...] = v_new
        d_ref[...] = -ADAM_LR * ((m_new / c1) / (jnp.sqrt(v_new / c2) + ADAM_EPS) + ADAM_WD * w_ref[...])

    part = pl.BlockSpec((t, cols), lambda i, c_ref: (i % per_half, 0))
    spec = pl.BlockSpec((None, t, cols), lambda i, c_ref: (layer, i, 0))
    kept = [] if earlier is None else list(earlier)
    return pl.pallas_call(
        body, name="adamw",
        grid_spec=pltpu.PrefetchScalarGridSpec(
            num_scalar_prefetch=1, grid=(2 * per_half,),
            in_specs=[part, part, spec, spec, spec] + [_ANY] * len(kept), out_specs=[spec] * 4),
        out_shape=[jax.ShapeDtypeStruct(w.shape, F32)] * 4,
        input_output_aliases={6 + k: k for k in range(len(kept))},
        compiler_params=_params(("parallel",)),
    )(core, mine, other, w, m, v, *kept)


_REPLICATED = ("norm_mix_g", "gmlp_v_g", "gmlp_w_s", "gmlp_b_s", "conf_ln_g", "conf_ln_b", "mix_out_g", "norm_ffn_g",
               "final_norm_g")
_REP_SHAPES = {"norm_mix_g": (DEPTH, D_MODEL), "gmlp_v_g": (DEPTH, D_GROUP), "gmlp_w_s": (DEPTH, N_HEADS, CHUNK, CHUNK),
               "gmlp_b_s": (DEPTH, N_HEADS, CHUNK), "conf_ln_g": (DEPTH, D_GROUP), "conf_ln_b": (DEPTH, D_GROUP),
               "mix_out_g": (DEPTH, D_MODEL), "norm_ffn_g": (DEPTH, D_MODEL), "final_norm_g": (D_MODEL,)}
_BIG = ("w_in", "w_out", "w_up", "w_down")
_CONV_ROWS = 8
_REP_ROWS = 144
_SMALL_ROWS = 160
_CH_BLK = D_GROUP // N_BLK


def _pad_rows(flat, rows):
    pad = rows * D_MODEL - flat.shape[-1]
    flat = jnp.pad(flat, [(0, 0)] * (flat.ndim - 1) + [(0, pad)])
    return flat.reshape(flat.shape[:-1] + (rows, D_MODEL))


def _pack_small(scw, ccw, rep):
    lead = scw.shape[:-3]
    conv = jnp.concatenate([scw.reshape(lead + (-1,)), ccw.reshape(lead + (-1,))], axis=-1)
    flat = jnp.concatenate([rep[k].reshape(-1) for k in _REPLICATED])
    flat = jnp.broadcast_to(flat, lead + flat.shape)
    parts = [_pad_rows(conv, _CONV_ROWS), _pad_rows(flat, _REP_ROWS),
             jnp.zeros(lead + (_SMALL_ROWS - _CONV_ROWS - _REP_ROWS, D_MODEL), F32)]
    return jnp.concatenate(parts, axis=-2)


def _unpack_small(pk):
    out = {}
    conv = pk[:_CONV_ROWS].reshape(-1)
    n_s = DEPTH * SHORT_K * _CH_BLK
    out["short_conv_w"] = conv[:n_s].reshape(DEPTH, SHORT_K, _CH_BLK)
    out["conf_conv_w"] = conv[n_s:n_s + DEPTH * CONF_K * _CH_BLK].reshape(DEPTH, CONF_K, _CH_BLK)
    row = _CONV_ROWS
    flat = pk[row:row + _REP_ROWS].reshape(-1)
    at = 0
    for k in _REPLICATED:
        n = math.prod(_REP_SHAPES[k])
        out[k] = flat[at:at + n].reshape(_REP_SHAPES[k])
        at += n
    return out


def _conv_blocks(w):
    d, k, _ = w.shape
    return w.reshape(d, k, N_BLK, _CH_BLK).transpose(2, 0, 1, 3)


_WEIGHTS = ("norm_mix_g", "w_in", "gmlp_v_g", "gmlp_w_s", "gmlp_b_s", "short_conv_w", "conf_conv_w", "conf_ln_g",
            "conf_ln_b", "mix_out_g", "w_out", "norm_ffn_g", "w_up", "w_down", "final_norm_g")


def kernel(x, norm_mix_g, w_in, gmlp_v_g, gmlp_w_s, gmlp_b_s, short_conv_w, conf_conv_w, conf_ln_g, conf_ln_b, mix_out_g, w_out, norm_ffn_g, w_up, w_down, final_norm_g, loss_target, m_norm_mix_g, m_w_in, m_gmlp_v_g, m_gmlp_w_s, m_gmlp_b_s, m_short_conv_w, m_conf_conv_w, m_conf_ln_g, m_conf_ln_b, m_mix_out_g, m_w_out, m_norm_ffn_g, m_w_up, m_w_down, m_final_norm_g, v_norm_mix_g, v_w_in, v_gmlp_v_g, v_gmlp_w_s, v_gmlp_b_s, v_short_conv_w, v_conf_conv_w, v_conf_ln_g, v_conf_ln_b, v_mix_out_g, v_w_out, v_norm_ffn_g, v_w_up, v_w_down, v_final_norm_g):
    w = dict(norm_mix_g=norm_mix_g, w_in=w_in, gmlp_v_g=gmlp_v_g, gmlp_w_s=gmlp_w_s, gmlp_b_s=gmlp_b_s,
             short_conv_w=short_conv_w, conf_conv_w=conf_conv_w, conf_ln_g=conf_ln_g, conf_ln_b=conf_ln_b,
             mix_out_g=mix_out_g, w_out=w_out, norm_ffn_g=norm_ffn_g, w_up=w_up, w_down=w_down, final_norm_g=final_norm_g)
    m = dict(norm_mix_g=m_norm_mix_g, w_in=m_w_in, gmlp_v_g=m_gmlp_v_g, gmlp_w_s=m_gmlp_w_s, gmlp_b_s=m_gmlp_b_s,
             short_conv_w=m_short_conv_w, conf_conv_w=m_conf_conv_w, conf_ln_g=m_conf_ln_g, conf_ln_b=m_conf_ln_b,
             mix_out_g=m_mix_out_g, w_out=m_w_out, norm_ffn_g=m_norm_ffn_g, w_up=m_w_up, w_down=m_w_down,
             final_norm_g=m_final_norm_g)
    v = dict(norm_mix_g=v_norm_mix_g, w_in=v_w_in, gmlp_v_g=v_gmlp_v_g, gmlp_w_s=v_gmlp_w_s, gmlp_b_s=v_gmlp_b_s,
             short_conv_w=v_short_conv_w, conf_conv_w=v_conf_conv_w, conf_ln_g=v_conf_ln_g, conf_ln_b=v_conf_ln_b,
             mix_out_g=v_mix_out_g, w_out=v_w_out, norm_ffn_g=v_norm_ffn_g, w_up=v_w_up, w_down=v_w_down,
             final_norm_g=v_final_norm_g)
    core = lax.axis_index("c").astype(jnp.int32).reshape(1)
    me = (2 * lax.axis_index("x") + lax.axis_index("y")).astype(jnp.int32).reshape(1)

    conv_mine = _pad_rows(jnp.concatenate([short_conv_w.reshape(-1), conf_conv_w.reshape(-1)]), _CONV_ROWS)
    big = {k: _cast_into_slot(w[k], me, "cast_" + k) for k in _BIG}
    big["w_in"], conv_all = _gather_first([big["w_in"]], [(0, 0)], [conv_mine])
    conv_all = conv_all.reshape(N_BLK, -1)
    n_s = DEPTH * SHORT_K * _CH_BLK
    scw_all = conv_all[:, :n_s].reshape(N_BLK, DEPTH, SHORT_K, _CH_BLK)
    ccw_all = conv_all[:, n_s:n_s + DEPTH * CONF_K * _CH_BLK].reshape(N_BLK, DEPTH, CONF_K, _CH_BLK)
    small = {k: w[k] for k in _REPLICATED}
    small["short_conv_w"] = scw_all.transpose(1, 2, 0, 3).reshape(DEPTH, SHORT_K, D_GROUP)
    small["conf_conv_w"] = ccw_all.transpose(1, 2, 0, 3).reshape(DEPTH, CONF_K, D_GROUP)

    loss, dx, g, early = _local_step(x[0], loss_target[0], big, small, gather_pending=True, core=core)

    where = [(k, l) for k in _BIG for l in range(DEPTH)]
    late = [kl for kl in where if kl not in early]
    sums = _chip_sums(core, [g[k][l] for k, l in late]
                      + [_pack_small(_conv_blocks(g["short_conv_w"]), _conv_blocks(g["conf_conv_w"]), g)])
    sent = {**early, **dict(zip(late + ["small"], zip(sums, _send_to_owners(sums))))}
    mine = [_add_chips(me, *sent[kl]) for kl in where + ["small"]]
    other = _swap_reduced(mine)

    done = {}
    for n, (k, l) in enumerate(where):
        done[k] = _adamw(core, mine[n], other[n], w[k], m[k], v[k], l, done.get(k))
    small_own = [_pack_small(t["short_conv_w"], t["conf_conv_w"], t)[None] for t in (w, m, v)]
    small_done = [_unpack_small(a[0]) for a in _adamw(core, mine[-1], other[-1], *small_own, 0)]

    outs = [lax.psum(loss[0, 0], ("x", "y", "c")), dx[None]]
    for kind in range(4):
        outs += [done[k][kind] if k in _BIG else small_done[kind][k] for k in _WEIGHTS]
    return tuple(outs)
```

```python
import math

import jax
import jax.numpy as jnp
from jax import lax
from jax.experimental import pallas as pl
from jax.experimental.pallas import tpu as pltpu

F32 = jnp.float32
BF16 = jnp.bfloat16

D_MODEL = 1024
D_GROUP = 256
N_HEADS = 4
HEAD_DIM = 64
CHUNK = 128
D_IN = 2560
N_BLK = 4
W_IN_BLK = D_IN // N_BLK
D_FF_BLK = 1024
DEPTH = 2
EPS = 1e-6
HALO = 32
SHORT_K = 3
CONF_K = 31
ATT_TQ = 256
ATT_TK = 256
ATT_SCALE = 0.125
ATT_DEAD = -104.0
V7X_VMEM_LIMIT = 56 * 1024 * 1024

ADAM_LR, ADAM_B1, ADAM_B2, ADAM_EPS, ADAM_WD, ADAM_STEP = 0.001, 0.9, 0.999, 1e-08, 0.01, 10

MESH = pl.DeviceIdType.MESH


def _params(sem, vmem=None):
    return pltpu.CompilerParams(dimension_semantics=sem, vmem_limit_bytes=vmem)


def _tile(s, t):
    return min(s, t)


def _rsqrt_mean(v):
    return lax.rsqrt(jnp.mean(v * v, axis=-1, keepdims=True) + EPS)


def _sigmoid(v):
    return 1.0 / (1.0 + jnp.exp(-v))


_GELU_C = math.sqrt(2.0 / math.pi)


def _gelu_tanh(v):
    return jnp.tanh(_GELU_C * (v + 0.044715 * (v * v * v)))


def _gelu(v, t):
    return v * (0.5 * (1.0 + t))


def _gelu_grad(v, t):
    return 0.5 * (1.0 + t) + v * (0.5 * (1.0 - t * t) * _GELU_C * (1.0 + 3.0 * 0.044715 * (v * v)))


def _dot(a, b):
    return jnp.dot(a, b, preferred_element_type=F32)


def _dot_nt(a, b):
    return lax.dot_general(a, b, (((1,), (1,)), ((), ())), preferred_element_type=F32)


def _dot_tn(a, b):
    return lax.dot_general(a, b, (((0,), (0,)), ((), ())), preferred_element_type=F32)


def _cast_into_slot(w, me, name):
    n, r, c = w.shape
    tr = _tile(r, 256)

    def body(me_ref, w_ref, o_ref):
        o_ref[...] = w_ref[...].astype(BF16)

    return pl.pallas_call(
        body, name=name,
        grid_spec=pltpu.PrefetchScalarGridSpec(
            num_scalar_prefetch=1, grid=(n, r // tr),
            in_specs=[pl.BlockSpec((None, tr, c), lambda a, b, me_ref: (a, b, 0))],
            out_specs=pl.BlockSpec((None, None, tr, c), lambda a, b, me_ref: (me_ref[0], a, b, 0))),
        out_shape=jax.ShapeDtypeStruct((N_BLK,) + w.shape, BF16),
        compiler_params=_params(("parallel", "parallel")),
    )(me, w)


def _split_heads(xv, rows_ref, cols_ref):
    if rows_ref is not None:
        for h in range(N_HEADS):
            rows_ref[h] = xv[:, h * HEAD_DIM:(h + 1) * HEAD_DIM].astype(BF16)
    if cols_ref is not None:
        xt = xv.T
        for h in range(N_HEADS):
            cols_ref[h] = xt[h * HEAD_DIM:(h + 1) * HEAD_DIM, :].astype(BF16)


def _head_specs(t, s):
    rows = (pl.BlockSpec((N_HEADS, t, HEAD_DIM), lambda i: (0, i, 0)), jax.ShapeDtypeStruct((N_HEADS, s, HEAD_DIM), BF16))
    cols = (pl.BlockSpec((N_HEADS, HEAD_DIM, t), lambda i: (0, 0, i)), jax.ShapeDtypeStruct((N_HEADS, HEAD_DIM, s), BF16))
    return rows, cols


def _pair_blocks(w_ref, wide_ref):
    @pl.when(pl.program_id(0) == 0)
    def _():
        for b in range(N_BLK):
            wide_ref[b // 2, :, (b % 2) * W_IN_BLK:(b % 2 + 1) * W_IN_BLK] = w_ref[b]


def _in_proj(x, g, w, layer):
    s = x.shape[0]
    t = _tile(s, 512)

    def body(x_ref, g_ref, w_ref, z_ref, h_ref, qr_ref, qt_ref, kt_ref, vt_ref, wide_ref):
        _pair_blocks(w_ref, wide_ref)
        xv = x_ref[...]
        h = (xv * _rsqrt_mean(xv) * g_ref[...]).astype(BF16)
        h_ref[...] = h
        for n in range(N_BLK // 2):
            z_ref[:, 2 * n * W_IN_BLK:2 * (n + 1) * W_IN_BLK] = _dot(h, wide_ref[n])
        _split_heads(z_ref[:, 1280:1536] * ATT_SCALE, qr_ref, qt_ref)
        _split_heads(z_ref[:, 1536:1792], None, kt_ref)
        _split_heads(z_ref[:, 1792:2048], None, vt_ref)

    rows, cols = _head_specs(t, s)
    return pl.pallas_call(
        body, name="in_proj", grid=(s // t,),
        in_specs=[pl.BlockSpec((t, D_MODEL), lambda i: (i, 0)), _full((1, D_MODEL)),
                  pl.BlockSpec((N_BLK, None, D_MODEL, W_IN_BLK), lambda i: (0, layer, 0, 0))],
        out_specs=[pl.BlockSpec((t, D_IN), lambda i: (i, 0)), pl.BlockSpec((t, D_MODEL), lambda i: (i, 0)),
                   rows[0], cols[0], cols[0], cols[0]],
        out_shape=[jax.ShapeDtypeStruct((s, D_IN), F32), jax.ShapeDtypeStruct((s, D_MODEL), BF16),
                   rows[1], cols[1], cols[1], cols[1]],
        scratch_shapes=[pltpu.VMEM((N_BLK // 2, D_MODEL, 2 * W_IN_BLK), BF16)],
        compiler_params=_params(("arbitrary",), V7X_VMEM_LIMIT),
    )(x, g, w)


def _mix_a_fwd(z_ref, vg, wt_ref, bmat, t):
    zu = z_ref[:, 0:256]
    zv = z_ref[:, 256:512]
    tu = _gelu_tanh(zu)
    tv = _gelu_tanh(zv)
    u = _gelu(zu, tu)
    v = _gelu(zv, tv)
    rv = _rsqrt_mean(v)
    vh = v * rv
    vnb = (vh * vg).astype(BF16)
    head = lax.broadcasted_iota(jnp.int32, (CHUNK, D_GROUP), 1) // HEAD_DIM
    fs = []
    for c in range(t // CHUNK):
        vc = vnb[c * CHUNK:(c + 1) * CHUNK, :]
        fc = bmat
        for h in range(N_HEADS):
            fc = fc + jnp.where(head == h, _dot(wt_ref[h], vc), 0.0)
        fs.append(fc)
    f = jnp.concatenate(fs, axis=0) if len(fs) > 1 else fs[0]
    return (zu, tu), (zv, tv), u, rv, vh, vnb, f


def _windows(ext_ref, sh_ref, t):
    for b in range(1, 8):
        sh_ref[b - 1] = ext_ref[pl.ds(b, HALO + t - 8), :]

    def window(o):
        a, b = divmod(o, 8)
        return ext_ref[pl.ds(8 * a, t), :] if b == 0 else sh_ref[b - 1, pl.ds(8 * a, t), :]

    return window


def _mix_b_fwd(z_ref, zh_ref, first, scw_ref, ext_ref, t):
    gb = z_ref[:, 512:768]
    uh = zh_ref[:, 768:1024] * zh_ref[:, 1024:1280]
    ext_ref[0:HALO, :] = jnp.where(first, 0.0, uh)
    ext_ref[HALO:HALO + t, :] = z_ref[:, 768:1024] * z_ref[:, 1024:1280]
    cv = jnp.zeros((t, D_GROUP), F32)
    for k in range(SHORT_K):
        cv = cv + scw_ref[k:k + 1, :] * ext_ref[pl.ds(HALO - (SHORT_K - 1) + k, t), :]
    return gb, cv


def _mix_d_fwd(z_ref, zh_ref, first, ccw_ref, lg, lb, ext_ref, sh_ref, t):
    hh = zh_ref[:, 2048:2304] * _sigmoid(zh_ref[:, 2304:2560])
    ext_ref[0:HALO, :] = jnp.where(first, 0.0, hh)
    ext_ref[HALO:HALO + t, :] = z_ref[:, 2048:2304] * _sigmoid(z_ref[:, 2304:2560])
    window = _windows(ext_ref, sh_ref, t)
    cv = jnp.zeros((t, D_GROUP), F32)
    for k in range(CONF_K):
        cv = cv + ccw_ref[k:k + 1, :] * window(HALO - (CONF_K - 1) + k)
    xc = cv - jnp.mean(cv, axis=-1, keepdims=True)
    rs = lax.rsqrt(jnp.mean(xc * xc, axis=-1, keepdims=True) + EPS)
    xh = xc * rs
    ln = xh * lg + lb
    return xh, rs, ln, _sigmoid(ln), window


def _mix_specs(t, s):
    per = t // HALO
    return [pl.BlockSpec((t, D_IN), lambda i: (i, 0)),
            pl.BlockSpec((HALO, D_IN), lambda i: (jnp.maximum(i * per - 1, 0), 0))]


def _full(shape):
    return pl.BlockSpec(shape, lambda i: (0,) * len(shape))


def _mixers_fwd(z, p, bufs=(), parts=()):
    s = z.shape[0]
    t = _tile(s, 256)
    nb = len(bufs)

    def body(z_ref, zh_ref, vg_ref, wt_ref, bm_ref, scw_ref, ccw_ref, lg_ref, lb_ref, *rest):
        y_ref = rest[nb]
        eb_ref, ed_ref, sh_ref = rest[2 * nb + 1:2 * nb + 4]
        if nb:
            _gather_in_steps(rest[nb + 1:2 * nb + 1], parts, rest[2 * nb + 4:], s // t)
        first = pl.program_id(0) == 0
        _, _, u, _, _, _, f = _mix_a_fwd(z_ref, vg_ref[...], wt_ref, bm_ref[...], t)
        ya = u * f
        y_ref[:, 0:256] = ya * _rsqrt_mean(ya)
        gb, cv = _mix_b_fwd(z_ref, zh_ref, first, scw_ref, eb_ref, t)
        yb = gb * cv
        y_ref[:, 256:512] = yb * _rsqrt_mean(yb)
        _, _, ln, sg, _ = _mix_d_fwd(z_ref, zh_ref, first, ccw_ref, lg_ref[...], lb_ref[...], ed_ref, sh_ref, t)
        yd = ln * sg
        y_ref[:, 512:768] = yd * _rsqrt_mean(yd)

    out = pl.pallas_call(
        body, name="mixers_fwd", grid=(s // t,),
        in_specs=_mix_specs(t, s) + [_full((1, D_GROUP)), _full((N_HEADS, CHUNK, CHUNK)), _full((CHUNK, D_GROUP)),
                                     _full((8, D_GROUP)), _full((32, D_GROUP)), _full((1, D_GROUP)), _full((1, D_GROUP))]
                 + [_ANY] * nb,
        out_specs=[pl.BlockSpec((t, 768), lambda i: (i, 0))] + [_ANY] * nb,
        out_shape=[jax.ShapeDtypeStruct((s, 768), F32)] + [jax.ShapeDtypeStruct(b.shape, b.dtype) for b in bufs],
        input_output_aliases={9 + k: 1 + k for k in range(nb)},
        scratch_shapes=[pltpu.VMEM((HALO + t, D_GROUP), F32), pltpu.VMEM((HALO + t, D_GROUP), F32),
                        pltpu.VMEM((7, HALO + t - 8, D_GROUP), F32)] + (_gather_sems(parts) if nb else []),
        compiler_params=_params(("arbitrary",) if nb else ("parallel",)),
    )(z, z, p["vg"], p["wt"], p["bmat"], p["scw"], p["ccw"], p["lg"], p["lb"], *bufs)
    return (out[0], out[1:]) if nb else out[0]


def _mixers_bwd_a(z, dyn, o_t, p, sums=()):
    s = z.shape[0]
    t = _tile(s, 256)
    n_chunk = t // CHUNK
    ns = len(sums)

    def body(*refs):
        (z_ref, zh_ref, dyn_ref, ot_ref, vg_ref, wt_ref, wtt_ref, bm_ref, scw_ref, ccw_ref, lg_ref, lb_ref) = refs[:12]
        (dza_ref, dcb_ref, dcd_ref, dor_ref, dot_ref, ds_ref, dvg_ref, dws_ref, dbm_ref, dscw_ref, dccw_ref, dlg_ref,
         dlb_ref) = refs[12 + ns:25 + ns]
        eb_ref, ed_ref, sh_ref = refs[25 + 2 * ns:28 + 2 * ns]
        i = pl.program_id(0)
        first = i == 0
        if ns:
            _owners_in_steps(refs[12:12 + ns], refs[25 + ns:25 + 2 * ns], refs[28 + 2 * ns:], first, i == s // t - 1)

        @pl.when(first)
        def _():
            for r in (dvg_ref, dws_ref, dbm_ref, dscw_ref, dccw_ref, dlg_ref, dlb_ref):
                r[...] = jnp.zeros_like(r)

        def rms_bwd(y, dn):
            r = _rsqrt_mean(y)
            yn = y * r
            return r * (dn - yn * jnp.mean(dn * yn, axis=-1, keepdims=True))

        vg = vg_ref[...]
        gelu_u, gelu_v, u, rv, vh, vnb, f = _mix_a_fwd(z_ref, vg, wt_ref, bm_ref[...], t)
        dya = rms_bwd(u * f, dyn_ref[:, 0:256])
        du = dya * f
        df = dya * u
        head = lax.broadcasted_iota(jnp.int32, (CHUNK, D_GROUP), 1) // HEAD_DIM
        dvns = []
        dbm = jnp.zeros((CHUNK, D_GROUP), F32)
        for c in range(n_chunk):
            dfc = df[c * CHUNK:(c + 1) * CHUNK, :]
            vc = vnb[c * CHUNK:(c + 1) * CHUNK, :]
            dbm = dbm + dfc
            dvn = jnp.zeros((CHUNK, D_GROUP), F32)
            for h in range(N_HEADS):
                dfh = jnp.where(head == h, dfc, 0.0).astype(BF16)
                dvn = dvn + _dot(wtt_ref[h], dfh)
                dws_ref[h] += _dot_nt(dfh, vc)
            dvns.append(dvn)
        dbm_ref[...] += dbm
        dvn = jnp.concatenate(dvns, axis=0) if n_chunk > 1 else dvns[0]
        dvg_ref[...] += jnp.sum(dvn * vh, axis=0, keepdims=True)
        dvh = dvn * vg
        dv = rv * (dvh - vh * jnp.mean(dvh * vh, axis=-1, keepdims=True))
        dza_ref[:, 0:256] = (du * _gelu_grad(*gelu_u)).astype(BF16)
        dza_ref[:, 256:512] = (dv * _gelu_grad(*gelu_v)).astype(BF16)

        gb, cv = _mix_b_fwd(z_ref, zh_ref, first, scw_ref, eb_ref, t)
        dyb = rms_bwd(gb * cv, dyn_ref[:, 256:512])
        dza_ref[:, 512:768] = (dyb * cv).astype(BF16)
        dcb = dyb * gb
        dcb_ref[...] = dcb
        for k in range(SHORT_K):
            dscw_ref[k:k + 1, :] += jnp.sum(dcb * eb_ref[pl.ds(HALO - (SHORT_K - 1) + k, t), :], axis=0, keepdims=True)

        lg = lg_ref[...]
        xh, rs, ln, sg, window = _mix_d_fwd(z_ref, zh_ref, first, ccw_ref, lg, lb_ref[...], ed_ref, sh_ref, t)
        dyd = rms_bwd(ln * sg, dyn_ref[:, 768:1024])
        dln = dyd * (sg * (1.0 + ln * (1.0 - sg)))
        dlg_ref[...] += jnp.sum(dln * xh, axis=0, keepdims=True)
        dlb_ref[...] += jnp.sum(dln, axis=0, keepdims=True)
        dxh = dln * lg
        dcd = rs * (dxh - jnp.mean(dxh, axis=-1, keepdims=True) - xh * jnp.mean(dxh * xh, axis=-1, keepdims=True))
        dcd_ref[...] = dcd
        for k in range(CONF_K):
            dccw_ref[k:k + 1, :] += jnp.sum(dcd * window(HALO - (CONF_K - 1) + k), axis=0, keepdims=True)

        o = ot_ref[...].reshape(D_GROUP, t).T
        do = rms_bwd(o, dyn_ref[:, 512:768])
        _split_heads(do, dor_ref, dot_ref)
        prod = do.astype(BF16).astype(F32) * o
        for h in range(N_HEADS):
            ds_ref[h] = jnp.sum(prod[:, h * HEAD_DIM:(h + 1) * HEAD_DIM], axis=1, keepdims=True)

    small = [(1, D_GROUP), (N_HEADS, CHUNK, CHUNK), (CHUNK, D_GROUP), (8, D_GROUP), (32, D_GROUP), (1, D_GROUP), (1, D_GROUP)]
    rows, cols = _head_specs(t, s)
    out = pl.pallas_call(
        body, name="mixers_bwd_a", grid=(s // t,),
        in_specs=_mix_specs(t, s) + [pl.BlockSpec((t, D_MODEL), lambda i: (i, 0)),
                                     pl.BlockSpec((N_HEADS, HEAD_DIM, t), lambda i: (0, 0, i)),
                                     _full((1, D_GROUP)), _full((N_HEADS, CHUNK, CHUNK)), _full((N_HEADS, CHUNK, CHUNK)),
                                     _full((CHUNK, D_GROUP)), _full((8, D_GROUP)), _full((32, D_GROUP)),
                                     _full((1, D_GROUP)), _full((1, D_GROUP))] + [_ANY] * ns,
        out_specs=[pl.BlockSpec((t, 768), lambda i: (i, 0)), pl.BlockSpec((t, D_GROUP), lambda i: (i, 0)),
                   pl.BlockSpec((t, D_GROUP), lambda i: (i, 0)), rows[0], cols[0],
                   pl.BlockSpec((N_HEADS, t, 1), lambda i: (0, i, 0))]
                  + [_full(sh) for sh in small] + [_ANY] * ns,
        out_shape=[jax.ShapeDtypeStruct((s, 768), BF16), jax.ShapeDtypeStruct((s, D_GROUP), F32),
                   jax.ShapeDtypeStruct((s, D_GROUP), F32), rows[1], cols[1],
                   jax.ShapeDtypeStruct((N_HEADS, s, 1), F32)]
                  + [jax.ShapeDtypeStruct(sh, F32) for sh in small]
                  + [jax.ShapeDtypeStruct(a.shape, a.dtype) for a in sums],
        scratch_shapes=[pltpu.VMEM((HALO + t, D_GROUP), F32), pltpu.VMEM((HALO + t, D_GROUP), F32),
                        pltpu.VMEM((7, HALO + t - 8, D_GROUP), F32)] + (_owner_sems(ns) if ns else []),
        compiler_params=_params(("arbitrary",)),
    )(z, z, dyn, o_t, p["vg"], p["wt"], p["wtt"], p["bmat"], p["scw"], p["ccw"], p["lg"], p["lb"], *sums)
    return tuple(out[:13]) + (out[13:],) if ns else out


def _mixers_bwd_b(z, dza, dcb, dcd, dq_t, dk_t, dv_t, p):
    s = z.shape[0]
    t = _tile(s, 256)
    per = t // HALO
    n_halo = s // HALO

    def body(z_ref, dza_ref, dcb_ref, dcbn_ref, dcd_ref, dcdn_ref, dq_ref, dk_ref, dv_ref, scw_ref, ccw_ref,
             dz_ref, eb_ref, ed_ref, sh_ref):
        last = pl.program_id(0) == pl.num_programs(0) - 1
        dz_ref[:, 0:768] = dza_ref[...]
        eb_ref[0:t, :] = dcb_ref[...]
        eb_ref[t:t + HALO, :] = jnp.where(last, 0.0, dcbn_ref[...])
        du = jnp.zeros((t, D_GROUP), F32)
        for k in range(SHORT_K):
            du = du + scw_ref[k:k + 1, :] * eb_ref[pl.ds(SHORT_K - 1 - k, t), :]
        dz_ref[:, 768:1024] = (du * z_ref[:, 1024:1280]).astype(BF16)
        dz_ref[:, 1024:1280] = (du * z_ref[:, 768:1024]).astype(BF16)
        for n, r in enumerate((dq_ref, dk_ref, dv_ref)):
            dz_ref[:, 1280 + 256 * n:1536 + 256 * n] = r[...].reshape(D_GROUP, t).T.astype(BF16)
        ed_ref[0:t, :] = dcd_ref[...]
        ed_ref[t:t + HALO, :] = jnp.where(last, 0.0, dcdn_ref[...])
        window = _windows(ed_ref, sh_ref, t)
        dh = jnp.zeros((t, D_GROUP), F32)
        for k in range(CONF_K):
            dh = dh + ccw_ref[k:k + 1, :] * window(CONF_K - 1 - k)
        a = z_ref[:, 2048:2304]
        sg = _sigmoid(z_ref[:, 2304:2560])
        dz_ref[:, 2048:2304] = (dh * sg).astype(BF16)
        dz_ref[:, 2304:2560] = (dh * a * sg * (1.0 - sg)).astype(BF16)

    nxt = lambda i: (jnp.minimum((i + 1) * per, n_halo - 1), 0)
    tr = pl.BlockSpec((N_HEADS, HEAD_DIM, t), lambda i: (0, 0, i))
    return pl.pallas_call(
        body, name="mixers_bwd_b", grid=(s // t,),
        in_specs=[pl.BlockSpec((t, D_IN), lambda i: (i, 0)), pl.BlockSpec((t, 768), lambda i: (i, 0)),
                  pl.BlockSpec((t, D_GROUP), lambda i: (i, 0)), pl.BlockSpec((HALO, D_GROUP), nxt),
                  pl.BlockSpec((t, D_GROUP), lambda i: (i, 0)), pl.BlockSpec((HALO, D_GROUP), nxt),
                  tr, tr, tr, _full((8, D_GROUP)), _full((32, D_GROUP))],
        out_specs=pl.BlockSpec((t, D_IN), lambda i: (i, 0)),
        out_shape=jax.ShapeDtypeStruct((s, D_IN), BF16),
        scratch_shapes=[pltpu.VMEM((HALO + t, D_GROUP), F32), pltpu.VMEM((HALO + t, D_GROUP), F32),
                        pltpu.VMEM((7, HALO + t - 8, D_GROUP), F32)],
        compiler_params=_params(("parallel",)),
    )(z, dza, dcb, dcb, dcd, dcd, dq_t, dk_t, dv_t, p["scw"], p["ccw"])


def _split_bf16(v):
    hi = v.astype(BF16)
    return hi, (v - hi.astype(F32)).astype(BF16)


def _att_scores(qs, kts, carries, tri, mask):
    zs = [_dot(q, kt) for q, kt in zip(qs, kts)]
    lms, lbs, parts = [], [], []
    for z in zs:
        soft = jnp.log(1.0 + jnp.exp(-jnp.abs(z)))
        lm = -(jnp.maximum(z, 0.0) + soft)
        lbs.append(lm + z)
        if mask is not None:
            lm = jnp.where(mask, lm, 0.0)
        lms.append(lm)
        parts.append(_split_bf16(lm))
    rights = [_dot(hi, tri) + _dot(lo, tri) for hi, lo in parts]
    ws = []
    for lb, right, carry in zip(lbs, rights, carries):
        w = jnp.exp(lb + right + carry)
        ws.append(w if mask is None else jnp.where(mask, w, 0.0))
    return ws, lbs, [jnp.sum(lm, axis=1, keepdims=True) for lm in lms]


def _att_consts(i):
    j_hi = ((i + 1) * ATT_TQ - 1) // ATT_TK
    row = lax.broadcasted_iota(jnp.int32, (ATT_TQ, ATT_TK), 0) + i * ATT_TQ
    col = lax.broadcasted_iota(jnp.int32, (ATT_TQ, ATT_TK), 1) + j_hi * ATT_TK
    r_i = lax.broadcasted_iota(jnp.int32, (ATT_TK, ATT_TK), 0)
    c_i = lax.broadcasted_iota(jnp.int32, (ATT_TK, ATT_TK), 1)
    return j_hi, col < row, r_i, c_i


def _att_alive(j, carries):
    top = carries[0]
    for c in carries[1:]:
        top = jnp.maximum(top, c)
    return jnp.logical_and(j >= 0, jnp.max(top) > ATT_DEAD)


def _attn_fwd(q_r, k_t, v_t, bufs=(), parts=()):
    s = q_r.shape[1]
    nb = len(bufs)

    def body(q_ref, kt_ref, vt_ref, *rest):
        o_ref = rest[nb]
        if nb:
            _gather_in_steps(rest[nb + 1:2 * nb + 1], parts, rest[2 * nb + 1:], s // ATT_TQ)
        j_hi, mask, r_i, c_i = _att_consts(pl.program_id(0))
        tri = (r_i > c_i).astype(BF16)

        heads = range(N_HEADS)

        def tiles(j, carries, accs, mask):
            cols = pl.ds(pl.multiple_of(j * ATT_TK, ATT_TK), ATT_TK)
            ws, _, tots = _att_scores([q_ref[h] for h in heads], [kt_ref[h, :, cols] for h in heads], carries, tri, mask)
            accs = [acc + _dot_nt(vt_ref[h, :, cols], w.astype(BF16)) for h, acc, w in zip(heads, accs, ws)]
            return [c + t for c, t in zip(carries, tots)], accs

        state = tiles(j_hi, [jnp.zeros((ATT_TQ, 1), F32)] * N_HEADS, [jnp.zeros((HEAD_DIM, ATT_TQ), F32)] * N_HEADS, mask)

        def cond(c):
            return _att_alive(c[0], c[1])

        def step(c):
            return (c[0] - 1,) + tuple(tiles(c[0], c[1], c[2], None))

        _, _, accs = lax.while_loop(cond, step, (j_hi - 1,) + tuple(state))
        for h in heads:
            o_ref[h] = accs[h]

    whole = pl.BlockSpec((N_HEADS, HEAD_DIM, s), lambda i: (0, 0, 0), pipeline_mode=pl.Buffered(1))
    out = pl.pallas_call(
        body, name="attn_fwd", grid=(s // ATT_TQ,),
        in_specs=[pl.BlockSpec((N_HEADS, ATT_TQ, HEAD_DIM), lambda i: (0, i, 0)), whole, whole] + [_ANY] * nb,
        out_specs=[pl.BlockSpec((N_HEADS, HEAD_DIM, ATT_TQ), lambda i: (0, 0, i))] + [_ANY] * nb,
        out_shape=[jax.ShapeDtypeStruct((N_HEADS, HEAD_DIM, s), F32)] + [jax.ShapeDtypeStruct(b.shape, b.dtype) for b in bufs],
        input_output_aliases={3 + k: 1 + k for k in range(nb)},
        scratch_shapes=_gather_sems(parts) if nb else [],
        compiler_params=_params(("arbitrary",), V7X_VMEM_LIMIT),
    )(q_r, k_t, v_t, *bufs)
    return (out[0], out[1:]) if nb else out[0]


ATT_BWD_HEADS = 2


def _attn_bwd(q_r, q_t, k_t, v_t, do_r, do_t, dsum):
    s = q_r.shape[1]
    hps = ATT_BWD_HEADS

    def body(q_ref, qt_ref, kt_ref, vt_ref, do_ref, dot_ref, ds_ref, dq_ref, dk_ref, dv_ref):
        i = pl.program_id(1)

        @pl.when(i == 0)
        def _():
            dk_ref[...] = jnp.zeros_like(dk_ref)
            dv_ref[...] = jnp.zeros_like(dv_ref)

        j_hi, mask, r_i, c_i = _att_consts(i)
        tri_r = (r_i > c_i).astype(BF16)
        tri_ge = (r_i >= c_i).astype(BF16)

        heads = range(hps)

        def tiles(j, carries, gsums, accs, mask):
            cols = pl.ds(pl.multiple_of(j * ATT_TK, ATT_TK), ATT_TK)
            kts = [kt_ref[h, :, cols] for h in heads]
            das = [_dot(do_ref[h], vt_ref[h, :, cols]) for h in heads]
            ws, lbs, tots = _att_scores([q_ref[h] for h in heads], kts, carries, tri_r, mask)
            wbs = [w.astype(BF16) for w in ws]
            gs = [wb.astype(F32) * da for wb, da in zip(wbs, das)]
            parts = [_split_bf16(g) for g in gs]
            sfx = [_dot(hi, tri_ge) + _dot(lo, tri_ge) for hi, lo in parts]
            for h in heads:
                dv_ref[h, :, cols] += _dot(dot_ref[h], wbs[h])
            dzs = []
            for h in heads:
                left = ds_ref[h] - gsums[h] - sfx[h]
                dz = gs[h] - jnp.exp(lbs[h]) * (gs[h] + left)
                dzs.append((dz if mask is None else jnp.where(mask, dz, 0.0)).astype(BF16))
            for h in heads:
                dk_ref[h, :, cols] += _dot(qt_ref[h], dzs[h])
            accs = [accs[h] + _dot_nt(kts[h], dzs[h]) for h in heads]
            gsums = [gsums[h] + jnp.sum(gs[h], axis=1, keepdims=True) for h in heads]
            return [c + t for c, t in zip(carries, tots)], gsums, accs

        col0 = [jnp.zeros((ATT_TQ, 1), F32)] * hps
        state = tiles(j_hi, col0, col0, [jnp.zeros((HEAD_DIM, ATT_TQ), F32)] * hps, mask)

        def cond(c):
            return _att_alive(c[0], c[1])

        def step(c):
            return (c[0] - 1,) + tuple(tiles(c[0], c[1], c[2], c[3], None))

        _, _, _, accs = lax.while_loop(cond, step, (j_hi - 1,) + tuple(state))
        for h in heads:
            dq_ref[h] = accs[h] * ATT_SCALE

    whole = pl.BlockSpec((hps, HEAD_DIM, s), lambda g, i: (g, 0, 0))
    whole_in = pl.BlockSpec((hps, HEAD_DIM, s), lambda g, i: (g, 0, 0), pipeline_mode=pl.Buffered(1))
    rows = pl.BlockSpec((hps, ATT_TQ, HEAD_DIM), lambda g, i: (g, i, 0))
    cols = pl.BlockSpec((hps, HEAD_DIM, ATT_TQ), lambda g, i: (g, 0, i))
    shape = jax.ShapeDtypeStruct((N_HEADS, HEAD_DIM, s), F32)
    return pl.pallas_call(
        body, name="attn_bwd", grid=(N_HEADS // hps, s // ATT_TQ),
        in_specs=[rows, cols, whole_in, whole_in, rows, cols, pl.BlockSpec((hps, ATT_TQ, 1), lambda g, i: (g, i, 0))],
        out_specs=[cols, whole, whole],
        out_shape=[shape, shape, shape],
        compiler_params=_params(("parallel", "arbitrary"), V7X_VMEM_LIMIT),
    )(q_r, q_t, k_t, v_t, do_r, do_t, dsum)


def _out_proj(x, y_abd, o_t, gain, w, layer):
    s = x.shape[0]
    t = _tile(s, 512)

    def body(x_ref, y_ref, ot_ref, g_ref, w_ref, x1_ref, yn_ref):
        o = ot_ref[...].reshape(D_GROUP, t).T
        yn_ref[:, 0:512] = y_ref[:, 0:512]
        yn_ref[:, 512:768] = o * _rsqrt_mean(o)
        yn_ref[:, 768:1024] = y_ref[:, 512:768]
        yg = (yn_ref[...] * g_ref[...]).astype(BF16)
        acc = _dot(yg[:, 0:256], w_ref[0])
        for b in range(1, N_BLK):
            acc = acc + _dot(yg[:, 256 * b:256 * (b + 1)], w_ref[b])
        x1_ref[...] = x_ref[...] + acc

    return pl.pallas_call(
        body, name="out_proj", grid=(s // t,),
        in_specs=[pl.BlockSpec((t, D_MODEL), lambda i: (i, 0)), pl.BlockSpec((t, 768), lambda i: (i, 0)),
                  pl.BlockSpec((N_HEADS, HEAD_DIM, t), lambda i: (0, 0, i)), _full((1, D_MODEL)),
                  pl.BlockSpec((N_BLK, None, D_GROUP, D_MODEL), lambda i: (0, layer, 0, 0))],
        out_specs=[pl.BlockSpec((t, D_MODEL), lambda i: (i, 0)), pl.BlockSpec((t, D_MODEL), lambda i: (i, 0))],
        out_shape=[jax.ShapeDtypeStruct((s, D_MODEL), F32), jax.ShapeDtypeStruct((s, D_MODEL), F32)],
        compiler_params=_params(("parallel",)),
    )(x, y_abd, o_t, gain, w)


def _out_proj_bwd(x1, g_ffn, dh, dx2, yn, gain, w, layer):
    s = dx2.shape[0]
    t = _tile(s, 512)

    def body(x_ref, gf_ref, dh_ref, dx2_ref, yn_ref, g_ref, w_ref, dx1_ref, dgf_ref, dyn_ref, dg_ref, dw_ref):
        @pl.when(pl.program_id(0) == 0)
        def _():
            dg_ref[...] = jnp.zeros_like(dg_ref)
            dw_ref[...] = jnp.zeros_like(dw_ref)
            dgf_ref[...] = jnp.zeros_like(dgf_ref)

        dx1, dgf = _rms_bwd_rows(x_ref[...], gf_ref[...], dh_ref[...], dx2_ref[...])
        dx1_ref[...] = dx1
        dgf_ref[...] += dgf
        dxb = dx1.astype(BF16)
        g = g_ref[...]
        yn = yn_ref[...]
        yg = (yn * g).astype(BF16)
        for b in range(N_BLK):
            cols = slice(256 * b, 256 * (b + 1))
            dyg = _dot_nt(dxb, w_ref[b])
            dw_ref[b] += _dot_tn(yg[:, cols], dxb)
            dg_ref[:, cols] += jnp.sum(dyg * yn[:, cols], axis=0, keepdims=True)
            dyn_ref[:, cols] = dyg * g[:, cols]

    row = pl.BlockSpec((t, D_MODEL), lambda i: (i, 0))
    vec = _full((1, D_MODEL))
    return pl.pallas_call(
        body, name="out_proj_bwd", grid=(s // t,),
        in_specs=[row, vec, row, row, row, vec, pl.BlockSpec((N_BLK, None, D_GROUP, D_MODEL), lambda i: (0, layer, 0, 0))],
        out_specs=[row, vec, row, vec, _full((N_BLK, D_GROUP, D_MODEL))],
        out_shape=[jax.ShapeDtypeStruct((s, D_MODEL), F32), jax.ShapeDtypeStruct((1, D_MODEL), F32),
                   jax.ShapeDtypeStruct((s, D_MODEL), F32), jax.ShapeDtypeStruct((1, D_MODEL), F32),
                   jax.ShapeDtypeStruct((N_BLK, D_GROUP, D_MODEL), F32)],
        compiler_params=_params(("arbitrary",), V7X_VMEM_LIMIT),
    )(x1, g_ffn, dh, dx2, yn, gain, w)


def _ffn(x, g, w_up, w_down, layer, head=None):
    s = x.shape[0]
    t = _tile(s, 1024 if head is None else 512)

    def body(*refs):
        if head is None:
            x_ref, g_ref, wu_ref, wd_ref, acc_ref, p_ref, h_ref = refs
        else:
            x_ref, g_ref, wu_ref, wd_ref, gf_ref, t_ref, p_ref, h_ref, l_ref, dx_ref, dxb_ref, dgf_ref, acc_ref = refs
        i, j = pl.program_id(0), pl.program_id(1)

        @pl.when(j == 0)
        def _():
            xv = x_ref[...]
            h_ref[...] = (xv * _rsqrt_mean(xv) * g_ref[...]).astype(BF16)
            acc_ref[...] = xv

        pre = _dot(h_ref[...], wu_ref[...])
        p_ref[...] = pre.astype(BF16)
        a = jnp.maximum(pre, 0.0)
        acc_ref[...] += _dot((a * a).astype(BF16), wd_ref[...])

        if head is not None:
            @pl.when(jnp.logical_and(i == 0, j == 0))
            def _():
                l_ref[...] = jnp.zeros_like(l_ref)
                dgf_ref[...] = jnp.zeros_like(dgf_ref)

            @pl.when(j == N_BLK - 1)
            def _():
                xv = acc_ref[...]
                gf = gf_ref[...]
                r = _rsqrt_mean(xv)
                xh = xv * r
                err = xh * gf - t_ref[...]
                l_ref[...] += 0.5 * jnp.sum(jnp.mean(err * err, axis=-1, keepdims=True), axis=0, keepdims=True)
                dy = err * (1.0 / D_MODEL)
                dgf_ref[...] += jnp.sum(dy * xh, axis=0, keepdims=True)
                dxh = dy * gf
                dx = r * (dxh - xh * jnp.mean(dxh * xh, axis=-1, keepdims=True))
                dx_ref[...] = dx
                dxb_ref[...] = dx.astype(BF16)

    wspec = pl.BlockSpec((None, None, D_MODEL, D_FF_BLK), lambda i, j: (j, layer, 0, 0))
    row = pl.BlockSpec((t, D_MODEL), lambda i, j: (i, 0))
    vec = pl.BlockSpec((1, D_MODEL), lambda i, j: (0, 0))
    blk = pl.BlockSpec((t, D_FF_BLK), lambda i, j: (i, j))
    act = jax.ShapeDtypeStruct((s, D_MODEL), F32)
    kept = [jax.ShapeDtypeStruct((s, N_BLK * D_FF_BLK), BF16), jax.ShapeDtypeStruct((s, D_MODEL), BF16)]
    if head is None:
        return pl.pallas_call(
            body, name="ffn", grid=(s // t, N_BLK),
            in_specs=[row, vec, wspec, wspec], out_specs=[row, blk, row], out_shape=[act] + kept,
            compiler_params=_params(("parallel", "arbitrary"), V7X_VMEM_LIMIT),
        )(x, g, w_up, w_down)
    return pl.pallas_call(
        body, name="ffn_loss", grid=(s // t, N_BLK),
        in_specs=[row, vec, wspec, wspec, vec, row],
        out_specs=[blk, row, pl.BlockSpec((1, 128), lambda i, j: (0, 0)), row, row, vec],
        out_shape=kept + [jax.ShapeDtypeStruct((1, 128), F32), act, jax.ShapeDtypeStruct((s, D_MODEL), BF16),
                          jax.ShapeDtypeStruct((1, D_MODEL), F32)],
        scratch_shapes=[pltpu.VMEM((t, D_MODEL), F32)],
        compiler_params=_params(("arbitrary", "arbitrary"), V7X_VMEM_LIMIT),
    )(x, g, w_up, w_down, *head)


def _rms_bwd_rows(xv, g, dh, dres):
    r = _rsqrt_mean(xv)
    xh = xv * r
    dxh = dh * g
    dx = dres + r * (dxh - xh * jnp.mean(dxh * xh, axis=-1, keepdims=True))
    return dx, jnp.sum(dh * xh, axis=0, keepdims=True)


def _ffn_bwd(dxb, p, w_up, w_down, layer, sums=()):
    s = dxb.shape[0]
    t = _tile(s, 1024)
    ns = len(sums)

    def body(dx_ref, p_ref, wu_ref, wd_ref, *rest):
        dp_ref, dh_ref = rest[ns:ns + 2]
        if ns:
            i, j = pl.program_id(0), pl.program_id(1)
            _owners_in_steps(rest[:ns], rest[ns + 2:2 * ns + 2], rest[2 * ns + 2:],
                             jnp.logical_and(i == 0, j == 0), jnp.logical_and(i == s // t - 1, j == N_BLK - 1))
        da = _dot_nt(dx_ref[...], wd_ref[...])
        a = jnp.maximum(p_ref[...].astype(F32), 0.0)
        dp = (da * (2.0 * a)).astype(BF16)
        dp_ref[...] = dp
        dh = _dot_nt(dp, wu_ref[...])

        @pl.when(pl.program_id(1) == 0)
        def _():
            dh_ref[...] = dh

        @pl.when(pl.program_id(1) != 0)
        def _():
            dh_ref[...] += dh

    wspec = pl.BlockSpec((None, None, D_MODEL, D_FF_BLK), lambda i, j: (j, layer, 0, 0))
    row = pl.BlockSpec((t, D_MODEL), lambda i, j: (i, 0))
    blk = pl.BlockSpec((t, D_FF_BLK), lambda i, j: (i, j))
    out = pl.pallas_call(
        body, name="ffn_bwd", grid=(s // t, N_BLK),
        in_specs=[row, blk, wspec, wspec] + [_ANY] * ns, out_specs=[blk, row] + [_ANY] * ns,
        out_shape=[jax.ShapeDtypeStruct((s, N_BLK * D_FF_BLK), BF16), jax.ShapeDtypeStruct((s, D_MODEL), F32)]
                  + [jax.ShapeDtypeStruct(a.shape, a.dtype) for a in sums],
        scratch_shapes=_owner_sems(ns) if ns else [],
        compiler_params=_params(("arbitrary" if ns else "parallel", "arbitrary"), V7X_VMEM_LIMIT),
    )(dxb, p, w_up, w_down, *sums)
    return (out[0], out[1], out[2:]) if ns else out


def _ffn_wgrad(hb, p, dp, dxb):
    s = hb.shape[0]
    t = _tile(s, 1024)

    def body(h_ref, p_ref, dp_ref, dx_ref, du_ref, dd_ref):
        @pl.when(pl.program_id(1) == 0)
        def _():
            du_ref[...] = jnp.zeros_like(du_ref)
            dd_ref[...] = jnp.zeros_like(dd_ref)

        a = jnp.maximum(p_ref[...].astype(F32), 0.0)
        du_ref[...] += _dot_tn(h_ref[...], dp_ref[...])
        dd_ref[...] += _dot_tn((a * a).astype(BF16), dx_ref[...])

    row = pl.BlockSpec((t, D_MODEL), lambda j, i: (i, 0))
    blk = pl.BlockSpec((t, D_FF_BLK), lambda j, i: (i, j))
    out = pl.BlockSpec((None, D_MODEL, D_FF_BLK), lambda j, i: (j, 0, 0))
    shape = jax.ShapeDtypeStruct((N_BLK, D_MODEL, D_FF_BLK), F32)
    return pl.pallas_call(
        body, name="ffn_wgrad", grid=(N_BLK, s // t),
        in_specs=[row, blk, blk, row], out_specs=[out, out], out_shape=[shape, shape],
        compiler_params=_params(("parallel", "arbitrary"), V7X_VMEM_LIMIT),
    )(hb, p, dp, dxb)


def _in_proj_bwd(x, g, dx1, dz, w, layer):
    s = x.shape[0]
    t = _tile(s, 512)

    def body(x_ref, g_ref, dx1_ref, dz_ref, w_ref, dx0_ref, dxb_ref, dg_ref, wide_ref):
        _pair_blocks(w_ref, wide_ref)

        @pl.when(pl.program_id(0) == 0)
        def _():
            dg_ref[...] = jnp.zeros_like(dg_ref)

        dh = _dot_nt(dz_ref[:, 0:2 * W_IN_BLK], wide_ref[0])
        for n in range(1, N_BLK // 2):
            dh = dh + _dot_nt(dz_ref[:, 2 * n * W_IN_BLK:2 * (n + 1) * W_IN_BLK], wide_ref[n])
        dx, dg = _rms_bwd_rows(x_ref[...], g_ref[...], dh, dx1_ref[...])
        dx0_ref[...] = dx
        dxb_ref[...] = dx.astype(BF16)
        dg_ref[...] += dg

    row = pl.BlockSpec((t, D_MODEL), lambda i: (i, 0))
    return pl.pallas_call(
        body, name="in_proj_bwd", grid=(s // t,),
        in_specs=[row, _full((1, D_MODEL)), row, pl.BlockSpec((t, D_IN), lambda i: (i, 0)),
                  pl.BlockSpec((N_BLK, None, D_MODEL, W_IN_BLK), lambda i: (0, layer, 0, 0))],
        out_specs=[row, row, _full((1, D_MODEL))],
        out_shape=[jax.ShapeDtypeStruct((s, D_MODEL), F32), jax.ShapeDtypeStruct((s, D_MODEL), BF16),
                   jax.ShapeDtypeStruct((1, D_MODEL), F32)],
        scratch_shapes=[pltpu.VMEM((N_BLK // 2, D_MODEL, 2 * W_IN_BLK), BF16)],
        compiler_params=_params(("arbitrary",), V7X_VMEM_LIMIT),
    )(x, g, dx1, dz, w)


def _in_proj_wgrad(hb, dz):
    s = hb.shape[0]
    t = _tile(s, 512)

    def body(h_ref, dz_ref, dw_ref, wide_ref):
        @pl.when(pl.program_id(0) == 0)
        def _():
            wide_ref[...] = jnp.zeros_like(wide_ref)

        h = h_ref[...]
        for n in range(N_BLK // 2):
            wide_ref[n] += _dot_tn(h, dz_ref[:, 2 * n * W_IN_BLK:2 * (n + 1) * W_IN_BLK])

        @pl.when(pl.program_id(0) == s // t - 1)
        def _():
            for b in range(N_BLK):
                dw_ref[b] = wide_ref[b // 2, :, (b % 2) * W_IN_BLK:(b % 2 + 1) * W_IN_BLK]

    return pl.pallas_call(
        body, name="in_proj_wgrad", grid=(s // t,),
        in_specs=[pl.BlockSpec((t, D_MODEL), lambda i: (i, 0)), pl.BlockSpec((t, D_IN), lambda i: (i, 0))],
        out_specs=_full((N_BLK, D_MODEL, W_IN_BLK)),
        out_shape=jax.ShapeDtypeStruct((N_BLK, D_MODEL, W_IN_BLK), F32),
        scratch_shapes=[pltpu.VMEM((N_BLK // 2, D_MODEL, 2 * W_IN_BLK), F32)],
        compiler_params=_params(("arbitrary",), V7X_VMEM_LIMIT),
    )(hb, dz)


def _layer_params(small, layer):
    tril = jnp.tril(jnp.ones((CHUNK, CHUNK), bool))
    ws = jnp.where(tril, small["gmlp_w_s"][layer], 0.0)
    bmat = jnp.repeat(small["gmlp_b_s"][layer].T, HEAD_DIM, axis=1)
    scw = jnp.zeros((8, D_GROUP), F32).at[:SHORT_K].set(small["short_conv_w"][layer])
    ccw = jnp.zeros((32, D_GROUP), F32).at[:CONF_K].set(small["conf_conv_w"][layer])
    return dict(vg=small["gmlp_v_g"][layer][None], wt=ws.astype(BF16), wtt=jnp.swapaxes(ws, 1, 2).astype(BF16),
                bmat=bmat, scw=scw, ccw=ccw, lg=small["conf_ln_g"][layer][None], lb=small["conf_ln_b"][layer][None])


def _local_step(x, target, big, small, gather_pending=False, core=None):
    saved = []
    for l in range(DEPTH):
        p = _layer_params(small, l)
        z, hb, q_r, q_t, k_t, v_t = _in_proj(x, small["norm_mix_g"][l][None], big["w_in"], l)
        if gather_pending and l == 0:
            late = ("w_out", "w_up", "w_down")
            y_abd, filled = _mixers_fwd(z, p, [big[k] for k in late], [(n, 0) for n in range(len(late))])
            big = {**big, **dict(zip(late, filled))}
            o_t, filled = _attn_fwd(q_r, k_t, v_t, [big[k] for k in _BIG], [(n, 1) for n in range(len(_BIG))])
            big = dict(zip(_BIG, filled))
        else:
            y_abd = _mixers_fwd(z, p)
            o_t = _attn_fwd(q_r, k_t, v_t)
        x1, yn = _out_proj(x, y_abd, o_t, small["mix_out_g"][l][None], big["w_out"], l)
        x0 = x
        if l < DEPTH - 1:
            x, pre, h2b = _ffn(x1, small["norm_ffn_g"][l][None], big["w_up"], big["w_down"], l)
        else:
            pre, h2b, loss, dx, dxb, d_final = _ffn(x1, small["norm_ffn_g"][l][None], big["w_up"], big["w_down"], l,
                                                    (small["final_norm_g"][None], target))
        saved.append(dict(p=p, x0=x0, z=z, hb=hb, q_r=q_r, q_t=q_t, k_t=k_t, v_t=v_t, o_t=o_t, x1=x1, yn=yn, pre=pre,
                          h2b=h2b))

    g = {k: [None] * DEPTH for k in ("w_in", "w_out", "w_up", "w_down", "norm_mix_g", "gmlp_v_g", "gmlp_w_s", "gmlp_b_s",
                                     "short_conv_w", "conf_conv_w", "conf_ln_g", "conf_ln_b", "mix_out_g", "norm_ffn_g")}
    tril = jnp.tril(jnp.ones((CHUNK, CHUNK), bool))
    early = {}
    for l in reversed(range(DEPTH)):
        sv = saved[l]
        p = sv["p"]
        riding = []
        if core is not None and l == 0:
            riding = [(k, 1) for k in _BIG]
        sums = _chip_sums(core, [g[k][n] for k, n in riding])
        dpre, dh, *got = _ffn_bwd(dxb, sv["pre"], big["w_up"], big["w_down"], l, sums)
        early.update(zip(riding, zip(sums, *got)))
        g["w_up"][l], g["w_down"][l] = _ffn_wgrad(sv["h2b"], sv["pre"], dpre, dxb)
        dx1, g["norm_ffn_g"][l], dyn, g["mix_out_g"][l], g["w_out"][l] = _out_proj_bwd(
            sv["x1"], small["norm_ffn_g"][l][None], dh, dx, sv["yn"], small["mix_out_g"][l][None], big["w_out"], l)
        if core is not None and l == 0:
            riding = [("w_up", 0), ("w_down", 0)]
        sums = _chip_sums(core, [g[k][n] for k, n in riding])
        (dza, dcb, dcd, do_r, do_t, dsum, dvg, dws, dbm, dscw, dccw, dlg, dlb, *got) = _mixers_bwd_a(
            sv["z"], dyn, sv["o_t"], p, sums)
        early.update(zip(riding, zip(sums, *got)))
        dq_t, dk_t, dv_t = _attn_bwd(sv["q_r"], sv["q_t"], sv["k_t"], sv["v_t"], do_r, do_t, dsum)
        dz = _mixers_bwd_b(sv["z"], dza, dcb, dcd, dq_t, dk_t, dv_t, p)
        dx, dxb, g["norm_mix_g"][l] = _in_proj_bwd(sv["x0"], small["norm_mix_g"][l][None], dx1, dz, big["w_in"], l)
        g["w_in"][l] = _in_proj_wgrad(sv["hb"], dz)
        g["gmlp_v_g"][l] = dvg[0]
        g["gmlp_w_s"][l] = jnp.where(tril, dws, 0.0)
        g["gmlp_b_s"][l] = dbm.reshape(CHUNK, N_HEADS, HEAD_DIM).sum(-1).T
        g["short_conv_w"][l] = dscw[:SHORT_K]
        g["conf_conv_w"][l] = dccw[:CONF_K]
        g["conf_ln_g"][l] = dlg[0]
        g["conf_ln_b"][l] = dlb[0]
        g["norm_mix_g"][l] = g["norm_mix_g"][l][0]
        g["mix_out_g"][l] = g["mix_out_g"][l][0]
        g["norm_ffn_g"][l] = g["norm_ffn_g"][l][0]
    grads = {k: v if k in ("w_in", "w_out", "w_up", "w_down") else jnp.stack(v) for k, v in g.items()}
    grads["final_norm_g"] = d_final[0]
    return loss, dx, grads, early


_ANY = pl.BlockSpec(memory_space=pl.ANY)


def _mesh_place():
    x, y, c = lax.axis_index("x"), lax.axis_index("y"), lax.axis_index("c")
    chips = [(1 - x, y), (x, 1 - y), (1 - x, 1 - y)]
    return x, y, c, 2 * x + y, chips


def _gather_stages(bufs, parts, sems):
    ici_send, ici_recv, d2d_send, d2d_recv = sems
    x, y, c, me, chips = _mesh_place()
    blk = [2 * chip[0] + chip[1] for chip in chips]
    pairs = [(p, r) for p in range(len(parts)) for r in range(3)]

    def rows(p, block, half_of):
        k, layer = parts[p]
        half = bufs[k].shape[2] // 2
        return bufs[k].at[block, layer, pl.ds(half_of * half, half), :]

    def ici(p, r, block):
        return pltpu.make_async_remote_copy(
            src_ref=rows(p, me, c), dst_ref=rows(p, block, c), send_sem=ici_send.at[3 * p + r],
            recv_sem=ici_recv.at[3 * p + r], device_id=(chips[r][0], chips[r][1], c), device_id_type=MESH)

    def d2d(p, r, half_of):
        part = rows(p, blk[r], half_of)
        return pltpu.make_async_remote_copy(
            src_ref=part, dst_ref=part, send_sem=d2d_send.at[3 * p + r], recv_sem=d2d_recv.at[3 * p + r],
            device_id=(x, y, 1 - c), device_id_type=MESH)

    def start():
        for p, r in pairs:
            ici(p, r, me).start()

    def forward(p):
        for r in range(3):
            ici(p, r, blk[r]).wait_recv()
            d2d(p, r, c).start()

    def finish():
        for p, r in pairs:
            d2d(p, r, 1 - c).wait_recv()
        for p, r in pairs:
            ici(p, r, me).wait_send()
            d2d(p, r, c).wait_send()

    return start, forward, finish


def _gather_sems(parts):
    return [pltpu.SemaphoreType.DMA((3 * len(parts),)) for _ in range(4)]


def _gather_in_steps(bufs, parts, sems, n_steps):
    start, forward, finish = _gather_stages(bufs, parts, sems)
    i = pl.program_id(0)
    pl.when(i == 0)(start)
    for p in range(len(parts)):
        pl.when(i == n_steps * (2 * p + 3) // (2 * len(parts) + 2))(lambda p=p: forward(p))
    pl.when(i == n_steps - 1)(finish)


def _gather_first(bufs, parts, whole):
    n, m = len(bufs), len(whole)

    def body(*refs):
        whole_in, buf_out, whole_out = refs[n:n + m], refs[n + m:2 * n + m], refs[2 * n + m:2 * (n + m)]
        sems = refs[2 * (n + m):]
        send_sems, recv_sems, local_sems = sems[4:]
        x, y, c, me, chips = _mesh_place()
        start, forward, finish = _gather_stages(buf_out, parts, sems[:4])

        def push(k, r, block):
            return pltpu.make_async_remote_copy(
                src_ref=whole_in[k], dst_ref=whole_out[k].at[block], send_sem=send_sems.at[3 * k + r],
                recv_sem=recv_sems.at[3 * k + r], device_id=(chips[r][0], chips[r][1], c), device_id_type=MESH)

        local = [pltpu.make_async_copy(whole_in[k], whole_out[k].at[me], local_sems.at[k]) for k in range(m)]
        for cp in local:
            cp.start()
        start()
        for k in range(m):
            for r in range(3):
                push(k, r, me).start()
        for p in range(len(parts)):
            forward(p)
        for k in range(m):
            for r, chip in enumerate(chips):
                push(k, r, 2 * chip[0] + chip[1]).wait_recv()
        for k in range(m):
            for r in range(3):
                push(k, r, me).wait_send()
        finish()
        for cp in local:
            cp.wait()

    return pl.pallas_call(
        body, name="gather_first",
        in_specs=[_ANY] * (n + m), out_specs=[_ANY] * (n + m),
        out_shape=[jax.ShapeDtypeStruct(b.shape, b.dtype) for b in bufs]
                  + [jax.ShapeDtypeStruct((N_BLK,) + b.shape, b.dtype) for b in whole],
        input_output_aliases={k: k for k in range(n)},
        scratch_shapes=_gather_sems(parts) + [pltpu.SemaphoreType.DMA((3 * m,)), pltpu.SemaphoreType.DMA((3 * m,)),
                                              pltpu.SemaphoreType.DMA((m,))],
    )(*bufs, *whole)


def _swap_halves(gs):
    n = len(gs)

    def body(*refs):
        ins, outs, (send_sems, recv_sems) = refs[:n], refs[n:2 * n], refs[2 * n:]
        x, y, c, _, _ = _mesh_place()
        cps = []
        for k in range(n):
            half = ins[k].shape[1] // 2
            cps.append(pltpu.make_async_remote_copy(
                src_ref=ins[k].at[:, pl.ds((1 - c) * half, half), :], dst_ref=outs[k],
                send_sem=send_sems.at[k], recv_sem=recv_sems.at[k], device_id=(x, y, 1 - c), device_id_type=MESH))
        for cp in cps:
            cp.start()
        for cp in cps:
            cp.wait()

    return pl.pallas_call(
        body, name="swap_halves", in_specs=[_ANY] * n, out_specs=[_ANY] * n,
        out_shape=[jax.ShapeDtypeStruct((g.shape[0], g.shape[1] // 2, g.shape[2]), F32) for g in gs],
        scratch_shapes=[pltpu.SemaphoreType.DMA((n,)), pltpu.SemaphoreType.DMA((n,))],
    )(*gs)


def _owner_stages(ins, outs, sems):
    send_sems, recv_sems = sems
    x, y, c, me, chips = _mesh_place()
    pairs = [(k, r) for k in range(len(ins)) for r in range(3)]

    def remote(k, r, src_block, dst_block):
        return pltpu.make_async_remote_copy(
            src_ref=ins[k].at[src_block], dst_ref=outs[k].at[dst_block], send_sem=send_sems.at[3 * k + r],
            recv_sem=recv_sems.at[3 * k + r], device_id=(chips[r][0], chips[r][1], c), device_id_type=MESH)

    def start():
        for k, r in pairs:
            remote(k, r, 2 * chips[r][0] + chips[r][1], me).start()

    def finish():
        for k, r in pairs:
            remote(k, r, me, 2 * chips[r][0] + chips[r][1]).wait_recv()
        for k, r in pairs:
            remote(k, r, 2 * chips[r][0] + chips[r][1], me).wait_send()

    return start, finish


def _owner_sems(n):
    return [pltpu.SemaphoreType.DMA((3 * n,)), pltpu.SemaphoreType.DMA((3 * n,))]


def _owners_in_steps(ins, outs, sems, first, last):
    start, finish = _owner_stages(ins, outs, sems)
    pl.when(first)(start)
    pl.when(last)(finish)


def _send_to_owners(sums):
    n = len(sums)

    def body(*refs):
        start, finish = _owner_stages(refs[:n], refs[n:2 * n], refs[2 * n:])
        start()
        finish()

    return pl.pallas_call(
        body, name="send_to_owners", in_specs=[_ANY] * n, out_specs=[_ANY] * n,
        out_shape=[jax.ShapeDtypeStruct(s.shape, s.dtype) for s in sums],
        scratch_shapes=_owner_sems(n),
    )(*sums)


def _swap_reduced(fs):
    n = len(fs)

    def body(*refs):
        ins, outs, (send_sems, recv_sems) = refs[:n], refs[n:2 * n], refs[2 * n:]
        x, y, c, _, _ = _mesh_place()
        cps = [pltpu.make_async_remote_copy(src_ref=ins[k], dst_ref=outs[k], send_sem=send_sems.at[k],
                                            recv_sem=recv_sems.at[k], device_id=(x, y, 1 - c), device_id_type=MESH)
               for k in range(n)]
        for cp in cps:
            cp.start()
        for cp in cps:
            cp.wait()

    return pl.pallas_call(
        body, name="swap_reduced", in_specs=[_ANY] * n, out_specs=[_ANY] * n,
        out_shape=[jax.ShapeDtypeStruct(f.shape, F32) for f in fs],
        scratch_shapes=[pltpu.SemaphoreType.DMA((n,)), pltpu.SemaphoreType.DMA((n,))],
    )(*fs)


def _row_tile(rows):
    return min(rows, 256)


def _chip_sums(core, grads):
    if not grads:
        return []
    return [_add_pairs(core, a, b) for a, b in zip(grads, _swap_halves(grads))]


def _add_pairs(core, g, other):
    n, half, cols = other.shape
    t = _row_tile(half)
    per_half = half // t

    def body(c_ref, a_ref, b_ref, o_ref):
        o_ref[...] = (a_ref[...] + b_ref[...]).astype(BF16)

    spec = pl.BlockSpec((None, t, cols), lambda i, j, c_ref: (i, j, 0))
    return pl.pallas_call(
        body, name="add_pairs",
        grid_spec=pltpu.PrefetchScalarGridSpec(
            num_scalar_prefetch=1, grid=(n, per_half),
            in_specs=[pl.BlockSpec((None, t, cols), lambda i, j, c_ref: (i, c_ref[0] * per_half + j, 0)), spec],
            out_specs=spec),
        out_shape=jax.ShapeDtypeStruct(other.shape, BF16), compiler_params=_params(("parallel", "parallel")),
    )(core, g, other)


def _add_chips(me, s1, r2):
    _, r, cols = r2.shape
    t = _row_tile(r)

    def body(me_ref, s_ref, r_ref, o_ref):
        own = s_ref[...].astype(F32)
        parts = [jnp.where(me_ref[0] == k, own, r_ref[k].astype(F32)) for k in range(N_BLK)]
        o_ref[...] = ((parts[0] + parts[1]) + parts[2]) + parts[3]

    return pl.pallas_call(
        body, name="add_chips",
        grid_spec=pltpu.PrefetchScalarGridSpec(
            num_scalar_prefetch=1, grid=(r // t,),
            in_specs=[pl.BlockSpec((None, t, cols), lambda i, me_ref: (me_ref[0], i, 0)),
                      pl.BlockSpec((N_BLK, t, cols), lambda i, me_ref: (0, i, 0))],
            out_specs=pl.BlockSpec((t, cols), lambda i, me_ref: (i, 0))),
        out_shape=jax.ShapeDtypeStruct((r, cols), F32), compiler_params=_params(("parallel",)),
    )(me, s1, r2)


def _adamw(core, mine, other, w, m, v, layer, earlier=None):
    half, cols = mine.shape
    t = _row_tile(half)
    per_half = half // t
    c1 = 1.0 - ADAM_B1 ** ADAM_STEP
    c2 = 1.0 - ADAM_B2 ** ADAM_STEP

    def body(c_ref, a_ref, b_ref, w_ref, m_ref, v_ref, *rest):
        g_ref, d_ref, mo_ref, vo_ref = rest[-4:]
        gv = jnp.where(pl.program_id(0) // per_half == c_ref[0], a_ref[...], b_ref[...])
        g_ref[...] = gv
        m_new = ADAM_B1 * m_ref[...] + (1.0 - ADAM_B1) * gv
        v_new = ADAM_B2 * v_ref[...] + (1.0 - ADAM_B2) * (gv * gv)
        mo_ref[...] = m_new
        vo_ref[...] = v_new
        d_ref[...] = -ADAM_LR * ((m_new / c1) / (jnp.sqrt(v_new / c2) + ADAM_EPS) + ADAM_WD * w_ref[...])

    part = pl.BlockSpec((t, cols), lambda i, c_ref: (i % per_half, 0))
    spec = pl.BlockSpec((None, t, cols), lambda i, c_ref: (layer, i, 0))
    kept = [] if earlier is None else list(earlier)
    return pl.pallas_call(
        body, name="adamw",
        grid_spec=pltpu.PrefetchScalarGridSpec(
            num_scalar_prefetch=1, grid=(2 * per_half,),
            in_specs=[part, part, spec, spec, spec] + [_ANY] * len(kept), out_specs=[spec] * 4),
        out_shape=[jax.ShapeDtypeStruct(w.shape, F32)] * 4,
        input_output_aliases={6 + k: k for k in range(len(kept))},
        compiler_params=_params(("parallel",)),
    )(core, mine, other, w, m, v, *kept)


_REPLICATED = ("norm_mix_g", "gmlp_v_g", "gmlp_w_s", "gmlp_b_s", "conf_ln_g", "conf_ln_b", "mix_out_g", "norm_ffn_g",
               "final_norm_g")
_REP_SHAPES = {"norm_mix_g": (DEPTH, D_MODEL), "gmlp_v_g": (DEPTH, D_GROUP), "gmlp_w_s": (DEPTH, N_HEADS, CHUNK, CHUNK),
               "gmlp_b_s": (DEPTH, N_HEADS, CHUNK), "conf_ln_g": (DEPTH, D_GROUP), "conf_ln_b": (DEPTH, D_GROUP),
               "mix_out_g": (DEPTH, D_MODEL), "norm_ffn_g": (DEPTH, D_MODEL), "final_norm_g": (D_MODEL,)}
_BIG = ("w_in", "w_out", "w_up", "w_down")
_CONV_ROWS = 8
_REP_ROWS = 144
_SMALL_ROWS = 160
_CH_BLK = D_GROUP // N_BLK


def _pad_rows(flat, rows):
    pad = rows * D_MODEL - flat.shape[-1]
    flat = jnp.pad(flat, [(0, 0)] * (flat.ndim - 1) + [(0, pad)])
    return flat.reshape(flat.shape[:-1] + (rows, D_MODEL))


def _pack_small(scw, ccw, rep):
    lead = scw.shape[:-3]
    conv = jnp.concatenate([scw.reshape(lead + (-1,)), ccw.reshape(lead + (-1,))], axis=-1)
    flat = jnp.concatenate([rep[k].reshape(-1) for k in _REPLICATED])
    flat = jnp.broadcast_to(flat, lead + flat.shape)
    parts = [_pad_rows(conv, _CONV_ROWS), _pad_rows(flat, _REP_ROWS),
             jnp.zeros(lead + (_SMALL_ROWS - _CONV_ROWS - _REP_ROWS, D_MODEL), F32)]
    return jnp.concatenate(parts, axis=-2)


def _unpack_small(pk):
    out = {}
    conv = pk[:_CONV_ROWS].reshape(-1)
    n_s = DEPTH * SHORT_K * _CH_BLK
    out["short_conv_w"] = conv[:n_s].reshape(DEPTH, SHORT_K, _CH_BLK)
    out["conf_conv_w"] = conv[n_s:n_s + DEPTH * CONF_K * _CH_BLK].reshape(DEPTH, CONF_K, _CH_BLK)
    row = _CONV_ROWS
    flat = pk[row:row + _REP_ROWS].reshape(-1)
    at = 0
    for k in _REPLICATED:
        n = math.prod(_REP_SHAPES[k])
        out[k] = flat[at:at + n].reshape(_REP_SHAPES[k])
        at += n
    return out


def _conv_blocks(w):
    d, k, _ = w.shape
    return w.reshape(d, k, N_BLK, _CH_BLK).transpose(2, 0, 1, 3)


_WEIGHTS = ("norm_mix_g", "w_in", "gmlp_v_g", "gmlp_w_s", "gmlp_b_s", "short_conv_w", "conf_conv_w", "conf_ln_g",
            "conf_ln_b", "mix_out_g", "w_out", "norm_ffn_g", "w_up", "w_down", "final_norm_g")


def kernel(x, norm_mix_g, w_in, gmlp_v_g, gmlp_w_s, gmlp_b_s, short_conv_w, conf_conv_w, conf_ln_g, conf_ln_b, mix_out_g, w_out, norm_ffn_g, w_up, w_down, final_norm_g, loss_target, m_norm_mix_g, m_w_in, m_gmlp_v_g, m_gmlp_w_s, m_gmlp_b_s, m_short_conv_w, m_conf_conv_w, m_conf_ln_g, m_conf_ln_b, m_mix_out_g, m_w_out, m_norm_ffn_g, m_w_up, m_w_down, m_final_norm_g, v_norm_mix_g, v_w_in, v_gmlp_v_g, v_gmlp_w_s, v_gmlp_b_s, v_short_conv_w, v_conf_conv_w, v_conf_ln_g, v_conf_ln_b, v_mix_out_g, v_w_out, v_norm_ffn_g, v_w_up, v_w_down, v_final_norm_g):
    w = dict(norm_mix_g=norm_mix_g, w_in=w_in, gmlp_v_g=gmlp_v_g, gmlp_w_s=gmlp_w_s, gmlp_b_s=gmlp_b_s,
             short_conv_w=short_conv_w, conf_conv_w=conf_conv_w, conf_ln_g=conf_ln_g, conf_ln_b=conf_ln_b,
             mix_out_g=mix_out_g, w_out=w_out, norm_ffn_g=norm_ffn_g, w_up=w_up, w_down=w_down, final_norm_g=final_norm_g)
    m = dict(norm_mix_g=m_norm_mix_g, w_in=m_w_in, gmlp_v_g=m_gmlp_v_g, gmlp_w_s=m_gmlp_w_s, gmlp_b_s=m_gmlp_b_s,
             short_conv_w=m_short_conv_w, conf_conv_w=m_conf_conv_w, conf_ln_g=m_conf_ln_g, conf_ln_b=m_conf_ln_b,
             mix_out_g=m_mix_out_g, w_out=m_w_out, norm_ffn_g=m_norm_ffn_g, w_up=m_w_up, w_down=m_w_down,
             final_norm_g=m_final_norm_g)
    v = dict(norm_mix_g=v_norm_mix_g, w_in=v_w_in, gmlp_v_g=v_gmlp_v_g, gmlp_w_s=v_gmlp_w_s, gmlp_b_s=v_gmlp_b_s,
             short_conv_w=v_short_conv_w, conf_conv_w=v_conf_conv_w, conf_ln_g=v_conf_ln_g, conf_ln_b=v_conf_ln_b,
             mix_out_g=v_mix_out_g, w_out=v_w_out, norm_ffn_g=v_norm_ffn_g, w_up=v_w_up, w_down=v_w_down,
             final_norm_g=v_final_norm_g)
    core = lax.axis_index("c").astype(jnp.int32).reshape(1)
    me = (2 * lax.axis_index("x") + lax.axis_index("y")).astype(jnp.int32).reshape(1)

    conv_mine = _pad_rows(jnp.concatenate([short_conv_w.reshape(-1), conf_conv_w.reshape(-1)]), _CONV_ROWS)
    big = {k: _cast_into_slot(w[k], me, "cast_" + k) for k in _BIG}
    big["w_in"], conv_all = _gather_first([big["w_in"]], [(0, 0)], [conv_mine])
    conv_all = conv_all.reshape(N_BLK, -1)
    n_s = DEPTH * SHORT_K * _CH_BLK
    scw_all = conv_all[:, :n_s].reshape(N_BLK, DEPTH, SHORT_K, _CH_BLK)
    ccw_all = conv_all[:, n_s:n_s + DEPTH * CONF_K * _CH_BLK].reshape(N_BLK, DEPTH, CONF_K, _CH_BLK)
    small = {k: w[k] for k in _REPLICATED}
    small["short_conv_w"] = scw_all.transpose(1, 2, 0, 3).reshape(DEPTH, SHORT_K, D_GROUP)
    small["conf_conv_w"] = ccw_all.transpose(1, 2, 0, 3).reshape(DEPTH, CONF_K, D_GROUP)

    loss, dx, g, early = _local_step(x[0], loss_target[0], big, small, gather_pending=True, core=core)

    where = [(k, l) for k in _BIG for l in range(DEPTH)]
    late = [kl for kl in where if kl not in early]
    sums = _chip_sums(core, [g[k][l] for k, l in late]
                      + [_pack_small(_conv_blocks(g["short_conv_w"]), _conv_blocks(g["conf_conv_w"]), g)])
    sent = {**early, **dict(zip(late + ["small"], zip(sums, _send_to_owners(sums))))}
    mine = [_add_chips(me, *sent[kl]) for kl in where + ["small"]]
    other = _swap_reduced(mine)

    done = {}
    for n, (k, l) in enumerate(where):
        done[k] = _adamw(core, mine[n], other[n], w[k], m[k], v[k], l, done.get(k))
    small_own = [_pack_small(t["short_conv_w"], t["conf_conv_w"], t)[None] for t in (w, m, v)]
    small_done = [_unpack_small(a[0]) for a in _adamw(core, mine[-1], other[-1], *small_own, 0)]

    outs = [lax.psum(loss[0, 0], ("x", "y", "c")), dx[None]]
    for kind in range(4):
        outs += [done[k][kind] if k in _BIG else small_done[kind][k] for k in _WEIGHTS]
    return tuple(outs)
```

```python
import math

import jax
import jax.numpy as jnp
from jax import lax
from jax.experimental import pallas as pl
from jax.experimental.pallas import tpu as pltpu

F32 = jnp.float32
BF16 = jnp.bfloat16

D_MODEL = 1024
D_GROUP = 256
N_HEADS = 4
HEAD_DIM = 64
CHUNK = 128
D_IN = 2560
N_BLK = 4
W_IN_BLK = D_IN // N_BLK
D_FF_BLK = 1024
DEPTH = 2
EPS = 1e-6
HALO = 32
LOSS_ROWS = 256
SHORT_K = 3
CONF_K = 31
ATT_TQ = 256
ATT_TK = 256
ATT_SCALE = 0.125
ATT_DEAD = -104.0
V7X_VMEM_LIMIT = 56 * 1024 * 1024

ADAM_LR, ADAM_B1, ADAM_B2, ADAM_EPS, ADAM_WD, ADAM_STEP = 0.001, 0.9, 0.999, 1e-08, 0.01, 10

MESH = pl.DeviceIdType.MESH


def _params(sem, vmem=None):
    return pltpu.CompilerParams(dimension_semantics=sem, vmem_limit_bytes=vmem)


def _tile(s, t):
    return min(s, t)


def _rsqrt_mean(v):
    return lax.rsqrt(jnp.mean(v * v, axis=-1, keepdims=True) + EPS)


def _sigmoid(v):
    return 1.0 / (1.0 + jnp.exp(-v))


_GELU_C = math.sqrt(2.0 / math.pi)


def _gelu_tanh(v):
    return jnp.tanh(_GELU_C * (v + 0.044715 * (v * v * v)))


def _gelu(v, t):
    return v * (0.5 * (1.0 + t))


def _gelu_grad(v, t):
    return 0.5 * (1.0 + t) + v * (0.5 * (1.0 - t * t) * _GELU_C * (1.0 + 3.0 * 0.044715 * (v * v)))


def _dot(a, b):
    return jnp.dot(a, b, preferred_element_type=F32)


def _dot_nt(a, b):
    return lax.dot_general(a, b, (((1,), (1,)), ((), ())), preferred_element_type=F32)


def _dot_tn(a, b):
    return lax.dot_general(a, b, (((0,), (0,)), ((), ())), preferred_element_type=F32)


def _cast_into_slot(w, me, name):
    n, r, c = w.shape
    tr = _tile(r, 256)

    def body(me_ref, w_ref, o_ref):
        o_ref[...] = w_ref[...].astype(BF16)

    return pl.pallas_call(
        body, name=name,
        grid_spec=pltpu.PrefetchScalarGridSpec(
            num_scalar_prefetch=1, grid=(n, r // tr),
            in_specs=[pl.BlockSpec((None, tr, c), lambda a, b, me_ref: (a, b, 0))],
            out_specs=pl.BlockSpec((None, None, tr, c), lambda a, b, me_ref: (me_ref[0], a, b, 0))),
        out_shape=jax.ShapeDtypeStruct((N_BLK,) + w.shape, BF16),
        compiler_params=_params(("parallel", "parallel")),
    )(me, w)


def _split_heads(xv, rows_ref, cols_ref):
    if rows_ref is not None:
        for h in range(N_HEADS):
            rows_ref[h] = xv[:, h * HEAD_DIM:(h + 1) * HEAD_DIM].astype(BF16)
    if cols_ref is not None:
        xt = xv.T
        for h in range(N_HEADS):
            cols_ref[h] = xt[h * HEAD_DIM:(h + 1) * HEAD_DIM, :].astype(BF16)


def _head_specs(t, s):
    rows = (pl.BlockSpec((N_HEADS, t, HEAD_DIM), lambda i: (0, i, 0)), jax.ShapeDtypeStruct((N_HEADS, s, HEAD_DIM), BF16))
    cols = (pl.BlockSpec((N_HEADS, HEAD_DIM, t), lambda i: (0, 0, i)), jax.ShapeDtypeStruct((N_HEADS, HEAD_DIM, s), BF16))
    return rows, cols


def _cols(ref, lo, hi):
    return ref[:, lo:hi].astype(F32)


def _pair_blocks(w_ref, wide_ref):
    @pl.when(pl.program_id(0) == 0)
    def _():
        for b in range(N_BLK):
            wide_ref[b // 2, :, (b % 2) * W_IN_BLK:(b % 2 + 1) * W_IN_BLK] = w_ref[b]


def _in_proj(x, g, w, layer):
    s = x.shape[0]
    t = _tile(s, 512)

    def body(x_ref, g_ref, w_ref, z_ref, h_ref, qr_ref, qt_ref, kt_ref, vt_ref, wide_ref):
        _pair_blocks(w_ref, wide_ref)
        xv = x_ref[...]
        h = (xv * _rsqrt_mean(xv) * g_ref[...]).astype(BF16)
        h_ref[...] = h
        for n in range(N_BLK // 2):
            z_ref[:, 2 * n * W_IN_BLK:2 * (n + 1) * W_IN_BLK] = _dot(h, wide_ref[n]).astype(BF16)
        _split_heads(_cols(z_ref, 1280, 1536) * ATT_SCALE, qr_ref, qt_ref)
        _split_heads(_cols(z_ref, 1536, 1792), None, kt_ref)
        _split_heads(_cols(z_ref, 1792, 2048), None, vt_ref)

    rows, cols = _head_specs(t, s)
    return pl.pallas_call(
        body, name="in_proj", grid=(s // t,),
        in_specs=[pl.BlockSpec((t, D_MODEL), lambda i: (i, 0)), _full((1, D_MODEL)),
                  pl.BlockSpec((N_BLK, None, D_MODEL, W_IN_BLK), lambda i: (0, layer, 0, 0))],
        out_specs=[pl.BlockSpec((t, D_IN), lambda i: (i, 0)), pl.BlockSpec((t, D_MODEL), lambda i: (i, 0)),
                   rows[0], cols[0], cols[0], cols[0]],
        out_shape=[jax.ShapeDtypeStruct((s, D_IN), BF16), jax.ShapeDtypeStruct((s, D_MODEL), BF16),
                   rows[1], cols[1], cols[1], cols[1]],
        scratch_shapes=[pltpu.VMEM((N_BLK // 2, D_MODEL, 2 * W_IN_BLK), BF16)],
        compiler_params=_params(("arbitrary",), V7X_VMEM_LIMIT),
    )(x, g, w)


def _mix_a_fwd(z_ref, vg, wt_ref, bmat, t):
    zu = _cols(z_ref, 0, 256)
    zv = _cols(z_ref, 256, 512)
    tu = _gelu_tanh(zu)
    tv = _gelu_tanh(zv)
    u = _gelu(zu, tu)
    v = _gelu(zv, tv)
    rv = _rsqrt_mean(v)
    vh = v * rv
    vnb = (vh * vg).astype(BF16)
    head = lax.broadcasted_iota(jnp.int32, (CHUNK, D_GROUP), 1) // HEAD_DIM
    fs = []
    for c in range(t // CHUNK):
        vc = vnb[c * CHUNK:(c + 1) * CHUNK, :]
        fc = bmat
        for h in range(N_HEADS):
            fc = fc + jnp.where(head == h, _dot(wt_ref[h], vc), 0.0)
        fs.append(fc)
    f = jnp.concatenate(fs, axis=0) if len(fs) > 1 else fs[0]
    return (zu, tu), (zv, tv), u, rv, vh, vnb, f


def _windows(ext_ref, sh_ref, t):
    for b in range(1, 8):
        sh_ref[b - 1] = ext_ref[pl.ds(b, HALO + t - 8), :]

    def window(o):
        a, b = divmod(o, 8)
        return ext_ref[pl.ds(8 * a, t), :] if b == 0 else sh_ref[b - 1, pl.ds(8 * a, t), :]

    return window


def _mix_b_fwd(z_ref, zh_ref, first, scw_ref, ext_ref, t):
    gb = _cols(z_ref, 512, 768)
    uh = _cols(zh_ref, 768, 1024) * _cols(zh_ref, 1024, 1280)
    ext_ref[0:HALO, :] = jnp.where(first, 0.0, uh)
    ext_ref[HALO:HALO + t, :] = _cols(z_ref, 768, 1024) * _cols(z_ref, 1024, 1280)
    cv = jnp.zeros((t, D_GROUP), F32)
    for k in range(SHORT_K):
        cv = cv + scw_ref[k:k + 1, :] * ext_ref[pl.ds(HALO - (SHORT_K - 1) + k, t), :]
    return gb, cv


def _mix_d_fwd(z_ref, zh_ref, first, ccw_ref, lg, lb, ext_ref, sh_ref, t):
    hh = _cols(zh_ref, 2048, 2304) * _sigmoid(_cols(zh_ref, 2304, 2560))
    ext_ref[0:HALO, :] = jnp.where(first, 0.0, hh)
    ext_ref[HALO:HALO + t, :] = _cols(z_ref, 2048, 2304) * _sigmoid(_cols(z_ref, 2304, 2560))
    window = _windows(ext_ref, sh_ref, t)
    cv = jnp.zeros((t, D_GROUP), F32)
    for k in range(CONF_K):
        cv = cv + ccw_ref[k:k + 1, :] * window(HALO - (CONF_K - 1) + k)
    xc = cv - jnp.mean(cv, axis=-1, keepdims=True)
    rs = lax.rsqrt(jnp.mean(xc * xc, axis=-1, keepdims=True) + EPS)
    xh = xc * rs
    ln = xh * lg + lb
    return xh, rs, ln, _sigmoid(ln), window


def _mix_specs(t, s):
    per = t // HALO
    return [pl.BlockSpec((t, D_IN), lambda i: (i, 0)),
            pl.BlockSpec((HALO, D_IN), lambda i: (jnp.maximum(i * per - 1, 0), 0))]


def _full(shape):
    return pl.BlockSpec(shape, lambda i: (0,) * len(shape))


def _mixers_fwd(z, p, bufs=(), parts=()):
    s = z.shape[0]
    t = _tile(s, 256)
    nb = len(bufs)

    def body(z_ref, zh_ref, vg_ref, wt_ref, bm_ref, scw_ref, ccw_ref, lg_ref, lb_ref, *rest):
        y_ref = rest[nb]
        eb_ref, ed_ref, sh_ref = rest[2 * nb + 1:2 * nb + 4]
        if nb:
            _gather_in_steps(rest[nb + 1:2 * nb + 1], parts, rest[2 * nb + 4:], s // t)
        first = pl.program_id(0) == 0
        _, _, u, _, _, _, f = _mix_a_fwd(z_ref, vg_ref[...], wt_ref, bm_ref[...], t)
        ya = u * f
        y_ref[:, 0:256] = ya * _rsqrt_mean(ya)
        gb, cv = _mix_b_fwd(z_ref, zh_ref, first, scw_ref, eb_ref, t)
        yb = gb * cv
        y_ref[:, 256:512] = yb * _rsqrt_mean(yb)
        _, _, ln, sg, _ = _mix_d_fwd(z_ref, zh_ref, first, ccw_ref, lg_ref[...], lb_ref[...], ed_ref, sh_ref, t)
        yd = ln * sg
        y_ref[:, 512:768] = yd * _rsqrt_mean(yd)

    out = pl.pallas_call(
        body, name="mixers_fwd", grid=(s // t,),
        in_specs=_mix_specs(t, s) + [_full((1, D_GROUP)), _full((N_HEADS, CHUNK, CHUNK)), _full((CHUNK, D_GROUP)),
                                     _full((8, D_GROUP)), _full((32, D_GROUP)), _full((1, D_GROUP)), _full((1, D_GROUP))]
                 + [_ANY] * nb,
        out_specs=[pl.BlockSpec((t, 768), lambda i: (i, 0))] + [_ANY] * nb,
        out_shape=[jax.ShapeDtypeStruct((s, 768), F32)] + [jax.ShapeDtypeStruct(b.shape, b.dtype) for b in bufs],
        input_output_aliases={9 + k: 1 + k for k in range(nb)},
        scratch_shapes=[pltpu.VMEM((HALO + t, D_GROUP), F32), pltpu.VMEM((HALO + t, D_GROUP), F32),
                        pltpu.VMEM((7, HALO + t - 8, D_GROUP), F32)] + (_gather_sems(parts) if nb else []),
        compiler_params=_params(("arbitrary",) if nb else ("parallel",)),
    )(z, z, p["vg"], p["wt"], p["bmat"], p["scw"], p["ccw"], p["lg"], p["lb"], *bufs)
    return (out[0], out[1:]) if nb else out[0]


def _mixers_bwd_a(z, dyn, o_t, p, sums=()):
    s = z.shape[0]
    t = _tile(s, 256)
    n_chunk = t // CHUNK
    ns = len(sums)

    def body(*refs):
        (z_ref, zh_ref, dyn_ref, ot_ref, vg_ref, wt_ref, wtt_ref, bm_ref, scw_ref, ccw_ref, lg_ref, lb_ref) = refs[:12]
        (dza_ref, dcb_ref, dcd_ref, dor_ref, dot_ref, ds_ref, dvg_ref, dws_ref, dbm_ref, dscw_ref, dccw_ref, dlg_ref,
         dlb_ref) = refs[12 + ns:25 + ns]
        eb_ref, ed_ref, sh_ref = refs[25 + 2 * ns:28 + 2 * ns]
        i = pl.program_id(0)
        first = i == 0
        if ns:
            _owners_in_steps(refs[12:12 + ns], refs[25 + ns:25 + 2 * ns], refs[28 + 2 * ns:], first, i == s // t - 1)

        @pl.when(first)
        def _():
            for r in (dvg_ref, dws_ref, dbm_ref, dscw_ref, dccw_ref, dlg_ref, dlb_ref):
                r[...] = jnp.zeros_like(r)

        def rms_bwd(y, dn):
            r = _rsqrt_mean(y)
            yn = y * r
            return r * (dn - yn * jnp.mean(dn * yn, axis=-1, keepdims=True))

        vg = vg_ref[...]
        gelu_u, gelu_v, u, rv, vh, vnb, f = _mix_a_fwd(z_ref, vg, wt_ref, bm_ref[...], t)
        dya = rms_bwd(u * f, _cols(dyn_ref, 0, 256))
        du = dya * f
        df = dya * u
        head = lax.broadcasted_iota(jnp.int32, (CHUNK, D_GROUP), 1) // HEAD_DIM
        dvns = []
        dbm = jnp.zeros((CHUNK, D_GROUP), F32)
        for c in range(n_chunk):
            dfc = df[c * CHUNK:(c + 1) * CHUNK, :]
            vc = vnb[c * CHUNK:(c + 1) * CHUNK, :]
            dbm = dbm + dfc
            dvn = jnp.zeros((CHUNK, D_GROUP), F32)
            for h in range(N_HEADS):
                dfh = jnp.where(head == h, dfc, 0.0).astype(BF16)
                dvn = dvn + _dot(wtt_ref[h], dfh)
                dws_ref[h] += _dot_nt(dfh, vc)
            dvns.append(dvn)
        dbm_ref[...] += dbm
        dvn = jnp.concatenate(dvns, axis=0) if n_chunk > 1 else dvns[0]
        dvg_ref[...] += jnp.sum(dvn * vh, axis=0, keepdims=True)
        dvh = dvn * vg
        dv = rv * (dvh - vh * jnp.mean(dvh * vh, axis=-1, keepdims=True))
        dza_ref[:, 0:256] = (du * _gelu_grad(*gelu_u)).astype(BF16)
        dza_ref[:, 256:512] = (dv * _gelu_grad(*gelu_v)).astype(BF16)

        gb, cv = _mix_b_fwd(z_ref, zh_ref, first, scw_ref, eb_ref, t)
        dyb = rms_bwd(gb * cv, _cols(dyn_ref, 256, 512))
        dza_ref[:, 512:768] = (dyb * cv).astype(BF16)
        dcb = dyb * gb
        dcb_ref[...] = dcb
        for k in range(SHORT_K):
            dscw_ref[k:k + 1, :] += jnp.sum(dcb * eb_ref[pl.ds(HALO - (SHORT_K - 1) + k, t), :], axis=0, keepdims=True)

        lg = lg_ref[...]
        xh, rs, ln, sg, window = _mix_d_fwd(z_ref, zh_ref, first, ccw_ref, lg, lb_ref[...], ed_ref, sh_ref, t)
        dyd = rms_bwd(ln * sg, _cols(dyn_ref, 768, 1024))
        dln = dyd * (sg * (1.0 + ln * (1.0 - sg)))
        dlg_ref[...] += jnp.sum(dln * xh, axis=0, keepdims=True)
        dlb_ref[...] += jnp.sum(dln, axis=0, keepdims=True)
        dxh = dln * lg
        dcd = rs * (dxh - jnp.mean(dxh, axis=-1, keepdims=True) - xh * jnp.mean(dxh * xh, axis=-1, keepdims=True))
        dcd_ref[...] = dcd
        for k in range(CONF_K):
            dccw_ref[k:k + 1, :] += jnp.sum(dcd * window(HALO - (CONF_K - 1) + k), axis=0, keepdims=True)

        o = ot_ref[...].reshape(D_GROUP, t).T
        do = rms_bwd(o, _cols(dyn_ref, 512, 768))
        _split_heads(do, dor_ref, dot_ref)
        prod = do.astype(BF16).astype(F32) * o
        for h in range(N_HEADS):
            ds_ref[h] = jnp.sum(prod[:, h * HEAD_DIM:(h + 1) * HEAD_DIM], axis=1, keepdims=True)

    small = [(1, D_GROUP), (N_HEADS, CHUNK, CHUNK), (CHUNK, D_GROUP), (8, D_GROUP), (32, D_GROUP), (1, D_GROUP), (1, D_GROUP)]
    rows, cols = _head_specs(t, s)
    out = pl.pallas_call(
        body, name="mixers_bwd_a", grid=(s // t,),
        in_specs=_mix_specs(t, s) + [pl.BlockSpec((t, D_MODEL), lambda i: (i, 0)),
                                     pl.BlockSpec((N_HEADS, HEAD_DIM, t), lambda i: (0, 0, i)),
                                     _full((1, D_GROUP)), _full((N_HEADS, CHUNK, CHUNK)), _full((N_HEADS, CHUNK, CHUNK)),
                                     _full((CHUNK, D_GROUP)), _full((8, D_GROUP)), _full((32, D_GROUP)),
                                     _full((1, D_GROUP)), _full((1, D_GROUP))] + [_ANY] * ns,
        out_specs=[pl.BlockSpec((t, 768), lambda i: (i, 0)), pl.BlockSpec((t, D_GROUP), lambda i: (i, 0)),
                   pl.BlockSpec((t, D_GROUP), lambda i: (i, 0)), rows[0], cols[0],
                   pl.BlockSpec((N_HEADS, t, 1), lambda i: (0, i, 0))]
                  + [_full(sh) for sh in small] + [_ANY] * ns,
        out_shape=[jax.ShapeDtypeStruct((s, 768), BF16), jax.ShapeDtypeStruct((s, D_GROUP), F32),
                   jax.ShapeDtypeStruct((s, D_GROUP), F32), rows[1], cols[1],
                   jax.ShapeDtypeStruct((N_HEADS, s, 1), F32)]
                  + [jax.ShapeDtypeStruct(sh, F32) for sh in small]
                  + [jax.ShapeDtypeStruct(a.shape, a.dtype) for a in sums],
        scratch_shapes=[pltpu.VMEM((HALO + t, D_GROUP), F32), pltpu.VMEM((HALO + t, D_GROUP), F32),
                        pltpu.VMEM((7, HALO + t - 8, D_GROUP), F32)] + (_owner_sems(ns) if ns else []),
        compiler_params=_params(("arbitrary",)),
    )(z, z, dyn, o_t, p["vg"], p["wt"], p["wtt"], p["bmat"], p["scw"], p["ccw"], p["lg"], p["lb"], *sums)
    return tuple(out[:13]) + (out[13:],) if ns else out


def _mixers_bwd_b(z, dza, dcb, dcd, dq_t, dk_t, dv_t, p):
    s = z.shape[0]
    t = _tile(s, 256)
    per = t // HALO
    n_halo = s // HALO

    def body(z_ref, dza_ref, dcb_ref, dcbn_ref, dcd_ref, dcdn_ref, dq_ref, dk_ref, dv_ref, scw_ref, ccw_ref,
             dz_ref, eb_ref, ed_ref, sh_ref):
        last = pl.program_id(0) == pl.num_programs(0) - 1
        dz_ref[:, 0:768] = dza_ref[...]
        eb_ref[0:t, :] = dcb_ref[...]
        eb_ref[t:t + HALO, :] = jnp.where(last, 0.0, dcbn_ref[...])
        du = jnp.zeros((t, D_GROUP), F32)
        for k in range(SHORT_K):
            du = du + scw_ref[k:k + 1, :] * eb_ref[pl.ds(SHORT_K - 1 - k, t), :]
        dz_ref[:, 768:1024] = (du * _cols(z_ref, 1024, 1280)).astype(BF16)
        dz_ref[:, 1024:1280] = (du * _cols(z_ref, 768, 1024)).astype(BF16)
        for n, r in enumerate((dq_ref, dk_ref, dv_ref)):
            dz_ref[:, 1280 + 256 * n:1536 + 256 * n] = r[...].reshape(D_GROUP, t).T.astype(BF16)
        ed_ref[0:t, :] = dcd_ref[...]
        ed_ref[t:t + HALO, :] = jnp.where(last, 0.0, dcdn_ref[...])
        window = _windows(ed_ref, sh_ref, t)
        dh = jnp.zeros((t, D_GROUP), F32)
        for k in range(CONF_K):
            dh = dh + ccw_ref[k:k + 1, :] * window(CONF_K - 1 - k)
        a = _cols(z_ref, 2048, 2304)
        sg = _sigmoid(_cols(z_ref, 2304, 2560))
        dz_ref[:, 2048:2304] = (dh * sg).astype(BF16)
        dz_ref[:, 2304:2560] = (dh * a * sg * (1.0 - sg)).astype(BF16)

    nxt = lambda i: (jnp.minimum((i + 1) * per, n_halo - 1), 0)
    tr = pl.BlockSpec((N_HEADS, HEAD_DIM, t), lambda i: (0, 0, i))
    return pl.pallas_call(
        body, name="mixers_bwd_b", grid=(s // t,),
        in_specs=[pl.BlockSpec((t, D_IN), lambda i: (i, 0)), pl.BlockSpec((t, 768), lambda i: (i, 0)),
                  pl.BlockSpec((t, D_GROUP), lambda i: (i, 0)), pl.BlockSpec((HALO, D_GROUP), nxt),
                  pl.BlockSpec((t, D_GROUP), lambda i: (i, 0)), pl.BlockSpec((HALO, D_GROUP), nxt),
                  tr, tr, tr, _full((8, D_GROUP)), _full((32, D_GROUP))],
        out_specs=pl.BlockSpec((t, D_IN), lambda i: (i, 0)),
        out_shape=jax.ShapeDtypeStruct((s, D_IN), BF16),
        scratch_shapes=[pltpu.VMEM((HALO + t, D_GROUP), F32), pltpu.VMEM((HALO + t, D_GROUP), F32),
                        pltpu.VMEM((7, HALO + t - 8, D_GROUP), F32)],
        compiler_params=_params(("parallel",)),
    )(z, dza, dcb, dcb, dcd, dcd, dq_t, dk_t, dv_t, p["scw"], p["ccw"])


def _split_bf16(v):
    hi = v.astype(BF16)
    return hi, (v - hi.astype(F32)).astype(BF16)


def _att_scores(qs, kts, carries, tri, mask):
    zs = [_dot(q, kt) for q, kt in zip(qs, kts)]
    lms, lbs, parts = [], [], []
    for z in zs:
        soft = jnp.log(1.0 + jnp.exp(-jnp.abs(z)))
        lm = -(jnp.maximum(z, 0.0) + soft)
        lbs.append(lm + z)
        if mask is not None:
            lm = jnp.where(mask, lm, 0.0)
        lms.append(lm)
        parts.append(_split_bf16(lm))
    rights = [_dot(hi, tri) + _dot(lo, tri) for hi, lo in parts]
    ws = []
    for lb, right, carry in zip(lbs, rights, carries):
        w = jnp.exp(lb + right + carry)
        ws.append(w if mask is None else jnp.where(mask, w, 0.0))
    return ws, lbs, [jnp.sum(lm, axis=1, keepdims=True) for lm in lms]


def _att_consts(i):
    j_hi = ((i + 1) * ATT_TQ - 1) // ATT_TK
    row = lax.broadcasted_iota(jnp.int32, (ATT_TQ, ATT_TK), 0) + i * ATT_TQ
    col = lax.broadcasted_iota(jnp.int32, (ATT_TQ, ATT_TK), 1) + j_hi * ATT_TK
    r_i = lax.broadcasted_iota(jnp.int32, (ATT_TK, ATT_TK), 0)
    c_i = lax.broadcasted_iota(jnp.int32, (ATT_TK, ATT_TK), 1)
    return j_hi, col < row, r_i, c_i


def _att_alive(j, carries):
    top = carries[0]
    for c in carries[1:]:
        top = jnp.maximum(top, c)
    return jnp.logical_and(j >= 0, jnp.max(top) > ATT_DEAD)


def _attn_fwd(q_r, k_t, v_t, bufs=(), parts=()):
    s = q_r.shape[1]
    nb = len(bufs)

    def body(q_ref, kt_ref, vt_ref, *rest):
        o_ref = rest[nb]
        if nb:
            _gather_in_steps(rest[nb + 1:2 * nb + 1], parts, rest[2 * nb + 1:], s // ATT_TQ)
        j_hi, mask, r_i, c_i = _att_consts(pl.program_id(0))
        tri = (r_i > c_i).astype(BF16)

        heads = range(N_HEADS)

        def tiles(j, carries, accs, mask):
            cols = pl.ds(pl.multiple_of(j * ATT_TK, ATT_TK), ATT_TK)
            ws, _, tots = _att_scores([q_ref[h] for h in heads], [kt_ref[h, :, cols] for h in heads], carries, tri, mask)
            accs = [acc + _dot_nt(vt_ref[h, :, cols], w.astype(BF16)) for h, acc, w in zip(heads, accs, ws)]
            return [c + t for c, t in zip(carries, tots)], accs

        state = tiles(j_hi, [jnp.zeros((ATT_TQ, 1), F32)] * N_HEADS, [jnp.zeros((HEAD_DIM, ATT_TQ), F32)] * N_HEADS, mask)

        def cond(c):
            return _att_alive(c[0], c[1])

        def step(c):
            return (c[0] - 1,) + tuple(tiles(c[0], c[1], c[2], None))

        _, _, accs = lax.while_loop(cond, step, (j_hi - 1,) + tuple(state))
        for h in heads:
            o_ref[h] = accs[h]

    whole = pl.BlockSpec((N_HEADS, HEAD_DIM, s), lambda i: (0, 0, 0), pipeline_mode=pl.Buffered(1))
    out = pl.pallas_call(
        body, name="attn_fwd", grid=(s // ATT_TQ,),
        in_specs=[pl.BlockSpec((N_HEADS, ATT_TQ, HEAD_DIM), lambda i: (0, i, 0)), whole, whole] + [_ANY] * nb,
        out_specs=[pl.BlockSpec((N_HEADS, HEAD_DIM, ATT_TQ), lambda i: (0, 0, i))] + [_ANY] * nb,
        out_shape=[jax.ShapeDtypeStruct((N_HEADS, HEAD_DIM, s), F32)] + [jax.ShapeDtypeStruct(b.shape, b.dtype) for b in bufs],
        input_output_aliases={3 + k: 1 + k for k in range(nb)},
        scratch_shapes=_gather_sems(parts) if nb else [],
        compiler_params=_params(("arbitrary",), V7X_VMEM_LIMIT),
    )(q_r, k_t, v_t, *bufs)
    return (out[0], out[1:]) if nb else out[0]


ATT_BWD_HEADS = 2


def _attn_bwd(q_r, q_t, k_t, v_t, do_r, do_t, dsum):
    s = q_r.shape[1]
    hps = ATT_BWD_HEADS

    def body(q_ref, qt_ref, kt_ref, vt_ref, do_ref, dot_ref, ds_ref, dq_ref, dk_ref, dv_ref):
        i = pl.program_id(1)

        @pl.when(i == 0)
        def _():
            dk_ref[...] = jnp.zeros_like(dk_ref)
            dv_ref[...] = jnp.zeros_like(dv_ref)

        j_hi, mask, r_i, c_i = _att_consts(i)
        tri_r = (r_i > c_i).astype(BF16)
        tri_ge = (r_i >= c_i).astype(BF16)

        heads = range(hps)

        def tiles(j, carries, gsums, accs, mask):
            cols = pl.ds(pl.multiple_of(j * ATT_TK, ATT_TK), ATT_TK)
            kts = [kt_ref[h, :, cols] for h in heads]
            das = [_dot(do_ref[h], vt_ref[h, :, cols]) for h in heads]
            ws, lbs, tots = _att_scores([q_ref[h] for h in heads], kts, carries, tri_r, mask)
            wbs = [w.astype(BF16) for w in ws]
            gs = [wb.astype(F32) * da for wb, da in zip(wbs, das)]
            parts = [_split_bf16(g) for g in gs]
            sfx = [_dot(hi, tri_ge) + _dot(lo, tri_ge) for hi, lo in parts]
            for h in heads:
                dv_ref[h, :, cols] += _dot(dot_ref[h], wbs[h])
            dzs = []
            for h in heads:
                left = ds_ref[h] - gsums[h] - sfx[h]
                dz = gs[h] - jnp.exp(lbs[h]) * (gs[h] + left)
                dzs.append((dz if mask is None else jnp.where(mask, dz, 0.0)).astype(BF16))
            for h in heads:
                dk_ref[h, :, cols] += _dot(qt_ref[h], dzs[h])
            accs = [accs[h] + _dot_nt(kts[h], dzs[h]) for h in heads]
            gsums = [gsums[h] + jnp.sum(gs[h], axis=1, keepdims=True) for h in heads]
            return [c + t for c, t in zip(carries, tots)], gsums, accs

        col0 = [jnp.zeros((ATT_TQ, 1), F32)] * hps
        state = tiles(j_hi, col0, col0, [jnp.zeros((HEAD_DIM, ATT_TQ), F32)] * hps, mask)

        def cond(c):
            return _att_alive(c[0], c[1])

        def step(c):
            return (c[0] - 1,) + tuple(tiles(c[0], c[1], c[2], c[3], None))

        _, _, _, accs = lax.while_loop(cond, step, (j_hi - 1,) + tuple(state))
        for h in heads:
            dq_ref[h] = accs[h] * ATT_SCALE

    whole = pl.BlockSpec((hps, HEAD_DIM, s), lambda g, i: (g, 0, 0))
    whole_in = pl.BlockSpec((hps, HEAD_DIM, s), lambda g, i: (g, 0, 0), pipeline_mode=pl.Buffered(1))
    rows = pl.BlockSpec((hps, ATT_TQ, HEAD_DIM), lambda g, i: (g, i, 0))
    cols = pl.BlockSpec((hps, HEAD_DIM, ATT_TQ), lambda g, i: (g, 0, i))
    shape = jax.ShapeDtypeStruct((N_HEADS, HEAD_DIM, s), F32)
    return pl.pallas_call(
        body, name="attn_bwd", grid=(N_HEADS // hps, s // ATT_TQ),
        in_specs=[rows, cols, whole_in, whole_in, rows, cols, pl.BlockSpec((hps, ATT_TQ, 1), lambda g, i: (g, i, 0))],
        out_specs=[cols, whole, whole],
        out_shape=[shape, shape, shape],
        compiler_params=_params(("parallel", "arbitrary"), V7X_VMEM_LIMIT),
    )(q_r, q_t, k_t, v_t, do_r, do_t, dsum)


def _out_proj(x, y_abd, o_t, gain, w, layer):
    s = x.shape[0]
    t = _tile(s, 512)

    def body(x_ref, y_ref, ot_ref, g_ref, w_ref, x1_ref, yn_ref):
        o = ot_ref[...].reshape(D_GROUP, t).T
        groups = [y_ref[:, 0:256], y_ref[:, 256:512], o * _rsqrt_mean(o), y_ref[:, 512:768]]
        g = g_ref[...]
        acc = None
        for b, yn in enumerate(groups):
            cols = slice(256 * b, 256 * (b + 1))
            yn_ref[:, cols] = yn.astype(BF16)
            part = _dot((yn * g[:, cols]).astype(BF16), w_ref[b])
            acc = part if acc is None else acc + part
        x1_ref[...] = x_ref[...] + acc

    return pl.pallas_call(
        body, name="out_proj", grid=(s // t,),
        in_specs=[pl.BlockSpec((t, D_MODEL), lambda i: (i, 0)), pl.BlockSpec((t, 768), lambda i: (i, 0)),
                  pl.BlockSpec((N_HEADS, HEAD_DIM, t), lambda i: (0, 0, i)), _full((1, D_MODEL)),
                  pl.BlockSpec((N_BLK, None, D_GROUP, D_MODEL), lambda i: (0, layer, 0, 0))],
        out_specs=[pl.BlockSpec((t, D_MODEL), lambda i: (i, 0)), pl.BlockSpec((t, D_MODEL), lambda i: (i, 0))],
        out_shape=[jax.ShapeDtypeStruct((s, D_MODEL), F32), jax.ShapeDtypeStruct((s, D_MODEL), BF16)],
        compiler_params=_params(("parallel",)),
    )(x, y_abd, o_t, gain, w)


def _out_proj_bwd(x1, g_ffn, dh, dx2, yn, gain, w, layer):
    s = dx2.shape[0]
    t = _tile(s, 512)

    def body(x_ref, gf_ref, dh_ref, dx2_ref, yn_ref, g_ref, w_ref, dx1_ref, dgf_ref, dyn_ref, dg_ref, dw_ref):
        @pl.when(pl.program_id(0) == 0)
        def _():
            dg_ref[...] = jnp.zeros_like(dg_ref)
            dw_ref[...] = jnp.zeros_like(dw_ref)
            dgf_ref[...] = jnp.zeros_like(dgf_ref)

        dx1, dgf = _rms_bwd_rows(x_ref[...], gf_ref[...], dh_ref[...], dx2_ref[...])
        dx1_ref[...] = dx1
        dgf_ref[...] += dgf
        dxb = dx1.astype(BF16)
        g = g_ref[...]
        yn = yn_ref[...].astype(F32)
        yg = (yn * g).astype(BF16)
        for b in range(N_BLK):
            cols = slice(256 * b, 256 * (b + 1))
            dyg = _dot_nt(dxb, w_ref[b])
            dw_ref[b] += _dot_tn(yg[:, cols], dxb)
            dg_ref[:, cols] += jnp.sum(dyg * yn[:, cols], axis=0, keepdims=True)
            dyn_ref[:, cols] = (dyg * g[:, cols]).astype(BF16)

    row = pl.BlockSpec((t, D_MODEL), lambda i: (i, 0))
    vec = _full((1, D_MODEL))
    return pl.pallas_call(
        body, name="out_proj_bwd", grid=(s // t,),
        in_specs=[row, vec, row, row, row, vec, pl.BlockSpec((N_BLK, None, D_GROUP, D_MODEL), lambda i: (0, layer, 0, 0))],
        out_specs=[row, vec, row, vec, _full((N_BLK, D_GROUP, D_MODEL))],
        out_shape=[jax.ShapeDtypeStruct((s, D_MODEL), F32), jax.ShapeDtypeStruct((1, D_MODEL), F32),
                   jax.ShapeDtypeStruct((s, D_MODEL), BF16), jax.ShapeDtypeStruct((1, D_MODEL), F32),
                   jax.ShapeDtypeStruct((N_BLK, D_GROUP, D_MODEL), F32)],
        compiler_params=_params(("arbitrary",), V7X_VMEM_LIMIT),
    )(x1, g_ffn, dh, dx2, yn, gain, w)


def _ffn(x, g, w_up, w_down, layer, head=None):
    s = x.shape[0]
    t = _tile(s, 1024)

    def body(*refs):
        if head is None:
            x_ref, g_ref, wu_ref, wd_ref, acc_ref, p_ref, h_ref = refs
        else:
            x_ref, g_ref, wu_ref, wd_ref, gf_ref, t_ref, p_ref, h_ref, l_ref, dx_ref, dxb_ref, dgf_ref, acc_ref = refs
        i, j = pl.program_id(0), pl.program_id(1)

        @pl.when(j == 0)
        def _():
            xv = x_ref[...]
            h_ref[...] = (xv * _rsqrt_mean(xv) * g_ref[...]).astype(BF16)
            acc_ref[...] = xv

        pre = _dot(h_ref[...], wu_ref[...])
        p_ref[...] = pre.astype(BF16)
        a = jnp.maximum(pre, 0.0)
        acc_ref[...] += _dot((a * a).astype(BF16), wd_ref[...])

        if head is not None:
            @pl.when(jnp.logical_and(i == 0, j == 0))
            def _():
                l_ref[...] = jnp.zeros_like(l_ref)
                dgf_ref[...] = jnp.zeros_like(dgf_ref)

            @pl.when(j == N_BLK - 1)
            def _():
                gf = gf_ref[...]

                def tail(n, carry):
                    rows = pl.ds(pl.multiple_of(n * LOSS_ROWS, LOSS_ROWS), LOSS_ROWS)
                    xv = acc_ref[rows, :]
                    r = _rsqrt_mean(xv)
                    xh = xv * r
                    err = xh * gf - t_ref[rows, :]
                    l_ref[...] += 0.5 * jnp.sum(jnp.mean(err * err, axis=-1, keepdims=True), axis=0, keepdims=True)
                    dy = err * (1.0 / D_MODEL)
                    dgf_ref[...] += jnp.sum(dy * xh, axis=0, keepdims=True)
                    dxh = dy * gf
                    dx = r * (dxh - xh * jnp.mean(dxh * xh, axis=-1, keepdims=True))
                    dx_ref[rows, :] = dx
                    dxb_ref[rows, :] = dx.astype(BF16)
                    return carry

                lax.fori_loop(0, t // LOSS_ROWS, tail, 0)

    wspec = pl.BlockSpec((None, None, D_MODEL, D_FF_BLK), lambda i, j: (j, layer, 0, 0))
    row = pl.BlockSpec((t, D_MODEL), lambda i, j: (i, 0))
    vec = pl.BlockSpec((1, D_MODEL), lambda i, j: (0, 0))
    blk = pl.BlockSpec((t, D_FF_BLK), lambda i, j: (i, j))
    act = jax.ShapeDtypeStruct((s, D_MODEL), F32)
    kept = [jax.ShapeDtypeStruct((s, N_BLK * D_FF_BLK), BF16), jax.ShapeDtypeStruct((s, D_MODEL), BF16)]
    if head is None:
        return pl.pallas_call(
            body, name="ffn", grid=(s // t, N_BLK),
            in_specs=[row, vec, wspec, wspec], out_specs=[row, blk, row], out_shape=[act] + kept,
            compiler_params=_params(("parallel", "arbitrary"), V7X_VMEM_LIMIT),
        )(x, g, w_up, w_down)
    once = pl.BlockSpec((t, D_MODEL), lambda i, j: (i, 0), pipeline_mode=pl.Buffered(1))
    return pl.pallas_call(
        body, name="ffn_loss", grid=(s // t, N_BLK),
        in_specs=[once, vec, wspec, wspec, vec, once],
        out_specs=[blk, row, pl.BlockSpec((1, 128), lambda i, j: (0, 0)), row, row, vec],
        out_shape=kept + [jax.ShapeDtypeStruct((1, 128), F32), act, jax.ShapeDtypeStruct((s, D_MODEL), BF16),
                          jax.ShapeDtypeStruct((1, D_MODEL), F32)],
        scratch_shapes=[pltpu.VMEM((t, D_MODEL), F32)],
        compiler_params=_params(("arbitrary", "arbitrary"), V7X_VMEM_LIMIT),
    )(x, g, w_up, w_down, *head)


def _rms_bwd_rows(xv, g, dh, dres):
    r = _rsqrt_mean(xv)
    xh = xv * r
    dxh = dh * g
    dx = dres + r * (dxh - xh * jnp.mean(dxh * xh, axis=-1, keepdims=True))
    return dx, jnp.sum(dh * xh, axis=0, keepdims=True)


def _ffn_bwd(dxb, p, w_up, w_down, layer, sums=()):
    s = dxb.shape[0]
    t = _tile(s, 1024)
    ns = len(sums)

    def body(dx_ref, p_ref, wu_ref, wd_ref, *rest):
        dp_ref, dh_ref = rest[ns:ns + 2]
        if ns:
            i, j = pl.program_id(0), pl.program_id(1)
            _owners_in_steps(rest[:ns], rest[ns + 2:2 * ns + 2], rest[2 * ns + 2:],
                             jnp.logical_and(i == 0, j == 0), jnp.logical_and(i == s // t - 1, j == N_BLK - 1))
        da = _dot_nt(dx_ref[...], wd_ref[...])
        a = jnp.maximum(p_ref[...].astype(F32), 0.0)
        dp = (da * (2.0 * a)).astype(BF16)
        dp_ref[...] = dp
        dh = _dot_nt(dp, wu_ref[...])

        @pl.when(pl.program_id(1) == 0)
        def _():
            dh_ref[...] = dh

        @pl.when(pl.program_id(1) != 0)
        def _():
            dh_ref[...] += dh

    wspec = pl.BlockSpec((None, None, D_MODEL, D_FF_BLK), lambda i, j: (j, layer, 0, 0))
    row = pl.BlockSpec((t, D_MODEL), lambda i, j: (i, 0))
    blk = pl.BlockSpec((t, D_FF_BLK), lambda i, j: (i, j))
    out = pl.pallas_call(
        body, name="ffn_bwd", grid=(s // t, N_BLK),
        in_specs=[row, blk, wspec, wspec] + [_ANY] * ns, out_specs=[blk, row] + [_ANY] * ns,
        out_shape=[jax.ShapeDtypeStruct((s, N_BLK * D_FF_BLK), BF16), jax.ShapeDtypeStruct((s, D_MODEL), F32)]
                  + [jax.ShapeDtypeStruct(a.shape, a.dtype) for a in sums],
        scratch_shapes=_owner_sems(ns) if ns else [],
        compiler_params=_params(("arbitrary" if ns else "parallel", "arbitrary"), V7X_VMEM_LIMIT),
    )(dxb, p, w_up, w_down, *sums)
    return (out[0], out[1], out[2:]) if ns else out


def _ffn_wgrad(hb, p, dp, dxb):
    s = hb.shape[0]
    t = _tile(s, 1024)

    def body(h_ref, p_ref, dp_ref, dx_ref, du_ref, dd_ref):
        @pl.when(pl.program_id(1) == 0)
        def _():
            du_ref[...] = jnp.zeros_like(du_ref)
            dd_ref[...] = jnp.zeros_like(dd_ref)

        a = jnp.maximum(p_ref[...].astype(F32), 0.0)
        du_ref[...] += _dot_tn(h_ref[...], dp_ref[...])
        dd_ref[...] += _dot_tn((a * a).astype(BF16), dx_ref[...])

    row = pl.BlockSpec((t, D_MODEL), lambda j, i: (i, 0))
    blk = pl.BlockSpec((t, D_FF_BLK), lambda j, i: (i, j))
    out = pl.BlockSpec((None, D_MODEL, D_FF_BLK), lambda j, i: (j, 0, 0))
    shape = jax.ShapeDtypeStruct((N_BLK, D_MODEL, D_FF_BLK), F32)
    return pl.pallas_call(
        body, name="ffn_wgrad", grid=(N_BLK, s // t),
        in_specs=[row, blk, blk, row], out_specs=[out, out], out_shape=[shape, shape],
        compiler_params=_params(("parallel", "arbitrary"), V7X_VMEM_LIMIT),
    )(hb, p, dp, dxb)


def _in_proj_bwd(x, g, dx1, dz, w, layer):
    s = x.shape[0]
    t = _tile(s, 512)

    def body(x_ref, g_ref, dx1_ref, dz_ref, w_ref, dx0_ref, dxb_ref, dg_ref, wide_ref):
        _pair_blocks(w_ref, wide_ref)

        @pl.when(pl.program_id(0) == 0)
        def _():
            dg_ref[...] = jnp.zeros_like(dg_ref)

        dh = _dot_nt(dz_ref[:, 0:2 * W_IN_BLK], wide_ref[0])
        for n in range(1, N_BLK // 2):
            dh = dh + _dot_nt(dz_ref[:, 2 * n * W_IN_BLK:2 * (n + 1) * W_IN_BLK], wide_ref[n])
        dx, dg = _rms_bwd_rows(x_ref[...], g_ref[...], dh, dx1_ref[...])
        dx0_ref[...] = dx
        dxb_ref[...] = dx.astype(BF16)
        dg_ref[...] += dg

    row = pl.BlockSpec((t, D_MODEL), lambda i: (i, 0))
    return pl.pallas_call(
        body, name="in_proj_bwd", grid=(s // t,),
        in_specs=[row, _full((1, D_MODEL)), row, pl.BlockSpec((t, D_IN), lambda i: (i, 0)),
                  pl.BlockSpec((N_BLK, None, D_MODEL, W_IN_BLK), lambda i: (0, layer, 0, 0))],
        out_specs=[row, row, _full((1, D_MODEL))],
        out_shape=[jax.ShapeDtypeStruct((s, D_MODEL), F32), jax.ShapeDtypeStruct((s, D_MODEL), BF16),
                   jax.ShapeDtypeStruct((1, D_MODEL), F32)],
        scratch_shapes=[pltpu.VMEM((N_BLK // 2, D_MODEL, 2 * W_IN_BLK), BF16)],
        compiler_params=_params(("arbitrary",), V7X_VMEM_LIMIT),
    )(x, g, dx1, dz, w)


def _in_proj_wgrad(hb, dz):
    s = hb.shape[0]
    t = _tile(s, 512)

    def body(h_ref, dz_ref, dw_ref, wide_ref):
        @pl.when(pl.program_id(0) == 0)
        def _():
            wide_ref[...] = jnp.zeros_like(wide_ref)

        h = h_ref[...]
        for n in range(N_BLK // 2):
            wide_ref[n] += _dot_tn(h, dz_ref[:, 2 * n * W_IN_BLK:2 * (n + 1) * W_IN_BLK])

        @pl.when(pl.program_id(0) == s // t - 1)
        def _():
            for b in range(N_BLK):
                dw_ref[b] = wide_ref[b // 2, :, (b % 2) * W_IN_BLK:(b % 2 + 1) * W_IN_BLK]

    return pl.pallas_call(
        body, name="in_proj_wgrad", grid=(s // t,),
        in_specs=[pl.BlockSpec((t, D_MODEL), lambda i: (i, 0)), pl.BlockSpec((t, D_IN), lambda i: (i, 0))],
        out_specs=_full((N_BLK, D_MODEL, W_IN_BLK)),
        out_shape=jax.ShapeDtypeStruct((N_BLK, D_MODEL, W_IN_BLK), F32),
        scratch_shapes=[pltpu.VMEM((N_BLK // 2, D_MODEL, 2 * W_IN_BLK), F32)],
        compiler_params=_params(("arbitrary",), V7X_VMEM_LIMIT),
    )(hb, dz)


def _layer_params(small, layer):
    tril = jnp.tril(jnp.ones((CHUNK, CHUNK), bool))
    ws = jnp.where(tril, small["gmlp_w_s"][layer], 0.0)
    bmat = jnp.repeat(small["gmlp_b_s"][layer].T, HEAD_DIM, axis=1)
    scw = jnp.zeros((8, D_GROUP), F32).at[:SHORT_K].set(small["short_conv_w"][layer])
    ccw = jnp.zeros((32, D_GROUP), F32).at[:CONF_K].set(small["conf_conv_w"][layer])
    return dict(vg=small["gmlp_v_g"][layer][None], wt=ws.astype(BF16), wtt=jnp.swapaxes(ws, 1, 2).astype(BF16),
                bmat=bmat, scw=scw, ccw=ccw, lg=small["conf_ln_g"][layer][None], lb=small["conf_ln_b"][layer][None])


def _local_step(x, target, big, small, gather_pending=False, core=None):
    saved = []
    for l in range(DEPTH):
        p = _layer_params(small, l)
        z, hb, q_r, q_t, k_t, v_t = _in_proj(x, small["norm_mix_g"][l][None], big["w_in"], l)
        if gather_pending and l == 0:
            late = ("w_out", "w_up", "w_down")
            y_abd, filled = _mixers_fwd(z, p, [big[k] for k in late], [(n, 0) for n in range(len(late))])
            big = {**big, **dict(zip(late, filled))}
            o_t, filled = _attn_fwd(q_r, k_t, v_t, [big[k] for k in _BIG], [(n, 1) for n in range(len(_BIG))])
            big = dict(zip(_BIG, filled))
        else:
            y_abd = _mixers_fwd(z, p)
            o_t = _attn_fwd(q_r, k_t, v_t)
        x1, yn = _out_proj(x, y_abd, o_t, small["mix_out_g"][l][None], big["w_out"], l)
        x0 = x
        if l < DEPTH - 1:
            x, pre, h2b = _ffn(x1, small["norm_ffn_g"][l][None], big["w_up"], big["w_down"], l)
        else:
            pre, h2b, loss, dx, dxb, d_final = _ffn(x1, small["norm_ffn_g"][l][None], big["w_up"], big["w_down"], l,
                                                    (small["final_norm_g"][None], target))
        saved.append(dict(p=p, x0=x0, z=z, hb=hb, q_r=q_r, q_t=q_t, k_t=k_t, v_t=v_t, o_t=o_t, x1=x1, yn=yn, pre=pre,
                          h2b=h2b))

    g = {k: [None] * DEPTH for k in ("w_in", "w_out", "w_up", "w_down", "norm_mix_g", "gmlp_v_g", "gmlp_w_s", "gmlp_b_s",
                                     "short_conv_w", "conf_conv_w", "conf_ln_g", "conf_ln_b", "mix_out_g", "norm_ffn_g")}
    tril = jnp.tril(jnp.ones((CHUNK, CHUNK), bool))
    early = {}
    for l in reversed(range(DEPTH)):
        sv = saved[l]
        p = sv["p"]
        riding = []
        if core is not None and l == 0:
            riding = [(k, 1) for k in _BIG]
        sums = _chip_sums(core, [g[k][n] for k, n in riding])
        dpre, dh, *got = _ffn_bwd(dxb, sv["pre"], big["w_up"], big["w_down"], l, sums)
        early.update(zip(riding, zip(sums, *got)))
        g["w_up"][l], g["w_down"][l] = _ffn_wgrad(sv["h2b"], sv["pre"], dpre, dxb)
        dx1, g["norm_ffn_g"][l], dyn, g["mix_out_g"][l], g["w_out"][l] = _out_proj_bwd(
            sv["x1"], small["norm_ffn_g"][l][None], dh, dx, sv["yn"], small["mix_out_g"][l][None], big["w_out"], l)
        if core is not None and l == 0:
            riding = [("w_up", 0), ("w_down", 0)]
        sums = _chip_sums(core, [g[k][n] for k, n in riding])
        (dza, dcb, dcd, do_r, do_t, dsum, dvg, dws, dbm, dscw, dccw, dlg, dlb, *got) = _mixers_bwd_a(
            sv["z"], dyn, sv["o_t"], p, sums)
        early.update(zip(riding, zip(sums, *got)))
        dq_t, dk_t, dv_t = _attn_bwd(sv["q_r"], sv["q_t"], sv["k_t"], sv["v_t"], do_r, do_t, dsum)
        dz = _mixers_bwd_b(sv["z"], dza, dcb, dcd, dq_t, dk_t, dv_t, p)
        dx, dxb, g["norm_mix_g"][l] = _in_proj_bwd(sv["x0"], small["norm_mix_g"][l][None], dx1, dz, big["w_in"], l)
        g["w_in"][l] = _in_proj_wgrad(sv["hb"], dz)
        g["gmlp_v_g"][l] = dvg[0]
        g["gmlp_w_s"][l] = jnp.where(tril, dws, 0.0)
        g["gmlp_b_s"][l] = dbm.reshape(CHUNK, N_HEADS, HEAD_DIM).sum(-1).T
        g["short_conv_w"][l] = dscw[:SHORT_K]
        g["conf_conv_w"][l] = dccw[:CONF_K]
        g["conf_ln_g"][l] = dlg[0]
        g["conf_ln_b"][l] = dlb[0]
        g["norm_mix_g"][l] = g["norm_mix_g"][l][0]
        g["mix_out_g"][l] = g["mix_out_g"][l][0]
        g["norm_ffn_g"][l] = g["norm_ffn_g"][l][0]
    grads = {k: v if k in ("w_in", "w_out", "w_up", "w_down") else jnp.stack(v) for k, v in g.items()}
    grads["final_norm_g"] = d_final[0]
    return loss, dx, grads, early


_ANY = pl.BlockSpec(memory_space=pl.ANY)


def _mesh_place():
    x, y, c = lax.axis_index("x"), lax.axis_index("y"), lax.axis_index("c")
    chips = [(1 - x, y), (x, 1 - y), (1 - x, 1 - y)]
    return x, y, c, 2 * x + y, chips


def _gather_stages(bufs, parts, sems):
    ici_send, ici_recv, d2d_send, d2d_recv = sems
    x, y, c, me, chips = _mesh_place()
    blk = [2 * chip[0] + chip[1] for chip in chips]
    pairs = [(p, r) for p in range(len(parts)) for r in range(3)]

    def rows(p, block, half_of):
        k, layer = parts[p]
        half = bufs[k].shape[2] // 2
        return bufs[k].at[block, layer, pl.ds(half_of * half, half), :]

    def ici(p, r, block):
        return pltpu.make_async_remote_copy(
            src_ref=rows(p, me, c), dst_ref=rows(p, block, c), send_sem=ici_send.at[3 * p + r],
            recv_sem=ici_recv.at[3 * p + r], device_id=(chips[r][0], chips[r][1], c), device_id_type=MESH)

    def d2d(p, r, half_of):
        part = rows(p, blk[r], half_of)
        return pltpu.make_async_remote_copy(
            src_ref=part, dst_ref=part, send_sem=d2d_send.at[3 * p + r], recv_sem=d2d_recv.at[3 * p + r],
            device_id=(x, y, 1 - c), device_id_type=MESH)

    def start():
        for p, r in pairs:
            ici(p, r, me).start()

    def forward(p):
        for r in range(3):
            ici(p, r, blk[r]).wait_recv()
            d2d(p, r, c).start()

    def finish():
        for p, r in pairs:
            d2d(p, r, 1 - c).wait_recv()
        for p, r in pairs:
            ici(p, r, me).wait_send()
            d2d(p, r, c).wait_send()

    return start, forward, finish


def _gather_sems(parts):
    return [pltpu.SemaphoreType.DMA((3 * len(parts),)) for _ in range(4)]


def _gather_in_steps(bufs, parts, sems, n_steps):
    start, forward, finish = _gather_stages(bufs, parts, sems)
    i = pl.program_id(0)
    pl.when(i == 0)(start)
    for p in range(len(parts)):
        pl.when(i == n_steps * (2 * p + 3) // (2 * len(parts) + 2))(lambda p=p: forward(p))
    pl.when(i == n_steps - 1)(finish)


def _gather_first(bufs, parts, whole):
    n, m = len(bufs), len(whole)

    def body(*refs):
        whole_in, buf_out, whole_out = refs[n:n + m], refs[n + m:2 * n + m], refs[2 * n + m:2 * (n + m)]
        sems = refs[2 * (n + m):]
        send_sems, recv_sems, local_sems = sems[4:]
        x, y, c, me, chips = _mesh_place()
        start, forward, finish = _gather_stages(buf_out, parts, sems[:4])

        def push(k, r, block):
            return pltpu.make_async_remote_copy(
                src_ref=whole_in[k], dst_ref=whole_out[k].at[block], send_sem=send_sems.at[3 * k + r],
                recv_sem=recv_sems.at[3 * k + r], device_id=(chips[r][0], chips[r][1], c), device_id_type=MESH)

        local = [pltpu.make_async_copy(whole_in[k], whole_out[k].at[me], local_sems.at[k]) for k in range(m)]
        for cp in local:
            cp.start()
        start()
        for k in range(m):
            for r in range(3):
                push(k, r, me).start()
        for p in range(len(parts)):
            forward(p)
        for k in range(m):
            for r, chip in enumerate(chips):
                push(k, r, 2 * chip[0] + chip[1]).wait_recv()
        for k in range(m):
            for r in range(3):
                push(k, r, me).wait_send()
        finish()
        for cp in local:
            cp.wait()

    return pl.pallas_call(
        body, name="gather_first",
        in_specs=[_ANY] * (n + m), out_specs=[_ANY] * (n + m),
        out_shape=[jax.ShapeDtypeStruct(b.shape, b.dtype) for b in bufs]
                  + [jax.ShapeDtypeStruct((N_BLK,) + b.shape, b.dtype) for b in whole],
        input_output_aliases={k: k for k in range(n)},
        scratch_shapes=_gather_sems(parts) + [pltpu.SemaphoreType.DMA((3 * m,)), pltpu.SemaphoreType.DMA((3 * m,)),
                                              pltpu.SemaphoreType.DMA((m,))],
    )(*bufs, *whole)


def _swap_halves(gs):
    n = len(gs)

    def body(*refs):
        ins, outs, (send_sems, recv_sems) = refs[:n], refs[n:2 * n], refs[2 * n:]
        x, y, c, _, _ = _mesh_place()
        cps = []
        for k in range(n):
            half = ins[k].shape[1] // 2
            cps.append(pltpu.make_async_remote_copy(
                src_ref=ins[k].at[:, pl.ds((1 - c) * half, half), :], dst_ref=outs[k],
                send_sem=send_sems.at[k], recv_sem=recv_sems.at[k], device_id=(x, y, 1 - c), device_id_type=MESH))
        for cp in cps:
            cp.start()
        for cp in cps:
            cp.wait()

    return pl.pallas_call(
        body, name="swap_halves", in_specs=[_ANY] * n, out_specs=[_ANY] * n,
        out_shape=[jax.ShapeDtypeStruct((g.shape[0], g.shape[1] // 2, g.shape[2]), F32) for g in gs],
        scratch_shapes=[pltpu.SemaphoreType.DMA((n,)), pltpu.SemaphoreType.DMA((n,))],
    )(*gs)


def _owner_stages(ins, outs, sems):
    send_sems, recv_sems = sems
    x, y, c, me, chips = _mesh_place()
    pairs = [(k, r) for k in range(len(ins)) for r in range(3)]

    def remote(k, r, src_block, dst_block):
        return pltpu.make_async_remote_copy(
            src_ref=ins[k].at[src_block], dst_ref=outs[k].at[dst_block], send_sem=send_sems.at[3 * k + r],
            recv_sem=recv_sems.at[3 * k + r], device_id=(chips[r][0], chips[r][1], c), device_id_type=MESH)

    def start():
        for k, r in pairs:
            remote(k, r, 2 * chips[r][0] + chips[r][1], me).start()

    def finish():
        for k, r in pairs:
            remote(k, r, me, 2 * chips[r][0] + chips[r][1]).wait_recv()
        for k, r in pairs:
            remote(k, r, 2 * chips[r][0] + chips[r][1], me).wait_send()

    return start, finish


def _owner_sems(n):
    return [pltpu.SemaphoreType.DMA((3 * n,)), pltpu.SemaphoreType.DMA((3 * n,))]


def _owners_in_steps(ins, outs, sems, first, last):
    start, finish = _owner_stages(ins, outs, sems)
    pl.when(first)(start)
    pl.when(last)(finish)


def _send_to_owners(sums):
    n = len(sums)

    def body(*refs):
        start, finish = _owner_stages(refs[:n], refs[n:2 * n], refs[2 * n:])
        start()
        finish()

    return pl.pallas_call(
        body, name="send_to_owners", in_specs=[_ANY] * n, out_specs=[_ANY] * n,
        out_shape=[jax.ShapeDtypeStruct(s.shape, s.dtype) for s in sums],
        scratch_shapes=_owner_sems(n),
    )(*sums)


def _swap_reduced(fs):
    n = len(fs)

    def body(*refs):
        ins, outs, (send_sems, recv_sems) = refs[:n], refs[n:2 * n], refs[2 * n:]
        x, y, c, _, _ = _mesh_place()
        cps = [pltpu.make_async_remote_copy(src_ref=ins[k], dst_ref=outs[k], send_sem=send_sems.at[k],
                                            recv_sem=recv_sems.at[k], device_id=(x, y, 1 - c), device_id_type=MESH)
               for k in range(n)]
        for cp in cps:
            cp.start()
        for cp in cps:
            cp.wait()

    return pl.pallas_call(
        body, name="swap_reduced", in_specs=[_ANY] * n, out_specs=[_ANY] * n,
        out_shape=[jax.ShapeDtypeStruct(f.shape, F32) for f in fs],
        scratch_shapes=[pltpu.SemaphoreType.DMA((n,)), pltpu.SemaphoreType.DMA((n,))],
    )(*fs)


def _row_tile(rows):
    return min(rows, 256)


def _chip_sums(core, grads):
    if not grads:
        return []
    return [_add_pairs(core, a, b) for a, b in zip(grads, _swap_halves(grads))]


def _add_pairs(core, g, other):
    n, half, cols = other.shape
    t = _row_tile(half)
    per_half = half // t

    def body(c_ref, a_ref, b_ref, o_ref):
        o_ref[...] = (a_ref[...] + b_ref[...]).astype(BF16)

    spec = pl.BlockSpec((None, t, cols), lambda i, j, c_ref: (i, j, 0))
    return pl.pallas_call(
        body, name="add_pairs",
        grid_spec=pltpu.PrefetchScalarGridSpec(
            num_scalar_prefetch=1, grid=(n, per_half),
            in_specs=[pl.BlockSpec((None, t, cols), lambda i, j, c_ref: (i, c_ref[0] * per_half + j, 0)), spec],
            out_specs=spec),
        out_shape=jax.ShapeDtypeStruct(other.shape, BF16), compiler_params=_params(("parallel", "parallel")),
    )(core, g, other)


def _add_chips(me, s1, r2):
    _, r, cols = r2.shape
    t = _row_tile(r)

    def body(me_ref, s_ref, r_ref, o_ref):
        own = s_ref[...].astype(F32)
        parts = [jnp.where(me_ref[0] == k, own, r_ref[k].astype(F32)) for k in range(N_BLK)]
        o_ref[...] = ((parts[0] + parts[1]) + parts[2]) + parts[3]

    return pl.pallas_call(
        body, name="add_chips",
        grid_spec=pltpu.PrefetchScalarGridSpec(
            num_scalar_prefetch=1, grid=(r // t,),
            in_specs=[pl.BlockSpec((None, t, cols), lambda i, me_ref: (me_ref[0], i, 0)),
                      pl.BlockSpec((N_BLK, t, cols), lambda i, me_ref: (0, i, 0))],
            out_specs=pl.BlockSpec((t, cols), lambda i, me_ref: (i, 0))),
        out_shape=jax.ShapeDtypeStruct((r, cols), F32), compiler_params=_params(("parallel",)),
    )(me, s1, r2)


def _adamw(core, mine, other, w, m, v, layer, earlier=None):
    half, cols = mine.shape
    t = _row_tile(half)
    per_half = half // t
    c1 = 1.0 - ADAM_B1 ** ADAM_STEP
    c2 = 1.0 - ADAM_B2 ** ADAM_STEP

    def body(c_ref, a_ref, b_ref, w_ref, m_ref, v_ref, *rest):
        g_ref, d_ref, mo_ref, vo_ref = rest[-4:]
        gv = jnp.where(pl.program_id(0) // per_half == c_ref[0], a_ref[...], b_ref[...])
        g_ref[...] = gv
        m_new = ADAM_B1 * m_ref[...] + (1.0 - ADAM_B1) * gv
        v_new = ADAM_B2 * v_ref[...] + (1.0 - ADAM_B2) * (gv * gv)
        mo_ref[...] = m_new
        vo_ref[...] = v_new
        d_ref[...] = -ADAM_LR * ((m_new / c1) / (jnp.sqrt(v_new / c2) + ADAM_EPS) + ADAM_WD * w_ref[...])

    part = pl.BlockSpec((t, cols), lambda i, c_ref: (i % per_half, 0))
    spec = pl.BlockSpec((None, t, cols), lambda i, c_ref: (layer, i, 0))
    kept = [] if earlier is None else list(earlier)
    return pl.pallas_call(
        body, name="adamw",
        grid_spec=pltpu.PrefetchScalarGridSpec(
            num_scalar_prefetch=1, grid=(2 * per_half,),
            in_specs=[part, part, spec, spec, spec] + [_ANY] * len(kept), out_specs=[spec] * 4),
        out_shape=[jax.ShapeDtypeStruct(w.shape, F32)] * 4,
        input_output_aliases={6 + k: k for k in range(len(kept))},
        compiler_params=_params(("parallel",)),
    )(core, mine, other, w, m, v, *kept)


_REPLICATED = ("norm_mix_g", "gmlp_v_g", "gmlp_w_s", "gmlp_b_s", "conf_ln_g", "conf_ln_b", "mix_out_g", "norm_ffn_g",
               "final_norm_g")
_REP_SHAPES = {"norm_mix_g": (DEPTH, D_MODEL), "gmlp_v_g": (DEPTH, D_GROUP), "gmlp_w_s": (DEPTH, N_HEADS, CHUNK, CHUNK),
               "gmlp_b_s": (DEPTH, N_HEADS, CHUNK), "conf_ln_g": (DEPTH, D_GROUP), "conf_ln_b": (DEPTH, D_GROUP),
               "mix_out_g": (DEPTH, D_MODEL), "norm_ffn_g": (DEPTH, D_MODEL), "final_norm_g": (D_MODEL,)}
_BIG = ("w_in", "w_out", "w_up", "w_down")
_CONV_ROWS = 8
_REP_ROWS = 144
_SMALL_ROWS = 160
_CH_BLK = D_GROUP // N_BLK


def _pad_rows(flat, rows):
    pad = rows * D_MODEL - flat.shape[-1]
    flat = jnp.pad(flat, [(0, 0)] * (flat.ndim - 1) + [(0, pad)])
    return flat.reshape(flat.shape[:-1] + (rows, D_MODEL))


def _pack_small(scw, ccw, rep):
    lead = scw.shape[:-3]
    conv = jnp.concatenate([scw.reshape(lead + (-1,)), ccw.reshape(lead + (-1,))], axis=-1)
    flat = jnp.concatenate([rep[k].reshape(-1) for k in _REPLICATED])
    flat = jnp.broadcast_to(flat, lead + flat.shape)
    parts = [_pad_rows(conv, _CONV_ROWS), _pad_rows(flat, _REP_ROWS),
             jnp.zeros(lead + (_SMALL_ROWS - _CONV_ROWS - _REP_ROWS, D_MODEL), F32)]
    return jnp.concatenate(parts, axis=-2)


def _unpack_small(pk):
    out = {}
    conv = pk[:_CONV_ROWS].reshape(-1)
    n_s = DEPTH * SHORT_K * _CH_BLK
    out["short_conv_w"] = conv[:n_s].reshape(DEPTH, SHORT_K, _CH_BLK)
    out["conf_conv_w"] = conv[n_s:n_s + DEPTH * CONF_K * _CH_BLK].reshape(DEPTH, CONF_K, _CH_BLK)
    row = _CONV_ROWS
    flat = pk[row:row + _REP_ROWS].reshape(-1)
    at = 0
    for k in _REPLICATED:
        n = math.prod(_REP_SHAPES[k])
        out[k] = flat[at:at + n].reshape(_REP_SHAPES[k])
        at += n
    return out


def _conv_blocks(w):
    d, k, _ = w.shape
    return w.reshape(d, k, N_BLK, _CH_BLK).transpose(2, 0, 1, 3)


_WEIGHTS = ("norm_mix_g", "w_in", "gmlp_v_g", "gmlp_w_s", "gmlp_b_s", "short_conv_w", "conf_conv_w", "conf_ln_g",
            "conf_ln_b", "mix_out_g", "w_out", "norm_ffn_g", "w_up", "w_down", "final_norm_g")


def kernel(x, norm_mix_g, w_in, gmlp_v_g, gmlp_w_s, gmlp_b_s, short_conv_w, conf_conv_w, conf_ln_g, conf_ln_b, mix_out_g, w_out, norm_ffn_g, w_up, w_down, final_norm_g, loss_target, m_norm_mix_g, m_w_in, m_gmlp_v_g, m_gmlp_w_s, m_gmlp_b_s, m_short_conv_w, m_conf_conv_w, m_conf_ln_g, m_conf_ln_b, m_mix_out_g, m_w_out, m_norm_ffn_g, m_w_up, m_w_down, m_final_norm_g, v_norm_mix_g, v_w_in, v_gmlp_v_g, v_gmlp_w_s, v_gmlp_b_s, v_short_conv_w, v_conf_conv_w, v_conf_ln_g, v_conf_ln_b, v_mix_out_g, v_w_out, v_norm_ffn_g, v_w_up, v_w_down, v_final_norm_g):
    w = dict(norm_mix_g=norm_mix_g, w_in=w_in, gmlp_v_g=gmlp_v_g, gmlp_w_s=gmlp_w_s, gmlp_b_s=gmlp_b_s,
             short_conv_w=short_conv_w, conf_conv_w=conf_conv_w, conf_ln_g=conf_ln_g, conf_ln_b=conf_ln_b,
             mix_out_g=mix_out_g, w_out=w_out, norm_ffn_g=norm_ffn_g, w_up=w_up, w_down=w_down, final_norm_g=final_norm_g)
    m = dict(norm_mix_g=m_norm_mix_g, w_in=m_w_in, gmlp_v_g=m_gmlp_v_g, gmlp_w_s=m_gmlp_w_s, gmlp_b_s=m_gmlp_b_s,
             short_conv_w=m_short_conv_w, conf_conv_w=m_conf_conv_w, conf_ln_g=m_conf_ln_g, conf_ln_b=m_conf_ln_b,
             mix_out_g=m_mix_out_g, w_out=m_w_out, norm_ffn_g=m_norm_ffn_g, w_up=m_w_up, w_down=m_w_down,
             final_norm_g=m_final_norm_g)
    v = dict(norm_mix_g=v_norm_mix_g, w_in=v_w_in, gmlp_v_g=v_gmlp_v_g, gmlp_w_s=v_gmlp_w_s, gmlp_b_s=v_gmlp_b_s,
             short_conv_w=v_short_conv_w, conf_conv_w=v_conf_conv_w, conf_ln_g=v_conf_ln_g, conf_ln_b=v_conf_ln_b,
             mix_out_g=v_mix_out_g, w_out=v_w_out, norm_ffn_g=v_norm_ffn_g, w_up=v_w_up, w_down=v_w_down,
             final_norm_g=v_final_norm_g)
    core = lax.axis_index("c").astype(jnp.int32).reshape(1)
    me = (2 * lax.axis_index("x") + lax.axis_index("y")).astype(jnp.int32).reshape(1)

    conv_mine = _pad_rows(jnp.concatenate([short_conv_w.reshape(-1), conf_conv_w.reshape(-1)]), _CONV_ROWS)
    big = {k: _cast_into_slot(w[k], me, "cast_" + k) for k in _BIG}
    big["w_in"], conv_all = _gather_first([big["w_in"]], [(0, 0)], [conv_mine])
    conv_all = conv_all.reshape(N_BLK, -1)
    n_s = DEPTH * SHORT_K * _CH_BLK
    scw_all = conv_all[:, :n_s].reshape(N_BLK, DEPTH, SHORT_K, _CH_BLK)
    ccw_all = conv_all[:, n_s:n_s + DEPTH * CONF_K * _CH_BLK].reshape(N_BLK, DEPTH, CONF_K, _CH_BLK)
    small = {k: w[k] for k in _REPLICATED}
    small["short_conv_w"] = scw_all.transpose(1, 2, 0, 3).reshape(DEPTH, SHORT_K, D_GROUP)
    small["conf_conv_w"] = ccw_all.transpose(1, 2, 0, 3).reshape(DEPTH, CONF_K, D_GROUP)

    loss, dx, g, early = _local_step(x[0], loss_target[0], big, small, gather_pending=True, core=core)

    where = [(k, l) for k in _BIG for l in range(DEPTH)]
    late = [kl for kl in where if kl not in early]
    sums = _chip_sums(core, [g[k][l] for k, l in late]
                      + [_pack_small(_conv_blocks(g["short_conv_w"]), _conv_blocks(g["conf_conv_w"]), g)])
    sent = {**early, **dict(zip(late + ["small"], zip(sums, _send_to_owners(sums))))}
    mine = [_add_chips(me, *sent[kl]) for kl in where + ["small"]]
    other = _swap_reduced(mine)

    done = {}
    for n, (k, l) in enumerate(where):
        done[k] = _adamw(core, mine[n], other[n], w[k], m[k], v[k], l, done.get(k))
    small_own = [_pack_small(t["short_conv_w"], t["conf_conv_w"], t)[None] for t in (w, m, v)]
    small_done = [_unpack_small(a[0]) for a in _adamw(core, mine[-1], other[-1], *small_own, 0)]

    outs = [lax.psum(loss[0, 0], ("x", "y", "c")), dx[None]]
    for kind in range(4):
        outs += [done[k][kind] if k in _BIG else small_done[kind][k] for k in _WEIGHTS]
    return tuple(outs)
```

```python
import math

import jax
import jax.numpy as jnp
from jax import lax
from jax.experimental import pallas as pl
from jax.experimental.pallas import tpu as pltpu

F32 = jnp.float32
BF16 = jnp.bfloat16

D_MODEL = 1024
D_GROUP = 256
N_HEADS = 4
HEAD_DIM = 64
CHUNK = 128
D_IN = 2560
N_BLK = 4
W_IN_BLK = D_IN // N_BLK
D_FF_BLK = 1024
DEPTH = 2
EPS = 1e-6
HALO = 32
SHORT_K = 3
CONF_K = 31
ATT_TQ = 256
ATT_TK = 256
ATT_SCALE = 0.125
ATT_DEAD = -104.0
V7X_VMEM_LIMIT = 56 * 1024 * 1024

ADAM_LR, ADAM_B1, ADAM_B2, ADAM_EPS, ADAM_WD, ADAM_STEP = 0.001, 0.9, 0.999, 1e-08, 0.01, 10

MESH = pl.DeviceIdType.MESH


def _params(sem, vmem=None):
    return pltpu.CompilerParams(dimension_semantics=sem, vmem_limit_bytes=vmem)


def _tile(s, t):
    return min(s, t)


def _rsqrt_mean(v):
    return lax.rsqrt(jnp.mean(v * v, axis=-1, keepdims=True) + EPS)


def _sigmoid(v):
    return 1.0 / (1.0 + jnp.exp(-v))


_GELU_C = math.sqrt(2.0 / math.pi)


def _gelu_tanh(v):
    return jnp.tanh(_GELU_C * (v + 0.044715 * (v * v * v)))


def _gelu(v, t):
    return v * (0.5 * (1.0 + t))


def _gelu_grad(v, t):
    return 0.5 * (1.0 + t) + v * (0.5 * (1.0 - t * t) * _GELU_C * (1.0 + 3.0 * 0.044715 * (v * v)))


def _dot(a, b):
    return jnp.dot(a, b, preferred_element_type=F32)


def _dot_nt(a, b):
    return lax.dot_general(a, b, (((1,), (1,)), ((), ())), preferred_element_type=F32)


def _dot_tn(a, b):
    return lax.dot_general(a, b, (((0,), (0,)), ((), ())), preferred_element_type=F32)


def _cast_into_slot(w, me, name):
    n, r, c = w.shape
    tr = _tile(r, 256)

    def body(me_ref, w_ref, o_ref):
        o_ref[...] = w_ref[...].astype(BF16)

    return pl.pallas_call(
        body, name=name,
        grid_spec=pltpu.PrefetchScalarGridSpec(
            num_scalar_prefetch=1, grid=(n, r // tr),
            in_specs=[pl.BlockSpec((None, tr, c), lambda a, b, me_ref: (a, b, 0))],
            out_specs=pl.BlockSpec((None, None, tr, c), lambda a, b, me_ref: (me_ref[0], a, b, 0))),
        out_shape=jax.ShapeDtypeStruct((N_BLK,) + w.shape, BF16),
        compiler_params=_params(("parallel", "parallel")),
    )(me, w)


def _split_heads(xv, rows_ref, cols_ref):
    if rows_ref is not None:
        for h in range(N_HEADS):
            rows_ref[h] = xv[:, h * HEAD_DIM:(h + 1) * HEAD_DIM].astype(BF16)
    if cols_ref is not None:
        xt = xv.T
        for h in range(N_HEADS):
            cols_ref[h] = xt[h * HEAD_DIM:(h + 1) * HEAD_DIM, :].astype(BF16)


def _head_specs(t, s):
    rows = (pl.BlockSpec((N_HEADS, t, HEAD_DIM), lambda i: (0, i, 0)), jax.ShapeDtypeStruct((N_HEADS, s, HEAD_DIM), BF16))
    cols = (pl.BlockSpec((N_HEADS, HEAD_DIM, t), lambda i: (0, 0, i)), jax.ShapeDtypeStruct((N_HEADS, HEAD_DIM, s), BF16))
    return rows, cols


def _cols(ref, lo, hi):
    return ref[:, lo:hi].astype(F32)


def _pair_blocks(w_ref, wide_ref):
    @pl.when(pl.program_id(0) == 0)
    def _():
        for b in range(N_BLK):
            wide_ref[b // 2, :, (b % 2) * W_IN_BLK:(b % 2 + 1) * W_IN_BLK] = w_ref[b]


def _in_proj(x, g, w, layer):
    s = x.shape[0]
    t = _tile(s, 512)

    def body(x_ref, g_ref, w_ref, z_ref, h_ref, qr_ref, qt_ref, kt_ref, vt_ref, wide_ref):
        _pair_blocks(w_ref, wide_ref)
        xv = x_ref[...]
        h = (xv * _rsqrt_mean(xv) * g_ref[...]).astype(BF16)
        h_ref[...] = h
        for n in range(N_BLK // 2):
            z_ref[:, 2 * n * W_IN_BLK:2 * (n + 1) * W_IN_BLK] = _dot(h, wide_ref[n]).astype(BF16)
        _split_heads(_cols(z_ref, 1280, 1536) * ATT_SCALE, qr_ref, qt_ref)
        _split_heads(_cols(z_ref, 1536, 1792), None, kt_ref)
        _split_heads(_cols(z_ref, 1792, 2048), None, vt_ref)

    rows, cols = _head_specs(t, s)
    return pl.pallas_call(
        body, name="in_proj", grid=(s // t,),
        in_specs=[pl.BlockSpec((t, D_MODEL), lambda i: (i, 0)), _full((1, D_MODEL)),
                  pl.BlockSpec((N_BLK, None, D_MODEL, W_IN_BLK), lambda i: (0, layer, 0, 0))],
        out_specs=[pl.BlockSpec((t, D_IN), lambda i: (i, 0)), pl.BlockSpec((t, D_MODEL), lambda i: (i, 0)),
                   rows[0], cols[0], cols[0], cols[0]],
        out_shape=[jax.ShapeDtypeStruct((s, D_IN), BF16), jax.ShapeDtypeStruct((s, D_MODEL), BF16),
                   rows[1], cols[1], cols[1], cols[1]],
        scratch_shapes=[pltpu.VMEM((N_BLK // 2, D_MODEL, 2 * W_IN_BLK), BF16)],
        compiler_params=_params(("arbitrary",), V7X_VMEM_LIMIT),
    )(x, g, w)


def _mix_a_fwd(z_ref, vg, wt_ref, bmat, t):
    zu = _cols(z_ref, 0, 256)
    zv = _cols(z_ref, 256, 512)
    tu = _gelu_tanh(zu)
    tv = _gelu_tanh(zv)
    u = _gelu(zu, tu)
    v = _gelu(zv, tv)
    rv = _rsqrt_mean(v)
    vh = v * rv
    vnb = (vh * vg).astype(BF16)
    head = lax.broadcasted_iota(jnp.int32, (CHUNK, D_GROUP), 1) // HEAD_DIM
    fs = []
    for c in range(t // CHUNK):
        vc = vnb[c * CHUNK:(c + 1) * CHUNK, :]
        fc = bmat
        for h in range(N_HEADS):
            fc = fc + jnp.where(head == h, _dot(wt_ref[h], vc), 0.0)
        fs.append(fc)
    f = jnp.concatenate(fs, axis=0) if len(fs) > 1 else fs[0]
    return (zu, tu), (zv, tv), u, rv, vh, vnb, f


def _windows(ext_ref, sh_ref, t):
    for b in range(1, 8):
        sh_ref[b - 1] = ext_ref[pl.ds(b, HALO + t - 8), :]

    def window(o):
        a, b = divmod(o, 8)
        return ext_ref[pl.ds(8 * a, t), :] if b == 0 else sh_ref[b - 1, pl.ds(8 * a, t), :]

    return window


def _mix_b_fwd(z_ref, zh_ref, first, scw_ref, ext_ref, t):
    gb = _cols(z_ref, 512, 768)
    uh = _cols(zh_ref, 768, 1024) * _cols(zh_ref, 1024, 1280)
    ext_ref[0:HALO, :] = jnp.where(first, 0.0, uh)
    ext_ref[HALO:HALO + t, :] = _cols(z_ref, 768, 1024) * _cols(z_ref, 1024, 1280)
    cv = jnp.zeros((t, D_GROUP), F32)
    for k in range(SHORT_K):
        cv = cv + scw_ref[k:k + 1, :] * ext_ref[pl.ds(HALO - (SHORT_K - 1) + k, t), :]
    return gb, cv


def _mix_d_fwd(z_ref, zh_ref, first, ccw_ref, lg, lb, ext_ref, sh_ref, t, cv=None):
    hh = _cols(zh_ref, 2048, 2304) * _sigmoid(_cols(zh_ref, 2304, 2560))
    ext_ref[0:HALO, :] = jnp.where(first, 0.0, hh)
    ext_ref[HALO:HALO + t, :] = _cols(z_ref, 2048, 2304) * _sigmoid(_cols(z_ref, 2304, 2560))
    window = _windows(ext_ref, sh_ref, t)
    if cv is None:
        cv = jnp.zeros((t, D_GROUP), F32)
        for k in range(CONF_K):
            cv = cv + ccw_ref[k:k + 1, :] * window(HALO - (CONF_K - 1) + k)
    xc = cv - jnp.mean(cv, axis=-1, keepdims=True)
    rs = lax.rsqrt(jnp.mean(xc * xc, axis=-1, keepdims=True) + EPS)
    xh = xc * rs
    ln = xh * lg + lb
    return xh, rs, ln, _sigmoid(ln), window, cv


def _mix_specs(t, s):
    per = t // HALO
    return [pl.BlockSpec((t, D_IN), lambda i: (i, 0)),
            pl.BlockSpec((HALO, D_IN), lambda i: (jnp.maximum(i * per - 1, 0), 0))]


def _full(shape):
    return pl.BlockSpec(shape, lambda i: (0,) * len(shape))


def _mixers_fwd(z, p, bufs=(), parts=()):
    s = z.shape[0]
    t = _tile(s, 256)
    nb = len(bufs)

    def body(z_ref, zh_ref, vg_ref, wt_ref, bm_ref, scw_ref, ccw_ref, lg_ref, lb_ref, *rest):
        y_ref = rest[nb]
        eb_ref, ed_ref, sh_ref = rest[2 * nb + 1:2 * nb + 4]
        if nb:
            _gather_in_steps(rest[nb + 1:2 * nb + 1], parts, rest[2 * nb + 4:], s // t)
        first = pl.program_id(0) == 0
        _, _, u, _, _, _, f = _mix_a_fwd(z_ref, vg_ref[...], wt_ref, bm_ref[...], t)
        ya = u * f
        y_ref[:, 0:256] = ya * _rsqrt_mean(ya)
        gb, cv = _mix_b_fwd(z_ref, zh_ref, first, scw_ref, eb_ref, t)
        yb = gb * cv
        y_ref[:, 256:512] = yb * _rsqrt_mean(yb)
        _, _, ln, sg, _, cvd = _mix_d_fwd(z_ref, zh_ref, first, ccw_ref, lg_ref[...], lb_ref[...], ed_ref, sh_ref, t)
        yd = ln * sg
        y_ref[:, 512:768] = yd * _rsqrt_mean(yd)
        y_ref[:, 768:1024] = cvd

    out = pl.pallas_call(
        body, name="mixers_fwd", grid=(s // t,),
        in_specs=_mix_specs(t, s) + [_full((1, D_GROUP)), _full((N_HEADS, CHUNK, CHUNK)), _full((CHUNK, D_GROUP)),
                                     _full((8, D_GROUP)), _full((32, D_GROUP)), _full((1, D_GROUP)), _full((1, D_GROUP))]
                 + [_ANY] * nb,
        out_specs=[pl.BlockSpec((t, D_MODEL), lambda i: (i, 0))] + [_ANY] * nb,
        out_shape=[jax.ShapeDtypeStruct((s, D_MODEL), F32)] + [jax.ShapeDtypeStruct(b.shape, b.dtype) for b in bufs],
        input_output_aliases={9 + k: 1 + k for k in range(nb)},
        scratch_shapes=[pltpu.VMEM((HALO + t, D_GROUP), F32), pltpu.VMEM((HALO + t, D_GROUP), F32),
                        pltpu.VMEM((7, HALO + t - 8, D_GROUP), F32)] + (_gather_sems(parts) if nb else []),
        compiler_params=_params(("arbitrary",) if nb else ("parallel",)),
    )(z, z, p["vg"], p["wt"], p["bmat"], p["scw"], p["ccw"], p["lg"], p["lb"], *bufs)
    return (out[0], out[1:]) if nb else out[0]


def _mixers_bwd_a(z, dyn, o_t, y_abd, p, sums=()):
    s = z.shape[0]
    t = _tile(s, 256)
    n_chunk = t // CHUNK
    ns = len(sums)

    def body(*refs):
        (z_ref, zh_ref, dyn_ref, ot_ref, cv_ref, vg_ref, wt_ref, wtt_ref, bm_ref, scw_ref, ccw_ref, lg_ref,
         lb_ref) = refs[:13]
        (dza_ref, dcb_ref, dcd_ref, dor_ref, dot_ref, ds_ref, dvg_ref, dws_ref, dbm_ref, dscw_ref, dccw_ref, dlg_ref,
         dlb_ref) = refs[13 + ns:26 + ns]
        eb_ref, ed_ref, sh_ref = refs[26 + 2 * ns:29 + 2 * ns]
        i = pl.program_id(0)
        first = i == 0
        if ns:
            _owners_in_steps(refs[13:13 + ns], refs[26 + ns:26 + 2 * ns], refs[29 + 2 * ns:], first, i == s // t - 1)

        @pl.when(first)
        def _():
            for r in (dvg_ref, dws_ref, dbm_ref, dscw_ref, dccw_ref, dlg_ref, dlb_ref):
                r[...] = jnp.zeros_like(r)

        def rms_bwd(y, dn):
            r = _rsqrt_mean(y)
            yn = y * r
            return r * (dn - yn * jnp.mean(dn * yn, axis=-1, keepdims=True))

        vg = vg_ref[...]
        gelu_u, gelu_v, u, rv, vh, vnb, f = _mix_a_fwd(z_ref, vg, wt_ref, bm_ref[...], t)
        dya = rms_bwd(u * f, _cols(dyn_ref, 0, 256))
        du = dya * f
        df = dya * u
        head = lax.broadcasted_iota(jnp.int32, (CHUNK, D_GROUP), 1) // HEAD_DIM
        dvns = []
        dbm = jnp.zeros((CHUNK, D_GROUP), F32)
        for c in range(n_chunk):
            dfc = df[c * CHUNK:(c + 1) * CHUNK, :]
            vc = vnb[c * CHUNK:(c + 1) * CHUNK, :]
            dbm = dbm + dfc
            dvn = jnp.zeros((CHUNK, D_GROUP), F32)
            for h in range(N_HEADS):
                dfh = jnp.where(head == h, dfc, 0.0).astype(BF16)
                dvn = dvn + _dot(wtt_ref[h], dfh)
                dws_ref[h] += _dot_nt(dfh, vc)
            dvns.append(dvn)
        dbm_ref[...] += dbm
        dvn = jnp.concatenate(dvns, axis=0) if n_chunk > 1 else dvns[0]
        dvg_ref[...] += jnp.sum(dvn * vh, axis=0, keepdims=True)
        dvh = dvn * vg
        dv = rv * (dvh - vh * jnp.mean(dvh * vh, axis=-1, keepdims=True))
        dza_ref[:, 0:256] = (du * _gelu_grad(*gelu_u)).astype(BF16)
        dza_ref[:, 256:512] = (dv * _gelu_grad(*gelu_v)).astype(BF16)

        gb, cv = _mix_b_fwd(z_ref, zh_ref, first, scw_ref, eb_ref, t)
        dyb = rms_bwd(gb * cv, _cols(dyn_ref, 256, 512))
        dza_ref[:, 512:768] = (dyb * cv).astype(BF16)
        dcb = dyb * gb
        dcb_ref[...] = dcb
        for k in range(SHORT_K):
            dscw_ref[k:k + 1, :] += jnp.sum(dcb * eb_ref[pl.ds(HALO - (SHORT_K - 1) + k, t), :], axis=0, keepdims=True)

        lg = lg_ref[...]
        xh, rs, ln, sg, window, _ = _mix_d_fwd(z_ref, zh_ref, first, ccw_ref, lg, lb_ref[...], ed_ref, sh_ref, t,
                                               cv_ref[...])
        dyd = rms_bwd(ln * sg, _cols(dyn_ref, 768, 1024))
        dln = dyd * (sg * (1.0 + ln * (1.0 - sg)))
        dlg_ref[...] += jnp.sum(dln * xh, axis=0, keepdims=True)
        dlb_ref[...] += jnp.sum(dln, axis=0, keepdims=True)
        dxh = dln * lg
        dcd = rs * (dxh - jnp.mean(dxh, axis=-1, keepdims=True) - xh * jnp.mean(dxh * xh, axis=-1, keepdims=True))
        dcd_ref[...] = dcd
        for k in range(CONF_K):
            dccw_ref[k:k + 1, :] += jnp.sum(dcd * window(HALO - (CONF_K - 1) + k), axis=0, keepdims=True)

        o = ot_ref[...].reshape(D_GROUP, t).T
        do = rms_bwd(o, _cols(dyn_ref, 512, 768))
        _split_heads(do, dor_ref, dot_ref)
        prod = do.astype(BF16).astype(F32) * o
        for h in range(N_HEADS):
            ds_ref[h] = jnp.sum(prod[:, h * HEAD_DIM:(h + 1) * HEAD_DIM], axis=1, keepdims=True)

    small = [(1, D_GROUP), (N_HEADS, CHUNK, CHUNK), (CHUNK, D_GROUP), (8, D_GROUP), (32, D_GROUP), (1, D_GROUP), (1, D_GROUP)]
    rows, cols = _head_specs(t, s)
    out = pl.pallas_call(
        body, name="mixers_bwd_a", grid=(s // t,),
        in_specs=_mix_specs(t, s) + [pl.BlockSpec((t, D_MODEL), lambda i: (i, 0)),
                                     pl.BlockSpec((N_HEADS, HEAD_DIM, t), lambda i: (0, 0, i)),
                                     pl.BlockSpec((t, D_GROUP), lambda i: (i, 3)),
                                     _full((1, D_GROUP)), _full((N_HEADS, CHUNK, CHUNK)), _full((N_HEADS, CHUNK, CHUNK)),
                                     _full((CHUNK, D_GROUP)), _full((8, D_GROUP)), _full((32, D_GROUP)),
                                     _full((1, D_GROUP)), _full((1, D_GROUP))] + [_ANY] * ns,
        out_specs=[pl.BlockSpec((t, 768), lambda i: (i, 0)), pl.BlockSpec((t, D_GROUP), lambda i: (i, 0)),
                   pl.BlockSpec((t, D_GROUP), lambda i: (i, 0)), rows[0], cols[0],
                   pl.BlockSpec((N_HEADS, t, 1), lambda i: (0, i, 0))]
                  + [_full(sh) for sh in small] + [_ANY] * ns,
        out_shape=[jax.ShapeDtypeStruct((s, 768), BF16), jax.ShapeDtypeStruct((s, D_GROUP), F32),
                   jax.ShapeDtypeStruct((s, D_GROUP), F32), rows[1], cols[1],
                   jax.ShapeDtypeStruct((N_HEADS, s, 1), F32)]
                  + [jax.ShapeDtypeStruct(sh, F32) for sh in small]
                  + [jax.ShapeDtypeStruct(a.shape, a.dtype) for a in sums],
        scratch_shapes=[pltpu.VMEM((HALO + t, D_GROUP), F32), pltpu.VMEM((HALO + t, D_GROUP), F32),
                        pltpu.VMEM((7, HALO + t - 8, D_GROUP), F32)] + (_owner_sems(ns) if ns else []),
        compiler_params=_params(("arbitrary",)),
    )(z, z, dyn, o_t, y_abd, p["vg"], p["wt"], p["wtt"], p["bmat"], p["scw"], p["ccw"], p["lg"], p["lb"], *sums)
    return tuple(out[:13]) + (out[13:],) if ns else out


def _mixers_bwd_b(z, dza, dcb, dcd, dq_t, dk_t, dv_t, p):
    s = z.shape[0]
    t = _tile(s, 256)
    per = t // HALO
    n_halo = s // HALO

    def body(z_ref, dza_ref, dcb_ref, dcbn_ref, dcd_ref, dcdn_ref, dq_ref, dk_ref, dv_ref, scw_ref, ccw_ref,
             dz_ref, eb_ref, ed_ref, sh_ref):
        last = pl.program_id(0) == pl.num_programs(0) - 1
        dz_ref[:, 0:768] = dza_ref[...]
        eb_ref[0:t, :] = dcb_ref[...]
        eb_ref[t:t + HALO, :] = jnp.where(last, 0.0, dcbn_ref[...])
        du = jnp.zeros((t, D_GROUP), F32)
        for k in range(SHORT_K):
            du = du + scw_ref[k:k + 1, :] * eb_ref[pl.ds(SHORT_K - 1 - k, t), :]
        dz_ref[:, 768:1024] = (du * _cols(z_ref, 1024, 1280)).astype(BF16)
        dz_ref[:, 1024:1280] = (du * _cols(z_ref, 768, 1024)).astype(BF16)
        for n, r in enumerate((dq_ref, dk_ref, dv_ref)):
            dz_ref[:, 1280 + 256 * n:1536 + 256 * n] = r[...].reshape(D_GROUP, t).T.astype(BF16)
        ed_ref[0:t, :] = dcd_ref[...]
        ed_ref[t:t + HALO, :] = jnp.where(last, 0.0, dcdn_ref[...])
        window = _windows(ed_ref, sh_ref, t)
        dh = jnp.zeros((t, D_GROUP), F32)
        for k in range(CONF_K):
            dh = dh + ccw_ref[k:k + 1, :] * window(CONF_K - 1 - k)
        a = _cols(z_ref, 2048, 2304)
        sg = _sigmoid(_cols(z_ref, 2304, 2560))
        dz_ref[:, 2048:2304] = (dh * sg).astype(BF16)
        dz_ref[:, 2304:2560] = (dh * a * sg * (1.0 - sg)).astype(BF16)

    nxt = lambda i: (jnp.minimum((i + 1) * per, n_halo - 1), 0)
    tr = pl.BlockSpec((N_HEADS, HEAD_DIM, t), lambda i: (0, 0, i))
    return pl.pallas_call(
        body, name="mixers_bwd_b", grid=(s // t,),
        in_specs=[pl.BlockSpec((t, D_IN), lambda i: (i, 0)), pl.BlockSpec((t, 768), lambda i: (i, 0)),
                  pl.BlockSpec((t, D_GROUP), lambda i: (i, 0)), pl.BlockSpec((HALO, D_GROUP), nxt),
                  pl.BlockSpec((t, D_GROUP), lambda i: (i, 0)), pl.BlockSpec((HALO, D_GROUP), nxt),
                  tr, tr, tr, _full((8, D_GROUP)), _full((32, D_GROUP))],
        out_specs=pl.BlockSpec((t, D_IN), lambda i: (i, 0)),
        out_shape=jax.ShapeDtypeStruct((s, D_IN), BF16),
        scratch_shapes=[pltpu.VMEM((HALO + t, D_GROUP), F32), pltpu.VMEM((HALO + t, D_GROUP), F32),
                        pltpu.VMEM((7, HALO + t - 8, D_GROUP), F32)],
        compiler_params=_params(("parallel",)),
    )(z, dza, dcb, dcb, dcd, dcd, dq_t, dk_t, dv_t, p["scw"], p["ccw"])


def _split_bf16(v):
    hi = v.astype(BF16)
    return hi, (v - hi.astype(F32)).astype(BF16)


def _att_scores(qs, kts, carries, tri, mask):
    zs = [_dot(q, kt) for q, kt in zip(qs, kts)]
    lms, lbs, parts = [], [], []
    for z in zs:
        soft = jnp.log(1.0 + jnp.exp(-jnp.abs(z)))
        lm = -(jnp.maximum(z, 0.0) + soft)
        lbs.append(lm + z)
        if mask is not None:
            lm = jnp.where(mask, lm, 0.0)
        lms.append(lm)
        parts.append(_split_bf16(lm))
    rights = [_dot(hi, tri) + _dot(lo, tri) for hi, lo in parts]
    ws = []
    for lb, right, carry in zip(lbs, rights, carries):
        w = jnp.exp(lb + right + carry)
        ws.append(w if mask is None else jnp.where(mask, w, 0.0))
    return ws, lbs, [jnp.sum(lm, axis=1, keepdims=True) for lm in lms]


def _att_consts(i):
    j_hi = ((i + 1) * ATT_TQ - 1) // ATT_TK
    row = lax.broadcasted_iota(jnp.int32, (ATT_TQ, ATT_TK), 0) + i * ATT_TQ
    col = lax.broadcasted_iota(jnp.int32, (ATT_TQ, ATT_TK), 1) + j_hi * ATT_TK
    r_i = lax.broadcasted_iota(jnp.int32, (ATT_TK, ATT_TK), 0)
    c_i = lax.broadcasted_iota(jnp.int32, (ATT_TK, ATT_TK), 1)
    return j_hi, col < row, r_i, c_i


def _att_alive(j, carries):
    top = carries[0]
    for c in carries[1:]:
        top = jnp.maximum(top, c)
    return jnp.logical_and(j >= 0, jnp.max(top) > ATT_DEAD)


def _attn_fwd(q_r, k_t, v_t, bufs=(), parts=()):
    s = q_r.shape[1]
    nb = len(bufs)

    def body(q_ref, kt_ref, vt_ref, *rest):
        o_ref = rest[nb]
        if nb:
            _gather_in_steps(rest[nb + 1:2 * nb + 1], parts, rest[2 * nb + 1:], s // ATT_TQ)
        j_hi, mask, r_i, c_i = _att_consts(pl.program_id(0))
        tri = (r_i > c_i).astype(BF16)

        heads = range(N_HEADS)

        def tiles(j, carries, accs, mask):
            cols = pl.ds(pl.multiple_of(j * ATT_TK, ATT_TK), ATT_TK)
            ws, _, tots = _att_scores([q_ref[h] for h in heads], [kt_ref[h, :, cols] for h in heads], carries, tri, mask)
            accs = [acc + _dot_nt(vt_ref[h, :, cols], w.astype(BF16)) for h, acc, w in zip(heads, accs, ws)]
            return [c + t for c, t in zip(carries, tots)], accs

        state = tiles(j_hi, [jnp.zeros((ATT_TQ, 1), F32)] * N_HEADS, [jnp.zeros((HEAD_DIM, ATT_TQ), F32)] * N_HEADS, mask)

        def cond(c):
            return _att_alive(c[0], c[1])

        def step(c):
            return (c[0] - 1,) + tuple(tiles(c[0], c[1], c[2], None))

        _, _, accs = lax.while_loop(cond, step, (j_hi - 1,) + tuple(state))
        for h in heads:
            o_ref[h] = accs[h]

    whole = pl.BlockSpec((N_HEADS, HEAD_DIM, s), lambda i: (0, 0, 0), pipeline_mode=pl.Buffered(1))
    out = pl.pallas_call(
        body, name="attn_fwd", grid=(s // ATT_TQ,),
        in_specs=[pl.BlockSpec((N_HEADS, ATT_TQ, HEAD_DIM), lambda i: (0, i, 0)), whole, whole] + [_ANY] * nb,
        out_specs=[pl.BlockSpec((N_HEADS, HEAD_DIM, ATT_TQ), lambda i: (0, 0, i))] + [_ANY] * nb,
        out_shape=[jax.ShapeDtypeStruct((N_HEADS, HEAD_DIM, s), F32)] + [jax.ShapeDtypeStruct(b.shape, b.dtype) for b in bufs],
        input_output_aliases={3 + k: 1 + k for k in range(nb)},
        scratch_shapes=_gather_sems(parts) if nb else [],
        compiler_params=_params(("arbitrary",), V7X_VMEM_LIMIT),
    )(q_r, k_t, v_t, *bufs)
    return (out[0], out[1:]) if nb else out[0]


ATT_BWD_HEADS = 2


def _attn_bwd(q_r, q_t, k_t, v_t, do_r, do_t, dsum):
    s = q_r.shape[1]
    hps = ATT_BWD_HEADS

    def body(q_ref, qt_ref, kt_ref, vt_ref, do_ref, dot_ref, ds_ref, dq_ref, dk_ref, dv_ref):
        i = pl.program_id(1)

        @pl.when(i == 0)
        def _():
            dk_ref[...] = jnp.zeros_like(dk_ref)
            dv_ref[...] = jnp.zeros_like(dv_ref)

        j_hi, mask, r_i, c_i = _att_consts(i)
        tri_r = (r_i > c_i).astype(BF16)
        tri_ge = (r_i >= c_i).astype(BF16)

        heads = range(hps)

        def tiles(j, carries, gsums, accs, mask):
            cols = pl.ds(pl.multiple_of(j * ATT_TK, ATT_TK), ATT_TK)
            kts = [kt_ref[h, :, cols] for h in heads]
            das = [_dot(do_ref[h], vt_ref[h, :, cols]) for h in heads]
            ws, lbs, tots = _att_scores([q_ref[h] for h in heads], kts, carries, tri_r, mask)
            wbs = [w.astype(BF16) for w in ws]
            gs = [wb.astype(F32) * da for wb, da in zip(wbs, das)]
            parts = [_split_bf16(g) for g in gs]
            sfx = [_dot(hi, tri_ge) + _dot(lo, tri_ge) for hi, lo in parts]
            for h in heads:
                dv_ref[h, :, cols] += _dot(dot_ref[h], wbs[h])
            dzs = []
            for h in heads:
                left = ds_ref[h] - gsums[h] - sfx[h]
                dz = gs[h] - jnp.exp(lbs[h]) * (gs[h] + left)
                dzs.append((dz if mask is None else jnp.where(mask, dz, 0.0)).astype(BF16))
            for h in heads:
                dk_ref[h, :, cols] += _dot(qt_ref[h], dzs[h])
            accs = [accs[h] + _dot_nt(kts[h], dzs[h]) for h in heads]
            gsums = [gsums[h] + jnp.sum(gs[h], axis=1, keepdims=True) for h in heads]
            return [c + t for c, t in zip(carries, tots)], gsums, accs

        def cond(c):
            return _att_alive(c[0], c[1])

        def step(c):
            return (c[0] - 1,) + tuple(tiles(c[0], c[1], c[2], c[3], jnp.logical_or(mask, c[0] < j_hi)))

        col0 = [jnp.zeros((ATT_TQ, 1), F32)] * hps
        _, _, _, accs = lax.while_loop(cond, step, (j_hi, col0, col0, [jnp.zeros((HEAD_DIM, ATT_TQ), F32)] * hps))
        for h in heads:
            dq_ref[h] = accs[h] * ATT_SCALE

    whole = pl.BlockSpec((hps, HEAD_DIM, s), lambda g, i: (g, 0, 0))
    whole_in = pl.BlockSpec((hps, HEAD_DIM, s), lambda g, i: (g, 0, 0), pipeline_mode=pl.Buffered(1))
    rows = pl.BlockSpec((hps, ATT_TQ, HEAD_DIM), lambda g, i: (g, i, 0))
    cols = pl.BlockSpec((hps, HEAD_DIM, ATT_TQ), lambda g, i: (g, 0, i))
    shape = jax.ShapeDtypeStruct((N_HEADS, HEAD_DIM, s), F32)
    return pl.pallas_call(
        body, name="attn_bwd", grid=(N_HEADS // hps, s // ATT_TQ),
        in_specs=[rows, cols, whole_in, whole_in, rows, cols, pl.BlockSpec((hps, ATT_TQ, 1), lambda g, i: (g, i, 0))],
        out_specs=[cols, whole, whole],
        out_shape=[shape, shape, shape],
        compiler_params=_params(("parallel", "arbitrary"), V7X_VMEM_LIMIT),
    )(q_r, q_t, k_t, v_t, do_r, do_t, dsum)


def _out_proj(x, y_abd, o_t, gain, w, layer):
    s = x.shape[0]
    t = _tile(s, 512)

    def body(x_ref, y_ref, ot_ref, g_ref, w_ref, x1_ref, yn_ref):
        o = ot_ref[...].reshape(D_GROUP, t).T
        groups = [y_ref[:, 0:256], y_ref[:, 256:512], o * _rsqrt_mean(o), y_ref[:, 512:768]]
        g = g_ref[...]
        acc = None
        for b, yn in enumerate(groups):
            cols = slice(256 * b, 256 * (b + 1))
            yn_ref[:, cols] = yn.astype(BF16)
            part = _dot((yn * g[:, cols]).astype(BF16), w_ref[b])
            acc = part if acc is None else acc + part
        x1_ref[...] = x_ref[...] + acc

    return pl.pallas_call(
        body, name="out_proj", grid=(s // t,),
        in_specs=[pl.BlockSpec((t, D_MODEL), lambda i: (i, 0)), pl.BlockSpec((t, 768), lambda i: (i, 0)),
                  pl.BlockSpec((N_HEADS, HEAD_DIM, t), lambda i: (0, 0, i)), _full((1, D_MODEL)),
                  pl.BlockSpec((N_BLK, None, D_GROUP, D_MODEL), lambda i: (0, layer, 0, 0))],
        out_specs=[pl.BlockSpec((t, D_MODEL), lambda i: (i, 0)), pl.BlockSpec((t, D_MODEL), lambda i: (i, 0))],
        out_shape=[jax.ShapeDtypeStruct((s, D_MODEL), F32), jax.ShapeDtypeStruct((s, D_MODEL), BF16)],
        compiler_params=_params(("parallel",)),
    )(x, y_abd, o_t, gain, w)


def _out_proj_bwd(x1, g_ffn, dh, dx2, yn, gain, w, layer):
    s = dx2.shape[0]
    t = _tile(s, 512)

    def body(x_ref, gf_ref, dh_ref, dx2_ref, yn_ref, g_ref, w_ref, dx1_ref, dgf_ref, dyn_ref, dg_ref, dw_ref):
        @pl.when(pl.program_id(0) == 0)
        def _():
            dg_ref[...] = jnp.zeros_like(dg_ref)
            dw_ref[...] = jnp.zeros_like(dw_ref)
            dgf_ref[...] = jnp.zeros_like(dgf_ref)

        dx1, dgf = _rms_bwd_rows(x_ref[...], gf_ref[...], dh_ref[...], dx2_ref[...])
        dx1_ref[...] = dx1
        dgf_ref[...] += dgf
        dxb = dx1.astype(BF16)
        g = g_ref[...]
        yn = yn_ref[...].astype(F32)
        yg = (yn * g).astype(BF16)
        for b in range(N_BLK):
            cols = slice(256 * b, 256 * (b + 1))
            dyg = _dot_nt(dxb, w_ref[b])
            dw_ref[b] += _dot_tn(yg[:, cols], dxb)
            dg_ref[:, cols] += jnp.sum(dyg * yn[:, cols], axis=0, keepdims=True)
            dyn_ref[:, cols] = (dyg * g[:, cols]).astype(BF16)

    row = pl.BlockSpec((t, D_MODEL), lambda i: (i, 0))
    vec = _full((1, D_MODEL))
    return pl.pallas_call(
        body, name="out_proj_bwd", grid=(s // t,),
        in_specs=[row, vec, row, row, row, vec, pl.BlockSpec((N_BLK, None, D_GROUP, D_MODEL), lambda i: (0, layer, 0, 0))],
        out_specs=[row, vec, row, vec, _full((N_BLK, D_GROUP, D_MODEL))],
        out_shape=[jax.ShapeDtypeStruct((s, D_MODEL), F32), jax.ShapeDtypeStruct((1, D_MODEL), F32),
                   jax.ShapeDtypeStruct((s, D_MODEL), BF16), jax.ShapeDtypeStruct((1, D_MODEL), F32),
                   jax.ShapeDtypeStruct((N_BLK, D_GROUP, D_MODEL), F32)],
        compiler_params=_params(("arbitrary",), V7X_VMEM_LIMIT),
    )(x1, g_ffn, dh, dx2, yn, gain, w)


def _ffn(x, g, w_up, w_down, layer):
    s = x.shape[0]
    t = _tile(s, 1024)

    def body(x_ref, g_ref, wu_ref, wd_ref, x2_ref, p_ref, h_ref):
        @pl.when(pl.program_id(1) == 0)
        def _():
            xv = x_ref[...]
            h_ref[...] = (xv * _rsqrt_mean(xv) * g_ref[...]).astype(BF16)
            x2_ref[...] = xv

        pre = _dot(h_ref[...], wu_ref[...])
        p_ref[...] = pre.astype(BF16)
        a = jnp.maximum(pre, 0.0)
        x2_ref[...] += _dot((a * a).astype(BF16), wd_ref[...])

    wspec = pl.BlockSpec((None, None, D_MODEL, D_FF_BLK), lambda i, j: (j, layer, 0, 0))
    row = pl.BlockSpec((t, D_MODEL), lambda i, j: (i, 0))
    return pl.pallas_call(
        body, name="ffn", grid=(s // t, N_BLK),
        in_specs=[row, pl.BlockSpec((1, D_MODEL), lambda i, j: (0, 0)), wspec, wspec],
        out_specs=[row, pl.BlockSpec((t, D_FF_BLK), lambda i, j: (i, j)), row],
        out_shape=[jax.ShapeDtypeStruct((s, D_MODEL), F32), jax.ShapeDtypeStruct((s, N_BLK * D_FF_BLK), BF16),
                   jax.ShapeDtypeStruct((s, D_MODEL), BF16)],
        compiler_params=_params(("parallel", "arbitrary"), V7X_VMEM_LIMIT),
    )(x, g, w_up, w_down)


def _loss_head(x, g, target):
    s = x.shape[0]
    t = _tile(s, 512)

    def body(x_ref, g_ref, t_ref, l_ref, dx_ref, dxb_ref, dg_ref):
        @pl.when(pl.program_id(0) == 0)
        def _():
            l_ref[...] = jnp.zeros_like(l_ref)
            dg_ref[...] = jnp.zeros_like(dg_ref)

        xv = x_ref[...]
        g = g_ref[...]
        r = _rsqrt_mean(xv)
        xh = xv * r
        err = xh * g - t_ref[...]
        l_ref[...] += 0.5 * jnp.sum(jnp.mean(err * err, axis=-1, keepdims=True), axis=0, keepdims=True)
        dy = err * (1.0 / D_MODEL)
        dg_ref[...] += jnp.sum(dy * xh, axis=0, keepdims=True)
        dxh = dy * g
        dx = r * (dxh - xh * jnp.mean(dxh * xh, axis=-1, keepdims=True))
        dx_ref[...] = dx
        dxb_ref[...] = dx.astype(BF16)

    row = pl.BlockSpec((t, D_MODEL), lambda i: (i, 0))
    return pl.pallas_call(
        body, name="loss_head", grid=(s // t,),
        in_specs=[row, _full((1, D_MODEL)), row],
        out_specs=[_full((1, 128)), row, row, _full((1, D_MODEL))],
        out_shape=[jax.ShapeDtypeStruct((1, 128), F32), jax.ShapeDtypeStruct((s, D_MODEL), F32),
                   jax.ShapeDtypeStruct((s, D_MODEL), BF16), jax.ShapeDtypeStruct((1, D_MODEL), F32)],
        compiler_params=_params(("arbitrary",)),
    )(x, g, target)


def _rms_bwd_rows(xv, g, dh, dres):
    r = _rsqrt_mean(xv)
    xh = xv * r
    dxh = dh * g
    dx = dres + r * (dxh - xh * jnp.mean(dxh * xh, axis=-1, keepdims=True))
    return dx, jnp.sum(dh * xh, axis=0, keepdims=True)


def _ffn_bwd(dxb, p, w_up, w_down, layer, sums=()):
    s = dxb.shape[0]
    t = _tile(s, 1024)
    ns = len(sums)

    def body(dx_ref, p_ref, wu_ref, wd_ref, *rest):
        dp_ref, dh_ref = rest[ns:ns + 2]
        if ns:
            i, j = pl.program_id(0), pl.program_id(1)
            _owners_in_steps(rest[:ns], rest[ns + 2:2 * ns + 2], rest[2 * ns + 2:],
                             jnp.logical_and(i == 0, j == 0), jnp.logical_and(i == s // t - 1, j == N_BLK - 1))
        da = _dot_nt(dx_ref[...], wd_ref[...])
        a = jnp.maximum(p_ref[...].astype(F32), 0.0)
        dp = (da * (2.0 * a)).astype(BF16)
        dp_ref[...] = dp
        dh = _dot_nt(dp, wu_ref[...])

        @pl.when(pl.program_id(1) == 0)
        def _():
            dh_ref[...] = dh

        @pl.when(pl.program_id(1) != 0)
        def _():
            dh_ref[...] += dh

    wspec = pl.BlockSpec((None, None, D_MODEL, D_FF_BLK), lambda i, j: (j, layer, 0, 0))
    row = pl.BlockSpec((t, D_MODEL), lambda i, j: (i, 0))
    blk = pl.BlockSpec((t, D_FF_BLK), lambda i, j: (i, j))
    out = pl.pallas_call(
        body, name="ffn_bwd", grid=(s // t, N_BLK),
        in_specs=[row, blk, wspec, wspec] + [_ANY] * ns, out_specs=[blk, row] + [_ANY] * ns,
        out_shape=[jax.ShapeDtypeStruct((s, N_BLK * D_FF_BLK), BF16), jax.ShapeDtypeStruct((s, D_MODEL), F32)]
                  + [jax.ShapeDtypeStruct(a.shape, a.dtype) for a in sums],
        scratch_shapes=_owner_sems(ns) if ns else [],
        compiler_params=_params(("arbitrary" if ns else "parallel", "arbitrary"), V7X_VMEM_LIMIT),
    )(dxb, p, w_up, w_down, *sums)
    return (out[0], out[1], out[2:]) if ns else out


def _ffn_wgrad(hb, p, dp, dxb):
    s = hb.shape[0]
    t = _tile(s, 1024)

    def body(h_ref, p_ref, dp_ref, dx_ref, du_ref, dd_ref):
        @pl.when(pl.program_id(1) == 0)
        def _():
            du_ref[...] = jnp.zeros_like(du_ref)
            dd_ref[...] = jnp.zeros_like(dd_ref)

        a = jnp.maximum(p_ref[...].astype(F32), 0.0)
        du_ref[...] += _dot_tn(h_ref[...], dp_ref[...])
        dd_ref[...] += _dot_tn((a * a).astype(BF16), dx_ref[...])

    row = pl.BlockSpec((t, D_MODEL), lambda j, i: (i, 0))
    blk = pl.BlockSpec((t, D_FF_BLK), lambda j, i: (i, j))
    out = pl.BlockSpec((None, D_MODEL, D_FF_BLK), lambda j, i: (j, 0, 0))
    shape = jax.ShapeDtypeStruct((N_BLK, D_MODEL, D_FF_BLK), F32)
    return pl.pallas_call(
        body, name="ffn_wgrad", grid=(N_BLK, s // t),
        in_specs=[row, blk, blk, row], out_specs=[out, out], out_shape=[shape, shape],
        compiler_params=_params(("parallel", "arbitrary"), V7X_VMEM_LIMIT),
    )(hb, p, dp, dxb)


def _in_proj_bwd(x, g, dx1, dz, w, layer):
    s = x.shape[0]
    t = _tile(s, 512)

    def body(x_ref, g_ref, dx1_ref, dz_ref, w_ref, dx0_ref, dxb_ref, dg_ref, wide_ref):
        _pair_blocks(w_ref, wide_ref)

        @pl.when(pl.program_id(0) == 0)
        def _():
            dg_ref[...] = jnp.zeros_like(dg_ref)

        dh = _dot_nt(dz_ref[:, 0:2 * W_IN_BLK], wide_ref[0])
        for n in range(1, N_BLK // 2):
            dh = dh + _dot_nt(dz_ref[:, 2 * n * W_IN_BLK:2 * (n + 1) * W_IN_BLK], wide_ref[n])
        dx, dg = _rms_bwd_rows(x_ref[...], g_ref[...], dh, dx1_ref[...])
        dx0_ref[...] = dx
        dxb_ref[...] = dx.astype(BF16)
        dg_ref[...] += dg

    row = pl.BlockSpec((t, D_MODEL), lambda i: (i, 0))
    return pl.pallas_call(
        body, name="in_proj_bwd", grid=(s // t,),
        in_specs=[row, _full((1, D_MODEL)), row, pl.BlockSpec((t, D_IN), lambda i: (i, 0)),
                  pl.BlockSpec((N_BLK, None, D_MODEL, W_IN_BLK), lambda i: (0, layer, 0, 0))],
        out_specs=[row, row, _full((1, D_MODEL))],
        out_shape=[jax.ShapeDtypeStruct((s, D_MODEL), F32), jax.ShapeDtypeStruct((s, D_MODEL), BF16),
                   jax.ShapeDtypeStruct((1, D_MODEL), F32)],
        scratch_shapes=[pltpu.VMEM((N_BLK // 2, D_MODEL, 2 * W_IN_BLK), BF16)],
        compiler_params=_params(("arbitrary",), V7X_VMEM_LIMIT),
    )(x, g, dx1, dz, w)


def _in_proj_wgrad(hb, dz):
    s = hb.shape[0]
    t = _tile(s, 512)

    def body(h_ref, dz_ref, dw_ref, wide_ref):
        @pl.when(pl.program_id(0) == 0)
        def _():
            wide_ref[...] = jnp.zeros_like(wide_ref)

        h = h_ref[...]
        for n in range(N_BLK // 2):
            wide_ref[n] += _dot_tn(h, dz_ref[:, 2 * n * W_IN_BLK:2 * (n + 1) * W_IN_BLK])

        @pl.when(pl.program_id(0) == s // t - 1)
        def _():
            for b in range(N_BLK):
                dw_ref[b] = wide_ref[b // 2, :, (b % 2) * W_IN_BLK:(b % 2 + 1) * W_IN_BLK]

    return pl.pallas_call(
        body, name="in_proj_wgrad", grid=(s // t,),
        in_specs=[pl.BlockSpec((t, D_MODEL), lambda i: (i, 0)), pl.BlockSpec((t, D_IN), lambda i: (i, 0))],
        out_specs=_full((N_BLK, D_MODEL, W_IN_BLK)),
        out_shape=jax.ShapeDtypeStruct((N_BLK, D_MODEL, W_IN_BLK), F32),
        scratch_shapes=[pltpu.VMEM((N_BLK // 2, D_MODEL, 2 * W_IN_BLK), F32)],
        compiler_params=_params(("arbitrary",), V7X_VMEM_LIMIT),
    )(hb, dz)


def _layer_params(small, layer):
    tril = jnp.tril(jnp.ones((CHUNK, CHUNK), bool))
    ws = jnp.where(tril, small["gmlp_w_s"][layer], 0.0)
    bmat = jnp.repeat(small["gmlp_b_s"][layer].T, HEAD_DIM, axis=1)
    scw = jnp.zeros((8, D_GROUP), F32).at[:SHORT_K].set(small["short_conv_w"][layer])
    ccw = jnp.zeros((32, D_GROUP), F32).at[:CONF_K].set(small["conf_conv_w"][layer])
    return dict(vg=small["gmlp_v_g"][layer][None], wt=ws.astype(BF16), wtt=jnp.swapaxes(ws, 1, 2).astype(BF16),
                bmat=bmat, scw=scw, ccw=ccw, lg=small["conf_ln_g"][layer][None], lb=small["conf_ln_b"][layer][None])


def _local_step(x, target, big, small, gather_pending=False, core=None):
    saved = []
    for l in range(DEPTH):
        p = _layer_params(small, l)
        z, hb, q_r, q_t, k_t, v_t = _in_proj(x, small["norm_mix_g"][l][None], big["w_in"], l)
        if gather_pending and l == 0:
            late = ("w_out", "w_up", "w_down")
            y_abd, filled = _mixers_fwd(z, p, [big[k] for k in late], [(n, 0) for n in range(len(late))])
            big = {**big, **dict(zip(late, filled))}
            o_t, filled = _attn_fwd(q_r, k_t, v_t, [big[k] for k in _BIG], [(n, 1) for n in range(len(_BIG))])
            big = dict(zip(_BIG, filled))
        else:
            y_abd = _mixers_fwd(z, p)
            o_t = _attn_fwd(q_r, k_t, v_t)
        x1, yn = _out_proj(x, y_abd, o_t, small["mix_out_g"][l][None], big["w_out"], l)
        x2, pre, h2b = _ffn(x1, small["norm_ffn_g"][l][None], big["w_up"], big["w_down"], l)
        saved.append(dict(p=p, x0=x, z=z, hb=hb, q_r=q_r, q_t=q_t, k_t=k_t, v_t=v_t, o_t=o_t, x1=x1, yn=yn, pre=pre,
                          h2b=h2b, y_abd=y_abd))
        x = x2

    loss, dx, dxb, d_final = _loss_head(x, small["final_norm_g"][None], target)

    g = {k: [None] * DEPTH for k in ("w_in", "w_out", "w_up", "w_down", "norm_mix_g", "gmlp_v_g", "gmlp_w_s", "gmlp_b_s",
                                     "short_conv_w", "conf_conv_w", "conf_ln_g", "conf_ln_b", "mix_out_g", "norm_ffn_g")}
    tril = jnp.tril(jnp.ones((CHUNK, CHUNK), bool))
    early = {}
    for l in reversed(range(DEPTH)):
        sv = saved[l]
        p = sv["p"]
        riding = []
        if core is not None and l == 0:
            riding = [(k, 1) for k in _BIG]
        sums = _chip_sums(core, [g[k][n] for k, n in riding])
        dpre, dh, *got = _ffn_bwd(dxb, sv["pre"], big["w_up"], big["w_down"], l, sums)
        early.update(zip(riding, zip(sums, *got)))
        g["w_up"][l], g["w_down"][l] = _ffn_wgrad(sv["h2b"], sv["pre"], dpre, dxb)
        dx1, g["norm_ffn_g"][l], dyn, g["mix_out_g"][l], g["w_out"][l] = _out_proj_bwd(
            sv["x1"], small["norm_ffn_g"][l][None], dh, dx, sv["yn"], small["mix_out_g"][l][None], big["w_out"], l)
        if core is not None and l == 0:
            riding = [("w_up", 0), ("w_down", 0)]
        sums = _chip_sums(core, [g[k][n] for k, n in riding])
        (dza, dcb, dcd, do_r, do_t, dsum, dvg, dws, dbm, dscw, dccw, dlg, dlb, *got) = _mixers_bwd_a(
            sv["z"], dyn, sv["o_t"], sv["y_abd"], p, sums)
        early.update(zip(riding, zip(sums, *got)))
        dq_t, dk_t, dv_t = _attn_bwd(sv["q_r"], sv["q_t"], sv["k_t"], sv["v_t"], do_r, do_t, dsum)
        dz = _mixers_bwd_b(sv["z"], dza, dcb, dcd, dq_t, dk_t, dv_t, p)
        dx, dxb, g["norm_mix_g"][l] = _in_proj_bwd(sv["x0"], small["norm_mix_g"][l][None], dx1, dz, big["w_in"], l)
        g["w_in"][l] = _in_proj_wgrad(sv["hb"], dz)
        g["gmlp_v_g"][l] = dvg[0]
        g["gmlp_w_s"][l] = jnp.where(tril, dws, 0.0)
        g["gmlp_b_s"][l] = dbm.reshape(CHUNK, N_HEADS, HEAD_DIM).sum(-1).T
        g["short_conv_w"][l] = dscw[:SHORT_K]
        g["conf_conv_w"][l] = dccw[:CONF_K]
        g["conf_ln_g"][l] = dlg[0]
        g["conf_ln_b"][l] = dlb[0]
        g["norm_mix_g"][l] = g["norm_mix_g"][l][0]
        g["mix_out_g"][l] = g["mix_out_g"][l][0]
        g["norm_ffn_g"][l] = g["norm_ffn_g"][l][0]
    grads = {k: v if k in ("w_in", "w_out", "w_up", "w_down") else jnp.stack(v) for k, v in g.items()}
    grads["final_norm_g"] = d_final[0]
    return loss, dx, grads, early


_ANY = pl.BlockSpec(memory_space=pl.ANY)


def _mesh_place():
    x, y, c = lax.axis_index("x"), lax.axis_index("y"), lax.axis_index("c")
    chips = [(1 - x, y), (x, 1 - y), (1 - x, 1 - y)]
    return x, y, c, 2 * x + y, chips


def _gather_stages(bufs, parts, sems):
    ici_send, ici_recv, d2d_send, d2d_recv = sems
    x, y, c, me, chips = _mesh_place()
    blk = [2 * chip[0] + chip[1] for chip in chips]
    pairs = [(p, r) for p in range(len(parts)) for r in range(3)]

    def rows(p, block, half_of):
        k, layer = parts[p]
        half = bufs[k].shape[2] // 2
        return bufs[k].at[block, layer, pl.ds(half_of * half, half), :]

    def ici(p, r, block):
        return pltpu.make_async_remote_copy(
            src_ref=rows(p, me, c), dst_ref=rows(p, block, c), send_sem=ici_send.at[3 * p + r],
            recv_sem=ici_recv.at[3 * p + r], device_id=(chips[r][0], chips[r][1], c), device_id_type=MESH)

    def d2d(p, r, half_of):
        part = rows(p, blk[r], half_of)
        return pltpu.make_async_remote_copy(
            src_ref=part, dst_ref=part, send_sem=d2d_send.at[3 * p + r], recv_sem=d2d_recv.at[3 * p + r],
            device_id=(x, y, 1 - c), device_id_type=MESH)

    def start():
        for p, r in pairs:
            ici(p, r, me).start()

    def forward(p):
        for r in range(3):
            ici(p, r, blk[r]).wait_recv()
            d2d(p, r, c).start()

    def finish():
        for p, r in pairs:
            d2d(p, r, 1 - c).wait_recv()
        for p, r in pairs:
            ici(p, r, me).wait_send()
            d2d(p, r, c).wait_send()

    return start, forward, finish


def _gather_sems(parts):
    return [pltpu.SemaphoreType.DMA((3 * len(parts),)) for _ in range(4)]


def _gather_in_steps(bufs, parts, sems, n_steps):
    start, forward, finish = _gather_stages(bufs, parts, sems)
    i = pl.program_id(0)
    pl.when(i == 0)(start)
    for p in range(len(parts)):
        pl.when(i == n_steps * (2 * p + 3) // (2 * len(parts) + 2))(lambda p=p: forward(p))
    pl.when(i == n_steps - 1)(finish)


def _gather_first(bufs, parts, whole):
    n, m = len(bufs), len(whole)

    def body(*refs):
        whole_in, buf_out, whole_out = refs[n:n + m], refs[n + m:2 * n + m], refs[2 * n + m:2 * (n + m)]
        sems = refs[2 * (n + m):]
        send_sems, recv_sems, local_sems = sems[4:]
        x, y, c, me, chips = _mesh_place()
        start, forward, finish = _gather_stages(buf_out, parts, sems[:4])

        def push(k, r, block):
            return pltpu.make_async_remote_copy(
                src_ref=whole_in[k], dst_ref=whole_out[k].at[block], send_sem=send_sems.at[3 * k + r],
                recv_sem=recv_sems.at[3 * k + r], device_id=(chips[r][0], chips[r][1], c), device_id_type=MESH)

        local = [pltpu.make_async_copy(whole_in[k], whole_out[k].at[me], local_sems.at[k]) for k in range(m)]
        for cp in local:
            cp.start()
        start()
        for k in range(m):
            for r in range(3):
                push(k, r, me).start()
        for p in range(len(parts)):
            forward(p)
        for k in range(m):
            for r, chip in enumerate(chips):
                push(k, r, 2 * chip[0] + chip[1]).wait_recv()
        for k in range(m):
            for r in range(3):
                push(k, r, me).wait_send()
        finish()
        for cp in local:
            cp.wait()

    return pl.pallas_call(
        body, name="gather_first",
        in_specs=[_ANY] * (n + m), out_specs=[_ANY] * (n + m),
        out_shape=[jax.ShapeDtypeStruct(b.shape, b.dtype) for b in bufs]
                  + [jax.ShapeDtypeStruct((N_BLK,) + b.shape, b.dtype) for b in whole],
        input_output_aliases={k: k for k in range(n)},
        scratch_shapes=_gather_sems(parts) + [pltpu.SemaphoreType.DMA((3 * m,)), pltpu.SemaphoreType.DMA((3 * m,)),
                                              pltpu.SemaphoreType.DMA((m,))],
    )(*bufs, *whole)


def _swap_halves(gs):
    n = len(gs)

    def body(*refs):
        ins, outs, (send_sems, recv_sems) = refs[:n], refs[n:2 * n], refs[2 * n:]
        x, y, c, _, _ = _mesh_place()
        cps = []
        for k in range(n):
            half = ins[k].shape[1] // 2
            cps.append(pltpu.make_async_remote_copy(
                src_ref=ins[k].at[:, pl.ds((1 - c) * half, half), :], dst_ref=outs[k],
                send_sem=send_sems.at[k], recv_sem=recv_sems.at[k], device_id=(x, y, 1 - c), device_id_type=MESH))
        for cp in cps:
            cp.start()
        for cp in cps:
            cp.wait()

    return pl.pallas_call(
        body, name="swap_halves", in_specs=[_ANY] * n, out_specs=[_ANY] * n,
        out_shape=[jax.ShapeDtypeStruct((g.shape[0], g.shape[1] // 2, g.shape[2]), F32) for g in gs],
        scratch_shapes=[pltpu.SemaphoreType.DMA((n,)), pltpu.SemaphoreType.DMA((n,))],
    )(*gs)


def _owner_stages(ins, outs, sems):
    send_sems, recv_sems = sems
    x, y, c, me, chips = _mesh_place()
    pairs = [(k, r) for k in range(len(ins)) for r in range(3)]

    def remote(k, r, src_block, dst_block):
        return pltpu.make_async_remote_copy(
            src_ref=ins[k].at[src_block], dst_ref=outs[k].at[dst_block], send_sem=send_sems.at[3 * k + r],
            recv_sem=recv_sems.at[3 * k + r], device_id=(chips[r][0], chips[r][1], c), device_id_type=MESH)

    def start():
        for k, r in pairs:
            remote(k, r, 2 * chips[r][0] + chips[r][1], me).start()

    def finish():
        for k, r in pairs:
            remote(k, r, me, 2 * chips[r][0] + chips[r][1]).wait_recv()
        for k, r in pairs:
            remote(k, r, 2 * chips[r][0] + chips[r][1], me).wait_send()

    return start, finish


def _owner_sems(n):
    return [pltpu.SemaphoreType.DMA((3 * n,)), pltpu.SemaphoreType.DMA((3 * n,))]


def _owners_in_steps(ins, outs, sems, first, last):
    start, finish = _owner_stages(ins, outs, sems)
    pl.when(first)(start)
    pl.when(last)(finish)


def _send_to_owners(sums):
    n = len(sums)

    def body(*refs):
        start, finish = _owner_stages(refs[:n], refs[n:2 * n], refs[2 * n:])
        start()
        finish()

    return pl.pallas_call(
        body, name="send_to_owners", in_specs=[_ANY] * n, out_specs=[_ANY] * n,
        out_shape=[jax.ShapeDtypeStruct(s.shape, s.dtype) for s in sums],
        scratch_shapes=_owner_sems(n),
    )(*sums)


def _swap_reduced(fs):
    n = len(fs)

    def body(*refs):
        ins, outs, (send_sems, recv_sems) = refs[:n], refs[n:2 * n], refs[2 * n:]
        x, y, c, _, _ = _mesh_place()
        cps = [pltpu.make_async_remote_copy(src_ref=ins[k], dst_ref=outs[k], send_sem=send_sems.at[k],
                                            recv_sem=recv_sems.at[k], device_id=(x, y, 1 - c), device_id_type=MESH)
               for k in range(n)]
        for cp in cps:
            cp.start()
        for cp in cps:
            cp.wait()

    return pl.pallas_call(
        body, name="swap_reduced", in_specs=[_ANY] * n, out_specs=[_ANY] * n,
        out_shape=[jax.ShapeDtypeStruct(f.shape, F32) for f in fs],
        scratch_shapes=[pltpu.SemaphoreType.DMA((n,)), pltpu.SemaphoreType.DMA((n,))],
    )(*fs)


def _row_tile(rows):
    return min(rows, 256)


def _chip_sums(core, grads):
    if not grads:
        return []
    return [_add_pairs(core, a, b) for a, b in zip(grads, _swap_halves(grads))]


def _add_pairs(core, g, other):
    n, half, cols = other.shape
    t = _row_tile(half)
    per_half = half // t

    def body(c_ref, a_ref, b_ref, o_ref):
        o_ref[...] = (a_ref[...] + b_ref[...]).astype(BF16)

    spec = pl.BlockSpec((None, t, cols), lambda i, j, c_ref: (i, j, 0))
    return pl.pallas_call(
        body, name="add_pairs",
        grid_spec=pltpu.PrefetchScalarGridSpec(
            num_scalar_prefetch=1, grid=(n, per_half),
            in_specs=[pl.BlockSpec((None, t, cols), lambda i, j, c_ref: (i, c_ref[0] * per_half + j, 0)), spec],
            out_specs=spec),
        out_shape=jax.ShapeDtypeStruct(other.shape, BF16), compiler_params=_params(("parallel", "parallel")),
    )(core, g, other)


def _add_chips(me, s1, r2):
    _, r, cols = r2.shape
    t = _row_tile(r)

    def body(me_ref, s_ref, r_ref, o_ref):
        own = s_ref[...].astype(F32)
        parts = [jnp.where(me_ref[0] == k, own, r_ref[k].astype(F32)) for k in range(N_BLK)]
        o_ref[...] = ((parts[0] + parts[1]) + parts[2]) + parts[3]

    return pl.pallas_call(
        body, name="add_chips",
        grid_spec=pltpu.PrefetchScalarGridSpec(
            num_scalar_prefetch=1, grid=(r // t,),
            in_specs=[pl.BlockSpec((None, t, cols), lambda i, me_ref: (me_ref[0], i, 0)),
                      pl.BlockSpec((N_BLK, t, cols), lambda i, me_ref: (0, i, 0))],
            out_specs=pl.BlockSpec((t, cols), lambda i, me_ref: (i, 0))),
        out_shape=jax.ShapeDtypeStruct((r, cols), F32), compiler_params=_params(("parallel",)),
    )(me, s1, r2)


def _adamw(core, mine, other, w, m, v, layer, earlier=None):
    half, cols = mine.shape
    t = _row_tile(half)
    per_half = half // t
    c1 = 1.0 - ADAM_B1 ** ADAM_STEP
    c2 = 1.0 - ADAM_B2 ** ADAM_STEP

    def body(c_ref, a_ref, b_ref, w_ref, m_ref, v_ref, *rest):
        g_ref, d_ref, mo_ref, vo_ref = rest[-4:]
        gv = jnp.where(pl.program_id(0) // per_half == c_ref[0], a_ref[...], b_ref[...])
        g_ref[...] = gv
        m_new = ADAM_B1 * m_ref[...] + (1.0 - ADAM_B1) * gv
        v_new = ADAM_B2 * v_ref[...] + (1.0 - ADAM_B2) * (gv * gv)
        mo_ref[...] = m_new
        vo_ref[...] = v_new
        d_ref[...] = -ADAM_LR * ((m_new / c1) / (jnp.sqrt(v_new / c2) + ADAM_EPS) + ADAM_WD * w_ref[...])

    part = pl.BlockSpec((t, cols), lambda i, c_ref: (i % per_half, 0))
    spec = pl.BlockSpec((None, t, cols), lambda i, c_ref: (layer, i, 0))
    kept = [] if earlier is None else list(earlier)
    return pl.pallas_call(
        body, name="adamw",
        grid_spec=pltpu.PrefetchScalarGridSpec(
            num_scalar_prefetch=1, grid=(2 * per_half,),
            in_specs=[part, part, spec, spec, spec] + [_ANY] * len(kept), out_specs=[spec] * 4),
        out_shape=[jax.ShapeDtypeStruct(w.shape, F32)] * 4,
        input_output_aliases={6 + k: k for k in range(len(kept))},
        compiler_params=_params(("parallel",)),
    )(core, mine, other, w, m, v, *kept)


_REPLICATED = ("norm_mix_g", "gmlp_v_g", "gmlp_w_s", "gmlp_b_s", "conf_ln_g", "conf_ln_b", "mix_out_g", "norm_ffn_g",
               "final_norm_g")
_REP_SHAPES = {"norm_mix_g": (DEPTH, D_MODEL), "gmlp_v_g": (DEPTH, D_GROUP), "gmlp_w_s": (DEPTH, N_HEADS, CHUNK, CHUNK),
               "gmlp_b_s": (DEPTH, N_HEADS, CHUNK), "conf_ln_g": (DEPTH, D_GROUP), "conf_ln_b": (DEPTH, D_GROUP),
               "mix_out_g": (DEPTH, D_MODEL), "norm_ffn_g": (DEPTH, D_MODEL), "final_norm_g": (D_MODEL,)}
_BIG = ("w_in", "w_out", "w_up", "w_down")
_CONV_ROWS = 8
_REP_ROWS = 144
_SMALL_ROWS = 160
_CH_BLK = D_GROUP // N_BLK


def _pad_rows(flat, rows):
    pad = rows * D_MODEL - flat.shape[-1]
    flat = jnp.pad(flat, [(0, 0)] * (flat.ndim - 1) + [(0, pad)])
    return flat.reshape(flat.shape[:-1] + (rows, D_MODEL))


def _pack_small(scw, ccw, rep):
    lead = scw.shape[:-3]
    conv = jnp.concatenate([scw.reshape(lead + (-1,)), ccw.reshape(lead + (-1,))], axis=-1)
    flat = jnp.concatenate([rep[k].reshape(-1) for k in _REPLICATED])
    flat = jnp.broadcast_to(flat, lead + flat.shape)
    parts = [_pad_rows(conv, _CONV_ROWS), _pad_rows(flat, _REP_ROWS),
             jnp.zeros(lead + (_SMALL_ROWS - _CONV_ROWS - _REP_ROWS, D_MODEL), F32)]
    return jnp.concatenate(parts, axis=-2)


def _unpack_small(pk):
    out = {}
    conv = pk[:_CONV_ROWS].reshape(-1)
    n_s = DEPTH * SHORT_K * _CH_BLK
    out["short_conv_w"] = conv[:n_s].reshape(DEPTH, SHORT_K, _CH_BLK)
    out["conf_conv_w"] = conv[n_s:n_s + DEPTH * CONF_K * _CH_BLK].reshape(DEPTH, CONF_K, _CH_BLK)
    row = _CONV_ROWS
    flat = pk[row:row + _REP_ROWS].reshape(-1)
    at = 0
    for k in _REPLICATED:
        n = math.prod(_REP_SHAPES[k])
        out[k] = flat[at:at + n].reshape(_REP_SHAPES[k])
        at += n
    return out


def _conv_blocks(w):
    d, k, _ = w.shape
    return w.reshape(d, k, N_BLK, _CH_BLK).transpose(2, 0, 1, 3)


_WEIGHTS = ("norm_mix_g", "w_in", "gmlp_v_g", "gmlp_w_s", "gmlp_b_s", "short_conv_w", "conf_conv_w", "conf_ln_g",
            "conf_ln_b", "mix_out_g", "w_out", "norm_ffn_g", "w_up", "w_down", "final_norm_g")


def kernel(x, norm_mix_g, w_in, gmlp_v_g, gmlp_w_s, gmlp_b_s, short_conv_w, conf_conv_w, conf_ln_g, conf_ln_b, mix_out_g, w_out, norm_ffn_g, w_up, w_down, final_norm_g, loss_target, m_norm_mix_g, m_w_in, m_gmlp_v_g, m_gmlp_w_s, m_gmlp_b_s, m_short_conv_w, m_conf_conv_w, m_conf_ln_g, m_conf_ln_b, m_mix_out_g, m_w_out, m_norm_ffn_g, m_w_up, m_w_down, m_final_norm_g, v_norm_mix_g, v_w_in, v_gmlp_v_g, v_gmlp_w_s, v_gmlp_b_s, v_short_conv_w, v_conf_conv_w, v_conf_ln_g, v_conf_ln_b, v_mix_out_g, v_w_out, v_norm_ffn_g, v_w_up, v_w_down, v_final_norm_g):
    w = dict(norm_mix_g=norm_mix_g, w_in=w_in, gmlp_v_g=gmlp_v_g, gmlp_w_s=gmlp_w_s, gmlp_b_s=gmlp_b_s,
             short_conv_w=short_conv_w, conf_conv_w=conf_conv_w, conf_ln_g=conf_ln_g, conf_ln_b=conf_ln_b,
             mix_out_g=mix_out_g, w_out=w_out, norm_ffn_g=norm_ffn_g, w_up=w_up, w_down=w_down, final_norm_g=final_norm_g)
    m = dict(norm_mix_g=m_norm_mix_g, w_in=m_w_in, gmlp_v_g=m_gmlp_v_g, gmlp_w_s=m_gmlp_w_s, gmlp_b_s=m_gmlp_b_s,
             short_conv_w=m_short_conv_w, conf_conv_w=m_conf_conv_w, conf_ln_g=m_conf_ln_g, conf_ln_b=m_conf_ln_b,
             mix_out_g=m_mix_out_g, w_out=m_w_out, norm_ffn_g=m_norm_ffn_g, w_up=m_w_up, w_down=m_w_down,
             final_norm_g=m_final_norm_g)
    v = dict(norm_mix_g=v_norm_mix_g, w_in=v_w_in, gmlp_v_g=v_gmlp_v_g, gmlp_w_s=v_gmlp_w_s, gmlp_b_s=v_gmlp_b_s,
             short_conv_w=v_short_conv_w, conf_conv_w=v_conf_conv_w, conf_ln_g=v_conf_ln_g, conf_ln_b=v_conf_ln_b,
             mix_out_g=v_mix_out_g, w_out=v_w_out, norm_ffn_g=v_norm_ffn_g, w_up=v_w_up, w_down=v_w_down,
             final_norm_g=v_final_norm_g)
    core = lax.axis_index("c").astype(jnp.int32).reshape(1)
    me = (2 * lax.axis_index("x") + lax.axis_index("y")).astype(jnp.int32).reshape(1)

    conv_mine = _pad_rows(jnp.concatenate([short_conv_w.reshape(-1), conf_conv_w.reshape(-1)]), _CONV_ROWS)
    big = {k: _cast_into_slot(w[k], me, "cast_" + k) for k in _BIG}
    big["w_in"], conv_all = _gather_first([big["w_in"]], [(0, 0)], [conv_mine])
    conv_all = conv_all.reshape(N_BLK, -1)
    n_s = DEPTH * SHORT_K * _CH_BLK
    scw_all = conv_all[:, :n_s].reshape(N_BLK, DEPTH, SHORT_K, _CH_BLK)
    ccw_all = conv_all[:, n_s:n_s + DEPTH * CONF_K * _CH_BLK].reshape(N_BLK, DEPTH, CONF_K, _CH_BLK)
    small = {k: w[k] for k in _REPLICATED}
    small["short_conv_w"] = scw_all.transpose(1, 2, 0, 3).reshape(DEPTH, SHORT_K, D_GROUP)
    small["conf_conv_w"] = ccw_all.transpose(1, 2, 0, 3).reshape(DEPTH, CONF_K, D_GROUP)

    loss, dx, g, early = _local_step(x[0], loss_target[0], big, small, gather_pending=True, core=core)

    where = [(k, l) for k in _BIG for l in range(DEPTH)]
    late = [kl for kl in where if kl not in early]
    sums = _chip_sums(core, [g[k][l] for k, l in late]
                      + [_pack_small(_conv_blocks(g["short_conv_w"]), _conv_blocks(g["conf_conv_w"]), g)])
    sent = {**early, **dict(zip(late + ["small"], zip(sums, _send_to_owners(sums))))}
    mine = [_add_chips(me, *sent[kl]) for kl in where + ["small"]]
    other = _swap_reduced(mine)

    done = {}
    for n, (k, l) in enumerate(where):
        done[k] = _adamw(core, mine[n], other[n], w[k], m[k], v[k], l, done.get(k))
    small_own = [_pack_small(t["short_conv_w"], t["conf_conv_w"], t)[None] for t in (w, m, v)]
    small_done = [_unpack_small(a[0]) for a in _adamw(core, mine[-1], other[-1], *small_own, 0)]

    outs = [lax.psum(loss[0, 0], ("x", "y", "c")), dx[None]]
    for kind in range(4):
        outs += [done[k][kind] if k in _BIG else small_done[kind][k] for k in _WEIGHTS]
    return tuple(outs)
```

```python
import math

import jax
import jax.numpy as jnp
from jax import lax
from jax.experimental import pallas as pl
from jax.experimental.pallas import tpu as pltpu

F32 = jnp.float32
BF16 = jnp.bfloat16

D_MODEL = 1024
D_GROUP = 256
N_HEADS = 4
HEAD_DIM = 64
CHUNK = 128
D_IN = 2560
N_BLK = 4
W_IN_BLK = D_IN // N_BLK
D_FF_BLK = 1024
DEPTH = 2
EPS = 1e-6
HALO = 32
MIX_ROWS = 512
SHORT_K = 3
CONF_K = 31
ATT_TQ = 256
ATT_TK = 256
ATT_SCALE = 0.125
ATT_DEAD = -104.0
V7X_VMEM_LIMIT = 56 * 1024 * 1024

ADAM_LR, ADAM_B1, ADAM_B2, ADAM_EPS, ADAM_WD, ADAM_STEP = 0.001, 0.9, 0.999, 1e-08, 0.01, 10

MESH = pl.DeviceIdType.MESH


def _params(sem, vmem=None):
    return pltpu.CompilerParams(dimension_semantics=sem, vmem_limit_bytes=vmem)


def _tile(s, t):
    return min(s, t)


def _rsqrt_mean(v):
    return lax.rsqrt(jnp.mean(v * v, axis=-1, keepdims=True) + EPS)


def _sigmoid(v):
    return 1.0 / (1.0 + jnp.exp(-v))


_GELU_C = math.sqrt(2.0 / math.pi)


def _gelu_tanh(v):
    return jnp.tanh(_GELU_C * (v + 0.044715 * (v * v * v)))


def _gelu(v, t):
    return v * (0.5 * (1.0 + t))


def _gelu_grad(v, t):
    return 0.5 * (1.0 + t) + v * (0.5 * (1.0 - t * t) * _GELU_C * (1.0 + 3.0 * 0.044715 * (v * v)))


def _dot(a, b):
    return jnp.dot(a, b, preferred_element_type=F32)


def _dot_nt(a, b):
    return lax.dot_general(a, b, (((1,), (1,)), ((), ())), preferred_element_type=F32)


def _dot_tn(a, b):
    return lax.dot_general(a, b, (((0,), (0,)), ((), ())), preferred_element_type=F32)


def _cast_into_slot(w, me, name):
    n, r, c = w.shape
    tr = _tile(r, 256)

    def body(me_ref, w_ref, o_ref):
        o_ref[...] = w_ref[...].astype(BF16)

    return pl.pallas_call(
        body, name=name,
        grid_spec=pltpu.PrefetchScalarGridSpec(
            num_scalar_prefetch=1, grid=(n, r // tr),
            in_specs=[pl.BlockSpec((None, tr, c), lambda a, b, me_ref: (a, b, 0))],
            out_specs=pl.BlockSpec((None, None, tr, c), lambda a, b, me_ref: (me_ref[0], a, b, 0))),
        out_shape=jax.ShapeDtypeStruct((N_BLK,) + w.shape, BF16),
        compiler_params=_params(("parallel", "parallel")),
    )(me, w)


def _split_heads(xv, rows_ref, cols_ref):
    if rows_ref is not None:
        for h in range(N_HEADS):
            rows_ref[h] = xv[:, h * HEAD_DIM:(h + 1) * HEAD_DIM].astype(BF16)
    if cols_ref is not None:
        xt = xv.T
        for h in range(N_HEADS):
            cols_ref[h] = xt[h * HEAD_DIM:(h + 1) * HEAD_DIM, :].astype(BF16)


def _head_specs(t, s):
    rows = (pl.BlockSpec((N_HEADS, t, HEAD_DIM), lambda i: (0, i, 0)), jax.ShapeDtypeStruct((N_HEADS, s, HEAD_DIM), BF16))
    cols = (pl.BlockSpec((N_HEADS, HEAD_DIM, t), lambda i: (0, 0, i)), jax.ShapeDtypeStruct((N_HEADS, HEAD_DIM, s), BF16))
    return rows, cols


def _cols(ref, lo, hi):
    return ref[:, lo:hi].astype(F32)


def _pair_blocks(w_ref, wide_ref):
    @pl.when(pl.program_id(0) == 0)
    def _():
        for b in range(N_BLK):
            wide_ref[b // 2, :, (b % 2) * W_IN_BLK:(b % 2 + 1) * W_IN_BLK] = w_ref[b]


def _in_proj(x, g, w, layer):
    s = x.shape[0]
    t = _tile(s, 512)

    def body(x_ref, g_ref, w_ref, z_ref, h_ref, qr_ref, qt_ref, kt_ref, vt_ref, wide_ref):
        _pair_blocks(w_ref, wide_ref)
        xv = x_ref[...]
        h = (xv * _rsqrt_mean(xv) * g_ref[...]).astype(BF16)
        h_ref[...] = h
        for n in range(N_BLK // 2):
            z_ref[:, 2 * n * W_IN_BLK:2 * (n + 1) * W_IN_BLK] = _dot(h, wide_ref[n]).astype(BF16)
        _split_heads(_cols(z_ref, 1280, 1536) * ATT_SCALE, qr_ref, qt_ref)
        _split_heads(_cols(z_ref, 1536, 1792), None, kt_ref)
        _split_heads(_cols(z_ref, 1792, 2048), None, vt_ref)

    rows, cols = _head_specs(t, s)
    return pl.pallas_call(
        body, name="in_proj", grid=(s // t,),
        in_specs=[pl.BlockSpec((t, D_MODEL), lambda i: (i, 0)), _full((1, D_MODEL)),
                  pl.BlockSpec((N_BLK, None, D_MODEL, W_IN_BLK), lambda i: (0, layer, 0, 0))],
        out_specs=[pl.BlockSpec((t, D_IN), lambda i: (i, 0)), pl.BlockSpec((t, D_MODEL), lambda i: (i, 0)),
                   rows[0], cols[0], cols[0], cols[0]],
        out_shape=[jax.ShapeDtypeStruct((s, D_IN), BF16), jax.ShapeDtypeStruct((s, D_MODEL), BF16),
                   rows[1], cols[1], cols[1], cols[1]],
        scratch_shapes=[pltpu.VMEM((N_BLK // 2, D_MODEL, 2 * W_IN_BLK), BF16)],
        compiler_params=_params(("arbitrary",), V7X_VMEM_LIMIT),
    )(x, g, w)


def _mix_a_fwd(z_ref, vg, wt_ref, bmat, t):
    zu = _cols(z_ref, 0, 256)
    zv = _cols(z_ref, 256, 512)
    tu = _gelu_tanh(zu)
    tv = _gelu_tanh(zv)
    u = _gelu(zu, tu)
    v = _gelu(zv, tv)
    rv = _rsqrt_mean(v)
    vh = v * rv
    vnb = (vh * vg).astype(BF16)
    head = lax.broadcasted_iota(jnp.int32, (CHUNK, D_GROUP), 1) // HEAD_DIM
    fs = []
    for c in range(t // CHUNK):
        vc = vnb[c * CHUNK:(c + 1) * CHUNK, :]
        fc = bmat
        for h in range(N_HEADS):
            fc = fc + jnp.where(head == h, _dot(wt_ref[h], vc), 0.0)
        fs.append(fc)
    f = jnp.concatenate(fs, axis=0) if len(fs) > 1 else fs[0]
    return (zu, tu), (zv, tv), u, rv, vh, vnb, f


def _windows(ext_ref, sh_ref, t):
    for b in range(1, 8):
        sh_ref[b - 1] = ext_ref[pl.ds(b, HALO + t - 8), :]

    def window(o):
        a, b = divmod(o, 8)
        return ext_ref[pl.ds(8 * a, t), :] if b == 0 else sh_ref[b - 1, pl.ds(8 * a, t), :]

    return window


def _mix_b_fwd(z_ref, zh_ref, first, scw_ref, ext_ref, t):
    gb = _cols(z_ref, 512, 768)
    uh = _cols(zh_ref, 768, 1024) * _cols(zh_ref, 1024, 1280)
    ext_ref[0:HALO, :] = jnp.where(first, 0.0, uh)
    ext_ref[HALO:HALO + t, :] = _cols(z_ref, 768, 1024) * _cols(z_ref, 1024, 1280)
    cv = jnp.zeros((t, D_GROUP), F32)
    for k in range(SHORT_K):
        cv = cv + scw_ref[k:k + 1, :] * ext_ref[pl.ds(HALO - (SHORT_K - 1) + k, t), :]
    return gb, cv


def _mix_d_fwd(z_ref, zh_ref, first, ccw_ref, lg, lb, ext_ref, sh_ref, t, cv=None):
    hh = _cols(zh_ref, 2048, 2304) * _sigmoid(_cols(zh_ref, 2304, 2560))
    ext_ref[0:HALO, :] = jnp.where(first, 0.0, hh)
    ext_ref[HALO:HALO + t, :] = _cols(z_ref, 2048, 2304) * _sigmoid(_cols(z_ref, 2304, 2560))
    window = _windows(ext_ref, sh_ref, t)
    if cv is None:
        cv = jnp.zeros((t, D_GROUP), F32)
        for k in range(CONF_K):
            cv = cv + ccw_ref[k:k + 1, :] * window(HALO - (CONF_K - 1) + k)
    xc = cv - jnp.mean(cv, axis=-1, keepdims=True)
    rs = lax.rsqrt(jnp.mean(xc * xc, axis=-1, keepdims=True) + EPS)
    xh = xc * rs
    ln = xh * lg + lb
    return xh, rs, ln, _sigmoid(ln), window, cv


def _mix_specs(t, s):
    per = t // HALO
    return [pl.BlockSpec((t, D_IN), lambda i: (i, 0)),
            pl.BlockSpec((HALO, D_IN), lambda i: (jnp.maximum(i * per - 1, 0), 0))]


def _full(shape):
    return pl.BlockSpec(shape, lambda i: (0,) * len(shape))


def _mixers_fwd(z, p, bufs=(), parts=()):
    s = z.shape[0]
    t = _tile(s, MIX_ROWS)
    nb = len(bufs)

    def body(z_ref, zh_ref, vg_ref, wt_ref, bm_ref, scw_ref, ccw_ref, lg_ref, lb_ref, *rest):
        y_ref = rest[nb]
        eb_ref, ed_ref, sh_ref = rest[2 * nb + 1:2 * nb + 4]
        if nb:
            _gather_in_steps(rest[nb + 1:2 * nb + 1], parts, rest[2 * nb + 4:], s // t)
        first = pl.program_id(0) == 0
        _, _, u, _, _, _, f = _mix_a_fwd(z_ref, vg_ref[...], wt_ref, bm_ref[...], t)
        ya = u * f
        y_ref[:, 0:256] = ya * _rsqrt_mean(ya)
        gb, cv = _mix_b_fwd(z_ref, zh_ref, first, scw_ref, eb_ref, t)
        yb = gb * cv
        y_ref[:, 256:512] = yb * _rsqrt_mean(yb)
        _, _, ln, sg, _, cvd = _mix_d_fwd(z_ref, zh_ref, first, ccw_ref, lg_ref[...], lb_ref[...], ed_ref, sh_ref, t)
        yd = ln * sg
        y_ref[:, 512:768] = yd * _rsqrt_mean(yd)
        y_ref[:, 768:1024] = cvd

    out = pl.pallas_call(
        body, name="mixers_fwd", grid=(s // t,),
        in_specs=_mix_specs(t, s) + [_full((1, D_GROUP)), _full((N_HEADS, CHUNK, CHUNK)), _full((CHUNK, D_GROUP)),
                                     _full((8, D_GROUP)), _full((32, D_GROUP)), _full((1, D_GROUP)), _full((1, D_GROUP))]
                 + [_ANY] * nb,
        out_specs=[pl.BlockSpec((t, D_MODEL), lambda i: (i, 0))] + [_ANY] * nb,
        out_shape=[jax.ShapeDtypeStruct((s, D_MODEL), F32)] + [jax.ShapeDtypeStruct(b.shape, b.dtype) for b in bufs],
        input_output_aliases={9 + k: 1 + k for k in range(nb)},
        scratch_shapes=[pltpu.VMEM((HALO + t, D_GROUP), F32), pltpu.VMEM((HALO + t, D_GROUP), F32),
                        pltpu.VMEM((7, HALO + t - 8, D_GROUP), F32)] + (_gather_sems(parts) if nb else []),
        compiler_params=_params(("arbitrary",) if nb else ("parallel",), V7X_VMEM_LIMIT),
    )(z, z, p["vg"], p["wt"], p["bmat"], p["scw"], p["ccw"], p["lg"], p["lb"], *bufs)
    return (out[0], out[1:]) if nb else out[0]


def _mixers_bwd_a(z, dyn, o_t, y_abd, p, sums=()):
    s = z.shape[0]
    t = _tile(s, MIX_ROWS)
    n_chunk = t // CHUNK
    ns = len(sums)

    def body(*refs):
        (z_ref, zh_ref, dyn_ref, ot_ref, cv_ref, vg_ref, wt_ref, wtt_ref, bm_ref, scw_ref, ccw_ref, lg_ref,
         lb_ref) = refs[:13]
        (dza_ref, dcb_ref, dcd_ref, dor_ref, dot_ref, ds_ref, dvg_ref, dws_ref, dbm_ref, dscw_ref, dccw_ref, dlg_ref,
         dlb_ref) = refs[13 + ns:26 + ns]
        eb_ref, ed_ref, sh_ref = refs[26 + 2 * ns:29 + 2 * ns]
        i = pl.program_id(0)
        first = i == 0
        if ns:
            _owners_in_steps(refs[13:13 + ns], refs[26 + ns:26 + 2 * ns], refs[29 + 2 * ns:], first, i == s // t - 1)

        @pl.when(first)
        def _():
            for r in (dvg_ref, dws_ref, dbm_ref, dscw_ref, dccw_ref, dlg_ref, dlb_ref):
                r[...] = jnp.zeros_like(r)

        def rms_bwd(y, dn):
            r = _rsqrt_mean(y)
            yn = y * r
            return r * (dn - yn * jnp.mean(dn * yn, axis=-1, keepdims=True))

        vg = vg_ref[...]
        gelu_u, gelu_v, u, rv, vh, vnb, f = _mix_a_fwd(z_ref, vg, wt_ref, bm_ref[...], t)
        dya = rms_bwd(u * f, _cols(dyn_ref, 0, 256))
        du = dya * f
        df = dya * u
        head = lax.broadcasted_iota(jnp.int32, (CHUNK, D_GROUP), 1) // HEAD_DIM
        dvns = []
        dbm = jnp.zeros((CHUNK, D_GROUP), F32)
        for c in range(n_chunk):
            dfc = df[c * CHUNK:(c + 1) * CHUNK, :]
            vc = vnb[c * CHUNK:(c + 1) * CHUNK, :]
            dbm = dbm + dfc
            dvn = jnp.zeros((CHUNK, D_GROUP), F32)
            for h in range(N_HEADS):
                dfh = jnp.where(head == h, dfc, 0.0).astype(BF16)
                dvn = dvn + _dot(wtt_ref[h], dfh)
                dws_ref[h] += _dot_nt(dfh, vc)
            dvns.append(dvn)
        dbm_ref[...] += dbm
        dvn = jnp.concatenate(dvns, axis=0) if n_chunk > 1 else dvns[0]
        dvg_ref[...] += jnp.sum(dvn * vh, axis=0, keepdims=True)
        dvh = dvn * vg
        dv = rv * (dvh - vh * jnp.mean(dvh * vh, axis=-1, keepdims=True))
        dza_ref[:, 0:256] = (du * _gelu_grad(*gelu_u)).astype(BF16)
        dza_ref[:, 256:512] = (dv * _gelu_grad(*gelu_v)).astype(BF16)

        gb, cv = _mix_b_fwd(z_ref, zh_ref, first, scw_ref, eb_ref, t)
        dyb = rms_bwd(gb * cv, _cols(dyn_ref, 256, 512))
        dza_ref[:, 512:768] = (dyb * cv).astype(BF16)
        dcb = dyb * gb
        dcb_ref[...] = dcb
        for k in range(SHORT_K):
            dscw_ref[k:k + 1, :] += jnp.sum(dcb * eb_ref[pl.ds(HALO - (SHORT_K - 1) + k, t), :], axis=0, keepdims=True)

        lg = lg_ref[...]
        xh, rs, ln, sg, window, _ = _mix_d_fwd(z_ref, zh_ref, first, ccw_ref, lg, lb_ref[...], ed_ref, sh_ref, t,
                                               cv_ref[...])
        dyd = rms_bwd(ln * sg, _cols(dyn_ref, 768, 1024))
        dln = dyd * (sg * (1.0 + ln * (1.0 - sg)))
        dlg_ref[...] += jnp.sum(dln * xh, axis=0, keepdims=True)
        dlb_ref[...] += jnp.sum(dln, axis=0, keepdims=True)
        dxh = dln * lg
        dcd = rs * (dxh - jnp.mean(dxh, axis=-1, keepdims=True) - xh * jnp.mean(dxh * xh, axis=-1, keepdims=True))
        dcd_ref[...] = dcd
        for k in range(CONF_K):
            dccw_ref[k:k + 1, :] += jnp.sum(dcd * window(HALO - (CONF_K - 1) + k), axis=0, keepdims=True)

        o = ot_ref[...].reshape(D_GROUP, t).T
        do = rms_bwd(o, _cols(dyn_ref, 512, 768))
        _split_heads(do, dor_ref, dot_ref)
        prod = do.astype(BF16).astype(F32) * o
        for h in range(N_HEADS):
            ds_ref[h] = jnp.sum(prod[:, h * HEAD_DIM:(h + 1) * HEAD_DIM], axis=1, keepdims=True)

    small = [(1, D_GROUP), (N_HEADS, CHUNK, CHUNK), (CHUNK, D_GROUP), (8, D_GROUP), (32, D_GROUP), (1, D_GROUP), (1, D_GROUP)]
    rows, cols = _head_specs(t, s)
    out = pl.pallas_call(
        body, name="mixers_bwd_a", grid=(s // t,),
        in_specs=_mix_specs(t, s) + [pl.BlockSpec((t, D_MODEL), lambda i: (i, 0)),
                                     pl.BlockSpec((N_HEADS, HEAD_DIM, t), lambda i: (0, 0, i)),
                                     pl.BlockSpec((t, D_GROUP), lambda i: (i, 3)),
                                     _full((1, D_GROUP)), _full((N_HEADS, CHUNK, CHUNK)), _full((N_HEADS, CHUNK, CHUNK)),
                                     _full((CHUNK, D_GROUP)), _full((8, D_GROUP)), _full((32, D_GROUP)),
                                     _full((1, D_GROUP)), _full((1, D_GROUP))] + [_ANY] * ns,
        out_specs=[pl.BlockSpec((t, 768), lambda i: (i, 0)), pl.BlockSpec((t, D_GROUP), lambda i: (i, 0)),
                   pl.BlockSpec((t, D_GROUP), lambda i: (i, 0)), rows[0], cols[0],
                   pl.BlockSpec((N_HEADS, t, 1), lambda i: (0, i, 0))]
                  + [_full(sh) for sh in small] + [_ANY] * ns,
        out_shape=[jax.ShapeDtypeStruct((s, 768), BF16), jax.ShapeDtypeStruct((s, D_GROUP), F32),
                   jax.ShapeDtypeStruct((s, D_GROUP), F32), rows[1], cols[1],
                   jax.ShapeDtypeStruct((N_HEADS, s, 1), F32)]
                  + [jax.ShapeDtypeStruct(sh, F32) for sh in small]
                  + [jax.ShapeDtypeStruct(a.shape, a.dtype) for a in sums],
        scratch_shapes=[pltpu.VMEM((HALO + t, D_GROUP), F32), pltpu.VMEM((HALO + t, D_GROUP), F32),
                        pltpu.VMEM((7, HALO + t - 8, D_GROUP), F32)] + (_owner_sems(ns) if ns else []),
        compiler_params=_params(("arbitrary",), V7X_VMEM_LIMIT),
    )(z, z, dyn, o_t, y_abd, p["vg"], p["wt"], p["wtt"], p["bmat"], p["scw"], p["ccw"], p["lg"], p["lb"], *sums)
    return tuple(out[:13]) + (out[13:],) if ns else out


def _mixers_bwd_b(z, dza, dcb, dcd, dq_t, dk_t, dv_t, p):
    s = z.shape[0]
    t = _tile(s, MIX_ROWS)
    per = t // HALO
    n_halo = s // HALO

    def body(z_ref, dza_ref, dcb_ref, dcbn_ref, dcd_ref, dcdn_ref, dq_ref, dk_ref, dv_ref, scw_ref, ccw_ref,
             dz_ref, eb_ref, ed_ref, sh_ref):
        last = pl.program_id(0) == pl.num_programs(0) - 1
        dz_ref[:, 0:768] = dza_ref[...]
        eb_ref[0:t, :] = dcb_ref[...]
        eb_ref[t:t + HALO, :] = jnp.where(last, 0.0, dcbn_ref[...])
        du = jnp.zeros((t, D_GROUP), F32)
        for k in range(SHORT_K):
            du = du + scw_ref[k:k + 1, :] * eb_ref[pl.ds(SHORT_K - 1 - k, t), :]
        dz_ref[:, 768:1024] = (du * _cols(z_ref, 1024, 1280)).astype(BF16)
        dz_ref[:, 1024:1280] = (du * _cols(z_ref, 768, 1024)).astype(BF16)
        for n, r in enumerate((dq_ref, dk_ref, dv_ref)):
            dz_ref[:, 1280 + 256 * n:1536 + 256 * n] = r[...].reshape(D_GROUP, t).T.astype(BF16)
        ed_ref[0:t, :] = dcd_ref[...]
        ed_ref[t:t + HALO, :] = jnp.where(last, 0.0, dcdn_ref[...])
        window = _windows(ed_ref, sh_ref, t)
        dh = jnp.zeros((t, D_GROUP), F32)
        for k in range(CONF_K):
            dh = dh + ccw_ref[k:k + 1, :] * window(CONF_K - 1 - k)
        a = _cols(z_ref, 2048, 2304)
        sg = _sigmoid(_cols(z_ref, 2304, 2560))
        dz_ref[:, 2048:2304] = (dh * sg).astype(BF16)
        dz_ref[:, 2304:2560] = (dh * a * sg * (1.0 - sg)).astype(BF16)

    nxt = lambda i: (jnp.minimum((i + 1) * per, n_halo - 1), 0)
    tr = pl.BlockSpec((N_HEADS, HEAD_DIM, t), lambda i: (0, 0, i))
    return pl.pallas_call(
        body, name="mixers_bwd_b", grid=(s // t,),
        in_specs=[pl.BlockSpec((t, D_IN), lambda i: (i, 0)), pl.BlockSpec((t, 768), lambda i: (i, 0)),
                  pl.BlockSpec((t, D_GROUP), lambda i: (i, 0)), pl.BlockSpec((HALO, D_GROUP), nxt),
                  pl.BlockSpec((t, D_GROUP), lambda i: (i, 0)), pl.BlockSpec((HALO, D_GROUP), nxt),
                  tr, tr, tr, _full((8, D_GROUP)), _full((32, D_GROUP))],
        out_specs=pl.BlockSpec((t, D_IN), lambda i: (i, 0)),
        out_shape=jax.ShapeDtypeStruct((s, D_IN), BF16),
        scratch_shapes=[pltpu.VMEM((HALO + t, D_GROUP), F32), pltpu.VMEM((HALO + t, D_GROUP), F32),
                        pltpu.VMEM((7, HALO + t - 8, D_GROUP), F32)],
        compiler_params=_params(("parallel",), V7X_VMEM_LIMIT),
    )(z, dza, dcb, dcb, dcd, dcd, dq_t, dk_t, dv_t, p["scw"], p["ccw"])


def _split_bf16(v):
    hi = v.astype(BF16)
    return hi, (v - hi.astype(F32)).astype(BF16)


def _att_scores(qs, kts, carries, tri, mask):
    zs = [_dot(q, kt) for q, kt in zip(qs, kts)]
    lms, lbs, parts = [], [], []
    for z in zs:
        soft = jnp.log(1.0 + jnp.exp(-jnp.abs(z)))
        lm = -(jnp.maximum(z, 0.0) + soft)
        lbs.append(lm + z)
        if mask is not None:
            lm = jnp.where(mask, lm, 0.0)
        lms.append(lm)
        parts.append(_split_bf16(lm))
    rights = [_dot(hi, tri) + _dot(lo, tri) for hi, lo in parts]
    ws = []
    for lb, right, carry in zip(lbs, rights, carries):
        w = jnp.exp(lb + right + carry)
        ws.append(w if mask is None else jnp.where(mask, w, 0.0))
    return ws, lbs, [jnp.sum(lm, axis=1, keepdims=True) for lm in lms]


def _att_consts(i):
    j_hi = ((i + 1) * ATT_TQ - 1) // ATT_TK
    row = lax.broadcasted_iota(jnp.int32, (ATT_TQ, ATT_TK), 0) + i * ATT_TQ
    col = lax.broadcasted_iota(jnp.int32, (ATT_TQ, ATT_TK), 1) + j_hi * ATT_TK
    r_i = lax.broadcasted_iota(jnp.int32, (ATT_TK, ATT_TK), 0)
    c_i = lax.broadcasted_iota(jnp.int32, (ATT_TK, ATT_TK), 1)
    return j_hi, col < row, r_i, c_i


def _att_alive(j, carries):
    top = carries[0]
    for c in carries[1:]:
        top = jnp.maximum(top, c)
    return jnp.logical_and(j >= 0, jnp.max(top) > ATT_DEAD)


def _attn_fwd(q_r, k_t, v_t, bufs=(), parts=()):
    s = q_r.shape[1]
    nb = len(bufs)

    def body(q_ref, kt_ref, vt_ref, *rest):
        o_ref = rest[nb]
        if nb:
            _gather_in_steps(rest[nb + 1:2 * nb + 1], parts, rest[2 * nb + 1:], s // ATT_TQ)
        j_hi, mask, r_i, c_i = _att_consts(pl.program_id(0))
        tri = (r_i > c_i).astype(BF16)

        heads = range(N_HEADS)

        def tiles(j, carries, accs, mask):
            cols = pl.ds(pl.multiple_of(j * ATT_TK, ATT_TK), ATT_TK)
            ws, _, tots = _att_scores([q_ref[h] for h in heads], [kt_ref[h, :, cols] for h in heads], carries, tri, mask)
            accs = [acc + _dot_nt(vt_ref[h, :, cols], w.astype(BF16)) for h, acc, w in zip(heads, accs, ws)]
            return [c + t for c, t in zip(carries, tots)], accs

        state = tiles(j_hi, [jnp.zeros((ATT_TQ, 1), F32)] * N_HEADS, [jnp.zeros((HEAD_DIM, ATT_TQ), F32)] * N_HEADS, mask)

        def cond(c):
            return _att_alive(c[0], c[1])

        def step(c):
            return (c[0] - 1,) + tuple(tiles(c[0], c[1], c[2], None))

        _, _, accs = lax.while_loop(cond, step, (j_hi - 1,) + tuple(state))
        for h in heads:
            o_ref[h] = accs[h]

    whole = pl.BlockSpec((N_HEADS, HEAD_DIM, s), lambda i: (0, 0, 0), pipeline_mode=pl.Buffered(1))
    out = pl.pallas_call(
        body, name="attn_fwd", grid=(s // ATT_TQ,),
        in_specs=[pl.BlockSpec((N_HEADS, ATT_TQ, HEAD_DIM), lambda i: (0, i, 0)), whole, whole] + [_ANY] * nb,
        out_specs=[pl.BlockSpec((N_HEADS, HEAD_DIM, ATT_TQ), lambda i: (0, 0, i))] + [_ANY] * nb,
        out_shape=[jax.ShapeDtypeStruct((N_HEADS, HEAD_DIM, s), F32)] + [jax.ShapeDtypeStruct(b.shape, b.dtype) for b in bufs],
        input_output_aliases={3 + k: 1 + k for k in range(nb)},
        scratch_shapes=_gather_sems(parts) if nb else [],
        compiler_params=_params(("arbitrary",), V7X_VMEM_LIMIT),
    )(q_r, k_t, v_t, *bufs)
    return (out[0], out[1:]) if nb else out[0]


ATT_BWD_HEADS = 2


def _attn_bwd(q_r, q_t, k_t, v_t, do_r, do_t, dsum):
    s = q_r.shape[1]
    hps = ATT_BWD_HEADS

    def body(q_ref, qt_ref, kt_ref, vt_ref, do_ref, dot_ref, ds_ref, dq_ref, dk_ref, dv_ref):
        i = pl.program_id(1)

        @pl.when(i == 0)
        def _():
            dk_ref[...] = jnp.zeros_like(dk_ref)
            dv_ref[...] = jnp.zeros_like(dv_ref)

        j_hi, mask, r_i, c_i = _att_consts(i)
        tri_r = (r_i > c_i).astype(BF16)
        tri_ge = (r_i >= c_i).astype(BF16)

        heads = range(hps)

        def tiles(j, carries, gsums, accs, mask):
            cols = pl.ds(pl.multiple_of(j * ATT_TK, ATT_TK), ATT_TK)
            kts = [kt_ref[h, :, cols] for h in heads]
            das = [_dot(do_ref[h], vt_ref[h, :, cols]) for h in heads]
            ws, lbs, tots = _att_scores([q_ref[h] for h in heads], kts, carries, tri_r, mask)
            wbs = [w.astype(BF16) for w in ws]
            gs = [wb.astype(F32) * da for wb, da in zip(wbs, das)]
            parts = [_split_bf16(g) for g in gs]
            sfx = [_dot(hi, tri_ge) + _dot(lo, tri_ge) for hi, lo in parts]
            for h in heads:
                dv_ref[h, :, cols] += _dot(dot_ref[h], wbs[h])
            dzs = []
            for h in heads:
                left = ds_ref[h] - gsums[h] - sfx[h]
                dz = gs[h] - jnp.exp(lbs[h]) * (gs[h] + left)
                dzs.append((dz if mask is None else jnp.where(mask, dz, 0.0)).astype(BF16))
            for h in heads:
                dk_ref[h, :, cols] += _dot(qt_ref[h], dzs[h])
            accs = [accs[h] + _dot_nt(kts[h], dzs[h]) for h in heads]
            gsums = [gsums[h] + jnp.sum(gs[h], axis=1, keepdims=True) for h in heads]
            return [c + t for c, t in zip(carries, tots)], gsums, accs

        col0 = [jnp.zeros((ATT_TQ, 1), F32)] * hps
        state = tiles(j_hi, col0, col0, [jnp.zeros((HEAD_DIM, ATT_TQ), F32)] * hps, mask)

        def cond(c):
            return _att_alive(c[0], c[1])

        def step(c):
            return (c[0] - 1,) + tuple(tiles(c[0], c[1], c[2], c[3], None))

        _, _, _, accs = lax.while_loop(cond, step, (j_hi - 1,) + tuple(state))
        for h in heads:
            dq_ref[h] = accs[h] * ATT_SCALE

    whole = pl.BlockSpec((hps, HEAD_DIM, s), lambda g, i: (g, 0, 0))
    whole_in = pl.BlockSpec((hps, HEAD_DIM, s), lambda g, i: (g, 0, 0), pipeline_mode=pl.Buffered(1))
    rows = pl.BlockSpec((hps, ATT_TQ, HEAD_DIM), lambda g, i: (g, i, 0))
    cols = pl.BlockSpec((hps, HEAD_DIM, ATT_TQ), lambda g, i: (g, 0, i))
    shape = jax.ShapeDtypeStruct((N_HEADS, HEAD_DIM, s), F32)
    return pl.pallas_call(
        body, name="attn_bwd", grid=(N_HEADS // hps, s // ATT_TQ),
        in_specs=[rows, cols, whole_in, whole_in, rows, cols, pl.BlockSpec((hps, ATT_TQ, 1), lambda g, i: (g, i, 0))],
        out_specs=[cols, whole, whole],
        out_shape=[shape, shape, shape],
        compiler_params=_params(("parallel", "arbitrary"), V7X_VMEM_LIMIT),
    )(q_r, q_t, k_t, v_t, do_r, do_t, dsum)


def _out_proj(x, y_abd, o_t, gain, w, layer):
    s = x.shape[0]
    t = _tile(s, 512)

    def body(x_ref, y_ref, ot_ref, g_ref, w_ref, x1_ref, yn_ref):
        o = ot_ref[...].reshape(D_GROUP, t).T
        groups = [y_ref[:, 0:256], y_ref[:, 256:512], o * _rsqrt_mean(o), y_ref[:, 512:768]]
        g = g_ref[...]
        acc = None
        for b, yn in enumerate(groups):
            cols = slice(256 * b, 256 * (b + 1))
            yn_ref[:, cols] = yn.astype(BF16)
            part = _dot((yn * g[:, cols]).astype(BF16), w_ref[b])
            acc = part if acc is None else acc + part
        x1_ref[...] = x_ref[...] + acc

    return pl.pallas_call(
        body, name="out_proj", grid=(s // t,),
        in_specs=[pl.BlockSpec((t, D_MODEL), lambda i: (i, 0)), pl.BlockSpec((t, 768), lambda i: (i, 0)),
                  pl.BlockSpec((N_HEADS, HEAD_DIM, t), lambda i: (0, 0, i)), _full((1, D_MODEL)),
                  pl.BlockSpec((N_BLK, None, D_GROUP, D_MODEL), lambda i: (0, layer, 0, 0))],
        out_specs=[pl.BlockSpec((t, D_MODEL), lambda i: (i, 0)), pl.BlockSpec((t, D_MODEL), lambda i: (i, 0))],
        out_shape=[jax.ShapeDtypeStruct((s, D_MODEL), F32), jax.ShapeDtypeStruct((s, D_MODEL), BF16)],
        compiler_params=_params(("parallel",)),
    )(x, y_abd, o_t, gain, w)


def _out_proj_bwd(x1, g_ffn, dh, dx2, yn, gain, w, layer):
    s = dx2.shape[0]
    t = _tile(s, 512)

    def body(x_ref, gf_ref, dh_ref, dx2_ref, yn_ref, g_ref, w_ref, dx1_ref, dgf_ref, dyn_ref, dg_ref, dw_ref):
        @pl.when(pl.program_id(0) == 0)
        def _():
            dg_ref[...] = jnp.zeros_like(dg_ref)
            dw_ref[...] = jnp.zeros_like(dw_ref)
            dgf_ref[...] = jnp.zeros_like(dgf_ref)

        dx1, dgf = _rms_bwd_rows(x_ref[...], gf_ref[...], dh_ref[...], dx2_ref[...])
        dx1_ref[...] = dx1
        dgf_ref[...] += dgf
        dxb = dx1.astype(BF16)
        g = g_ref[...]
        yn = yn_ref[...].astype(F32)
        yg = (yn * g).astype(BF16)
        for b in range(N_BLK):
            cols = slice(256 * b, 256 * (b + 1))
            dyg = _dot_nt(dxb, w_ref[b])
            dw_ref[b] += _dot_tn(yg[:, cols], dxb)
            dg_ref[:, cols] += jnp.sum(dyg * yn[:, cols], axis=0, keepdims=True)
            dyn_ref[:, cols] = (dyg * g[:, cols]).astype(BF16)

    row = pl.BlockSpec((t, D_MODEL), lambda i: (i, 0))
    vec = _full((1, D_MODEL))
    return pl.pallas_call(
        body, name="out_proj_bwd", grid=(s // t,),
        in_specs=[row, vec, row, row, row, vec, pl.BlockSpec((N_BLK, None, D_GROUP, D_MODEL), lambda i: (0, layer, 0, 0))],
        out_specs=[row, vec, row, vec, _full((N_BLK, D_GROUP, D_MODEL))],
        out_shape=[jax.ShapeDtypeStruct((s, D_MODEL), F32), jax.ShapeDtypeStruct((1, D_MODEL), F32),
                   jax.ShapeDtypeStruct((s, D_MODEL), BF16), jax.ShapeDtypeStruct((1, D_MODEL), F32),
                   jax.ShapeDtypeStruct((N_BLK, D_GROUP, D_MODEL), F32)],
        compiler_params=_params(("arbitrary",), V7X_VMEM_LIMIT),
    )(x1, g_ffn, dh, dx2, yn, gain, w)


def _ffn(x, g, w_up, w_down, layer):
    s = x.shape[0]
    t = _tile(s, 1024)

    def body(x_ref, g_ref, wu_ref, wd_ref, x2_ref, p_ref, h_ref):
        @pl.when(pl.program_id(1) == 0)
        def _():
            xv = x_ref[...]
            h_ref[...] = (xv * _rsqrt_mean(xv) * g_ref[...]).astype(BF16)
            x2_ref[...] = xv

        pre = _dot(h_ref[...], wu_ref[...])
        p_ref[...] = pre.astype(BF16)
        a = jnp.maximum(pre, 0.0)
        x2_ref[...] += _dot((a * a).astype(BF16), wd_ref[...])

    wspec = pl.BlockSpec((None, None, D_MODEL, D_FF_BLK), lambda i, j: (j, layer, 0, 0))
    row = pl.BlockSpec((t, D_MODEL), lambda i, j: (i, 0))
    return pl.pallas_call(
        body, name="ffn", grid=(s // t, N_BLK),
        in_specs=[row, pl.BlockSpec((1, D_MODEL), lambda i, j: (0, 0)), wspec, wspec],
        out_specs=[row, pl.BlockSpec((t, D_FF_BLK), lambda i, j: (i, j)), row],
        out_shape=[jax.ShapeDtypeStruct((s, D_MODEL), F32), jax.ShapeDtypeStruct((s, N_BLK * D_FF_BLK), BF16),
                   jax.ShapeDtypeStruct((s, D_MODEL), BF16)],
        compiler_params=_params(("parallel", "arbitrary"), V7X_VMEM_LIMIT),
    )(x, g, w_up, w_down)


def _loss_head(x, g, target):
    s = x.shape[0]
    t = _tile(s, 512)

    def body(x_ref, g_ref, t_ref, l_ref, dx_ref, dxb_ref, dg_ref):
        @pl.when(pl.program_id(0) == 0)
        def _():
            l_ref[...] = jnp.zeros_like(l_ref)
            dg_ref[...] = jnp.zeros_like(dg_ref)

        xv = x_ref[...]
        g = g_ref[...]
        r = _rsqrt_mean(xv)
        xh = xv * r
        err = xh * g - t_ref[...]
        l_ref[...] += 0.5 * jnp.sum(jnp.mean(err * err, axis=-1, keepdims=True), axis=0, keepdims=True)
        dy = err * (1.0 / D_MODEL)
        dg_ref[...] += jnp.sum(dy * xh, axis=0, keepdims=True)
        dxh = dy * g
        dx = r * (dxh - xh * jnp.mean(dxh * xh, axis=-1, keepdims=True))
        dx_ref[...] = dx
        dxb_ref[...] = dx.astype(BF16)

    row = pl.BlockSpec((t, D_MODEL), lambda i: (i, 0))
    return pl.pallas_call(
        body, name="loss_head", grid=(s // t,),
        in_specs=[row, _full((1, D_MODEL)), row],
        out_specs=[_full((1, 128)), row, row, _full((1, D_MODEL))],
        out_shape=[jax.ShapeDtypeStruct((1, 128), F32), jax.ShapeDtypeStruct((s, D_MODEL), F32),
                   jax.ShapeDtypeStruct((s, D_MODEL), BF16), jax.ShapeDtypeStruct((1, D_MODEL), F32)],
        compiler_params=_params(("arbitrary",)),
    )(x, g, target)


def _rms_bwd_rows(xv, g, dh, dres):
    r = _rsqrt_mean(xv)
    xh = xv * r
    dxh = dh * g
    dx = dres + r * (dxh - xh * jnp.mean(dxh * xh, axis=-1, keepdims=True))
    return dx, jnp.sum(dh * xh, axis=0, keepdims=True)


def _ffn_bwd(dxb, p, w_up, w_down, layer, sums=()):
    s = dxb.shape[0]
    t = _tile(s, 1024)
    ns = len(sums)

    def body(dx_ref, p_ref, wu_ref, wd_ref, *rest):
        dp_ref, dh_ref = rest[ns:ns + 2]
        if ns:
            i, j = pl.program_id(0), pl.program_id(1)
            _owners_in_steps(rest[:ns], rest[ns + 2:2 * ns + 2], rest[2 * ns + 2:],
                             jnp.logical_and(i == 0, j == 0), jnp.logical_and(i == s // t - 1, j == N_BLK - 1))
        da = _dot_nt(dx_ref[...], wd_ref[...])
        a = jnp.maximum(p_ref[...].astype(F32), 0.0)
        dp = (da * (2.0 * a)).astype(BF16)
        dp_ref[...] = dp
        dh = _dot_nt(dp, wu_ref[...])

        @pl.when(pl.program_id(1) == 0)
        def _():
            dh_ref[...] = dh

        @pl.when(pl.program_id(1) != 0)
        def _():
            dh_ref[...] += dh

    wspec = pl.BlockSpec((None, None, D_MODEL, D_FF_BLK), lambda i, j: (j, layer, 0, 0))
    row = pl.BlockSpec((t, D_MODEL), lambda i, j: (i, 0))
    blk = pl.BlockSpec((t, D_FF_BLK), lambda i, j: (i, j))
    out = pl.pallas_call(
        body, name="ffn_bwd", grid=(s // t, N_BLK),
        in_specs=[row, blk, wspec, wspec] + [_ANY] * ns, out_specs=[blk, row] + [_ANY] * ns,
        out_shape=[jax.ShapeDtypeStruct((s, N_BLK * D_FF_BLK), BF16), jax.ShapeDtypeStruct((s, D_MODEL), F32)]
                  + [jax.ShapeDtypeStruct(a.shape, a.dtype) for a in sums],
        scratch_shapes=_owner_sems(ns) if ns else [],
        compiler_params=_params(("arbitrary" if ns else "parallel", "arbitrary"), V7X_VMEM_LIMIT),
    )(dxb, p, w_up, w_down, *sums)
    return (out[0], out[1], out[2:]) if ns else out


def _ffn_wgrad(hb, p, dp, dxb):
    s = hb.shape[0]
    t = _tile(s, 1024)

    def body(h_ref, p_ref, dp_ref, dx_ref, du_ref, dd_ref):
        @pl.when(pl.program_id(1) == 0)
        def _():
            du_ref[...] = jnp.zeros_like(du_ref)
            dd_ref[...] = jnp.zeros_like(dd_ref)

        a = jnp.maximum(p_ref[...].astype(F32), 0.0)
        du_ref[...] += _dot_tn(h_ref[...], dp_ref[...])
        dd_ref[...] += _dot_tn((a * a).astype(BF16), dx_ref[...])

    row = pl.BlockSpec((t, D_MODEL), lambda j, i: (i, 0))
    blk = pl.BlockSpec((t, D_FF_BLK), lambda j, i: (i, j))
    out = pl.BlockSpec((None, D_MODEL, D_FF_BLK), lambda j, i: (j, 0, 0))
    shape = jax.ShapeDtypeStruct((N_BLK, D_MODEL, D_FF_BLK), F32)
    return pl.pallas_call(
        body, name="ffn_wgrad", grid=(N_BLK, s // t),
        in_specs=[row, blk, blk, row], out_specs=[out, out], out_shape=[shape, shape],
        compiler_params=_params(("parallel", "arbitrary"), V7X_VMEM_LIMIT),
    )(hb, p, dp, dxb)


def _in_proj_bwd(x, g, dx1, dz, w, layer):
    s = x.shape[0]
    t = _tile(s, 512)

    def body(x_ref, g_ref, dx1_ref, dz_ref, w_ref, dx0_ref, dxb_ref, dg_ref, wide_ref):
        _pair_blocks(w_ref, wide_ref)

        @pl.when(pl.program_id(0) == 0)
        def _():
            dg_ref[...] = jnp.zeros_like(dg_ref)

        dh = _dot_nt(dz_ref[:, 0:2 * W_IN_BLK], wide_ref[0])
        for n in range(1, N_BLK // 2):
            dh = dh + _dot_nt(dz_ref[:, 2 * n * W_IN_BLK:2 * (n + 1) * W_IN_BLK], wide_ref[n])
        dx, dg = _rms_bwd_rows(x_ref[...], g_ref[...], dh, dx1_ref[...])
        dx0_ref[...] = dx
        dxb_ref[...] = dx.astype(BF16)
        dg_ref[...] += dg

    row = pl.BlockSpec((t, D_MODEL), lambda i: (i, 0))
    return pl.pallas_call(
        body, name="in_proj_bwd", grid=(s // t,),
        in_specs=[row, _full((1, D_MODEL)), row, pl.BlockSpec((t, D_IN), lambda i: (i, 0)),
                  pl.BlockSpec((N_BLK, None, D_MODEL, W_IN_BLK), lambda i: (0, layer, 0, 0))],
        out_specs=[row, row, _full((1, D_MODEL))],
        out_shape=[jax.ShapeDtypeStruct((s, D_MODEL), F32), jax.ShapeDtypeStruct((s, D_MODEL), BF16),
                   jax.ShapeDtypeStruct((1, D_MODEL), F32)],
        scratch_shapes=[pltpu.VMEM((N_BLK // 2, D_MODEL, 2 * W_IN_BLK), BF16)],
        compiler_params=_params(("arbitrary",), V7X_VMEM_LIMIT),
    )(x, g, dx1, dz, w)


def _in_proj_wgrad(hb, dz):
    s = hb.shape[0]
    t = _tile(s, 512)

    def body(h_ref, dz_ref, dw_ref, wide_ref):
        @pl.when(pl.program_id(0) == 0)
        def _():
            wide_ref[...] = jnp.zeros_like(wide_ref)

        h = h_ref[...]
        for n in range(N_BLK // 2):
            wide_ref[n] += _dot_tn(h, dz_ref[:, 2 * n * W_IN_BLK:2 * (n + 1) * W_IN_BLK])

        @pl.when(pl.program_id(0) == s // t - 1)
        def _():
            for b in range(N_BLK):
                dw_ref[b] = wide_ref[b // 2, :, (b % 2) * W_IN_BLK:(b % 2 + 1) * W_IN_BLK]

    return pl.pallas_call(
        body, name="in_proj_wgrad", grid=(s // t,),
        in_specs=[pl.BlockSpec((t, D_MODEL), lambda i: (i, 0)), pl.BlockSpec((t, D_IN), lambda i: (i, 0))],
        out_specs=_full((N_BLK, D_MODEL, W_IN_BLK)),
        out_shape=jax.ShapeDtypeStruct((N_BLK, D_MODEL, W_IN_BLK), F32),
        scratch_shapes=[pltpu.VMEM((N_BLK // 2, D_MODEL, 2 * W_IN_BLK), F32)],
        compiler_params=_params(("arbitrary",), V7X_VMEM_LIMIT),
    )(hb, dz)


def _layer_params(small, layer):
    tril = jnp.tril(jnp.ones((CHUNK, CHUNK), bool))
    ws = jnp.where(tril, small["gmlp_w_s"][layer], 0.0)
    bmat = jnp.repeat(small["gmlp_b_s"][layer].T, HEAD_DIM, axis=1)
    scw = jnp.zeros((8, D_GROUP), F32).at[:SHORT_K].set(small["short_conv_w"][layer])
    ccw = jnp.zeros((32, D_GROUP), F32).at[:CONF_K].set(small["conf_conv_w"][layer])
    return dict(vg=small["gmlp_v_g"][layer][None], wt=ws.astype(BF16), wtt=jnp.swapaxes(ws, 1, 2).astype(BF16),
                bmat=bmat, scw=scw, ccw=ccw, lg=small["conf_ln_g"][layer][None], lb=small["conf_ln_b"][layer][None])


def _local_step(x, target, big, small, gather_pending=False, core=None):
    saved = []
    for l in range(DEPTH):
        p = _layer_params(small, l)
        z, hb, q_r, q_t, k_t, v_t = _in_proj(x, small["norm_mix_g"][l][None], big["w_in"], l)
        if gather_pending and l == 0:
            late = ("w_out", "w_up", "w_down")
            y_abd, filled = _mixers_fwd(z, p, [big[k] for k in late], [(n, 0) for n in range(len(late))])
            big = {**big, **dict(zip(late, filled))}
            o_t, filled = _attn_fwd(q_r, k_t, v_t, [big[k] for k in _BIG], [(n, 1) for n in range(len(_BIG))])
            big = dict(zip(_BIG, filled))
        else:
            y_abd = _mixers_fwd(z, p)
            o_t = _attn_fwd(q_r, k_t, v_t)
        x1, yn = _out_proj(x, y_abd, o_t, small["mix_out_g"][l][None], big["w_out"], l)
        x2, pre, h2b = _ffn(x1, small["norm_ffn_g"][l][None], big["w_up"], big["w_down"], l)
        saved.append(dict(p=p, x0=x, z=z, hb=hb, q_r=q_r, q_t=q_t, k_t=k_t, v_t=v_t, o_t=o_t, x1=x1, yn=yn, pre=pre,
                          h2b=h2b, y_abd=y_abd))
        x = x2

    loss, dx, dxb, d_final = _loss_head(x, small["final_norm_g"][None], target)

    g = {k: [None] * DEPTH for k in ("w_in", "w_out", "w_up", "w_down", "norm_mix_g", "gmlp_v_g", "gmlp_w_s", "gmlp_b_s",
                                     "short_conv_w", "conf_conv_w", "conf_ln_g", "conf_ln_b", "mix_out_g", "norm_ffn_g")}
    tril = jnp.tril(jnp.ones((CHUNK, CHUNK), bool))
    early = {}
    for l in reversed(range(DEPTH)):
        sv = saved[l]
        p = sv["p"]
        riding = []
        if core is not None and l == 0:
            riding = [(k, 1) for k in _BIG]
        sums = _chip_sums(core, [g[k][n] for k, n in riding])
        dpre, dh, *got = _ffn_bwd(dxb, sv["pre"], big["w_up"], big["w_down"], l, sums)
        early.update(zip(riding, zip(sums, *got)))
        g["w_up"][l], g["w_down"][l] = _ffn_wgrad(sv["h2b"], sv["pre"], dpre, dxb)
        dx1, g["norm_ffn_g"][l], dyn, g["mix_out_g"][l], g["w_out"][l] = _out_proj_bwd(
            sv["x1"], small["norm_ffn_g"][l][None], dh, dx, sv["yn"], small["mix_out_g"][l][None], big["w_out"], l)
        if core is not None and l == 0:
            riding = [("w_up", 0), ("w_down", 0)]
        sums = _chip_sums(core, [g[k][n] for k, n in riding])
        (dza, dcb, dcd, do_r, do_t, dsum, dvg, dws, dbm, dscw, dccw, dlg, dlb, *got) = _mixers_bwd_a(
            sv["z"], dyn, sv["o_t"], sv["y_abd"], p, sums)
        early.update(zip(riding, zip(sums, *got)))
        dq_t, dk_t, dv_t = _attn_bwd(sv["q_r"], sv["q_t"], sv["k_t"], sv["v_t"], do_r, do_t, dsum)
        dz = _mixers_bwd_b(sv["z"], dza, dcb, dcd, dq_t, dk_t, dv_t, p)
        dx, dxb, g["norm_mix_g"][l] = _in_proj_bwd(sv["x0"], small["norm_mix_g"][l][None], dx1, dz, big["w_in"], l)
        g["w_in"][l] = _in_proj_wgrad(sv["hb"], dz)
        g["gmlp_v_g"][l] = dvg[0]
        g["gmlp_w_s"][l] = jnp.where(tril, dws, 0.0)
        g["gmlp_b_s"][l] = dbm.reshape(CHUNK, N_HEADS, HEAD_DIM).sum(-1).T
        g["short_conv_w"][l] = dscw[:SHORT_K]
        g["conf_conv_w"][l] = dccw[:CONF_K]
        g["conf_ln_g"][l] = dlg[0]
        g["conf_ln_b"][l] = dlb[0]
        g["norm_mix_g"][l] = g["norm_mix_g"][l][0]
        g["mix_out_g"][l] = g["mix_out_g"][l][0]
        g["norm_ffn_g"][l] = g["norm_ffn_g"][l][0]
    grads = {k: v if k in ("w_in", "w_out", "w_up", "w_down") else jnp.stack(v) for k, v in g.items()}
    grads["final_norm_g"] = d_final[0]
    return loss, dx, grads, early


_ANY = pl.BlockSpec(memory_space=pl.ANY)


def _mesh_place():
    x, y, c = lax.axis_index("x"), lax.axis_index("y"), lax.axis_index("c")
    chips = [(1 - x, y), (x, 1 - y), (1 - x, 1 - y)]
    return x, y, c, 2 * x + y, chips


def _gather_stages(bufs, parts, sems):
    ici_send, ici_recv, d2d_send, d2d_recv = sems
    x, y, c, me, chips = _mesh_place()
    blk = [2 * chip[0] + chip[1] for chip in chips]
    pairs = [(p, r) for p in range(len(parts)) for r in range(3)]

    def rows(p, block, half_of):
        k, layer = parts[p]
        half = bufs[k].shape[2] // 2
        return bufs[k].at[block, layer, pl.ds(half_of * half, half), :]

    def ici(p, r, block):
        return pltpu.make_async_remote_copy(
            src_ref=rows(p, me, c), dst_ref=rows(p, block, c), send_sem=ici_send.at[3 * p + r],
            recv_sem=ici_recv.at[3 * p + r], device_id=(chips[r][0], chips[r][1], c), device_id_type=MESH)

    def d2d(p, r, half_of):
        part = rows(p, blk[r], half_of)
        return pltpu.make_async_remote_copy(
            src_ref=part, dst_ref=part, send_sem=d2d_send.at[3 * p + r], recv_sem=d2d_recv.at[3 * p + r],
            device_id=(x, y, 1 - c), device_id_type=MESH)

    def start():
        for p, r in pairs:
            ici(p, r, me).start()

    def forward(p):
        for r in range(3):
            ici(p, r, blk[r]).wait_recv()
            d2d(p, r, c).start()

    def finish():
        for p, r in pairs:
            d2d(p, r, 1 - c).wait_recv()
        for p, r in pairs:
            ici(p, r, me).wait_send()
            d2d(p, r, c).wait_send()

    return start, forward, finish


def _gather_sems(parts):
    return [pltpu.SemaphoreType.DMA((3 * len(parts),)) for _ in range(4)]


def _gather_in_steps(bufs, parts, sems, n_steps):
    start, forward, finish = _gather_stages(bufs, parts, sems)
    i = pl.program_id(0)
    pl.when(i == 0)(start)
    for p in range(len(parts)):
        pl.when(i == n_steps * (2 * p + 3) // (2 * len(parts) + 2))(lambda p=p: forward(p))
    pl.when(i == n_steps - 1)(finish)


def _gather_first(bufs, parts, whole):
    n, m = len(bufs), len(whole)

    def body(*refs):
        whole_in, buf_out, whole_out = refs[n:n + m], refs[n + m:2 * n + m], refs[2 * n + m:2 * (n + m)]
        sems = refs[2 * (n + m):]
        send_sems, recv_sems, local_sems = sems[4:]
        x, y, c, me, chips = _mesh_place()
        start, forward, finish = _gather_stages(buf_out, parts, sems[:4])

        def push(k, r, block):
            return pltpu.make_async_remote_copy(
                src_ref=whole_in[k], dst_ref=whole_out[k].at[block], send_sem=send_sems.at[3 * k + r],
                recv_sem=recv_sems.at[3 * k + r], device_id=(chips[r][0], chips[r][1], c), device_id_type=MESH)

        local = [pltpu.make_async_copy(whole_in[k], whole_out[k].at[me], local_sems.at[k]) for k in range(m)]
        for cp in local:
            cp.start()
        start()
        for k in range(m):
            for r in range(3):
                push(k, r, me).start()
        for p in range(len(parts)):
            forward(p)
        for k in range(m):
            for r, chip in enumerate(chips):
                push(k, r, 2 * chip[0] + chip[1]).wait_recv()
        for k in range(m):
            for r in range(3):
                push(k, r, me).wait_send()
        finish()
        for cp in local:
            cp.wait()

    return pl.pallas_call(
        body, name="gather_first",
        in_specs=[_ANY] * (n + m), out_specs=[_ANY] * (n + m),
        out_shape=[jax.ShapeDtypeStruct(b.shape, b.dtype) for b in bufs]
                  + [jax.ShapeDtypeStruct((N_BLK,) + b.shape, b.dtype) for b in whole],
        input_output_aliases={k: k for k in range(n)},
        scratch_shapes=_gather_sems(parts) + [pltpu.SemaphoreType.DMA((3 * m,)), pltpu.SemaphoreType.DMA((3 * m,)),
                                              pltpu.SemaphoreType.DMA((m,))],
    )(*bufs, *whole)


def _swap_halves(gs):
    n = len(gs)

    def body(*refs):
        ins, outs, (send_sems, recv_sems) = refs[:n], refs[n:2 * n], refs[2 * n:]
        x, y, c, _, _ = _mesh_place()
        cps = []
        for k in range(n):
            half = ins[k].shape[1] // 2
            cps.append(pltpu.make_async_remote_copy(
                src_ref=ins[k].at[:, pl.ds((1 - c) * half, half), :], dst_ref=outs[k],
                send_sem=send_sems.at[k], recv_sem=recv_sems.at[k], device_id=(x, y, 1 - c), device_id_type=MESH))
        for cp in cps:
            cp.start()
        for cp in cps:
            cp.wait()

    return pl.pallas_call(
        body, name="swap_halves", in_specs=[_ANY] * n, out_specs=[_ANY] * n,
        out_shape=[jax.ShapeDtypeStruct((g.shape[0], g.shape[1] // 2, g.shape[2]), F32) for g in gs],
        scratch_shapes=[pltpu.SemaphoreType.DMA((n,)), pltpu.SemaphoreType.DMA((n,))],
    )(*gs)


def _owner_stages(ins, outs, sems):
    send_sems, recv_sems = sems
    x, y, c, me, chips = _mesh_place()
    pairs = [(k, r) for k in range(len(ins)) for r in range(3)]

    def remote(k, r, src_block, dst_block):
        return pltpu.make_async_remote_copy(
            src_ref=ins[k].at[src_block], dst_ref=outs[k].at[dst_block], send_sem=send_sems.at[3 * k + r],
            recv_sem=recv_sems.at[3 * k + r], device_id=(chips[r][0], chips[r][1], c), device_id_type=MESH)

    def start():
        for k, r in pairs:
            remote(k, r, 2 * chips[r][0] + chips[r][1], me).start()

    def finish():
        for k, r in pairs:
            remote(k, r, me, 2 * chips[r][0] + chips[r][1]).wait_recv()
        for k, r in pairs:
            remote(k, r, 2 * chips[r][0] + chips[r][1], me).wait_send()

    return start, finish


def _owner_sems(n):
    return [pltpu.SemaphoreType.DMA((3 * n,)), pltpu.SemaphoreType.DMA((3 * n,))]


def _owners_in_steps(ins, outs, sems, first, last):
    start, finish = _owner_stages(ins, outs, sems)
    pl.when(first)(start)
    pl.when(last)(finish)


def _send_to_owners(sums):
    n = len(sums)

    def body(*refs):
        start, finish = _owner_stages(refs[:n], refs[n:2 * n], refs[2 * n:])
        start()
        finish()

    return pl.pallas_call(
        body, name="send_to_owners", in_specs=[_ANY] * n, out_specs=[_ANY] * n,
        out_shape=[jax.ShapeDtypeStruct(s.shape, s.dtype) for s in sums],
        scratch_shapes=_owner_sems(n),
    )(*sums)


def _swap_reduced(fs):
    n = len(fs)

    def body(*refs):
        ins, outs, (send_sems, recv_sems) = refs[:n], refs[n:2 * n], refs[2 * n:]
        x, y, c, _, _ = _mesh_place()
        cps = [pltpu.make_async_remote_copy(src_ref=ins[k], dst_ref=outs[k], send_sem=send_sems.at[k],
                                            recv_sem=recv_sems.at[k], device_id=(x, y, 1 - c), device_id_type=MESH)
               for k in range(n)]
        for cp in cps:
            cp.start()
        for cp in cps:
            cp.wait()

    return pl.pallas_call(
        body, name="swap_reduced", in_specs=[_ANY] * n, out_specs=[_ANY] * n,
        out_shape=[jax.ShapeDtypeStruct(f.shape, F32) for f in fs],
        scratch_shapes=[pltpu.SemaphoreType.DMA((n,)), pltpu.SemaphoreType.DMA((n,))],
    )(*fs)


def _row_tile(rows):
    return min(rows, 256)


def _chip_sums(core, grads):
    if not grads:
        return []
    return [_add_pairs(core, a, b) for a, b in zip(grads, _swap_halves(grads))]


def _add_pairs(core, g, other):
    n, half, cols = other.shape
    t = _row_tile(half)
    per_half = half // t

    def body(c_ref, a_ref, b_ref, o_ref):
        o_ref[...] = (a_ref[...] + b_ref[...]).astype(BF16)

    spec = pl.BlockSpec((None, t, cols), lambda i, j, c_ref: (i, j, 0))
    return pl.pallas_call(
        body, name="add_pairs",
        grid_spec=pltpu.PrefetchScalarGridSpec(
            num_scalar_prefetch=1, grid=(n, per_half),
            in_specs=[pl.BlockSpec((None, t, cols), lambda i, j, c_ref: (i, c_ref[0] * per_half + j, 0)), spec],
            out_specs=spec),
        out_shape=jax.ShapeDtypeStruct(other.shape, BF16), compiler_params=_params(("parallel", "parallel")),
    )(core, g, other)


def _add_chips(me, s1, r2):
    _, r, cols = r2.shape
    t = _row_tile(r)

    def body(me_ref, s_ref, r_ref, o_ref):
        own = s_ref[...].astype(F32)
        parts = [jnp.where(me_ref[0] == k, own, r_ref[k].astype(F32)) for k in range(N_BLK)]
        o_ref[...] = ((parts[0] + parts[1]) + parts[2]) + parts[3]

    return pl.pallas_call(
        body, name="add_chips",
        grid_spec=pltpu.PrefetchScalarGridSpec(
            num_scalar_prefetch=1, grid=(r // t,),
            in_specs=[pl.BlockSpec((None, t, cols), lambda i, me_ref: (me_ref[0], i, 0)),
                      pl.BlockSpec((N_BLK, t, cols), lambda i, me_ref: (0, i, 0))],
            out_specs=pl.BlockSpec((t, cols), lambda i, me_ref: (i, 0))),
        out_shape=jax.ShapeDtypeStruct((r, cols), F32), compiler_params=_params(("parallel",)),
    )(me, s1, r2)


def _adamw(core, mine, other, w, m, v, layer, earlier=None):
    half, cols = mine.shape
    t = _row_tile(half)
    per_half = half // t
    c1 = 1.0 - ADAM_B1 ** ADAM_STEP
    c2 = 1.0 - ADAM_B2 ** ADAM_STEP

    def body(c_ref, a_ref, b_ref, w_ref, m_ref, v_ref, *rest):
        g_ref, d_ref, mo_ref, vo_ref = rest[-4:]
        gv = jnp.where(pl.program_id(0) // per_half == c_ref[0], a_ref[...], b_ref[...])
        g_ref[...] = gv
        m_new = ADAM_B1 * m_ref[...] + (1.0 - ADAM_B1) * gv
        v_new = ADAM_B2 * v_ref[...] + (1.0 - ADAM_B2) * (gv * gv)
        mo_ref[...] = m_new
        vo_ref[...] = v_new
        d_ref[...] = -ADAM_LR * ((m_new / c1) / (jnp.sqrt(v_new / c2) + ADAM_EPS) + ADAM_WD * w_ref[...])

    part = pl.BlockSpec((t, cols), lambda i, c_ref: (i % per_half, 0))
    spec = pl.BlockSpec((None, t, cols), lambda i, c_ref: (layer, i, 0))
    kept = [] if earlier is None else list(earlier)
    return pl.pallas_call(
        body, name="adamw",
        grid_spec=pltpu.PrefetchScalarGridSpec(
            num_scalar_prefetch=1, grid=(2 * per_half,),
            in_specs=[part, part, spec, spec, spec] + [_ANY] * len(kept), out_specs=[spec] * 4),
        out_shape=[jax.ShapeDtypeStruct(w.shape, F32)] * 4,
        input_output_aliases={6 + k: k for k in range(len(kept))},
        compiler_params=_params(("parallel",)),
    )(core, mine, other, w, m, v, *kept)


_REPLICATED = ("norm_mix_g", "gmlp_v_g", "gmlp_w_s", "gmlp_b_s", "conf_ln_g", "conf_ln_b", "mix_out_g", "norm_ffn_g",
               "final_norm_g")
_REP_SHAPES = {"norm_mix_g": (DEPTH, D_MODEL), "gmlp_v_g": (DEPTH, D_GROUP), "gmlp_w_s": (DEPTH, N_HEADS, CHUNK, CHUNK),
               "gmlp_b_s": (DEPTH, N_HEADS, CHUNK), "conf_ln_g": (DEPTH, D_GROUP), "conf_ln_b": (DEPTH, D_GROUP),
               "mix_out_g": (DEPTH, D_MODEL), "norm_ffn_g": (DEPTH, D_MODEL), "final_norm_g": (D_MODEL,)}
_BIG = ("w_in", "w_out", "w_up", "w_down")
_CONV_ROWS = 8
_REP_ROWS = 144
_SMALL_ROWS = 160
_CH_BLK = D_GROUP // N_BLK


def _pad_rows(flat, rows):
    pad = rows * D_MODEL - flat.shape[-1]
    flat = jnp.pad(flat, [(0, 0)] * (flat.ndim - 1) + [(0, pad)])
    return flat.reshape(flat.shape[:-1] + (rows, D_MODEL))


def _pack_small(scw, ccw, rep):
    lead = scw.shape[:-3]
    conv = jnp.concatenate([scw.reshape(lead + (-1,)), ccw.reshape(lead + (-1,))], axis=-1)
    flat = jnp.concatenate([rep[k].reshape(-1) for k in _REPLICATED])
    flat = jnp.broadcast_to(flat, lead + flat.shape)
    parts = [_pad_rows(conv, _CONV_ROWS), _pad_rows(flat, _REP_ROWS),
             jnp.zeros(lead + (_SMALL_ROWS - _CONV_ROWS - _REP_ROWS, D_MODEL), F32)]
    return jnp.concatenate(parts, axis=-2)


def _unpack_small(pk):
    out = {}
    conv = pk[:_CONV_ROWS].reshape(-1)
    n_s = DEPTH * SHORT_K * _CH_BLK
    out["short_conv_w"] = conv[:n_s].reshape(DEPTH, SHORT_K, _CH_BLK)
    out["conf_conv_w"] = conv[n_s:n_s + DEPTH * CONF_K * _CH_BLK].reshape(DEPTH, CONF_K, _CH_BLK)
    row = _CONV_ROWS
    flat = pk[row:row + _REP_ROWS].reshape(-1)
    at = 0
    for k in _REPLICATED:
        n = math.prod(_REP_SHAPES[k])
        out[k] = flat[at:at + n].reshape(_REP_SHAPES[k])
        at += n
    return out


def _conv_blocks(w):
    d, k, _ = w.shape
    return w.reshape(d, k, N_BLK, _CH_BLK).transpose(2, 0, 1, 3)


_WEIGHTS = ("norm_mix_g", "w_in", "gmlp_v_g", "gmlp_w_s", "gmlp_b_s", "short_conv_w", "conf_conv_w", "conf_ln_g",
            "conf_ln_b", "mix_out_g", "w_out", "norm_ffn_g", "w_up", "w_down", "final_norm_g")


def kernel(x, norm_mix_g, w_in, gmlp_v_g, gmlp_w_s, gmlp_b_s, short_conv_w, conf_conv_w, conf_ln_g, conf_ln_b, mix_out_g, w_out, norm_ffn_g, w_up, w_down, final_norm_g, loss_target, m_norm_mix_g, m_w_in, m_gmlp_v_g, m_gmlp_w_s, m_gmlp_b_s, m_short_conv_w, m_conf_conv_w, m_conf_ln_g, m_conf_ln_b, m_mix_out_g, m_w_out, m_norm_ffn_g, m_w_up, m_w_down, m_final_norm_g, v_norm_mix_g, v_w_in, v_gmlp_v_g, v_gmlp_w_s, v_gmlp_b_s, v_short_conv_w, v_conf_conv_w, v_conf_ln_g, v_conf_ln_b, v_mix_out_g, v_w_out, v_norm_ffn_g, v_w_up, v_w_down, v_final_norm_g):
    w = dict(norm_mix_g=norm_mix_g, w_in=w_in, gmlp_v_g=gmlp_v_g, gmlp_w_s=gmlp_w_s, gmlp_b_s=gmlp_b_s,
             short_conv_w=short_conv_w, conf_conv_w=conf_conv_w, conf_ln_g=conf_ln_g, conf_ln_b=conf_ln_b,
             mix_out_g=mix_out_g, w_out=w_out, norm_ffn_g=norm_ffn_g, w_up=w_up, w_down=w_down, final_norm_g=final_norm_g)
    m = dict(norm_mix_g=m_norm_mix_g, w_in=m_w_in, gmlp_v_g=m_gmlp_v_g, gmlp_w_s=m_gmlp_w_s, gmlp_b_s=m_gmlp_b_s,
             short_conv_w=m_short_conv_w, conf_conv_w=m_conf_conv_w, conf_ln_g=m_conf_ln_g, conf_ln_b=m_conf_ln_b,
             mix_out_g=m_mix_out_g, w_out=m_w_out, norm_ffn_g=m_norm_ffn_g, w_up=m_w_up, w_down=m_w_down,
             final_norm_g=m_final_norm_g)
    v = dict(norm_mix_g=v_norm_mix_g, w_in=v_w_in, gmlp_v_g=v_gmlp_v_g, gmlp_w_s=v_gmlp_w_s, gmlp_b_s=v_gmlp_b_s,
             short_conv_w=v_short_conv_w, conf_conv_w=v_conf_conv_w, conf_ln_g=v_conf_ln_g, conf_ln_b=v_conf_ln_b,
             mix_out_g=v_mix_out_g, w_out=v_w_out, norm_ffn_g=v_norm_ffn_g, w_up=v_w_up, w_down=v_w_down,
             final_norm_g=v_final_norm_g)
    core = lax.axis_index("c").astype(jnp.int32).reshape(1)
    me = (2 * lax.axis_index("x") + lax.axis_index("y")).astype(jnp.int32).reshape(1)

    conv_mine = _pad_rows(jnp.concatenate([short_conv_w.reshape(-1), conf_conv_w.reshape(-1)]), _CONV_ROWS)
    big = {k: _cast_into_slot(w[k], me, "cast_" + k) for k in _BIG}
    big["w_in"], conv_all = _gather_first([big["w_in"]], [(0, 0)], [conv_mine])
    conv_all = conv_all.reshape(N_BLK, -1)
    n_s = DEPTH * SHORT_K * _CH_BLK
    scw_all = conv_all[:, :n_s].reshape(N_BLK, DEPTH, SHORT_K, _CH_BLK)
    ccw_all = conv_all[:, n_s:n_s + DEPTH * CONF_K * _CH_BLK].reshape(N_BLK, DEPTH, CONF_K, _CH_BLK)
    small = {k: w[k] for k in _REPLICATED}
    small["short_conv_w"] = scw_all.transpose(1, 2, 0, 3).reshape(DEPTH, SHORT_K, D_GROUP)
    small["conf_conv_w"] = ccw_all.transpose(1, 2, 0, 3).reshape(DEPTH, CONF_K, D_GROUP)

    loss, dx, g, early = _local_step(x[0], loss_target[0], big, small, gather_pending=True, core=core)

    where = [(k, l) for k in _BIG for l in range(DEPTH)]
    late = [kl for kl in where if kl not in early]
    sums = _chip_sums(core, [g[k][l] for k, l in late]
                      + [_pack_small(_conv_blocks(g["short_conv_w"]), _conv_blocks(g["conf_conv_w"]), g)])
    sent = {**early, **dict(zip(late + ["small"], zip(sums, _send_to_owners(sums))))}
    mine = [_add_chips(me, *sent[kl]) for kl in where + ["small"]]
    other = _swap_reduced(mine)

    done = {}
    for n, (k, l) in enumerate(where):
        done[k] = _adamw(core, mine[n], other[n], w[k], m[k], v[k], l, done.get(k))
    small_own = [_pack_small(t["short_conv_w"], t["conf_conv_w"], t)[None] for t in (w, m, v)]
    small_done = [_unpack_small(a[0]) for a in _adamw(core, mine[-1], other[-1], *small_own, 0)]

    outs = [lax.psum(loss[0, 0], ("x", "y", "c")), dx[None]]
    for kind in range(4):
        outs += [done[k][kind] if k in _BIG else small_done[kind][k] for k in _WEIGHTS]
    return tuple(outs)
```

```python
import math

import jax
import jax.numpy as jnp
from jax import lax
from jax.experimental import pallas as pl
from jax.experimental.pallas import tpu as pltpu

F32 = jnp.float32
BF16 = jnp.bfloat16

D_MODEL = 1024
D_GROUP = 256
N_HEADS = 4
HEAD_DIM = 64
CHUNK = 128
D_IN = 2560
N_BLK = 4
W_IN_BLK = D_IN // N_BLK
D_FF_BLK = 1024
DEPTH = 2
EPS = 1e-6
HALO = 32
MIX_ROWS = 512
SHORT_K = 3
CONF_K = 31
ATT_TQ = 256
ATT_TK = 256
ATT_SCALE = 0.125
ATT_DEAD = -104.0
V7X_VMEM_LIMIT = 56 * 1024 * 1024

ADAM_LR, ADAM_B1, ADAM_B2, ADAM_EPS, ADAM_WD, ADAM_STEP = 0.001, 0.9, 0.999, 1e-08, 0.01, 10

MESH = pl.DeviceIdType.MESH


def _params(sem, vmem=None):
    return pltpu.CompilerParams(dimension_semantics=sem, vmem_limit_bytes=vmem)


def _tile(s, t):
    return min(s, t)


def _rsqrt_mean(v):
    return lax.rsqrt(jnp.mean(v * v, axis=-1, keepdims=True) + EPS)


def _sigmoid(v):
    return 1.0 / (1.0 + jnp.exp(-v))


_GELU_C = math.sqrt(2.0 / math.pi)


def _gelu_tanh(v):
    return jnp.tanh(_GELU_C * (v + 0.044715 * (v * v * v)))


def _gelu(v, t):
    return v * (0.5 * (1.0 + t))


def _gelu_grad(v, t):
    return 0.5 * (1.0 + t) + v * (0.5 * (1.0 - t * t) * _GELU_C * (1.0 + 3.0 * 0.044715 * (v * v)))


def _dot(a, b):
    return jnp.dot(a, b, preferred_element_type=F32)


def _dot_nt(a, b):
    return lax.dot_general(a, b, (((1,), (1,)), ((), ())), preferred_element_type=F32)


def _dot_tn(a, b):
    return lax.dot_general(a, b, (((0,), (0,)), ((), ())), preferred_element_type=F32)


def _cast_into_slot(w, me, name):
    n, r, c = w.shape
    tr = _tile(r, 256)

    def body(me_ref, w_ref, o_ref):
        o_ref[...] = w_ref[...].astype(BF16)

    return pl.pallas_call(
        body, name=name,
        grid_spec=pltpu.PrefetchScalarGridSpec(
            num_scalar_prefetch=1, grid=(n, r // tr),
            in_specs=[pl.BlockSpec((None, tr, c), lambda a, b, me_ref: (a, b, 0))],
            out_specs=pl.BlockSpec((None, None, tr, c), lambda a, b, me_ref: (me_ref[0], a, b, 0))),
        out_shape=jax.ShapeDtypeStruct((N_BLK,) + w.shape, BF16),
        compiler_params=_params(("parallel", "parallel")),
    )(me, w)


def _split_heads(xv, rows_ref, cols_ref):
    if rows_ref is not None:
        for h in range(N_HEADS):
            rows_ref[h] = xv[:, h * HEAD_DIM:(h + 1) * HEAD_DIM].astype(BF16)
    if cols_ref is not None:
        xt = xv.T
        for h in range(N_HEADS):
            cols_ref[h] = xt[h * HEAD_DIM:(h + 1) * HEAD_DIM, :].astype(BF16)


def _head_specs(t, s):
    rows = (pl.BlockSpec((N_HEADS, t, HEAD_DIM), lambda i: (0, i, 0)), jax.ShapeDtypeStruct((N_HEADS, s, HEAD_DIM), BF16))
    cols = (pl.BlockSpec((N_HEADS, HEAD_DIM, t), lambda i: (0, 0, i)), jax.ShapeDtypeStruct((N_HEADS, HEAD_DIM, s), BF16))
    return rows, cols


def _cols(ref, lo, hi):
    return ref[:, lo:hi].astype(F32)


def _pair_blocks(w_ref, wide_ref):
    @pl.when(pl.program_id(0) == 0)
    def _():
        for b in range(N_BLK):
            wide_ref[b // 2, :, (b % 2) * W_IN_BLK:(b % 2 + 1) * W_IN_BLK] = w_ref[b]


def _in_proj(x, g, w, layer):
    s = x.shape[0]
    t = _tile(s, 1024)

    def body(x_ref, g_ref, w_ref, z_ref, h_ref, qr_ref, qt_ref, kt_ref, vt_ref, wide_ref):
        _pair_blocks(w_ref, wide_ref)
        xv = x_ref[...]
        h = (xv * _rsqrt_mean(xv) * g_ref[...]).astype(BF16)
        h_ref[...] = h
        for n in range(N_BLK // 2):
            z_ref[:, 2 * n * W_IN_BLK:2 * (n + 1) * W_IN_BLK] = _dot(h, wide_ref[n]).astype(BF16)
        _split_heads(_cols(z_ref, 1280, 1536) * ATT_SCALE, qr_ref, qt_ref)
        _split_heads(_cols(z_ref, 1536, 1792), None, kt_ref)
        _split_heads(_cols(z_ref, 1792, 2048), None, vt_ref)

    rows, cols = _head_specs(t, s)
    return pl.pallas_call(
        body, name="in_proj", grid=(s // t,),
        in_specs=[pl.BlockSpec((t, D_MODEL), lambda i: (i, 0)), _full((1, D_MODEL)),
                  pl.BlockSpec((N_BLK, None, D_MODEL, W_IN_BLK), lambda i: (0, layer, 0, 0))],
        out_specs=[pl.BlockSpec((t, D_IN), lambda i: (i, 0)), pl.BlockSpec((t, D_MODEL), lambda i: (i, 0)),
                   rows[0], cols[0], cols[0], cols[0]],
        out_shape=[jax.ShapeDtypeStruct((s, D_IN), BF16), jax.ShapeDtypeStruct((s, D_MODEL), BF16),
                   rows[1], cols[1], cols[1], cols[1]],
        scratch_shapes=[pltpu.VMEM((N_BLK // 2, D_MODEL, 2 * W_IN_BLK), BF16)],
        compiler_params=_params(("arbitrary",), V7X_VMEM_LIMIT),
    )(x, g, w)


def _mix_a_fwd(z_ref, vg, wt_ref, bmat, t):
    zu = _cols(z_ref, 0, 256)
    zv = _cols(z_ref, 256, 512)
    tu = _gelu_tanh(zu)
    tv = _gelu_tanh(zv)
    u = _gelu(zu, tu)
    v = _gelu(zv, tv)
    rv = _rsqrt_mean(v)
    vh = v * rv
    vnb = (vh * vg).astype(BF16)
    head = lax.broadcasted_iota(jnp.int32, (CHUNK, D_GROUP), 1) // HEAD_DIM
    fs = []
    for c in range(t // CHUNK):
        vc = vnb[c * CHUNK:(c + 1) * CHUNK, :]
        fc = bmat
        for h in range(N_HEADS):
            fc = fc + jnp.where(head == h, _dot(wt_ref[h], vc), 0.0)
        fs.append(fc)
    f = jnp.concatenate(fs, axis=0) if len(fs) > 1 else fs[0]
    return (zu, tu), (zv, tv), u, rv, vh, vnb, f


def _windows(ext_ref, sh_ref, t):
    for b in range(1, 8):
        sh_ref[b - 1] = ext_ref[pl.ds(b, HALO + t - 8), :]

    def window(o):
        a, b = divmod(o, 8)
        return ext_ref[pl.ds(8 * a, t), :] if b == 0 else sh_ref[b - 1, pl.ds(8 * a, t), :]

    return window


def _mix_b_fwd(z_ref, zh_ref, first, scw_ref, ext_ref, t):
    gb = _cols(z_ref, 512, 768)
    uh = _cols(zh_ref, 768, 1024) * _cols(zh_ref, 1024, 1280)
    ext_ref[0:HALO, :] = jnp.where(first, 0.0, uh)
    ext_ref[HALO:HALO + t, :] = _cols(z_ref, 768, 1024) * _cols(z_ref, 1024, 1280)
    cv = jnp.zeros((t, D_GROUP), F32)
    for k in range(SHORT_K):
        cv = cv + scw_ref[k:k + 1, :] * ext_ref[pl.ds(HALO - (SHORT_K - 1) + k, t), :]
    return gb, cv


def _mix_d_fwd(z_ref, zh_ref, first, ccw_ref, lg, lb, ext_ref, sh_ref, t, cv=None):
    hh = _cols(zh_ref, 2048, 2304) * _sigmoid(_cols(zh_ref, 2304, 2560))
    ext_ref[0:HALO, :] = jnp.where(first, 0.0, hh)
    ext_ref[HALO:HALO + t, :] = _cols(z_ref, 2048, 2304) * _sigmoid(_cols(z_ref, 2304, 2560))
    window = _windows(ext_ref, sh_ref, t)
    if cv is None:
        cv = jnp.zeros((t, D_GROUP), F32)
        for k in range(CONF_K):
            cv = cv + ccw_ref[k:k + 1, :] * window(HALO - (CONF_K - 1) + k)
    xc = cv - jnp.mean(cv, axis=-1, keepdims=True)
    rs = lax.rsqrt(jnp.mean(xc * xc, axis=-1, keepdims=True) + EPS)
    xh = xc * rs
    ln = xh * lg + lb
    return xh, rs, ln, _sigmoid(ln), window, cv


def _mix_specs(t, s):
    per = t // HALO
    return [pl.BlockSpec((t, D_IN), lambda i: (i, 0)),
            pl.BlockSpec((HALO, D_IN), lambda i: (jnp.maximum(i * per - 1, 0), 0))]


def _full(shape):
    return pl.BlockSpec(shape, lambda i: (0,) * len(shape))


def _mixers_fwd(z, p, bufs=(), parts=()):
    s = z.shape[0]
    t = _tile(s, MIX_ROWS)
    nb = len(bufs)

    def body(z_ref, zh_ref, vg_ref, wt_ref, bm_ref, scw_ref, ccw_ref, lg_ref, lb_ref, *rest):
        y_ref = rest[nb]
        eb_ref, ed_ref, sh_ref = rest[2 * nb + 1:2 * nb + 4]
        if nb:
            _gather_in_steps(rest[nb + 1:2 * nb + 1], parts, rest[2 * nb + 4:], s // t)
        first = pl.program_id(0) == 0
        _, _, u, _, _, _, f = _mix_a_fwd(z_ref, vg_ref[...], wt_ref, bm_ref[...], t)
        ya = u * f
        y_ref[:, 0:256] = ya * _rsqrt_mean(ya)
        gb, cv = _mix_b_fwd(z_ref, zh_ref, first, scw_ref, eb_ref, t)
        yb = gb * cv
        y_ref[:, 256:512] = yb * _rsqrt_mean(yb)
        _, _, ln, sg, _, cvd = _mix_d_fwd(z_ref, zh_ref, first, ccw_ref, lg_ref[...], lb_ref[...], ed_ref, sh_ref, t)
        yd = ln * sg
        y_ref[:, 512:768] = yd * _rsqrt_mean(yd)
        y_ref[:, 768:1024] = cvd

    out = pl.pallas_call(
        body, name="mixers_fwd", grid=(s // t,),
        in_specs=_mix_specs(t, s) + [_full((1, D_GROUP)), _full((N_HEADS, CHUNK, CHUNK)), _full((CHUNK, D_GROUP)),
                                     _full((8, D_GROUP)), _full((32, D_GROUP)), _full((1, D_GROUP)), _full((1, D_GROUP))]
                 + [_ANY] * nb,
        out_specs=[pl.BlockSpec((t, D_MODEL), lambda i: (i, 0))] + [_ANY] * nb,
        out_shape=[jax.ShapeDtypeStruct((s, D_MODEL), F32)] + [jax.ShapeDtypeStruct(b.shape, b.dtype) for b in bufs],
        input_output_aliases={9 + k: 1 + k for k in range(nb)},
        scratch_shapes=[pltpu.VMEM((HALO + t, D_GROUP), F32), pltpu.VMEM((HALO + t, D_GROUP), F32),
                        pltpu.VMEM((7, HALO + t - 8, D_GROUP), F32)] + (_gather_sems(parts) if nb else []),
        compiler_params=_params(("arbitrary",) if nb else ("parallel",), V7X_VMEM_LIMIT),
    )(z, z, p["vg"], p["wt"], p["bmat"], p["scw"], p["ccw"], p["lg"], p["lb"], *bufs)
    return (out[0], out[1:]) if nb else out[0]


def _mixers_bwd_a(z, dyn, o_t, y_abd, p, sums=()):
    s = z.shape[0]
    t = _tile(s, MIX_ROWS)
    n_chunk = t // CHUNK
    ns = len(sums)

    def body(*refs):
        (z_ref, zh_ref, dyn_ref, ot_ref, cv_ref, vg_ref, wt_ref, wtt_ref, bm_ref, scw_ref, ccw_ref, lg_ref,
         lb_ref) = refs[:13]
        (dza_ref, dcb_ref, dcd_ref, dor_ref, dot_ref, ds_ref, dvg_ref, dws_ref, dbm_ref, dscw_ref, dccw_ref, dlg_ref,
         dlb_ref) = refs[13 + ns:26 + ns]
        eb_ref, ed_ref, sh_ref = refs[26 + 2 * ns:29 + 2 * ns]
        i = pl.program_id(0)
        first = i == 0
        if ns:
            _owners_in_steps(refs[13:13 + ns], refs[26 + ns:26 + 2 * ns], refs[29 + 2 * ns:], first, i == s // t - 1)

        @pl.when(first)
        def _():
            for r in (dvg_ref, dws_ref, dbm_ref, dscw_ref, dccw_ref, dlg_ref, dlb_ref):
                r[...] = jnp.zeros_like(r)

        def rms_bwd(y, dn):
            r = _rsqrt_mean(y)
            yn = y * r
            return r * (dn - yn * jnp.mean(dn * yn, axis=-1, keepdims=True))

        vg = vg_ref[...]
        gelu_u, gelu_v, u, rv, vh, vnb, f = _mix_a_fwd(z_ref, vg, wt_ref, bm_ref[...], t)
        dya = rms_bwd(u * f, _cols(dyn_ref, 0, 256))
        du = dya * f
        df = dya * u
        head = lax.broadcasted_iota(jnp.int32, (CHUNK, D_GROUP), 1) // HEAD_DIM
        dvns = []
        dbm = jnp.zeros((CHUNK, D_GROUP), F32)
        for c in range(n_chunk):
            dfc = df[c * CHUNK:(c + 1) * CHUNK, :]
            vc = vnb[c * CHUNK:(c + 1) * CHUNK, :]
            dbm = dbm + dfc
            dvn = jnp.zeros((CHUNK, D_GROUP), F32)
            for h in range(N_HEADS):
                dfh = jnp.where(head == h, dfc, 0.0).astype(BF16)
                dvn = dvn + _dot(wtt_ref[h], dfh)
                dws_ref[h] += _dot_nt(dfh, vc)
            dvns.append(dvn)
        dbm_ref[...] += dbm
        dvn = jnp.concatenate(dvns, axis=0) if n_chunk > 1 else dvns[0]
        dvg_ref[...] += jnp.sum(dvn * vh, axis=0, keepdims=True)
        dvh = dvn * vg
        dv = rv * (dvh - vh * jnp.mean(dvh * vh, axis=-1, keepdims=True))
        dza_ref[:, 0:256] = (du * _gelu_grad(*gelu_u)).astype(BF16)
        dza_ref[:, 256:512] = (dv * _gelu_grad(*gelu_v)).astype(BF16)

        gb, cv = _mix_b_fwd(z_ref, zh_ref, first, scw_ref, eb_ref, t)
        dyb = rms_bwd(gb * cv, _cols(dyn_ref, 256, 512))
        dza_ref[:, 512:768] = (dyb * cv).astype(BF16)
        dcb = dyb * gb
        dcb_ref[...] = dcb
        for k in range(SHORT_K):
            dscw_ref[k:k + 1, :] += jnp.sum(dcb * eb_ref[pl.ds(HALO - (SHORT_K - 1) + k, t), :], axis=0, keepdims=True)

        lg = lg_ref[...]
        xh, rs, ln, sg, window, _ = _mix_d_fwd(z_ref, zh_ref, first, ccw_ref, lg, lb_ref[...], ed_ref, sh_ref, t,
                                               cv_ref[...])
        dyd = rms_bwd(ln * sg, _cols(dyn_ref, 768, 1024))
        dln = dyd * (sg * (1.0 + ln * (1.0 - sg)))
        dlg_ref[...] += jnp.sum(dln * xh, axis=0, keepdims=True)
        dlb_ref[...] += jnp.sum(dln, axis=0, keepdims=True)
        dxh = dln * lg
        dcd = rs * (dxh - jnp.mean(dxh, axis=-1, keepdims=True) - xh * jnp.mean(dxh * xh, axis=-1, keepdims=True))
        dcd_ref[...] = dcd
        for k in range(CONF_K):
            dccw_ref[k:k + 1, :] += jnp.sum(dcd * window(HALO - (CONF_K - 1) + k), axis=0, keepdims=True)

        o = ot_ref[...].reshape(D_GROUP, t).T
        do = rms_bwd(o, _cols(dyn_ref, 512, 768))
        _split_heads(do, dor_ref, dot_ref)
        prod = do.astype(BF16).astype(F32) * o
        for h in range(N_HEADS):
            ds_ref[h] = jnp.sum(prod[:, h * HEAD_DIM:(h + 1) * HEAD_DIM], axis=1, keepdims=True)

    small = [(1, D_GROUP), (N_HEADS, CHUNK, CHUNK), (CHUNK, D_GROUP), (8, D_GROUP), (32, D_GROUP), (1, D_GROUP), (1, D_GROUP)]
    rows, cols = _head_specs(t, s)
    out = pl.pallas_call(
        body, name="mixers_bwd_a", grid=(s // t,),
        in_specs=_mix_specs(t, s) + [pl.BlockSpec((t, D_MODEL), lambda i: (i, 0)),
                                     pl.BlockSpec((N_HEADS, HEAD_DIM, t), lambda i: (0, 0, i)),
                                     pl.BlockSpec((t, D_GROUP), lambda i: (i, 3)),
                                     _full((1, D_GROUP)), _full((N_HEADS, CHUNK, CHUNK)), _full((N_HEADS, CHUNK, CHUNK)),
                                     _full((CHUNK, D_GROUP)), _full((8, D_GROUP)), _full((32, D_GROUP)),
                                     _full((1, D_GROUP)), _full((1, D_GROUP))] + [_ANY] * ns,
        out_specs=[pl.BlockSpec((t, 768), lambda i: (i, 0)), pl.BlockSpec((t, D_GROUP), lambda i: (i, 0)),
                   pl.BlockSpec((t, D_GROUP), lambda i: (i, 0)), rows[0], cols[0],
                   pl.BlockSpec((N_HEADS, t, 1), lambda i: (0, i, 0))]
                  + [_full(sh) for sh in small] + [_ANY] * ns,
        out_shape=[jax.ShapeDtypeStruct((s, 768), BF16), jax.ShapeDtypeStruct((s, D_GROUP), F32),
                   jax.ShapeDtypeStruct((s, D_GROUP), F32), rows[1], cols[1],
                   jax.ShapeDtypeStruct((N_HEADS, s, 1), F32)]
                  + [jax.ShapeDtypeStruct(sh, F32) for sh in small]
                  + [jax.ShapeDtypeStruct(a.shape, a.dtype) for a in sums],
        scratch_shapes=[pltpu.VMEM((HALO + t, D_GROUP), F32), pltpu.VMEM((HALO + t, D_GROUP), F32),
                        pltpu.VMEM((7, HALO + t - 8, D_GROUP), F32)] + (_owner_sems(ns) if ns else []),
        compiler_params=_params(("arbitrary",), V7X_VMEM_LIMIT),
    )(z, z, dyn, o_t, y_abd, p["vg"], p["wt"], p["wtt"], p["bmat"], p["scw"], p["ccw"], p["lg"], p["lb"], *sums)
    return tuple(out[:13]) + (out[13:],) if ns else out


def _mixers_bwd_b(z, dza, dcb, dcd, dq_t, dk_t, dv_t, p):
    s = z.shape[0]
    t = _tile(s, MIX_ROWS)
    per = t // HALO
    n_halo = s // HALO

    def body(z_ref, dza_ref, dcb_ref, dcbn_ref, dcd_ref, dcdn_ref, dq_ref, dk_ref, dv_ref, scw_ref, ccw_ref,
             dz_ref, eb_ref, ed_ref, sh_ref):
        last = pl.program_id(0) == pl.num_programs(0) - 1
        dz_ref[:, 0:768] = dza_ref[...]
        eb_ref[0:t, :] = dcb_ref[...]
        eb_ref[t:t + HALO, :] = jnp.where(last, 0.0, dcbn_ref[...])
        du = jnp.zeros((t, D_GROUP), F32)
        for k in range(SHORT_K):
            du = du + scw_ref[k:k + 1, :] * eb_ref[pl.ds(SHORT_K - 1 - k, t), :]
        dz_ref[:, 768:1024] = (du * _cols(z_ref, 1024, 1280)).astype(BF16)
        dz_ref[:, 1024:1280] = (du * _cols(z_ref, 768, 1024)).astype(BF16)
        for n, r in enumerate((dq_ref, dk_ref, dv_ref)):
            dz_ref[:, 1280 + 256 * n:1536 + 256 * n] = r[...].reshape(D_GROUP, t).T.astype(BF16)
        ed_ref[0:t, :] = dcd_ref[...]
        ed_ref[t:t + HALO, :] = jnp.where(last, 0.0, dcdn_ref[...])
        window = _windows(ed_ref, sh_ref, t)
        dh = jnp.zeros((t, D_GROUP), F32)
        for k in range(CONF_K):
            dh = dh + ccw_ref[k:k + 1, :] * window(CONF_K - 1 - k)
        a = _cols(z_ref, 2048, 2304)
        sg = _sigmoid(_cols(z_ref, 2304, 2560))
        dz_ref[:, 2048:2304] = (dh * sg).astype(BF16)
        dz_ref[:, 2304:2560] = (dh * a * sg * (1.0 - sg)).astype(BF16)

    nxt = lambda i: (jnp.minimum((i + 1) * per, n_halo - 1), 0)
    tr = pl.BlockSpec((N_HEADS, HEAD_DIM, t), lambda i: (0, 0, i))
    return pl.pallas_call(
        body, name="mixers_bwd_b", grid=(s // t,),
        in_specs=[pl.BlockSpec((t, D_IN), lambda i: (i, 0)), pl.BlockSpec((t, 768), lambda i: (i, 0)),
                  pl.BlockSpec((t, D_GROUP), lambda i: (i, 0)), pl.BlockSpec((HALO, D_GROUP), nxt),
                  pl.BlockSpec((t, D_GROUP), lambda i: (i, 0)), pl.BlockSpec((HALO, D_GROUP), nxt),
                  tr, tr, tr, _full((8, D_GROUP)), _full((32, D_GROUP))],
        out_specs=pl.BlockSpec((t, D_IN), lambda i: (i, 0)),
        out_shape=jax.ShapeDtypeStruct((s, D_IN), BF16),
        scratch_shapes=[pltpu.VMEM((HALO + t, D_GROUP), F32), pltpu.VMEM((HALO + t, D_GROUP), F32),
                        pltpu.VMEM((7, HALO + t - 8, D_GROUP), F32)],
        compiler_params=_params(("parallel",), V7X_VMEM_LIMIT),
    )(z, dza, dcb, dcb, dcd, dcd, dq_t, dk_t, dv_t, p["scw"], p["ccw"])


def _split_bf16(v):
    hi = v.astype(BF16)
    return hi, (v - hi.astype(F32)).astype(BF16)


def _att_scores(qs, kts, carries, tri, mask):
    zs = [_dot(q, kt) for q, kt in zip(qs, kts)]
    lms, lbs, parts = [], [], []
    for z in zs:
        soft = jnp.log(1.0 + jnp.exp(-jnp.abs(z)))
        lm = -(jnp.maximum(z, 0.0) + soft)
        lbs.append(lm + z)
        if mask is not None:
            lm = jnp.where(mask, lm, 0.0)
        lms.append(lm)
        parts.append(_split_bf16(lm))
    rights = [_dot(hi, tri) + _dot(lo, tri) for hi, lo in parts]
    ws = []
    for lb, right, carry in zip(lbs, rights, carries):
        w = jnp.exp(lb + right + carry)
        ws.append(w if mask is None else jnp.where(mask, w, 0.0))
    return ws, lbs, [jnp.sum(lm, axis=1, keepdims=True) for lm in lms]


def _att_consts(i):
    j_hi = ((i + 1) * ATT_TQ - 1) // ATT_TK
    row = lax.broadcasted_iota(jnp.int32, (ATT_TQ, ATT_TK), 0) + i * ATT_TQ
    col = lax.broadcasted_iota(jnp.int32, (ATT_TQ, ATT_TK), 1) + j_hi * ATT_TK
    r_i = lax.broadcasted_iota(jnp.int32, (ATT_TK, ATT_TK), 0)
    c_i = lax.broadcasted_iota(jnp.int32, (ATT_TK, ATT_TK), 1)
    return j_hi, col < row, r_i, c_i


def _att_alive(j, carries):
    top = carries[0]
    for c in carries[1:]:
        top = jnp.maximum(top, c)
    return jnp.logical_and(j >= 0, jnp.max(top) > ATT_DEAD)


def _attn_fwd(q_r, k_t, v_t, bufs=(), parts=()):
    s = q_r.shape[1]
    nb = len(bufs)

    def body(q_ref, kt_ref, vt_ref, *rest):
        o_ref = rest[nb]
        if nb:
            _gather_in_steps(rest[nb + 1:2 * nb + 1], parts, rest[2 * nb + 1:], s // ATT_TQ)
        j_hi, mask, r_i, c_i = _att_consts(pl.program_id(0))
        tri = (r_i > c_i).astype(BF16)

        heads = range(N_HEADS)

        def tiles(j, carries, accs, mask):
            cols = pl.ds(pl.multiple_of(j * ATT_TK, ATT_TK), ATT_TK)
            ws, _, tots = _att_scores([q_ref[h] for h in heads], [kt_ref[h, :, cols] for h in heads], carries, tri, mask)
            accs = [acc + _dot_nt(vt_ref[h, :, cols], w.astype(BF16)) for h, acc, w in zip(heads, accs, ws)]
            return [c + t for c, t in zip(carries, tots)], accs

        state = tiles(j_hi, [jnp.zeros((ATT_TQ, 1), F32)] * N_HEADS, [jnp.zeros((HEAD_DIM, ATT_TQ), F32)] * N_HEADS, mask)

        def cond(c):
            return _att_alive(c[0], c[1])

        def step(c):
            return (c[0] - 1,) + tuple(tiles(c[0], c[1], c[2], None))

        _, _, accs = lax.while_loop(cond, step, (j_hi - 1,) + tuple(state))
        for h in heads:
            o_ref[h] = accs[h]

    whole = pl.BlockSpec((N_HEADS, HEAD_DIM, s), lambda i: (0, 0, 0), pipeline_mode=pl.Buffered(1))
    out = pl.pallas_call(
        body, name="attn_fwd", grid=(s // ATT_TQ,),
        in_specs=[pl.BlockSpec((N_HEADS, ATT_TQ, HEAD_DIM), lambda i: (0, i, 0)), whole, whole] + [_ANY] * nb,
        out_specs=[pl.BlockSpec((N_HEADS, HEAD_DIM, ATT_TQ), lambda i: (0, 0, i))] + [_ANY] * nb,
        out_shape=[jax.ShapeDtypeStruct((N_HEADS, HEAD_DIM, s), F32)] + [jax.ShapeDtypeStruct(b.shape, b.dtype) for b in bufs],
        input_output_aliases={3 + k: 1 + k for k in range(nb)},
        scratch_shapes=_gather_sems(parts) if nb else [],
        compiler_params=_params(("arbitrary",), V7X_VMEM_LIMIT),
    )(q_r, k_t, v_t, *bufs)
    return (out[0], out[1:]) if nb else out[0]


ATT_BWD_HEADS = 2


def _attn_bwd(q_r, q_t, k_t, v_t, do_r, do_t, dsum):
    s = q_r.shape[1]
    hps = ATT_BWD_HEADS

    def body(q_ref, qt_ref, kt_ref, vt_ref, do_ref, dot_ref, ds_ref, dq_ref, dk_ref, dv_ref):
        i = pl.program_id(1)

        @pl.when(i == 0)
        def _():
            dk_ref[...] = jnp.zeros_like(dk_ref)
            dv_ref[...] = jnp.zeros_like(dv_ref)

        j_hi, mask, r_i, c_i = _att_consts(i)
        tri_r = (r_i > c_i).astype(BF16)
        tri_ge = (r_i >= c_i).astype(BF16)

        heads = range(hps)

        def tiles(j, carries, gsums, accs, mask):
            cols = pl.ds(pl.multiple_of(j * ATT_TK, ATT_TK), ATT_TK)
            kts = [kt_ref[h, :, cols] for h in heads]
            das = [_dot(do_ref[h], vt_ref[h, :, cols]) for h in heads]
            ws, lbs, tots = _att_scores([q_ref[h] for h in heads], kts, carries, tri_r, mask)
            wbs = [w.astype(BF16) for w in ws]
            gs = [wb.astype(F32) * da for wb, da in zip(wbs, das)]
            parts = [_split_bf16(g) for g in gs]
            sfx = [_dot(hi, tri_ge) + _dot(lo, tri_ge) for hi, lo in parts]
            for h in heads:
                dv_ref[h, :, cols] += _dot(dot_ref[h], wbs[h])
            dzs = []
            for h in heads:
                left = ds_ref[h] - gsums[h] - sfx[h]
                dz = gs[h] - jnp.exp(lbs[h]) * (gs[h] + left)
                dzs.append((dz if mask is None else jnp.where(mask, dz, 0.0)).astype(BF16))
            for h in heads:
                dk_ref[h, :, cols] += _dot(qt_ref[h], dzs[h])
            accs = [accs[h] + _dot_nt(kts[h], dzs[h]) for h in heads]
            gsums = [gsums[h] + jnp.sum(gs[h], axis=1, keepdims=True) for h in heads]
            return [c + t for c, t in zip(carries, tots)], gsums, accs

        col0 = [jnp.zeros((ATT_TQ, 1), F32)] * hps
        state = tiles(j_hi, col0, col0, [jnp.zeros((HEAD_DIM, ATT_TQ), F32)] * hps, mask)

        def cond(c):
            return _att_alive(c[0], c[1])

        def step(c):
            return (c[0] - 1,) + tuple(tiles(c[0], c[1], c[2], c[3], None))

        _, _, _, accs = lax.while_loop(cond, step, (j_hi - 1,) + tuple(state))
        for h in heads:
            dq_ref[h] = accs[h] * ATT_SCALE

    whole = pl.BlockSpec((hps, HEAD_DIM, s), lambda g, i: (g, 0, 0))
    whole_in = pl.BlockSpec((hps, HEAD_DIM, s), lambda g, i: (g, 0, 0), pipeline_mode=pl.Buffered(1))
    rows = pl.BlockSpec((hps, ATT_TQ, HEAD_DIM), lambda g, i: (g, i, 0))
    cols = pl.BlockSpec((hps, HEAD_DIM, ATT_TQ), lambda g, i: (g, 0, i))
    shape = jax.ShapeDtypeStruct((N_HEADS, HEAD_DIM, s), F32)
    return pl.pallas_call(
        body, name="attn_bwd", grid=(N_HEADS // hps, s // ATT_TQ),
        in_specs=[rows, cols, whole_in, whole_in, rows, cols, pl.BlockSpec((hps, ATT_TQ, 1), lambda g, i: (g, i, 0))],
        out_specs=[cols, whole, whole],
        out_shape=[shape, shape, shape],
        compiler_params=_params(("parallel", "arbitrary"), V7X_VMEM_LIMIT),
    )(q_r, q_t, k_t, v_t, do_r, do_t, dsum)


def _out_proj(x, y_abd, o_t, gain, w, layer):
    s = x.shape[0]
    t = _tile(s, 1024)

    def body(x_ref, y_ref, ot_ref, g_ref, w_ref, x1_ref, yn_ref):
        o = ot_ref[...].reshape(D_GROUP, t).T
        groups = [y_ref[:, 0:256], y_ref[:, 256:512], o * _rsqrt_mean(o), y_ref[:, 512:768]]
        g = g_ref[...]
        acc = None
        for b, yn in enumerate(groups):
            cols = slice(256 * b, 256 * (b + 1))
            yn_ref[:, cols] = yn.astype(BF16)
            part = _dot((yn * g[:, cols]).astype(BF16), w_ref[b])
            acc = part if acc is None else acc + part
        x1_ref[...] = x_ref[...] + acc

    return pl.pallas_call(
        body, name="out_proj", grid=(s // t,),
        in_specs=[pl.BlockSpec((t, D_MODEL), lambda i: (i, 0)), pl.BlockSpec((t, 768), lambda i: (i, 0)),
                  pl.BlockSpec((N_HEADS, HEAD_DIM, t), lambda i: (0, 0, i)), _full((1, D_MODEL)),
                  pl.BlockSpec((N_BLK, None, D_GROUP, D_MODEL), lambda i: (0, layer, 0, 0))],
        out_specs=[pl.BlockSpec((t, D_MODEL), lambda i: (i, 0)), pl.BlockSpec((t, D_MODEL), lambda i: (i, 0))],
        out_shape=[jax.ShapeDtypeStruct((s, D_MODEL), F32), jax.ShapeDtypeStruct((s, D_MODEL), BF16)],
        compiler_params=_params(("parallel",), V7X_VMEM_LIMIT),
    )(x, y_abd, o_t, gain, w)


def _out_proj_bwd(x1, g_ffn, dh, dx2, yn, gain, w, layer, halves=()):
    s = dx2.shape[0]
    t = _tile(s, 512)
    nh = len(halves)

    def body(x_ref, gf_ref, dh_ref, dx2_ref, yn_ref, g_ref, w_ref, *rest):
        dx1_ref, dgf_ref, dyn_ref, dg_ref, dw_ref = rest[nh:nh + 5]
        if nh:
            start, finish = _halves_stages(rest[:nh], rest[nh + 5:2 * nh + 5], rest[2 * nh + 5:])
            pl.when(pl.program_id(0) == 0)(start)
            pl.when(pl.program_id(0) == s // t - 1)(finish)

        @pl.when(pl.program_id(0) == 0)
        def _():
            dg_ref[...] = jnp.zeros_like(dg_ref)
            dw_ref[...] = jnp.zeros_like(dw_ref)
            dgf_ref[...] = jnp.zeros_like(dgf_ref)

        dx1, dgf = _rms_bwd_rows(x_ref[...], gf_ref[...], dh_ref[...], dx2_ref[...])
        dx1_ref[...] = dx1
        dgf_ref[...] += dgf
        dxb = dx1.astype(BF16)
        g = g_ref[...]
        yn = yn_ref[...].astype(F32)
        yg = (yn * g).astype(BF16)
        for b in range(N_BLK):
            cols = slice(256 * b, 256 * (b + 1))
            dyg = _dot_nt(dxb, w_ref[b])
            dw_ref[b] += _dot_tn(yg[:, cols], dxb)
            dg_ref[:, cols] += jnp.sum(dyg * yn[:, cols], axis=0, keepdims=True)
            dyn_ref[:, cols] = (dyg * g[:, cols]).astype(BF16)

    row = pl.BlockSpec((t, D_MODEL), lambda i: (i, 0))
    vec = _full((1, D_MODEL))
    out = pl.pallas_call(
        body, name="out_proj_bwd", grid=(s // t,),
        in_specs=[row, vec, row, row, row, vec, pl.BlockSpec((N_BLK, None, D_GROUP, D_MODEL), lambda i: (0, layer, 0, 0))]
                 + [_ANY] * nh,
        out_specs=[row, vec, row, vec, _full((N_BLK, D_GROUP, D_MODEL))] + [_ANY] * nh,
        out_shape=[jax.ShapeDtypeStruct((s, D_MODEL), F32), jax.ShapeDtypeStruct((1, D_MODEL), F32),
                   jax.ShapeDtypeStruct((s, D_MODEL), BF16), jax.ShapeDtypeStruct((1, D_MODEL), F32),
                   jax.ShapeDtypeStruct((N_BLK, D_GROUP, D_MODEL), F32)] + _halves_shapes(halves),
        scratch_shapes=_halves_sems(nh) if nh else [],
        compiler_params=_params(("arbitrary",), V7X_VMEM_LIMIT),
    )(x1, g_ffn, dh, dx2, yn, gain, w, *halves)
    return tuple(out[:5]) + (out[5:],) if nh else out


def _ffn(x, g, w_up, w_down, layer):
    s = x.shape[0]
    t = _tile(s, 1024)

    def body(x_ref, g_ref, wu_ref, wd_ref, x2_ref, p_ref, h_ref):
        @pl.when(pl.program_id(1) == 0)
        def _():
            xv = x_ref[...]
            h_ref[...] = (xv * _rsqrt_mean(xv) * g_ref[...]).astype(BF16)
            x2_ref[...] = xv

        pre = _dot(h_ref[...], wu_ref[...])
        p_ref[...] = pre.astype(BF16)
        a = jnp.maximum(pre, 0.0)
        x2_ref[...] += _dot((a * a).astype(BF16), wd_ref[...])

    wspec = pl.BlockSpec((None, None, D_MODEL, D_FF_BLK), lambda i, j: (j, layer, 0, 0))
    row = pl.BlockSpec((t, D_MODEL), lambda i, j: (i, 0))
    return pl.pallas_call(
        body, name="ffn", grid=(s // t, N_BLK),
        in_specs=[row, pl.BlockSpec((1, D_MODEL), lambda i, j: (0, 0)), wspec, wspec],
        out_specs=[row, pl.BlockSpec((t, D_FF_BLK), lambda i, j: (i, j)), row],
        out_shape=[jax.ShapeDtypeStruct((s, D_MODEL), F32), jax.ShapeDtypeStruct((s, N_BLK * D_FF_BLK), BF16),
                   jax.ShapeDtypeStruct((s, D_MODEL), BF16)],
        compiler_params=_params(("parallel", "arbitrary"), V7X_VMEM_LIMIT),
    )(x, g, w_up, w_down)


def _loss_head(x, g, target):
    s = x.shape[0]
    t = _tile(s, 512)

    def body(x_ref, g_ref, t_ref, l_ref, dx_ref, dxb_ref, dg_ref):
        @pl.when(pl.program_id(0) == 0)
        def _():
            l_ref[...] = jnp.zeros_like(l_ref)
            dg_ref[...] = jnp.zeros_like(dg_ref)

        xv = x_ref[...]
        g = g_ref[...]
        r = _rsqrt_mean(xv)
        xh = xv * r
        err = xh * g - t_ref[...]
        l_ref[...] += 0.5 * jnp.sum(jnp.mean(err * err, axis=-1, keepdims=True), axis=0, keepdims=True)
        dy = err * (1.0 / D_MODEL)
        dg_ref[...] += jnp.sum(dy * xh, axis=0, keepdims=True)
        dxh = dy * g
        dx = r * (dxh - xh * jnp.mean(dxh * xh, axis=-1, keepdims=True))
        dx_ref[...] = dx
        dxb_ref[...] = dx.astype(BF16)

    row = pl.BlockSpec((t, D_MODEL), lambda i: (i, 0))
    return pl.pallas_call(
        body, name="loss_head", grid=(s // t,),
        in_specs=[row, _full((1, D_MODEL)), row],
        out_specs=[_full((1, 128)), row, row, _full((1, D_MODEL))],
        out_shape=[jax.ShapeDtypeStruct((1, 128), F32), jax.ShapeDtypeStruct((s, D_MODEL), F32),
                   jax.ShapeDtypeStruct((s, D_MODEL), BF16), jax.ShapeDtypeStruct((1, D_MODEL), F32)],
        compiler_params=_params(("arbitrary",)),
    )(x, g, target)


def _rms_bwd_rows(xv, g, dh, dres):
    r = _rsqrt_mean(xv)
    xh = xv * r
    dxh = dh * g
    dx = dres + r * (dxh - xh * jnp.mean(dxh * xh, axis=-1, keepdims=True))
    return dx, jnp.sum(dh * xh, axis=0, keepdims=True)


def _ffn_bwd(dxb, p, w_up, w_down, layer, sums=()):
    s = dxb.shape[0]
    t = _tile(s, 1024)
    ns = len(sums)

    def body(dx_ref, p_ref, wu_ref, wd_ref, *rest):
        dp_ref, dh_ref = rest[ns:ns + 2]
        if ns:
            i, j = pl.program_id(0), pl.program_id(1)
            _owners_in_steps(rest[:ns], rest[ns + 2:2 * ns + 2], rest[2 * ns + 2:],
                             jnp.logical_and(i == 0, j == 0), jnp.logical_and(i == s // t - 1, j == N_BLK - 1))
        da = _dot_nt(dx_ref[...], wd_ref[...])
        a = jnp.maximum(p_ref[...].astype(F32), 0.0)
        dp = (da * (2.0 * a)).astype(BF16)
        dp_ref[...] = dp
        dh = _dot_nt(dp, wu_ref[...])

        @pl.when(pl.program_id(1) == 0)
        def _():
            dh_ref[...] = dh

        @pl.when(pl.program_id(1) != 0)
        def _():
            dh_ref[...] += dh

    wspec = pl.BlockSpec((None, None, D_MODEL, D_FF_BLK), lambda i, j: (j, layer, 0, 0))
    row = pl.BlockSpec((t, D_MODEL), lambda i, j: (i, 0))
    blk = pl.BlockSpec((t, D_FF_BLK), lambda i, j: (i, j))
    out = pl.pallas_call(
        body, name="ffn_bwd", grid=(s // t, N_BLK),
        in_specs=[row, blk, wspec, wspec] + [_ANY] * ns, out_specs=[blk, row] + [_ANY] * ns,
        out_shape=[jax.ShapeDtypeStruct((s, N_BLK * D_FF_BLK), BF16), jax.ShapeDtypeStruct((s, D_MODEL), F32)]
                  + [jax.ShapeDtypeStruct(a.shape, a.dtype) for a in sums],
        scratch_shapes=_owner_sems(ns) if ns else [],
        compiler_params=_params(("arbitrary" if ns else "parallel", "arbitrary"), V7X_VMEM_LIMIT),
    )(dxb, p, w_up, w_down, *sums)
    return (out[0], out[1], out[2:]) if ns else out


def _ffn_wgrad(hb, p, dp, dxb):
    s = hb.shape[0]
    t = _tile(s, 1024)

    def body(h_ref, p_ref, dp_ref, dx_ref, du_ref, dd_ref):
        @pl.when(pl.program_id(1) == 0)
        def _():
            du_ref[...] = jnp.zeros_like(du_ref)
            dd_ref[...] = jnp.zeros_like(dd_ref)

        a = jnp.maximum(p_ref[...].astype(F32), 0.0)
        du_ref[...] += _dot_tn(h_ref[...], dp_ref[...])
        dd_ref[...] += _dot_tn((a * a).astype(BF16), dx_ref[...])

    row = pl.BlockSpec((t, D_MODEL), lambda j, i: (i, 0))
    blk = pl.BlockSpec((t, D_FF_BLK), lambda j, i: (i, j))
    out = pl.BlockSpec((None, D_MODEL, D_FF_BLK), lambda j, i: (j, 0, 0))
    shape = jax.ShapeDtypeStruct((N_BLK, D_MODEL, D_FF_BLK), F32)
    return pl.pallas_call(
        body, name="ffn_wgrad", grid=(N_BLK, s // t),
        in_specs=[row, blk, blk, row], out_specs=[out, out], out_shape=[shape, shape],
        compiler_params=_params(("parallel", "arbitrary"), V7X_VMEM_LIMIT),
    )(hb, p, dp, dxb)


def _in_proj_bwd(x, g, dx1, dz, w, layer):
    s = x.shape[0]
    t = _tile(s, 512)

    def body(x_ref, g_ref, dx1_ref, dz_ref, w_ref, dx0_ref, dxb_ref, dg_ref, wide_ref):
        _pair_blocks(w_ref, wide_ref)

        @pl.when(pl.program_id(0) == 0)
        def _():
            dg_ref[...] = jnp.zeros_like(dg_ref)

        dh = _dot_nt(dz_ref[:, 0:2 * W_IN_BLK], wide_ref[0])
        for n in range(1, N_BLK // 2):
            dh = dh + _dot_nt(dz_ref[:, 2 * n * W_IN_BLK:2 * (n + 1) * W_IN_BLK], wide_ref[n])
        dx, dg = _rms_bwd_rows(x_ref[...], g_ref[...], dh, dx1_ref[...])
        dx0_ref[...] = dx
        dxb_ref[...] = dx.astype(BF16)
        dg_ref[...] += dg

    row = pl.BlockSpec((t, D_MODEL), lambda i: (i, 0))
    return pl.pallas_call(
        body, name="in_proj_bwd", grid=(s // t,),
        in_specs=[row, _full((1, D_MODEL)), row, pl.BlockSpec((t, D_IN), lambda i: (i, 0)),
                  pl.BlockSpec((N_BLK, None, D_MODEL, W_IN_BLK), lambda i: (0, layer, 0, 0))],
        out_specs=[row, row, _full((1, D_MODEL))],
        out_shape=[jax.ShapeDtypeStruct((s, D_MODEL), F32), jax.ShapeDtypeStruct((s, D_MODEL), BF16),
                   jax.ShapeDtypeStruct((1, D_MODEL), F32)],
        scratch_shapes=[pltpu.VMEM((N_BLK // 2, D_MODEL, 2 * W_IN_BLK), BF16)],
        compiler_params=_params(("arbitrary",), V7X_VMEM_LIMIT),
    )(x, g, dx1, dz, w)


def _in_proj_wgrad(hb, dz):
    s = hb.shape[0]
    t = _tile(s, 1024)

    def body(h_ref, dz_ref, dw_ref, wide_ref):
        @pl.when(pl.program_id(0) == 0)
        def _():
            wide_ref[...] = jnp.zeros_like(wide_ref)

        h = h_ref[...]
        for n in range(N_BLK // 2):
            wide_ref[n] += _dot_tn(h, dz_ref[:, 2 * n * W_IN_BLK:2 * (n + 1) * W_IN_BLK])

        @pl.when(pl.program_id(0) == s // t - 1)
        def _():
            for b in range(N_BLK):
                dw_ref[b] = wide_ref[b // 2, :, (b % 2) * W_IN_BLK:(b % 2 + 1) * W_IN_BLK]

    return pl.pallas_call(
        body, name="in_proj_wgrad", grid=(s // t,),
        in_specs=[pl.BlockSpec((t, D_MODEL), lambda i: (i, 0)), pl.BlockSpec((t, D_IN), lambda i: (i, 0))],
        out_specs=_full((N_BLK, D_MODEL, W_IN_BLK)),
        out_shape=jax.ShapeDtypeStruct((N_BLK, D_MODEL, W_IN_BLK), F32),
        scratch_shapes=[pltpu.VMEM((N_BLK // 2, D_MODEL, 2 * W_IN_BLK), F32)],
        compiler_params=_params(("arbitrary",), V7X_VMEM_LIMIT),
    )(hb, dz)


def _layer_params(small, layer):
    tril = jnp.tril(jnp.ones((CHUNK, CHUNK), bool))
    ws = jnp.where(tril, small["gmlp_w_s"][layer], 0.0)
    bmat = jnp.repeat(small["gmlp_b_s"][layer].T, HEAD_DIM, axis=1)
    scw = jnp.zeros((8, D_GROUP), F32).at[:SHORT_K].set(small["short_conv_w"][layer])
    ccw = jnp.zeros((32, D_GROUP), F32).at[:CONF_K].set(small["conf_conv_w"][layer])
    return dict(vg=small["gmlp_v_g"][layer][None], wt=ws.astype(BF16), wtt=jnp.swapaxes(ws, 1, 2).astype(BF16),
                bmat=bmat, scw=scw, ccw=ccw, lg=small["conf_ln_g"][layer][None], lb=small["conf_ln_b"][layer][None])


def _local_step(x, target, big, small, gather_pending=False, core=None):
    saved = []
    for l in range(DEPTH):
        p = _layer_params(small, l)
        z, hb, q_r, q_t, k_t, v_t = _in_proj(x, small["norm_mix_g"][l][None], big["w_in"], l)
        if gather_pending and l == 0:
            late = ("w_out", "w_up", "w_down")
            y_abd, filled = _mixers_fwd(z, p, [big[k] for k in late], [(n, 0) for n in range(len(late))])
            big = {**big, **dict(zip(late, filled))}
            o_t, filled = _attn_fwd(q_r, k_t, v_t, [big[k] for k in _BIG], [(n, 1) for n in range(len(_BIG))])
            big = dict(zip(_BIG, filled))
        else:
            y_abd = _mixers_fwd(z, p)
            o_t = _attn_fwd(q_r, k_t, v_t)
        x1, yn = _out_proj(x, y_abd, o_t, small["mix_out_g"][l][None], big["w_out"], l)
        x2, pre, h2b = _ffn(x1, small["norm_ffn_g"][l][None], big["w_up"], big["w_down"], l)
        saved.append(dict(p=p, x0=x, z=z, hb=hb, q_r=q_r, q_t=q_t, k_t=k_t, v_t=v_t, o_t=o_t, x1=x1, yn=yn, pre=pre,
                          h2b=h2b, y_abd=y_abd))
        x = x2

    loss, dx, dxb, d_final = _loss_head(x, small["final_norm_g"][None], target)

    g = {k: [None] * DEPTH for k in ("w_in", "w_out", "w_up", "w_down", "norm_mix_g", "gmlp_v_g", "gmlp_w_s", "gmlp_b_s",
                                     "short_conv_w", "conf_conv_w", "conf_ln_g", "conf_ln_b", "mix_out_g", "norm_ffn_g")}
    tril = jnp.tril(jnp.ones((CHUNK, CHUNK), bool))
    early = {}
    ffn_sums = [[]] * DEPTH
    for l in reversed(range(DEPTH)):
        sv = saved[l]
        p = sv["p"]
        riding, sums = [], []
        if core is not None and l == 0:
            riding = [(k, 1) for k in _BIG]
            sums = _chip_sums(core, [g["w_in"][1], g["w_out"][1]]) + ffn_sums[1]
        dpre, dh, *got = _ffn_bwd(dxb, sv["pre"], big["w_up"], big["w_down"], l, sums)
        early.update(zip(riding, zip(sums, *got)))
        g["w_up"][l], g["w_down"][l] = _ffn_wgrad(sv["h2b"], sv["pre"], dpre, dxb)
        halves = [g["w_up"][l], g["w_down"][l]] if core is not None else []
        dx1, g["norm_ffn_g"][l], dyn, g["mix_out_g"][l], g["w_out"][l], *swapped = _out_proj_bwd(
            sv["x1"], small["norm_ffn_g"][l][None], dh, dx, sv["yn"], small["mix_out_g"][l][None], big["w_out"], l,
            halves)
        if halves:
            ffn_sums[l] = [_add_pairs(core, a, b) for a, b in zip(halves, *swapped)]
        riding, sums = [], []
        if core is not None and l == 0:
            riding, sums = [("w_up", 0), ("w_down", 0)], ffn_sums[0]
        (dza, dcb, dcd, do_r, do_t, dsum, dvg, dws, dbm, dscw, dccw, dlg, dlb, *got) = _mixers_bwd_a(
            sv["z"], dyn, sv["o_t"], sv["y_abd"], p, sums)
        early.update(zip(riding, zip(sums, *got)))
        dq_t, dk_t, dv_t = _attn_bwd(sv["q_r"], sv["q_t"], sv["k_t"], sv["v_t"], do_r, do_t, dsum)
        dz = _mixers_bwd_b(sv["z"], dza, dcb, dcd, dq_t, dk_t, dv_t, p)
        dx, dxb, g["norm_mix_g"][l] = _in_proj_bwd(sv["x0"], small["norm_mix_g"][l][None], dx1, dz, big["w_in"], l)
        g["w_in"][l] = _in_proj_wgrad(sv["hb"], dz)
        g["gmlp_v_g"][l] = dvg[0]
        g["gmlp_w_s"][l] = jnp.where(tril, dws, 0.0)
        g["gmlp_b_s"][l] = dbm.reshape(CHUNK, N_HEADS, HEAD_DIM).sum(-1).T
        g["short_conv_w"][l] = dscw[:SHORT_K]
        g["conf_conv_w"][l] = dccw[:CONF_K]
        g["conf_ln_g"][l] = dlg[0]
        g["conf_ln_b"][l] = dlb[0]
        g["norm_mix_g"][l] = g["norm_mix_g"][l][0]
        g["mix_out_g"][l] = g["mix_out_g"][l][0]
        g["norm_ffn_g"][l] = g["norm_ffn_g"][l][0]
    grads = {k: v if k in ("w_in", "w_out", "w_up", "w_down") else jnp.stack(v) for k, v in g.items()}
    grads["final_norm_g"] = d_final[0]
    return loss, dx, grads, early


_ANY = pl.BlockSpec(memory_space=pl.ANY)


def _mesh_place():
    x, y, c = lax.axis_index("x"), lax.axis_index("y"), lax.axis_index("c")
    chips = [(1 - x, y), (x, 1 - y), (1 - x, 1 - y)]
    return x, y, c, 2 * x + y, chips


def _gather_stages(bufs, parts, sems):
    ici_send, ici_recv, d2d_send, d2d_recv = sems
    x, y, c, me, chips = _mesh_place()
    blk = [2 * chip[0] + chip[1] for chip in chips]
    pairs = [(p, r) for p in range(len(parts)) for r in range(3)]

    def rows(p, block, half_of):
        k, layer = parts[p]
        half = bufs[k].shape[2] // 2
        return bufs[k].at[block, layer, pl.ds(half_of * half, half), :]

    def ici(p, r, block):
        return pltpu.make_async_remote_copy(
            src_ref=rows(p, me, c), dst_ref=rows(p, block, c), send_sem=ici_send.at[3 * p + r],
            recv_sem=ici_recv.at[3 * p + r], device_id=(chips[r][0], chips[r][1], c), device_id_type=MESH)

    def d2d(p, r, half_of):
        part = rows(p, blk[r], half_of)
        return pltpu.make_async_remote_copy(
            src_ref=part, dst_ref=part, send_sem=d2d_send.at[3 * p + r], recv_sem=d2d_recv.at[3 * p + r],
            device_id=(x, y, 1 - c), device_id_type=MESH)

    def start():
        for p, r in pairs:
            ici(p, r, me).start()

    def forward(p):
        for r in range(3):
            ici(p, r, blk[r]).wait_recv()
            d2d(p, r, c).start()

    def finish():
        for p, r in pairs:
            d2d(p, r, 1 - c).wait_recv()
        for p, r in pairs:
            ici(p, r, me).wait_send()
            d2d(p, r, c).wait_send()

    return start, forward, finish


def _gather_sems(parts):
    return [pltpu.SemaphoreType.DMA((3 * len(parts),)) for _ in range(4)]


def _gather_in_steps(bufs, parts, sems, n_steps):
    start, forward, finish = _gather_stages(bufs, parts, sems)
    i = pl.program_id(0)
    pl.when(i == 0)(start)
    for p in range(len(parts)):
        pl.when(i == n_steps * (2 * p + 3) // (2 * len(parts) + 2))(lambda p=p: forward(p))
    pl.when(i == n_steps - 1)(finish)


def _gather_first(bufs, parts, whole):
    n, m = len(bufs), len(whole)

    def body(*refs):
        whole_in, buf_out, whole_out = refs[n:n + m], refs[n + m:2 * n + m], refs[2 * n + m:2 * (n + m)]
        sems = refs[2 * (n + m):]
        send_sems, recv_sems, local_sems = sems[4:]
        x, y, c, me, chips = _mesh_place()
        start, forward, finish = _gather_stages(buf_out, parts, sems[:4])

        def push(k, r, block):
            return pltpu.make_async_remote_copy(
                src_ref=whole_in[k], dst_ref=whole_out[k].at[block], send_sem=send_sems.at[3 * k + r],
                recv_sem=recv_sems.at[3 * k + r], device_id=(chips[r][0], chips[r][1], c), device_id_type=MESH)

        local = [pltpu.make_async_copy(whole_in[k], whole_out[k].at[me], local_sems.at[k]) for k in range(m)]
        for cp in local:
            cp.start()
        start()
        for k in range(m):
            for r in range(3):
                push(k, r, me).start()
        for p in range(len(parts)):
            forward(p)
        for k in range(m):
            for r, chip in enumerate(chips):
                push(k, r, 2 * chip[0] + chip[1]).wait_recv()
        for k in range(m):
            for r in range(3):
                push(k, r, me).wait_send()
        finish()
        for cp in local:
            cp.wait()

    return pl.pallas_call(
        body, name="gather_first",
        in_specs=[_ANY] * (n + m), out_specs=[_ANY] * (n + m),
        out_shape=[jax.ShapeDtypeStruct(b.shape, b.dtype) for b in bufs]
                  + [jax.ShapeDtypeStruct((N_BLK,) + b.shape, b.dtype) for b in whole],
        input_output_aliases={k: k for k in range(n)},
        scratch_shapes=_gather_sems(parts) + [pltpu.SemaphoreType.DMA((3 * m,)), pltpu.SemaphoreType.DMA((3 * m,)),
                                              pltpu.SemaphoreType.DMA((m,))],
    )(*bufs, *whole)


def _swap_halves(gs):
    n = len(gs)

    def body(*refs):
        start, finish = _halves_stages(refs[:n], refs[n:2 * n], refs[2 * n:])
        start()
        finish()

    return pl.pallas_call(
        body, name="swap_halves", in_specs=[_ANY] * n, out_specs=[_ANY] * n,
        out_shape=_halves_shapes(gs), scratch_shapes=_halves_sems(n),
    )(*gs)


def _halves_stages(ins, outs, sems):
    send_sems, recv_sems = sems
    x, y, c, _, _ = _mesh_place()

    def copy(k):
        half = ins[k].shape[1] // 2
        return pltpu.make_async_remote_copy(
            src_ref=ins[k].at[:, pl.ds((1 - c) * half, half), :], dst_ref=outs[k],
            send_sem=send_sems.at[k], recv_sem=recv_sems.at[k], device_id=(x, y, 1 - c), device_id_type=MESH)

    def start():
        for k in range(len(ins)):
            copy(k).start()

    def finish():
        for k in range(len(ins)):
            copy(k).wait()

    return start, finish


def _halves_shapes(gs):
    return [jax.ShapeDtypeStruct((g.shape[0], g.shape[1] // 2, g.shape[2]), F32) for g in gs]


def _halves_sems(n):
    return [pltpu.SemaphoreType.DMA((n,)), pltpu.SemaphoreType.DMA((n,))]


def _owner_stages(ins, outs, sems):
    send_sems, recv_sems = sems
    x, y, c, me, chips = _mesh_place()
    pairs = [(k, r) for k in range(len(ins)) for r in range(3)]

    def remote(k, r, src_block, dst_block):
        return pltpu.make_async_remote_copy(
            src_ref=ins[k].at[src_block], dst_ref=outs[k].at[dst_block], send_sem=send_sems.at[3 * k + r],
            recv_sem=recv_sems.at[3 * k + r], device_id=(chips[r][0], chips[r][1], c), device_id_type=MESH)

    def start():
        for k, r in pairs:
            remote(k, r, 2 * chips[r][0] + chips[r][1], me).start()

    def finish():
        for k, r in pairs:
            remote(k, r, me, 2 * chips[r][0] + chips[r][1]).wait_recv()
        for k, r in pairs:
            remote(k, r, 2 * chips[r][0] + chips[r][1], me).wait_send()

    return start, finish


def _owner_sems(n):
    return [pltpu.SemaphoreType.DMA((3 * n,)), pltpu.SemaphoreType.DMA((3 * n,))]


def _owners_in_steps(ins, outs, sems, first, last):
    start, finish = _owner_stages(ins, outs, sems)
    pl.when(first)(start)
    pl.when(last)(finish)


def _send_to_owners(sums):
    n = len(sums)

    def body(*refs):
        start, finish = _owner_stages(refs[:n], refs[n:2 * n], refs[2 * n:])
        start()
        finish()

    return pl.pallas_call(
        body, name="send_to_owners", in_specs=[_ANY] * n, out_specs=[_ANY] * n,
        out_shape=[jax.ShapeDtypeStruct(s.shape, s.dtype) for s in sums],
        scratch_shapes=_owner_sems(n),
    )(*sums)


def _swap_reduced(fs):
    n = len(fs)

    def body(*refs):
        ins, outs, (send_sems, recv_sems) = refs[:n], refs[n:2 * n], refs[2 * n:]
        x, y, c, _, _ = _mesh_place()
        cps = [pltpu.make_async_remote_copy(src_ref=ins[k], dst_ref=outs[k], send_sem=send_sems.at[k],
                                            recv_sem=recv_sems.at[k], device_id=(x, y, 1 - c), device_id_type=MESH)
               for k in range(n)]
        for cp in cps:
            cp.start()
        for cp in cps:
            cp.wait()

    return pl.pallas_call(
        body, name="swap_reduced", in_specs=[_ANY] * n, out_specs=[_ANY] * n,
        out_shape=[jax.ShapeDtypeStruct(f.shape, F32) for f in fs],
        scratch_shapes=[pltpu.SemaphoreType.DMA((n,)), pltpu.SemaphoreType.DMA((n,))],
    )(*fs)


def _row_tile(rows):
    return min(rows, 256)


def _chip_sums(core, grads):
    if not grads:
        return []
    return [_add_pairs(core, a, b) for a, b in zip(grads, _swap_halves(grads))]


def _add_pairs(core, g, other):
    n, half, cols = other.shape
    t = _row_tile(half)
    per_half = half // t

    def body(c_ref, a_ref, b_ref, o_ref):
        o_ref[...] = (a_ref[...] + b_ref[...]).astype(BF16)

    spec = pl.BlockSpec((None, t, cols), lambda i, j, c_ref: (i, j, 0))
    return pl.pallas_call(
        body, name="add_pairs",
        grid_spec=pltpu.PrefetchScalarGridSpec(
            num_scalar_prefetch=1, grid=(n, per_half),
            in_specs=[pl.BlockSpec((None, t, cols), lambda i, j, c_ref: (i, c_ref[0] * per_half + j, 0)), spec],
            out_specs=spec),
        out_shape=jax.ShapeDtypeStruct(other.shape, BF16), compiler_params=_params(("parallel", "parallel")),
    )(core, g, other)


def _add_chips(me, s1, r2):
    _, r, cols = r2.shape
    t = _row_tile(r)

    def body(me_ref, s_ref, r_ref, o_ref):
        own = s_ref[...].astype(F32)
        parts = [jnp.where(me_ref[0] == k, own, r_ref[k].astype(F32)) for k in range(N_BLK)]
        o_ref[...] = ((parts[0] + parts[1]) + parts[2]) + parts[3]

    return pl.pallas_call(
        body, name="add_chips",
        grid_spec=pltpu.PrefetchScalarGridSpec(
            num_scalar_prefetch=1, grid=(r // t,),
            in_specs=[pl.BlockSpec((None, t, cols), lambda i, me_ref: (me_ref[0], i, 0)),
                      pl.BlockSpec((N_BLK, t, cols), lambda i, me_ref: (0, i, 0))],
            out_specs=pl.BlockSpec((t, cols), lambda i, me_ref: (i, 0))),
        out_shape=jax.ShapeDtypeStruct((r, cols), F32), compiler_params=_params(("parallel",)),
    )(me, s1, r2)


def _adamw(core, mine, other, w, m, v, layer, earlier=None):
    half, cols = mine.shape
    t = _row_tile(half)
    per_half = half // t
    c1 = 1.0 - ADAM_B1 ** ADAM_STEP
    c2 = 1.0 - ADAM_B2 ** ADAM_STEP

    def body(c_ref, a_ref, b_ref, w_ref, m_ref, v_ref, *rest):
        g_ref, d_ref, mo_ref, vo_ref = rest[-4:]
        gv = jnp.where(pl.program_id(0) // per_half == c_ref[0], a_ref[...], b_ref[...])
        g_ref[...] = gv
        m_new = ADAM_B1 * m_ref[...] + (1.0 - ADAM_B1) * gv
        v_new = ADAM_B2 * v_ref[...] + (1.0 - ADAM_B2) * (gv * gv)
        mo_ref[...] = m_new
        vo_ref[...] = v_new
        d_ref[...] = -ADAM_LR * ((m_new / c1) / (jnp.sqrt(v_new / c2) + ADAM_EPS) + ADAM_WD * w_ref[...])

    part = pl.BlockSpec((t, cols), lambda i, c_ref: (i % per_half, 0))
    spec = pl.BlockSpec((None, t, cols), lambda i, c_ref: (layer, i, 0))
    kept = [] if earlier is None else list(earlier)
    return pl.pallas_call(
        body, name="adamw",
        grid_spec=pltpu.PrefetchScalarGridSpec(
            num_scalar_prefetch=1, grid=(2 * per_half,),
            in_specs=[part, part, spec, spec, spec] + [_ANY] * len(kept), out_specs=[spec] * 4),
        out_shape=[jax.ShapeDtypeStruct(w.shape, F32)] * 4,
        input_output_aliases={6 + k: k for k in range(len(kept))},
        compiler_params=_params(("parallel",)),
    )(core, mine, other, w, m, v, *kept)


_REPLICATED = ("norm_mix_g", "gmlp_v_g", "gmlp_w_s", "gmlp_b_s", "conf_ln_g", "conf_ln_b", "mix_out_g", "norm_ffn_g",
               "final_norm_g")
_REP_SHAPES = {"norm_mix_g": (DEPTH, D_MODEL), "gmlp_v_g": (DEPTH, D_GROUP), "gmlp_w_s": (DEPTH, N_HEADS, CHUNK, CHUNK),
               "gmlp_b_s": (DEPTH, N_HEADS, CHUNK), "conf_ln_g": (DEPTH, D_GROUP), "conf_ln_b": (DEPTH, D_GROUP),
               "mix_out_g": (DEPTH, D_MODEL), "norm_ffn_g": (DEPTH, D_MODEL), "final_norm_g": (D_MODEL,)}
_BIG = ("w_in", "w_out", "w_up", "w_down")
_CONV_ROWS = 8
_REP_ROWS = 144
_SMALL_ROWS = 160
_CH_BLK = D_GROUP // N_BLK


def _pad_rows(flat, rows):
    pad = rows * D_MODEL - flat.shape[-1]
    flat = jnp.pad(flat, [(0, 0)] * (flat.ndim - 1) + [(0, pad)])
    return flat.reshape(flat.shape[:-1] + (rows, D_MODEL))


def _pack_small(scw, ccw, rep):
    lead = scw.shape[:-3]
    conv = jnp.concatenate([scw.reshape(lead + (-1,)), ccw.reshape(lead + (-1,))], axis=-1)
    flat = jnp.concatenate([rep[k].reshape(-1) for k in _REPLICATED])
    flat = jnp.broadcast_to(flat, lead + flat.shape)
    parts = [_pad_rows(conv, _CONV_ROWS), _pad_rows(flat, _REP_ROWS),
             jnp.zeros(lead + (_SMALL_ROWS - _CONV_ROWS - _REP_ROWS, D_MODEL), F32)]
    return jnp.concatenate(parts, axis=-2)


def _unpack_small(pk):
    out = {}
    conv = pk[:_CONV_ROWS].reshape(-1)
    n_s = DEPTH * SHORT_K * _CH_BLK
    out["short_conv_w"] = conv[:n_s].reshape(DEPTH, SHORT_K, _CH_BLK)
    out["conf_conv_w"] = conv[n_s:n_s + DEPTH * CONF_K * _CH_BLK].reshape(DEPTH, CONF_K, _CH_BLK)
    row = _CONV_ROWS
    flat = pk[row:row + _REP_ROWS].reshape(-1)
    at = 0
    for k in _REPLICATED:
        n = math.prod(_REP_SHAPES[k])
        out[k] = flat[at:at + n].reshape(_REP_SHAPES[k])
        at += n
    return out


def _conv_blocks(w):
    d, k, _ = w.shape
    return w.reshape(d, k, N_BLK, _CH_BLK).transpose(2, 0, 1, 3)


_WEIGHTS = ("norm_mix_g", "w_in", "gmlp_v_g", "gmlp_w_s", "gmlp_b_s", "short_conv_w", "conf_conv_w", "conf_ln_g",
            "conf_ln_b", "mix_out_g", "w_out", "norm_ffn_g", "w_up", "w_down", "final_norm_g")


def kernel(x, norm_mix_g, w_in, gmlp_v_g, gmlp_w_s, gmlp_b_s, short_conv_w, conf_conv_w, conf_ln_g, conf_ln_b, mix_out_g, w_out, norm_ffn_g, w_up, w_down, final_norm_g, loss_target, m_norm_mix_g, m_w_in, m_gmlp_v_g, m_gmlp_w_s, m_gmlp_b_s, m_short_conv_w, m_conf_conv_w, m_conf_ln_g, m_conf_ln_b, m_mix_out_g, m_w_out, m_norm_ffn_g, m_w_up, m_w_down, m_final_norm_g, v_norm_mix_g, v_w_in, v_gmlp_v_g, v_gmlp_w_s, v_gmlp_b_s, v_short_conv_w, v_conf_conv_w, v_conf_ln_g, v_conf_ln_b, v_mix_out_g, v_w_out, v_norm_ffn_g, v_w_up, v_w_down, v_final_norm_g):
    w = dict(norm_mix_g=norm_mix_g, w_in=w_in, gmlp_v_g=gmlp_v_g, gmlp_w_s=gmlp_w_s, gmlp_b_s=gmlp_b_s,
             short_conv_w=short_conv_w, conf_conv_w=conf_conv_w, conf_ln_g=conf_ln_g, conf_ln_b=conf_ln_b,
             mix_out_g=mix_out_g, w_out=w_out, norm_ffn_g=norm_ffn_g, w_up=w_up, w_down=w_down, final_norm_g=final_norm_g)
    m = dict(norm_mix_g=m_norm_mix_g, w_in=m_w_in, gmlp_v_g=m_gmlp_v_g, gmlp_w_s=m_gmlp_w_s, gmlp_b_s=m_gmlp_b_s,
             short_conv_w=m_short_conv_w, conf_conv_w=m_conf_conv_w, conf_ln_g=m_conf_ln_g, conf_ln_b=m_conf_ln_b,
             mix_out_g=m_mix_out_g, w_out=m_w_out, norm_ffn_g=m_norm_ffn_g, w_up=m_w_up, w_down=m_w_down,
             final_norm_g=m_final_norm_g)
    v = dict(norm_mix_g=v_norm_mix_g, w_in=v_w_in, gmlp_v_g=v_gmlp_v_g, gmlp_w_s=v_gmlp_w_s, gmlp_b_s=v_gmlp_b_s,
             short_conv_w=v_short_conv_w, conf_conv_w=v_conf_conv_w, conf_ln_g=v_conf_ln_g, conf_ln_b=v_conf_ln_b,
             mix_out_g=v_mix_out_g, w_out=v_w_out, norm_ffn_g=v_norm_ffn_g, w_up=v_w_up, w_down=v_w_down,
             final_norm_g=v_final_norm_g)
    core = lax.axis_index("c").astype(jnp.int32).reshape(1)
    me = (2 * lax.axis_index("x") + lax.axis_index("y")).astype(jnp.int32).reshape(1)

    conv_mine = _pad_rows(jnp.concatenate([short_conv_w.reshape(-1), conf_conv_w.reshape(-1)]), _CONV_ROWS)
    big = {k: _cast_into_slot(w[k], me, "cast_" + k) for k in _BIG}
    big["w_in"], conv_all = _gather_first([big["w_in"]], [(0, 0)], [conv_mine])
    conv_all = conv_all.reshape(N_BLK, -1)
    n_s = DEPTH * SHORT_K * _CH_BLK
    scw_all = conv_all[:, :n_s].reshape(N_BLK, DEPTH, SHORT_K, _CH_BLK)
    ccw_all = conv_all[:, n_s:n_s + DEPTH * CONF_K * _CH_BLK].reshape(N_BLK, DEPTH, CONF_K, _CH_BLK)
    small = {k: w[k] for k in _REPLICATED}
    small["short_conv_w"] = scw_all.transpose(1, 2, 0, 3).reshape(DEPTH, SHORT_K, D_GROUP)
    small["conf_conv_w"] = ccw_all.transpose(1, 2, 0, 3).reshape(DEPTH, CONF_K, D_GROUP)

    loss, dx, g, early = _local_step(x[0], loss_target[0], big, small, gather_pending=True, core=core)

    where = [(k, l) for k in _BIG for l in range(DEPTH)]
    late = [kl for kl in where if kl not in early]
    sums = _chip_sums(core, [g[k][l] for k, l in late]
                      + [_pack_small(_conv_blocks(g["short_conv_w"]), _conv_blocks(g["conf_conv_w"]), g)])
    sent = {**early, **dict(zip(late + ["small"], zip(sums, _send_to_owners(sums))))}
    mine = [_add_chips(me, *sent[kl]) for kl in where + ["small"]]
    other = _swap_reduced(mine)

    done = {}
    for n, (k, l) in enumerate(where):
        done[k] = _adamw(core, mine[n], other[n], w[k], m[k], v[k], l, done.get(k))
    small_own = [_pack_small(t["short_conv_w"], t["conf_conv_w"], t)[None] for t in (w, m, v)]
    small_done = [_unpack_small(a[0]) for a in _adamw(core, mine[-1], other[-1], *small_own, 0)]

    outs = [lax.psum(loss[0, 0], ("x", "y", "c")), dx[None]]
    for kind in range(4):
        outs += [done[k][kind] if k in _BIG else small_done[kind][k] for k in _WEIGHTS]
    return tuple(outs)
```

```python
import math

import jax
import jax.numpy as jnp
from jax import lax
from jax.experimental import pallas as pl
from jax.experimental.pallas import tpu as pltpu

F32 = jnp.float32
BF16 = jnp.bfloat16

D_MODEL = 1024
D_GROUP = 256
N_HEADS = 4
HEAD_DIM = 64
CHUNK = 128
D_IN = 2560
N_BLK = 4
W_IN_BLK = D_IN // N_BLK
D_FF_BLK = 1024
DEPTH = 2
EPS = 1e-6
HALO = 32
MIX_ROWS = 512
SHORT_K = 3
CONF_K = 31
ATT_TQ = 256
ATT_TK = 256
ATT_SCALE = 0.125
ATT_DEAD = -104.0
V7X_VMEM_LIMIT = 56 * 1024 * 1024

ADAM_LR, ADAM_B1, ADAM_B2, ADAM_EPS, ADAM_WD, ADAM_STEP = 0.001, 0.9, 0.999, 1e-08, 0.01, 10

MESH = pl.DeviceIdType.MESH


def _params(sem, vmem=None):
    return pltpu.CompilerParams(dimension_semantics=sem, vmem_limit_bytes=vmem)


def _tile(s, t):
    return min(s, t)


def _rsqrt_mean(v):
    return lax.rsqrt(jnp.mean(v * v, axis=-1, keepdims=True) + EPS)


def _sigmoid(v):
    return 1.0 / (1.0 + jnp.exp(-v))


_GELU_C = math.sqrt(2.0 / math.pi)


def _gelu_tanh(v):
    return jnp.tanh(_GELU_C * (v + 0.044715 * (v * v * v)))


def _gelu(v, t):
    return v * (0.5 * (1.0 + t))


def _gelu_grad(v, t):
    return 0.5 * (1.0 + t) + v * (0.5 * (1.0 - t * t) * _GELU_C * (1.0 + 3.0 * 0.044715 * (v * v)))


def _dot(a, b):
    return jnp.dot(a, b, preferred_element_type=F32)


def _dot_nt(a, b):
    return lax.dot_general(a, b, (((1,), (1,)), ((), ())), preferred_element_type=F32)


def _dot_tn(a, b):
    return lax.dot_general(a, b, (((0,), (0,)), ((), ())), preferred_element_type=F32)


def _cast_into_slot(w, me, name):
    n, r, c = w.shape
    tr = _tile(r, 256)

    def body(me_ref, w_ref, o_ref):
        o_ref[...] = w_ref[...].astype(BF16)

    return pl.pallas_call(
        body, name=name,
        grid_spec=pltpu.PrefetchScalarGridSpec(
            num_scalar_prefetch=1, grid=(n, r // tr),
            in_specs=[pl.BlockSpec((None, tr, c), lambda a, b, me_ref: (a, b, 0))],
            out_specs=pl.BlockSpec((None, None, tr, c), lambda a, b, me_ref: (me_ref[0], a, b, 0))),
        out_shape=jax.ShapeDtypeStruct((N_BLK,) + w.shape, BF16),
        compiler_params=_params(("parallel", "parallel")),
    )(me, w)


def _split_heads(xv, rows_ref, cols_ref):
    if rows_ref is not None:
        for h in range(N_HEADS):
            rows_ref[h] = xv[:, h * HEAD_DIM:(h + 1) * HEAD_DIM].astype(BF16)
    if cols_ref is not None:
        xt = xv.T
        for h in range(N_HEADS):
            cols_ref[h] = xt[h * HEAD_DIM:(h + 1) * HEAD_DIM, :].astype(BF16)


def _head_specs(t, s):
    rows = (pl.BlockSpec((N_HEADS, t, HEAD_DIM), lambda i: (0, i, 0)), jax.ShapeDtypeStruct((N_HEADS, s, HEAD_DIM), BF16))
    cols = (pl.BlockSpec((N_HEADS, HEAD_DIM, t), lambda i: (0, 0, i)), jax.ShapeDtypeStruct((N_HEADS, HEAD_DIM, s), BF16))
    return rows, cols


def _cols(ref, lo, hi):
    return ref[:, lo:hi].astype(F32)


def _pair_blocks(w_ref, wide_ref):
    @pl.when(pl.program_id(0) == 0)
    def _():
        for b in range(N_BLK):
            wide_ref[b // 2, :, (b % 2) * W_IN_BLK:(b % 2 + 1) * W_IN_BLK] = w_ref[b]


def _in_proj(x, g, w, layer):
    s = x.shape[0]
    t = _tile(s, 1024)

    def body(x_ref, g_ref, w_ref, z_ref, h_ref, qr_ref, qt_ref, kt_ref, vt_ref, wide_ref):
        _pair_blocks(w_ref, wide_ref)
        xv = x_ref[...]
        h = (xv * _rsqrt_mean(xv) * g_ref[...]).astype(BF16)
        h_ref[...] = h
        for n in range(N_BLK // 2):
            z_ref[:, 2 * n * W_IN_BLK:2 * (n + 1) * W_IN_BLK] = _dot(h, wide_ref[n]).astype(BF16)
        _split_heads(_cols(z_ref, 1280, 1536) * ATT_SCALE, qr_ref, qt_ref)
        _split_heads(_cols(z_ref, 1536, 1792), None, kt_ref)
        _split_heads(_cols(z_ref, 1792, 2048), None, vt_ref)

    rows, cols = _head_specs(t, s)
    return pl.pallas_call(
        body, name="in_proj", grid=(s // t,),
        in_specs=[pl.BlockSpec((t, D_MODEL), lambda i: (i, 0)), _full((1, D_MODEL)),
                  pl.BlockSpec((N_BLK, None, D_MODEL, W_IN_BLK), lambda i: (0, layer, 0, 0))],
        out_specs=[pl.BlockSpec((t, D_IN), lambda i: (i, 0)), pl.BlockSpec((t, D_MODEL), lambda i: (i, 0)),
                   rows[0], cols[0], cols[0], cols[0]],
        out_shape=[jax.ShapeDtypeStruct((s, D_IN), BF16), jax.ShapeDtypeStruct((s, D_MODEL), BF16),
                   rows[1], cols[1], cols[1], cols[1]],
        scratch_shapes=[pltpu.VMEM((N_BLK // 2, D_MODEL, 2 * W_IN_BLK), BF16)],
        compiler_params=_params(("arbitrary",), V7X_VMEM_LIMIT),
    )(x, g, w)


def _mix_a_fwd(z_ref, vg, wt_ref, bmat, t):
    zu = _cols(z_ref, 0, 256)
    zv = _cols(z_ref, 256, 512)
    tu = _gelu_tanh(zu)
    tv = _gelu_tanh(zv)
    u = _gelu(zu, tu)
    v = _gelu(zv, tv)
    rv = _rsqrt_mean(v)
    vh = v * rv
    vnb = (vh * vg).astype(BF16)
    head = lax.broadcasted_iota(jnp.int32, (CHUNK, D_GROUP), 1) // HEAD_DIM
    fs = []
    for c in range(t // CHUNK):
        vc = vnb[c * CHUNK:(c + 1) * CHUNK, :]
        fc = bmat
        for h in range(N_HEADS):
            fc = fc + jnp.where(head == h, _dot(wt_ref[h], vc), 0.0)
        fs.append(fc)
    f = jnp.concatenate(fs, axis=0) if len(fs) > 1 else fs[0]
    return (zu, tu), (zv, tv), u, rv, vh, vnb, f


def _windows(ext_ref, sh_ref, t):
    for b in range(1, 8):
        sh_ref[b - 1] = ext_ref[pl.ds(b, HALO + t - 8), :]

    def window(o):
        a, b = divmod(o, 8)
        return ext_ref[pl.ds(8 * a, t), :] if b == 0 else sh_ref[b - 1, pl.ds(8 * a, t), :]

    return window


def _mix_b_fwd(z_ref, zh_ref, first, scw_ref, ext_ref, t):
    gb = _cols(z_ref, 512, 768)
    uh = _cols(zh_ref, 768, 1024) * _cols(zh_ref, 1024, 1280)
    ext_ref[0:HALO, :] = jnp.where(first, 0.0, uh)
    ext_ref[HALO:HALO + t, :] = _cols(z_ref, 768, 1024) * _cols(z_ref, 1024, 1280)
    cv = jnp.zeros((t, D_GROUP), F32)
    for k in range(SHORT_K):
        cv = cv + scw_ref[k:k + 1, :] * ext_ref[pl.ds(HALO - (SHORT_K - 1) + k, t), :]
    return gb, cv


def _mix_d_fwd(z_ref, zh_ref, first, ccw_ref, lg, lb, ext_ref, sh_ref, t, cv=None):
    hh = _cols(zh_ref, 2048, 2304) * _sigmoid(_cols(zh_ref, 2304, 2560))
    ext_ref[0:HALO, :] = jnp.where(first, 0.0, hh)
    ext_ref[HALO:HALO + t, :] = _cols(z_ref, 2048, 2304) * _sigmoid(_cols(z_ref, 2304, 2560))
    window = _windows(ext_ref, sh_ref, t)
    if cv is None:
        cv = jnp.zeros((t, D_GROUP), F32)
        for k in range(CONF_K):
            cv = cv + ccw_ref[k:k + 1, :] * window(HALO - (CONF_K - 1) + k)
    xc = cv - jnp.mean(cv, axis=-1, keepdims=True)
    rs = lax.rsqrt(jnp.mean(xc * xc, axis=-1, keepdims=True) + EPS)
    xh = xc * rs
    ln = xh * lg + lb
    return xh, rs, ln, _sigmoid(ln), window, cv


def _mix_specs(t, s):
    per = t // HALO
    return [pl.BlockSpec((t, D_IN), lambda i: (i, 0)),
            pl.BlockSpec((HALO, D_IN), lambda i: (jnp.maximum(i * per - 1, 0), 0))]


def _full(shape):
    return pl.BlockSpec(shape, lambda i: (0,) * len(shape))


def _mixers_fwd(z, p, bufs=(), parts=()):
    s = z.shape[0]
    t = _tile(s, MIX_ROWS)
    nb = len(bufs)

    def body(z_ref, zh_ref, vg_ref, wt_ref, bm_ref, scw_ref, ccw_ref, lg_ref, lb_ref, *rest):
        y_ref = rest[nb]
        eb_ref, ed_ref, sh_ref = rest[2 * nb + 1:2 * nb + 4]
        if nb:
            _gather_in_steps(rest[nb + 1:2 * nb + 1], parts, rest[2 * nb + 4:], s // t)
        first = pl.program_id(0) == 0
        _, _, u, _, _, _, f = _mix_a_fwd(z_ref, vg_ref[...], wt_ref, bm_ref[...], t)
        ya = u * f
        y_ref[:, 0:256] = ya * _rsqrt_mean(ya)
        gb, cv = _mix_b_fwd(z_ref, zh_ref, first, scw_ref, eb_ref, t)
        yb = gb * cv
        y_ref[:, 256:512] = yb * _rsqrt_mean(yb)
        _, _, ln, sg, _, cvd = _mix_d_fwd(z_ref, zh_ref, first, ccw_ref, lg_ref[...], lb_ref[...], ed_ref, sh_ref, t)
        yd = ln * sg
        y_ref[:, 512:768] = yd * _rsqrt_mean(yd)
        y_ref[:, 768:1024] = cvd

    out = pl.pallas_call(
        body, name="mixers_fwd", grid=(s // t,),
        in_specs=_mix_specs(t, s) + [_full((1, D_GROUP)), _full((N_HEADS, CHUNK, CHUNK)), _full((CHUNK, D_GROUP)),
                                     _full((8, D_GROUP)), _full((32, D_GROUP)), _full((1, D_GROUP)), _full((1, D_GROUP))]
                 + [_ANY] * nb,
        out_specs=[pl.BlockSpec((t, D_MODEL), lambda i: (i, 0))] + [_ANY] * nb,
        out_shape=[jax.ShapeDtypeStruct((s, D_MODEL), F32)] + [jax.ShapeDtypeStruct(b.shape, b.dtype) for b in bufs],
        input_output_aliases={9 + k: 1 + k for k in range(nb)},
        scratch_shapes=[pltpu.VMEM((HALO + t, D_GROUP), F32), pltpu.VMEM((HALO + t, D_GROUP), F32),
                        pltpu.VMEM((7, HALO + t - 8, D_GROUP), F32)] + (_gather_sems(parts) if nb else []),
        compiler_params=_params(("arbitrary",) if nb else ("parallel",), V7X_VMEM_LIMIT),
    )(z, z, p["vg"], p["wt"], p["bmat"], p["scw"], p["ccw"], p["lg"], p["lb"], *bufs)
    return (out[0], out[1:]) if nb else out[0]


def _mixers_bwd_a(z, dyn, o_t, y_abd, p, sums=()):
    s = z.shape[0]
    t = _tile(s, MIX_ROWS)
    n_chunk = t // CHUNK
    ns = len(sums)

    def body(*refs):
        (z_ref, zh_ref, dyn_ref, ot_ref, cv_ref, vg_ref, wt_ref, wtt_ref, bm_ref, scw_ref, ccw_ref, lg_ref,
         lb_ref) = refs[:13]
        (dza_ref, dcb_ref, dcd_ref, dor_ref, dot_ref, ds_ref, dvg_ref, dws_ref, dbm_ref, dscw_ref, dccw_ref, dlg_ref,
         dlb_ref) = refs[13 + ns:26 + ns]
        eb_ref, ed_ref, sh_ref = refs[26 + 2 * ns:29 + 2 * ns]
        i = pl.program_id(0)
        first = i == 0
        if ns:
            _owners_in_steps(refs[13:13 + ns], refs[26 + ns:26 + 2 * ns], refs[29 + 2 * ns:], first, i == s // t - 1)

        @pl.when(first)
        def _():
            for r in (dvg_ref, dws_ref, dbm_ref, dscw_ref, dccw_ref, dlg_ref, dlb_ref):
                r[...] = jnp.zeros_like(r)

        def rms_bwd(y, dn):
            r = _rsqrt_mean(y)
            yn = y * r
            return r * (dn - yn * jnp.mean(dn * yn, axis=-1, keepdims=True))

        vg = vg_ref[...]
        gelu_u, gelu_v, u, rv, vh, vnb, f = _mix_a_fwd(z_ref, vg, wt_ref, bm_ref[...], t)
        dya = rms_bwd(u * f, _cols(dyn_ref, 0, 256))
        du = dya * f
        df = dya * u
        head = lax.broadcasted_iota(jnp.int32, (CHUNK, D_GROUP), 1) // HEAD_DIM
        dvns = []
        dbm = jnp.zeros((CHUNK, D_GROUP), F32)
        for c in range(n_chunk):
            dfc = df[c * CHUNK:(c + 1) * CHUNK, :]
            vc = vnb[c * CHUNK:(c + 1) * CHUNK, :]
            dbm = dbm + dfc
            dvn = jnp.zeros((CHUNK, D_GROUP), F32)
            for h in range(N_HEADS):
                dfh = jnp.where(head == h, dfc, 0.0).astype(BF16)
                dvn = dvn + _dot(wtt_ref[h], dfh)
                dws_ref[h] += _dot_nt(dfh, vc)
            dvns.append(dvn)
        dbm_ref[...] += dbm
        dvn = jnp.concatenate(dvns, axis=0) if n_chunk > 1 else dvns[0]
        dvg_ref[...] += jnp.sum(dvn * vh, axis=0, keepdims=True)
        dvh = dvn * vg
        dv = rv * (dvh - vh * jnp.mean(dvh * vh, axis=-1, keepdims=True))
        dza_ref[:, 0:256] = (du * _gelu_grad(*gelu_u)).astype(BF16)
        dza_ref[:, 256:512] = (dv * _gelu_grad(*gelu_v)).astype(BF16)

        gb, cv = _mix_b_fwd(z_ref, zh_ref, first, scw_ref, eb_ref, t)
        dyb = rms_bwd(gb * cv, _cols(dyn_ref, 256, 512))
        dza_ref[:, 512:768] = (dyb * cv).astype(BF16)
        dcb = dyb * gb
        dcb_ref[...] = dcb
        for k in range(SHORT_K):
            dscw_ref[k:k + 1, :] += jnp.sum(dcb * eb_ref[pl.ds(HALO - (SHORT_K - 1) + k, t), :], axis=0, keepdims=True)

        lg = lg_ref[...]
        xh, rs, ln, sg, window, _ = _mix_d_fwd(z_ref, zh_ref, first, ccw_ref, lg, lb_ref[...], ed_ref, sh_ref, t,
                                               cv_ref[...])
        dyd = rms_bwd(ln * sg, _cols(dyn_ref, 768, 1024))
        dln = dyd * (sg * (1.0 + ln * (1.0 - sg)))
        dlg_ref[...] += jnp.sum(dln * xh, axis=0, keepdims=True)
        dlb_ref[...] += jnp.sum(dln, axis=0, keepdims=True)
        dxh = dln * lg
        dcd = rs * (dxh - jnp.mean(dxh, axis=-1, keepdims=True) - xh * jnp.mean(dxh * xh, axis=-1, keepdims=True))
        dcd_ref[...] = dcd
        for k in range(CONF_K):
            dccw_ref[k:k + 1, :] += jnp.sum(dcd * window(HALO - (CONF_K - 1) + k), axis=0, keepdims=True)

        o = ot_ref[...].reshape(D_GROUP, t).T
        do = rms_bwd(o, _cols(dyn_ref, 512, 768))
        _split_heads(do, dor_ref, dot_ref)
        prod = do.astype(BF16).astype(F32) * o
        for h in range(N_HEADS):
            ds_ref[h] = jnp.sum(prod[:, h * HEAD_DIM:(h + 1) * HEAD_DIM], axis=1, keepdims=True)

    small = [(1, D_GROUP), (N_HEADS, CHUNK, CHUNK), (CHUNK, D_GROUP), (8, D_GROUP), (32, D_GROUP), (1, D_GROUP), (1, D_GROUP)]
    rows, cols = _head_specs(t, s)
    out = pl.pallas_call(
        body, name="mixers_bwd_a", grid=(s // t,),
        in_specs=_mix_specs(t, s) + [pl.BlockSpec((t, D_MODEL), lambda i: (i, 0)),
                                     pl.BlockSpec((N_HEADS, HEAD_DIM, t), lambda i: (0, 0, i)),
                                     pl.BlockSpec((t, D_GROUP), lambda i: (i, 3)),
                                     _full((1, D_GROUP)), _full((N_HEADS, CHUNK, CHUNK)), _full((N_HEADS, CHUNK, CHUNK)),
                                     _full((CHUNK, D_GROUP)), _full((8, D_GROUP)), _full((32, D_GROUP)),
                                     _full((1, D_GROUP)), _full((1, D_GROUP))] + [_ANY] * ns,
        out_specs=[pl.BlockSpec((t, 768), lambda i: (i, 0)), pl.BlockSpec((t, D_GROUP), lambda i: (i, 0)),
                   pl.BlockSpec((t, D_GROUP), lambda i: (i, 0)), rows[0], cols[0],
                   pl.BlockSpec((N_HEADS, t, 1), lambda i: (0, i, 0))]
                  + [_full(sh) for sh in small] + [_ANY] * ns,
        out_shape=[jax.ShapeDtypeStruct((s, 768), BF16), jax.ShapeDtypeStruct((s, D_GROUP), F32),
                   jax.ShapeDtypeStruct((s, D_GROUP), F32), rows[1], cols[1],
                   jax.ShapeDtypeStruct((N_HEADS, s, 1), F32)]
                  + [jax.ShapeDtypeStruct(sh, F32) for sh in small]
                  + [jax.ShapeDtypeStruct(a.shape, a.dtype) for a in sums],
        scratch_shapes=[pltpu.VMEM((HALO + t, D_GROUP), F32), pltpu.VMEM((HALO + t, D_GROUP), F32),
                        pltpu.VMEM((7, HALO + t - 8, D_GROUP), F32)] + (_owner_sems(ns) if ns else []),
        compiler_params=_params(("arbitrary",), V7X_VMEM_LIMIT),
    )(z, z, dyn, o_t, y_abd, p["vg"], p["wt"], p["wtt"], p["bmat"], p["scw"], p["ccw"], p["lg"], p["lb"], *sums)
    return tuple(out[:13]) + (out[13:],) if ns else out


def _mixers_bwd_b(z, dza, dcb, dcd, dq_t, dk_t, dv_t, p):
    s = z.shape[0]
    t = _tile(s, MIX_ROWS)
    per = t // HALO
    n_halo = s // HALO

    def body(z_ref, dza_ref, dcb_ref, dcbn_ref, dcd_ref, dcdn_ref, dq_ref, dk_ref, dv_ref, scw_ref, ccw_ref,
             dz_ref, eb_ref, ed_ref, sh_ref):
        last = pl.program_id(0) == pl.num_programs(0) - 1
        dz_ref[:, 0:768] = dza_ref[...]
        eb_ref[0:t, :] = dcb_ref[...]
        eb_ref[t:t + HALO, :] = jnp.where(last, 0.0, dcbn_ref[...])
        du = jnp.zeros((t, D_GROUP), F32)
        for k in range(SHORT_K):
            du = du + scw_ref[k:k + 1, :] * eb_ref[pl.ds(SHORT_K - 1 - k, t), :]
        dz_ref[:, 768:1024] = (du * _cols(z_ref, 1024, 1280)).astype(BF16)
        dz_ref[:, 1024:1280] = (du * _cols(z_ref, 768, 1024)).astype(BF16)
        for n, r in enumerate((dq_ref, dk_ref, dv_ref)):
            dz_ref[:, 1280 + 256 * n:1536 + 256 * n] = r[...].reshape(D_GROUP, t).T.astype(BF16)
        ed_ref[0:t, :] = dcd_ref[...]
        ed_ref[t:t + HALO, :] = jnp.where(last, 0.0, dcdn_ref[...])
        window = _windows(ed_ref, sh_ref, t)
        dh = jnp.zeros((t, D_GROUP), F32)
        for k in range(CONF_K):
            dh = dh + ccw_ref[k:k + 1, :] * window(CONF_K - 1 - k)
        a = _cols(z_ref, 2048, 2304)
        sg = _sigmoid(_cols(z_ref, 2304, 2560))
        dz_ref[:, 2048:2304] = (dh * sg).astype(BF16)
        dz_ref[:, 2304:2560] = (dh * a * sg * (1.0 - sg)).astype(BF16)

    nxt = lambda i: (jnp.minimum((i + 1) * per, n_halo - 1), 0)
    tr = pl.BlockSpec((N_HEADS, HEAD_DIM, t), lambda i: (0, 0, i))
    return pl.pallas_call(
        body, name="mixers_bwd_b", grid=(s // t,),
        in_specs=[pl.BlockSpec((t, D_IN), lambda i: (i, 0)), pl.BlockSpec((t, 768), lambda i: (i, 0)),
                  pl.BlockSpec((t, D_GROUP), lambda i: (i, 0)), pl.BlockSpec((HALO, D_GROUP), nxt),
                  pl.BlockSpec((t, D_GROUP), lambda i: (i, 0)), pl.BlockSpec((HALO, D_GROUP), nxt),
                  tr, tr, tr, _full((8, D_GROUP)), _full((32, D_GROUP))],
        out_specs=pl.BlockSpec((t, D_IN), lambda i: (i, 0)),
        out_shape=jax.ShapeDtypeStruct((s, D_IN), BF16),
        scratch_shapes=[pltpu.VMEM((HALO + t, D_GROUP), F32), pltpu.VMEM((HALO + t, D_GROUP), F32),
                        pltpu.VMEM((7, HALO + t - 8, D_GROUP), F32)],
        compiler_params=_params(("parallel",), V7X_VMEM_LIMIT),
    )(z, dza, dcb, dcb, dcd, dcd, dq_t, dk_t, dv_t, p["scw"], p["ccw"])


def _split_bf16(v):
    hi = v.astype(BF16)
    return hi, (v - hi.astype(F32)).astype(BF16)


def _att_scores(qs, kts, carries, tri, mask):
    zs = [_dot(q, kt) for q, kt in zip(qs, kts)]
    lms, lbs, parts = [], [], []
    for z in zs:
        soft = jnp.log(1.0 + jnp.exp(-jnp.abs(z)))
        lm = -(jnp.maximum(z, 0.0) + soft)
        lbs.append(lm + z)
        if mask is not None:
            lm = jnp.where(mask, lm, 0.0)
        lms.append(lm)
        parts.append(_split_bf16(lm))
    rights = [_dot(hi, tri) + _dot(lo, tri) for hi, lo in parts]
    ws = []
    for lb, right, carry in zip(lbs, rights, carries):
        w = jnp.exp(lb + right + carry)
        ws.append(w if mask is None else jnp.where(mask, w, 0.0))
    return ws, lbs, [jnp.sum(lm, axis=1, keepdims=True) for lm in lms]


def _att_consts(i):
    j_hi = ((i + 1) * ATT_TQ - 1) // ATT_TK
    row = lax.broadcasted_iota(jnp.int32, (ATT_TQ, ATT_TK), 0) + i * ATT_TQ
    col = lax.broadcasted_iota(jnp.int32, (ATT_TQ, ATT_TK), 1) + j_hi * ATT_TK
    r_i = lax.broadcasted_iota(jnp.int32, (ATT_TK, ATT_TK), 0)
    c_i = lax.broadcasted_iota(jnp.int32, (ATT_TK, ATT_TK), 1)
    return j_hi, col < row, r_i, c_i


def _att_alive(j, carries):
    top = carries[0]
    for c in carries[1:]:
        top = jnp.maximum(top, c)
    return jnp.logical_and(j >= 0, jnp.max(top) > ATT_DEAD)


def _attn_fwd(q_r, k_t, v_t, bufs=(), parts=()):
    s = q_r.shape[1]
    nb = len(bufs)

    def body(q_ref, kt_ref, vt_ref, *rest):
        o_ref = rest[nb]
        if nb:
            _gather_in_steps(rest[nb + 1:2 * nb + 1], parts, rest[2 * nb + 1:], s // ATT_TQ)
        j_hi, mask, r_i, c_i = _att_consts(pl.program_id(0))
        tri = (r_i > c_i).astype(BF16)

        heads = range(N_HEADS)

        def tiles(j, carries, accs, mask):
            cols = pl.ds(pl.multiple_of(j * ATT_TK, ATT_TK), ATT_TK)
            ws, _, tots = _att_scores([q_ref[h] for h in heads], [kt_ref[h, :, cols] for h in heads], carries, tri, mask)
            accs = [acc + _dot_nt(vt_ref[h, :, cols], w.astype(BF16)) for h, acc, w in zip(heads, accs, ws)]
            return [c + t for c, t in zip(carries, tots)], accs

        state = tiles(j_hi, [jnp.zeros((ATT_TQ, 1), F32)] * N_HEADS, [jnp.zeros((HEAD_DIM, ATT_TQ), F32)] * N_HEADS, mask)

        def cond(c):
            return _att_alive(c[0], c[1])

        def step(c):
            return (c[0] - 1,) + tuple(tiles(c[0], c[1], c[2], None))

        _, _, accs = lax.while_loop(cond, step, (j_hi - 1,) + tuple(state))
        for h in heads:
            o_ref[h] = accs[h]

    whole = pl.BlockSpec((N_HEADS, HEAD_DIM, s), lambda i: (0, 0, 0), pipeline_mode=pl.Buffered(1))
    out = pl.pallas_call(
        body, name="attn_fwd", grid=(s // ATT_TQ,),
        in_specs=[pl.BlockSpec((N_HEADS, ATT_TQ, HEAD_DIM), lambda i: (0, i, 0)), whole, whole] + [_ANY] * nb,
        out_specs=[pl.BlockSpec((N_HEADS, HEAD_DIM, ATT_TQ), lambda i: (0, 0, i))] + [_ANY] * nb,
        out_shape=[jax.ShapeDtypeStruct((N_HEADS, HEAD_DIM, s), F32)] + [jax.ShapeDtypeStruct(b.shape, b.dtype) for b in bufs],
        input_output_aliases={3 + k: 1 + k for k in range(nb)},
        scratch_shapes=_gather_sems(parts) if nb else [],
        compiler_params=_params(("arbitrary",), V7X_VMEM_LIMIT),
    )(q_r, k_t, v_t, *bufs)
    return (out[0], out[1:]) if nb else out[0]


ATT_BWD_HEADS = 2
ATT_BWD_BLOCKS = 2


def _attn_bwd(q_r, q_t, k_t, v_t, do_r, do_t, dsum):
    assert ATT_TQ == ATT_TK
    s = q_r.shape[1]
    hps = ATT_BWD_HEADS
    nbq = min(ATT_BWD_BLOCKS, s // ATT_TQ)
    step_rows = nbq * ATT_TQ
    lanes = [(h, b) for h in range(hps) for b in range(nbq)]

    def body(q_ref, qt_ref, kt_ref, vt_ref, do_ref, dot_ref, ds_ref, dq_ref, dk_ref, dv_ref):
        i = pl.program_id(1)

        @pl.when(i == 0)
        def _():
            dk_ref[...] = jnp.zeros_like(dk_ref)
            dv_ref[...] = jnp.zeros_like(dv_ref)

        _, mask, r_i, c_i = _att_consts(0)
        tri_r = (r_i > c_i).astype(BF16)
        tri_ge = (r_i >= c_i).astype(BF16)
        rows_of = [pl.ds(b * ATT_TQ, ATT_TQ) for _, b in lanes]

        def tiles(t, carries, gsums, accs, mask):
            js = [nbq * i + b - t for _, b in lanes]
            live = [None if mask is not None or b == nbq - 1 else j >= 0 for (_, b), j in zip(lanes, js)]
            cols = [pl.ds(pl.multiple_of(jnp.maximum(j, 0) * ATT_TK, ATT_TK), ATT_TK) for j in js]
            kts = [kt_ref[h, :, c] for (h, _), c in zip(lanes, cols)]
            das = [_dot(do_ref[h, r, :], vt_ref[h, :, c]) for (h, _), r, c in zip(lanes, rows_of, cols)]
            ws, lbs, tots = _att_scores([q_ref[h, r, :] for (h, _), r in zip(lanes, rows_of)], kts, carries, tri_r, mask)
            ws = [w if ok is None else jnp.where(ok, w, 0.0) for w, ok in zip(ws, live)]
            tots = [tot if ok is None else jnp.where(ok, tot, -1e30) for tot, ok in zip(tots, live)]
            wbs = [w.astype(BF16) for w in ws]
            gs = [wb.astype(F32) * da for wb, da in zip(wbs, das)]
            parts = [_split_bf16(g) for g in gs]
            sfx = [_dot(hi, tri_ge) + _dot(lo, tri_ge) for hi, lo in parts]
            for n, (h, _) in enumerate(lanes):
                dv_ref[h, :, cols[n]] += _dot(dot_ref[h, :, rows_of[n]], wbs[n])
            dzs = []
            for n, (h, _) in enumerate(lanes):
                left = ds_ref[h, rows_of[n], :] - gsums[n] - sfx[n]
                dz = gs[n] - jnp.exp(lbs[n]) * (gs[n] + left)
                if mask is not None:
                    dz = jnp.where(mask, dz, 0.0)
                elif live[n] is not None:
                    dz = jnp.where(live[n], dz, 0.0)
                dzs.append(dz.astype(BF16))
            for n, (h, _) in enumerate(lanes):
                dk_ref[h, :, cols[n]] += _dot(qt_ref[h, :, rows_of[n]], dzs[n])
            accs = [acc + _dot_nt(kt, dz) for acc, kt, dz in zip(accs, kts, dzs)]
            gsums = [gsum + jnp.sum(g, axis=1, keepdims=True) for gsum, g in zip(gsums, gs)]
            return [c + tot for c, tot in zip(carries, tots)], gsums, accs

        col0 = [jnp.zeros((ATT_TQ, 1), F32)] * len(lanes)
        state = tiles(0, col0, col0, [jnp.zeros((HEAD_DIM, ATT_TQ), F32)] * len(lanes), mask)

        def cond(c):
            return _att_alive(nbq * i + nbq - 1 - c[0], c[1])

        def step(c):
            return (c[0] + 1,) + tuple(tiles(c[0], c[1], c[2], c[3], None))

        _, _, _, accs = lax.while_loop(cond, step, (1,) + tuple(state))
        for n, (h, _) in enumerate(lanes):
            dq_ref[h, :, rows_of[n]] = accs[n] * ATT_SCALE

    whole = pl.BlockSpec((hps, HEAD_DIM, s), lambda g, i: (g, 0, 0))
    whole_in = pl.BlockSpec((hps, HEAD_DIM, s), lambda g, i: (g, 0, 0), pipeline_mode=pl.Buffered(1))
    rows = pl.BlockSpec((hps, step_rows, HEAD_DIM), lambda g, i: (g, i, 0))
    cols = pl.BlockSpec((hps, HEAD_DIM, step_rows), lambda g, i: (g, 0, i))
    shape = jax.ShapeDtypeStruct((N_HEADS, HEAD_DIM, s), F32)
    return pl.pallas_call(
        body, name="attn_bwd", grid=(N_HEADS // hps, s // step_rows),
        in_specs=[rows, cols, whole_in, whole_in, rows, cols, pl.BlockSpec((hps, step_rows, 1), lambda g, i: (g, i, 0))],
        out_specs=[cols, whole, whole],
        out_shape=[shape, shape, shape],
        compiler_params=_params(("parallel", "arbitrary"), V7X_VMEM_LIMIT),
    )(q_r, q_t, k_t, v_t, do_r, do_t, dsum)


def _out_proj(x, y_abd, o_t, gain, w, layer):
    s = x.shape[0]
    t = _tile(s, 1024)

    def body(x_ref, y_ref, ot_ref, g_ref, w_ref, x1_ref, yn_ref):
        o = ot_ref[...].reshape(D_GROUP, t).T
        groups = [y_ref[:, 0:256], y_ref[:, 256:512], o * _rsqrt_mean(o), y_ref[:, 512:768]]
        g = g_ref[...]
        acc = None
        for b, yn in enumerate(groups):
            cols = slice(256 * b, 256 * (b + 1))
            yn_ref[:, cols] = yn.astype(BF16)
            part = _dot((yn * g[:, cols]).astype(BF16), w_ref[b])
            acc = part if acc is None else acc + part
        x1_ref[...] = x_ref[...] + acc

    return pl.pallas_call(
        body, name="out_proj", grid=(s // t,),
        in_specs=[pl.BlockSpec((t, D_MODEL), lambda i: (i, 0)), pl.BlockSpec((t, 768), lambda i: (i, 0)),
                  pl.BlockSpec((N_HEADS, HEAD_DIM, t), lambda i: (0, 0, i)), _full((1, D_MODEL)),
                  pl.BlockSpec((N_BLK, None, D_GROUP, D_MODEL), lambda i: (0, layer, 0, 0))],
        out_specs=[pl.BlockSpec((t, D_MODEL), lambda i: (i, 0)), pl.BlockSpec((t, D_MODEL), lambda i: (i, 0))],
        out_shape=[jax.ShapeDtypeStruct((s, D_MODEL), F32), jax.ShapeDtypeStruct((s, D_MODEL), BF16)],
        compiler_params=_params(("parallel",), V7X_VMEM_LIMIT),
    )(x, y_abd, o_t, gain, w)


def _out_proj_bwd(x1, g_ffn, dh, dx2, yn, gain, w, layer, halves=()):
    s = dx2.shape[0]
    t = _tile(s, 512)
    nh = len(halves)

    def body(x_ref, gf_ref, dh_ref, dx2_ref, yn_ref, g_ref, w_ref, *rest):
        dx1_ref, dgf_ref, dyn_ref, dg_ref, dw_ref = rest[nh:nh + 5]
        if nh:
            start, finish = _halves_stages(rest[:nh], rest[nh + 5:2 * nh + 5], rest[2 * nh + 5:])
            pl.when(pl.program_id(0) == 0)(start)
            pl.when(pl.program_id(0) == s // t - 1)(finish)

        @pl.when(pl.program_id(0) == 0)
        def _():
            dg_ref[...] = jnp.zeros_like(dg_ref)
            dw_ref[...] = jnp.zeros_like(dw_ref)
            dgf_ref[...] = jnp.zeros_like(dgf_ref)

        dx1, dgf = _rms_bwd_rows(x_ref[...], gf_ref[...], dh_ref[...], dx2_ref[...])
        dx1_ref[...] = dx1
        dgf_ref[...] += dgf
        dxb = dx1.astype(BF16)
        g = g_ref[...]
        yn = yn_ref[...].astype(F32)
        yg = (yn * g).astype(BF16)
        for b in range(N_BLK):
            cols = slice(256 * b, 256 * (b + 1))
            dyg = _dot_nt(dxb, w_ref[b])
            dw_ref[b] += _dot_tn(yg[:, cols], dxb)
            dg_ref[:, cols] += jnp.sum(dyg * yn[:, cols], axis=0, keepdims=True)
            dyn_ref[:, cols] = (dyg * g[:, cols]).astype(BF16)

    row = pl.BlockSpec((t, D_MODEL), lambda i: (i, 0))
    vec = _full((1, D_MODEL))
    out = pl.pallas_call(
        body, name="out_proj_bwd", grid=(s // t,),
        in_specs=[row, vec, row, row, row, vec, pl.BlockSpec((N_BLK, None, D_GROUP, D_MODEL), lambda i: (0, layer, 0, 0))]
                 + [_ANY] * nh,
        out_specs=[row, vec, row, vec, _full((N_BLK, D_GROUP, D_MODEL))] + [_ANY] * nh,
        out_shape=[jax.ShapeDtypeStruct((s, D_MODEL), F32), jax.ShapeDtypeStruct((1, D_MODEL), F32),
                   jax.ShapeDtypeStruct((s, D_MODEL), BF16), jax.ShapeDtypeStruct((1, D_MODEL), F32),
                   jax.ShapeDtypeStruct((N_BLK, D_GROUP, D_MODEL), F32)] + _halves_shapes(halves),
        scratch_shapes=_halves_sems(nh) if nh else [],
        compiler_params=_params(("arbitrary",), V7X_VMEM_LIMIT),
    )(x1, g_ffn, dh, dx2, yn, gain, w, *halves)
    return tuple(out[:5]) + (out[5:],) if nh else out


def _ffn(x, g, w_up, w_down, layer):
    s = x.shape[0]
    t = _tile(s, 1024)

    def body(x_ref, g_ref, wu_ref, wd_ref, x2_ref, p_ref, h_ref):
        @pl.when(pl.program_id(1) == 0)
        def _():
            xv = x_ref[...]
            h_ref[...] = (xv * _rsqrt_mean(xv) * g_ref[...]).astype(BF16)
            x2_ref[...] = xv

        pre = _dot(h_ref[...], wu_ref[...])
        p_ref[...] = pre.astype(BF16)
        a = jnp.maximum(pre, 0.0)
        x2_ref[...] += _dot((a * a).astype(BF16), wd_ref[...])

    wspec = pl.BlockSpec((None, None, D_MODEL, D_FF_BLK), lambda i, j: (j, layer, 0, 0))
    row = pl.BlockSpec((t, D_MODEL), lambda i, j: (i, 0))
    return pl.pallas_call(
        body, name="ffn", grid=(s // t, N_BLK),
        in_specs=[row, pl.BlockSpec((1, D_MODEL), lambda i, j: (0, 0)), wspec, wspec],
        out_specs=[row, pl.BlockSpec((t, D_FF_BLK), lambda i, j: (i, j)), row],
        out_shape=[jax.ShapeDtypeStruct((s, D_MODEL), F32), jax.ShapeDtypeStruct((s, N_BLK * D_FF_BLK), BF16),
                   jax.ShapeDtypeStruct((s, D_MODEL), BF16)],
        compiler_params=_params(("parallel", "arbitrary"), V7X_VMEM_LIMIT),
    )(x, g, w_up, w_down)


def _loss_head(x, g, target):
    s = x.shape[0]
    t = _tile(s, 512)

    def body(x_ref, g_ref, t_ref, l_ref, dx_ref, dxb_ref, dg_ref):
        @pl.when(pl.program_id(0) == 0)
        def _():
            l_ref[...] = jnp.zeros_like(l_ref)
            dg_ref[...] = jnp.zeros_like(dg_ref)

        xv = x_ref[...]
        g = g_ref[...]
        r = _rsqrt_mean(xv)
        xh = xv * r
        err = xh * g - t_ref[...]
        l_ref[...] += 0.5 * jnp.sum(jnp.mean(err * err, axis=-1, keepdims=True), axis=0, keepdims=True)
        dy = err * (1.0 / D_MODEL)
        dg_ref[...] += jnp.sum(dy * xh, axis=0, keepdims=True)
        dxh = dy * g
        dx = r * (dxh - xh * jnp.mean(dxh * xh, axis=-1, keepdims=True))
        dx_ref[...] = dx
        dxb_ref[...] = dx.astype(BF16)

    row = pl.BlockSpec((t, D_MODEL), lambda i: (i, 0))
    return pl.pallas_call(
        body, name="loss_head", grid=(s // t,),
        in_specs=[row, _full((1, D_MODEL)), row],
        out_specs=[_full((1, 128)), row, row, _full((1, D_MODEL))],
        out_shape=[jax.ShapeDtypeStruct((1, 128), F32), jax.ShapeDtypeStruct((s, D_MODEL), F32),
                   jax.ShapeDtypeStruct((s, D_MODEL), BF16), jax.ShapeDtypeStruct((1, D_MODEL), F32)],
        compiler_params=_params(("arbitrary",)),
    )(x, g, target)


def _rms_bwd_rows(xv, g, dh, dres):
    r = _rsqrt_mean(xv)
    xh = xv * r
    dxh = dh * g
    dx = dres + r * (dxh - xh * jnp.mean(dxh * xh, axis=-1, keepdims=True))
    return dx, jnp.sum(dh * xh, axis=0, keepdims=True)


def _ffn_bwd(dxb, p, w_up, w_down, layer, sums=()):
    s = dxb.shape[0]
    t = _tile(s, 1024)
    ns = len(sums)

    def body(dx_ref, p_ref, wu_ref, wd_ref, *rest):
        dp_ref, dh_ref = rest[ns:ns + 2]
        if ns:
            i, j = pl.program_id(0), pl.program_id(1)
            _owners_in_steps(rest[:ns], rest[ns + 2:2 * ns + 2], rest[2 * ns + 2:],
                             jnp.logical_and(i == 0, j == 0), jnp.logical_and(i == s // t - 1, j == N_BLK - 1))
        da = _dot_nt(dx_ref[...], wd_ref[...])
        a = jnp.maximum(p_ref[...].astype(F32), 0.0)
        dp = (da * (2.0 * a)).astype(BF16)
        dp_ref[...] = dp
        dh = _dot_nt(dp, wu_ref[...])

        @pl.when(pl.program_id(1) == 0)
        def _():
            dh_ref[...] = dh

        @pl.when(pl.program_id(1) != 0)
        def _():
            dh_ref[...] += dh

    wspec = pl.BlockSpec((None, None, D_MODEL, D_FF_BLK), lambda i, j: (j, layer, 0, 0))
    row = pl.BlockSpec((t, D_MODEL), lambda i, j: (i, 0))
    blk = pl.BlockSpec((t, D_FF_BLK), lambda i, j: (i, j))
    out = pl.pallas_call(
        body, name="ffn_bwd", grid=(s // t, N_BLK),
        in_specs=[row, blk, wspec, wspec] + [_ANY] * ns, out_specs=[blk, row] + [_ANY] * ns,
        out_shape=[jax.ShapeDtypeStruct((s, N_BLK * D_FF_BLK), BF16), jax.ShapeDtypeStruct((s, D_MODEL), F32)]
                  + [jax.ShapeDtypeStruct(a.shape, a.dtype) for a in sums],
        scratch_shapes=_owner_sems(ns) if ns else [],
        compiler_params=_params(("arbitrary" if ns else "parallel", "arbitrary"), V7X_VMEM_LIMIT),
    )(dxb, p, w_up, w_down, *sums)
    return (out[0], out[1], out[2:]) if ns else out


def _ffn_wgrad(hb, p, dp, dxb):
    s = hb.shape[0]
    t = _tile(s, 1024)

    def body(h_ref, p_ref, dp_ref, dx_ref, du_ref, dd_ref):
        @pl.when(pl.program_id(1) == 0)
        def _():
            du_ref[...] = jnp.zeros_like(du_ref)
            dd_ref[...] = jnp.zeros_like(dd_ref)

        a = jnp.maximum(p_ref[...].astype(F32), 0.0)
        du_ref[...] += _dot_tn(h_ref[...], dp_ref[...])
        dd_ref[...] += _dot_tn((a * a).astype(BF16), dx_ref[...])

    row = pl.BlockSpec((t, D_MODEL), lambda j, i: (i, 0))
    blk = pl.BlockSpec((t, D_FF_BLK), lambda j, i: (i, j))
    out = pl.BlockSpec((None, D_MODEL, D_FF_BLK), lambda j, i: (j, 0, 0))
    shape = jax.ShapeDtypeStruct((N_BLK, D_MODEL, D_FF_BLK), F32)
    return pl.pallas_call(
        body, name="ffn_wgrad", grid=(N_BLK, s // t),
        in_specs=[row, blk, blk, row], out_specs=[out, out], out_shape=[shape, shape],
        compiler_params=_params(("parallel", "arbitrary"), V7X_VMEM_LIMIT),
    )(hb, p, dp, dxb)


def _in_proj_bwd(x, g, dx1, dz, w, layer):
    s = x.shape[0]
    t = _tile(s, 512)

    def body(x_ref, g_ref, dx1_ref, dz_ref, w_ref, dx0_ref, dxb_ref, dg_ref, wide_ref):
        _pair_blocks(w_ref, wide_ref)

        @pl.when(pl.program_id(0) == 0)
        def _():
            dg_ref[...] = jnp.zeros_like(dg_ref)

        dh = _dot_nt(dz_ref[:, 0:2 * W_IN_BLK], wide_ref[0])
        for n in range(1, N_BLK // 2):
            dh = dh + _dot_nt(dz_ref[:, 2 * n * W_IN_BLK:2 * (n + 1) * W_IN_BLK], wide_ref[n])
        dx, dg = _rms_bwd_rows(x_ref[...], g_ref[...], dh, dx1_ref[...])
        dx0_ref[...] = dx
        dxb_ref[...] = dx.astype(BF16)
        dg_ref[...] += dg

    row = pl.BlockSpec((t, D_MODEL), lambda i: (i, 0))
    return pl.pallas_call(
        body, name="in_proj_bwd", grid=(s // t,),
        in_specs=[row, _full((1, D_MODEL)), row, pl.BlockSpec((t, D_IN), lambda i: (i, 0)),
                  pl.BlockSpec((N_BLK, None, D_MODEL, W_IN_BLK), lambda i: (0, layer, 0, 0))],
        out_specs=[row, row, _full((1, D_MODEL))],
        out_shape=[jax.ShapeDtypeStruct((s, D_MODEL), F32), jax.ShapeDtypeStruct((s, D_MODEL), BF16),
                   jax.ShapeDtypeStruct((1, D_MODEL), F32)],
        scratch_shapes=[pltpu.VMEM((N_BLK // 2, D_MODEL, 2 * W_IN_BLK), BF16)],
        compiler_params=_params(("arbitrary",), V7X_VMEM_LIMIT),
    )(x, g, dx1, dz, w)


def _in_proj_wgrad(hb, dz):
    s = hb.shape[0]
    t = _tile(s, 1024)

    def body(h_ref, dz_ref, dw_ref, wide_ref):
        @pl.when(pl.program_id(0) == 0)
        def _():
            wide_ref[...] = jnp.zeros_like(wide_ref)

        h = h_ref[...]
        for n in range(N_BLK // 2):
            wide_ref[n] += _dot_tn(h, dz_ref[:, 2 * n * W_IN_BLK:2 * (n + 1) * W_IN_BLK])

        @pl.when(pl.program_id(0) == s // t - 1)
        def _():
            for b in range(N_BLK):
                dw_ref[b] = wide_ref[b // 2, :, (b % 2) * W_IN_BLK:(b % 2 + 1) * W_IN_BLK]

    return pl.pallas_call(
        body, name="in_proj_wgrad", grid=(s // t,),
        in_specs=[pl.BlockSpec((t, D_MODEL), lambda i: (i, 0)), pl.BlockSpec((t, D_IN), lambda i: (i, 0))],
        out_specs=_full((N_BLK, D_MODEL, W_IN_BLK)),
        out_shape=jax.ShapeDtypeStruct((N_BLK, D_MODEL, W_IN_BLK), F32),
        scratch_shapes=[pltpu.VMEM((N_BLK // 2, D_MODEL, 2 * W_IN_BLK), F32)],
        compiler_params=_params(("arbitrary",), V7X_VMEM_LIMIT),
    )(hb, dz)


def _layer_params(small, layer):
    tril = jnp.tril(jnp.ones((CHUNK, CHUNK), bool))
    ws = jnp.where(tril, small["gmlp_w_s"][layer], 0.0)
    bmat = jnp.repeat(small["gmlp_b_s"][layer].T, HEAD_DIM, axis=1)
    scw = jnp.zeros((8, D_GROUP), F32).at[:SHORT_K].set(small["short_conv_w"][layer])
    ccw = jnp.zeros((32, D_GROUP), F32).at[:CONF_K].set(small["conf_conv_w"][layer])
    return dict(vg=small["gmlp_v_g"][layer][None], wt=ws.astype(BF16), wtt=jnp.swapaxes(ws, 1, 2).astype(BF16),
                bmat=bmat, scw=scw, ccw=ccw, lg=small["conf_ln_g"][layer][None], lb=small["conf_ln_b"][layer][None])


def _local_step(x, target, big, small, gather_pending=False, core=None):
    saved = []
    for l in range(DEPTH):
        p = _layer_params(small, l)
        z, hb, q_r, q_t, k_t, v_t = _in_proj(x, small["norm_mix_g"][l][None], big["w_in"], l)
        if gather_pending and l == 0:
            late = ("w_out", "w_up", "w_down")
            y_abd, filled = _mixers_fwd(z, p, [big[k] for k in late], [(n, 0) for n in range(len(late))])
            big = {**big, **dict(zip(late, filled))}
            o_t, filled = _attn_fwd(q_r, k_t, v_t, [big[k] for k in _BIG], [(n, 1) for n in range(len(_BIG))])
            big = dict(zip(_BIG, filled))
        else:
            y_abd = _mixers_fwd(z, p)
            o_t = _attn_fwd(q_r, k_t, v_t)
        x1, yn = _out_proj(x, y_abd, o_t, small["mix_out_g"][l][None], big["w_out"], l)
        x2, pre, h2b = _ffn(x1, small["norm_ffn_g"][l][None], big["w_up"], big["w_down"], l)
        saved.append(dict(p=p, x0=x, z=z, hb=hb, q_r=q_r, q_t=q_t, k_t=k_t, v_t=v_t, o_t=o_t, x1=x1, yn=yn, pre=pre,
                          h2b=h2b, y_abd=y_abd))
        x = x2

    loss, dx, dxb, d_final = _loss_head(x, small["final_norm_g"][None], target)

    g = {k: [None] * DEPTH for k in ("w_in", "w_out", "w_up", "w_down", "norm_mix_g", "gmlp_v_g", "gmlp_w_s", "gmlp_b_s",
                                     "short_conv_w", "conf_conv_w", "conf_ln_g", "conf_ln_b", "mix_out_g", "norm_ffn_g")}
    tril = jnp.tril(jnp.ones((CHUNK, CHUNK), bool))
    early = {}
    ffn_sums = [[]] * DEPTH
    for l in reversed(range(DEPTH)):
        sv = saved[l]
        p = sv["p"]
        riding, sums = [], []
        if core is not None and l == 0:
            riding = [(k, 1) for k in _BIG]
            sums = _chip_sums(core, [g["w_in"][1], g["w_out"][1]]) + ffn_sums[1]
        dpre, dh, *got = _ffn_bwd(dxb, sv["pre"], big["w_up"], big["w_down"], l, sums)
        early.update(zip(riding, zip(sums, *got)))
        g["w_up"][l], g["w_down"][l] = _ffn_wgrad(sv["h2b"], sv["pre"], dpre, dxb)
        halves = [g["w_up"][l], g["w_down"][l]] if core is not None else []
        dx1, g["norm_ffn_g"][l], dyn, g["mix_out_g"][l], g["w_out"][l], *swapped = _out_proj_bwd(
            sv["x1"], small["norm_ffn_g"][l][None], dh, dx, sv["yn"], small["mix_out_g"][l][None], big["w_out"], l,
            halves)
        if halves:
            ffn_sums[l] = [_add_pairs(core, a, b) for a, b in zip(halves, *swapped)]
        riding, sums = [], []
        if core is not None and l == 0:
            riding, sums = [("w_up", 0), ("w_down", 0)], ffn_sums[0]
        (dza, dcb, dcd, do_r, do_t, dsum, dvg, dws, dbm, dscw, dccw, dlg, dlb, *got) = _mixers_bwd_a(
            sv["z"], dyn, sv["o_t"], sv["y_abd"], p, sums)
        early.update(zip(riding, zip(sums, *got)))
        dq_t, dk_t, dv_t = _attn_bwd(sv["q_r"], sv["q_t"], sv["k_t"], sv["v_t"], do_r, do_t, dsum)
        dz = _mixers_bwd_b(sv["z"], dza, dcb, dcd, dq_t, dk_t, dv_t, p)
        dx, dxb, g["norm_mix_g"][l] = _in_proj_bwd(sv["x0"], small["norm_mix_g"][l][None], dx1, dz, big["w_in"], l)
        g["w_in"][l] = _in_proj_wgrad(sv["hb"], dz)
        g["gmlp_v_g"][l] = dvg[0]
        g["gmlp_w_s"][l] = jnp.where(tril, dws, 0.0)
        g["gmlp_b_s"][l] = dbm.reshape(CHUNK, N_HEADS, HEAD_DIM).sum(-1).T
        g["short_conv_w"][l] = dscw[:SHORT_K]
        g["conf_conv_w"][l] = dccw[:CONF_K]
        g["conf_ln_g"][l] = dlg[0]
        g["conf_ln_b"][l] = dlb[0]
        g["norm_mix_g"][l] = g["norm_mix_g"][l][0]
        g["mix_out_g"][l] = g["mix_out_g"][l][0]
        g["norm_ffn_g"][l] = g["norm_ffn_g"][l][0]
    grads = {k: v if k in ("w_in", "w_out", "w_up", "w_down") else jnp.stack(v) for k, v in g.items()}
    grads["final_norm_g"] = d_final[0]
    return loss, dx, grads, early


_ANY = pl.BlockSpec(memory_space=pl.ANY)


def _mesh_place():
    x, y, c = lax.axis_index("x"), lax.axis_index("y"), lax.axis_index("c")
    chips = [(1 - x, y), (x, 1 - y), (1 - x, 1 - y)]
    return x, y, c, 2 * x + y, chips


def _gather_stages(bufs, parts, sems):
    ici_send, ici_recv, d2d_send, d2d_recv = sems
    x, y, c, me, chips = _mesh_place()
    blk = [2 * chip[0] + chip[1] for chip in chips]
    pairs = [(p, r) for p in range(len(parts)) for r in range(3)]

    def rows(p, block, half_of):
        k, layer = parts[p]
        half = bufs[k].shape[2] // 2
        return bufs[k].at[block, layer, pl.ds(half_of * half, half), :]

    def ici(p, r, block):
        return pltpu.make_async_remote_copy(
            src_ref=rows(p, me, c), dst_ref=rows(p, block, c), send_sem=ici_send.at[3 * p + r],
            recv_sem=ici_recv.at[3 * p + r], device_id=(chips[r][0], chips[r][1], c), device_id_type=MESH)

    def d2d(p, r, half_of):
        part = rows(p, blk[r], half_of)
        return pltpu.make_async_remote_copy(
            src_ref=part, dst_ref=part, send_sem=d2d_send.at[3 * p + r], recv_sem=d2d_recv.at[3 * p + r],
            device_id=(x, y, 1 - c), device_id_type=MESH)

    def start():
        for p, r in pairs:
            ici(p, r, me).start()

    def forward(p):
        for r in range(3):
            ici(p, r, blk[r]).wait_recv()
            d2d(p, r, c).start()

    def finish():
        for p, r in pairs:
            d2d(p, r, 1 - c).wait_recv()
        for p, r in pairs:
            ici(p, r, me).wait_send()
            d2d(p, r, c).wait_send()

    return start, forward, finish


def _gather_sems(parts):
    return [pltpu.SemaphoreType.DMA((3 * len(parts),)) for _ in range(4)]


def _gather_in_steps(bufs, parts, sems, n_steps):
    start, forward, finish = _gather_stages(bufs, parts, sems)
    i = pl.program_id(0)
    pl.when(i == 0)(start)
    for p in range(len(parts)):
        pl.when(i == n_steps * (2 * p + 3) // (2 * len(parts) + 2))(lambda p=p: forward(p))
    pl.when(i == n_steps - 1)(finish)


def _gather_first(bufs, parts, whole):
    n, m = len(bufs), len(whole)

    def body(*refs):
        whole_in, buf_out, whole_out = refs[n:n + m], refs[n + m:2 * n + m], refs[2 * n + m:2 * (n + m)]
        sems = refs[2 * (n + m):]
        send_sems, recv_sems, local_sems = sems[4:]
        x, y, c, me, chips = _mesh_place()
        start, forward, finish = _gather_stages(buf_out, parts, sems[:4])

        def push(k, r, block):
            return pltpu.make_async_remote_copy(
                src_ref=whole_in[k], dst_ref=whole_out[k].at[block], send_sem=send_sems.at[3 * k + r],
                recv_sem=recv_sems.at[3 * k + r], device_id=(chips[r][0], chips[r][1], c), device_id_type=MESH)

        local = [pltpu.make_async_copy(whole_in[k], whole_out[k].at[me], local_sems.at[k]) for k in range(m)]
        for cp in local:
            cp.start()
        start()
        for k in range(m):
            for r in range(3):
                push(k, r, me).start()
        for p in range(len(parts)):
            forward(p)
        for k in range(m):
            for r, chip in enumerate(chips):
                push(k, r, 2 * chip[0] + chip[1]).wait_recv()
        for k in range(m):
            for r in range(3):
                push(k, r, me).wait_send()
        finish()
        for cp in local:
            cp.wait()

    return pl.pallas_call(
        body, name="gather_first",
        in_specs=[_ANY] * (n + m), out_specs=[_ANY] * (n + m),
        out_shape=[jax.ShapeDtypeStruct(b.shape, b.dtype) for b in bufs]
                  + [jax.ShapeDtypeStruct((N_BLK,) + b.shape, b.dtype) for b in whole],
        input_output_aliases={k: k for k in range(n)},
        scratch_shapes=_gather_sems(parts) + [pltpu.SemaphoreType.DMA((3 * m,)), pltpu.SemaphoreType.DMA((3 * m,)),
                                              pltpu.SemaphoreType.DMA((m,))],
    )(*bufs, *whole)


def _swap_halves(gs):
    n = len(gs)

    def body(*refs):
        start, finish = _halves_stages(refs[:n], refs[n:2 * n], refs[2 * n:])
        start()
        finish()

    return pl.pallas_call(
        body, name="swap_halves", in_specs=[_ANY] * n, out_specs=[_ANY] * n,
        out_shape=_halves_shapes(gs), scratch_shapes=_halves_sems(n),
    )(*gs)


def _halves_stages(ins, outs, sems):
    send_sems, recv_sems = sems
    x, y, c, _, _ = _mesh_place()

    def copy(k):
        half = ins[k].shape[1] // 2
        return pltpu.make_async_remote_copy(
            src_ref=ins[k].at[:, pl.ds((1 - c) * half, half), :], dst_ref=outs[k],
            send_sem=send_sems.at[k], recv_sem=recv_sems.at[k], device_id=(x, y, 1 - c), device_id_type=MESH)

    def start():
        for k in range(len(ins)):
            copy(k).start()

    def finish():
        for k in range(len(ins)):
            copy(k).wait()

    return start, finish


def _halves_shapes(gs):
    return [jax.ShapeDtypeStruct((g.shape[0], g.shape[1] // 2, g.shape[2]), F32) for g in gs]


def _halves_sems(n):
    return [pltpu.SemaphoreType.DMA((n,)), pltpu.SemaphoreType.DMA((n,))]


def _owner_stages(ins, outs, sems):
    send_sems, recv_sems = sems
    x, y, c, me, chips = _mesh_place()
    pairs = [(k, r) for k in range(len(ins)) for r in range(3)]

    def remote(k, r, src_block, dst_block):
        return pltpu.make_async_remote_copy(
            src_ref=ins[k].at[src_block], dst_ref=outs[k].at[dst_block], send_sem=send_sems.at[3 * k + r],
            recv_sem=recv_sems.at[3 * k + r], device_id=(chips[r][0], chips[r][1], c), device_id_type=MESH)

    def start():
        for k, r in pairs:
            remote(k, r, 2 * chips[r][0] + chips[r][1], me).start()

    def finish():
        for k, r in pairs:
            remote(k, r, me, 2 * chips[r][0] + chips[r][1]).wait_recv()
        for k, r in pairs:
            remote(k, r, 2 * chips[r][0] + chips[r][1], me).wait_send()

    return start, finish


def _owner_sems(n):
    return [pltpu.SemaphoreType.DMA((3 * n,)), pltpu.SemaphoreType.DMA((3 * n,))]


def _owners_in_steps(ins, outs, sems, first, last):
    start, finish = _owner_stages(ins, outs, sems)
    pl.when(first)(start)
    pl.when(last)(finish)


def _send_to_owners(sums):
    n = len(sums)

    def body(*refs):
        start, finish = _owner_stages(refs[:n], refs[n:2 * n], refs[2 * n:])
        start()
        finish()

    return pl.pallas_call(
        body, name="send_to_owners", in_specs=[_ANY] * n, out_specs=[_ANY] * n,
        out_shape=[jax.ShapeDtypeStruct(s.shape, s.dtype) for s in sums],
        scratch_shapes=_owner_sems(n),
    )(*sums)


def _swap_reduced(fs):
    n = len(fs)

    def body(*refs):
        ins, outs, (send_sems, recv_sems) = refs[:n], refs[n:2 * n], refs[2 * n:]
        x, y, c, _, _ = _mesh_place()
        cps = [pltpu.make_async_remote_copy(src_ref=ins[k], dst_ref=outs[k], send_sem=send_sems.at[k],
                                            recv_sem=recv_sems.at[k], device_id=(x, y, 1 - c), device_id_type=MESH)
               for k in range(n)]
        for cp in cps:
            cp.start()
        for cp in cps:
            cp.wait()

    return pl.pallas_call(
        body, name="swap_reduced", in_specs=[_ANY] * n, out_specs=[_ANY] * n,
        out_shape=[jax.ShapeDtypeStruct(f.shape, F32) for f in fs],
        scratch_shapes=[pltpu.SemaphoreType.DMA((n,)), pltpu.SemaphoreType.DMA((n,))],
    )(*fs)


def _row_tile(rows):
    return min(rows, 256)


def _chip_sums(core, grads):
    if not grads:
        return []
    return [_add_pairs(core, a, b) for a, b in zip(grads, _swap_halves(grads))]


def _add_pairs(core, g, other):
    n, half, cols = other.shape
    t = _row_tile(half)
    per_half = half // t

    def body(c_ref, a_ref, b_ref, o_ref):
        o_ref[...] = (a_ref[...] + b_ref[...]).astype(BF16)

    spec = pl.BlockSpec((None, t, cols), lambda i, j, c_ref: (i, j, 0))
    return pl.pallas_call(
        body, name="add_pairs",
        grid_spec=pltpu.PrefetchScalarGridSpec(
            num_scalar_prefetch=1, grid=(n, per_half),
            in_specs=[pl.BlockSpec((None, t, cols), lambda i, j, c_ref: (i, c_ref[0] * per_half + j, 0)), spec],
            out_specs=spec),
        out_shape=jax.ShapeDtypeStruct(other.shape, BF16), compiler_params=_params(("parallel", "parallel")),
    )(core, g, other)


def _add_chips(me, s1, r2):
    _, r, cols = r2.shape
    t = _row_tile(r)

    def body(me_ref, s_ref, r_ref, o_ref):
        own = s_ref[...].astype(F32)
        parts = [jnp.where(me_ref[0] == k, own, r_ref[k].astype(F32)) for k in range(N_BLK)]
        o_ref[...] = ((parts[0] + parts[1]) + parts[2]) + parts[3]

    return pl.pallas_call(
        body, name="add_chips",
        grid_spec=pltpu.PrefetchScalarGridSpec(
            num_scalar_prefetch=1, grid=(r // t,),
            in_specs=[pl.BlockSpec((None, t, cols), lambda i, me_ref: (me_ref[0], i, 0)),
                      pl.BlockSpec((N_BLK, t, cols), lambda i, me_ref: (0, i, 0))],
            out_specs=pl.BlockSpec((t, cols), lambda i, me_ref: (i, 0))),
        out_shape=jax.ShapeDtypeStruct((r, cols), F32), compiler_params=_params(("parallel",)),
    )(me, s1, r2)


def _adamw(core, mine, other, w, m, v, layer, earlier=None):
    half, cols = mine.shape
    t = _row_tile(half)
    per_half = half // t
    c1 = 1.0 - ADAM_B1 ** ADAM_STEP
    c2 = 1.0 - ADAM_B2 ** ADAM_STEP

    def body(c_ref, a_ref, b_ref, w_ref, m_ref, v_ref, *rest):
        g_ref, d_ref, mo_ref, vo_ref = rest[-4:]
        gv = jnp.where(pl.program_id(0) // per_half == c_ref[0], a_ref[...], b_ref[...])
        g_ref[...] = gv
        m_new = ADAM_B1 * m_ref[...] + (1.0 - ADAM_B1) * gv
        v_new = ADAM_B2 * v_ref[...] + (1.0 - ADAM_B2) * (gv * gv)
        mo_ref[...] = m_new
        vo_ref[...] = v_new
        d_ref[...] = -ADAM_LR * ((m_new / c1) / (jnp.sqrt(v_new / c2) + ADAM_EPS) + ADAM_WD * w_ref[...])

    part = pl.BlockSpec((t, cols), lambda i, c_ref: (i % per_half, 0))
    spec = pl.BlockSpec((None, t, cols), lambda i, c_ref: (layer, i, 0))
    kept = [] if earlier is None else list(earlier)
    return pl.pallas_call(
        body, name="adamw",
        grid_spec=pltpu.PrefetchScalarGridSpec(
            num_scalar_prefetch=1, grid=(2 * per_half,),
            in_specs=[part, part, spec, spec, spec] + [_ANY] * len(kept), out_specs=[spec] * 4),
        out_shape=[jax.ShapeDtypeStruct(w.shape, F32)] * 4,
        input_output_aliases={6 + k: k for k in range(len(kept))},
        compiler_params=_params(("parallel",)),
    )(core, mine, other, w, m, v, *kept)


_REPLICATED = ("norm_mix_g", "gmlp_v_g", "gmlp_w_s", "gmlp_b_s", "conf_ln_g", "conf_ln_b", "mix_out_g", "norm_ffn_g",
               "final_norm_g")
_REP_SHAPES = {"norm_mix_g": (DEPTH, D_MODEL), "gmlp_v_g": (DEPTH, D_GROUP), "gmlp_w_s": (DEPTH, N_HEADS, CHUNK, CHUNK),
               "gmlp_b_s": (DEPTH, N_HEADS, CHUNK), "conf_ln_g": (DEPTH, D_GROUP), "conf_ln_b": (DEPTH, D_GROUP),
               "mix_out_g": (DEPTH, D_MODEL), "norm_ffn_g": (DEPTH, D_MODEL), "final_norm_g": (D_MODEL,)}
_BIG = ("w_in", "w_out", "w_up", "w_down")
_CONV_ROWS = 8
_REP_ROWS = 144
_SMALL_ROWS = 160
_CH_BLK = D_GROUP // N_BLK


def _pad_rows(flat, rows):
    pad = rows * D_MODEL - flat.shape[-1]
    flat = jnp.pad(flat, [(0, 0)] * (flat.ndim - 1) + [(0, pad)])
    return flat.reshape(flat.shape[:-1] + (rows, D_MODEL))


def _pack_small(scw, ccw, rep):
    lead = scw.shape[:-3]
    conv = jnp.concatenate([scw.reshape(lead + (-1,)), ccw.reshape(lead + (-1,))], axis=-1)
    flat = jnp.concatenate([rep[k].reshape(-1) for k in _REPLICATED])
    flat = jnp.broadcast_to(flat, lead + flat.shape)
    parts = [_pad_rows(conv, _CONV_ROWS), _pad_rows(flat, _REP_ROWS),
             jnp.zeros(lead + (_SMALL_ROWS - _CONV_ROWS - _REP_ROWS, D_MODEL), F32)]
    return jnp.concatenate(parts, axis=-2)


def _unpack_small(pk):
    out = {}
    conv = pk[:_CONV_ROWS].reshape(-1)
    n_s = DEPTH * SHORT_K * _CH_BLK
    out["short_conv_w"] = conv[:n_s].reshape(DEPTH, SHORT_K, _CH_BLK)
    out["conf_conv_w"] = conv[n_s:n_s + DEPTH * CONF_K * _CH_BLK].reshape(DEPTH, CONF_K, _CH_BLK)
    row = _CONV_ROWS
    flat = pk[row:row + _REP_ROWS].reshape(-1)
    at = 0
    for k in _REPLICATED:
        n = math.prod(_REP_SHAPES[k])
        out[k] = flat[at:at + n].reshape(_REP_SHAPES[k])
        at += n
    return out


def _conv_blocks(w):
    d, k, _ = w.shape
    return w.reshape(d, k, N_BLK, _CH_BLK).transpose(2, 0, 1, 3)


_WEIGHTS = ("norm_mix_g", "w_in", "gmlp_v_g", "gmlp_w_s", "gmlp_b_s", "short_conv_w", "conf_conv_w", "conf_ln_g",
            "conf_ln_b", "mix_out_g", "w_out", "norm_ffn_g", "w_up", "w_down", "final_norm_g")


def kernel(x, norm_mix_g, w_in, gmlp_v_g, gmlp_w_s, gmlp_b_s, short_conv_w, conf_conv_w, conf_ln_g, conf_ln_b, mix_out_g, w_out, norm_ffn_g, w_up, w_down, final_norm_g, loss_target, m_norm_mix_g, m_w_in, m_gmlp_v_g, m_gmlp_w_s, m_gmlp_b_s, m_short_conv_w, m_conf_conv_w, m_conf_ln_g, m_conf_ln_b, m_mix_out_g, m_w_out, m_norm_ffn_g, m_w_up, m_w_down, m_final_norm_g, v_norm_mix_g, v_w_in, v_gmlp_v_g, v_gmlp_w_s, v_gmlp_b_s, v_short_conv_w, v_conf_conv_w, v_conf_ln_g, v_conf_ln_b, v_mix_out_g, v_w_out, v_norm_ffn_g, v_w_up, v_w_down, v_final_norm_g):
    w = dict(norm_mix_g=norm_mix_g, w_in=w_in, gmlp_v_g=gmlp_v_g, gmlp_w_s=gmlp_w_s, gmlp_b_s=gmlp_b_s,
             short_conv_w=short_conv_w, conf_conv_w=conf_conv_w, conf_ln_g=conf_ln_g, conf_ln_b=conf_ln_b,
             mix_out_g=mix_out_g, w_out=w_out, norm_ffn_g=norm_ffn_g, w_up=w_up, w_down=w_down, final_norm_g=final_norm_g)
    m = dict(norm_mix_g=m_norm_mix_g, w_in=m_w_in, gmlp_v_g=m_gmlp_v_g, gmlp_w_s=m_gmlp_w_s, gmlp_b_s=m_gmlp_b_s,
             short_conv_w=m_short_conv_w, conf_conv_w=m_conf_conv_w, conf_ln_g=m_conf_ln_g, conf_ln_b=m_conf_ln_b,
             mix_out_g=m_mix_out_g, w_out=m_w_out, norm_ffn_g=m_norm_ffn_g, w_up=m_w_up, w_down=m_w_down,
             final_norm_g=m_final_norm_g)
    v = dict(norm_mix_g=v_norm_mix_g, w_in=v_w_in, gmlp_v_g=v_gmlp_v_g, gmlp_w_s=v_gmlp_w_s, gmlp_b_s=v_gmlp_b_s,
             short_conv_w=v_short_conv_w, conf_conv_w=v_conf_conv_w, conf_ln_g=v_conf_ln_g, conf_ln_b=v_conf_ln_b,
             mix_out_g=v_mix_out_g, w_out=v_w_out, norm_ffn_g=v_norm_ffn_g, w_up=v_w_up, w_down=v_w_down,
             final_norm_g=v_final_norm_g)
    core = lax.axis_index("c").astype(jnp.int32).reshape(1)
    me = (2 * lax.axis_index("x") + lax.axis_index("y")).astype(jnp.int32).reshape(1)

    conv_mine = _pad_rows(jnp.concatenate([short_conv_w.reshape(-1), conf_conv_w.reshape(-1)]), _CONV_ROWS)
    big = {k: _cast_into_slot(w[k], me, "cast_" + k) for k in _BIG}
    big["w_in"], conv_all = _gather_first([big["w_in"]], [(0, 0)], [conv_mine])
    conv_all = conv_all.reshape(N_BLK, -1)
    n_s = DEPTH * SHORT_K * _CH_BLK
    scw_all = conv_all[:, :n_s].reshape(N_BLK, DEPTH, SHORT_K, _CH_BLK)
    ccw_all = conv_all[:, n_s:n_s + DEPTH * CONF_K * _CH_BLK].reshape(N_BLK, DEPTH, CONF_K, _CH_BLK)
    small = {k: w[k] for k in _REPLICATED}
    small["short_conv_w"] = scw_all.transpose(1, 2, 0, 3).reshape(DEPTH, SHORT_K, D_GROUP)
    small["conf_conv_w"] = ccw_all.transpose(1, 2, 0, 3).reshape(DEPTH, CONF_K, D_GROUP)

    loss, dx, g, early = _local_step(x[0], loss_target[0], big, small, gather_pending=True, core=core)

    where = [(k, l) for k in _BIG for l in range(DEPTH)]
    late = [kl for kl in where if kl not in early]
    sums = _chip_sums(core, [g[k][l] for k, l in late]
                      + [_pack_small(_conv_blocks(g["short_conv_w"]), _conv_blocks(g["conf_conv_w"]), g)])
    sent = {**early, **dict(zip(late + ["small"], zip(sums, _send_to_owners(sums))))}
    mine = [_add_chips(me, *sent[kl]) for kl in where + ["small"]]
    other = _swap_reduced(mine)

    done = {}
    for n, (k, l) in enumerate(where):
        done[k] = _adamw(core, mine[n], other[n], w[k], m[k], v[k], l, done.get(k))
    small_own = [_pack_small(t["short_conv_w"], t["conf_conv_w"], t)[None] for t in (w, m, v)]
    small_done = [_unpack_small(a[0]) for a in _adamw(core, mine[-1], other[-1], *small_own, 0)]

    outs = [lax.psum(loss[0, 0], ("x", "y", "c")), dx[None]]
    for kind in range(4):
        outs += [done[k][kind] if k in _BIG else small_done[kind][k] for k in _WEIGHTS]
    return tuple(outs)
```

```python
import math

import jax
import jax.numpy as jnp
from jax import lax
from jax.experimental import pallas as pl
from jax.experimental.pallas import tpu as pltpu

F32 = jnp.float32
BF16 = jnp.bfloat16

D_MODEL = 1024
D_GROUP = 256
N_HEADS = 4
HEAD_DIM = 64
CHUNK = 128
D_IN = 2560
N_BLK = 4
W_IN_BLK = D_IN // N_BLK
D_FF_BLK = 1024
DEPTH = 2
EPS = 1e-6
HALO = 32
MIX_ROWS = 512
SHORT_K = 3
CONF_K = 31
ATT_TQ = 256
ATT_TK = 256
ATT_SCALE = 0.125
ATT_DEAD = -104.0
V7X_VMEM_LIMIT = 56 * 1024 * 1024

ADAM_LR, ADAM_B1, ADAM_B2, ADAM_EPS, ADAM_WD, ADAM_STEP = 0.001, 0.9, 0.999, 1e-08, 0.01, 10

MESH = pl.DeviceIdType.MESH


def _params(sem, vmem=None):
    return pltpu.CompilerParams(dimension_semantics=sem, vmem_limit_bytes=vmem)


def _tile(s, t):
    return min(s, t)


def _rsqrt_mean(v):
    return lax.rsqrt(jnp.mean(v * v, axis=-1, keepdims=True) + EPS)


def _sigmoid(v):
    return 1.0 / (1.0 + jnp.exp(-v))


_GELU_C = math.sqrt(2.0 / math.pi)


def _gelu_tanh(v):
    return jnp.tanh(_GELU_C * (v + 0.044715 * (v * v * v)))


def _gelu(v, t):
    return v * (0.5 * (1.0 + t))


def _gelu_grad(v, t):
    return 0.5 * (1.0 + t) + v * (0.5 * (1.0 - t * t) * _GELU_C * (1.0 + 3.0 * 0.044715 * (v * v)))


def _dot(a, b):
    return jnp.dot(a, b, preferred_element_type=F32)


def _dot_nt(a, b):
    return lax.dot_general(a, b, (((1,), (1,)), ((), ())), preferred_element_type=F32)


def _dot_tn(a, b):
    return lax.dot_general(a, b, (((0,), (0,)), ((), ())), preferred_element_type=F32)


def _cast_into_slot(w, me, name):
    n, r, c = w.shape
    tr = _tile(r, 256)

    def body(me_ref, w_ref, o_ref):
        o_ref[...] = w_ref[...].astype(BF16)

    return pl.pallas_call(
        body, name=name,
        grid_spec=pltpu.PrefetchScalarGridSpec(
            num_scalar_prefetch=1, grid=(n, r // tr),
            in_specs=[pl.BlockSpec((None, tr, c), lambda a, b, me_ref: (a, b, 0))],
            out_specs=pl.BlockSpec((None, None, tr, c), lambda a, b, me_ref: (me_ref[0], a, b, 0))),
        out_shape=jax.ShapeDtypeStruct((N_BLK,) + w.shape, BF16),
        compiler_params=_params(("parallel", "parallel")),
    )(me, w)


def _split_heads(xv, rows_ref, cols_ref):
    if rows_ref is not None:
        for h in range(N_HEADS):
            rows_ref[h] = xv[:, h * HEAD_DIM:(h + 1) * HEAD_DIM].astype(BF16)
    if cols_ref is not None:
        xt = xv.T
        for h in range(N_HEADS):
            cols_ref[h] = xt[h * HEAD_DIM:(h + 1) * HEAD_DIM, :].astype(BF16)


def _head_specs(t, s):
    rows = (pl.BlockSpec((N_HEADS, t, HEAD_DIM), lambda i: (0, i, 0)), jax.ShapeDtypeStruct((N_HEADS, s, HEAD_DIM), BF16))
    cols = (pl.BlockSpec((N_HEADS, HEAD_DIM, t), lambda i: (0, 0, i)), jax.ShapeDtypeStruct((N_HEADS, HEAD_DIM, s), BF16))
    return rows, cols


def _cols(ref, lo, hi):
    return ref[:, lo:hi].astype(F32)


def _pair_blocks(w_ref, wide_ref):
    @pl.when(pl.program_id(0) == 0)
    def _():
        for b in range(N_BLK):
            wide_ref[b // 2, :, (b % 2) * W_IN_BLK:(b % 2 + 1) * W_IN_BLK] = w_ref[b]


def _in_proj(x, g, w, layer):
    s = x.shape[0]
    t = _tile(s, 1024)

    def body(x_ref, g_ref, w_ref, z_ref, h_ref, qr_ref, qt_ref, kt_ref, vt_ref, wide_ref):
        _pair_blocks(w_ref, wide_ref)
        xv = x_ref[...]
        h = (xv * _rsqrt_mean(xv) * g_ref[...]).astype(BF16)
        h_ref[...] = h
        for n in range(N_BLK // 2):
            z_ref[:, 2 * n * W_IN_BLK:2 * (n + 1) * W_IN_BLK] = _dot(h, wide_ref[n]).astype(BF16)
        _split_heads(_cols(z_ref, 1280, 1536) * ATT_SCALE, qr_ref, qt_ref)
        _split_heads(_cols(z_ref, 1536, 1792), None, kt_ref)
        _split_heads(_cols(z_ref, 1792, 2048), None, vt_ref)

    rows, cols = _head_specs(t, s)
    return pl.pallas_call(
        body, name="in_proj", grid=(s // t,),
        in_specs=[pl.BlockSpec((t, D_MODEL), lambda i: (i, 0)), _full((1, D_MODEL)),
                  pl.BlockSpec((N_BLK, None, D_MODEL, W_IN_BLK), lambda i: (0, layer, 0, 0))],
        out_specs=[pl.BlockSpec((t, D_IN), lambda i: (i, 0)), pl.BlockSpec((t, D_MODEL), lambda i: (i, 0)),
                   rows[0], cols[0], cols[0], cols[0]],
        out_shape=[jax.ShapeDtypeStruct((s, D_IN), BF16), jax.ShapeDtypeStruct((s, D_MODEL), BF16),
                   rows[1], cols[1], cols[1], cols[1]],
        scratch_shapes=[pltpu.VMEM((N_BLK // 2, D_MODEL, 2 * W_IN_BLK), BF16)],
        compiler_params=_params(("arbitrary",), V7X_VMEM_LIMIT),
    )(x, g, w)


def _mix_a_fwd(z_ref, vg, wt_ref, bmat, t):
    zu = _cols(z_ref, 0, 256)
    zv = _cols(z_ref, 256, 512)
    tu = _gelu_tanh(zu)
    tv = _gelu_tanh(zv)
    u = _gelu(zu, tu)
    v = _gelu(zv, tv)
    rv = _rsqrt_mean(v)
    vh = v * rv
    vnb = (vh * vg).astype(BF16)
    head = lax.broadcasted_iota(jnp.int32, (CHUNK, D_GROUP), 1) // HEAD_DIM
    fs = []
    for c in range(t // CHUNK):
        vc = vnb[c * CHUNK:(c + 1) * CHUNK, :]
        fc = bmat
        for h in range(N_HEADS):
            fc = fc + jnp.where(head == h, _dot(wt_ref[h], vc), 0.0)
        fs.append(fc)
    f = jnp.concatenate(fs, axis=0) if len(fs) > 1 else fs[0]
    return (zu, tu), (zv, tv), u, rv, vh, vnb, f


def _windows(ext_ref, sh_ref, t):
    for b in range(1, 8):
        sh_ref[b - 1] = ext_ref[pl.ds(b, HALO + t - 8), :]

    def window(o):
        a, b = divmod(o, 8)
        return ext_ref[pl.ds(8 * a, t), :] if b == 0 else sh_ref[b - 1, pl.ds(8 * a, t), :]

    return window


def _mix_b_fwd(z_ref, zh_ref, first, scw_ref, ext_ref, t):
    gb = _cols(z_ref, 512, 768)
    uh = _cols(zh_ref, 768, 1024) * _cols(zh_ref, 1024, 1280)
    ext_ref[0:HALO, :] = jnp.where(first, 0.0, uh)
    ext_ref[HALO:HALO + t, :] = _cols(z_ref, 768, 1024) * _cols(z_ref, 1024, 1280)
    cv = jnp.zeros((t, D_GROUP), F32)
    for k in range(SHORT_K):
        cv = cv + scw_ref[k:k + 1, :] * ext_ref[pl.ds(HALO - (SHORT_K - 1) + k, t), :]
    return gb, cv


def _mix_d_fwd(z_ref, zh_ref, first, ccw_ref, lg, lb, ext_ref, sh_ref, t, cv=None):
    hh = _cols(zh_ref, 2048, 2304) * _sigmoid(_cols(zh_ref, 2304, 2560))
    ext_ref[0:HALO, :] = jnp.where(first, 0.0, hh)
    ext_ref[HALO:HALO + t, :] = _cols(z_ref, 2048, 2304) * _sigmoid(_cols(z_ref, 2304, 2560))
    window = _windows(ext_ref, sh_ref, t)
    if cv is None:
        cv = jnp.zeros((t, D_GROUP), F32)
        for k in range(CONF_K):
            cv = cv + ccw_ref[k:k + 1, :] * window(HALO - (CONF_K - 1) + k)
    xc = cv - jnp.mean(cv, axis=-1, keepdims=True)
    rs = lax.rsqrt(jnp.mean(xc * xc, axis=-1, keepdims=True) + EPS)
    xh = xc * rs
    ln = xh * lg + lb
    return xh, rs, ln, _sigmoid(ln), window, cv


def _mix_specs(t, s):
    per = t // HALO
    return [pl.BlockSpec((t, D_IN), lambda i: (i, 0)),
            pl.BlockSpec((HALO, D_IN), lambda i: (jnp.maximum(i * per - 1, 0), 0))]


def _full(shape):
    return pl.BlockSpec(shape, lambda i: (0,) * len(shape))


def _mixers_fwd(z, p, bufs=(), parts=()):
    s = z.shape[0]
    t = _tile(s, MIX_ROWS)
    nb = len(bufs)

    def body(z_ref, zh_ref, vg_ref, wt_ref, bm_ref, scw_ref, ccw_ref, lg_ref, lb_ref, *rest):
        y_ref = rest[nb]
        eb_ref, ed_ref, sh_ref = rest[2 * nb + 1:2 * nb + 4]
        if nb:
            _gather_in_steps(rest[nb + 1:2 * nb + 1], parts, rest[2 * nb + 4:], s // t)
        first = pl.program_id(0) == 0
        _, _, u, _, _, _, f = _mix_a_fwd(z_ref, vg_ref[...], wt_ref, bm_ref[...], t)
        ya = u * f
        y_ref[:, 0:256] = ya * _rsqrt_mean(ya)
        gb, cv = _mix_b_fwd(z_ref, zh_ref, first, scw_ref, eb_ref, t)
        yb = gb * cv
        y_ref[:, 256:512] = yb * _rsqrt_mean(yb)
        _, _, ln, sg, _, cvd = _mix_d_fwd(z_ref, zh_ref, first, ccw_ref, lg_ref[...], lb_ref[...], ed_ref, sh_ref, t)
        yd = ln * sg
        y_ref[:, 512:768] = yd * _rsqrt_mean(yd)
        y_ref[:, 768:1024] = cvd

    out = pl.pallas_call(
        body, name="mixers_fwd", grid=(s // t,),
        in_specs=_mix_specs(t, s) + [_full((1, D_GROUP)), _full((N_HEADS, CHUNK, CHUNK)), _full((CHUNK, D_GROUP)),
                                     _full((8, D_GROUP)), _full((32, D_GROUP)), _full((1, D_GROUP)), _full((1, D_GROUP))]
                 + [_ANY] * nb,
        out_specs=[pl.BlockSpec((t, D_MODEL), lambda i: (i, 0))] + [_ANY] * nb,
        out_shape=[jax.ShapeDtypeStruct((s, D_MODEL), F32)] + [jax.ShapeDtypeStruct(b.shape, b.dtype) for b in bufs],
        input_output_aliases={9 + k: 1 + k for k in range(nb)},
        scratch_shapes=[pltpu.VMEM((HALO + t, D_GROUP), F32), pltpu.VMEM((HALO + t, D_GROUP), F32),
                        pltpu.VMEM((7, HALO + t - 8, D_GROUP), F32)] + (_gather_sems(parts) if nb else []),
        compiler_params=_params(("arbitrary",) if nb else ("parallel",), V7X_VMEM_LIMIT),
    )(z, z, p["vg"], p["wt"], p["bmat"], p["scw"], p["ccw"], p["lg"], p["lb"], *bufs)
    return (out[0], out[1:]) if nb else out[0]


def _mixers_bwd_a(z, dyn, o_t, y_abd, p, sums=()):
    s = z.shape[0]
    t = _tile(s, MIX_ROWS)
    n_chunk = t // CHUNK
    ns = len(sums)

    def body(*refs):
        (z_ref, zh_ref, dyn_ref, ot_ref, cv_ref, vg_ref, wt_ref, wtt_ref, bm_ref, scw_ref, ccw_ref, lg_ref,
         lb_ref) = refs[:13]
        (dza_ref, dcb_ref, dcd_ref, dor_ref, dot_ref, ds_ref, dvg_ref, dws_ref, dbm_ref, dscw_ref, dccw_ref, dlg_ref,
         dlb_ref) = refs[13 + ns:26 + ns]
        eb_ref, ed_ref, sh_ref = refs[26 + 2 * ns:29 + 2 * ns]
        i = pl.program_id(0)
        first = i == 0
        if ns:
            _owners_in_steps(refs[13:13 + ns], refs[26 + ns:26 + 2 * ns], refs[29 + 2 * ns:], first, i == s // t - 1)

        @pl.when(first)
        def _():
            for r in (dvg_ref, dws_ref, dbm_ref, dscw_ref, dccw_ref, dlg_ref, dlb_ref):
                r[...] = jnp.zeros_like(r)

        def rms_bwd(y, dn):
            r = _rsqrt_mean(y)
            yn = y * r
            return r * (dn - yn * jnp.mean(dn * yn, axis=-1, keepdims=True))

        vg = vg_ref[...]
        gelu_u, gelu_v, u, rv, vh, vnb, f = _mix_a_fwd(z_ref, vg, wt_ref, bm_ref[...], t)
        dya = rms_bwd(u * f, _cols(dyn_ref, 0, 256))
        du = dya * f
        df = dya * u
        head = lax.broadcasted_iota(jnp.int32, (CHUNK, D_GROUP), 1) // HEAD_DIM
        dvns = []
        dbm = jnp.zeros((CHUNK, D_GROUP), F32)
        for c in range(n_chunk):
            dfc = df[c * CHUNK:(c + 1) * CHUNK, :]
            vc = vnb[c * CHUNK:(c + 1) * CHUNK, :]
            dbm = dbm + dfc
            dvn = jnp.zeros((CHUNK, D_GROUP), F32)
            for h in range(N_HEADS):
                dfh = jnp.where(head == h, dfc, 0.0).astype(BF16)
                dvn = dvn + _dot(wtt_ref[h], dfh)
                dws_ref[h] += _dot_nt(dfh, vc)
            dvns.append(dvn)
        dbm_ref[...] += dbm
        dvn = jnp.concatenate(dvns, axis=0) if n_chunk > 1 else dvns[0]
        dvg_ref[...] += jnp.sum(dvn * vh, axis=0, keepdims=True)
        dvh = dvn * vg
        dv = rv * (dvh - vh * jnp.mean(dvh * vh, axis=-1, keepdims=True))
        dza_ref[:, 0:256] = (du * _gelu_grad(*gelu_u)).astype(BF16)
        dza_ref[:, 256:512] = (dv * _gelu_grad(*gelu_v)).astype(BF16)

        gb, cv = _mix_b_fwd(z_ref, zh_ref, first, scw_ref, eb_ref, t)
        dyb = rms_bwd(gb * cv, _cols(dyn_ref, 256, 512))
        dza_ref[:, 512:768] = (dyb * cv).astype(BF16)
        dcb = dyb * gb
        dcb_ref[...] = dcb
        for k in range(SHORT_K):
            dscw_ref[k:k + 1, :] += jnp.sum(dcb * eb_ref[pl.ds(HALO - (SHORT_K - 1) + k, t), :], axis=0, keepdims=True)

        lg = lg_ref[...]
        xh, rs, ln, sg, window, _ = _mix_d_fwd(z_ref, zh_ref, first, ccw_ref, lg, lb_ref[...], ed_ref, sh_ref, t,
                                               cv_ref[...])
        dyd = rms_bwd(ln * sg, _cols(dyn_ref, 768, 1024))
        dln = dyd * (sg * (1.0 + ln * (1.0 - sg)))
        dlg_ref[...] += jnp.sum(dln * xh, axis=0, keepdims=True)
        dlb_ref[...] += jnp.sum(dln, axis=0, keepdims=True)
        dxh = dln * lg
        dcd = rs * (dxh - jnp.mean(dxh, axis=-1, keepdims=True) - xh * jnp.mean(dxh * xh, axis=-1, keepdims=True))
        dcd_ref[...] = dcd
        for k in range(CONF_K):
            dccw_ref[k:k + 1, :] += jnp.sum(dcd * window(HALO - (CONF_K - 1) + k), axis=0, keepdims=True)

        o = ot_ref[...].reshape(D_GROUP, t).T
        do = rms_bwd(o, _cols(dyn_ref, 512, 768))
        _split_heads(do, dor_ref, dot_ref)
        prod = do.astype(BF16).astype(F32) * o
        for h in range(N_HEADS):
            ds_ref[h] = jnp.sum(prod[:, h * HEAD_DIM:(h + 1) * HEAD_DIM], axis=1, keepdims=True)

    small = [(1, D_GROUP), (N_HEADS, CHUNK, CHUNK), (CHUNK, D_GROUP), (8, D_GROUP), (32, D_GROUP), (1, D_GROUP), (1, D_GROUP)]
    rows, cols = _head_specs(t, s)
    out = pl.pallas_call(
        body, name="mixers_bwd_a", grid=(s // t,),
        in_specs=_mix_specs(t, s) + [pl.BlockSpec((t, D_MODEL), lambda i: (i, 0)),
                                     pl.BlockSpec((N_HEADS, HEAD_DIM, t), lambda i: (0, 0, i)),
                                     pl.BlockSpec((t, D_GROUP), lambda i: (i, 3)),
                                     _full((1, D_GROUP)), _full((N_HEADS, CHUNK, CHUNK)), _full((N_HEADS, CHUNK, CHUNK)),
                                     _full((CHUNK, D_GROUP)), _full((8, D_GROUP)), _full((32, D_GROUP)),
                                     _full((1, D_GROUP)), _full((1, D_GROUP))] + [_ANY] * ns,
        out_specs=[pl.BlockSpec((t, 768), lambda i: (i, 0)), pl.BlockSpec((t, D_GROUP), lambda i: (i, 0)),
                   pl.BlockSpec((t, D_GROUP), lambda i: (i, 0)), rows[0], cols[0],
                   pl.BlockSpec((N_HEADS, t, 1), lambda i: (0, i, 0))]
                  + [_full(sh) for sh in small] + [_ANY] * ns,
        out_shape=[jax.ShapeDtypeStruct((s, 768), BF16), jax.ShapeDtypeStruct((s, D_GROUP), F32),
                   jax.ShapeDtypeStruct((s, D_GROUP), F32), rows[1], cols[1],
                   jax.ShapeDtypeStruct((N_HEADS, s, 1), F32)]
                  + [jax.ShapeDtypeStruct(sh, F32) for sh in small]
                  + [jax.ShapeDtypeStruct(a.shape, a.dtype) for a in sums],
        scratch_shapes=[pltpu.VMEM((HALO + t, D_GROUP), F32), pltpu.VMEM((HALO + t, D_GROUP), F32),
                        pltpu.VMEM((7, HALO + t - 8, D_GROUP), F32)] + (_owner_sems(ns) if ns else []),
        compiler_params=_params(("arbitrary",), V7X_VMEM_LIMIT),
    )(z, z, dyn, o_t, y_abd, p["vg"], p["wt"], p["wtt"], p["bmat"], p["scw"], p["ccw"], p["lg"], p["lb"], *sums)
    return tuple(out[:13]) + (out[13:],) if ns else out


def _mixers_bwd_b(z, dza, dcb, dcd, dq_t, dk_t, dv_t, p):
    s = z.shape[0]
    t = _tile(s, MIX_ROWS)
    per = t // HALO
    n_halo = s // HALO

    def body(z_ref, dza_ref, dcb_ref, dcbn_ref, dcd_ref, dcdn_ref, dq_ref, dk_ref, dv_ref, scw_ref, ccw_ref,
             dz_ref, eb_ref, ed_ref, sh_ref):
        last = pl.program_id(0) == pl.num_programs(0) - 1
        dz_ref[:, 0:768] = dza_ref[...]
        eb_ref[0:t, :] = dcb_ref[...]
        eb_ref[t:t + HALO, :] = jnp.where(last, 0.0, dcbn_ref[...])
        du = jnp.zeros((t, D_GROUP), F32)
        for k in range(SHORT_K):
            du = du + scw_ref[k:k + 1, :] * eb_ref[pl.ds(SHORT_K - 1 - k, t), :]
        dz_ref[:, 768:1024] = (du * _cols(z_ref, 1024, 1280)).astype(BF16)
        dz_ref[:, 1024:1280] = (du * _cols(z_ref, 768, 1024)).astype(BF16)
        for n, r in enumerate((dq_ref, dk_ref, dv_ref)):
            dz_ref[:, 1280 + 256 * n:1536 + 256 * n] = r[...].reshape(D_GROUP, t).T.astype(BF16)
        ed_ref[0:t, :] = dcd_ref[...]
        ed_ref[t:t + HALO, :] = jnp.where(last, 0.0, dcdn_ref[...])
        window = _windows(ed_ref, sh_ref, t)
        dh = jnp.zeros((t, D_GROUP), F32)
        for k in range(CONF_K):
            dh = dh + ccw_ref[k:k + 1, :] * window(CONF_K - 1 - k)
        a = _cols(z_ref, 2048, 2304)
        sg = _sigmoid(_cols(z_ref, 2304, 2560))
        dz_ref[:, 2048:2304] = (dh * sg).astype(BF16)
        dz_ref[:, 2304:2560] = (dh * a * sg * (1.0 - sg)).astype(BF16)

    nxt = lambda i: (jnp.minimum((i + 1) * per, n_halo - 1), 0)
    tr = pl.BlockSpec((N_HEADS, HEAD_DIM, t), lambda i: (0, 0, i))
    return pl.pallas_call(
        body, name="mixers_bwd_b", grid=(s // t,),
        in_specs=[pl.BlockSpec((t, D_IN), lambda i: (i, 0)), pl.BlockSpec((t, 768), lambda i: (i, 0)),
                  pl.BlockSpec((t, D_GROUP), lambda i: (i, 0)), pl.BlockSpec((HALO, D_GROUP), nxt),
                  pl.BlockSpec((t, D_GROUP), lambda i: (i, 0)), pl.BlockSpec((HALO, D_GROUP), nxt),
                  tr, tr, tr, _full((8, D_GROUP)), _full((32, D_GROUP))],
        out_specs=pl.BlockSpec((t, D_IN), lambda i: (i, 0)),
        out_shape=jax.ShapeDtypeStruct((s, D_IN), BF16),
        scratch_shapes=[pltpu.VMEM((HALO + t, D_GROUP), F32), pltpu.VMEM((HALO + t, D_GROUP), F32),
                        pltpu.VMEM((7, HALO + t - 8, D_GROUP), F32)],
        compiler_params=_params(("parallel",), V7X_VMEM_LIMIT),
    )(z, dza, dcb, dcb, dcd, dcd, dq_t, dk_t, dv_t, p["scw"], p["ccw"])


def _split_bf16(v):
    hi = v.astype(BF16)
    return hi, (v - hi.astype(F32)).astype(BF16)


def _att_scores(qs, kts, carries, tri, mask):
    zs = [_dot(q, kt) for q, kt in zip(qs, kts)]
    lms, lbs, parts = [], [], []
    for z in zs:
        soft = jnp.log(1.0 + jnp.exp(-jnp.abs(z)))
        lm = -(jnp.maximum(z, 0.0) + soft)
        lbs.append(lm + z)
        if mask is not None:
            lm = jnp.where(mask, lm, 0.0)
        lms.append(lm)
        parts.append(_split_bf16(lm))
    rights = [_dot(hi, tri) + _dot(lo, tri) for hi, lo in parts]
    ws = []
    for lb, right, carry in zip(lbs, rights, carries):
        w = jnp.exp(lb + right + carry)
        ws.append(w if mask is None else jnp.where(mask, w, 0.0))
    return ws, lbs, [jnp.sum(lm, axis=1, keepdims=True) for lm in lms]


def _att_consts(i):
    j_hi = ((i + 1) * ATT_TQ - 1) // ATT_TK
    row = lax.broadcasted_iota(jnp.int32, (ATT_TQ, ATT_TK), 0) + i * ATT_TQ
    col = lax.broadcasted_iota(jnp.int32, (ATT_TQ, ATT_TK), 1) + j_hi * ATT_TK
    r_i = lax.broadcasted_iota(jnp.int32, (ATT_TK, ATT_TK), 0)
    c_i = lax.broadcasted_iota(jnp.int32, (ATT_TK, ATT_TK), 1)
    return j_hi, col < row, r_i, c_i


def _att_alive(j, carries):
    top = carries[0]
    for c in carries[1:]:
        top = jnp.maximum(top, c)
    return jnp.logical_and(j >= 0, jnp.max(top) > ATT_DEAD)


def _attn_fwd(q_r, k_t, v_t, bufs=(), parts=()):
    assert ATT_TQ == ATT_TK
    s = q_r.shape[1]
    nb = len(bufs)
    nbq = min(ATT_FWD_BLOCKS, s // ATT_TQ)
    step_rows = nbq * ATT_TQ
    lanes = [(h, b) for h in range(N_HEADS) for b in range(nbq)]

    def body(q_ref, kt_ref, vt_ref, *rest):
        o_ref = rest[nb]
        if nb:
            _gather_in_steps(rest[nb + 1:2 * nb + 1], parts, rest[2 * nb + 1:], s // step_rows)
        i = pl.program_id(0)
        _, mask, r_i, c_i = _att_consts(0)
        tri = (r_i > c_i).astype(BF16)
        rows_of = [pl.ds(b * ATT_TQ, ATT_TQ) for _, b in lanes]

        def tiles(t, carries, accs, mask):
            js = [nbq * i + b - t for _, b in lanes]
            live = [None if mask is not None or b == nbq - 1 else j >= 0 for (_, b), j in zip(lanes, js)]
            cols = [pl.ds(pl.multiple_of(jnp.maximum(j, 0) * ATT_TK, ATT_TK), ATT_TK) for j in js]
            ws, _, tots = _att_scores([q_ref[h, r, :] for (h, _), r in zip(lanes, rows_of)],
                                      [kt_ref[h, :, c] for (h, _), c in zip(lanes, cols)], carries, tri, mask)
            ws = [w if ok is None else jnp.where(ok, w, 0.0) for w, ok in zip(ws, live)]
            tots = [tot if ok is None else jnp.where(ok, tot, -1e30) for tot, ok in zip(tots, live)]
            accs = [acc + _dot_nt(vt_ref[h, :, c], w.astype(BF16)) for (h, _), c, acc, w in zip(lanes, cols, accs, ws)]
            return [c + tot for c, tot in zip(carries, tots)], accs

        state = tiles(0, [jnp.zeros((ATT_TQ, 1), F32)] * len(lanes), [jnp.zeros((HEAD_DIM, ATT_TQ), F32)] * len(lanes), mask)

        def cond(c):
            return _att_alive(nbq * i + nbq - 1 - c[0], c[1])

        def step(c):
            return (c[0] + 1,) + tuple(tiles(c[0], c[1], c[2], None))

        _, _, accs = lax.while_loop(cond, step, (1,) + tuple(state))
        for n, (h, _) in enumerate(lanes):
            o_ref[h, :, rows_of[n]] = accs[n]

    whole = pl.BlockSpec((N_HEADS, HEAD_DIM, s), lambda i: (0, 0, 0), pipeline_mode=pl.Buffered(1))
    out = pl.pallas_call(
        body, name="attn_fwd", grid=(s // step_rows,),
        in_specs=[pl.BlockSpec((N_HEADS, step_rows, HEAD_DIM), lambda i: (0, i, 0)), whole, whole] + [_ANY] * nb,
        out_specs=[pl.BlockSpec((N_HEADS, HEAD_DIM, step_rows), lambda i: (0, 0, i))] + [_ANY] * nb,
        out_shape=[jax.ShapeDtypeStruct((N_HEADS, HEAD_DIM, s), F32)] + [jax.ShapeDtypeStruct(b.shape, b.dtype) for b in bufs],
        input_output_aliases={3 + k: 1 + k for k in range(nb)},
        scratch_shapes=_gather_sems(parts) if nb else [],
        compiler_params=_params(("arbitrary",), V7X_VMEM_LIMIT),
    )(q_r, k_t, v_t, *bufs)
    return (out[0], out[1:]) if nb else out[0]


ATT_FWD_BLOCKS = 2
ATT_BWD_HEADS = 2
ATT_BWD_BLOCKS = 2


def _attn_bwd(q_r, q_t, k_t, v_t, do_r, do_t, dsum):
    assert ATT_TQ == ATT_TK
    s = q_r.shape[1]
    hps = ATT_BWD_HEADS
    nbq = min(ATT_BWD_BLOCKS, s // ATT_TQ)
    step_rows = nbq * ATT_TQ
    lanes = [(h, b) for h in range(hps) for b in range(nbq)]

    def body(q_ref, qt_ref, kt_ref, vt_ref, do_ref, dot_ref, ds_ref, dq_ref, dk_ref, dv_ref):
        i = pl.program_id(1)

        @pl.when(i == 0)
        def _():
            dk_ref[...] = jnp.zeros_like(dk_ref)
            dv_ref[...] = jnp.zeros_like(dv_ref)

        _, mask, r_i, c_i = _att_consts(0)
        tri_r = (r_i > c_i).astype(BF16)
        tri_ge = (r_i >= c_i).astype(BF16)
        rows_of = [pl.ds(b * ATT_TQ, ATT_TQ) for _, b in lanes]

        def tiles(t, carries, gsums, accs, mask):
            js = [nbq * i + b - t for _, b in lanes]
            live = [None if mask is not None or b == nbq - 1 else j >= 0 for (_, b), j in zip(lanes, js)]
            cols = [pl.ds(pl.multiple_of(jnp.maximum(j, 0) * ATT_TK, ATT_TK), ATT_TK) for j in js]
            kts = [kt_ref[h, :, c] for (h, _), c in zip(lanes, cols)]
            das = [_dot(do_ref[h, r, :], vt_ref[h, :, c]) for (h, _), r, c in zip(lanes, rows_of, cols)]
            ws, lbs, tots = _att_scores([q_ref[h, r, :] for (h, _), r in zip(lanes, rows_of)], kts, carries, tri_r, mask)
            ws = [w if ok is None else jnp.where(ok, w, 0.0) for w, ok in zip(ws, live)]
            tots = [tot if ok is None else jnp.where(ok, tot, -1e30) for tot, ok in zip(tots, live)]
            wbs = [w.astype(BF16) for w in ws]
            gs = [wb.astype(F32) * da for wb, da in zip(wbs, das)]
            parts = [_split_bf16(g) for g in gs]
            sfx = [_dot(hi, tri_ge) + _dot(lo, tri_ge) for hi, lo in parts]
            for n, (h, _) in enumerate(lanes):
                dv_ref[h, :, cols[n]] += _dot(dot_ref[h, :, rows_of[n]], wbs[n])
            dzs = []
            for n, (h, _) in enumerate(lanes):
                left = ds_ref[h, rows_of[n], :] - gsums[n] - sfx[n]
                dz = gs[n] - jnp.exp(lbs[n]) * (gs[n] + left)
                if mask is not None:
                    dz = jnp.where(mask, dz, 0.0)
                elif live[n] is not None:
                    dz = jnp.where(live[n], dz, 0.0)
                dzs.append(dz.astype(BF16))
            for n, (h, _) in enumerate(lanes):
                dk_ref[h, :, cols[n]] += _dot(qt_ref[h, :, rows_of[n]], dzs[n])
            accs = [acc + _dot_nt(kt, dz) for acc, kt, dz in zip(accs, kts, dzs)]
            gsums = [gsum + jnp.sum(g, axis=1, keepdims=True) for gsum, g in zip(gsums, gs)]
            return [c + tot for c, tot in zip(carries, tots)], gsums, accs

        col0 = [jnp.zeros((ATT_TQ, 1), F32)] * len(lanes)
        state = tiles(0, col0, col0, [jnp.zeros((HEAD_DIM, ATT_TQ), F32)] * len(lanes), mask)

        def cond(c):
            return _att_alive(nbq * i + nbq - 1 - c[0], c[1])

        def step(c):
            return (c[0] + 1,) + tuple(tiles(c[0], c[1], c[2], c[3], None))

        _, _, _, accs = lax.while_loop(cond, step, (1,) + tuple(state))
        for n, (h, _) in enumerate(lanes):
            dq_ref[h, :, rows_of[n]] = accs[n] * ATT_SCALE

    whole = pl.BlockSpec((hps, HEAD_DIM, s), lambda g, i: (g, 0, 0))
    whole_in = pl.BlockSpec((hps, HEAD_DIM, s), lambda g, i: (g, 0, 0), pipeline_mode=pl.Buffered(1))
    rows = pl.BlockSpec((hps, step_rows, HEAD_DIM), lambda g, i: (g, i, 0))
    cols = pl.BlockSpec((hps, HEAD_DIM, step_rows), lambda g, i: (g, 0, i))
    shape = jax.ShapeDtypeStruct((N_HEADS, HEAD_DIM, s), F32)
    return pl.pallas_call(
        body, name="attn_bwd", grid=(N_HEADS // hps, s // step_rows),
        in_specs=[rows, cols, whole_in, whole_in, rows, cols, pl.BlockSpec((hps, step_rows, 1), lambda g, i: (g, i, 0))],
        out_specs=[cols, whole, whole],
        out_shape=[shape, shape, shape],
        compiler_params=_params(("parallel", "arbitrary"), V7X_VMEM_LIMIT),
    )(q_r, q_t, k_t, v_t, do_r, do_t, dsum)


def _out_proj(x, y_abd, o_t, gain, w, layer):
    s = x.shape[0]
    t = _tile(s, 1024)

    def body(x_ref, y_ref, ot_ref, g_ref, w_ref, x1_ref, yn_ref):
        o = ot_ref[...].reshape(D_GROUP, t).T
        groups = [y_ref[:, 0:256], y_ref[:, 256:512], o * _rsqrt_mean(o), y_ref[:, 512:768]]
        g = g_ref[...]
        acc = None
        for b, yn in enumerate(groups):
            cols = slice(256 * b, 256 * (b + 1))
            yn_ref[:, cols] = yn.astype(BF16)
            part = _dot((yn * g[:, cols]).astype(BF16), w_ref[b])
            acc = part if acc is None else acc + part
        x1_ref[...] = x_ref[...] + acc

    return pl.pallas_call(
        body, name="out_proj", grid=(s // t,),
        in_specs=[pl.BlockSpec((t, D_MODEL), lambda i: (i, 0)), pl.BlockSpec((t, 768), lambda i: (i, 0)),
                  pl.BlockSpec((N_HEADS, HEAD_DIM, t), lambda i: (0, 0, i)), _full((1, D_MODEL)),
                  pl.BlockSpec((N_BLK, None, D_GROUP, D_MODEL), lambda i: (0, layer, 0, 0))],
        out_specs=[pl.BlockSpec((t, D_MODEL), lambda i: (i, 0)), pl.BlockSpec((t, D_MODEL), lambda i: (i, 0))],
        out_shape=[jax.ShapeDtypeStruct((s, D_MODEL), F32), jax.ShapeDtypeStruct((s, D_MODEL), BF16)],
        compiler_params=_params(("parallel",), V7X_VMEM_LIMIT),
    )(x, y_abd, o_t, gain, w)


def _out_proj_bwd(x1, g_ffn, dh, dx2, yn, gain, w, layer, halves=()):
    s = dx2.shape[0]
    t = _tile(s, 512)
    nh = len(halves)

    def body(x_ref, gf_ref, dh_ref, dx2_ref, yn_ref, g_ref, w_ref, *rest):
        dx1_ref, dgf_ref, dyn_ref, dg_ref, dw_ref = rest[nh:nh + 5]
        if nh:
            start, finish = _halves_stages(rest[:nh], rest[nh + 5:2 * nh + 5], rest[2 * nh + 5:])
            pl.when(pl.program_id(0) == 0)(start)
            pl.when(pl.program_id(0) == s // t - 1)(finish)

        @pl.when(pl.program_id(0) == 0)
        def _():
            dg_ref[...] = jnp.zeros_like(dg_ref)
            dw_ref[...] = jnp.zeros_like(dw_ref)
            dgf_ref[...] = jnp.zeros_like(dgf_ref)

        dx1, dgf = _rms_bwd_rows(x_ref[...], gf_ref[...], dh_ref[...], dx2_ref[...])
        dx1_ref[...] = dx1
        dgf_ref[...] += dgf
        dxb = dx1.astype(BF16)
        g = g_ref[...]
        yn = yn_ref[...].astype(F32)
        yg = (yn * g).astype(BF16)
        for b in range(N_BLK):
            cols = slice(256 * b, 256 * (b + 1))
            dyg = _dot_nt(dxb, w_ref[b])
            dw_ref[b] += _dot_tn(yg[:, cols], dxb)
            dg_ref[:, cols] += jnp.sum(dyg * yn[:, cols], axis=0, keepdims=True)
            dyn_ref[:, cols] = (dyg * g[:, cols]).astype(BF16)

    row = pl.BlockSpec((t, D_MODEL), lambda i: (i, 0))
    vec = _full((1, D_MODEL))
    out = pl.pallas_call(
        body, name="out_proj_bwd", grid=(s // t,),
        in_specs=[row, vec, row, row, row, vec, pl.BlockSpec((N_BLK, None, D_GROUP, D_MODEL), lambda i: (0, layer, 0, 0))]
                 + [_ANY] * nh,
        out_specs=[row, vec, row, vec, _full((N_BLK, D_GROUP, D_MODEL))] + [_ANY] * nh,
        out_shape=[jax.ShapeDtypeStruct((s, D_MODEL), F32), jax.ShapeDtypeStruct((1, D_MODEL), F32),
                   jax.ShapeDtypeStruct((s, D_MODEL), BF16), jax.ShapeDtypeStruct((1, D_MODEL), F32),
                   jax.ShapeDtypeStruct((N_BLK, D_GROUP, D_MODEL), F32)] + _halves_shapes(halves),
        scratch_shapes=_halves_sems(nh) if nh else [],
        compiler_params=_params(("arbitrary",), V7X_VMEM_LIMIT),
    )(x1, g_ffn, dh, dx2, yn, gain, w, *halves)
    return tuple(out[:5]) + (out[5:],) if nh else out


def _ffn(x, g, w_up, w_down, layer):
    s = x.shape[0]
    t = _tile(s, 1024)

    def body(x_ref, g_ref, wu_ref, wd_ref, x2_ref, p_ref, h_ref):
        @pl.when(pl.program_id(1) == 0)
        def _():
            xv = x_ref[...]
            h_ref[...] = (xv * _rsqrt_mean(xv) * g_ref[...]).astype(BF16)
            x2_ref[...] = xv

        pre = _dot(h_ref[...], wu_ref[...])
        p_ref[...] = pre.astype(BF16)
        a = jnp.maximum(pre, 0.0)
        x2_ref[...] += _dot((a * a).astype(BF16), wd_ref[...])

    wspec = pl.BlockSpec((None, None, D_MODEL, D_FF_BLK), lambda i, j: (j, layer, 0, 0))
    row = pl.BlockSpec((t, D_MODEL), lambda i, j: (i, 0))
    return pl.pallas_call(
        body, name="ffn", grid=(s // t, N_BLK),
        in_specs=[row, pl.BlockSpec((1, D_MODEL), lambda i, j: (0, 0)), wspec, wspec],
        out_specs=[row, pl.BlockSpec((t, D_FF_BLK), lambda i, j: (i, j)), row],
        out_shape=[jax.ShapeDtypeStruct((s, D_MODEL), F32), jax.ShapeDtypeStruct((s, N_BLK * D_FF_BLK), BF16),
                   jax.ShapeDtypeStruct((s, D_MODEL), BF16)],
        compiler_params=_params(("parallel", "arbitrary"), V7X_VMEM_LIMIT),
    )(x, g, w_up, w_down)


def _loss_head(x, g, target):
    s = x.shape[0]
    t = _tile(s, 512)

    def body(x_ref, g_ref, t_ref, l_ref, dx_ref, dxb_ref, dg_ref):
        @pl.when(pl.program_id(0) == 0)
        def _():
            l_ref[...] = jnp.zeros_like(l_ref)
            dg_ref[...] = jnp.zeros_like(dg_ref)

        xv = x_ref[...]
        g = g_ref[...]
        r = _rsqrt_mean(xv)
        xh = xv * r
        err = xh * g - t_ref[...]
        l_ref[...] += 0.5 * jnp.sum(jnp.mean(err * err, axis=-1, keepdims=True), axis=0, keepdims=True)
        dy = err * (1.0 / D_MODEL)
        dg_ref[...] += jnp.sum(dy * xh, axis=0, keepdims=True)
        dxh = dy * g
        dx = r * (dxh - xh * jnp.mean(dxh * xh, axis=-1, keepdims=True))
        dx_ref[...] = dx
        dxb_ref[...] = dx.astype(BF16)

    row = pl.BlockSpec((t, D_MODEL), lambda i: (i, 0))
    return pl.pallas_call(
        body, name="loss_head", grid=(s // t,),
        in_specs=[row, _full((1, D_MODEL)), row],
        out_specs=[_full((1, 128)), row, row, _full((1, D_MODEL))],
        out_shape=[jax.ShapeDtypeStruct((1, 128), F32), jax.ShapeDtypeStruct((s, D_MODEL), F32),
                   jax.ShapeDtypeStruct((s, D_MODEL), BF16), jax.ShapeDtypeStruct((1, D_MODEL), F32)],
        compiler_params=_params(("arbitrary",)),
    )(x, g, target)


def _rms_bwd_rows(xv, g, dh, dres):
    r = _rsqrt_mean(xv)
    xh = xv * r
    dxh = dh * g
    dx = dres + r * (dxh - xh * jnp.mean(dxh * xh, axis=-1, keepdims=True))
    return dx, jnp.sum(dh * xh, axis=0, keepdims=True)


def _ffn_bwd(dxb, p, w_up, w_down, layer, sums=()):
    s = dxb.shape[0]
    t = _tile(s, 1024)
    ns = len(sums)

    def body(dx_ref, p_ref, wu_ref, wd_ref, *rest):
        dp_ref, dh_ref = rest[ns:ns + 2]
        if ns:
            i, j = pl.program_id(0), pl.program_id(1)
            _owners_in_steps(rest[:ns], rest[ns + 2:2 * ns + 2], rest[2 * ns + 2:],
                             jnp.logical_and(i == 0, j == 0), jnp.logical_and(i == s // t - 1, j == N_BLK - 1))
        da = _dot_nt(dx_ref[...], wd_ref[...])
        a = jnp.maximum(p_ref[...].astype(F32), 0.0)
        dp = (da * (2.0 * a)).astype(BF16)
        dp_ref[...] = dp
        dh = _dot_nt(dp, wu_ref[...])

        @pl.when(pl.program_id(1) == 0)
        def _():
            dh_ref[...] = dh

        @pl.when(pl.program_id(1) != 0)
        def _():
            dh_ref[...] += dh

    wspec = pl.BlockSpec((None, None, D_MODEL, D_FF_BLK), lambda i, j: (j, layer, 0, 0))
    row = pl.BlockSpec((t, D_MODEL), lambda i, j: (i, 0))
    blk = pl.BlockSpec((t, D_FF_BLK), lambda i, j: (i, j))
    out = pl.pallas_call(
        body, name="ffn_bwd", grid=(s // t, N_BLK),
        in_specs=[row, blk, wspec, wspec] + [_ANY] * ns, out_specs=[blk, row] + [_ANY] * ns,
        out_shape=[jax.ShapeDtypeStruct((s, N_BLK * D_FF_BLK), BF16), jax.ShapeDtypeStruct((s, D_MODEL), F32)]
                  + [jax.ShapeDtypeStruct(a.shape, a.dtype) for a in sums],
        scratch_shapes=_owner_sems(ns) if ns else [],
        compiler_params=_params(("arbitrary" if ns else "parallel", "arbitrary"), V7X_VMEM_LIMIT),
    )(dxb, p, w_up, w_down, *sums)
    return (out[0], out[1], out[2:]) if ns else out


def _ffn_wgrad(hb, p, dp, dxb):
    s = hb.shape[0]
    t = _tile(s, 1024)

    def body(h_ref, p_ref, dp_ref, dx_ref, du_ref, dd_ref):
        @pl.when(pl.program_id(1) == 0)
        def _():
            du_ref[...] = jnp.zeros_like(du_ref)
            dd_ref[...] = jnp.zeros_like(dd_ref)

        a = jnp.maximum(p_ref[...].astype(F32), 0.0)
        du_ref[...] += _dot_tn(h_ref[...], dp_ref[...])
        dd_ref[...] += _dot_tn((a * a).astype(BF16), dx_ref[...])

    row = pl.BlockSpec((t, D_MODEL), lambda j, i: (i, 0))
    blk = pl.BlockSpec((t, D_FF_BLK), lambda j, i: (i, j))
    out = pl.BlockSpec((None, D_MODEL, D_FF_BLK), lambda j, i: (j, 0, 0))
    shape = jax.ShapeDtypeStruct((N_BLK, D_MODEL, D_FF_BLK), F32)
    return pl.pallas_call(
        body, name="ffn_wgrad", grid=(N_BLK, s // t),
        in_specs=[row, blk, blk, row], out_specs=[out, out], out_shape=[shape, shape],
        compiler_params=_params(("parallel", "arbitrary"), V7X_VMEM_LIMIT),
    )(hb, p, dp, dxb)


def _in_proj_bwd(x, g, dx1, dz, w, layer):
    s = x.shape[0]
    t = _tile(s, 512)

    def body(x_ref, g_ref, dx1_ref, dz_ref, w_ref, dx0_ref, dxb_ref, dg_ref, wide_ref):
        _pair_blocks(w_ref, wide_ref)

        @pl.when(pl.program_id(0) == 0)
        def _():
            dg_ref[...] = jnp.zeros_like(dg_ref)

        dh = _dot_nt(dz_ref[:, 0:2 * W_IN_BLK], wide_ref[0])
        for n in range(1, N_BLK // 2):
            dh = dh + _dot_nt(dz_ref[:, 2 * n * W_IN_BLK:2 * (n + 1) * W_IN_BLK], wide_ref[n])
        dx, dg = _rms_bwd_rows(x_ref[...], g_ref[...], dh, dx1_ref[...])
        dx0_ref[...] = dx
        dxb_ref[...] = dx.astype(BF16)
        dg_ref[...] += dg

    row = pl.BlockSpec((t, D_MODEL), lambda i: (i, 0))
    return pl.pallas_call(
        body, name="in_proj_bwd", grid=(s // t,),
        in_specs=[row, _full((1, D_MODEL)), row, pl.BlockSpec((t, D_IN), lambda i: (i, 0)),
                  pl.BlockSpec((N_BLK, None, D_MODEL, W_IN_BLK), lambda i: (0, layer, 0, 0))],
        out_specs=[row, row, _full((1, D_MODEL))],
        out_shape=[jax.ShapeDtypeStruct((s, D_MODEL), F32), jax.ShapeDtypeStruct((s, D_MODEL), BF16),
                   jax.ShapeDtypeStruct((1, D_MODEL), F32)],
        scratch_shapes=[pltpu.VMEM((N_BLK // 2, D_MODEL, 2 * W_IN_BLK), BF16)],
        compiler_params=_params(("arbitrary",), V7X_VMEM_LIMIT),
    )(x, g, dx1, dz, w)


def _in_proj_wgrad(hb, dz):
    s = hb.shape[0]
    t = _tile(s, 1024)

    def body(h_ref, dz_ref, dw_ref, wide_ref):
        @pl.when(pl.program_id(0) == 0)
        def _():
            wide_ref[...] = jnp.zeros_like(wide_ref)

        h = h_ref[...]
        for n in range(N_BLK // 2):
            wide_ref[n] += _dot_tn(h, dz_ref[:, 2 * n * W_IN_BLK:2 * (n + 1) * W_IN_BLK])

        @pl.when(pl.program_id(0) == s // t - 1)
        def _():
            for b in range(N_BLK):
                dw_ref[b] = wide_ref[b // 2, :, (b % 2) * W_IN_BLK:(b % 2 + 1) * W_IN_BLK]

    return pl.pallas_call(
        body, name="in_proj_wgrad", grid=(s // t,),
        in_specs=[pl.BlockSpec((t, D_MODEL), lambda i: (i, 0)), pl.BlockSpec((t, D_IN), lambda i: (i, 0))],
        out_specs=_full((N_BLK, D_MODEL, W_IN_BLK)),
        out_shape=jax.ShapeDtypeStruct((N_BLK, D_MODEL, W_IN_BLK), F32),
        scratch_shapes=[pltpu.VMEM((N_BLK // 2, D_MODEL, 2 * W_IN_BLK), F32)],
        compiler_params=_params(("arbitrary",), V7X_VMEM_LIMIT),
    )(hb, dz)


def _layer_params(small, layer):
    tril = jnp.tril(jnp.ones((CHUNK, CHUNK), bool))
    ws = jnp.where(tril, small["gmlp_w_s"][layer], 0.0)
    bmat = jnp.repeat(small["gmlp_b_s"][layer].T, HEAD_DIM, axis=1)
    scw = jnp.zeros((8, D_GROUP), F32).at[:SHORT_K].set(small["short_conv_w"][layer])
    ccw = jnp.zeros((32, D_GROUP), F32).at[:CONF_K].set(small["conf_conv_w"][layer])
    return dict(vg=small["gmlp_v_g"][layer][None], wt=ws.astype(BF16), wtt=jnp.swapaxes(ws, 1, 2).astype(BF16),
                bmat=bmat, scw=scw, ccw=ccw, lg=small["conf_ln_g"][layer][None], lb=small["conf_ln_b"][layer][None])


def _local_step(x, target, big, small, gather_pending=False, core=None):
    saved = []
    for l in range(DEPTH):
        p = _layer_params(small, l)
        z, hb, q_r, q_t, k_t, v_t = _in_proj(x, small["norm_mix_g"][l][None], big["w_in"], l)
        if gather_pending and l == 0:
            late = ("w_out", "w_up", "w_down")
            y_abd, filled = _mixers_fwd(z, p, [big[k] for k in late], [(n, 0) for n in range(len(late))])
            big = {**big, **dict(zip(late, filled))}
            o_t, filled = _attn_fwd(q_r, k_t, v_t, [big[k] for k in _BIG], [(n, 1) for n in range(len(_BIG))])
            big = dict(zip(_BIG, filled))
        else:
            y_abd = _mixers_fwd(z, p)
            o_t = _attn_fwd(q_r, k_t, v_t)
        x1, yn = _out_proj(x, y_abd, o_t, small["mix_out_g"][l][None], big["w_out"], l)
        x2, pre, h2b = _ffn(x1, small["norm_ffn_g"][l][None], big["w_up"], big["w_down"], l)
        saved.append(dict(p=p, x0=x, z=z, hb=hb, q_r=q_r, q_t=q_t, k_t=k_t, v_t=v_t, o_t=o_t, x1=x1, yn=yn, pre=pre,
                          h2b=h2b, y_abd=y_abd))
        x = x2

    loss, dx, dxb, d_final = _loss_head(x, small["final_norm_g"][None], target)

    g = {k: [None] * DEPTH for k in ("w_in", "w_out", "w_up", "w_down", "norm_mix_g", "gmlp_v_g", "gmlp_w_s", "gmlp_b_s",
                                     "short_conv_w", "conf_conv_w", "conf_ln_g", "conf_ln_b", "mix_out_g", "norm_ffn_g")}
    tril = jnp.tril(jnp.ones((CHUNK, CHUNK), bool))
    early = {}
    ffn_sums = [[]] * DEPTH
    for l in reversed(range(DEPTH)):
        sv = saved[l]
        p = sv["p"]
        riding, sums = [], []
        if core is not None and l == 0:
            riding = [(k, 1) for k in _BIG]
            sums = _chip_sums(core, [g["w_in"][1], g["w_out"][1]]) + ffn_sums[1]
        dpre, dh, *got = _ffn_bwd(dxb, sv["pre"], big["w_up"], big["w_down"], l, sums)
        early.update(zip(riding, zip(sums, *got)))
        g["w_up"][l], g["w_down"][l] = _ffn_wgrad(sv["h2b"], sv["pre"], dpre, dxb)
        halves = [g["w_up"][l], g["w_down"][l]] if core is not None else []
        dx1, g["norm_ffn_g"][l], dyn, g["mix_out_g"][l], g["w_out"][l], *swapped = _out_proj_bwd(
            sv["x1"], small["norm_ffn_g"][l][None], dh, dx, sv["yn"], small["mix_out_g"][l][None], big["w_out"], l,
            halves)
        if halves:
            ffn_sums[l] = [_add_pairs(core, a, b) for a, b in zip(halves, *swapped)]
        riding, sums = [], []
        if core is not None and l == 0:
            riding, sums = [("w_up", 0), ("w_down", 0)], ffn_sums[0]
        (dza, dcb, dcd, do_r, do_t, dsum, dvg, dws, dbm, dscw, dccw, dlg, dlb, *got) = _mixers_bwd_a(
            sv["z"], dyn, sv["o_t"], sv["y_abd"], p, sums)
        early.update(zip(riding, zip(sums, *got)))
        dq_t, dk_t, dv_t = _attn_bwd(sv["q_r"], sv["q_t"], sv["k_t"], sv["v_t"], do_r, do_t, dsum)
        dz = _mixers_bwd_b(sv["z"], dza, dcb, dcd, dq_t, dk_t, dv_t, p)
        dx, dxb, g["norm_mix_g"][l] = _in_proj_bwd(sv["x0"], small["norm_mix_g"][l][None], dx1, dz, big["w_in"], l)
        g["w_in"][l] = _in_proj_wgrad(sv["hb"], dz)
        g["gmlp_v_g"][l] = dvg[0]
        g["gmlp_w_s"][l] = jnp.where(tril, dws, 0.0)
        g["gmlp_b_s"][l] = dbm.reshape(CHUNK, N_HEADS, HEAD_DIM).sum(-1).T
        g["short_conv_w"][l] = dscw[:SHORT_K]
        g["conf_conv_w"][l] = dccw[:CONF_K]
        g["conf_ln_g"][l] = dlg[0]
        g["conf_ln_b"][l] = dlb[0]
        g["norm_mix_g"][l] = g["norm_mix_g"][l][0]
        g["mix_out_g"][l] = g["mix_out_g"][l][0]
        g["norm_ffn_g"][l] = g["norm_ffn_g"][l][0]
    grads = {k: v if k in ("w_in", "w_out", "w_up", "w_down") else jnp.stack(v) for k, v in g.items()}
    grads["final_norm_g"] = d_final[0]
    return loss, dx, grads, early


_ANY = pl.BlockSpec(memory_space=pl.ANY)


def _mesh_place():
    x, y, c = lax.axis_index("x"), lax.axis_index("y"), lax.axis_index("c")
    chips = [(1 - x, y), (x, 1 - y), (1 - x, 1 - y)]
    return x, y, c, 2 * x + y, chips


def _gather_stages(bufs, parts, sems):
    ici_send, ici_recv, d2d_send, d2d_recv = sems
    x, y, c, me, chips = _mesh_place()
    blk = [2 * chip[0] + chip[1] for chip in chips]
    pairs = [(p, r) for p in range(len(parts)) for r in range(3)]

    def rows(p, block, half_of):
        k, layer = parts[p]
        half = bufs[k].shape[2] // 2
        return bufs[k].at[block, layer, pl.ds(half_of * half, half), :]

    def ici(p, r, block):
        return pltpu.make_async_remote_copy(
            src_ref=rows(p, me, c), dst_ref=rows(p, block, c), send_sem=ici_send.at[3 * p + r],
            recv_sem=ici_recv.at[3 * p + r], device_id=(chips[r][0], chips[r][1], c), device_id_type=MESH)

    def d2d(p, r, half_of):
        part = rows(p, blk[r], half_of)
        return pltpu.make_async_remote_copy(
            src_ref=part, dst_ref=part, send_sem=d2d_send.at[3 * p + r], recv_sem=d2d_recv.at[3 * p + r],
            device_id=(x, y, 1 - c), device_id_type=MESH)

    def start():
        for p, r in pairs:
            ici(p, r, me).start()

    def forward(p):
        for r in range(3):
            ici(p, r, blk[r]).wait_recv()
            d2d(p, r, c).start()

    def finish():
        for p, r in pairs:
            d2d(p, r, 1 - c).wait_recv()
        for p, r in pairs:
            ici(p, r, me).wait_send()
            d2d(p, r, c).wait_send()

    return start, forward, finish


def _gather_sems(parts):
    return [pltpu.SemaphoreType.DMA((3 * len(parts),)) for _ in range(4)]


def _gather_in_steps(bufs, parts, sems, n_steps):
    start, forward, finish = _gather_stages(bufs, parts, sems)
    i = pl.program_id(0)
    pl.when(i == 0)(start)
    for p in range(len(parts)):
        pl.when(i == n_steps * (2 * p + 3) // (2 * len(parts) + 2))(lambda p=p: forward(p))
    pl.when(i == n_steps - 1)(finish)


def _gather_first(bufs, parts, whole):
    n, m = len(bufs), len(whole)

    def body(*refs):
        whole_in, buf_out, whole_out = refs[n:n + m], refs[n + m:2 * n + m], refs[2 * n + m:2 * (n + m)]
        sems = refs[2 * (n + m):]
        send_sems, recv_sems, local_sems = sems[4:]
        x, y, c, me, chips = _mesh_place()
        start, forward, finish = _gather_stages(buf_out, parts, sems[:4])

        def push(k, r, block):
            return pltpu.make_async_remote_copy(
                src_ref=whole_in[k], dst_ref=whole_out[k].at[block], send_sem=send_sems.at[3 * k + r],
                recv_sem=recv_sems.at[3 * k + r], device_id=(chips[r][0], chips[r][1], c), device_id_type=MESH)

        local = [pltpu.make_async_copy(whole_in[k], whole_out[k].at[me], local_sems.at[k]) for k in range(m)]
        for cp in local:
            cp.start()
        start()
        for k in range(m):
            for r in range(3):
                push(k, r, me).start()
        for p in range(len(parts)):
            forward(p)
        for k in range(m):
            for r, chip in enumerate(chips):
                push(k, r, 2 * chip[0] + chip[1]).wait_recv()
        for k in range(m):
            for r in range(3):
                push(k, r, me).wait_send()
        finish()
        for cp in local:
            cp.wait()

    return pl.pallas_call(
        body, name="gather_first",
        in_specs=[_ANY] * (n + m), out_specs=[_ANY] * (n + m),
        out_shape=[jax.ShapeDtypeStruct(b.shape, b.dtype) for b in bufs]
                  + [jax.ShapeDtypeStruct((N_BLK,) + b.shape, b.dtype) for b in whole],
        input_output_aliases={k: k for k in range(n)},
        scratch_shapes=_gather_sems(parts) + [pltpu.SemaphoreType.DMA((3 * m,)), pltpu.SemaphoreType.DMA((3 * m,)),
                                              pltpu.SemaphoreType.DMA((m,))],
    )(*bufs, *whole)


def _swap_halves(gs):
    n = len(gs)

    def body(*refs):
        start, finish = _halves_stages(refs[:n], refs[n:2 * n], refs[2 * n:])
        start()
        finish()

    return pl.pallas_call(
        body, name="swap_halves", in_specs=[_ANY] * n, out_specs=[_ANY] * n,
        out_shape=_halves_shapes(gs), scratch_shapes=_halves_sems(n),
    )(*gs)


def _halves_stages(ins, outs, sems):
    send_sems, recv_sems = sems
    x, y, c, _, _ = _mesh_place()

    def copy(k):
        half = ins[k].shape[1] // 2
        return pltpu.make_async_remote_copy(
            src_ref=ins[k].at[:, pl.ds((1 - c) * half, half), :], dst_ref=outs[k],
            send_sem=send_sems.at[k], recv_sem=recv_sems.at[k], device_id=(x, y, 1 - c), device_id_type=MESH)

    def start():
        for k in range(len(ins)):
            copy(k).start()

    def finish():
        for k in range(len(ins)):
            copy(k).wait()

    return start, finish


def _halves_shapes(gs):
    return [jax.ShapeDtypeStruct((g.shape[0], g.shape[1] // 2, g.shape[2]), F32) for g in gs]


def _halves_sems(n):
    return [pltpu.SemaphoreType.DMA((n,)), pltpu.SemaphoreType.DMA((n,))]


def _owner_stages(ins, outs, sems):
    send_sems, recv_sems = sems
    x, y, c, me, chips = _mesh_place()
    pairs = [(k, r) for k in range(len(ins)) for r in range(3)]

    def remote(k, r, src_block, dst_block):
        return pltpu.make_async_remote_copy(
            src_ref=ins[k].at[src_block], dst_ref=outs[k].at[dst_block], send_sem=send_sems.at[3 * k + r],
            recv_sem=recv_sems.at[3 * k + r], device_id=(chips[r][0], chips[r][1], c), device_id_type=MESH)

    def start():
        for k, r in pairs:
            remote(k, r, 2 * chips[r][0] + chips[r][1], me).start()

    def finish():
        for k, r in pairs:
            remote(k, r, me, 2 * chips[r][0] + chips[r][1]).wait_recv()
        for k, r in pairs:
            remote(k, r, 2 * chips[r][0] + chips[r][1], me).wait_send()

    return start, finish


def _owner_sems(n):
    return [pltpu.SemaphoreType.DMA((3 * n,)), pltpu.SemaphoreType.DMA((3 * n,))]


def _owners_in_steps(ins, outs, sems, first, last):
    start, finish = _owner_stages(ins, outs, sems)
    pl.when(first)(start)
    pl.when(last)(finish)


def _send_to_owners(sums):
    n = len(sums)

    def body(*refs):
        start, finish = _owner_stages(refs[:n], refs[n:2 * n], refs[2 * n:])
        start()
        finish()

    return pl.pallas_call(
        body, name="send_to_owners", in_specs=[_ANY] * n, out_specs=[_ANY] * n,
        out_shape=[jax.ShapeDtypeStruct(s.shape, s.dtype) for s in sums],
        scratch_shapes=_owner_sems(n),
    )(*sums)


def _swap_reduced(fs):
    n = len(fs)

    def body(*refs):
        ins, outs, (send_sems, recv_sems) = refs[:n], refs[n:2 * n], refs[2 * n:]
        x, y, c, _, _ = _mesh_place()
        cps = [pltpu.make_async_remote_copy(src_ref=ins[k], dst_ref=outs[k], send_sem=send_sems.at[k],
                                            recv_sem=recv_sems.at[k], device_id=(x, y, 1 - c), device_id_type=MESH)
               for k in range(n)]
        for cp in cps:
            cp.start()
        for cp in cps:
            cp.wait()

    return pl.pallas_call(
        body, name="swap_reduced", in_specs=[_ANY] * n, out_specs=[_ANY] * n,
        out_shape=[jax.ShapeDtypeStruct(f.shape, F32) for f in fs],
        scratch_shapes=[pltpu.SemaphoreType.DMA((n,)), pltpu.SemaphoreType.DMA((n,))],
    )(*fs)


def _row_tile(rows):
    return min(rows, 256)


def _chip_sums(core, grads):
    if not grads:
        return []
    return [_add_pairs(core, a, b) for a, b in zip(grads, _swap_halves(grads))]


def _add_pairs(core, g, other):
    n, half, cols = other.shape
    t = _row_tile(half)
    per_half = half // t

    def body(c_ref, a_ref, b_ref, o_ref):
        o_ref[...] = (a_ref[...] + b_ref[...]).astype(BF16)

    spec = pl.BlockSpec((None, t, cols), lambda i, j, c_ref: (i, j, 0))
    return pl.pallas_call(
        body, name="add_pairs",
        grid_spec=pltpu.PrefetchScalarGridSpec(
            num_scalar_prefetch=1, grid=(n, per_half),
            in_specs=[pl.BlockSpec((None, t, cols), lambda i, j, c_ref: (i, c_ref[0] * per_half + j, 0)), spec],
            out_specs=spec),
        out_shape=jax.ShapeDtypeStruct(other.shape, BF16), compiler_params=_params(("parallel", "parallel")),
    )(core, g, other)


def _add_chips(me, s1, r2):
    _, r, cols = r2.shape
    t = _row_tile(r)

    def body(me_ref, s_ref, r_ref, o_ref):
        own = s_ref[...].astype(F32)
        parts = [jnp.where(me_ref[0] == k, own, r_ref[k].astype(F32)) for k in range(N_BLK)]
        o_ref[...] = ((parts[0] + parts[1]) + parts[2]) + parts[3]

    return pl.pallas_call(
        body, name="add_chips",
        grid_spec=pltpu.PrefetchScalarGridSpec(
            num_scalar_prefetch=1, grid=(r // t,),
            in_specs=[pl.BlockSpec((None, t, cols), lambda i, me_ref: (me_ref[0], i, 0)),
                      pl.BlockSpec((N_BLK, t, cols), lambda i, me_ref: (0, i, 0))],
            out_specs=pl.BlockSpec((t, cols), lambda i, me_ref: (i, 0))),
        out_shape=jax.ShapeDtypeStruct((r, cols), F32), compiler_params=_params(("parallel",)),
    )(me, s1, r2)


def _adamw(core, mine, other, w, m, v, layer, earlier=None):
    half, cols = mine.shape
    t = _row_tile(half)
    per_half = half // t
    c1 = 1.0 - ADAM_B1 ** ADAM_STEP
    c2 = 1.0 - ADAM_B2 ** ADAM_STEP

    def body(c_ref, a_ref, b_ref, w_ref, m_ref, v_ref, *rest):
        g_ref, d_ref, mo_ref, vo_ref = rest[-4:]
        gv = jnp.where(pl.program_id(0) // per_half == c_ref[0], a_ref[...], b_ref[...])
        g_ref[...] = gv
        m_new = ADAM_B1 * m_ref[...] + (1.0 - ADAM_B1) * gv
        v_new = ADAM_B2 * v_ref[...] + (1.0 - ADAM_B2) * (gv * gv)
        mo_ref[...] = m_new
        vo_ref[...] = v_new
        d_ref[...] = -ADAM_LR * ((m_new / c1) / (jnp.sqrt(v_new / c2) + ADAM_EPS) + ADAM_WD * w_ref[...])

    part = pl.BlockSpec((t, cols), lambda i, c_ref: (i % per_half, 0))
    spec = pl.BlockSpec((None, t, cols), lambda i, c_ref: (layer, i, 0))
    kept = [] if earlier is None else list(earlier)
    return pl.pallas_call(
        body, name="adamw",
        grid_spec=pltpu.PrefetchScalarGridSpec(
            num_scalar_prefetch=1, grid=(2 * per_half,),
            in_specs=[part, part, spec, spec, spec] + [_ANY] * len(kept), out_specs=[spec] * 4),
        out_shape=[jax.ShapeDtypeStruct(w.shape, F32)] * 4,
        input_output_aliases={6 + k: k for k in range(len(kept))},
        compiler_params=_params(("parallel",)),
    )(core, mine, other, w, m, v, *kept)


_REPLICATED = ("norm_mix_g", "gmlp_v_g", "gmlp_w_s", "gmlp_b_s", "conf_ln_g", "conf_ln_b", "mix_out_g", "norm_ffn_g",
               "final_norm_g")
_REP_SHAPES = {"norm_mix_g": (DEPTH, D_MODEL), "gmlp_v_g": (DEPTH, D_GROUP), "gmlp_w_s": (DEPTH, N_HEADS, CHUNK, CHUNK),
               "gmlp_b_s": (DEPTH, N_HEADS, CHUNK), "conf_ln_g": (DEPTH, D_GROUP), "conf_ln_b": (DEPTH, D_GROUP),
               "mix_out_g": (DEPTH, D_MODEL), "norm_ffn_g": (DEPTH, D_MODEL), "final_norm_g": (D_MODEL,)}
_BIG = ("w_in", "w_out", "w_up", "w_down")
_CONV_ROWS = 8
_REP_ROWS = 144
_SMALL_ROWS = 160
_CH_BLK = D_GROUP // N_BLK


def _pad_rows(flat, rows):
    pad = rows * D_MODEL - flat.shape[-1]
    flat = jnp.pad(flat, [(0, 0)] * (flat.ndim - 1) + [(0, pad)])
    return flat.reshape(flat.shape[:-1] + (rows, D_MODEL))


def _pack_small(scw, ccw, rep):
    lead = scw.shape[:-3]
    conv = jnp.concatenate([scw.reshape(lead + (-1,)), ccw.reshape(lead + (-1,))], axis=-1)
    flat = jnp.concatenate([rep[k].reshape(-1) for k in _REPLICATED])
    flat = jnp.broadcast_to(flat, lead + flat.shape)
    parts = [_pad_rows(conv, _CONV_ROWS), _pad_rows(flat, _REP_ROWS),
             jnp.zeros(lead + (_SMALL_ROWS - _CONV_ROWS - _REP_ROWS, D_MODEL), F32)]
    return jnp.concatenate(parts, axis=-2)


def _unpack_small(pk):
    out = {}
    conv = pk[:_CONV_ROWS].reshape(-1)
    n_s = DEPTH * SHORT_K * _CH_BLK
    out["short_conv_w"] = conv[:n_s].reshape(DEPTH, SHORT_K, _CH_BLK)
    out["conf_conv_w"] = conv[n_s:n_s + DEPTH * CONF_K * _CH_BLK].reshape(DEPTH, CONF_K, _CH_BLK)
    row = _CONV_ROWS
    flat = pk[row:row + _REP_ROWS].reshape(-1)
    at = 0
    for k in _REPLICATED:
        n = math.prod(_REP_SHAPES[k])
        out[k] = flat[at:at + n].reshape(_REP_SHAPES[k])
        at += n
    return out


def _conv_blocks(w):
    d, k, _ = w.shape
    return w.reshape(d, k, N_BLK, _CH_BLK).transpose(2, 0, 1, 3)


_WEIGHTS = ("norm_mix_g", "w_in", "gmlp_v_g", "gmlp_w_s", "gmlp_b_s", "short_conv_w", "conf_conv_w", "conf_ln_g",
            "conf_ln_b", "mix_out_g", "w_out", "norm_ffn_g", "w_up", "w_down", "final_norm_g")


def kernel(x, norm_mix_g, w_in, gmlp_v_g, gmlp_w_s, gmlp_b_s, short_conv_w, conf_conv_w, conf_ln_g, conf_ln_b, mix_out_g, w_out, norm_ffn_g, w_up, w_down, final_norm_g, loss_target, m_norm_mix_g, m_w_in, m_gmlp_v_g, m_gmlp_w_s, m_gmlp_b_s, m_short_conv_w, m_conf_conv_w, m_conf_ln_g, m_conf_ln_b, m_mix_out_g, m_w_out, m_norm_ffn_g, m_w_up, m_w_down, m_final_norm_g, v_norm_mix_g, v_w_in, v_gmlp_v_g, v_gmlp_w_s, v_gmlp_b_s, v_short_conv_w, v_conf_conv_w, v_conf_ln_g, v_conf_ln_b, v_mix_out_g, v_w_out, v_norm_ffn_g, v_w_up, v_w_down, v_final_norm_g):
    w = dict(norm_mix_g=norm_mix_g, w_in=w_in, gmlp_v_g=gmlp_v_g, gmlp_w_s=gmlp_w_s, gmlp_b_s=gmlp_b_s,
             short_conv_w=short_conv_w, conf_conv_w=conf_conv_w, conf_ln_g=conf_ln_g, conf_ln_b=conf_ln_b,
             mix_out_g=mix_out_g, w_out=w_out, norm_ffn_g=norm_ffn_g, w_up=w_up, w_down=w_down, final_norm_g=final_norm_g)
    m = dict(norm_mix_g=m_norm_mix_g, w_in=m_w_in, gmlp_v_g=m_gmlp_v_g, gmlp_w_s=m_gmlp_w_s, gmlp_b_s=m_gmlp_b_s,
             short_conv_w=m_short_conv_w, conf_conv_w=m_conf_conv_w, conf_ln_g=m_conf_ln_g, conf_ln_b=m_conf_ln_b,
             mix_out_g=m_mix_out_g, w_out=m_w_out, norm_ffn_g=m_norm_ffn_g, w_up=m_w_up, w_down=m_w_down,
             final_norm_g=m_final_norm_g)
    v = dict(norm_mix_g=v_norm_mix_g, w_in=v_w_in, gmlp_v_g=v_gmlp_v_g, gmlp_w_s=v_gmlp_w_s, gmlp_b_s=v_gmlp_b_s,
             short_conv_w=v_short_conv_w, conf_conv_w=v_conf_conv_w, conf_ln_g=v_conf_ln_g, conf_ln_b=v_conf_ln_b,
             mix_out_g=v_mix_out_g, w_out=v_w_out, norm_ffn_g=v_norm_ffn_g, w_up=v_w_up, w_down=v_w_down,
             final_norm_g=v_final_norm_g)
    core = lax.axis_index("c").astype(jnp.int32).reshape(1)
    me = (2 * lax.axis_index("x") + lax.axis_index("y")).astype(jnp.int32).reshape(1)

    conv_mine = _pad_rows(jnp.concatenate([short_conv_w.reshape(-1), conf_conv_w.reshape(-1)]), _CONV_ROWS)
    big = {k: _cast_into_slot(w[k], me, "cast_" + k) for k in _BIG}
    big["w_in"], conv_all = _gather_first([big["w_in"]], [(0, 0)], [conv_mine])
    conv_all = conv_all.reshape(N_BLK, -1)
    n_s = DEPTH * SHORT_K * _CH_BLK
    scw_all = conv_all[:, :n_s].reshape(N_BLK, DEPTH, SHORT_K, _CH_BLK)
    ccw_all = conv_all[:, n_s:n_s + DEPTH * CONF_K * _CH_BLK].reshape(N_BLK, DEPTH, CONF_K, _CH_BLK)
    small = {k: w[k] for k in _REPLICATED}
    small["short_conv_w"] = scw_all.transpose(1, 2, 0, 3).reshape(DEPTH, SHORT_K, D_GROUP)
    small["conf_conv_w"] = ccw_all.transpose(1, 2, 0, 3).reshape(DEPTH, CONF_K, D_GROUP)

    loss, dx, g, early = _local_step(x[0], loss_target[0], big, small, gather_pending=True, core=core)

    where = [(k, l) for k in _BIG for l in range(DEPTH)]
    late = [kl for kl in where if kl not in early]
    sums = _chip_sums(core, [g[k][l] for k, l in late]
                      + [_pack_small(_conv_blocks(g["short_conv_w"]), _conv_blocks(g["conf_conv_w"]), g)])
    sent = {**early, **dict(zip(late + ["small"], zip(sums, _send_to_owners(sums))))}
    mine = [_add_chips(me, *sent[kl]) for kl in where + ["small"]]
    other = _swap_reduced(mine)

    done = {}
    for n, (k, l) in enumerate(where):
        done[k] = _adamw(core, mine[n], other[n], w[k], m[k], v[k], l, done.get(k))
    small_own = [_pack_small(t["short_conv_w"], t["conf_conv_w"], t)[None] for t in (w, m, v)]
    small_done = [_unpack_small(a[0]) for a in _adamw(core, mine[-1], other[-1], *small_own, 0)]

    outs = [lax.psum(loss[0, 0], ("x", "y", "c")), dx[None]]
    for kind in range(4):
        outs += [done[k][kind] if k in _BIG else small_done[kind][k] for k in _WEIGHTS]
    return tuple(outs)
```

```python
import math

import jax
import jax.numpy as jnp
from jax import lax
from jax.experimental import pallas as pl
from jax.experimental.pallas import tpu as pltpu

F32 = jnp.float32
BF16 = jnp.bfloat16

D_MODEL = 1024
D_GROUP = 256
N_HEADS = 4
HEAD_DIM = 64
CHUNK = 128
D_IN = 2560
N_BLK = 4
W_IN_BLK = D_IN // N_BLK
D_FF_BLK = 1024
DEPTH = 2
EPS = 1e-6
HALO = 32
MIX_ROWS = 512
SHORT_K = 3
CONF_K = 31
ATT_TQ = 256
ATT_TK = 256
ATT_SCALE = 0.125
ATT_DEAD = -104.0
V7X_VMEM_LIMIT = 56 * 1024 * 1024

ADAM_LR, ADAM_B1, ADAM_B2, ADAM_EPS, ADAM_WD, ADAM_STEP = 0.001, 0.9, 0.999, 1e-08, 0.01, 10

MESH = pl.DeviceIdType.MESH


def _params(sem, vmem=None):
    return pltpu.CompilerParams(dimension_semantics=sem, vmem_limit_bytes=vmem)


def _tile(s, t):
    return min(s, t)


def _rsqrt_mean(v):
    return lax.rsqrt(jnp.mean(v * v, axis=-1, keepdims=True) + EPS)


def _sigmoid(v):
    return 1.0 / (1.0 + jnp.exp(-v))


_GELU_C = math.sqrt(2.0 / math.pi)


def _gelu_tanh(v):
    return jnp.tanh(_GELU_C * (v + 0.044715 * (v * v * v)))


def _gelu(v, t):
    return v * (0.5 * (1.0 + t))


def _gelu_grad(v, t):
    return 0.5 * (1.0 + t) + v * (0.5 * (1.0 - t * t) * _GELU_C * (1.0 + 3.0 * 0.044715 * (v * v)))


def _dot(a, b):
    return jnp.dot(a, b, preferred_element_type=F32)


def _dot_nt(a, b):
    return lax.dot_general(a, b, (((1,), (1,)), ((), ())), preferred_element_type=F32)


def _dot_tn(a, b):
    return lax.dot_general(a, b, (((0,), (0,)), ((), ())), preferred_element_type=F32)


def _cast_into_slot(w, me, name):
    n, r, c = w.shape
    tr = _tile(r, 256)

    def body(me_ref, w_ref, o_ref):
        o_ref[...] = w_ref[...].astype(BF16)

    return pl.pallas_call(
        body, name=name,
        grid_spec=pltpu.PrefetchScalarGridSpec(
            num_scalar_prefetch=1, grid=(n, r // tr),
            in_specs=[pl.BlockSpec((None, tr, c), lambda a, b, me_ref: (a, b, 0))],
            out_specs=pl.BlockSpec((None, None, tr, c), lambda a, b, me_ref: (me_ref[0], a, b, 0))),
        out_shape=jax.ShapeDtypeStruct((N_BLK,) + w.shape, BF16),
        compiler_params=_params(("parallel", "parallel")),
    )(me, w)


def _split_heads(xv, rows_ref, cols_ref):
    if rows_ref is not None:
        for h in range(N_HEADS):
            rows_ref[h] = xv[:, h * HEAD_DIM:(h + 1) * HEAD_DIM].astype(BF16)
    if cols_ref is not None:
        xt = xv.T
        for h in range(N_HEADS):
            cols_ref[h] = xt[h * HEAD_DIM:(h + 1) * HEAD_DIM, :].astype(BF16)


def _head_specs(t, s):
    rows = (pl.BlockSpec((N_HEADS, t, HEAD_DIM), lambda i: (0, i, 0)), jax.ShapeDtypeStruct((N_HEADS, s, HEAD_DIM), BF16))
    cols = (pl.BlockSpec((N_HEADS, HEAD_DIM, t), lambda i: (0, 0, i)), jax.ShapeDtypeStruct((N_HEADS, HEAD_DIM, s), BF16))
    return rows, cols


def _cols(ref, lo, hi):
    return ref[:, lo:hi].astype(F32)


def _pair_blocks(w_ref, wide_ref):
    @pl.when(pl.program_id(0) == 0)
    def _():
        for b in range(N_BLK):
            wide_ref[b // 2, :, (b % 2) * W_IN_BLK:(b % 2 + 1) * W_IN_BLK] = w_ref[b]


def _in_proj(x, g, w, layer):
    s = x.shape[0]
    t = _tile(s, 1024)

    def body(x_ref, g_ref, w_ref, z_ref, h_ref, qr_ref, qt_ref, kt_ref, vt_ref, wide_ref):
        _pair_blocks(w_ref, wide_ref)
        xv = x_ref[...]
        h = (xv * _rsqrt_mean(xv) * g_ref[...]).astype(BF16)
        h_ref[...] = h
        for n in range(N_BLK // 2):
            z_ref[:, 2 * n * W_IN_BLK:2 * (n + 1) * W_IN_BLK] = _dot(h, wide_ref[n]).astype(BF16)
        _split_heads(_cols(z_ref, 1280, 1536) * ATT_SCALE, qr_ref, qt_ref)
        _split_heads(_cols(z_ref, 1536, 1792), None, kt_ref)
        _split_heads(_cols(z_ref, 1792, 2048), None, vt_ref)

    rows, cols = _head_specs(t, s)
    return pl.pallas_call(
        body, name="in_proj", grid=(s // t,),
        in_specs=[pl.BlockSpec((t, D_MODEL), lambda i: (i, 0)), _full((1, D_MODEL)),
                  pl.BlockSpec((N_BLK, None, D_MODEL, W_IN_BLK), lambda i: (0, layer, 0, 0))],
        out_specs=[pl.BlockSpec((t, D_IN), lambda i: (i, 0)), pl.BlockSpec((t, D_MODEL), lambda i: (i, 0)),
                   rows[0], cols[0], cols[0], cols[0]],
        out_shape=[jax.ShapeDtypeStruct((s, D_IN), BF16), jax.ShapeDtypeStruct((s, D_MODEL), BF16),
                   rows[1], cols[1], cols[1], cols[1]],
        scratch_shapes=[pltpu.VMEM((N_BLK // 2, D_MODEL, 2 * W_IN_BLK), BF16)],
        compiler_params=_params(("arbitrary",), V7X_VMEM_LIMIT),
    )(x, g, w)


def _mix_a_fwd(z_ref, vg, wt_ref, bmat, t):
    zu = _cols(z_ref, 0, 256)
    zv = _cols(z_ref, 256, 512)
    tu = _gelu_tanh(zu)
    tv = _gelu_tanh(zv)
    u = _gelu(zu, tu)
    v = _gelu(zv, tv)
    rv = _rsqrt_mean(v)
    vh = v * rv
    vnb = (vh * vg).astype(BF16)
    head = lax.broadcasted_iota(jnp.int32, (CHUNK, D_GROUP), 1) // HEAD_DIM
    fs = []
    for c in range(t // CHUNK):
        vc = vnb[c * CHUNK:(c + 1) * CHUNK, :]
        fc = bmat
        for h in range(N_HEADS):
            fc = fc + jnp.where(head == h, _dot(wt_ref[h], vc), 0.0)
        fs.append(fc)
    f = jnp.concatenate(fs, axis=0) if len(fs) > 1 else fs[0]
    return (zu, tu), (zv, tv), u, rv, vh, vnb, f


def _windows(ext_ref, sh_ref, t):
    for b in range(1, 8):
        sh_ref[b - 1] = ext_ref[pl.ds(b, HALO + t - 8), :]

    def window(o):
        a, b = divmod(o, 8)
        return ext_ref[pl.ds(8 * a, t), :] if b == 0 else sh_ref[b - 1, pl.ds(8 * a, t), :]

    return window


def _mix_b_fwd(z_ref, zh_ref, first, scw_ref, ext_ref, t):
    gb = _cols(z_ref, 512, 768)
    uh = _cols(zh_ref, 768, 1024) * _cols(zh_ref, 1024, 1280)
    ext_ref[0:HALO, :] = jnp.where(first, 0.0, uh)
    ext_ref[HALO:HALO + t, :] = _cols(z_ref, 768, 1024) * _cols(z_ref, 1024, 1280)
    cv = jnp.zeros((t, D_GROUP), F32)
    for k in range(SHORT_K):
        cv = cv + scw_ref[k:k + 1, :] * ext_ref[pl.ds(HALO - (SHORT_K - 1) + k, t), :]
    return gb, cv


def _mix_d_fwd(z_ref, zh_ref, first, ccw_ref, lg, lb, ext_ref, sh_ref, t, cv=None):
    hh = _cols(zh_ref, 2048, 2304) * _sigmoid(_cols(zh_ref, 2304, 2560))
    ext_ref[0:HALO, :] = jnp.where(first, 0.0, hh)
    ext_ref[HALO:HALO + t, :] = _cols(z_ref, 2048, 2304) * _sigmoid(_cols(z_ref, 2304, 2560))
    window = _windows(ext_ref, sh_ref, t)
    if cv is None:
        cv = jnp.zeros((t, D_GROUP), F32)
        for k in range(CONF_K):
            cv = cv + ccw_ref[k:k + 1, :] * window(HALO - (CONF_K - 1) + k)
    xc = cv - jnp.mean(cv, axis=-1, keepdims=True)
    rs = lax.rsqrt(jnp.mean(xc * xc, axis=-1, keepdims=True) + EPS)
    xh = xc * rs
    ln = xh * lg + lb
    return xh, rs, ln, _sigmoid(ln), window, cv


def _mix_specs(t, s):
    per = t // HALO
    return [pl.BlockSpec((t, D_IN), lambda i: (i, 0)),
            pl.BlockSpec((HALO, D_IN), lambda i: (jnp.maximum(i * per - 1, 0), 0))]


def _full(shape):
    return pl.BlockSpec(shape, lambda i: (0,) * len(shape))


def _mixers_fwd(z, p, bufs=(), parts=()):
    s = z.shape[0]
    t = _tile(s, MIX_ROWS)
    nb = len(bufs)

    def body(z_ref, zh_ref, vg_ref, wt_ref, bm_ref, scw_ref, ccw_ref, lg_ref, lb_ref, *rest):
        y_ref = rest[nb]
        eb_ref, ed_ref, sh_ref = rest[2 * nb + 1:2 * nb + 4]
        if nb:
            _gather_in_steps(rest[nb + 1:2 * nb + 1], parts, rest[2 * nb + 4:], s // t)
        first = pl.program_id(0) == 0
        _, _, u, _, _, _, f = _mix_a_fwd(z_ref, vg_ref[...], wt_ref, bm_ref[...], t)
        ya = u * f
        y_ref[:, 0:256] = ya * _rsqrt_mean(ya)
        gb, cv = _mix_b_fwd(z_ref, zh_ref, first, scw_ref, eb_ref, t)
        yb = gb * cv
        y_ref[:, 256:512] = yb * _rsqrt_mean(yb)
        _, _, ln, sg, _, cvd = _mix_d_fwd(z_ref, zh_ref, first, ccw_ref, lg_ref[...], lb_ref[...], ed_ref, sh_ref, t)
        yd = ln * sg
        y_ref[:, 512:768] = yd * _rsqrt_mean(yd)
        y_ref[:, 768:1024] = cvd

    out = pl.pallas_call(
        body, name="mixers_fwd", grid=(s // t,),
        in_specs=_mix_specs(t, s) + [_full((1, D_GROUP)), _full((N_HEADS, CHUNK, CHUNK)), _full((CHUNK, D_GROUP)),
                                     _full((8, D_GROUP)), _full((32, D_GROUP)), _full((1, D_GROUP)), _full((1, D_GROUP))]
                 + [_ANY] * nb,
        out_specs=[pl.BlockSpec((t, D_MODEL), lambda i: (i, 0))] + [_ANY] * nb,
        out_shape=[jax.ShapeDtypeStruct((s, D_MODEL), F32)] + [jax.ShapeDtypeStruct(b.shape, b.dtype) for b in bufs],
        input_output_aliases={9 + k: 1 + k for k in range(nb)},
        scratch_shapes=[pltpu.VMEM((HALO + t, D_GROUP), F32), pltpu.VMEM((HALO + t, D_GROUP), F32),
                        pltpu.VMEM((7, HALO + t - 8, D_GROUP), F32)] + (_gather_sems(parts) if nb else []),
        compiler_params=_params(("arbitrary",) if nb else ("parallel",), V7X_VMEM_LIMIT),
    )(z, z, p["vg"], p["wt"], p["bmat"], p["scw"], p["ccw"], p["lg"], p["lb"], *bufs)
    return (out[0], out[1:]) if nb else out[0]


def _mixers_bwd_a(z, dyn, o_t, y_abd, p, sums=()):
    s = z.shape[0]
    t = _tile(s, MIX_ROWS)
    n_chunk = t // CHUNK
    ns = len(sums)

    def body(*refs):
        (z_ref, zh_ref, dyn_ref, ot_ref, cv_ref, vg_ref, wt_ref, wtt_ref, bm_ref, scw_ref, ccw_ref, lg_ref,
         lb_ref) = refs[:13]
        (dza_ref, dcb_ref, dcd_ref, dor_ref, dot_ref, ds_ref, dvg_ref, dws_ref, dbm_ref, dscw_ref, dccw_ref, dlg_ref,
         dlb_ref) = refs[13 + ns:26 + ns]
        eb_ref, ed_ref, sh_ref = refs[26 + 2 * ns:29 + 2 * ns]
        i = pl.program_id(0)
        first = i == 0
        if ns:
            _owners_in_steps(refs[13:13 + ns], refs[26 + ns:26 + 2 * ns], refs[29 + 2 * ns:], first, i == s // t - 1)

        @pl.when(first)
        def _():
            for r in (dvg_ref, dws_ref, dbm_ref, dscw_ref, dccw_ref, dlg_ref, dlb_ref):
                r[...] = jnp.zeros_like(r)

        def rms_bwd(y, dn):
            r = _rsqrt_mean(y)
            yn = y * r
            return r * (dn - yn * jnp.mean(dn * yn, axis=-1, keepdims=True))

        vg = vg_ref[...]
        gelu_u, gelu_v, u, rv, vh, vnb, f = _mix_a_fwd(z_ref, vg, wt_ref, bm_ref[...], t)
        dya = rms_bwd(u * f, _cols(dyn_ref, 0, 256))
        du = dya * f
        df = dya * u
        head = lax.broadcasted_iota(jnp.int32, (CHUNK, D_GROUP), 1) // HEAD_DIM
        dvns = []
        dbm = jnp.zeros((CHUNK, D_GROUP), F32)
        for c in range(n_chunk):
            dfc = df[c * CHUNK:(c + 1) * CHUNK, :]
            vc = vnb[c * CHUNK:(c + 1) * CHUNK, :]
            dbm = dbm + dfc
            dvn = jnp.zeros((CHUNK, D_GROUP), F32)
            for h in range(N_HEADS):
                dfh = jnp.where(head == h, dfc, 0.0).astype(BF16)
                dvn = dvn + _dot(wtt_ref[h], dfh)
                dws_ref[h] += _dot_nt(dfh, vc)
            dvns.append(dvn)
        dbm_ref[...] += dbm
        dvn = jnp.concatenate(dvns, axis=0) if n_chunk > 1 else dvns[0]
        dvg_ref[...] += jnp.sum(dvn * vh, axis=0, keepdims=True)
        dvh = dvn * vg
        dv = rv * (dvh - vh * jnp.mean(dvh * vh, axis=-1, keepdims=True))
        dza_ref[:, 0:256] = (du * _gelu_grad(*gelu_u)).astype(BF16)
        dza_ref[:, 256:512] = (dv * _gelu_grad(*gelu_v)).astype(BF16)

        gb, cv = _mix_b_fwd(z_ref, zh_ref, first, scw_ref, eb_ref, t)
        dyb = rms_bwd(gb * cv, _cols(dyn_ref, 256, 512))
        dza_ref[:, 512:768] = (dyb * cv).astype(BF16)
        dcb = dyb * gb
        dcb_ref[...] = dcb
        for k in range(SHORT_K):
            dscw_ref[k:k + 1, :] += jnp.sum(dcb * eb_ref[pl.ds(HALO - (SHORT_K - 1) + k, t), :], axis=0, keepdims=True)

        lg = lg_ref[...]
        xh, rs, ln, sg, window, _ = _mix_d_fwd(z_ref, zh_ref, first, ccw_ref, lg, lb_ref[...], ed_ref, sh_ref, t,
                                               cv_ref[...])
        dyd = rms_bwd(ln * sg, _cols(dyn_ref, 768, 1024))
        dln = dyd * (sg * (1.0 + ln * (1.0 - sg)))
        dlg_ref[...] += jnp.sum(dln * xh, axis=0, keepdims=True)
        dlb_ref[...] += jnp.sum(dln, axis=0, keepdims=True)
        dxh = dln * lg
        dcd = rs * (dxh - jnp.mean(dxh, axis=-1, keepdims=True) - xh * jnp.mean(dxh * xh, axis=-1, keepdims=True))
        dcd_ref[...] = dcd
        for k in range(CONF_K):
            dccw_ref[k:k + 1, :] += jnp.sum(dcd * window(HALO - (CONF_K - 1) + k), axis=0, keepdims=True)

        o = ot_ref[...].reshape(D_GROUP, t).T
        do = rms_bwd(o, _cols(dyn_ref, 512, 768))
        _split_heads(do, dor_ref, dot_ref)
        prod = do.astype(BF16).astype(F32) * o
        for h in range(N_HEADS):
            ds_ref[h] = jnp.sum(prod[:, h * HEAD_DIM:(h + 1) * HEAD_DIM], axis=1, keepdims=True)

    small = [(1, D_GROUP), (N_HEADS, CHUNK, CHUNK), (CHUNK, D_GROUP), (8, D_GROUP), (32, D_GROUP), (1, D_GROUP), (1, D_GROUP)]
    rows, cols = _head_specs(t, s)
    out = pl.pallas_call(
        body, name="mixers_bwd_a", grid=(s // t,),
        in_specs=_mix_specs(t, s) + [pl.BlockSpec((t, D_MODEL), lambda i: (i, 0)),
                                     pl.BlockSpec((N_HEADS, HEAD_DIM, t), lambda i: (0, 0, i)),
                                     pl.BlockSpec((t, D_GROUP), lambda i: (i, 3)),
                                     _full((1, D_GROUP)), _full((N_HEADS, CHUNK, CHUNK)), _full((N_HEADS, CHUNK, CHUNK)),
                                     _full((CHUNK, D_GROUP)), _full((8, D_GROUP)), _full((32, D_GROUP)),
                                     _full((1, D_GROUP)), _full((1, D_GROUP))] + [_ANY] * ns,
        out_specs=[pl.BlockSpec((t, 768), lambda i: (i, 0)), pl.BlockSpec((t, D_GROUP), lambda i: (i, 0)),
                   pl.BlockSpec((t, D_GROUP), lambda i: (i, 0)), rows[0], cols[0],
                   pl.BlockSpec((N_HEADS, t, 1), lambda i: (0, i, 0))]
                  + [_full(sh) for sh in small] + [_ANY] * ns,
        out_shape=[jax.ShapeDtypeStruct((s, 768), BF16), jax.ShapeDtypeStruct((s, D_GROUP), F32),
                   jax.ShapeDtypeStruct((s, D_GROUP), F32), rows[1], cols[1],
                   jax.ShapeDtypeStruct((N_HEADS, s, 1), F32)]
                  + [jax.ShapeDtypeStruct(sh, F32) for sh in small]
                  + [jax.ShapeDtypeStruct(a.shape, a.dtype) for a in sums],
        scratch_shapes=[pltpu.VMEM((HALO + t, D_GROUP), F32), pltpu.VMEM((HALO + t, D_GROUP), F32),
                        pltpu.VMEM((7, HALO + t - 8, D_GROUP), F32)] + (_owner_sems(ns) if ns else []),
        compiler_params=_params(("arbitrary",), V7X_VMEM_LIMIT),
    )(z, z, dyn, o_t, y_abd, p["vg"], p["wt"], p["wtt"], p["bmat"], p["scw"], p["ccw"], p["lg"], p["lb"], *sums)
    return tuple(out[:13]) + (out[13:],) if ns else out


def _mixers_bwd_b(z, dza, dcb, dcd, dq_t, dk_t, dv_t, p):
    s = z.shape[0]
    t = _tile(s, MIX_ROWS)
    per = t // HALO
    n_halo = s // HALO

    def body(z_ref, dza_ref, dcb_ref, dcbn_ref, dcd_ref, dcdn_ref, dq_ref, dk_ref, dv_ref, scw_ref, ccw_ref,
             dz_ref, eb_ref, ed_ref, sh_ref):
        last = pl.program_id(0) == pl.num_programs(0) - 1
        dz_ref[:, 0:768] = dza_ref[...]
        eb_ref[0:t, :] = dcb_ref[...]
        eb_ref[t:t + HALO, :] = jnp.where(last, 0.0, dcbn_ref[...])
        du = jnp.zeros((t, D_GROUP), F32)
        for k in range(SHORT_K):
            du = du + scw_ref[k:k + 1, :] * eb_ref[pl.ds(SHORT_K - 1 - k, t), :]
        dz_ref[:, 768:1024] = (du * _cols(z_ref, 1024, 1280)).astype(BF16)
        dz_ref[:, 1024:1280] = (du * _cols(z_ref, 768, 1024)).astype(BF16)
        for n, r in enumerate((dq_ref, dk_ref, dv_ref)):
            dz_ref[:, 1280 + 256 * n:1536 + 256 * n] = r[...].reshape(D_GROUP, t).T.astype(BF16)
        ed_ref[0:t, :] = dcd_ref[...]
        ed_ref[t:t + HALO, :] = jnp.where(last, 0.0, dcdn_ref[...])
        window = _windows(ed_ref, sh_ref, t)
        dh = jnp.zeros((t, D_GROUP), F32)
        for k in range(CONF_K):
            dh = dh + ccw_ref[k:k + 1, :] * window(CONF_K - 1 - k)
        a = _cols(z_ref, 2048, 2304)
        sg = _sigmoid(_cols(z_ref, 2304, 2560))
        dz_ref[:, 2048:2304] = (dh * sg).astype(BF16)
        dz_ref[:, 2304:2560] = (dh * a * sg * (1.0 - sg)).astype(BF16)

    nxt = lambda i: (jnp.minimum((i + 1) * per, n_halo - 1), 0)
    tr = pl.BlockSpec((N_HEADS, HEAD_DIM, t), lambda i: (0, 0, i))
    return pl.pallas_call(
        body, name="mixers_bwd_b", grid=(s // t,),
        in_specs=[pl.BlockSpec((t, D_IN), lambda i: (i, 0)), pl.BlockSpec((t, 768), lambda i: (i, 0)),
                  pl.BlockSpec((t, D_GROUP), lambda i: (i, 0)), pl.BlockSpec((HALO, D_GROUP), nxt),
                  pl.BlockSpec((t, D_GROUP), lambda i: (i, 0)), pl.BlockSpec((HALO, D_GROUP), nxt),
                  tr, tr, tr, _full((8, D_GROUP)), _full((32, D_GROUP))],
        out_specs=pl.BlockSpec((t, D_IN), lambda i: (i, 0)),
        out_shape=jax.ShapeDtypeStruct((s, D_IN), BF16),
        scratch_shapes=[pltpu.VMEM((HALO + t, D_GROUP), F32), pltpu.VMEM((HALO + t, D_GROUP), F32),
                        pltpu.VMEM((7, HALO + t - 8, D_GROUP), F32)],
        compiler_params=_params(("parallel",), V7X_VMEM_LIMIT),
    )(z, dza, dcb, dcb, dcd, dcd, dq_t, dk_t, dv_t, p["scw"], p["ccw"])


def _split_bf16(v):
    hi = v.astype(BF16)
    return hi, (v - hi.astype(F32)).astype(BF16)


def _att_scores(qs, kts, carries, tri, mask):
    zs = [_dot(q, kt) for q, kt in zip(qs, kts)]
    lms, lbs, parts = [], [], []
    for z in zs:
        soft = jnp.log(1.0 + jnp.exp(-jnp.abs(z)))
        lm = -(jnp.maximum(z, 0.0) + soft)
        lbs.append(lm + z)
        if mask is not None:
            lm = jnp.where(mask, lm, 0.0)
        lms.append(lm)
        parts.append(_split_bf16(lm))
    rights = [_dot(hi, tri) + _dot(lo, tri) for hi, lo in parts]
    ws = []
    for lb, right, carry in zip(lbs, rights, carries):
        w = jnp.exp(lb + right + carry)
        ws.append(w if mask is None else jnp.where(mask, w, 0.0))
    return ws, lbs, [jnp.sum(lm, axis=1, keepdims=True) for lm in lms]


def _att_consts(i):
    j_hi = ((i + 1) * ATT_TQ - 1) // ATT_TK
    row = lax.broadcasted_iota(jnp.int32, (ATT_TQ, ATT_TK), 0) + i * ATT_TQ
    col = lax.broadcasted_iota(jnp.int32, (ATT_TQ, ATT_TK), 1) + j_hi * ATT_TK
    r_i = lax.broadcasted_iota(jnp.int32, (ATT_TK, ATT_TK), 0)
    c_i = lax.broadcasted_iota(jnp.int32, (ATT_TK, ATT_TK), 1)
    return j_hi, col < row, r_i, c_i


def _att_alive(j, carries):
    top = carries[0]
    for c in carries[1:]:
        top = jnp.maximum(top, c)
    return jnp.logical_and(j >= 0, jnp.max(top) > ATT_DEAD)


def _attn_fwd(q_r, k_t, v_t, bufs=(), parts=()):
    assert ATT_TQ == ATT_TK
    s = q_r.shape[1]
    nb = len(bufs)
    nbq = min(ATT_FWD_BLOCKS, s // ATT_TQ)
    step_rows = nbq * ATT_TQ
    lanes = [(h, b) for h in range(N_HEADS) for b in range(nbq)]

    def body(q_ref, kt_ref, vt_ref, *rest):
        o_ref = rest[nb]
        if nb:
            _gather_in_steps(rest[nb + 1:2 * nb + 1], parts, rest[2 * nb + 1:], s // step_rows)
        i = pl.program_id(0)
        _, mask, r_i, c_i = _att_consts(0)
        tri = (r_i > c_i).astype(BF16)
        rows_of = [pl.ds(b * ATT_TQ, ATT_TQ) for _, b in lanes]

        def tiles(t, carries, accs, mask):
            js = [nbq * i + b - t for _, b in lanes]
            live = [None if mask is not None or b == nbq - 1 else j >= 0 for (_, b), j in zip(lanes, js)]
            cols = [pl.ds(pl.multiple_of(jnp.maximum(j, 0) * ATT_TK, ATT_TK), ATT_TK) for j in js]
            ws, _, tots = _att_scores([q_ref[h, r, :] for (h, _), r in zip(lanes, rows_of)],
                                      [kt_ref[h, :, c] for (h, _), c in zip(lanes, cols)], carries, tri, mask)
            ws = [w if ok is None else jnp.where(ok, w, 0.0) for w, ok in zip(ws, live)]
            tots = [tot if ok is None else jnp.where(ok, tot, -1e30) for tot, ok in zip(tots, live)]
            accs = [acc + _dot_nt(vt_ref[h, :, c], w.astype(BF16)) for (h, _), c, acc, w in zip(lanes, cols, accs, ws)]
            return [c + tot for c, tot in zip(carries, tots)], accs

        state = tiles(0, [jnp.zeros((ATT_TQ, 1), F32)] * len(lanes), [jnp.zeros((HEAD_DIM, ATT_TQ), F32)] * len(lanes), mask)

        def cond(c):
            return _att_alive(nbq * i + nbq - 1 - c[0], c[1])

        def step(c):
            return (c[0] + 1,) + tuple(tiles(c[0], c[1], c[2], None))

        _, _, accs = lax.while_loop(cond, step, (1,) + tuple(state))
        for n, (h, _) in enumerate(lanes):
            o_ref[h, :, rows_of[n]] = accs[n]

    whole = pl.BlockSpec((N_HEADS, HEAD_DIM, s), lambda i: (0, 0, 0), pipeline_mode=pl.Buffered(1))
    out = pl.pallas_call(
        body, name="attn_fwd", grid=(s // step_rows,),
        in_specs=[pl.BlockSpec((N_HEADS, step_rows, HEAD_DIM), lambda i: (0, i, 0)), whole, whole] + [_ANY] * nb,
        out_specs=[pl.BlockSpec((N_HEADS, HEAD_DIM, step_rows), lambda i: (0, 0, i))] + [_ANY] * nb,
        out_shape=[jax.ShapeDtypeStruct((N_HEADS, HEAD_DIM, s), F32)] + [jax.ShapeDtypeStruct(b.shape, b.dtype) for b in bufs],
        input_output_aliases={3 + k: 1 + k for k in range(nb)},
        scratch_shapes=_gather_sems(parts) if nb else [],
        compiler_params=_params(("arbitrary",), V7X_VMEM_LIMIT),
    )(q_r, k_t, v_t, *bufs)
    return (out[0], out[1:]) if nb else out[0]


ATT_FWD_BLOCKS = 2
ATT_BWD_HEADS = 2
ATT_BWD_BLOCKS = 2


def _attn_bwd(q_r, q_t, k_t, v_t, do_r, do_t, dsum):
    assert ATT_TQ == ATT_TK
    s = q_r.shape[1]
    hps = ATT_BWD_HEADS
    nbq = min(ATT_BWD_BLOCKS, s // ATT_TQ)
    step_rows = nbq * ATT_TQ
    lanes = [(h, b) for h in range(hps) for b in range(nbq)]

    def body(q_ref, qt_ref, kt_ref, vt_ref, do_ref, dot_ref, ds_ref, dq_ref, dk_ref, dv_ref):
        i = pl.program_id(1)

        @pl.when(i == 0)
        def _():
            dk_ref[...] = jnp.zeros_like(dk_ref)
            dv_ref[...] = jnp.zeros_like(dv_ref)

        _, mask, r_i, c_i = _att_consts(0)
        tri_r = (r_i > c_i).astype(BF16)
        tri_ge = (r_i >= c_i).astype(BF16)
        rows_of = [pl.ds(b * ATT_TQ, ATT_TQ) for _, b in lanes]

        def tiles(t, carries, gsums, accs, mask):
            js = [nbq * i + b - t for _, b in lanes]
            live = [None if mask is not None or b == nbq - 1 else j >= 0 for (_, b), j in zip(lanes, js)]
            cols = [pl.ds(pl.multiple_of(jnp.maximum(j, 0) * ATT_TK, ATT_TK), ATT_TK) for j in js]
            kts = [kt_ref[h, :, c] for (h, _), c in zip(lanes, cols)]
            das = [_dot(do_ref[h, r, :], vt_ref[h, :, c]) for (h, _), r, c in zip(lanes, rows_of, cols)]
            ws, lbs, tots = _att_scores([q_ref[h, r, :] for (h, _), r in zip(lanes, rows_of)], kts, carries, tri_r, mask)
            ws = [w if ok is None else jnp.where(ok, w, 0.0) for w, ok in zip(ws, live)]
            tots = [tot if ok is None else jnp.where(ok, tot, -1e30) for tot, ok in zip(tots, live)]
            wbs = [w.astype(BF16) for w in ws]
            gs = [wb.astype(F32) * da for wb, da in zip(wbs, das)]
            parts = [_split_bf16(g) for g in gs]
            sfx = [_dot(hi, tri_ge) + _dot(lo, tri_ge) for hi, lo in parts]
            for n, (h, _) in enumerate(lanes):
                dv_ref[h, :, cols[n]] += _dot(dot_ref[h, :, rows_of[n]], wbs[n])
            dzs = []
            for n, (h, _) in enumerate(lanes):
                left = ds_ref[h, rows_of[n], :] - gsums[n] - sfx[n]
                dz = gs[n] - jnp.exp(lbs[n]) * (gs[n] + left)
                if mask is not None:
                    dz = jnp.where(mask, dz, 0.0)
                elif live[n] is not None:
                    dz = jnp.where(live[n], dz, 0.0)
                dzs.append(dz.astype(BF16))
            for n, (h, _) in enumerate(lanes):
                dk_ref[h, :, cols[n]] += _dot(qt_ref[h, :, rows_of[n]], dzs[n])
            accs = [acc + _dot_nt(kt, dz) for acc, kt, dz in zip(accs, kts, dzs)]
            gsums = [gsum + jnp.sum(g, axis=1, keepdims=True) for gsum, g in zip(gsums, gs)]
            return [c + tot for c, tot in zip(carries, tots)], gsums, accs

        col0 = [jnp.zeros((ATT_TQ, 1), F32)] * len(lanes)
        state = tiles(0, col0, col0, [jnp.zeros((HEAD_DIM, ATT_TQ), F32)] * len(lanes), mask)

        def cond(c):
            return _att_alive(nbq * i + nbq - 1 - c[0], c[1])

        def step(c):
            return (c[0] + 1,) + tuple(tiles(c[0], c[1], c[2], c[3], None))

        _, _, _, accs = lax.while_loop(cond, step, (1,) + tuple(state))
        for n, (h, _) in enumerate(lanes):
            dq_ref[h, :, rows_of[n]] = accs[n] * ATT_SCALE

    whole = pl.BlockSpec((hps, HEAD_DIM, s), lambda g, i: (g, 0, 0))
    whole_in = pl.BlockSpec((hps, HEAD_DIM, s), lambda g, i: (g, 0, 0), pipeline_mode=pl.Buffered(1))
    rows = pl.BlockSpec((hps, step_rows, HEAD_DIM), lambda g, i: (g, i, 0))
    cols = pl.BlockSpec((hps, HEAD_DIM, step_rows), lambda g, i: (g, 0, i))
    shape = jax.ShapeDtypeStruct((N_HEADS, HEAD_DIM, s), F32)
    return pl.pallas_call(
        body, name="attn_bwd", grid=(N_HEADS // hps, s // step_rows),
        in_specs=[rows, cols, whole_in, whole_in, rows, cols, pl.BlockSpec((hps, step_rows, 1), lambda g, i: (g, i, 0))],
        out_specs=[cols, whole, whole],
        out_shape=[shape, shape, shape],
        compiler_params=_params(("parallel", "arbitrary"), V7X_VMEM_LIMIT),
    )(q_r, q_t, k_t, v_t, do_r, do_t, dsum)


def _out_proj(x, y_abd, o_t, gain, w, layer):
    s = x.shape[0]
    t = _tile(s, 1024)

    def body(x_ref, y_ref, ot_ref, g_ref, w_ref, x1_ref, yn_ref):
        o = ot_ref[...].reshape(D_GROUP, t).T
        groups = [y_ref[:, 0:256], y_ref[:, 256:512], o * _rsqrt_mean(o), y_ref[:, 512:768]]
        g = g_ref[...]
        acc = None
        for b, yn in enumerate(groups):
            cols = slice(256 * b, 256 * (b + 1))
            yn_ref[:, cols] = yn.astype(BF16)
            part = _dot((yn * g[:, cols]).astype(BF16), w_ref[b])
            acc = part if acc is None else acc + part
        x1_ref[...] = x_ref[...] + acc

    return pl.pallas_call(
        body, name="out_proj", grid=(s // t,),
        in_specs=[pl.BlockSpec((t, D_MODEL), lambda i: (i, 0)), pl.BlockSpec((t, 768), lambda i: (i, 0)),
                  pl.BlockSpec((N_HEADS, HEAD_DIM, t), lambda i: (0, 0, i)), _full((1, D_MODEL)),
                  pl.BlockSpec((N_BLK, None, D_GROUP, D_MODEL), lambda i: (0, layer, 0, 0))],
        out_specs=[pl.BlockSpec((t, D_MODEL), lambda i: (i, 0)), pl.BlockSpec((t, D_MODEL), lambda i: (i, 0))],
        out_shape=[jax.ShapeDtypeStruct((s, D_MODEL), F32), jax.ShapeDtypeStruct((s, D_MODEL), BF16)],
        compiler_params=_params(("parallel",), V7X_VMEM_LIMIT),
    )(x, y_abd, o_t, gain, w)


def _out_proj_bwd(x1, g_ffn, dh, dx2, yn, gain, w, layer, halves=()):
    s = dx2.shape[0]
    t = _tile(s, 512)
    nh = len(halves)

    def body(x_ref, gf_ref, dh_ref, dx2_ref, yn_ref, g_ref, w_ref, *rest):
        dx1_ref, dgf_ref, dyn_ref, dg_ref, dw_ref = rest[nh:nh + 5]
        if nh:
            start, finish = _halves_stages(rest[:nh], rest[nh + 5:2 * nh + 5], rest[2 * nh + 5:])
            pl.when(pl.program_id(0) == 0)(start)
            pl.when(pl.program_id(0) == s // t - 1)(finish)

        @pl.when(pl.program_id(0) == 0)
        def _():
            dg_ref[...] = jnp.zeros_like(dg_ref)
            dw_ref[...] = jnp.zeros_like(dw_ref)
            dgf_ref[...] = jnp.zeros_like(dgf_ref)

        dx1, dgf = _rms_bwd_rows(x_ref[...], gf_ref[...], dh_ref[...], dx2_ref[...])
        dx1_ref[...] = dx1
        dgf_ref[...] += dgf
        dxb = dx1.astype(BF16)
        g = g_ref[...]
        yn = yn_ref[...].astype(F32)
        yg = (yn * g).astype(BF16)
        for b in range(N_BLK):
            cols = slice(256 * b, 256 * (b + 1))
            dyg = _dot_nt(dxb, w_ref[b])
            dw_ref[b] += _dot_tn(yg[:, cols], dxb)
            dg_ref[:, cols] += jnp.sum(dyg * yn[:, cols], axis=0, keepdims=True)
            dyn_ref[:, cols] = (dyg * g[:, cols]).astype(BF16)

    row = pl.BlockSpec((t, D_MODEL), lambda i: (i, 0))
    vec = _full((1, D_MODEL))
    out = pl.pallas_call(
        body, name="out_proj_bwd", grid=(s // t,),
        in_specs=[row, vec, row, row, row, vec, pl.BlockSpec((N_BLK, None, D_GROUP, D_MODEL), lambda i: (0, layer, 0, 0))]
                 + [_ANY] * nh,
        out_specs=[row, vec, row, vec, _full((N_BLK, D_GROUP, D_MODEL))] + [_ANY] * nh,
        out_shape=[jax.ShapeDtypeStruct((s, D_MODEL), F32), jax.ShapeDtypeStruct((1, D_MODEL), F32),
                   jax.ShapeDtypeStruct((s, D_MODEL), BF16), jax.ShapeDtypeStruct((1, D_MODEL), F32),
                   jax.ShapeDtypeStruct((N_BLK, D_GROUP, D_MODEL), F32)] + _halves_shapes(halves),
        scratch_shapes=_halves_sems(nh) if nh else [],
        compiler_params=_params(("arbitrary",), V7X_VMEM_LIMIT),
    )(x1, g_ffn, dh, dx2, yn, gain, w, *halves)
    return tuple(out[:5]) + (out[5:],) if nh else out


def _ffn(x, g, w_up, w_down, layer):
    s = x.shape[0]
    t = _tile(s, 1024)

    def body(x_ref, g_ref, wu_ref, wd_ref, x2_ref, p_ref, h_ref):
        @pl.when(pl.program_id(1) == 0)
        def _():
            xv = x_ref[...]
            h_ref[...] = (xv * _rsqrt_mean(xv) * g_ref[...]).astype(BF16)
            x2_ref[...] = xv

        pre = _dot(h_ref[...], wu_ref[...])
        p_ref[...] = pre.astype(BF16)
        a = jnp.maximum(pre, 0.0)
        x2_ref[...] += _dot((a * a).astype(BF16), wd_ref[...])

    wspec = pl.BlockSpec((None, None, D_MODEL, D_FF_BLK), lambda i, j: (j, layer, 0, 0))
    row = pl.BlockSpec((t, D_MODEL), lambda i, j: (i, 0))
    return pl.pallas_call(
        body, name="ffn", grid=(s // t, N_BLK),
        in_specs=[row, pl.BlockSpec((1, D_MODEL), lambda i, j: (0, 0)), wspec, wspec],
        out_specs=[row, pl.BlockSpec((t, D_FF_BLK), lambda i, j: (i, j)), row],
        out_shape=[jax.ShapeDtypeStruct((s, D_MODEL), F32), jax.ShapeDtypeStruct((s, N_BLK * D_FF_BLK), BF16),
                   jax.ShapeDtypeStruct((s, D_MODEL), BF16)],
        compiler_params=_params(("parallel", "arbitrary"), V7X_VMEM_LIMIT),
    )(x, g, w_up, w_down)


def _loss_head(x, g, target):
    s = x.shape[0]
    t = _tile(s, 512)

    def body(x_ref, g_ref, t_ref, l_ref, dx_ref, dxb_ref, dg_ref):
        @pl.when(pl.program_id(0) == 0)
        def _():
            l_ref[...] = jnp.zeros_like(l_ref)
            dg_ref[...] = jnp.zeros_like(dg_ref)

        xv = x_ref[...]
        g = g_ref[...]
        r = _rsqrt_mean(xv)
        xh = xv * r
        err = xh * g - t_ref[...]
        l_ref[...] += 0.5 * jnp.sum(jnp.mean(err * err, axis=-1, keepdims=True), axis=0, keepdims=True)
        dy = err * (1.0 / D_MODEL)
        dg_ref[...] += jnp.sum(dy * xh, axis=0, keepdims=True)
        dxh = dy * g
        dx = r * (dxh - xh * jnp.mean(dxh * xh, axis=-1, keepdims=True))
        dx_ref[...] = dx
        dxb_ref[...] = dx.astype(BF16)

    row = pl.BlockSpec((t, D_MODEL), lambda i: (i, 0))
    return pl.pallas_call(
        body, name="loss_head", grid=(s // t,),
        in_specs=[row, _full((1, D_MODEL)), row],
        out_specs=[_full((1, 128)), row, row, _full((1, D_MODEL))],
        out_shape=[jax.ShapeDtypeStruct((1, 128), F32), jax.ShapeDtypeStruct((s, D_MODEL), F32),
                   jax.ShapeDtypeStruct((s, D_MODEL), BF16), jax.ShapeDtypeStruct((1, D_MODEL), F32)],
        compiler_params=_params(("arbitrary",)),
    )(x, g, target)


def _rms_bwd_rows(xv, g, dh, dres):
    r = _rsqrt_mean(xv)
    xh = xv * r
    dxh = dh * g
    dx = dres + r * (dxh - xh * jnp.mean(dxh * xh, axis=-1, keepdims=True))
    return dx, jnp.sum(dh * xh, axis=0, keepdims=True)


def _ffn_bwd(dxb, p, w_up, w_down, layer, sums=()):
    s = dxb.shape[0]
    t = _tile(s, 1024)
    ns = len(sums)

    def body(dx_ref, p_ref, wu_ref, wd_ref, *rest):
        dp_ref, dh_ref = rest[ns:ns + 2]
        if ns:
            i, j = pl.program_id(0), pl.program_id(1)
            _owners_in_steps(rest[:ns], rest[ns + 2:2 * ns + 2], rest[2 * ns + 2:],
                             jnp.logical_and(i == 0, j == 0), jnp.logical_and(i == s // t - 1, j == N_BLK - 1))
        da = _dot_nt(dx_ref[...], wd_ref[...])
        a = jnp.maximum(p_ref[...].astype(F32), 0.0)
        dp = (da * (2.0 * a)).astype(BF16)
        dp_ref[...] = dp
        dh = _dot_nt(dp, wu_ref[...])

        @pl.when(pl.program_id(1) == 0)
        def _():
            dh_ref[...] = dh

        @pl.when(pl.program_id(1) != 0)
        def _():
            dh_ref[...] += dh

    wspec = pl.BlockSpec((None, None, D_MODEL, D_FF_BLK), lambda i, j: (j, layer, 0, 0))
    row = pl.BlockSpec((t, D_MODEL), lambda i, j: (i, 0))
    blk = pl.BlockSpec((t, D_FF_BLK), lambda i, j: (i, j))
    out = pl.pallas_call(
        body, name="ffn_bwd", grid=(s // t, N_BLK),
        in_specs=[row, blk, wspec, wspec] + [_ANY] * ns, out_specs=[blk, row] + [_ANY] * ns,
        out_shape=[jax.ShapeDtypeStruct((s, N_BLK * D_FF_BLK), BF16), jax.ShapeDtypeStruct((s, D_MODEL), F32)]
                  + [jax.ShapeDtypeStruct(a.shape, a.dtype) for a in sums],
        scratch_shapes=_owner_sems(ns) if ns else [],
        compiler_params=_params(("arbitrary" if ns else "parallel", "arbitrary"), V7X_VMEM_LIMIT),
    )(dxb, p, w_up, w_down, *sums)
    return (out[0], out[1], out[2:]) if ns else out


def _ffn_wgrad(hb, p, dp, dxb):
    s = hb.shape[0]
    t = _tile(s, 2048)

    def body(h_ref, p_ref, dp_ref, dx_ref, du_ref, dd_ref):
        @pl.when(pl.program_id(1) == 0)
        def _():
            du_ref[...] = jnp.zeros_like(du_ref)
            dd_ref[...] = jnp.zeros_like(dd_ref)

        a = jnp.maximum(p_ref[...].astype(F32), 0.0)
        du_ref[...] += _dot_tn(h_ref[...], dp_ref[...])
        dd_ref[...] += _dot_tn((a * a).astype(BF16), dx_ref[...])

    row = pl.BlockSpec((t, D_MODEL), lambda j, i: (i, 0))
    blk = pl.BlockSpec((t, D_FF_BLK), lambda j, i: (i, j))
    out = pl.BlockSpec((None, D_MODEL, D_FF_BLK), lambda j, i: (j, 0, 0))
    shape = jax.ShapeDtypeStruct((N_BLK, D_MODEL, D_FF_BLK), F32)
    return pl.pallas_call(
        body, name="ffn_wgrad", grid=(N_BLK, s // t),
        in_specs=[row, blk, blk, row], out_specs=[out, out], out_shape=[shape, shape],
        compiler_params=_params(("parallel", "arbitrary"), V7X_VMEM_LIMIT),
    )(hb, p, dp, dxb)


def _in_proj_bwd(x, g, dx1, dz, w, layer):
    s = x.shape[0]
    t = _tile(s, 512)

    def body(x_ref, g_ref, dx1_ref, dz_ref, w_ref, dx0_ref, dxb_ref, dg_ref, wide_ref):
        _pair_blocks(w_ref, wide_ref)

        @pl.when(pl.program_id(0) == 0)
        def _():
            dg_ref[...] = jnp.zeros_like(dg_ref)

        dh = _dot_nt(dz_ref[:, 0:2 * W_IN_BLK], wide_ref[0])
        for n in range(1, N_BLK // 2):
            dh = dh + _dot_nt(dz_ref[:, 2 * n * W_IN_BLK:2 * (n + 1) * W_IN_BLK], wide_ref[n])
        dx, dg = _rms_bwd_rows(x_ref[...], g_ref[...], dh, dx1_ref[...])
        dx0_ref[...] = dx
        dxb_ref[...] = dx.astype(BF16)
        dg_ref[...] += dg

    row = pl.BlockSpec((t, D_MODEL), lambda i: (i, 0))
    return pl.pallas_call(
        body, name="in_proj_bwd", grid=(s // t,),
        in_specs=[row, _full((1, D_MODEL)), row, pl.BlockSpec((t, D_IN), lambda i: (i, 0)),
                  pl.BlockSpec((N_BLK, None, D_MODEL, W_IN_BLK), lambda i: (0, layer, 0, 0))],
        out_specs=[row, row, _full((1, D_MODEL))],
        out_shape=[jax.ShapeDtypeStruct((s, D_MODEL), F32), jax.ShapeDtypeStruct((s, D_MODEL), BF16),
                   jax.ShapeDtypeStruct((1, D_MODEL), F32)],
        scratch_shapes=[pltpu.VMEM((N_BLK // 2, D_MODEL, 2 * W_IN_BLK), BF16)],
        compiler_params=_params(("arbitrary",), V7X_VMEM_LIMIT),
    )(x, g, dx1, dz, w)


def _in_proj_wgrad(hb, dz):
    s = hb.shape[0]
    t = _tile(s, 1024)

    def body(h_ref, dz_ref, dw_ref, wide_ref):
        @pl.when(pl.program_id(0) == 0)
        def _():
            wide_ref[...] = jnp.zeros_like(wide_ref)

        h = h_ref[...]
        for n in range(N_BLK // 2):
            wide_ref[n] += _dot_tn(h, dz_ref[:, 2 * n * W_IN_BLK:2 * (n + 1) * W_IN_BLK])

        @pl.when(pl.program_id(0) == s // t - 1)
        def _():
            for b in range(N_BLK):
                dw_ref[b] = wide_ref[b // 2, :, (b % 2) * W_IN_BLK:(b % 2 + 1) * W_IN_BLK]

    return pl.pallas_call(
        body, name="in_proj_wgrad", grid=(s // t,),
        in_specs=[pl.BlockSpec((t, D_MODEL), lambda i: (i, 0)), pl.BlockSpec((t, D_IN), lambda i: (i, 0))],
        out_specs=_full((N_BLK, D_MODEL, W_IN_BLK)),
        out_shape=jax.ShapeDtypeStruct((N_BLK, D_MODEL, W_IN_BLK), F32),
        scratch_shapes=[pltpu.VMEM((N_BLK // 2, D_MODEL, 2 * W_IN_BLK), F32)],
        compiler_params=_params(("arbitrary",), V7X_VMEM_LIMIT),
    )(hb, dz)


def _layer_params(small, layer):
    tril = jnp.tril(jnp.ones((CHUNK, CHUNK), bool))
    ws = jnp.where(tril, small["gmlp_w_s"][layer], 0.0)
    bmat = jnp.repeat(small["gmlp_b_s"][layer].T, HEAD_DIM, axis=1)
    scw = jnp.zeros((8, D_GROUP), F32).at[:SHORT_K].set(small["short_conv_w"][layer])
    ccw = jnp.zeros((32, D_GROUP), F32).at[:CONF_K].set(small["conf_conv_w"][layer])
    return dict(vg=small["gmlp_v_g"][layer][None], wt=ws.astype(BF16), wtt=jnp.swapaxes(ws, 1, 2).astype(BF16),
                bmat=bmat, scw=scw, ccw=ccw, lg=small["conf_ln_g"][layer][None], lb=small["conf_ln_b"][layer][None])


def _local_step(x, target, big, small, gather_pending=False, core=None):
    saved = []
    for l in range(DEPTH):
        p = _layer_params(small, l)
        z, hb, q_r, q_t, k_t, v_t = _in_proj(x, small["norm_mix_g"][l][None], big["w_in"], l)
        if gather_pending and l == 0:
            late = ("w_out", "w_up", "w_down")
            y_abd, filled = _mixers_fwd(z, p, [big[k] for k in late], [(n, 0) for n in range(len(late))])
            big = {**big, **dict(zip(late, filled))}
            o_t, filled = _attn_fwd(q_r, k_t, v_t, [big[k] for k in _BIG], [(n, 1) for n in range(len(_BIG))])
            big = dict(zip(_BIG, filled))
        else:
            y_abd = _mixers_fwd(z, p)
            o_t = _attn_fwd(q_r, k_t, v_t)
        x1, yn = _out_proj(x, y_abd, o_t, small["mix_out_g"][l][None], big["w_out"], l)
        x2, pre, h2b = _ffn(x1, small["norm_ffn_g"][l][None], big["w_up"], big["w_down"], l)
        saved.append(dict(p=p, x0=x, z=z, hb=hb, q_r=q_r, q_t=q_t, k_t=k_t, v_t=v_t, o_t=o_t, x1=x1, yn=yn, pre=pre,
                          h2b=h2b, y_abd=y_abd))
        x = x2

    loss, dx, dxb, d_final = _loss_head(x, small["final_norm_g"][None], target)

    g = {k: [None] * DEPTH for k in ("w_in", "w_out", "w_up", "w_down", "norm_mix_g", "gmlp_v_g", "gmlp_w_s", "gmlp_b_s",
                                     "short_conv_w", "conf_conv_w", "conf_ln_g", "conf_ln_b", "mix_out_g", "norm_ffn_g")}
    tril = jnp.tril(jnp.ones((CHUNK, CHUNK), bool))
    early = {}
    ffn_sums = [[]] * DEPTH
    for l in reversed(range(DEPTH)):
        sv = saved[l]
        p = sv["p"]
        riding, sums = [], []
        if core is not None and l == 0:
            riding = [(k, 1) for k in _BIG]
            sums = _chip_sums(core, [g["w_in"][1], g["w_out"][1]]) + ffn_sums[1]
        dpre, dh, *got = _ffn_bwd(dxb, sv["pre"], big["w_up"], big["w_down"], l, sums)
        early.update(zip(riding, zip(sums, *got)))
        g["w_up"][l], g["w_down"][l] = _ffn_wgrad(sv["h2b"], sv["pre"], dpre, dxb)
        halves = [g["w_up"][l], g["w_down"][l]] if core is not None else []
        dx1, g["norm_ffn_g"][l], dyn, g["mix_out_g"][l], g["w_out"][l], *swapped = _out_proj_bwd(
            sv["x1"], small["norm_ffn_g"][l][None], dh, dx, sv["yn"], small["mix_out_g"][l][None], big["w_out"], l,
            halves)
        if halves:
            ffn_sums[l] = [_add_pairs(core, a, b) for a, b in zip(halves, *swapped)]
        riding, sums = [], []
        if core is not None and l == 0:
            riding, sums = [("w_up", 0), ("w_down", 0)], ffn_sums[0]
        (dza, dcb, dcd, do_r, do_t, dsum, dvg, dws, dbm, dscw, dccw, dlg, dlb, *got) = _mixers_bwd_a(
            sv["z"], dyn, sv["o_t"], sv["y_abd"], p, sums)
        early.update(zip(riding, zip(sums, *got)))
        dq_t, dk_t, dv_t = _attn_bwd(sv["q_r"], sv["q_t"], sv["k_t"], sv["v_t"], do_r, do_t, dsum)
        dz = _mixers_bwd_b(sv["z"], dza, dcb, dcd, dq_t, dk_t, dv_t, p)
        dx, dxb, g["norm_mix_g"][l] = _in_proj_bwd(sv["x0"], small["norm_mix_g"][l][None], dx1, dz, big["w_in"], l)
        g["w_in"][l] = _in_proj_wgrad(sv["hb"], dz)
        g["gmlp_v_g"][l] = dvg[0]
        g["gmlp_w_s"][l] = jnp.where(tril, dws, 0.0)
        g["gmlp_b_s"][l] = dbm.reshape(CHUNK, N_HEADS, HEAD_DIM).sum(-1).T
        g["short_conv_w"][l] = dscw[:SHORT_K]
        g["conf_conv_w"][l] = dccw[:CONF_K]
        g["conf_ln_g"][l] = dlg[0]
        g["conf_ln_b"][l] = dlb[0]
        g["norm_mix_g"][l] = g["norm_mix_g"][l][0]
        g["mix_out_g"][l] = g["mix_out_g"][l][0]
        g["norm_ffn_g"][l] = g["norm_ffn_g"][l][0]
    grads = {k: v if k in ("w_in", "w_out", "w_up", "w_down") else jnp.stack(v) for k, v in g.items()}
    grads["final_norm_g"] = d_final[0]
    return loss, dx, grads, early


_ANY = pl.BlockSpec(memory_space=pl.ANY)


def _mesh_place():
    x, y, c = lax.axis_index("x"), lax.axis_index("y"), lax.axis_index("c")
    chips = [(1 - x, y), (x, 1 - y), (1 - x, 1 - y)]
    return x, y, c, 2 * x + y, chips


def _gather_stages(bufs, parts, sems):
    ici_send, ici_recv, d2d_send, d2d_recv = sems
    x, y, c, me, chips = _mesh_place()
    blk = [2 * chip[0] + chip[1] for chip in chips]
    pairs = [(p, r) for p in range(len(parts)) for r in range(3)]

    def rows(p, block, half_of):
        k, layer = parts[p]
        half = bufs[k].shape[2] // 2
        return bufs[k].at[block, layer, pl.ds(half_of * half, half), :]

    def ici(p, r, block):
        return pltpu.make_async_remote_copy(
            src_ref=rows(p, me, c), dst_ref=rows(p, block, c), send_sem=ici_send.at[3 * p + r],
            recv_sem=ici_recv.at[3 * p + r], device_id=(chips[r][0], chips[r][1], c), device_id_type=MESH)

    def d2d(p, r, half_of):
        part = rows(p, blk[r], half_of)
        return pltpu.make_async_remote_copy(
            src_ref=part, dst_ref=part, send_sem=d2d_send.at[3 * p + r], recv_sem=d2d_recv.at[3 * p + r],
            device_id=(x, y, 1 - c), device_id_type=MESH)

    def start():
        for p, r in pairs:
            ici(p, r, me).start()

    def forward(p):
        for r in range(3):
            ici(p, r, blk[r]).wait_recv()
            d2d(p, r, c).start()

    def finish():
        for p, r in pairs:
            d2d(p, r, 1 - c).wait_recv()
        for p, r in pairs:
            ici(p, r, me).wait_send()
            d2d(p, r, c).wait_send()

    return start, forward, finish


def _gather_sems(parts):
    return [pltpu.SemaphoreType.DMA((3 * len(parts),)) for _ in range(4)]


def _gather_in_steps(bufs, parts, sems, n_steps):
    start, forward, finish = _gather_stages(bufs, parts, sems)
    i = pl.program_id(0)
    pl.when(i == 0)(start)
    for p in range(len(parts)):
        pl.when(i == n_steps * (2 * p + 3) // (2 * len(parts) + 2))(lambda p=p: forward(p))
    pl.when(i == n_steps - 1)(finish)


def _gather_first(bufs, parts, whole):
    n, m = len(bufs), len(whole)

    def body(*refs):
        whole_in, buf_out, whole_out = refs[n:n + m], refs[n + m:2 * n + m], refs[2 * n + m:2 * (n + m)]
        sems = refs[2 * (n + m):]
        send_sems, recv_sems, local_sems = sems[4:]
        x, y, c, me, chips = _mesh_place()
        start, forward, finish = _gather_stages(buf_out, parts, sems[:4])

        def push(k, r, block):
            return pltpu.make_async_remote_copy(
                src_ref=whole_in[k], dst_ref=whole_out[k].at[block], send_sem=send_sems.at[3 * k + r],
                recv_sem=recv_sems.at[3 * k + r], device_id=(chips[r][0], chips[r][1], c), device_id_type=MESH)

        local = [pltpu.make_async_copy(whole_in[k], whole_out[k].at[me], local_sems.at[k]) for k in range(m)]
        for cp in local:
            cp.start()
        start()
        for k in range(m):
            for r in range(3):
                push(k, r, me).start()
        for p in range(len(parts)):
            forward(p)
        for k in range(m):
            for r, chip in enumerate(chips):
                push(k, r, 2 * chip[0] + chip[1]).wait_recv()
        for k in range(m):
            for r in range(3):
                push(k, r, me).wait_send()
        finish()
        for cp in local:
            cp.wait()

    return pl.pallas_call(
        body, name="gather_first",
        in_specs=[_ANY] * (n + m), out_specs=[_ANY] * (n + m),
        out_shape=[jax.ShapeDtypeStruct(b.shape, b.dtype) for b in bufs]
                  + [jax.ShapeDtypeStruct((N_BLK,) + b.shape, b.dtype) for b in whole],
        input_output_aliases={k: k for k in range(n)},
        scratch_shapes=_gather_sems(parts) + [pltpu.SemaphoreType.DMA((3 * m,)), pltpu.SemaphoreType.DMA((3 * m,)),
                                              pltpu.SemaphoreType.DMA((m,))],
    )(*bufs, *whole)


def _swap_halves(gs):
    n = len(gs)

    def body(*refs):
        start, finish = _halves_stages(refs[:n], refs[n:2 * n], refs[2 * n:])
        start()
        finish()

    return pl.pallas_call(
        body, name="swap_halves", in_specs=[_ANY] * n, out_specs=[_ANY] * n,
        out_shape=_halves_shapes(gs), scratch_shapes=_halves_sems(n),
    )(*gs)


def _halves_stages(ins, outs, sems):
    send_sems, recv_sems = sems
    x, y, c, _, _ = _mesh_place()

    def copy(k):
        half = ins[k].shape[1] // 2
        return pltpu.make_async_remote_copy(
            src_ref=ins[k].at[:, pl.ds((1 - c) * half, half), :], dst_ref=outs[k],
            send_sem=send_sems.at[k], recv_sem=recv_sems.at[k], device_id=(x, y, 1 - c), device_id_type=MESH)

    def start():
        for k in range(len(ins)):
            copy(k).start()

    def finish():
        for k in range(len(ins)):
            copy(k).wait()

    return start, finish


def _halves_shapes(gs):
    return [jax.ShapeDtypeStruct((g.shape[0], g.shape[1] // 2, g.shape[2]), F32) for g in gs]


def _halves_sems(n):
    return [pltpu.SemaphoreType.DMA((n,)), pltpu.SemaphoreType.DMA((n,))]


def _owner_stages(ins, outs, sems):
    send_sems, recv_sems = sems
    x, y, c, me, chips = _mesh_place()
    pairs = [(k, r) for k in range(len(ins)) for r in range(3)]

    def remote(k, r, src_block, dst_block):
        return pltpu.make_async_remote_copy(
            src_ref=ins[k].at[src_block], dst_ref=outs[k].at[dst_block], send_sem=send_sems.at[3 * k + r],
            recv_sem=recv_sems.at[3 * k + r], device_id=(chips[r][0], chips[r][1], c), device_id_type=MESH)

    def start():
        for k, r in pairs:
            remote(k, r, 2 * chips[r][0] + chips[r][1], me).start()

    def finish():
        for k, r in pairs:
            remote(k, r, me, 2 * chips[r][0] + chips[r][1]).wait_recv()
        for k, r in pairs:
            remote(k, r, 2 * chips[r][0] + chips[r][1], me).wait_send()

    return start, finish


def _owner_sems(n):
    return [pltpu.SemaphoreType.DMA((3 * n,)), pltpu.SemaphoreType.DMA((3 * n,))]


def _owners_in_steps(ins, outs, sems, first, last):
    start, finish = _owner_stages(ins, outs, sems)
    pl.when(first)(start)
    pl.when(last)(finish)


def _send_to_owners(sums):
    n = len(sums)

    def body(*refs):
        start, finish = _owner_stages(refs[:n], refs[n:2 * n], refs[2 * n:])
        start()
        finish()

    return pl.pallas_call(
        body, name="send_to_owners", in_specs=[_ANY] * n, out_specs=[_ANY] * n,
        out_shape=[jax.ShapeDtypeStruct(s.shape, s.dtype) for s in sums],
        scratch_shapes=_owner_sems(n),
    )(*sums)


def _swap_reduced(fs):
    n = len(fs)

    def body(*refs):
        ins, outs, (send_sems, recv_sems) = refs[:n], refs[n:2 * n], refs[2 * n:]
        x, y, c, _, _ = _mesh_place()
        cps = [pltpu.make_async_remote_copy(src_ref=ins[k], dst_ref=outs[k], send_sem=send_sems.at[k],
                                            recv_sem=recv_sems.at[k], device_id=(x, y, 1 - c), device_id_type=MESH)
               for k in range(n)]
        for cp in cps:
            cp.start()
        for cp in cps:
            cp.wait()

    return pl.pallas_call(
        body, name="swap_reduced", in_specs=[_ANY] * n, out_specs=[_ANY] * n,
        out_shape=[jax.ShapeDtypeStruct(f.shape, F32) for f in fs],
        scratch_shapes=[pltpu.SemaphoreType.DMA((n,)), pltpu.SemaphoreType.DMA((n,))],
    )(*fs)


def _row_tile(rows):
    return min(rows, 256)


def _chip_sums(core, grads):
    if not grads:
        return []
    return [_add_pairs(core, a, b) for a, b in zip(grads, _swap_halves(grads))]


def _add_pairs(core, g, other):
    n, half, cols = other.shape
    t = _row_tile(half)
    per_half = half // t

    def body(c_ref, a_ref, b_ref, o_ref):
        o_ref[...] = (a_ref[...] + b_ref[...]).astype(BF16)

    spec = pl.BlockSpec((None, t, cols), lambda i, j, c_ref: (i, j, 0))
    return pl.pallas_call(
        body, name="add_pairs",
        grid_spec=pltpu.PrefetchScalarGridSpec(
            num_scalar_prefetch=1, grid=(n, per_half),
            in_specs=[pl.BlockSpec((None, t, cols), lambda i, j, c_ref: (i, c_ref[0] * per_half + j, 0)), spec],
            out_specs=spec),
        out_shape=jax.ShapeDtypeStruct(other.shape, BF16), compiler_params=_params(("parallel", "parallel")),
    )(core, g, other)


def _add_chips(me, s1, r2):
    _, r, cols = r2.shape
    t = _row_tile(r)

    def body(me_ref, s_ref, r_ref, o_ref):
        own = s_ref[...].astype(F32)
        parts = [jnp.where(me_ref[0] == k, own, r_ref[k].astype(F32)) for k in range(N_BLK)]
        o_ref[...] = ((parts[0] + parts[1]) + parts[2]) + parts[3]

    return pl.pallas_call(
        body, name="add_chips",
        grid_spec=pltpu.PrefetchScalarGridSpec(
            num_scalar_prefetch=1, grid=(r // t,),
            in_specs=[pl.BlockSpec((None, t, cols), lambda i, me_ref: (me_ref[0], i, 0)),
                      pl.BlockSpec((N_BLK, t, cols), lambda i, me_ref: (0, i, 0))],
            out_specs=pl.BlockSpec((t, cols), lambda i, me_ref: (i, 0))),
        out_shape=jax.ShapeDtypeStruct((r, cols), F32), compiler_params=_params(("parallel",)),
    )(me, s1, r2)


def _adamw(core, mine, other, w, m, v, layer, earlier=None):
    half, cols = mine.shape
    t = _row_tile(half)
    per_half = half // t
    c1 = 1.0 - ADAM_B1 ** ADAM_STEP
    c2 = 1.0 - ADAM_B2 ** ADAM_STEP

    def body(c_ref, a_ref, b_ref, w_ref, m_ref, v_ref, *rest):
        g_ref, d_ref, mo_ref, vo_ref = rest[-4:]
        gv = jnp.where(pl.program_id(0) // per_half == c_ref[0], a_ref[...], b_ref[...])
        g_ref[...] = gv
        m_new = ADAM_B1 * m_ref[...] + (1.0 - ADAM_B1) * gv
        v_new = ADAM_B2 * v_ref[...] + (1.0 - ADAM_B2) * (gv * gv)
        mo_ref[...] = m_new
        vo_ref[...] = v_new
        d_ref[...] = -ADAM_LR * ((m_new / c1) / (jnp.sqrt(v_new / c2) + ADAM_EPS) + ADAM_WD * w_ref[...])

    part = pl.BlockSpec((t, cols), lambda i, c_ref: (i % per_half, 0))
    spec = pl.BlockSpec((None, t, cols), lambda i, c_ref: (layer, i, 0))
    kept = [] if earlier is None else list(earlier)
    return pl.pallas_call(
        body, name="adamw",
        grid_spec=pltpu.PrefetchScalarGridSpec(
            num_scalar_prefetch=1, grid=(2 * per_half,),
            in_specs=[part, part, spec, spec, spec] + [_ANY] * len(kept), out_specs=[spec] * 4),
        out_shape=[jax.ShapeDtypeStruct(w.shape, F32)] * 4,
        input_output_aliases={6 + k: k for k in range(len(kept))},
        compiler_params=_params(("parallel",)),
    )(core, mine, other, w, m, v, *kept)


_REPLICATED = ("norm_mix_g", "gmlp_v_g", "gmlp_w_s", "gmlp_b_s", "conf_ln_g", "conf_ln_b", "mix_out_g", "norm_ffn_g",
               "final_norm_g")
_REP_SHAPES = {"norm_mix_g": (DEPTH, D_MODEL), "gmlp_v_g": (DEPTH, D_GROUP), "gmlp_w_s": (DEPTH, N_HEADS, CHUNK, CHUNK),
               "gmlp_b_s": (DEPTH, N_HEADS, CHUNK), "conf_ln_g": (DEPTH, D_GROUP), "conf_ln_b": (DEPTH, D_GROUP),
               "mix_out_g": (DEPTH, D_MODEL), "norm_ffn_g": (DEPTH, D_MODEL), "final_norm_g": (D_MODEL,)}
_BIG = ("w_in", "w_out", "w_up", "w_down")
_CONV_ROWS = 8
_REP_ROWS = 144
_SMALL_ROWS = 160
_CH_BLK = D_GROUP // N_BLK


def _pad_rows(flat, rows):
    pad = rows * D_MODEL - flat.shape[-1]
    flat = jnp.pad(flat, [(0, 0)] * (flat.ndim - 1) + [(0, pad)])
    return flat.reshape(flat.shape[:-1] + (rows, D_MODEL))


def _pack_small(scw, ccw, rep):
    lead = scw.shape[:-3]
    conv = jnp.concatenate([scw.reshape(lead + (-1,)), ccw.reshape(lead + (-1,))], axis=-1)
    flat = jnp.concatenate([rep[k].reshape(-1) for k in _REPLICATED])
    flat = jnp.broadcast_to(flat, lead + flat.shape)
    parts = [_pad_rows(conv, _CONV_ROWS), _pad_rows(flat, _REP_ROWS),
             jnp.zeros(lead + (_SMALL_ROWS - _CONV_ROWS - _REP_ROWS, D_MODEL), F32)]
    return jnp.concatenate(parts, axis=-2)


def _unpack_small(pk):
    out = {}
    conv = pk[:_CONV_ROWS].reshape(-1)
    n_s = DEPTH * SHORT_K * _CH_BLK
    out["short_conv_w"] = conv[:n_s].reshape(DEPTH, SHORT_K, _CH_BLK)
    out["conf_conv_w"] = conv[n_s:n_s + DEPTH * CONF_K * _CH_BLK].reshape(DEPTH, CONF_K, _CH_BLK)
    row = _CONV_ROWS
    flat = pk[row:row + _REP_ROWS].reshape(-1)
    at = 0
    for k in _REPLICATED:
        n = math.prod(_REP_SHAPES[k])
        out[k] = flat[at:at + n].reshape(_REP_SHAPES[k])
        at += n
    return out


def _conv_blocks(w):
    d, k, _ = w.shape
    return w.reshape(d, k, N_BLK, _CH_BLK).transpose(2, 0, 1, 3)


_WEIGHTS = ("norm_mix_g", "w_in", "gmlp_v_g", "gmlp_w_s", "gmlp_b_s", "short_conv_w", "conf_conv_w", "conf_ln_g",
            "conf_ln_b", "mix_out_g", "w_out", "norm_ffn_g", "w_up", "w_down", "final_norm_g")


def kernel(x, norm_mix_g, w_in, gmlp_v_g, gmlp_w_s, gmlp_b_s, short_conv_w, conf_conv_w, conf_ln_g, conf_ln_b, mix_out_g, w_out, norm_ffn_g, w_up, w_down, final_norm_g, loss_target, m_norm_mix_g, m_w_in, m_gmlp_v_g, m_gmlp_w_s, m_gmlp_b_s, m_short_conv_w, m_conf_conv_w, m_conf_ln_g, m_conf_ln_b, m_mix_out_g, m_w_out, m_norm_ffn_g, m_w_up, m_w_down, m_final_norm_g, v_norm_mix_g, v_w_in, v_gmlp_v_g, v_gmlp_w_s, v_gmlp_b_s, v_short_conv_w, v_conf_conv_w, v_conf_ln_g, v_conf_ln_b, v_mix_out_g, v_w_out, v_norm_ffn_g, v_w_up, v_w_down, v_final_norm_g):
    w = dict(norm_mix_g=norm_mix_g, w_in=w_in, gmlp_v_g=gmlp_v_g, gmlp_w_s=gmlp_w_s, gmlp_b_s=gmlp_b_s,
             short_conv_w=short_conv_w, conf_conv_w=conf_conv_w, conf_ln_g=conf_ln_g, conf_ln_b=conf_ln_b,
             mix_out_g=mix_out_g, w_out=w_out, norm_ffn_g=norm_ffn_g, w_up=w_up, w_down=w_down, final_norm_g=final_norm_g)
    m = dict(norm_mix_g=m_norm_mix_g, w_in=m_w_in, gmlp_v_g=m_gmlp_v_g, gmlp_w_s=m_gmlp_w_s, gmlp_b_s=m_gmlp_b_s,
             short_conv_w=m_short_conv_w, conf_conv_w=m_conf_conv_w, conf_ln_g=m_conf_ln_g, conf_ln_b=m_conf_ln_b,
             mix_out_g=m_mix_out_g, w_out=m_w_out, norm_ffn_g=m_norm_ffn_g, w_up=m_w_up, w_down=m_w_down,
             final_norm_g=m_final_norm_g)
    v = dict(norm_mix_g=v_norm_mix_g, w_in=v_w_in, gmlp_v_g=v_gmlp_v_g, gmlp_w_s=v_gmlp_w_s, gmlp_b_s=v_gmlp_b_s,
             short_conv_w=v_short_conv_w, conf_conv_w=v_conf_conv_w, conf_ln_g=v_conf_ln_g, conf_ln_b=v_conf_ln_b,
             mix_out_g=v_mix_out_g, w_out=v_w_out, norm_ffn_g=v_norm_ffn_g, w_up=v_w_up, w_down=v_w_down,
             final_norm_g=v_final_norm_g)
    core = lax.axis_index("c").astype(jnp.int32).reshape(1)
    me = (2 * lax.axis_index("x") + lax.axis_index("y")).astype(jnp.int32).reshape(1)

    conv_mine = _pad_rows(jnp.concatenate([short_conv_w.reshape(-1), conf_conv_w.reshape(-1)]), _CONV_ROWS)
    big = {k: _cast_into_slot(w[k], me, "cast_" + k) for k in _BIG}
    big["w_in"], conv_all = _gather_first([big["w_in"]], [(0, 0)], [conv_mine])
    conv_all = conv_all.reshape(N_BLK, -1)
    n_s = DEPTH * SHORT_K * _CH_BLK
    scw_all = conv_all[:, :n_s].reshape(N_BLK, DEPTH, SHORT_K, _CH_BLK)
    ccw_all = conv_all[:, n_s:n_s + DEPTH * CONF_K * _CH_BLK].reshape(N_BLK, DEPTH, CONF_K, _CH_BLK)
    small = {k: w[k] for k in _REPLICATED}
    small["short_conv_w"] = scw_all.transpose(1, 2, 0, 3).reshape(DEPTH, SHORT_K, D_GROUP)
    small["conf_conv_w"] = ccw_all.transpose(1, 2, 0, 3).reshape(DEPTH, CONF_K, D_GROUP)

    loss, dx, g, early = _local_step(x[0], loss_target[0], big, small, gather_pending=True, core=core)

    where = [(k, l) for k in _BIG for l in range(DEPTH)]
    late = [kl for kl in where if kl not in early]
    sums = _chip_sums(core, [g[k][l] for k, l in late]
                      + [_pack_small(_conv_blocks(g["short_conv_w"]), _conv_blocks(g["conf_conv_w"]), g)])
    sent = {**early, **dict(zip(late + ["small"], zip(sums, _send_to_owners(sums))))}
    mine = [_add_chips(me, *sent[kl]) for kl in where + ["small"]]
    other = _swap_reduced(mine)

    done = {}
    for n, (k, l) in enumerate(where):
        done[k] = _adamw(core, mine[n], other[n], w[k], m[k], v[k], l, done.get(k))
    small_own = [_pack_small(t["short_conv_w"], t["conf_conv_w"], t)[None] for t in (w, m, v)]
    small_done = [_unpack_small(a[0]) for a in _adamw(core, mine[-1], other[-1], *small_own, 0)]

    outs = [lax.psum(loss[0, 0], ("x", "y", "c")), dx[None]]
    for kind in range(4):
        outs += [done[k][kind] if k in _BIG else small_done[kind][k] for k in _WEIGHTS]
    return tuple(outs)
```

```python
import math

import jax
import jax.numpy as jnp
from jax import lax
from jax.experimental import pallas as pl
from jax.experimental.pallas import tpu as pltpu

F32 = jnp.float32
BF16 = jnp.bfloat16

D_MODEL = 1024
D_GROUP = 256
N_HEADS = 4
HEAD_DIM = 64
CHUNK = 128
D_IN = 2560
N_BLK = 4
W_IN_BLK = D_IN // N_BLK
D_FF_BLK = 1024
DEPTH = 2
EPS = 1e-6
HALO = 32
MIX_ROWS = 512
SHORT_K = 3
CONF_K = 31
ATT_TQ = 256
ATT_TK = 256
ATT_SCALE = 0.125
ATT_DEAD = -104.0
V7X_VMEM_LIMIT = 56 * 1024 * 1024

ADAM_LR, ADAM_B1, ADAM_B2, ADAM_EPS, ADAM_WD, ADAM_STEP = 0.001, 0.9, 0.999, 1e-08, 0.01, 10

MESH = pl.DeviceIdType.MESH


def _params(sem, vmem=None):
    return pltpu.CompilerParams(dimension_semantics=sem, vmem_limit_bytes=vmem)


def _tile(s, t):
    return min(s, t)


def _rsqrt_mean(v):
    return lax.rsqrt(jnp.mean(v * v, axis=-1, keepdims=True) + EPS)


def _sigmoid(v):
    return 1.0 / (1.0 + jnp.exp(-v))


_GELU_C = math.sqrt(2.0 / math.pi)


def _gelu_tanh(v):
    return jnp.tanh(_GELU_C * (v + 0.044715 * (v * v * v)))


def _gelu(v, t):
    return v * (0.5 * (1.0 + t))


def _gelu_grad(v, t):
    return 0.5 * (1.0 + t) + v * (0.5 * (1.0 - t * t) * _GELU_C * (1.0 + 3.0 * 0.044715 * (v * v)))


def _dot(a, b):
    return jnp.dot(a, b, preferred_element_type=F32)


def _dot_nt(a, b):
    return lax.dot_general(a, b, (((1,), (1,)), ((), ())), preferred_element_type=F32)


def _dot_tn(a, b):
    return lax.dot_general(a, b, (((0,), (0,)), ((), ())), preferred_element_type=F32)


def _cast_into_slot(w, me, name):
    n, r, c = w.shape
    tr = _tile(r, 256)

    def body(me_ref, w_ref, o_ref):
        o_ref[...] = w_ref[...].astype(BF16)

    return pl.pallas_call(
        body, name=name,
        grid_spec=pltpu.PrefetchScalarGridSpec(
            num_scalar_prefetch=1, grid=(n, r // tr),
            in_specs=[pl.BlockSpec((None, tr, c), lambda a, b, me_ref: (a, b, 0))],
            out_specs=pl.BlockSpec((None, None, tr, c), lambda a, b, me_ref: (me_ref[0], a, b, 0))),
        out_shape=jax.ShapeDtypeStruct((N_BLK,) + w.shape, BF16),
        compiler_params=_params(("parallel", "parallel")),
    )(me, w)


def _split_heads(xv, rows_ref, cols_ref):
    if rows_ref is not None:
        for h in range(N_HEADS):
            rows_ref[h] = xv[:, h * HEAD_DIM:(h + 1) * HEAD_DIM].astype(BF16)
    if cols_ref is not None:
        xt = xv.T
        for h in range(N_HEADS):
            cols_ref[h] = xt[h * HEAD_DIM:(h + 1) * HEAD_DIM, :].astype(BF16)


def _head_specs(t, s):
    rows = (pl.BlockSpec((N_HEADS, t, HEAD_DIM), lambda i: (0, i, 0)), jax.ShapeDtypeStruct((N_HEADS, s, HEAD_DIM), BF16))
    cols = (pl.BlockSpec((N_HEADS, HEAD_DIM, t), lambda i: (0, 0, i)), jax.ShapeDtypeStruct((N_HEADS, HEAD_DIM, s), BF16))
    return rows, cols


def _cols(ref, lo, hi):
    return ref[:, lo:hi].astype(F32)


def _pair_blocks(w_ref, wide_ref):
    @pl.when(pl.program_id(0) == 0)
    def _():
        for b in range(N_BLK):
            wide_ref[b // 2, :, (b % 2) * W_IN_BLK:(b % 2 + 1) * W_IN_BLK] = w_ref[b]


def _in_proj(x, g, w, layer):
    s = x.shape[0]
    t = _tile(s, 1024)

    def body(x_ref, g_ref, w_ref, z_ref, h_ref, qr_ref, qt_ref, kt_ref, vt_ref, wide_ref):
        _pair_blocks(w_ref, wide_ref)
        xv = x_ref[...]
        h = (xv * _rsqrt_mean(xv) * g_ref[...]).astype(BF16)
        h_ref[...] = h
        for n in range(N_BLK // 2):
            z_ref[:, 2 * n * W_IN_BLK:2 * (n + 1) * W_IN_BLK] = _dot(h, wide_ref[n]).astype(BF16)
        _split_heads(_cols(z_ref, 1280, 1536) * ATT_SCALE, qr_ref, qt_ref)
        _split_heads(_cols(z_ref, 1536, 1792), None, kt_ref)
        _split_heads(_cols(z_ref, 1792, 2048), None, vt_ref)

    rows, cols = _head_specs(t, s)
    return pl.pallas_call(
        body, name="in_proj", grid=(s // t,),
        in_specs=[pl.BlockSpec((t, D_MODEL), lambda i: (i, 0)), _full((1, D_MODEL)),
                  pl.BlockSpec((N_BLK, None, D_MODEL, W_IN_BLK), lambda i: (0, layer, 0, 0))],
        out_specs=[pl.BlockSpec((t, D_IN), lambda i: (i, 0)), pl.BlockSpec((t, D_MODEL), lambda i: (i, 0)),
                   rows[0], cols[0], cols[0], cols[0]],
        out_shape=[jax.ShapeDtypeStruct((s, D_IN), BF16), jax.ShapeDtypeStruct((s, D_MODEL), BF16),
                   rows[1], cols[1], cols[1], cols[1]],
        scratch_shapes=[pltpu.VMEM((N_BLK // 2, D_MODEL, 2 * W_IN_BLK), BF16)],
        compiler_params=_params(("arbitrary",), V7X_VMEM_LIMIT),
    )(x, g, w)


def _mix_a_fwd(z_ref, vg, wt_ref, bmat, t):
    zu = _cols(z_ref, 0, 256)
    zv = _cols(z_ref, 256, 512)
    tu = _gelu_tanh(zu)
    tv = _gelu_tanh(zv)
    u = _gelu(zu, tu)
    v = _gelu(zv, tv)
    rv = _rsqrt_mean(v)
    vh = v * rv
    vnb = (vh * vg).astype(BF16)
    head = lax.broadcasted_iota(jnp.int32, (CHUNK, D_GROUP), 1) // HEAD_DIM
    fs = []
    for c in range(t // CHUNK):
        vc = vnb[c * CHUNK:(c + 1) * CHUNK, :]
        fc = bmat
        for h in range(N_HEADS):
            fc = fc + jnp.where(head == h, _dot(wt_ref[h], vc), 0.0)
        fs.append(fc)
    f = jnp.concatenate(fs, axis=0) if len(fs) > 1 else fs[0]
    return (zu, tu), (zv, tv), u, rv, vh, vnb, f


def _windows(ext_ref, sh_ref, t):
    for b in range(1, 8):
        sh_ref[b - 1] = ext_ref[pl.ds(b, HALO + t - 8), :]

    def window(o):
        a, b = divmod(o, 8)
        return ext_ref[pl.ds(8 * a, t), :] if b == 0 else sh_ref[b - 1, pl.ds(8 * a, t), :]

    return window


def _mix_b_fwd(z_ref, zh_ref, first, scw_ref, ext_ref, t):
    gb = _cols(z_ref, 512, 768)
    uh = _cols(zh_ref, 768, 1024) * _cols(zh_ref, 1024, 1280)
    ext_ref[0:HALO, :] = jnp.where(first, 0.0, uh)
    ext_ref[HALO:HALO + t, :] = _cols(z_ref, 768, 1024) * _cols(z_ref, 1024, 1280)
    cv = jnp.zeros((t, D_GROUP), F32)
    for k in range(SHORT_K):
        cv = cv + scw_ref[k:k + 1, :] * ext_ref[pl.ds(HALO - (SHORT_K - 1) + k, t), :]
    return gb, cv


def _mix_d_fwd(z_ref, zh_ref, first, ccw_ref, lg, lb, ext_ref, sh_ref, t, cv=None):
    hh = _cols(zh_ref, 2048, 2304) * _sigmoid(_cols(zh_ref, 2304, 2560))
    ext_ref[0:HALO, :] = jnp.where(first, 0.0, hh)
    ext_ref[HALO:HALO + t, :] = _cols(z_ref, 2048, 2304) * _sigmoid(_cols(z_ref, 2304, 2560))
    window = _windows(ext_ref, sh_ref, t)
    if cv is None:
        cv = jnp.zeros((t, D_GROUP), F32)
        for k in range(CONF_K):
            cv = cv + ccw_ref[k:k + 1, :] * window(HALO - (CONF_K - 1) + k)
    xc = cv - jnp.mean(cv, axis=-1, keepdims=True)
    rs = lax.rsqrt(jnp.mean(xc * xc, axis=-1, keepdims=True) + EPS)
    xh = xc * rs
    ln = xh * lg + lb
    return xh, rs, ln, _sigmoid(ln), window, cv


def _mix_specs(t, s):
    per = t // HALO
    return [pl.BlockSpec((t, D_IN), lambda i: (i, 0)),
            pl.BlockSpec((HALO, D_IN), lambda i: (jnp.maximum(i * per - 1, 0), 0))]


def _full(shape):
    return pl.BlockSpec(shape, lambda i: (0,) * len(shape))


def _mixers_fwd(z, p, bufs=(), parts=()):
    s = z.shape[0]
    t = _tile(s, MIX_ROWS)
    nb = len(bufs)

    def body(z_ref, zh_ref, vg_ref, wt_ref, bm_ref, scw_ref, ccw_ref, lg_ref, lb_ref, *rest):
        y_ref = rest[nb]
        eb_ref, ed_ref, sh_ref = rest[2 * nb + 1:2 * nb + 4]
        if nb:
            _gather_in_steps(rest[nb + 1:2 * nb + 1], parts, rest[2 * nb + 4:], s // t)
        first = pl.program_id(0) == 0
        _, _, u, _, _, _, f = _mix_a_fwd(z_ref, vg_ref[...], wt_ref, bm_ref[...], t)
        ya = u * f
        y_ref[:, 0:256] = ya * _rsqrt_mean(ya)
        gb, cv = _mix_b_fwd(z_ref, zh_ref, first, scw_ref, eb_ref, t)
        yb = gb * cv
        y_ref[:, 256:512] = yb * _rsqrt_mean(yb)
        _, _, ln, sg, _, cvd = _mix_d_fwd(z_ref, zh_ref, first, ccw_ref, lg_ref[...], lb_ref[...], ed_ref, sh_ref, t)
        yd = ln * sg
        y_ref[:, 512:768] = yd * _rsqrt_mean(yd)
        y_ref[:, 768:1024] = cvd

    out = pl.pallas_call(
        body, name="mixers_fwd", grid=(s // t,),
        in_specs=_mix_specs(t, s) + [_full((1, D_GROUP)), _full((N_HEADS, CHUNK, CHUNK)), _full((CHUNK, D_GROUP)),
                                     _full((8, D_GROUP)), _full((32, D_GROUP)), _full((1, D_GROUP)), _full((1, D_GROUP))]
                 + [_ANY] * nb,
        out_specs=[pl.BlockSpec((t, D_MODEL), lambda i: (i, 0))] + [_ANY] * nb,
        out_shape=[jax.ShapeDtypeStruct((s, D_MODEL), F32)] + [jax.ShapeDtypeStruct(b.shape, b.dtype) for b in bufs],
        input_output_aliases={9 + k: 1 + k for k in range(nb)},
        scratch_shapes=[pltpu.VMEM((HALO + t, D_GROUP), F32), pltpu.VMEM((HALO + t, D_GROUP), F32),
                        pltpu.VMEM((7, HALO + t - 8, D_GROUP), F32)] + (_gather_sems(parts) if nb else []),
        compiler_params=_params(("arbitrary",) if nb else ("parallel",), V7X_VMEM_LIMIT),
    )(z, z, p["vg"], p["wt"], p["bmat"], p["scw"], p["ccw"], p["lg"], p["lb"], *bufs)
    return (out[0], out[1:]) if nb else out[0]


def _mixers_bwd_a(z, dyn, o_t, y_abd, p, sums=()):
    s = z.shape[0]
    t = _tile(s, MIX_ROWS)
    n_chunk = t // CHUNK
    ns = len(sums)

    def body(*refs):
        (z_ref, zh_ref, dyn_ref, ot_ref, cv_ref, vg_ref, wt_ref, wtt_ref, bm_ref, scw_ref, ccw_ref, lg_ref,
         lb_ref) = refs[:13]
        (dza_ref, dcb_ref, dcd_ref, dor_ref, dot_ref, ds_ref, dvg_ref, dws_ref, dbm_ref, dscw_ref, dccw_ref, dlg_ref,
         dlb_ref) = refs[13 + ns:26 + ns]
        eb_ref, ed_ref, sh_ref = refs[26 + 2 * ns:29 + 2 * ns]
        i = pl.program_id(0)
        first = i == 0
        if ns:
            _owners_in_steps(refs[13:13 + ns], refs[26 + ns:26 + 2 * ns], refs[29 + 2 * ns:], first, i == s // t - 1)

        @pl.when(first)
        def _():
            for r in (dvg_ref, dws_ref, dbm_ref, dscw_ref, dccw_ref, dlg_ref, dlb_ref):
                r[...] = jnp.zeros_like(r)

        def rms_bwd(y, dn):
            r = _rsqrt_mean(y)
            yn = y * r
            return r * (dn - yn * jnp.mean(dn * yn, axis=-1, keepdims=True))

        vg = vg_ref[...]
        gelu_u, gelu_v, u, rv, vh, vnb, f = _mix_a_fwd(z_ref, vg, wt_ref, bm_ref[...], t)
        dya = rms_bwd(u * f, _cols(dyn_ref, 0, 256))
        du = dya * f
        df = dya * u
        head = lax.broadcasted_iota(jnp.int32, (CHUNK, D_GROUP), 1) // HEAD_DIM
        dvns = []
        dbm = jnp.zeros((CHUNK, D_GROUP), F32)
        for c in range(n_chunk):
            dfc = df[c * CHUNK:(c + 1) * CHUNK, :]
            vc = vnb[c * CHUNK:(c + 1) * CHUNK, :]
            dbm = dbm + dfc
            dvn = jnp.zeros((CHUNK, D_GROUP), F32)
            for h in range(N_HEADS):
                dfh = jnp.where(head == h, dfc, 0.0).astype(BF16)
                dvn = dvn + _dot(wtt_ref[h], dfh)
                dws_ref[h] += _dot_nt(dfh, vc)
            dvns.append(dvn)
        dbm_ref[...] += dbm
        dvn = jnp.concatenate(dvns, axis=0) if n_chunk > 1 else dvns[0]
        dvg_ref[...] += jnp.sum(dvn * vh, axis=0, keepdims=True)
        dvh = dvn * vg
        dv = rv * (dvh - vh * jnp.mean(dvh * vh, axis=-1, keepdims=True))
        dza_ref[:, 0:256] = (du * _gelu_grad(*gelu_u)).astype(BF16)
        dza_ref[:, 256:512] = (dv * _gelu_grad(*gelu_v)).astype(BF16)

        gb, cv = _mix_b_fwd(z_ref, zh_ref, first, scw_ref, eb_ref, t)
        dyb = rms_bwd(gb * cv, _cols(dyn_ref, 256, 512))
        dza_ref[:, 512:768] = (dyb * cv).astype(BF16)
        dcb = dyb * gb
        dcb_ref[...] = dcb
        for k in range(SHORT_K):
            dscw_ref[k:k + 1, :] += jnp.sum(dcb * eb_ref[pl.ds(HALO - (SHORT_K - 1) + k, t), :], axis=0, keepdims=True)

        lg = lg_ref[...]
        xh, rs, ln, sg, window, _ = _mix_d_fwd(z_ref, zh_ref, first, ccw_ref, lg, lb_ref[...], ed_ref, sh_ref, t,
                                               cv_ref[...])
        dyd = rms_bwd(ln * sg, _cols(dyn_ref, 768, 1024))
        dln = dyd * (sg * (1.0 + ln * (1.0 - sg)))
        dlg_ref[...] += jnp.sum(dln * xh, axis=0, keepdims=True)
        dlb_ref[...] += jnp.sum(dln, axis=0, keepdims=True)
        dxh = dln * lg
        dcd = rs * (dxh - jnp.mean(dxh, axis=-1, keepdims=True) - xh * jnp.mean(dxh * xh, axis=-1, keepdims=True))
        dcd_ref[...] = dcd
        for k in range(CONF_K):
            dccw_ref[k:k + 1, :] += jnp.sum(dcd * window(HALO - (CONF_K - 1) + k), axis=0, keepdims=True)

        o = ot_ref[...].reshape(D_GROUP, t).T
        do = rms_bwd(o, _cols(dyn_ref, 512, 768))
        _split_heads(do, dor_ref, dot_ref)
        prod = do.astype(BF16).astype(F32) * o
        for h in range(N_HEADS):
            ds_ref[h] = jnp.sum(prod[:, h * HEAD_DIM:(h + 1) * HEAD_DIM], axis=1, keepdims=True)

    small = [(1, D_GROUP), (N_HEADS, CHUNK, CHUNK), (CHUNK, D_GROUP), (8, D_GROUP), (32, D_GROUP), (1, D_GROUP), (1, D_GROUP)]
    rows, cols = _head_specs(t, s)
    out = pl.pallas_call(
        body, name="mixers_bwd_a", grid=(s // t,),
        in_specs=_mix_specs(t, s) + [pl.BlockSpec((t, D_MODEL), lambda i: (i, 0)),
                                     pl.BlockSpec((N_HEADS, HEAD_DIM, t), lambda i: (0, 0, i)),
                                     pl.BlockSpec((t, D_GROUP), lambda i: (i, 3)),
                                     _full((1, D_GROUP)), _full((N_HEADS, CHUNK, CHUNK)), _full((N_HEADS, CHUNK, CHUNK)),
                                     _full((CHUNK, D_GROUP)), _full((8, D_GROUP)), _full((32, D_GROUP)),
                                     _full((1, D_GROUP)), _full((1, D_GROUP))] + [_ANY] * ns,
        out_specs=[pl.BlockSpec((t, 768), lambda i: (i, 0)), pl.BlockSpec((t, D_GROUP), lambda i: (i, 0)),
                   pl.BlockSpec((t, D_GROUP), lambda i: (i, 0)), rows[0], cols[0],
                   pl.BlockSpec((N_HEADS, t, 1), lambda i: (0, i, 0))]
                  + [_full(sh) for sh in small] + [_ANY] * ns,
        out_shape=[jax.ShapeDtypeStruct((s, 768), BF16), jax.ShapeDtypeStruct((s, D_GROUP), F32),
                   jax.ShapeDtypeStruct((s, D_GROUP), F32), rows[1], cols[1],
                   jax.ShapeDtypeStruct((N_HEADS, s, 1), F32)]
                  + [jax.ShapeDtypeStruct(sh, F32) for sh in small]
                  + [jax.ShapeDtypeStruct(a.shape, a.dtype) for a in sums],
        scratch_shapes=[pltpu.VMEM((HALO + t, D_GROUP), F32), pltpu.VMEM((HALO + t, D_GROUP), F32),
                        pltpu.VMEM((7, HALO + t - 8, D_GROUP), F32)] + (_owner_sems(ns) if ns else []),
        compiler_params=_params(("arbitrary",), V7X_VMEM_LIMIT),
    )(z, z, dyn, o_t, y_abd, p["vg"], p["wt"], p["wtt"], p["bmat"], p["scw"], p["ccw"], p["lg"], p["lb"], *sums)
    return tuple(out[:13]) + (out[13:],) if ns else out


def _mixers_bwd_b(z, dza, dcb, dcd, dq_t, dk_t, dv_t, p):
    s = z.shape[0]
    t = _tile(s, MIX_ROWS)
    per = t // HALO
    n_halo = s // HALO

    def body(z_ref, dza_ref, dcb_ref, dcbn_ref, dcd_ref, dcdn_ref, dq_ref, dk_ref, dv_ref, scw_ref, ccw_ref,
             dz_ref, eb_ref, ed_ref, sh_ref):
        last = pl.program_id(0) == pl.num_programs(0) - 1
        dz_ref[:, 0:768] = dza_ref[...]
        eb_ref[0:t, :] = dcb_ref[...]
        eb_ref[t:t + HALO, :] = jnp.where(last, 0.0, dcbn_ref[...])
        du = jnp.zeros((t, D_GROUP), F32)
        for k in range(SHORT_K):
            du = du + scw_ref[k:k + 1, :] * eb_ref[pl.ds(SHORT_K - 1 - k, t), :]
        dz_ref[:, 768:1024] = (du * _cols(z_ref, 1024, 1280)).astype(BF16)
        dz_ref[:, 1024:1280] = (du * _cols(z_ref, 768, 1024)).astype(BF16)
        for n, r in enumerate((dq_ref, dk_ref, dv_ref)):
            dz_ref[:, 1280 + 256 * n:1536 + 256 * n] = r[...].reshape(D_GROUP, t).T.astype(BF16)
        ed_ref[0:t, :] = dcd_ref[...]
        ed_ref[t:t + HALO, :] = jnp.where(last, 0.0, dcdn_ref[...])
        window = _windows(ed_ref, sh_ref, t)
        dh = jnp.zeros((t, D_GROUP), F32)
        for k in range(CONF_K):
            dh = dh + ccw_ref[k:k + 1, :] * window(CONF_K - 1 - k)
        a = _cols(z_ref, 2048, 2304)
        sg = _sigmoid(_cols(z_ref, 2304, 2560))
        dz_ref[:, 2048:2304] = (dh * sg).astype(BF16)
        dz_ref[:, 2304:2560] = (dh * a * sg * (1.0 - sg)).astype(BF16)

    nxt = lambda i: (jnp.minimum((i + 1) * per, n_halo - 1), 0)
    tr = pl.BlockSpec((N_HEADS, HEAD_DIM, t), lambda i: (0, 0, i))
    return pl.pallas_call(
        body, name="mixers_bwd_b", grid=(s // t,),
        in_specs=[pl.BlockSpec((t, D_IN), lambda i: (i, 0)), pl.BlockSpec((t, 768), lambda i: (i, 0)),
                  pl.BlockSpec((t, D_GROUP), lambda i: (i, 0)), pl.BlockSpec((HALO, D_GROUP), nxt),
                  pl.BlockSpec((t, D_GROUP), lambda i: (i, 0)), pl.BlockSpec((HALO, D_GROUP), nxt),
                  tr, tr, tr, _full((8, D_GROUP)), _full((32, D_GROUP))],
        out_specs=pl.BlockSpec((t, D_IN), lambda i: (i, 0)),
        out_shape=jax.ShapeDtypeStruct((s, D_IN), BF16),
        scratch_shapes=[pltpu.VMEM((HALO + t, D_GROUP), F32), pltpu.VMEM((HALO + t, D_GROUP), F32),
                        pltpu.VMEM((7, HALO + t - 8, D_GROUP), F32)],
        compiler_params=_params(("parallel",), V7X_VMEM_LIMIT),
    )(z, dza, dcb, dcb, dcd, dcd, dq_t, dk_t, dv_t, p["scw"], p["ccw"])


def _split_bf16(v):
    hi = v.astype(BF16)
    return hi, (v - hi.astype(F32)).astype(BF16)


def _att_scores(qs, kts, carries, tri, mask):
    zs = [_dot(q, kt) for q, kt in zip(qs, kts)]
    lms, lbs, parts = [], [], []
    for z in zs:
        soft = jnp.log(1.0 + jnp.exp(-jnp.abs(z)))
        lm = -(jnp.maximum(z, 0.0) + soft)
        lbs.append(lm + z)
        if mask is not None:
            lm = jnp.where(mask, lm, 0.0)
        lms.append(lm)
        parts.append(_split_bf16(lm))
    rights = [_dot(hi, tri) + _dot(lo, tri) for hi, lo in parts]
    ws = []
    for lb, right, carry in zip(lbs, rights, carries):
        w = jnp.exp(lb + right + carry)
        ws.append(w if mask is None else jnp.where(mask, w, 0.0))
    return ws, lbs, [jnp.sum(lm, axis=1, keepdims=True) for lm in lms]


def _att_consts(i):
    j_hi = ((i + 1) * ATT_TQ - 1) // ATT_TK
    row = lax.broadcasted_iota(jnp.int32, (ATT_TQ, ATT_TK), 0) + i * ATT_TQ
    col = lax.broadcasted_iota(jnp.int32, (ATT_TQ, ATT_TK), 1) + j_hi * ATT_TK
    r_i = lax.broadcasted_iota(jnp.int32, (ATT_TK, ATT_TK), 0)
    c_i = lax.broadcasted_iota(jnp.int32, (ATT_TK, ATT_TK), 1)
    return j_hi, col < row, r_i, c_i


def _att_alive(j, carries):
    top = carries[0]
    for c in carries[1:]:
        top = jnp.maximum(top, c)
    return jnp.logical_and(j >= 0, jnp.max(top) > ATT_DEAD)


def _attn_fwd(q_r, k_t, v_t, bufs=(), parts=()):
    assert ATT_TQ == ATT_TK
    s = q_r.shape[1]
    nb = len(bufs)
    nbq = min(ATT_FWD_BLOCKS, s // ATT_TQ)
    step_rows = nbq * ATT_TQ
    lanes = [(h, b) for h in range(N_HEADS) for b in range(nbq)]

    def body(q_ref, kt_ref, vt_ref, *rest):
        o_ref = rest[nb]
        if nb:
            _gather_in_steps(rest[nb + 1:2 * nb + 1], parts, rest[2 * nb + 1:], s // step_rows)
        i = pl.program_id(0)
        _, mask, r_i, c_i = _att_consts(0)
        tri = (r_i > c_i).astype(BF16)
        rows_of = [pl.ds(b * ATT_TQ, ATT_TQ) for _, b in lanes]

        def tiles(t, carries, accs, mask):
            js = [nbq * i + b - t for _, b in lanes]
            live = [None if mask is not None or b == nbq - 1 else j >= 0 for (_, b), j in zip(lanes, js)]
            cols = [pl.ds(pl.multiple_of(jnp.maximum(j, 0) * ATT_TK, ATT_TK), ATT_TK) for j in js]
            ws, _, tots = _att_scores([q_ref[h, r, :] for (h, _), r in zip(lanes, rows_of)],
                                      [kt_ref[h, :, c] for (h, _), c in zip(lanes, cols)], carries, tri, mask)
            ws = [w if ok is None else jnp.where(ok, w, 0.0) for w, ok in zip(ws, live)]
            tots = [tot if ok is None else jnp.where(ok, tot, -1e30) for tot, ok in zip(tots, live)]
            accs = [acc + _dot_nt(vt_ref[h, :, c], w.astype(BF16)) for (h, _), c, acc, w in zip(lanes, cols, accs, ws)]
            return [c + tot for c, tot in zip(carries, tots)], accs

        state = tiles(0, [jnp.zeros((ATT_TQ, 1), F32)] * len(lanes), [jnp.zeros((HEAD_DIM, ATT_TQ), F32)] * len(lanes), mask)

        def cond(c):
            return _att_alive(nbq * i + nbq - 1 - c[0], c[1])

        def step(c):
            return (c[0] + 1,) + tuple(tiles(c[0], c[1], c[2], None))

        _, _, accs = lax.while_loop(cond, step, (1,) + tuple(state))
        for n, (h, _) in enumerate(lanes):
            o_ref[h, :, rows_of[n]] = accs[n]

    whole = pl.BlockSpec((N_HEADS, HEAD_DIM, s), lambda i: (0, 0, 0), pipeline_mode=pl.Buffered(1))
    out = pl.pallas_call(
        body, name="attn_fwd", grid=(s // step_rows,),
        in_specs=[pl.BlockSpec((N_HEADS, step_rows, HEAD_DIM), lambda i: (0, i, 0)), whole, whole] + [_ANY] * nb,
        out_specs=[pl.BlockSpec((N_HEADS, HEAD_DIM, step_rows), lambda i: (0, 0, i))] + [_ANY] * nb,
        out_shape=[jax.ShapeDtypeStruct((N_HEADS, HEAD_DIM, s), F32)] + [jax.ShapeDtypeStruct(b.shape, b.dtype) for b in bufs],
        input_output_aliases={3 + k: 1 + k for k in range(nb)},
        scratch_shapes=_gather_sems(parts) if nb else [],
        compiler_params=_params(("arbitrary",), V7X_VMEM_LIMIT),
    )(q_r, k_t, v_t, *bufs)
    return (out[0], out[1:]) if nb else out[0]


ATT_FWD_BLOCKS = 2
ATT_BWD_HEADS = 2
ATT_BWD_BLOCKS = 2


def _attn_bwd(q_r, q_t, k_t, v_t, do_r, do_t, dsum):
    assert ATT_TQ == ATT_TK
    s = q_r.shape[1]
    hps = ATT_BWD_HEADS
    nbq = min(ATT_BWD_BLOCKS, s // ATT_TQ)
    step_rows = nbq * ATT_TQ
    lanes = [(h, b) for h in range(hps) for b in range(nbq)]

    def body(q_ref, qt_ref, kt_ref, vt_ref, do_ref, dot_ref, ds_ref, dq_ref, dk_ref, dv_ref):
        i = pl.program_id(1)

        @pl.when(i == 0)
        def _():
            dk_ref[...] = jnp.zeros_like(dk_ref)
            dv_ref[...] = jnp.zeros_like(dv_ref)

        _, mask, r_i, c_i = _att_consts(0)
        tri_r = (r_i > c_i).astype(BF16)
        tri_ge = (r_i >= c_i).astype(BF16)
        rows_of = [pl.ds(b * ATT_TQ, ATT_TQ) for _, b in lanes]

        def tiles(t, carries, gsums, accs, mask):
            js = [nbq * i + b - t for _, b in lanes]
            live = [None if mask is not None or b == nbq - 1 else j >= 0 for (_, b), j in zip(lanes, js)]
            cols = [pl.ds(pl.multiple_of(jnp.maximum(j, 0) * ATT_TK, ATT_TK), ATT_TK) for j in js]
            kts = [kt_ref[h, :, c] for (h, _), c in zip(lanes, cols)]
            das = [_dot(do_ref[h, r, :], vt_ref[h, :, c]) for (h, _), r, c in zip(lanes, rows_of, cols)]
            ws, lbs, tots = _att_scores([q_ref[h, r, :] for (h, _), r in zip(lanes, rows_of)], kts, carries, tri_r, mask)
            ws = [w if ok is None else jnp.where(ok, w, 0.0) for w, ok in zip(ws, live)]
            tots = [tot if ok is None else jnp.where(ok, tot, -1e30) for tot, ok in zip(tots, live)]
            wbs = [w.astype(BF16) for w in ws]
            gs = [wb.astype(F32) * da for wb, da in zip(wbs, das)]
            parts = [_split_bf16(g) for g in gs]
            sfx = [_dot(hi, tri_ge) + _dot(lo, tri_ge) for hi, lo in parts]
            for n, (h, _) in enumerate(lanes):
                dv_ref[h, :, cols[n]] += _dot(dot_ref[h, :, rows_of[n]], wbs[n])
            dzs = []
            for n, (h, _) in enumerate(lanes):
                left = ds_ref[h, rows_of[n], :] - gsums[n] - sfx[n]
                dz = gs[n] - jnp.exp(lbs[n]) * (gs[n] + left)
                if mask is not None:
                    dz = jnp.where(mask, dz, 0.0)
                elif live[n] is not None:
                    dz = jnp.where(live[n], dz, 0.0)
                dzs.append(dz.astype(BF16))
            for n, (h, _) in enumerate(lanes):
                dk_ref[h, :, cols[n]] += _dot(qt_ref[h, :, rows_of[n]], dzs[n])
            accs = [acc + _dot_nt(kt, dz) for acc, kt, dz in zip(accs, kts, dzs)]
            gsums = [gsum + jnp.sum(g, axis=1, keepdims=True) for gsum, g in zip(gsums, gs)]
            return [c + tot for c, tot in zip(carries, tots)], gsums, accs

        col0 = [jnp.zeros((ATT_TQ, 1), F32)] * len(lanes)
        state = tiles(0, col0, col0, [jnp.zeros((HEAD_DIM, ATT_TQ), F32)] * len(lanes), mask)

        def cond(c):
            return _att_alive(nbq * i + nbq - 1 - c[0], c[1])

        def step(c):
            return (c[0] + 1,) + tuple(tiles(c[0], c[1], c[2], c[3], None))

        _, _, _, accs = lax.while_loop(cond, step, (1,) + tuple(state))
        for n, (h, _) in enumerate(lanes):
            dq_ref[h, :, rows_of[n]] = accs[n] * ATT_SCALE

    whole = pl.BlockSpec((hps, HEAD_DIM, s), lambda g, i: (g, 0, 0))
    whole_in = pl.BlockSpec((hps, HEAD_DIM, s), lambda g, i: (g, 0, 0), pipeline_mode=pl.Buffered(1))
    rows = pl.BlockSpec((hps, step_rows, HEAD_DIM), lambda g, i: (g, i, 0))
    cols = pl.BlockSpec((hps, HEAD_DIM, step_rows), lambda g, i: (g, 0, i))
    shape = jax.ShapeDtypeStruct((N_HEADS, HEAD_DIM, s), F32)
    return pl.pallas_call(
        body, name="attn_bwd", grid=(N_HEADS // hps, s // step_rows),
        in_specs=[rows, cols, whole_in, whole_in, rows, cols, pl.BlockSpec((hps, step_rows, 1), lambda g, i: (g, i, 0))],
        out_specs=[cols, whole, whole],
        out_shape=[shape, shape, shape],
        compiler_params=_params(("parallel", "arbitrary"), V7X_VMEM_LIMIT),
    )(q_r, q_t, k_t, v_t, do_r, do_t, dsum)


def _out_proj(x, y_abd, o_t, gain, w, layer):
    s = x.shape[0]
    t = _tile(s, 1024)

    def body(x_ref, y_ref, ot_ref, g_ref, w_ref, x1_ref, yn_ref):
        o = ot_ref[...].reshape(D_GROUP, t).T
        groups = [y_ref[:, 0:256], y_ref[:, 256:512], o * _rsqrt_mean(o), y_ref[:, 512:768]]
        g = g_ref[...]
        acc = None
        for b, yn in enumerate(groups):
            cols = slice(256 * b, 256 * (b + 1))
            yn_ref[:, cols] = yn.astype(BF16)
            part = _dot((yn * g[:, cols]).astype(BF16), w_ref[b])
            acc = part if acc is None else acc + part
        x1_ref[...] = x_ref[...] + acc

    return pl.pallas_call(
        body, name="out_proj", grid=(s // t,),
        in_specs=[pl.BlockSpec((t, D_MODEL), lambda i: (i, 0)), pl.BlockSpec((t, 768), lambda i: (i, 0)),
                  pl.BlockSpec((N_HEADS, HEAD_DIM, t), lambda i: (0, 0, i)), _full((1, D_MODEL)),
                  pl.BlockSpec((N_BLK, None, D_GROUP, D_MODEL), lambda i: (0, layer, 0, 0))],
        out_specs=[pl.BlockSpec((t, D_MODEL), lambda i: (i, 0)), pl.BlockSpec((t, D_MODEL), lambda i: (i, 0))],
        out_shape=[jax.ShapeDtypeStruct((s, D_MODEL), F32), jax.ShapeDtypeStruct((s, D_MODEL), BF16)],
        compiler_params=_params(("parallel",), V7X_VMEM_LIMIT),
    )(x, y_abd, o_t, gain, w)


def _out_proj_bwd(x1, g_ffn, dh, dx2, yn, gain, w, layer, halves=()):
    s = dx2.shape[0]
    t = _tile(s, 1024)
    nh = len(halves)

    def body(x_ref, gf_ref, dh_ref, dx2_ref, yn_ref, g_ref, w_ref, *rest):
        dx1_ref, dgf_ref, dyn_ref, dg_ref, dw_ref = rest[nh:nh + 5]
        if nh:
            start, finish = _halves_stages(rest[:nh], rest[nh + 5:2 * nh + 5], rest[2 * nh + 5:])
            pl.when(pl.program_id(0) == 0)(start)
            pl.when(pl.program_id(0) == s // t - 1)(finish)

        @pl.when(pl.program_id(0) == 0)
        def _():
            dg_ref[...] = jnp.zeros_like(dg_ref)
            dw_ref[...] = jnp.zeros_like(dw_ref)
            dgf_ref[...] = jnp.zeros_like(dgf_ref)

        dx1, dgf = _rms_bwd_rows(x_ref[...], gf_ref[...], dh_ref[...], dx2_ref[...])
        dx1_ref[...] = dx1
        dgf_ref[...] += dgf
        dxb = dx1.astype(BF16)
        g = g_ref[...]
        yn = yn_ref[...].astype(F32)
        yg = (yn * g).astype(BF16)
        for b in range(N_BLK):
            cols = slice(256 * b, 256 * (b + 1))
            dyg = _dot_nt(dxb, w_ref[b])
            dw_ref[b] += _dot_tn(yg[:, cols], dxb)
            dg_ref[:, cols] += jnp.sum(dyg * yn[:, cols], axis=0, keepdims=True)
            dyn_ref[:, cols] = (dyg * g[:, cols]).astype(BF16)

    row = pl.BlockSpec((t, D_MODEL), lambda i: (i, 0))
    vec = _full((1, D_MODEL))
    out = pl.pallas_call(
        body, name="out_proj_bwd", grid=(s // t,),
        in_specs=[row, vec, row, row, row, vec, pl.BlockSpec((N_BLK, None, D_GROUP, D_MODEL), lambda i: (0, layer, 0, 0))]
                 + [_ANY] * nh,
        out_specs=[row, vec, row, vec, _full((N_BLK, D_GROUP, D_MODEL))] + [_ANY] * nh,
        out_shape=[jax.ShapeDtypeStruct((s, D_MODEL), F32), jax.ShapeDtypeStruct((1, D_MODEL), F32),
                   jax.ShapeDtypeStruct((s, D_MODEL), BF16), jax.ShapeDtypeStruct((1, D_MODEL), F32),
                   jax.ShapeDtypeStruct((N_BLK, D_GROUP, D_MODEL), F32)] + _halves_shapes(halves),
        scratch_shapes=_halves_sems(nh) if nh else [],
        compiler_params=_params(("arbitrary",), V7X_VMEM_LIMIT),
    )(x1, g_ffn, dh, dx2, yn, gain, w, *halves)
    return tuple(out[:5]) + (out[5:],) if nh else out


def _ffn(x, g, w_up, w_down, layer):
    s = x.shape[0]
    t = _tile(s, 1024)

    def body(x_ref, g_ref, wu_ref, wd_ref, x2_ref, p_ref, h_ref):
        @pl.when(pl.program_id(1) == 0)
        def _():
            xv = x_ref[...]
            h_ref[...] = (xv * _rsqrt_mean(xv) * g_ref[...]).astype(BF16)
            x2_ref[...] = xv

        pre = _dot(h_ref[...], wu_ref[...])
        p_ref[...] = pre.astype(BF16)
        a = jnp.maximum(pre, 0.0)
        x2_ref[...] += _dot((a * a).astype(BF16), wd_ref[...])

    wspec = pl.BlockSpec((None, None, D_MODEL, D_FF_BLK), lambda i, j: (j, layer, 0, 0))
    row = pl.BlockSpec((t, D_MODEL), lambda i, j: (i, 0))
    return pl.pallas_call(
        body, name="ffn", grid=(s // t, N_BLK),
        in_specs=[row, pl.BlockSpec((1, D_MODEL), lambda i, j: (0, 0)), wspec, wspec],
        out_specs=[row, pl.BlockSpec((t, D_FF_BLK), lambda i, j: (i, j)), row],
        out_shape=[jax.ShapeDtypeStruct((s, D_MODEL), F32), jax.ShapeDtypeStruct((s, N_BLK * D_FF_BLK), BF16),
                   jax.ShapeDtypeStruct((s, D_MODEL), BF16)],
        compiler_params=_params(("parallel", "arbitrary"), V7X_VMEM_LIMIT),
    )(x, g, w_up, w_down)


def _loss_head(x, g, target):
    s = x.shape[0]
    t = _tile(s, 512)

    def body(x_ref, g_ref, t_ref, l_ref, dx_ref, dxb_ref, dg_ref):
        @pl.when(pl.program_id(0) == 0)
        def _():
            l_ref[...] = jnp.zeros_like(l_ref)
            dg_ref[...] = jnp.zeros_like(dg_ref)

        xv = x_ref[...]
        g = g_ref[...]
        r = _rsqrt_mean(xv)
        xh = xv * r
        err = xh * g - t_ref[...]
        l_ref[...] += 0.5 * jnp.sum(jnp.mean(err * err, axis=-1, keepdims=True), axis=0, keepdims=True)
        dy = err * (1.0 / D_MODEL)
        dg_ref[...] += jnp.sum(dy * xh, axis=0, keepdims=True)
        dxh = dy * g
        dx = r * (dxh - xh * jnp.mean(dxh * xh, axis=-1, keepdims=True))
        dx_ref[...] = dx
        dxb_ref[...] = dx.astype(BF16)

    row = pl.BlockSpec((t, D_MODEL), lambda i: (i, 0))
    return pl.pallas_call(
        body, name="loss_head", grid=(s // t,),
        in_specs=[row, _full((1, D_MODEL)), row],
        out_specs=[_full((1, 128)), row, row, _full((1, D_MODEL))],
        out_shape=[jax.ShapeDtypeStruct((1, 128), F32), jax.ShapeDtypeStruct((s, D_MODEL), F32),
                   jax.ShapeDtypeStruct((s, D_MODEL), BF16), jax.ShapeDtypeStruct((1, D_MODEL), F32)],
        compiler_params=_params(("arbitrary",)),
    )(x, g, target)


def _rms_bwd_rows(xv, g, dh, dres):
    r = _rsqrt_mean(xv)
    xh = xv * r
    dxh = dh * g
    dx = dres + r * (dxh - xh * jnp.mean(dxh * xh, axis=-1, keepdims=True))
    return dx, jnp.sum(dh * xh, axis=0, keepdims=True)


def _ffn_bwd(dxb, p, w_up, w_down, layer, sums=()):
    s = dxb.shape[0]
    t = _tile(s, 1024)
    ns = len(sums)

    def body(dx_ref, p_ref, wu_ref, wd_ref, *rest):
        dp_ref, dh_ref = rest[ns:ns + 2]
        if ns:
            i, j = pl.program_id(0), pl.program_id(1)
            _owners_in_steps(rest[:ns], rest[ns + 2:2 * ns + 2], rest[2 * ns + 2:],
                             jnp.logical_and(i == 0, j == 0), jnp.logical_and(i == s // t - 1, j == N_BLK - 1))
        da = _dot_nt(dx_ref[...], wd_ref[...])
        a = jnp.maximum(p_ref[...].astype(F32), 0.0)
        dp = (da * (2.0 * a)).astype(BF16)
        dp_ref[...] = dp
        dh = _dot_nt(dp, wu_ref[...])

        @pl.when(pl.program_id(1) == 0)
        def _():
            dh_ref[...] = dh

        @pl.when(pl.program_id(1) != 0)
        def _():
            dh_ref[...] += dh

    wspec = pl.BlockSpec((None, None, D_MODEL, D_FF_BLK), lambda i, j: (j, layer, 0, 0))
    row = pl.BlockSpec((t, D_MODEL), lambda i, j: (i, 0))
    blk = pl.BlockSpec((t, D_FF_BLK), lambda i, j: (i, j))
    out = pl.pallas_call(
        body, name="ffn_bwd", grid=(s // t, N_BLK),
        in_specs=[row, blk, wspec, wspec] + [_ANY] * ns, out_specs=[blk, row] + [_ANY] * ns,
        out_shape=[jax.ShapeDtypeStruct((s, N_BLK * D_FF_BLK), BF16), jax.ShapeDtypeStruct((s, D_MODEL), F32)]
                  + [jax.ShapeDtypeStruct(a.shape, a.dtype) for a in sums],
        scratch_shapes=_owner_sems(ns) if ns else [],
        compiler_params=_params(("arbitrary" if ns else "parallel", "arbitrary"), V7X_VMEM_LIMIT),
    )(dxb, p, w_up, w_down, *sums)
    return (out[0], out[1], out[2:]) if ns else out


def _ffn_wgrad(hb, p, dp, dxb):
    s = hb.shape[0]
    t = _tile(s, 2048)

    def body(h_ref, p_ref, dp_ref, dx_ref, du_ref, dd_ref):
        @pl.when(pl.program_id(1) == 0)
        def _():
            du_ref[...] = jnp.zeros_like(du_ref)
            dd_ref[...] = jnp.zeros_like(dd_ref)

        a = jnp.maximum(p_ref[...].astype(F32), 0.0)
        du_ref[...] += _dot_tn(h_ref[...], dp_ref[...])
        dd_ref[...] += _dot_tn((a * a).astype(BF16), dx_ref[...])

    row = pl.BlockSpec((t, D_MODEL), lambda j, i: (i, 0))
    blk = pl.BlockSpec((t, D_FF_BLK), lambda j, i: (i, j))
    out = pl.BlockSpec((None, D_MODEL, D_FF_BLK), lambda j, i: (j, 0, 0))
    shape = jax.ShapeDtypeStruct((N_BLK, D_MODEL, D_FF_BLK), F32)
    return pl.pallas_call(
        body, name="ffn_wgrad", grid=(N_BLK, s // t),
        in_specs=[row, blk, blk, row], out_specs=[out, out], out_shape=[shape, shape],
        compiler_params=_params(("parallel", "arbitrary"), V7X_VMEM_LIMIT),
    )(hb, p, dp, dxb)


def _in_proj_bwd(x, g, dx1, dz, w, layer):
    s = x.shape[0]
    t = _tile(s, 512)

    def body(x_ref, g_ref, dx1_ref, dz_ref, w_ref, dx0_ref, dxb_ref, dg_ref, wide_ref):
        _pair_blocks(w_ref, wide_ref)

        @pl.when(pl.program_id(0) == 0)
        def _():
            dg_ref[...] = jnp.zeros_like(dg_ref)

        dh = _dot_nt(dz_ref[:, 0:2 * W_IN_BLK], wide_ref[0])
        for n in range(1, N_BLK // 2):
            dh = dh + _dot_nt(dz_ref[:, 2 * n * W_IN_BLK:2 * (n + 1) * W_IN_BLK], wide_ref[n])
        dx, dg = _rms_bwd_rows(x_ref[...], g_ref[...], dh, dx1_ref[...])
        dx0_ref[...] = dx
        dxb_ref[...] = dx.astype(BF16)
        dg_ref[...] += dg

    row = pl.BlockSpec((t, D_MODEL), lambda i: (i, 0))
    return pl.pallas_call(
        body, name="in_proj_bwd", grid=(s // t,),
        in_specs=[row, _full((1, D_MODEL)), row, pl.BlockSpec((t, D_IN), lambda i: (i, 0)),
                  pl.BlockSpec((N_BLK, None, D_MODEL, W_IN_BLK), lambda i: (0, layer, 0, 0))],
        out_specs=[row, row, _full((1, D_MODEL))],
        out_shape=[jax.ShapeDtypeStruct((s, D_MODEL), F32), jax.ShapeDtypeStruct((s, D_MODEL), BF16),
                   jax.ShapeDtypeStruct((1, D_MODEL), F32)],
        scratch_shapes=[pltpu.VMEM((N_BLK // 2, D_MODEL, 2 * W_IN_BLK), BF16)],
        compiler_params=_params(("arbitrary",), V7X_VMEM_LIMIT),
    )(x, g, dx1, dz, w)


def _in_proj_wgrad(hb, dz):
    s = hb.shape[0]
    t = _tile(s, 1024)

    def body(h_ref, dz_ref, dw_ref, wide_ref):
        @pl.when(pl.program_id(0) == 0)
        def _():
            wide_ref[...] = jnp.zeros_like(wide_ref)

        h = h_ref[...]
        for n in range(N_BLK // 2):
            wide_ref[n] += _dot_tn(h, dz_ref[:, 2 * n * W_IN_BLK:2 * (n + 1) * W_IN_BLK])

        @pl.when(pl.program_id(0) == s // t - 1)
        def _():
            for b in range(N_BLK):
                dw_ref[b] = wide_ref[b // 2, :, (b % 2) * W_IN_BLK:(b % 2 + 1) * W_IN_BLK]

    return pl.pallas_call(
        body, name="in_proj_wgrad", grid=(s // t,),
        in_specs=[pl.BlockSpec((t, D_MODEL), lambda i: (i, 0)), pl.BlockSpec((t, D_IN), lambda i: (i, 0))],
        out_specs=_full((N_BLK, D_MODEL, W_IN_BLK)),
        out_shape=jax.ShapeDtypeStruct((N_BLK, D_MODEL, W_IN_BLK), F32),
        scratch_shapes=[pltpu.VMEM((N_BLK // 2, D_MODEL, 2 * W_IN_BLK), F32)],
        compiler_params=_params(("arbitrary",), V7X_VMEM_LIMIT),
    )(hb, dz)


def _layer_params(small, layer):
    tril = jnp.tril(jnp.ones((CHUNK, CHUNK), bool))
    ws = jnp.where(tril, small["gmlp_w_s"][layer], 0.0)
    bmat = jnp.repeat(small["gmlp_b_s"][layer].T, HEAD_DIM, axis=1)
    scw = jnp.zeros((8, D_GROUP), F32).at[:SHORT_K].set(small["short_conv_w"][layer])
    ccw = jnp.zeros((32, D_GROUP), F32).at[:CONF_K].set(small["conf_conv_w"][layer])
    return dict(vg=small["gmlp_v_g"][layer][None], wt=ws.astype(BF16), wtt=jnp.swapaxes(ws, 1, 2).astype(BF16),
                bmat=bmat, scw=scw, ccw=ccw, lg=small["conf_ln_g"][layer][None], lb=small["conf_ln_b"][layer][None])


def _local_step(x, target, big, small, gather_pending=False, core=None):
    saved = []
    for l in range(DEPTH):
        p = _layer_params(small, l)
        z, hb, q_r, q_t, k_t, v_t = _in_proj(x, small["norm_mix_g"][l][None], big["w_in"], l)
        if gather_pending and l == 0:
            late = ("w_out", "w_up", "w_down")
            y_abd, filled = _mixers_fwd(z, p, [big[k] for k in late], [(n, 0) for n in range(len(late))])
            big = {**big, **dict(zip(late, filled))}
            o_t, filled = _attn_fwd(q_r, k_t, v_t, [big[k] for k in _BIG], [(n, 1) for n in range(len(_BIG))])
            big = dict(zip(_BIG, filled))
        else:
            y_abd = _mixers_fwd(z, p)
            o_t = _attn_fwd(q_r, k_t, v_t)
        x1, yn = _out_proj(x, y_abd, o_t, small["mix_out_g"][l][None], big["w_out"], l)
        x2, pre, h2b = _ffn(x1, small["norm_ffn_g"][l][None], big["w_up"], big["w_down"], l)
        saved.append(dict(p=p, x0=x, z=z, hb=hb, q_r=q_r, q_t=q_t, k_t=k_t, v_t=v_t, o_t=o_t, x1=x1, yn=yn, pre=pre,
                          h2b=h2b, y_abd=y_abd))
        x = x2

    loss, dx, dxb, d_final = _loss_head(x, small["final_norm_g"][None], target)

    g = {k: [None] * DEPTH for k in ("w_in", "w_out", "w_up", "w_down", "norm_mix_g", "gmlp_v_g", "gmlp_w_s", "gmlp_b_s",
                                     "short_conv_w", "conf_conv_w", "conf_ln_g", "conf_ln_b", "mix_out_g", "norm_ffn_g")}
    tril = jnp.tril(jnp.ones((CHUNK, CHUNK), bool))
    early = {}
    ffn_sums = [[]] * DEPTH
    for l in reversed(range(DEPTH)):
        sv = saved[l]
        p = sv["p"]
        riding, sums = [], []
        if core is not None and l == 0:
            riding = [(k, 1) for k in _BIG]
            sums = _chip_sums(core, [g["w_in"][1], g["w_out"][1]]) + ffn_sums[1]
        dpre, dh, *got = _ffn_bwd(dxb, sv["pre"], big["w_up"], big["w_down"], l, sums)
        early.update(zip(riding, zip(sums, *got)))
        g["w_up"][l], g["w_down"][l] = _ffn_wgrad(sv["h2b"], sv["pre"], dpre, dxb)
        halves = [g["w_up"][l], g["w_down"][l]] if core is not None else []
        dx1, g["norm_ffn_g"][l], dyn, g["mix_out_g"][l], g["w_out"][l], *swapped = _out_proj_bwd(
            sv["x1"], small["norm_ffn_g"][l][None], dh, dx, sv["yn"], small["mix_out_g"][l][None], big["w_out"], l,
            halves)
        if halves:
            ffn_sums[l] = [_add_pairs(core, a, b) for a, b in zip(halves, *swapped)]
        riding, sums = [], []
        if core is not None and l == 0:
            riding, sums = [("w_up", 0), ("w_down", 0)], ffn_sums[0]
        (dza, dcb, dcd, do_r, do_t, dsum, dvg, dws, dbm, dscw, dccw, dlg, dlb, *got) = _mixers_bwd_a(
            sv["z"], dyn, sv["o_t"], sv["y_abd"], p, sums)
        early.update(zip(riding, zip(sums, *got)))
        dq_t, dk_t, dv_t = _attn_bwd(sv["q_r"], sv["q_t"], sv["k_t"], sv["v_t"], do_r, do_t, dsum)
        dz = _mixers_bwd_b(sv["z"], dza, dcb, dcd, dq_t, dk_t, dv_t, p)
        dx, dxb, g["norm_mix_g"][l] = _in_proj_bwd(sv["x0"], small["norm_mix_g"][l][None], dx1, dz, big["w_in"], l)
        g["w_in"][l] = _in_proj_wgrad(sv["hb"], dz)
        g["gmlp_v_g"][l] = dvg[0]
        g["gmlp_w_s"][l] = jnp.where(tril, dws, 0.0)
        g["gmlp_b_s"][l] = dbm.reshape(CHUNK, N_HEADS, HEAD_DIM).sum(-1).T
        g["short_conv_w"][l] = dscw[:SHORT_K]
        g["conf_conv_w"][l] = dccw[:CONF_K]
        g["conf_ln_g"][l] = dlg[0]
        g["conf_ln_b"][l] = dlb[0]
        g["norm_mix_g"][l] = g["norm_mix_g"][l][0]
        g["mix_out_g"][l] = g["mix_out_g"][l][0]
        g["norm_ffn_g"][l] = g["norm_ffn_g"][l][0]
    grads = {k: v if k in ("w_in", "w_out", "w_up", "w_down") else jnp.stack(v) for k, v in g.items()}
    grads["final_norm_g"] = d_final[0]
    return loss, dx, grads, early


_ANY = pl.BlockSpec(memory_space=pl.ANY)


def _mesh_place():
    x, y, c = lax.axis_index("x"), lax.axis_index("y"), lax.axis_index("c")
    chips = [(1 - x, y), (x, 1 - y), (1 - x, 1 - y)]
    return x, y, c, 2 * x + y, chips


def _gather_stages(bufs, parts, sems):
    ici_send, ici_recv, d2d_send, d2d_recv = sems
    x, y, c, me, chips = _mesh_place()
    blk = [2 * chip[0] + chip[1] for chip in chips]
    pairs = [(p, r) for p in range(len(parts)) for r in range(3)]

    def rows(p, block, half_of):
        k, layer = parts[p]
        half = bufs[k].shape[2] // 2
        return bufs[k].at[block, layer, pl.ds(half_of * half, half), :]

    def ici(p, r, block):
        return pltpu.make_async_remote_copy(
            src_ref=rows(p, me, c), dst_ref=rows(p, block, c), send_sem=ici_send.at[3 * p + r],
            recv_sem=ici_recv.at[3 * p + r], device_id=(chips[r][0], chips[r][1], c), device_id_type=MESH)

    def d2d(p, r, half_of):
        part = rows(p, blk[r], half_of)
        return pltpu.make_async_remote_copy(
            src_ref=part, dst_ref=part, send_sem=d2d_send.at[3 * p + r], recv_sem=d2d_recv.at[3 * p + r],
            device_id=(x, y, 1 - c), device_id_type=MESH)

    def start():
        for p, r in pairs:
            ici(p, r, me).start()

    def forward(p):
        for r in range(3):
            ici(p, r, blk[r]).wait_recv()
            d2d(p, r, c).start()

    def finish():
        for p, r in pairs:
            d2d(p, r, 1 - c).wait_recv()
        for p, r in pairs:
            ici(p, r, me).wait_send()
            d2d(p, r, c).wait_send()

    return start, forward, finish


def _gather_sems(parts):
    return [pltpu.SemaphoreType.DMA((3 * len(parts),)) for _ in range(4)]


def _gather_in_steps(bufs, parts, sems, n_steps):
    start, forward, finish = _gather_stages(bufs, parts, sems)
    i = pl.program_id(0)
    pl.when(i == 0)(start)
    for p in range(len(parts)):
        pl.when(i == n_steps * (2 * p + 3) // (2 * len(parts) + 2))(lambda p=p: forward(p))
    pl.when(i == n_steps - 1)(finish)


def _gather_first(bufs, parts, whole):
    n, m = len(bufs), len(whole)

    def body(*refs):
        whole_in, buf_out, whole_out = refs[n:n + m], refs[n + m:2 * n + m], refs[2 * n + m:2 * (n + m)]
        sems = refs[2 * (n + m):]
        send_sems, recv_sems, local_sems = sems[4:]
        x, y, c, me, chips = _mesh_place()
        start, forward, finish = _gather_stages(buf_out, parts, sems[:4])

        def push(k, r, block):
            return pltpu.make_async_remote_copy(
                src_ref=whole_in[k], dst_ref=whole_out[k].at[block], send_sem=send_sems.at[3 * k + r],
                recv_sem=recv_sems.at[3 * k + r], device_id=(chips[r][0], chips[r][1], c), device_id_type=MESH)

        local = [pltpu.make_async_copy(whole_in[k], whole_out[k].at[me], local_sems.at[k]) for k in range(m)]
        for cp in local:
            cp.start()
        start()
        for k in range(m):
            for r in range(3):
                push(k, r, me).start()
        for p in range(len(parts)):
            forward(p)
        for k in range(m):
            for r, chip in enumerate(chips):
                push(k, r, 2 * chip[0] + chip[1]).wait_recv()
        for k in range(m):
            for r in range(3):
                push(k, r, me).wait_send()
        finish()
        for cp in local:
            cp.wait()

    return pl.pallas_call(
        body, name="gather_first",
        in_specs=[_ANY] * (n + m), out_specs=[_ANY] * (n + m),
        out_shape=[jax.ShapeDtypeStruct(b.shape, b.dtype) for b in bufs]
                  + [jax.ShapeDtypeStruct((N_BLK,) + b.shape, b.dtype) for b in whole],
        input_output_aliases={k: k for k in range(n)},
        scratch_shapes=_gather_sems(parts) + [pltpu.SemaphoreType.DMA((3 * m,)), pltpu.SemaphoreType.DMA((3 * m,)),
                                              pltpu.SemaphoreType.DMA((m,))],
    )(*bufs, *whole)


def _swap_halves(gs):
    n = len(gs)

    def body(*refs):
        start, finish = _halves_stages(refs[:n], refs[n:2 * n], refs[2 * n:])
        start()
        finish()

    return pl.pallas_call(
        body, name="swap_halves", in_specs=[_ANY] * n, out_specs=[_ANY] * n,
        out_shape=_halves_shapes(gs), scratch_shapes=_halves_sems(n),
    )(*gs)


def _halves_stages(ins, outs, sems):
    send_sems, recv_sems = sems
    x, y, c, _, _ = _mesh_place()

    def copy(k):
        half = ins[k].shape[1] // 2
        return pltpu.make_async_remote_copy(
            src_ref=ins[k].at[:, pl.ds((1 - c) * half, half), :], dst_ref=outs[k],
            send_sem=send_sems.at[k], recv_sem=recv_sems.at[k], device_id=(x, y, 1 - c), device_id_type=MESH)

    def start():
        for k in range(len(ins)):
            copy(k).start()

    def finish():
        for k in range(len(ins)):
            copy(k).wait()

    return start, finish


def _halves_shapes(gs):
    return [jax.ShapeDtypeStruct((g.shape[0], g.shape[1] // 2, g.shape[2]), F32) for g in gs]


def _halves_sems(n):
    return [pltpu.SemaphoreType.DMA((n,)), pltpu.SemaphoreType.DMA((n,))]


def _owner_stages(ins, outs, sems):
    send_sems, recv_sems = sems
    x, y, c, me, chips = _mesh_place()
    pairs = [(k, r) for k in range(len(ins)) for r in range(3)]

    def remote(k, r, src_block, dst_block):
        return pltpu.make_async_remote_copy(
            src_ref=ins[k].at[src_block], dst_ref=outs[k].at[dst_block], send_sem=send_sems.at[3 * k + r],
            recv_sem=recv_sems.at[3 * k + r], device_id=(chips[r][0], chips[r][1], c), device_id_type=MESH)

    def start():
        for k, r in pairs:
            remote(k, r, 2 * chips[r][0] + chips[r][1], me).start()

    def finish():
        for k, r in pairs:
            remote(k, r, me, 2 * chips[r][0] + chips[r][1]).wait_recv()
        for k, r in pairs:
            remote(k, r, 2 * chips[r][0] + chips[r][1], me).wait_send()

    return start, finish


def _owner_sems(n):
    return [pltpu.SemaphoreType.DMA((3 * n,)), pltpu.SemaphoreType.DMA((3 * n,))]


def _owners_in_steps(ins, outs, sems, first, last):
    start, finish = _owner_stages(ins, outs, sems)
    pl.when(first)(start)
    pl.when(last)(finish)


def _send_to_owners(sums):
    n = len(sums)

    def body(*refs):
        start, finish = _owner_stages(refs[:n], refs[n:2 * n], refs[2 * n:])
        start()
        finish()

    return pl.pallas_call(
        body, name="send_to_owners", in_specs=[_ANY] * n, out_specs=[_ANY] * n,
        out_shape=[jax.ShapeDtypeStruct(s.shape, s.dtype) for s in sums],
        scratch_shapes=_owner_sems(n),
    )(*sums)


def _swap_reduced(fs):
    n = len(fs)

    def body(*refs):
        ins, outs, (send_sems, recv_sems) = refs[:n], refs[n:2 * n], refs[2 * n:]
        x, y, c, _, _ = _mesh_place()
        cps = [pltpu.make_async_remote_copy(src_ref=ins[k], dst_ref=outs[k], send_sem=send_sems.at[k],
                                            recv_sem=recv_sems.at[k], device_id=(x, y, 1 - c), device_id_type=MESH)
               for k in range(n)]
        for cp in cps:
            cp.start()
        for cp in cps:
            cp.wait()

    return pl.pallas_call(
        body, name="swap_reduced", in_specs=[_ANY] * n, out_specs=[_ANY] * n,
        out_shape=[jax.ShapeDtypeStruct(f.shape, F32) for f in fs],
        scratch_shapes=[pltpu.SemaphoreType.DMA((n,)), pltpu.SemaphoreType.DMA((n,))],
    )(*fs)


def _row_tile(rows):
    return min(rows, 256)


def _chip_sums(core, grads):
    if not grads:
        return []
    return [_add_pairs(core, a, b) for a, b in zip(grads, _swap_halves(grads))]


def _add_pairs(core, g, other):
    n, half, cols = other.shape
    t = _row_tile(half)
    per_half = half // t

    def body(c_ref, a_ref, b_ref, o_ref):
        o_ref[...] = (a_ref[...] + b_ref[...]).astype(BF16)

    spec = pl.BlockSpec((None, t, cols), lambda i, j, c_ref: (i, j, 0))
    return pl.pallas_call(
        body, name="add_pairs",
        grid_spec=pltpu.PrefetchScalarGridSpec(
            num_scalar_prefetch=1, grid=(n, per_half),
            in_specs=[pl.BlockSpec((None, t, cols), lambda i, j, c_ref: (i, c_ref[0] * per_half + j, 0)), spec],
            out_specs=spec),
        out_shape=jax.ShapeDtypeStruct(other.shape, BF16), compiler_params=_params(("parallel", "parallel")),
    )(core, g, other)


def _add_chips(me, s1, r2):
    _, r, cols = r2.shape
    t = _row_tile(r)

    def body(me_ref, s_ref, r_ref, o_ref):
        own = s_ref[...].astype(F32)
        parts = [jnp.where(me_ref[0] == k, own, r_ref[k].astype(F32)) for k in range(N_BLK)]
        o_ref[...] = ((parts[0] + parts[1]) + parts[2]) + parts[3]

    return pl.pallas_call(
        body, name="add_chips",
        grid_spec=pltpu.PrefetchScalarGridSpec(
            num_scalar_prefetch=1, grid=(r // t,),
            in_specs=[pl.BlockSpec((None, t, cols), lambda i, me_ref: (me_ref[0], i, 0)),
                      pl.BlockSpec((N_BLK, t, cols), lambda i, me_ref: (0, i, 0))],
            out_specs=pl.BlockSpec((t, cols), lambda i, me_ref: (i, 0))),
        out_shape=jax.ShapeDtypeStruct((r, cols), F32), compiler_params=_params(("parallel",)),
    )(me, s1, r2)


def _adamw(core, mine, other, w, m, v, layer, earlier=None):
    half, cols = mine.shape
    t = _row_tile(half)
    per_half = half // t
    c1 = 1.0 - ADAM_B1 ** ADAM_STEP
    c2 = 1.0 - ADAM_B2 ** ADAM_STEP

    def body(c_ref, a_ref, b_ref, w_ref, m_ref, v_ref, *rest):
        g_ref, d_ref, mo_ref, vo_ref = rest[-4:]
        gv = jnp.where(pl.program_id(0) // per_half == c_ref[0], a_ref[...], b_ref[...])
        g_ref[...] = gv
        m_new = ADAM_B1 * m_ref[...] + (1.0 - ADAM_B1) * gv
        v_new = ADAM_B2 * v_ref[...] + (1.0 - ADAM_B2) * (gv * gv)
        mo_ref[...] = m_new
        vo_ref[...] = v_new
        d_ref[...] = -ADAM_LR * ((m_new / c1) / (jnp.sqrt(v_new / c2) + ADAM_EPS) + ADAM_WD * w_ref[...])

    part = pl.BlockSpec((t, cols), lambda i, c_ref: (i % per_half, 0))
    spec = pl.BlockSpec((None, t, cols), lambda i, c_ref: (layer, i, 0))
    kept = [] if earlier is None else list(earlier)
    return pl.pallas_call(
        body, name="adamw",
        grid_spec=pltpu.PrefetchScalarGridSpec(
            num_scalar_prefetch=1, grid=(2 * per_half,),
            in_specs=[part, part, spec, spec, spec] + [_ANY] * len(kept), out_specs=[spec] * 4),
        out_shape=[jax.ShapeDtypeStruct(w.shape, F32)] * 4,
        input_output_aliases={6 + k: k for k in range(len(kept))},
        compiler_params=_params(("parallel",)),
    )(core, mine, other, w, m, v, *kept)


_REPLICATED = ("norm_mix_g", "gmlp_v_g", "gmlp_w_s", "gmlp_b_s", "conf_ln_g", "conf_ln_b", "mix_out_g", "norm_ffn_g",
               "final_norm_g")
_REP_SHAPES = {"norm_mix_g": (DEPTH, D_MODEL), "gmlp_v_g": (DEPTH, D_GROUP), "gmlp_w_s": (DEPTH, N_HEADS, CHUNK, CHUNK),
               "gmlp_b_s": (DEPTH, N_HEADS, CHUNK), "conf_ln_g": (DEPTH, D_GROUP), "conf_ln_b": (DEPTH, D_GROUP),
               "mix_out_g": (DEPTH, D_MODEL), "norm_ffn_g": (DEPTH, D_MODEL), "final_norm_g": (D_MODEL,)}
_BIG = ("w_in", "w_out", "w_up", "w_down")
_CONV_ROWS = 8
_REP_ROWS = 144
_SMALL_ROWS = 160
_CH_BLK = D_GROUP // N_BLK


def _pad_rows(flat, rows):
    pad = rows * D_MODEL - flat.shape[-1]
    flat = jnp.pad(flat, [(0, 0)] * (flat.ndim - 1) + [(0, pad)])
    return flat.reshape(flat.shape[:-1] + (rows, D_MODEL))


def _pack_small(scw, ccw, rep):
    lead = scw.shape[:-3]
    conv = jnp.concatenate([scw.reshape(lead + (-1,)), ccw.reshape(lead + (-1,))], axis=-1)
    flat = jnp.concatenate([rep[k].reshape(-1) for k in _REPLICATED])
    flat = jnp.broadcast_to(flat, lead + flat.shape)
    parts = [_pad_rows(conv, _CONV_ROWS), _pad_rows(flat, _REP_ROWS),
             jnp.zeros(lead + (_SMALL_ROWS - _CONV_ROWS - _REP_ROWS, D_MODEL), F32)]
    return jnp.concatenate(parts, axis=-2)


def _unpack_small(pk):
    out = {}
    conv = pk[:_CONV_ROWS].reshape(-1)
    n_s = DEPTH * SHORT_K * _CH_BLK
    out["short_conv_w"] = conv[:n_s].reshape(DEPTH, SHORT_K, _CH_BLK)
    out["conf_conv_w"] = conv[n_s:n_s + DEPTH * CONF_K * _CH_BLK].reshape(DEPTH, CONF_K, _CH_BLK)
    row = _CONV_ROWS
    flat = pk[row:row + _REP_ROWS].reshape(-1)
    at = 0
    for k in _REPLICATED:
        n = math.prod(_REP_SHAPES[k])
        out[k] = flat[at:at + n].reshape(_REP_SHAPES[k])
        at += n
    return out


def _conv_blocks(w):
    d, k, _ = w.shape
    return w.reshape(d, k, N_BLK, _CH_BLK).transpose(2, 0, 1, 3)


_WEIGHTS = ("norm_mix_g", "w_in", "gmlp_v_g", "gmlp_w_s", "gmlp_b_s", "short_conv_w", "conf_conv_w", "conf_ln_g",
            "conf_ln_b", "mix_out_g", "w_out", "norm_ffn_g", "w_up", "w_down", "final_norm_g")


def kernel(x, norm_mix_g, w_in, gmlp_v_g, gmlp_w_s, gmlp_b_s, short_conv_w, conf_conv_w, conf_ln_g, conf_ln_b, mix_out_g, w_out, norm_ffn_g, w_up, w_down, final_norm_g, loss_target, m_norm_mix_g, m_w_in, m_gmlp_v_g, m_gmlp_w_s, m_gmlp_b_s, m_short_conv_w, m_conf_conv_w, m_conf_ln_g, m_conf_ln_b, m_mix_out_g, m_w_out, m_norm_ffn_g, m_w_up, m_w_down, m_final_norm_g, v_norm_mix_g, v_w_in, v_gmlp_v_g, v_gmlp_w_s, v_gmlp_b_s, v_short_conv_w, v_conf_conv_w, v_conf_ln_g, v_conf_ln_b, v_mix_out_g, v_w_out, v_norm_ffn_g, v_w_up, v_w_down, v_final_norm_g):
    w = dict(norm_mix_g=norm_mix_g, w_in=w_in, gmlp_v_g=gmlp_v_g, gmlp_w_s=gmlp_w_s, gmlp_b_s=gmlp_b_s,
             short_conv_w=short_conv_w, conf_conv_w=conf_conv_w, conf_ln_g=conf_ln_g, conf_ln_b=conf_ln_b,
             mix_out_g=mix_out_g, w_out=w_out, norm_ffn_g=norm_ffn_g, w_up=w_up, w_down=w_down, final_norm_g=final_norm_g)
    m = dict(norm_mix_g=m_norm_mix_g, w_in=m_w_in, gmlp_v_g=m_gmlp_v_g, gmlp_w_s=m_gmlp_w_s, gmlp_b_s=m_gmlp_b_s,
             short_conv_w=m_short_conv_w, conf_conv_w=m_conf_conv_w, conf_ln_g=m_conf_ln_g, conf_ln_b=m_conf_ln_b,
             mix_out_g=m_mix_out_g, w_out=m_w_out, norm_ffn_g=m_norm_ffn_g, w_up=m_w_up, w_down=m_w_down,
             final_norm_g=m_final_norm_g)
    v = dict(norm_mix_g=v_norm_mix_g, w_in=v_w_in, gmlp_v_g=v_gmlp_v_g, gmlp_w_s=v_gmlp_w_s, gmlp_b_s=v_gmlp_b_s,
             short_conv_w=v_short_conv_w, conf_conv_w=v_conf_conv_w, conf_ln_g=v_conf_ln_g, conf_ln_b=v_conf_ln_b,
             mix_out_g=v_mix_out_g, w_out=v_w_out, norm_ffn_g=v_norm_ffn_g, w_up=v_w_up, w_down=v_w_down,
             final_norm_g=v_final_norm_g)
    core = lax.axis_index("c").astype(jnp.int32).reshape(1)
    me = (2 * lax.axis_index("x") + lax.axis_index("y")).astype(jnp.int32).reshape(1)

    conv_mine = _pad_rows(jnp.concatenate([short_conv_w.reshape(-1), conf_conv_w.reshape(-1)]), _CONV_ROWS)
    big = {k: _cast_into_slot(w[k], me, "cast_" + k) for k in _BIG}
    big["w_in"], conv_all = _gather_first([big["w_in"]], [(0, 0)], [conv_mine])
    conv_all = conv_all.reshape(N_BLK, -1)
    n_s = DEPTH * SHORT_K * _CH_BLK
    scw_all = conv_all[:, :n_s].reshape(N_BLK, DEPTH, SHORT_K, _CH_BLK)
    ccw_all = conv_all[:, n_s:n_s + DEPTH * CONF_K * _CH_BLK].reshape(N_BLK, DEPTH, CONF_K, _CH_BLK)
    small = {k: w[k] for k in _REPLICATED}
    small["short_conv_w"] = scw_all.transpose(1, 2, 0, 3).reshape(DEPTH, SHORT_K, D_GROUP)
    small["conf_conv_w"] = ccw_all.transpose(1, 2, 0, 3).reshape(DEPTH, CONF_K, D_GROUP)

    loss, dx, g, early = _local_step(x[0], loss_target[0], big, small, gather_pending=True, core=core)

    where = [(k, l) for k in _BIG for l in range(DEPTH)]
    late = [kl for kl in where if kl not in early]
    sums = _chip_sums(core, [g[k][l] for k, l in late]
                      + [_pack_small(_conv_blocks(g["short_conv_w"]), _conv_blocks(g["conf_conv_w"]), g)])
    sent = {**early, **dict(zip(late + ["small"], zip(sums, _send_to_owners(sums))))}
    mine = [_add_chips(me, *sent[kl]) for kl in where + ["small"]]
    other = _swap_reduced(mine)

    done = {}
    for n, (k, l) in enumerate(where):
        done[k] = _adamw(core, mine[n], other[n], w[k], m[k], v[k], l, done.get(k))
    small_own = [_pack_small(t["short_conv_w"], t["conf_conv_w"], t)[None] for t in (w, m, v)]
    small_done = [_unpack_small(a[0]) for a in _adamw(core, mine[-1], other[-1], *small_own, 0)]

    outs = [lax.psum(loss[0, 0], ("x", "y", "c")), dx[None]]
    for kind in range(4):
        outs += [done[k][kind] if k in _BIG else small_done[kind][k] for k in _WEIGHTS]
    return tuple(outs)
```

```python
import math

import jax
import jax.numpy as jnp
from jax import lax
from jax.experimental import pallas as pl
from jax.experimental.pallas import tpu as pltpu

F32 = jnp.float32
BF16 = jnp.bfloat16

D_MODEL = 1024
D_GROUP = 256
N_HEADS = 4
HEAD_DIM = 64
CHUNK = 128
D_IN = 2560
N_BLK = 4
W_IN_BLK = D_IN // N_BLK
D_FF_BLK = 1024
DEPTH = 2
EPS = 1e-6
HALO = 32
MIX_ROWS = 512
SHORT_K = 3
CONF_K = 31
ATT_TQ = 256
ATT_TK = 256
ATT_SCALE = 0.125
ATT_DEAD = -104.0
V7X_VMEM_LIMIT = 56 * 1024 * 1024

ADAM_LR, ADAM_B1, ADAM_B2, ADAM_EPS, ADAM_WD, ADAM_STEP = 0.001, 0.9, 0.999, 1e-08, 0.01, 10

MESH = pl.DeviceIdType.MESH


def _params(sem, vmem=None):
    return pltpu.CompilerParams(dimension_semantics=sem, vmem_limit_bytes=vmem)


def _tile(s, t):
    return min(s, t)


def _rsqrt_mean(v):
    return lax.rsqrt(jnp.mean(v * v, axis=-1, keepdims=True) + EPS)


def _sigmoid(v):
    return 1.0 / (1.0 + jnp.exp(-v))


_GELU_C = math.sqrt(2.0 / math.pi)


def _gelu_tanh(v):
    return jnp.tanh(_GELU_C * (v + 0.044715 * (v * v * v)))


def _gelu(v, t):
    return v * (0.5 * (1.0 + t))


def _gelu_grad(v, t):
    return 0.5 * (1.0 + t) + v * (0.5 * (1.0 - t * t) * _GELU_C * (1.0 + 3.0 * 0.044715 * (v * v)))


def _dot(a, b):
    return jnp.dot(a, b, preferred_element_type=F32)


def _dot_nt(a, b):
    return lax.dot_general(a, b, (((1,), (1,)), ((), ())), preferred_element_type=F32)


def _dot_tn(a, b):
    return lax.dot_general(a, b, (((0,), (0,)), ((), ())), preferred_element_type=F32)


def _cast_into_slot(w, me, name):
    n, r, c = w.shape
    tr = _tile(r, 256)

    def body(me_ref, w_ref, o_ref):
        o_ref[...] = w_ref[...].astype(BF16)

    return pl.pallas_call(
        body, name=name,
        grid_spec=pltpu.PrefetchScalarGridSpec(
            num_scalar_prefetch=1, grid=(n, r // tr),
            in_specs=[pl.BlockSpec((None, tr, c), lambda a, b, me_ref: (a, b, 0))],
            out_specs=pl.BlockSpec((None, None, tr, c), lambda a, b, me_ref: (me_ref[0], a, b, 0))),
        out_shape=jax.ShapeDtypeStruct((N_BLK,) + w.shape, BF16),
        compiler_params=_params(("parallel", "parallel")),
    )(me, w)


def _split_heads(xv, rows_ref, cols_ref):
    if rows_ref is not None:
        for h in range(N_HEADS):
            rows_ref[h] = xv[:, h * HEAD_DIM:(h + 1) * HEAD_DIM].astype(BF16)
    if cols_ref is not None:
        xt = xv.T
        for h in range(N_HEADS):
            cols_ref[h] = xt[h * HEAD_DIM:(h + 1) * HEAD_DIM, :].astype(BF16)


def _head_specs(t, s):
    rows = (pl.BlockSpec((N_HEADS, t, HEAD_DIM), lambda i: (0, i, 0)), jax.ShapeDtypeStruct((N_HEADS, s, HEAD_DIM), BF16))
    cols = (pl.BlockSpec((N_HEADS, HEAD_DIM, t), lambda i: (0, 0, i)), jax.ShapeDtypeStruct((N_HEADS, HEAD_DIM, s), BF16))
    return rows, cols


def _cols(ref, lo, hi):
    return ref[:, lo:hi].astype(F32)


def _pair_blocks(w_ref, wide_ref):
    @pl.when(pl.program_id(0) == 0)
    def _():
        for b in range(N_BLK):
            wide_ref[b // 2, :, (b % 2) * W_IN_BLK:(b % 2 + 1) * W_IN_BLK] = w_ref[b]


def _in_proj(x, g, w, layer):
    s = x.shape[0]
    t = _tile(s, 1024)

    def body(x_ref, g_ref, w_ref, z_ref, h_ref, qr_ref, qt_ref, kt_ref, vt_ref, wide_ref):
        _pair_blocks(w_ref, wide_ref)
        xv = x_ref[...]
        h = (xv * _rsqrt_mean(xv) * g_ref[...]).astype(BF16)
        h_ref[...] = h
        for n in range(N_BLK // 2):
            z_ref[:, 2 * n * W_IN_BLK:2 * (n + 1) * W_IN_BLK] = _dot(h, wide_ref[n]).astype(BF16)
        _split_heads(_cols(z_ref, 1280, 1536) * ATT_SCALE, qr_ref, qt_ref)
        _split_heads(_cols(z_ref, 1536, 1792), None, kt_ref)
        _split_heads(_cols(z_ref, 1792, 2048), None, vt_ref)

    rows, cols = _head_specs(t, s)
    return pl.pallas_call(
        body, name="in_proj", grid=(s // t,),
        in_specs=[pl.BlockSpec((t, D_MODEL), lambda i: (i, 0)), _full((1, D_MODEL)),
                  pl.BlockSpec((N_BLK, None, D_MODEL, W_IN_BLK), lambda i: (0, layer, 0, 0))],
        out_specs=[pl.BlockSpec((t, D_IN), lambda i: (i, 0)), pl.BlockSpec((t, D_MODEL), lambda i: (i, 0)),
                   rows[0], cols[0], cols[0], cols[0]],
        out_shape=[jax.ShapeDtypeStruct((s, D_IN), BF16), jax.ShapeDtypeStruct((s, D_MODEL), BF16),
                   rows[1], cols[1], cols[1], cols[1]],
        scratch_shapes=[pltpu.VMEM((N_BLK // 2, D_MODEL, 2 * W_IN_BLK), BF16)],
        compiler_params=_params(("arbitrary",), V7X_VMEM_LIMIT),
    )(x, g, w)


def _mix_a_fwd(z_ref, vg, wt_ref, bmat, t):
    zu = _cols(z_ref, 0, 256)
    zv = _cols(z_ref, 256, 512)
    tu = _gelu_tanh(zu)
    tv = _gelu_tanh(zv)
    u = _gelu(zu, tu)
    v = _gelu(zv, tv)
    rv = _rsqrt_mean(v)
    vh = v * rv
    vnb = (vh * vg).astype(BF16)
    head = lax.broadcasted_iota(jnp.int32, (CHUNK, D_GROUP), 1) // HEAD_DIM
    fs = []
    for c in range(t // CHUNK):
        vc = vnb[c * CHUNK:(c + 1) * CHUNK, :]
        fc = bmat
        for h in range(N_HEADS):
            fc = fc + jnp.where(head == h, _dot(wt_ref[h], vc), 0.0)
        fs.append(fc)
    f = jnp.concatenate(fs, axis=0) if len(fs) > 1 else fs[0]
    return (zu, tu), (zv, tv), u, rv, vh, vnb, f


def _windows(ext_ref, sh_ref, t):
    for b in range(1, 8):
        sh_ref[b - 1] = ext_ref[pl.ds(b, HALO + t - 8), :]

    def window(o):
        a, b = divmod(o, 8)
        return ext_ref[pl.ds(8 * a, t), :] if b == 0 else sh_ref[b - 1, pl.ds(8 * a, t), :]

    return window


def _mix_b_fwd(z_ref, zh_ref, first, scw_ref, ext_ref, t):
    gb = _cols(z_ref, 512, 768)
    uh = _cols(zh_ref, 768, 1024) * _cols(zh_ref, 1024, 1280)
    ext_ref[0:HALO, :] = jnp.where(first, 0.0, uh)
    ext_ref[HALO:HALO + t, :] = _cols(z_ref, 768, 1024) * _cols(z_ref, 1024, 1280)
    cv = jnp.zeros((t, D_GROUP), F32)
    for k in range(SHORT_K):
        cv = cv + scw_ref[k:k + 1, :] * ext_ref[pl.ds(HALO - (SHORT_K - 1) + k, t), :]
    return gb, cv


def _mix_d_fwd(z_ref, zh_ref, first, ccw_ref, lg, lb, ext_ref, sh_ref, t, cv=None):
    hh = _cols(zh_ref, 2048, 2304) * _sigmoid(_cols(zh_ref, 2304, 2560))
    ext_ref[0:HALO, :] = jnp.where(first, 0.0, hh)
    ext_ref[HALO:HALO + t, :] = _cols(z_ref, 2048, 2304) * _sigmoid(_cols(z_ref, 2304, 2560))
    window = _windows(ext_ref, sh_ref, t)
    if cv is None:
        cv = jnp.zeros((t, D_GROUP), F32)
        for k in range(CONF_K):
            cv = cv + ccw_ref[k:k + 1, :] * window(HALO - (CONF_K - 1) + k)
    xc = cv - jnp.mean(cv, axis=-1, keepdims=True)
    rs = lax.rsqrt(jnp.mean(xc * xc, axis=-1, keepdims=True) + EPS)
    xh = xc * rs
    ln = xh * lg + lb
    return xh, rs, ln, _sigmoid(ln), window, cv


def _mix_specs(t, s):
    per = t // HALO
    return [pl.BlockSpec((t, D_IN), lambda i: (i, 0)),
            pl.BlockSpec((HALO, D_IN), lambda i: (jnp.maximum(i * per - 1, 0), 0))]


def _full(shape):
    return pl.BlockSpec(shape, lambda i: (0,) * len(shape))


def _mixers_fwd(z, p, bufs=(), parts=()):
    s = z.shape[0]
    t = _tile(s, MIX_ROWS)
    nb = len(bufs)

    def body(z_ref, zh_ref, vg_ref, wt_ref, bm_ref, scw_ref, ccw_ref, lg_ref, lb_ref, *rest):
        y_ref = rest[nb]
        eb_ref, ed_ref, sh_ref = rest[2 * nb + 1:2 * nb + 4]
        if nb:
            _gather_in_steps(rest[nb + 1:2 * nb + 1], parts, rest[2 * nb + 4:], s // t)
        first = pl.program_id(0) == 0
        _, _, u, _, _, _, f = _mix_a_fwd(z_ref, vg_ref[...], wt_ref, bm_ref[...], t)
        ya = u * f
        y_ref[:, 0:256] = ya * _rsqrt_mean(ya)
        gb, cv = _mix_b_fwd(z_ref, zh_ref, first, scw_ref, eb_ref, t)
        yb = gb * cv
        y_ref[:, 256:512] = yb * _rsqrt_mean(yb)
        _, _, ln, sg, _, cvd = _mix_d_fwd(z_ref, zh_ref, first, ccw_ref, lg_ref[...], lb_ref[...], ed_ref, sh_ref, t)
        yd = ln * sg
        y_ref[:, 512:768] = yd * _rsqrt_mean(yd)
        y_ref[:, 768:1024] = cvd

    out = pl.pallas_call(
        body, name="mixers_fwd", grid=(s // t,),
        in_specs=_mix_specs(t, s) + [_full((1, D_GROUP)), _full((N_HEADS, CHUNK, CHUNK)), _full((CHUNK, D_GROUP)),
                                     _full((8, D_GROUP)), _full((32, D_GROUP)), _full((1, D_GROUP)), _full((1, D_GROUP))]
                 + [_ANY] * nb,
        out_specs=[pl.BlockSpec((t, D_MODEL), lambda i: (i, 0))] + [_ANY] * nb,
        out_shape=[jax.ShapeDtypeStruct((s, D_MODEL), F32)] + [jax.ShapeDtypeStruct(b.shape, b.dtype) for b in bufs],
        input_output_aliases={9 + k: 1 + k for k in range(nb)},
        scratch_shapes=[pltpu.VMEM((HALO + t, D_GROUP), F32), pltpu.VMEM((HALO + t, D_GROUP), F32),
                        pltpu.VMEM((7, HALO + t - 8, D_GROUP), F32)] + (_gather_sems(parts) if nb else []),
        compiler_params=_params(("arbitrary",) if nb else ("parallel",), V7X_VMEM_LIMIT),
    )(z, z, p["vg"], p["wt"], p["bmat"], p["scw"], p["ccw"], p["lg"], p["lb"], *bufs)
    return (out[0], out[1:]) if nb else out[0]


def _mixers_bwd_a(z, dyn, o_t, y_abd, p, sums=()):
    s = z.shape[0]
    t = _tile(s, MIX_ROWS)
    n_chunk = t // CHUNK
    ns = len(sums)

    def body(*refs):
        (z_ref, zh_ref, dyn_ref, ot_ref, cv_ref, vg_ref, wt_ref, wtt_ref, bm_ref, scw_ref, ccw_ref, lg_ref,
         lb_ref) = refs[:13]
        (dza_ref, dcb_ref, dcd_ref, dor_ref, dot_ref, ds_ref, dvg_ref, dws_ref, dbm_ref, dscw_ref, dccw_ref, dlg_ref,
         dlb_ref) = refs[13 + ns:26 + ns]
        eb_ref, ed_ref, sh_ref = refs[26 + 2 * ns:29 + 2 * ns]
        i = pl.program_id(0)
        first = i == 0
        if ns:
            _owners_in_steps(refs[13:13 + ns], refs[26 + ns:26 + 2 * ns], refs[29 + 2 * ns:], first, i == s // t - 1)

        @pl.when(first)
        def _():
            for r in (dvg_ref, dws_ref, dbm_ref, dscw_ref, dccw_ref, dlg_ref, dlb_ref):
                r[...] = jnp.zeros_like(r)

        def rms_bwd(y, dn):
            r = _rsqrt_mean(y)
            yn = y * r
            return r * (dn - yn * jnp.mean(dn * yn, axis=-1, keepdims=True))

        vg = vg_ref[...]
        gelu_u, gelu_v, u, rv, vh, vnb, f = _mix_a_fwd(z_ref, vg, wt_ref, bm_ref[...], t)
        dya = rms_bwd(u * f, _cols(dyn_ref, 0, 256))
        du = dya * f
        df = dya * u
        head = lax.broadcasted_iota(jnp.int32, (CHUNK, D_GROUP), 1) // HEAD_DIM
        dvns = []
        dbm = jnp.zeros((CHUNK, D_GROUP), F32)
        for c in range(n_chunk):
            dfc = df[c * CHUNK:(c + 1) * CHUNK, :]
            vc = vnb[c * CHUNK:(c + 1) * CHUNK, :]
            dbm = dbm + dfc
            dvn = jnp.zeros((CHUNK, D_GROUP), F32)
            for h in range(N_HEADS):
                dfh = jnp.where(head == h, dfc, 0.0).astype(BF16)
                dvn = dvn + _dot(wtt_ref[h], dfh)
                dws_ref[h] += _dot_nt(dfh, vc)
            dvns.append(dvn)
        dbm_ref[...] += dbm
        dvn = jnp.concatenate(dvns, axis=0) if n_chunk > 1 else dvns[0]
        dvg_ref[...] += jnp.sum(dvn * vh, axis=0, keepdims=True)
        dvh = dvn * vg
        dv = rv * (dvh - vh * jnp.mean(dvh * vh, axis=-1, keepdims=True))
        dza_ref[:, 0:256] = (du * _gelu_grad(*gelu_u)).astype(BF16)
        dza_ref[:, 256:512] = (dv * _gelu_grad(*gelu_v)).astype(BF16)

        gb, cv = _mix_b_fwd(z_ref, zh_ref, first, scw_ref, eb_ref, t)
        dyb = rms_bwd(gb * cv, _cols(dyn_ref, 256, 512))
        dza_ref[:, 512:768] = (dyb * cv).astype(BF16)
        dcb = dyb * gb
        dcb_ref[...] = dcb
        for k in range(SHORT_K):
            dscw_ref[k:k + 1, :] += jnp.sum(dcb * eb_ref[pl.ds(HALO - (SHORT_K - 1) + k, t), :], axis=0, keepdims=True)

        lg = lg_ref[...]
        xh, rs, ln, sg, window, _ = _mix_d_fwd(z_ref, zh_ref, first, ccw_ref, lg, lb_ref[...], ed_ref, sh_ref, t,
                                               cv_ref[...])
        dyd = rms_bwd(ln * sg, _cols(dyn_ref, 768, 1024))
        dln = dyd * (sg * (1.0 + ln * (1.0 - sg)))
        dlg_ref[...] += jnp.sum(dln * xh, axis=0, keepdims=True)
        dlb_ref[...] += jnp.sum(dln, axis=0, keepdims=True)
        dxh = dln * lg
        dcd = rs * (dxh - jnp.mean(dxh, axis=-1, keepdims=True) - xh * jnp.mean(dxh * xh, axis=-1, keepdims=True))
        dcd_ref[...] = dcd
        for k in range(CONF_K):
            dccw_ref[k:k + 1, :] += jnp.sum(dcd * window(HALO - (CONF_K - 1) + k), axis=0, keepdims=True)

        o = ot_ref[...].reshape(D_GROUP, t).T
        do = rms_bwd(o, _cols(dyn_ref, 512, 768))
        _split_heads(do, dor_ref, dot_ref)
        prod = do.astype(BF16).astype(F32) * o
        for h in range(N_HEADS):
            ds_ref[h] = jnp.sum(prod[:, h * HEAD_DIM:(h + 1) * HEAD_DIM], axis=1, keepdims=True)

    small = [(1, D_GROUP), (N_HEADS, CHUNK, CHUNK), (CHUNK, D_GROUP), (8, D_GROUP), (32, D_GROUP), (1, D_GROUP), (1, D_GROUP)]
    rows, cols = _head_specs(t, s)
    out = pl.pallas_call(
        body, name="mixers_bwd_a", grid=(s // t,),
        in_specs=_mix_specs(t, s) + [pl.BlockSpec((t, D_MODEL), lambda i: (i, 0)),
                                     pl.BlockSpec((N_HEADS, HEAD_DIM, t), lambda i: (0, 0, i)),
                                     pl.BlockSpec((t, D_GROUP), lambda i: (i, 3)),
                                     _full((1, D_GROUP)), _full((N_HEADS, CHUNK, CHUNK)), _full((N_HEADS, CHUNK, CHUNK)),
                                     _full((CHUNK, D_GROUP)), _full((8, D_GROUP)), _full((32, D_GROUP)),
                                     _full((1, D_GROUP)), _full((1, D_GROUP))] + [_ANY] * ns,
        out_specs=[pl.BlockSpec((t, 768), lambda i: (i, 0)), pl.BlockSpec((t, D_GROUP), lambda i: (i, 0)),
                   pl.BlockSpec((t, D_GROUP), lambda i: (i, 0)), rows[0], cols[0],
                   pl.BlockSpec((N_HEADS, t, 1), lambda i: (0, i, 0))]
                  + [_full(sh) for sh in small] + [_ANY] * ns,
        out_shape=[jax.ShapeDtypeStruct((s, 768), BF16), jax.ShapeDtypeStruct((s, D_GROUP), F32),
                   jax.ShapeDtypeStruct((s, D_GROUP), F32), rows[1], cols[1],
                   jax.ShapeDtypeStruct((N_HEADS, s, 1), F32)]
                  + [jax.ShapeDtypeStruct(sh, F32) for sh in small]
                  + [jax.ShapeDtypeStruct(a.shape, a.dtype) for a in sums],
        scratch_shapes=[pltpu.VMEM((HALO + t, D_GROUP), F32), pltpu.VMEM((HALO + t, D_GROUP), F32),
                        pltpu.VMEM((7, HALO + t - 8, D_GROUP), F32)] + (_owner_sems(ns) if ns else []),
        compiler_params=_params(("arbitrary",), V7X_VMEM_LIMIT),
    )(z, z, dyn, o_t, y_abd, p["vg"], p["wt"], p["wtt"], p["bmat"], p["scw"], p["ccw"], p["lg"], p["lb"], *sums)
    return tuple(out[:13]) + (out[13:],) if ns else out


def _mixers_bwd_b(z, dza, dcb, dcd, dq_t, dk_t, dv_t, p):
    s = z.shape[0]
    t = _tile(s, MIX_ROWS)
    per = t // HALO
    n_halo = s // HALO

    def body(z_ref, dza_ref, dcb_ref, dcbn_ref, dcd_ref, dcdn_ref, dq_ref, dk_ref, dv_ref, scw_ref, ccw_ref,
             dz_ref, eb_ref, ed_ref, sh_ref):
        last = pl.program_id(0) == pl.num_programs(0) - 1
        dz_ref[:, 0:768] = dza_ref[...]
        eb_ref[0:t, :] = dcb_ref[...]
        eb_ref[t:t + HALO, :] = jnp.where(last, 0.0, dcbn_ref[...])
        du = jnp.zeros((t, D_GROUP), F32)
        for k in range(SHORT_K):
            du = du + scw_ref[k:k + 1, :] * eb_ref[pl.ds(SHORT_K - 1 - k, t), :]
        dz_ref[:, 768:1024] = (du * _cols(z_ref, 1024, 1280)).astype(BF16)
        dz_ref[:, 1024:1280] = (du * _cols(z_ref, 768, 1024)).astype(BF16)
        for n, r in enumerate((dq_ref, dk_ref, dv_ref)):
            dz_ref[:, 1280 + 256 * n:1536 + 256 * n] = r[...].reshape(D_GROUP, t).T.astype(BF16)
        ed_ref[0:t, :] = dcd_ref[...]
        ed_ref[t:t + HALO, :] = jnp.where(last, 0.0, dcdn_ref[...])
        window = _windows(ed_ref, sh_ref, t)
        dh = jnp.zeros((t, D_GROUP), F32)
        for k in range(CONF_K):
            dh = dh + ccw_ref[k:k + 1, :] * window(CONF_K - 1 - k)
        a = _cols(z_ref, 2048, 2304)
        sg = _sigmoid(_cols(z_ref, 2304, 2560))
        dz_ref[:, 2048:2304] = (dh * sg).astype(BF16)
        dz_ref[:, 2304:2560] = (dh * a * sg * (1.0 - sg)).astype(BF16)

    nxt = lambda i: (jnp.minimum((i + 1) * per, n_halo - 1), 0)
    tr = pl.BlockSpec((N_HEADS, HEAD_DIM, t), lambda i: (0, 0, i))
    return pl.pallas_call(
        body, name="mixers_bwd_b", grid=(s // t,),
        in_specs=[pl.BlockSpec((t, D_IN), lambda i: (i, 0)), pl.BlockSpec((t, 768), lambda i: (i, 0)),
                  pl.BlockSpec((t, D_GROUP), lambda i: (i, 0)), pl.BlockSpec((HALO, D_GROUP), nxt),
                  pl.BlockSpec((t, D_GROUP), lambda i: (i, 0)), pl.BlockSpec((HALO, D_GROUP), nxt),
                  tr, tr, tr, _full((8, D_GROUP)), _full((32, D_GROUP))],
        out_specs=pl.BlockSpec((t, D_IN), lambda i: (i, 0)),
        out_shape=jax.ShapeDtypeStruct((s, D_IN), BF16),
        scratch_shapes=[pltpu.VMEM((HALO + t, D_GROUP), F32), pltpu.VMEM((HALO + t, D_GROUP), F32),
                        pltpu.VMEM((7, HALO + t - 8, D_GROUP), F32)],
        compiler_params=_params(("parallel",), V7X_VMEM_LIMIT),
    )(z, dza, dcb, dcb, dcd, dcd, dq_t, dk_t, dv_t, p["scw"], p["ccw"])


def _split_bf16(v):
    hi = v.astype(BF16)
    return hi, (v - hi.astype(F32)).astype(BF16)


def _att_scores(qs, kts, carries, tri, mask):
    zs = [_dot(q, kt) for q, kt in zip(qs, kts)]
    lms, lbs, parts = [], [], []
    for z in zs:
        soft = jnp.log(1.0 + jnp.exp(-jnp.abs(z)))
        lm = -(jnp.maximum(z, 0.0) + soft)
        lbs.append(lm + z)
        if mask is not None:
            lm = jnp.where(mask, lm, 0.0)
        lms.append(lm)
        parts.append(_split_bf16(lm))
    rights = [_dot(hi, tri) + _dot(lo, tri) for hi, lo in parts]
    ws = []
    for lb, right, carry in zip(lbs, rights, carries):
        w = jnp.exp(lb + right + carry)
        ws.append(w if mask is None else jnp.where(mask, w, 0.0))
    return ws, lbs, [jnp.sum(lm, axis=1, keepdims=True) for lm in lms]


def _att_consts(i):
    j_hi = ((i + 1) * ATT_TQ - 1) // ATT_TK
    row = lax.broadcasted_iota(jnp.int32, (ATT_TQ, ATT_TK), 0) + i * ATT_TQ
    col = lax.broadcasted_iota(jnp.int32, (ATT_TQ, ATT_TK), 1) + j_hi * ATT_TK
    r_i = lax.broadcasted_iota(jnp.int32, (ATT_TK, ATT_TK), 0)
    c_i = lax.broadcasted_iota(jnp.int32, (ATT_TK, ATT_TK), 1)
    return j_hi, col < row, r_i, c_i


def _att_alive(j, carries):
    top = carries[0]
    for c in carries[1:]:
        top = jnp.maximum(top, c)
    return jnp.logical_and(j >= 0, jnp.max(top) > ATT_DEAD)


def _attn_fwd(q_r, k_t, v_t, bufs=(), parts=()):
    assert ATT_TQ == ATT_TK
    s = q_r.shape[1]
    nb = len(bufs)
    nbq = min(ATT_FWD_BLOCKS, s // ATT_TQ)
    step_rows = nbq * ATT_TQ
    lanes = [(h, b) for h in range(N_HEADS) for b in range(nbq)]

    def body(q_ref, kt_ref, vt_ref, *rest):
        o_ref = rest[nb]
        if nb:
            _gather_in_steps(rest[nb + 1:2 * nb + 1], parts, rest[2 * nb + 1:], s // step_rows)
        i = pl.program_id(0)
        _, mask, r_i, c_i = _att_consts(0)
        tri = (r_i > c_i).astype(BF16)
        rows_of = [pl.ds(b * ATT_TQ, ATT_TQ) for _, b in lanes]

        def tiles(t, carries, accs, mask):
            js = [nbq * i + b - t for _, b in lanes]
            live = [None if mask is not None or b == nbq - 1 else j >= 0 for (_, b), j in zip(lanes, js)]
            cols = [pl.ds(pl.multiple_of(jnp.maximum(j, 0) * ATT_TK, ATT_TK), ATT_TK) for j in js]
            ws, _, tots = _att_scores([q_ref[h, r, :] for (h, _), r in zip(lanes, rows_of)],
                                      [kt_ref[h, :, c] for (h, _), c in zip(lanes, cols)], carries, tri, mask)
            ws = [w if ok is None else jnp.where(ok, w, 0.0) for w, ok in zip(ws, live)]
            tots = [tot if ok is None else jnp.where(ok, tot, -1e30) for tot, ok in zip(tots, live)]
            accs = [acc + _dot_nt(vt_ref[h, :, c], w.astype(BF16)) for (h, _), c, acc, w in zip(lanes, cols, accs, ws)]
            return [c + tot for c, tot in zip(carries, tots)], accs

        state = tiles(0, [jnp.zeros((ATT_TQ, 1), F32)] * len(lanes), [jnp.zeros((HEAD_DIM, ATT_TQ), F32)] * len(lanes), mask)

        def cond(c):
            return _att_alive(nbq * i + nbq - 1 - c[0], c[1])

        def step(c):
            return (c[0] + 1,) + tuple(tiles(c[0], c[1], c[2], None))

        _, _, accs = lax.while_loop(cond, step, (1,) + tuple(state))
        for n, (h, _) in enumerate(lanes):
            o_ref[h, :, rows_of[n]] = accs[n]

    whole = pl.BlockSpec((N_HEADS, HEAD_DIM, s), lambda i: (0, 0, 0), pipeline_mode=pl.Buffered(1))
    out = pl.pallas_call(
        body, name="attn_fwd", grid=(s // step_rows,),
        in_specs=[pl.BlockSpec((N_HEADS, step_rows, HEAD_DIM), lambda i: (0, i, 0)), whole, whole] + [_ANY] * nb,
        out_specs=[pl.BlockSpec((N_HEADS, HEAD_DIM, step_rows), lambda i: (0, 0, i))] + [_ANY] * nb,
        out_shape=[jax.ShapeDtypeStruct((N_HEADS, HEAD_DIM, s), F32)] + [jax.ShapeDtypeStruct(b.shape, b.dtype) for b in bufs],
        input_output_aliases={3 + k: 1 + k for k in range(nb)},
        scratch_shapes=_gather_sems(parts) if nb else [],
        compiler_params=_params(("arbitrary",), V7X_VMEM_LIMIT),
    )(q_r, k_t, v_t, *bufs)
    return (out[0], out[1:]) if nb else out[0]


ATT_FWD_BLOCKS = 2
ATT_BWD_HEADS = 2
ATT_BWD_BLOCKS = 2


def _attn_bwd(q_r, q_t, k_t, v_t, do_r, do_t, dsum):
    assert ATT_TQ == ATT_TK
    s = q_r.shape[1]
    hps = ATT_BWD_HEADS
    nbq = min(ATT_BWD_BLOCKS, s // ATT_TQ)
    step_rows = nbq * ATT_TQ
    lanes = [(h, b) for h in range(hps) for b in range(nbq)]

    def body(q_ref, qt_ref, kt_ref, vt_ref, do_ref, dot_ref, ds_ref, dq_ref, dk_ref, dv_ref):
        i = pl.program_id(1)

        @pl.when(i == 0)
        def _():
            dk_ref[...] = jnp.zeros_like(dk_ref)
            dv_ref[...] = jnp.zeros_like(dv_ref)

        _, mask, r_i, c_i = _att_consts(0)
        tri_r = (r_i > c_i).astype(BF16)
        tri_ge = (r_i >= c_i).astype(BF16)
        rows_of = [pl.ds(b * ATT_TQ, ATT_TQ) for _, b in lanes]

        def tiles(t, carries, gsums, accs, mask):
            js = [nbq * i + b - t for _, b in lanes]
            live = [None if mask is not None or b == nbq - 1 else j >= 0 for (_, b), j in zip(lanes, js)]
            cols = [pl.ds(pl.multiple_of(jnp.maximum(j, 0) * ATT_TK, ATT_TK), ATT_TK) for j in js]
            kts = [kt_ref[h, :, c] for (h, _), c in zip(lanes, cols)]
            ws, lbs, tots = _att_scores([q_ref[h, r, :] for (h, _), r in zip(lanes, rows_of)], kts, carries, tri_r, mask)
            ws = [w if ok is None else jnp.where(ok, w, 0.0) for w, ok in zip(ws, live)]
            tots = [tot if ok is None else jnp.where(ok, tot, -1e30) for tot, ok in zip(tots, live)]
            wbs = [w.astype(BF16) for w in ws]
            das = [_dot(do_ref[h, r, :], vt_ref[h, :, c]) for (h, _), r, c in zip(lanes, rows_of, cols)]
            gs = [wb.astype(F32) * da for wb, da in zip(wbs, das)]
            parts = [_split_bf16(g) for g in gs]
            sfx = [_dot(hi, tri_ge) + _dot(lo, tri_ge) for hi, lo in parts]
            for n, (h, _) in enumerate(lanes):
                dv_ref[h, :, cols[n]] += _dot(dot_ref[h, :, rows_of[n]], wbs[n])
            dzs = []
            for n, (h, _) in enumerate(lanes):
                left = ds_ref[h, rows_of[n], :] - gsums[n] - sfx[n]
                dz = gs[n] - jnp.exp(lbs[n]) * (gs[n] + left)
                if mask is not None:
                    dz = jnp.where(mask, dz, 0.0)
                elif live[n] is not None:
                    dz = jnp.where(live[n], dz, 0.0)
                dzs.append(dz.astype(BF16))
            for n, (h, _) in enumerate(lanes):
                dk_ref[h, :, cols[n]] += _dot(qt_ref[h, :, rows_of[n]], dzs[n])
            accs = [acc + _dot_nt(kt, dz) for acc, kt, dz in zip(accs, kts, dzs)]
            gsums = [gsum + jnp.sum(g, axis=1, keepdims=True) for gsum, g in zip(gsums, gs)]
            return [c + tot for c, tot in zip(carries, tots)], gsums, accs

        col0 = [jnp.zeros((ATT_TQ, 1), F32)] * len(lanes)
        state = tiles(0, col0, col0, [jnp.zeros((HEAD_DIM, ATT_TQ), F32)] * len(lanes), mask)

        def cond(c):
            return _att_alive(nbq * i + nbq - 1 - c[0], c[1])

        def step(c):
            return (c[0] + 1,) + tuple(tiles(c[0], c[1], c[2], c[3], None))

        _, _, _, accs = lax.while_loop(cond, step, (1,) + tuple(state))
        for n, (h, _) in enumerate(lanes):
            dq_ref[h, :, rows_of[n]] = accs[n] * ATT_SCALE

    whole = pl.BlockSpec((hps, HEAD_DIM, s), lambda g, i: (g, 0, 0))
    whole_in = pl.BlockSpec((hps, HEAD_DIM, s), lambda g, i: (g, 0, 0), pipeline_mode=pl.Buffered(1))
    rows = pl.BlockSpec((hps, step_rows, HEAD_DIM), lambda g, i: (g, i, 0))
    cols = pl.BlockSpec((hps, HEAD_DIM, step_rows), lambda g, i: (g, 0, i))
    shape = jax.ShapeDtypeStruct((N_HEADS, HEAD_DIM, s), F32)
    return pl.pallas_call(
        body, name="attn_bwd", grid=(N_HEADS // hps, s // step_rows),
        in_specs=[rows, cols, whole_in, whole_in, rows, cols, pl.BlockSpec((hps, step_rows, 1), lambda g, i: (g, i, 0))],
        out_specs=[cols, whole, whole],
        out_shape=[shape, shape, shape],
        compiler_params=_params(("parallel", "arbitrary"), V7X_VMEM_LIMIT),
    )(q_r, q_t, k_t, v_t, do_r, do_t, dsum)


def _out_proj(x, y_abd, o_t, gain, w, layer):
    s = x.shape[0]
    t = _tile(s, 1024)

    def body(x_ref, y_ref, ot_ref, g_ref, w_ref, x1_ref, yn_ref):
        o = ot_ref[...].reshape(D_GROUP, t).T
        groups = [y_ref[:, 0:256], y_ref[:, 256:512], o * _rsqrt_mean(o), y_ref[:, 512:768]]
        g = g_ref[...]
        acc = None
        for b, yn in enumerate(groups):
            cols = slice(256 * b, 256 * (b + 1))
            yn_ref[:, cols] = yn.astype(BF16)
            part = _dot((yn * g[:, cols]).astype(BF16), w_ref[b])
            acc = part if acc is None else acc + part
        x1_ref[...] = x_ref[...] + acc

    return pl.pallas_call(
        body, name="out_proj", grid=(s // t,),
        in_specs=[pl.BlockSpec((t, D_MODEL), lambda i: (i, 0)), pl.BlockSpec((t, 768), lambda i: (i, 0)),
                  pl.BlockSpec((N_HEADS, HEAD_DIM, t), lambda i: (0, 0, i)), _full((1, D_MODEL)),
                  pl.BlockSpec((N_BLK, None, D_GROUP, D_MODEL), lambda i: (0, layer, 0, 0))],
        out_specs=[pl.BlockSpec((t, D_MODEL), lambda i: (i, 0)), pl.BlockSpec((t, D_MODEL), lambda i: (i, 0))],
        out_shape=[jax.ShapeDtypeStruct((s, D_MODEL), F32), jax.ShapeDtypeStruct((s, D_MODEL), BF16)],
        compiler_params=_params(("parallel",), V7X_VMEM_LIMIT),
    )(x, y_abd, o_t, gain, w)


def _out_proj_bwd(x1, g_ffn, dh, dx2, yn, gain, w, layer, halves=()):
    s = dx2.shape[0]
    t = _tile(s, 1024)
    nh = len(halves)

    def body(x_ref, gf_ref, dh_ref, dx2_ref, yn_ref, g_ref, w_ref, *rest):
        dx1_ref, dgf_ref, dyn_ref, dg_ref, dw_ref = rest[nh:nh + 5]
        if nh:
            start, finish = _halves_stages(rest[:nh], rest[nh + 5:2 * nh + 5], rest[2 * nh + 5:])
            pl.when(pl.program_id(0) == 0)(start)
            pl.when(pl.program_id(0) == s // t - 1)(finish)

        @pl.when(pl.program_id(0) == 0)
        def _():
            dg_ref[...] = jnp.zeros_like(dg_ref)
            dw_ref[...] = jnp.zeros_like(dw_ref)
            dgf_ref[...] = jnp.zeros_like(dgf_ref)

        dx1, dgf = _rms_bwd_rows(x_ref[...], gf_ref[...], dh_ref[...], dx2_ref[...])
        dx1_ref[...] = dx1
        dgf_ref[...] += dgf
        dxb = dx1.astype(BF16)
        g = g_ref[...]
        yn = yn_ref[...].astype(F32)
        yg = (yn * g).astype(BF16)
        for b in range(N_BLK):
            cols = slice(256 * b, 256 * (b + 1))
            dyg = _dot_nt(dxb, w_ref[b])
            dw_ref[b] += _dot_tn(yg[:, cols], dxb)
            dg_ref[:, cols] += jnp.sum(dyg * yn[:, cols], axis=0, keepdims=True)
            dyn_ref[:, cols] = (dyg * g[:, cols]).astype(BF16)

    row = pl.BlockSpec((t, D_MODEL), lambda i: (i, 0))
    vec = _full((1, D_MODEL))
    out = pl.pallas_call(
        body, name="out_proj_bwd", grid=(s // t,),
        in_specs=[row, vec, row, row, row, vec, pl.BlockSpec((N_BLK, None, D_GROUP, D_MODEL), lambda i: (0, layer, 0, 0))]
                 + [_ANY] * nh,
        out_specs=[row, vec, row, vec, _full((N_BLK, D_GROUP, D_MODEL))] + [_ANY] * nh,
        out_shape=[jax.ShapeDtypeStruct((s, D_MODEL), F32), jax.ShapeDtypeStruct((1, D_MODEL), F32),
                   jax.ShapeDtypeStruct((s, D_MODEL), BF16), jax.ShapeDtypeStruct((1, D_MODEL), F32),
                   jax.ShapeDtypeStruct((N_BLK, D_GROUP, D_MODEL), F32)] + _halves_shapes(halves),
        scratch_shapes=_halves_sems(nh) if nh else [],
        compiler_params=_params(("arbitrary",), V7X_VMEM_LIMIT),
    )(x1, g_ffn, dh, dx2, yn, gain, w, *halves)
    return tuple(out[:5]) + (out[5:],) if nh else out


def _ffn(x, g, w_up, w_down, layer):
    s = x.shape[0]
    t = _tile(s, 1024)

    def body(x_ref, g_ref, wu_ref, wd_ref, x2_ref, p_ref, h_ref):
        @pl.when(pl.program_id(1) == 0)
        def _():
            xv = x_ref[...]
            h_ref[...] = (xv * _rsqrt_mean(xv) * g_ref[...]).astype(BF16)
            x2_ref[...] = xv

        pre = _dot(h_ref[...], wu_ref[...])
        p_ref[...] = pre.astype(BF16)
        a = jnp.maximum(pre, 0.0)
        x2_ref[...] += _dot((a * a).astype(BF16), wd_ref[...])

    wspec = pl.BlockSpec((None, None, D_MODEL, D_FF_BLK), lambda i, j: (j, layer, 0, 0))
    row = pl.BlockSpec((t, D_MODEL), lambda i, j: (i, 0))
    return pl.pallas_call(
        body, name="ffn", grid=(s // t, N_BLK),
        in_specs=[row, pl.BlockSpec((1, D_MODEL), lambda i, j: (0, 0)), wspec, wspec],
        out_specs=[row, pl.BlockSpec((t, D_FF_BLK), lambda i, j: (i, j)), row],
        out_shape=[jax.ShapeDtypeStruct((s, D_MODEL), F32), jax.ShapeDtypeStruct((s, N_BLK * D_FF_BLK), BF16),
                   jax.ShapeDtypeStruct((s, D_MODEL), BF16)],
        compiler_params=_params(("parallel", "arbitrary"), V7X_VMEM_LIMIT),
    )(x, g, w_up, w_down)


def _loss_head(x, g, target):
    s = x.shape[0]
    t = _tile(s, 512)

    def body(x_ref, g_ref, t_ref, l_ref, dx_ref, dxb_ref, dg_ref):
        @pl.when(pl.program_id(0) == 0)
        def _():
            l_ref[...] = jnp.zeros_like(l_ref)
            dg_ref[...] = jnp.zeros_like(dg_ref)

        xv = x_ref[...]
        g = g_ref[...]
        r = _rsqrt_mean(xv)
        xh = xv * r
        err = xh * g - t_ref[...]
        l_ref[...] += 0.5 * jnp.sum(jnp.mean(err * err, axis=-1, keepdims=True), axis=0, keepdims=True)
        dy = err * (1.0 / D_MODEL)
        dg_ref[...] += jnp.sum(dy * xh, axis=0, keepdims=True)
        dxh = dy * g
        dx = r * (dxh - xh * jnp.mean(dxh * xh, axis=-1, keepdims=True))
        dx_ref[...] = dx
        dxb_ref[...] = dx.astype(BF16)

    row = pl.BlockSpec((t, D_MODEL), lambda i: (i, 0))
    return pl.pallas_call(
        body, name="loss_head", grid=(s // t,),
        in_specs=[row, _full((1, D_MODEL)), row],
        out_specs=[_full((1, 128)), row, row, _full((1, D_MODEL))],
        out_shape=[jax.ShapeDtypeStruct((1, 128), F32), jax.ShapeDtypeStruct((s, D_MODEL), F32),
                   jax.ShapeDtypeStruct((s, D_MODEL), BF16), jax.ShapeDtypeStruct((1, D_MODEL), F32)],
        compiler_params=_params(("arbitrary",)),
    )(x, g, target)


def _rms_bwd_rows(xv, g, dh, dres):
    r = _rsqrt_mean(xv)
    xh = xv * r
    dxh = dh * g
    dx = dres + r * (dxh - xh * jnp.mean(dxh * xh, axis=-1, keepdims=True))
    return dx, jnp.sum(dh * xh, axis=0, keepdims=True)


def _ffn_bwd(dxb, p, w_up, w_down, layer, sums=()):
    s = dxb.shape[0]
    t = _tile(s, 1024)
    ns = len(sums)

    def body(dx_ref, p_ref, wu_ref, wd_ref, *rest):
        dp_ref, dh_ref = rest[ns:ns + 2]
        if ns:
            i, j = pl.program_id(0), pl.program_id(1)
            _owners_in_steps(rest[:ns], rest[ns + 2:2 * ns + 2], rest[2 * ns + 2:],
                             jnp.logical_and(i == 0, j == 0), jnp.logical_and(i == s // t - 1, j == N_BLK - 1))
        da = _dot_nt(dx_ref[...], wd_ref[...])
        a = jnp.maximum(p_ref[...].astype(F32), 0.0)
        dp = (da * (2.0 * a)).astype(BF16)
        dp_ref[...] = dp
        dh = _dot_nt(dp, wu_ref[...])

        @pl.when(pl.program_id(1) == 0)
        def _():
            dh_ref[...] = dh

        @pl.when(pl.program_id(1) != 0)
        def _():
            dh_ref[...] += dh

    wspec = pl.BlockSpec((None, None, D_MODEL, D_FF_BLK), lambda i, j: (j, layer, 0, 0))
    row = pl.BlockSpec((t, D_MODEL), lambda i, j: (i, 0))
    blk = pl.BlockSpec((t, D_FF_BLK), lambda i, j: (i, j))
    out = pl.pallas_call(
        body, name="ffn_bwd", grid=(s // t, N_BLK),
        in_specs=[row, blk, wspec, wspec] + [_ANY] * ns, out_specs=[blk, row] + [_ANY] * ns,
        out_shape=[jax.ShapeDtypeStruct((s, N_BLK * D_FF_BLK), BF16), jax.ShapeDtypeStruct((s, D_MODEL), F32)]
                  + [jax.ShapeDtypeStruct(a.shape, a.dtype) for a in sums],
        scratch_shapes=_owner_sems(ns) if ns else [],
        compiler_params=_params(("arbitrary" if ns else "parallel", "arbitrary"), V7X_VMEM_LIMIT),
    )(dxb, p, w_up, w_down, *sums)
    return (out[0], out[1], out[2:]) if ns else out


def _ffn_wgrad(hb, p, dp, dxb):
    s = hb.shape[0]
    t = _tile(s, 2048)

    def body(h_ref, p_ref, dp_ref, dx_ref, du_ref, dd_ref):
        @pl.when(pl.program_id(1) == 0)
        def _():
            du_ref[...] = jnp.zeros_like(du_ref)
            dd_ref[...] = jnp.zeros_like(dd_ref)

        a = jnp.maximum(p_ref[...].astype(F32), 0.0)
        du_ref[...] += _dot_tn(h_ref[...], dp_ref[...])
        dd_ref[...] += _dot_tn((a * a).astype(BF16), dx_ref[...])

    row = pl.BlockSpec((t, D_MODEL), lambda j, i: (i, 0))
    blk = pl.BlockSpec((t, D_FF_BLK), lambda j, i: (i, j))
    out = pl.BlockSpec((None, D_MODEL, D_FF_BLK), lambda j, i: (j, 0, 0))
    shape = jax.ShapeDtypeStruct((N_BLK, D_MODEL, D_FF_BLK), F32)
    return pl.pallas_call(
        body, name="ffn_wgrad", grid=(N_BLK, s // t),
        in_specs=[row, blk, blk, row], out_specs=[out, out], out_shape=[shape, shape],
        compiler_params=_params(("parallel", "arbitrary"), V7X_VMEM_LIMIT),
    )(hb, p, dp, dxb)


def _in_proj_bwd(x, g, dx1, dz, w, layer):
    s = x.shape[0]
    t = _tile(s, 512)

    def body(x_ref, g_ref, dx1_ref, dz_ref, w_ref, dx0_ref, dxb_ref, dg_ref, wide_ref):
        _pair_blocks(w_ref, wide_ref)

        @pl.when(pl.program_id(0) == 0)
        def _():
            dg_ref[...] = jnp.zeros_like(dg_ref)

        dh = _dot_nt(dz_ref[:, 0:2 * W_IN_BLK], wide_ref[0])
        for n in range(1, N_BLK // 2):
            dh = dh + _dot_nt(dz_ref[:, 2 * n * W_IN_BLK:2 * (n + 1) * W_IN_BLK], wide_ref[n])
        dx, dg = _rms_bwd_rows(x_ref[...], g_ref[...], dh, dx1_ref[...])
        dx0_ref[...] = dx
        dxb_ref[...] = dx.astype(BF16)
        dg_ref[...] += dg

    row = pl.BlockSpec((t, D_MODEL), lambda i: (i, 0))
    return pl.pallas_call(
        body, name="in_proj_bwd", grid=(s // t,),
        in_specs=[row, _full((1, D_MODEL)), row, pl.BlockSpec((t, D_IN), lambda i: (i, 0)),
                  pl.BlockSpec((N_BLK, None, D_MODEL, W_IN_BLK), lambda i: (0, layer, 0, 0))],
        out_specs=[row, row, _full((1, D_MODEL))],
        out_shape=[jax.ShapeDtypeStruct((s, D_MODEL), F32), jax.ShapeDtypeStruct((s, D_MODEL), BF16),
                   jax.ShapeDtypeStruct((1, D_MODEL), F32)],
        scratch_shapes=[pltpu.VMEM((N_BLK // 2, D_MODEL, 2 * W_IN_BLK), BF16)],
        compiler_params=_params(("arbitrary",), V7X_VMEM_LIMIT),
    )(x, g, dx1, dz, w)


def _in_proj_wgrad(hb, dz):
    s = hb.shape[0]
    t = _tile(s, 1024)

    def body(h_ref, dz_ref, dw_ref, wide_ref):
        @pl.when(pl.program_id(0) == 0)
        def _():
            wide_ref[...] = jnp.zeros_like(wide_ref)

        h = h_ref[...]
        for n in range(N_BLK // 2):
            wide_ref[n] += _dot_tn(h, dz_ref[:, 2 * n * W_IN_BLK:2 * (n + 1) * W_IN_BLK])

        @pl.when(pl.program_id(0) == s // t - 1)
        def _():
            for b in range(N_BLK):
                dw_ref[b] = wide_ref[b // 2, :, (b % 2) * W_IN_BLK:(b % 2 + 1) * W_IN_BLK]

    return pl.pallas_call(
        body, name="in_proj_wgrad", grid=(s // t,),
        in_specs=[pl.BlockSpec((t, D_MODEL), lambda i: (i, 0)), pl.BlockSpec((t, D_IN), lambda i: (i, 0))],
        out_specs=_full((N_BLK, D_MODEL, W_IN_BLK)),
        out_shape=jax.ShapeDtypeStruct((N_BLK, D_MODEL, W_IN_BLK), F32),
        scratch_shapes=[pltpu.VMEM((N_BLK // 2, D_MODEL, 2 * W_IN_BLK), F32)],
        compiler_params=_params(("arbitrary",), V7X_VMEM_LIMIT),
    )(hb, dz)


def _layer_params(small, layer):
    tril = jnp.tril(jnp.ones((CHUNK, CHUNK), bool))
    ws = jnp.where(tril, small["gmlp_w_s"][layer], 0.0)
    bmat = jnp.repeat(small["gmlp_b_s"][layer].T, HEAD_DIM, axis=1)
    scw = jnp.zeros((8, D_GROUP), F32).at[:SHORT_K].set(small["short_conv_w"][layer])
    ccw = jnp.zeros((32, D_GROUP), F32).at[:CONF_K].set(small["conf_conv_w"][layer])
    return dict(vg=small["gmlp_v_g"][layer][None], wt=ws.astype(BF16), wtt=jnp.swapaxes(ws, 1, 2).astype(BF16),
                bmat=bmat, scw=scw, ccw=ccw, lg=small["conf_ln_g"][layer][None], lb=small["conf_ln_b"][layer][None])


def _local_step(x, target, big, small, gather_pending=False, core=None):
    saved = []
    for l in range(DEPTH):
        p = _layer_params(small, l)
        z, hb, q_r, q_t, k_t, v_t = _in_proj(x, small["norm_mix_g"][l][None], big["w_in"], l)
        if gather_pending and l == 0:
            late = ("w_out", "w_up", "w_down")
            y_abd, filled = _mixers_fwd(z, p, [big[k] for k in late], [(n, 0) for n in range(len(late))])
            big = {**big, **dict(zip(late, filled))}
            o_t, filled = _attn_fwd(q_r, k_t, v_t, [big[k] for k in _BIG], [(n, 1) for n in range(len(_BIG))])
            big = dict(zip(_BIG, filled))
        else:
            y_abd = _mixers_fwd(z, p)
            o_t = _attn_fwd(q_r, k_t, v_t)
        x1, yn = _out_proj(x, y_abd, o_t, small["mix_out_g"][l][None], big["w_out"], l)
        x2, pre, h2b = _ffn(x1, small["norm_ffn_g"][l][None], big["w_up"], big["w_down"], l)
        saved.append(dict(p=p, x0=x, z=z, hb=hb, q_r=q_r, q_t=q_t, k_t=k_t, v_t=v_t, o_t=o_t, x1=x1, yn=yn, pre=pre,
                          h2b=h2b, y_abd=y_abd))
        x = x2

    loss, dx, dxb, d_final = _loss_head(x, small["final_norm_g"][None], target)

    g = {k: [None] * DEPTH for k in ("w_in", "w_out", "w_up", "w_down", "norm_mix_g", "gmlp_v_g", "gmlp_w_s", "gmlp_b_s",
                                     "short_conv_w", "conf_conv_w", "conf_ln_g", "conf_ln_b", "mix_out_g", "norm_ffn_g")}
    tril = jnp.tril(jnp.ones((CHUNK, CHUNK), bool))
    early = {}
    ffn_sums = [[]] * DEPTH
    for l in reversed(range(DEPTH)):
        sv = saved[l]
        p = sv["p"]
        riding, sums = [], []
        if core is not None and l == 0:
            riding = [(k, 1) for k in _BIG]
            sums = _chip_sums(core, [g["w_in"][1], g["w_out"][1]]) + ffn_sums[1]
        dpre, dh, *got = _ffn_bwd(dxb, sv["pre"], big["w_up"], big["w_down"], l, sums)
        early.update(zip(riding, zip(sums, *got)))
        g["w_up"][l], g["w_down"][l] = _ffn_wgrad(sv["h2b"], sv["pre"], dpre, dxb)
        halves = [g["w_up"][l], g["w_down"][l]] if core is not None else []
        dx1, g["norm_ffn_g"][l], dyn, g["mix_out_g"][l], g["w_out"][l], *swapped = _out_proj_bwd(
            sv["x1"], small["norm_ffn_g"][l][None], dh, dx, sv["yn"], small["mix_out_g"][l][None], big["w_out"], l,
            halves)
        if halves:
            ffn_sums[l] = [_add_pairs(core, a, b) for a, b in zip(halves, *swapped)]
        riding, sums = [], []
        if core is not None and l == 0:
            riding, sums = [("w_up", 0), ("w_down", 0)], ffn_sums[0]
        (dza, dcb, dcd, do_r, do_t, dsum, dvg, dws, dbm, dscw, dccw, dlg, dlb, *got) = _mixers_bwd_a(
            sv["z"], dyn, sv["o_t"], sv["y_abd"], p, sums)
        early.update(zip(riding, zip(sums, *got)))
        dq_t, dk_t, dv_t = _attn_bwd(sv["q_r"], sv["q_t"], sv["k_t"], sv["v_t"], do_r, do_t, dsum)
        dz = _mixers_bwd_b(sv["z"], dza, dcb, dcd, dq_t, dk_t, dv_t, p)
        dx, dxb, g["norm_mix_g"][l] = _in_proj_bwd(sv["x0"], small["norm_mix_g"][l][None], dx1, dz, big["w_in"], l)
        g["w_in"][l] = _in_proj_wgrad(sv["hb"], dz)
        g["gmlp_v_g"][l] = dvg[0]
        g["gmlp_w_s"][l] = jnp.where(tril, dws, 0.0)
        g["gmlp_b_s"][l] = dbm.reshape(CHUNK, N_HEADS, HEAD_DIM).sum(-1).T
        g["short_conv_w"][l] = dscw[:SHORT_K]
        g["conf_conv_w"][l] = dccw[:CONF_K]
        g["conf_ln_g"][l] = dlg[0]
        g["conf_ln_b"][l] = dlb[0]
        g["norm_mix_g"][l] = g["norm_mix_g"][l][0]
        g["mix_out_g"][l] = g["mix_out_g"][l][0]
        g["norm_ffn_g"][l] = g["norm_ffn_g"][l][0]
    grads = {k: v if k in ("w_in", "w_out", "w_up", "w_down") else jnp.stack(v) for k, v in g.items()}
    grads["final_norm_g"] = d_final[0]
    return loss, dx, grads, early


_ANY = pl.BlockSpec(memory_space=pl.ANY)


def _mesh_place():
    x, y, c = lax.axis_index("x"), lax.axis_index("y"), lax.axis_index("c")
    chips = [(1 - x, y), (x, 1 - y), (1 - x, 1 - y)]
    return x, y, c, 2 * x + y, chips


def _gather_stages(bufs, parts, sems):
    ici_send, ici_recv, d2d_send, d2d_recv = sems
    x, y, c, me, chips = _mesh_place()
    blk = [2 * chip[0] + chip[1] for chip in chips]
    pairs = [(p, r) for p in range(len(parts)) for r in range(3)]

    def rows(p, block, half_of):
        k, layer = parts[p]
        half = bufs[k].shape[2] // 2
        return bufs[k].at[block, layer, pl.ds(half_of * half, half), :]

    def ici(p, r, block):
        return pltpu.make_async_remote_copy(
            src_ref=rows(p, me, c), dst_ref=rows(p, block, c), send_sem=ici_send.at[3 * p + r],
            recv_sem=ici_recv.at[3 * p + r], device_id=(chips[r][0], chips[r][1], c), device_id_type=MESH)

    def d2d(p, r, half_of):
        part = rows(p, blk[r], half_of)
        return pltpu.make_async_remote_copy(
            src_ref=part, dst_ref=part, send_sem=d2d_send.at[3 * p + r], recv_sem=d2d_recv.at[3 * p + r],
            device_id=(x, y, 1 - c), device_id_type=MESH)

    def start():
        for p, r in pairs:
            ici(p, r, me).start()

    def forward(p):
        for r in range(3):
            ici(p, r, blk[r]).wait_recv()
            d2d(p, r, c).start()

    def finish():
        for p, r in pairs:
            d2d(p, r, 1 - c).wait_recv()
        for p, r in pairs:
            ici(p, r, me).wait_send()
            d2d(p, r, c).wait_send()

    return start, forward, finish


def _gather_sems(parts):
    return [pltpu.SemaphoreType.DMA((3 * len(parts),)) for _ in range(4)]


def _gather_in_steps(bufs, parts, sems, n_steps):
    start, forward, finish = _gather_stages(bufs, parts, sems)
    i = pl.program_id(0)
    pl.when(i == 0)(start)
    for p in range(len(parts)):
        pl.when(i == n_steps * (2 * p + 3) // (2 * len(parts) + 2))(lambda p=p: forward(p))
    pl.when(i == n_steps - 1)(finish)


def _gather_first(bufs, parts, whole):
    n, m = len(bufs), len(whole)

    def body(*refs):
        whole_in, buf_out, whole_out = refs[n:n + m], refs[n + m:2 * n + m], refs[2 * n + m:2 * (n + m)]
        sems = refs[2 * (n + m):]
        send_sems, recv_sems, local_sems = sems[4:]
        x, y, c, me, chips = _mesh_place()
        start, forward, finish = _gather_stages(buf_out, parts, sems[:4])

        def push(k, r, block):
            return pltpu.make_async_remote_copy(
                src_ref=whole_in[k], dst_ref=whole_out[k].at[block], send_sem=send_sems.at[3 * k + r],
                recv_sem=recv_sems.at[3 * k + r], device_id=(chips[r][0], chips[r][1], c), device_id_type=MESH)

        local = [pltpu.make_async_copy(whole_in[k], whole_out[k].at[me], local_sems.at[k]) for k in range(m)]
        for cp in local:
            cp.start()
        start()
        for k in range(m):
            for r in range(3):
                push(k, r, me).start()
        for p in range(len(parts)):
            forward(p)
        for k in range(m):
            for r, chip in enumerate(chips):
                push(k, r, 2 * chip[0] + chip[1]).wait_recv()
        for k in range(m):
            for r in range(3):
                push(k, r, me).wait_send()
        finish()
        for cp in local:
            cp.wait()

    return pl.pallas_call(
        body, name="gather_first",
        in_specs=[_ANY] * (n + m), out_specs=[_ANY] * (n + m),
        out_shape=[jax.ShapeDtypeStruct(b.shape, b.dtype) for b in bufs]
                  + [jax.ShapeDtypeStruct((N_BLK,) + b.shape, b.dtype) for b in whole],
        input_output_aliases={k: k for k in range(n)},
        scratch_shapes=_gather_sems(parts) + [pltpu.SemaphoreType.DMA((3 * m,)), pltpu.SemaphoreType.DMA((3 * m,)),
                                              pltpu.SemaphoreType.DMA((m,))],
    )(*bufs, *whole)


def _swap_halves(gs):
    n = len(gs)

    def body(*refs):
        start, finish = _halves_stages(refs[:n], refs[n:2 * n], refs[2 * n:])
        start()
        finish()

    return pl.pallas_call(
        body, name="swap_halves", in_specs=[_ANY] * n, out_specs=[_ANY] * n,
        out_shape=_halves_shapes(gs), scratch_shapes=_halves_sems(n),
    )(*gs)


def _halves_stages(ins, outs, sems):
    send_sems, recv_sems = sems
    x, y, c, _, _ = _mesh_place()

    def copy(k):
        half = ins[k].shape[1] // 2
        return pltpu.make_async_remote_copy(
            src_ref=ins[k].at[:, pl.ds((1 - c) * half, half), :], dst_ref=outs[k],
            send_sem=send_sems.at[k], recv_sem=recv_sems.at[k], device_id=(x, y, 1 - c), device_id_type=MESH)

    def start():
        for k in range(len(ins)):
            copy(k).start()

    def finish():
        for k in range(len(ins)):
            copy(k).wait()

    return start, finish


def _halves_shapes(gs):
    return [jax.ShapeDtypeStruct((g.shape[0], g.shape[1] // 2, g.shape[2]), F32) for g in gs]


def _halves_sems(n):
    return [pltpu.SemaphoreType.DMA((n,)), pltpu.SemaphoreType.DMA((n,))]


def _owner_stages(ins, outs, sems):
    send_sems, recv_sems = sems
    x, y, c, me, chips = _mesh_place()
    pairs = [(k, r) for k in range(len(ins)) for r in range(3)]

    def remote(k, r, src_block, dst_block):
        return pltpu.make_async_remote_copy(
            src_ref=ins[k].at[src_block], dst_ref=outs[k].at[dst_block], send_sem=send_sems.at[3 * k + r],
            recv_sem=recv_sems.at[3 * k + r], device_id=(chips[r][0], chips[r][1], c), device_id_type=MESH)

    def start():
        for k, r in pairs:
            remote(k, r, 2 * chips[r][0] + chips[r][1], me).start()

    def finish():
        for k, r in pairs:
            remote(k, r, me, 2 * chips[r][0] + chips[r][1]).wait_recv()
        for k, r in pairs:
            remote(k, r, 2 * chips[r][0] + chips[r][1], me).wait_send()

    return start, finish


def _owner_sems(n):
    return [pltpu.SemaphoreType.DMA((3 * n,)), pltpu.SemaphoreType.DMA((3 * n,))]


def _owners_in_steps(ins, outs, sems, first, last):
    start, finish = _owner_stages(ins, outs, sems)
    pl.when(first)(start)
    pl.when(last)(finish)


def _send_to_owners(sums):
    n = len(sums)

    def body(*refs):
        start, finish = _owner_stages(refs[:n], refs[n:2 * n], refs[2 * n:])
        start()
        finish()

    return pl.pallas_call(
        body, name="send_to_owners", in_specs=[_ANY] * n, out_specs=[_ANY] * n,
        out_shape=[jax.ShapeDtypeStruct(s.shape, s.dtype) for s in sums],
        scratch_shapes=_owner_sems(n),
    )(*sums)


def _swap_reduced(fs):
    n = len(fs)

    def body(*refs):
        ins, outs, (send_sems, recv_sems) = refs[:n], refs[n:2 * n], refs[2 * n:]
        x, y, c, _, _ = _mesh_place()
        cps = [pltpu.make_async_remote_copy(src_ref=ins[k], dst_ref=outs[k], send_sem=send_sems.at[k],
                                            recv_sem=recv_sems.at[k], device_id=(x, y, 1 - c), device_id_type=MESH)
               for k in range(n)]
        for cp in cps:
            cp.start()
        for cp in cps:
            cp.wait()

    return pl.pallas_call(
        body, name="swap_reduced", in_specs=[_ANY] * n, out_specs=[_ANY] * n,
        out_shape=[jax.ShapeDtypeStruct(f.shape, F32) for f in fs],
        scratch_shapes=[pltpu.SemaphoreType.DMA((n,)), pltpu.SemaphoreType.DMA((n,))],
    )(*fs)


def _row_tile(rows):
    return min(rows, 256)


def _chip_sums(core, grads):
    if not grads:
        return []
    return [_add_pairs(core, a, b) for a, b in zip(grads, _swap_halves(grads))]


def _add_pairs(core, g, other):
    n, half, cols = other.shape
    t = _row_tile(half)
    per_half = half // t

    def body(c_ref, a_ref, b_ref, o_ref):
        o_ref[...] = (a_ref[...] + b_ref[...]).astype(BF16)

    spec = pl.BlockSpec((None, t, cols), lambda i, j, c_ref: (i, j, 0))
    return pl.pallas_call(
        body, name="add_pairs",
        grid_spec=pltpu.PrefetchScalarGridSpec(
            num_scalar_prefetch=1, grid=(n, per_half),
            in_specs=[pl.BlockSpec((None, t, cols), lambda i, j, c_ref: (i, c_ref[0] * per_half + j, 0)), spec],
            out_specs=spec),
        out_shape=jax.ShapeDtypeStruct(other.shape, BF16), compiler_params=_params(("parallel", "parallel")),
    )(core, g, other)


def _add_chips(me, s1, r2):
    _, r, cols = r2.shape
    t = _row_tile(r)

    def body(me_ref, s_ref, r_ref, o_ref):
        own = s_ref[...].astype(F32)
        parts = [jnp.where(me_ref[0] == k, own, r_ref[k].astype(F32)) for k in range(N_BLK)]
        o_ref[...] = ((parts[0] + parts[1]) + parts[2]) + parts[3]

    return pl.pallas_call(
        body, name="add_chips",
        grid_spec=pltpu.PrefetchScalarGridSpec(
            num_scalar_prefetch=1, grid=(r // t,),
            in_specs=[pl.BlockSpec((None, t, cols), lambda i, me_ref: (me_ref[0], i, 0)),
                      pl.BlockSpec((N_BLK, t, cols), lambda i, me_ref: (0, i, 0))],
            out_specs=pl.BlockSpec((t, cols), lambda i, me_ref: (i, 0))),
        out_shape=jax.ShapeDtypeStruct((r, cols), F32), compiler_params=_params(("parallel",)),
    )(me, s1, r2)


def _adamw(core, mine, other, w, m, v, layer, earlier=None):
    half, cols = mine.shape
    t = _row_tile(half)
    per_half = half // t
    c1 = 1.0 - ADAM_B1 ** ADAM_STEP
    c2 = 1.0 - ADAM_B2 ** ADAM_STEP

    def body(c_ref, a_ref, b_ref, w_ref, m_ref, v_ref, *rest):
        g_ref, d_ref, mo_ref, vo_ref = rest[-4:]
        gv = jnp.where(pl.program_id(0) // per_half == c_ref[0], a_ref[...], b_ref[...])
        g_ref[...] = gv
        m_new = ADAM_B1 * m_ref[...] + (1.0 - ADAM_B1) * gv
        v_new = ADAM_B2 * v_ref[...] + (1.0 - ADAM_B2) * (gv * gv)
        mo_ref[...] = m_new
        vo_ref[...] = v_new
        d_ref[...] = -ADAM_LR * ((m_new / c1) / (jnp.sqrt(v_new / c2) + ADAM_EPS) + ADAM_WD * w_ref[...])

    part = pl.BlockSpec((t, cols), lambda i, c_ref: (i % per_half, 0))
    spec = pl.BlockSpec((None, t, cols), lambda i, c_ref: (layer, i, 0))
    kept = [] if earlier is None else list(earlier)
    return pl.pallas_call(
        body, name="adamw",
        grid_spec=pltpu.PrefetchScalarGridSpec(
            num_scalar_prefetch=1, grid=(2 * per_half,),
            in_specs=[part, part, spec, spec, spec] + [_ANY] * len(kept), out_specs=[spec] * 4),
        out_shape=[jax.ShapeDtypeStruct(w.shape, F32)] * 4,
        input_output_aliases={6 + k: k for k in range(len(kept))},
        compiler_params=_params(("parallel",)),
    )(core, mine, other, w, m, v, *kept)


_REPLICATED = ("norm_mix_g", "gmlp_v_g", "gmlp_w_s", "gmlp_b_s", "conf_ln_g", "conf_ln_b", "mix_out_g", "norm_ffn_g",
               "final_norm_g")
_REP_SHAPES = {"norm_mix_g": (DEPTH, D_MODEL), "gmlp_v_g": (DEPTH, D_GROUP), "gmlp_w_s": (DEPTH, N_HEADS, CHUNK, CHUNK),
               "gmlp_b_s": (DEPTH, N_HEADS, CHUNK), "conf_ln_g": (DEPTH, D_GROUP), "conf_ln_b": (DEPTH, D_GROUP),
               "mix_out_g": (DEPTH, D_MODEL), "norm_ffn_g": (DEPTH, D_MODEL), "final_norm_g": (D_MODEL,)}
_BIG = ("w_in", "w_out", "w_up", "w_down")
_CONV_ROWS = 8
_REP_ROWS = 144
_SMALL_ROWS = 160
_CH_BLK = D_GROUP // N_BLK


def _pad_rows(flat, rows):
    pad = rows * D_MODEL - flat.shape[-1]
    flat = jnp.pad(flat, [(0, 0)] * (flat.ndim - 1) + [(0, pad)])
    return flat.reshape(flat.shape[:-1] + (rows, D_MODEL))


def _pack_small(scw, ccw, rep):
    lead = scw.shape[:-3]
    conv = jnp.concatenate([scw.reshape(lead + (-1,)), ccw.reshape(lead + (-1,))], axis=-1)
    flat = jnp.concatenate([rep[k].reshape(-1) for k in _REPLICATED])
    flat = jnp.broadcast_to(flat, lead + flat.shape)
    parts = [_pad_rows(conv, _CONV_ROWS), _pad_rows(flat, _REP_ROWS),
             jnp.zeros(lead + (_SMALL_ROWS - _CONV_ROWS - _REP_ROWS, D_MODEL), F32)]
    return jnp.concatenate(parts, axis=-2)


def _unpack_small(pk):
    out = {}
    conv = pk[:_CONV_ROWS].reshape(-1)
    n_s = DEPTH * SHORT_K * _CH_BLK
    out["short_conv_w"] = conv[:n_s].reshape(DEPTH, SHORT_K, _CH_BLK)
    out["conf_conv_w"] = conv[n_s:n_s + DEPTH * CONF_K * _CH_BLK].reshape(DEPTH, CONF_K, _CH_BLK)
    row = _CONV_ROWS
    flat = pk[row:row + _REP_ROWS].reshape(-1)
    at = 0
    for k in _REPLICATED:
        n = math.prod(_REP_SHAPES[k])
        out[k] = flat[at:at + n].reshape(_REP_SHAPES[k])
        at += n
    return out


def _conv_blocks(w):
    d, k, _ = w.shape
    return w.reshape(d, k, N_BLK, _CH_BLK).transpose(2, 0, 1, 3)


_WEIGHTS = ("norm_mix_g", "w_in", "gmlp_v_g", "gmlp_w_s", "gmlp_b_s", "short_conv_w", "conf_conv_w", "conf_ln_g",
            "conf_ln_b", "mix_out_g", "w_out", "norm_ffn_g", "w_up", "w_down", "final_norm_g")


def kernel(x, norm_mix_g, w_in, gmlp_v_g, gmlp_w_s, gmlp_b_s, short_conv_w, conf_conv_w, conf_ln_g, conf_ln_b, mix_out_g, w_out, norm_ffn_g, w_up, w_down, final_norm_g, loss_target, m_norm_mix_g, m_w_in, m_gmlp_v_g, m_gmlp_w_s, m_gmlp_b_s, m_short_conv_w, m_conf_conv_w, m_conf_ln_g, m_conf_ln_b, m_mix_out_g, m_w_out, m_norm_ffn_g, m_w_up, m_w_down, m_final_norm_g, v_norm_mix_g, v_w_in, v_gmlp_v_g, v_gmlp_w_s, v_gmlp_b_s, v_short_conv_w, v_conf_conv_w, v_conf_ln_g, v_conf_ln_b, v_mix_out_g, v_w_out, v_norm_ffn_g, v_w_up, v_w_down, v_final_norm_g):
    w = dict(norm_mix_g=norm_mix_g, w_in=w_in, gmlp_v_g=gmlp_v_g, gmlp_w_s=gmlp_w_s, gmlp_b_s=gmlp_b_s,
             short_conv_w=short_conv_w, conf_conv_w=conf_conv_w, conf_ln_g=conf_ln_g, conf_ln_b=conf_ln_b,
             mix_out_g=mix_out_g, w_out=w_out, norm_ffn_g=norm_ffn_g, w_up=w_up, w_down=w_down, final_norm_g=final_norm_g)
    m = dict(norm_mix_g=m_norm_mix_g, w_in=m_w_in, gmlp_v_g=m_gmlp_v_g, gmlp_w_s=m_gmlp_w_s, gmlp_b_s=m_gmlp_b_s,
             short_conv_w=m_short_conv_w, conf_conv_w=m_conf_conv_w, conf_ln_g=m_conf_ln_g, conf_ln_b=m_conf_ln_b,
             mix_out_g=m_mix_out_g, w_out=m_w_out, norm_ffn_g=m_norm_ffn_g, w_up=m_w_up, w_down=m_w_down,
             final_norm_g=m_final_norm_g)
    v = dict(norm_mix_g=v_norm_mix_g, w_in=v_w_in, gmlp_v_g=v_gmlp_v_g, gmlp_w_s=v_gmlp_w_s, gmlp_b_s=v_gmlp_b_s,
             short_conv_w=v_short_conv_w, conf_conv_w=v_conf_conv_w, conf_ln_g=v_conf_ln_g, conf_ln_b=v_conf_ln_b,
             mix_out_g=v_mix_out_g, w_out=v_w_out, norm_ffn_g=v_norm_ffn_g, w_up=v_w_up, w_down=v_w_down,
             final_norm_g=v_final_norm_g)
    core = lax.axis_index("c").astype(jnp.int32).reshape(1)
    me = (2 * lax.axis_index("x") + lax.axis_index("y")).astype(jnp.int32).reshape(1)

    conv_mine = _pad_rows(jnp.concatenate([short_conv_w.reshape(-1), conf_conv_w.reshape(-1)]), _CONV_ROWS)
    big = {k: _cast_into_slot(w[k], me, "cast_" + k) for k in _BIG}
    big["w_in"], conv_all = _gather_first([big["w_in"]], [(0, 0)], [conv_mine])
    conv_all = conv_all.reshape(N_BLK, -1)
    n_s = DEPTH * SHORT_K * _CH_BLK
    scw_all = conv_all[:, :n_s].reshape(N_BLK, DEPTH, SHORT_K, _CH_BLK)
    ccw_all = conv_all[:, n_s:n_s + DEPTH * CONF_K * _CH_BLK].reshape(N_BLK, DEPTH, CONF_K, _CH_BLK)
    small = {k: w[k] for k in _REPLICATED}
    small["short_conv_w"] = scw_all.transpose(1, 2, 0, 3).reshape(DEPTH, SHORT_K, D_GROUP)
    small["conf_conv_w"] = ccw_all.transpose(1, 2, 0, 3).reshape(DEPTH, CONF_K, D_GROUP)

    loss, dx, g, early = _local_step(x[0], loss_target[0], big, small, gather_pending=True, core=core)

    where = [(k, l) for k in _BIG for l in range(DEPTH)]
    late = [kl for kl in where if kl not in early]
    sums = _chip_sums(core, [g[k][l] for k, l in late]
                      + [_pack_small(_conv_blocks(g["short_conv_w"]), _conv_blocks(g["conf_conv_w"]), g)])
    sent = {**early, **dict(zip(late + ["small"], zip(sums, _send_to_owners(sums))))}
    mine = [_add_chips(me, *sent[kl]) for kl in where + ["small"]]
    other = _swap_reduced(mine)

    done = {}
    for n, (k, l) in enumerate(where):
        done[k] = _adamw(core, mine[n], other[n], w[k], m[k], v[k], l, done.get(k))
    small_own = [_pack_small(t["short_conv_w"], t["conf_conv_w"], t)[None] for t in (w, m, v)]
    small_done = [_unpack_small(a[0]) for a in _adamw(core, mine[-1], other[-1], *small_own, 0)]

    outs = [lax.psum(loss[0, 0], ("x", "y", "c")), dx[None]]
    for kind in range(4):
        outs += [done[k][kind] if k in _BIG else small_done[kind][k] for k in _WEIGHTS]
    return tuple(outs)
```
